```python
import math
import jax, jax.numpy as jnp
from jax import lax
import numpy as np

D_MODEL = 1024
BATCH = 16
SEQ = 2048
DEPTH = 2

D_FF = 2816
HEAD_DIM_A = 64
N_HEADS_A = 8
N_KV_HEADS_A = 2
GROUP_A = N_HEADS_A // N_KV_HEADS_A
WINDOW = 128
BLOCK = 128
N_HEADS_B = 8
Q_LORA_RANK = 256
KV_LORA_RANK = 128
QK_NOPE_DIM = 64
QK_ROPE_DIM = 32
QK_DIM_B = QK_NOPE_DIM + QK_ROPE_DIM
V_DIM_B = 64
ROPE_BASE = 10000.0
WIDTH_A = N_HEADS_A * HEAD_DIM_A
WIDTH_B = N_HEADS_B * V_DIM_B
IN_SPLITS = (WIDTH_A, N_KV_HEADS_A * HEAD_DIM_A, N_KV_HEADS_A * HEAD_DIM_A,
             Q_LORA_RANK, KV_LORA_RANK, QK_ROPE_DIM, D_MODEL, D_MODEL)
IN_WIDTH = sum(IN_SPLITS)
IN_OFFSETS = tuple(int(o) for o in np.cumsum(IN_SPLITS)[:-1])
EPS = 1e-6
NEG = -1e30

kernel_name = "hybrid_swa_sink_mla_gated_macaron"


def rms_norm(x, g):
    xf = x.astype(jnp.float32)
    y = xf * lax.rsqrt(jnp.mean(xf * xf, axis=-1, keepdims=True) + EPS)
    return (y * g.astype(jnp.float32)).astype(x.dtype)


def swiglu(x, w_gate, w_up, w_down):
    return (jax.nn.silu(x @ w_gate) * (x @ w_up)) @ w_down


def apply_rope(x, cos, sin):
    half = x.shape[-1] // 2
    x1, x2 = x[..., :half], x[..., half:]
    return jnp.concatenate([x1 * cos - x2 * sin, x2 * cos + x1 * sin], axis=-1)


def alibi_slopes(n_heads):
    return jnp.exp2(-8.0 * (jnp.arange(n_heads, dtype=jnp.float32) + 1.0) / n_heads)


def swa_attention(q, k, v, positions, q_gain, k_gain, sinks):
    B, S = q.shape[0], q.shape[1]
    nb = S // BLOCK
    q = rms_norm(q, q_gain)
    k = rms_norm(k, k_gain)
    qb = q.reshape(B, nb, BLOCK, N_KV_HEADS_A, GROUP_A, HEAD_DIM_A)
    kb = k.reshape(B, nb, BLOCK, N_KV_HEADS_A, HEAD_DIM_A)
    vb = v.reshape(B, nb, BLOCK, N_KV_HEADS_A, HEAD_DIM_A)
    pad5 = ((0, 0), (1, 0), (0, 0), (0, 0), (0, 0))
    kk = jnp.concatenate([jnp.pad(kb, pad5)[:, :-1], kb], axis=2)
    vv = jnp.concatenate([jnp.pad(vb, pad5)[:, :-1], vb], axis=2)
    pb = positions.reshape(B, nb, BLOCK)
    pk = jnp.concatenate([jnp.pad(pb, ((0, 0), (1, 0), (0, 0)))[:, :-1], pb], axis=2)

    scale = HEAD_DIM_A ** -0.5
    s = jnp.einsum('bnqkgd,bnskd->bnkgqs', qb, kk).astype(jnp.float32) * scale
    dist = (pb[:, :, :, None] - pk[:, :, None, :]).astype(jnp.float32)
    slopes = alibi_slopes(N_HEADS_A).reshape(N_KV_HEADS_A, GROUP_A)
    s = s - slopes[None, None, :, :, None, None] * dist[:, :, None, None]

    qi = jnp.arange(BLOCK)[:, None] + BLOCK
    ki = jnp.arange(2 * BLOCK)[None, :]
    diff = qi - ki
    band = (diff >= 0) & (diff < WINDOW)
    has_prev = (jnp.arange(nb)[:, None, None] > 0) | (ki[None] >= BLOCK)
    valid = band[None] & has_prev
    s = jnp.where(valid[None, :, None, None], s, NEG)

    sink = sinks.astype(jnp.float32).reshape(N_KV_HEADS_A, GROUP_A)
    sink = jnp.broadcast_to(sink[None, None, :, :, None, None], s.shape[:-1] + (1,))
    p = jax.nn.softmax(jnp.concatenate([s, sink], axis=-1), axis=-1)[..., :-1]
    o = jnp.einsum('bnkgqs,bnskd->bnqkgd', p.astype(vv.dtype), vv)
    return o.reshape(B, S, WIDTH_A)


def mla_attention(c_q, c_kv, k_rope, cos, sin, q_lora_norm, w_uq, kv_lora_norm, w_ukv, q_gain, k_gain):
    B, S = c_q.shape[0], c_q.shape[1]
    nb = S // BLOCK
    q = (rms_norm(c_q, q_lora_norm) @ w_uq).reshape(B, S, N_HEADS_B, QK_DIM_B)
    kv = (rms_norm(c_kv, kv_lora_norm) @ w_ukv).reshape(B, S, N_HEADS_B, QK_NOPE_DIM + V_DIM_B)
    k_nope, v = kv[..., :QK_NOPE_DIM], kv[..., QK_NOPE_DIM:]
    k_r = jnp.broadcast_to(k_rope[:, :, None, :], (B, S, N_HEADS_B, QK_ROPE_DIM))
    k = jnp.concatenate([k_nope, k_r], axis=-1)
    q = rms_norm(q, q_gain)
    k = rms_norm(k, k_gain)
    q = jnp.concatenate([q[..., :QK_NOPE_DIM], apply_rope(q[..., QK_NOPE_DIM:], cos, sin)], axis=-1)
    k = jnp.concatenate([k[..., :QK_NOPE_DIM], apply_rope(k[..., QK_NOPE_DIM:], cos, sin)], axis=-1)

    scale = QK_DIM_B ** -0.5
    key_idx = jnp.arange(S)
    qb = jnp.moveaxis(q.reshape(B, nb, BLOCK, N_HEADS_B, QK_DIM_B), 1, 0)

    def one_block(args):
        q_blk, i = args
        s = jnp.einsum('bqhd,bshd->bhqs', q_blk, k).astype(jnp.float32) * scale
        t = i * BLOCK + jnp.arange(BLOCK)
        s = jnp.where(t[:, None] >= key_idx[None, :], s, NEG)
        p = jax.nn.softmax(s, axis=-1)
        return jnp.einsum('bhqs,bshd->bqhd', p.astype(v.dtype), v)

    o = lax.map(one_block, (qb, jnp.arange(nb)))
    return jnp.moveaxis(o, 0, 1).reshape(B, S, WIDTH_B)


def _fwd_setup_inputs(seed: int = 0) -> dict:
    key = jax.random.key(seed)
    ks = iter(jax.random.split(key, 32))
    L = DEPTH

    def w(shape, fan_in):
        return jax.random.normal(next(ks), shape, jnp.float32) * fan_in ** -0.5

    def gain(shape):
        return 1.0 + 0.02 * jax.random.normal(next(ks), shape, jnp.float32)

    x = jax.random.normal(next(ks), (BATCH, SEQ, D_MODEL), jnp.float32)
    start = jax.random.randint(next(ks), (BATCH, 1), 0, 4096, dtype=jnp.int32)
    positions = start + jnp.arange(SEQ, dtype=jnp.int32)[None, :]
    return {
        "x": x,
        "positions": positions,
        "ffn1_norm": gain((L, D_MODEL)),
        "ffn1_w_gate": w((L, D_MODEL, D_FF), D_MODEL),
        "ffn1_w_up": w((L, D_MODEL, D_FF), D_MODEL),
        "ffn1_w_down": w((L, D_FF, D_MODEL), D_FF),
        "mix_norm": gain((L, D_MODEL)),
        "w_in": w((L, D_MODEL, IN_WIDTH), D_MODEL),
        "swa_q_norm": gain((L, HEAD_DIM_A)),
        "swa_k_norm": gain((L, HEAD_DIM_A)),
        "swa_sinks": 0.5 * jax.random.normal(next(ks), (L, N_HEADS_A), jnp.float32),
        "mla_q_lora_norm": gain((L, Q_LORA_RANK)),
        "mla_w_uq": w((L, Q_LORA_RANK, N_HEADS_B * QK_DIM_B), Q_LORA_RANK),
        "mla_kv_lora_norm": gain((L, KV_LORA_RANK)),
        "mla_w_ukv": w((L, KV_LORA_RANK, N_HEADS_B * (QK_NOPE_DIM + V_DIM_B)), KV_LORA_RANK),
        "mla_q_norm": gain((L, QK_DIM_B)),
        "mla_k_norm": gain((L, QK_DIM_B)),
        "w_branch_a": w((L, WIDTH_A, D_MODEL), WIDTH_A),
        "w_branch_b": w((L, WIDTH_B, D_MODEL), WIDTH_B),
        "w_out": w((L, D_MODEL, D_MODEL), D_MODEL),
        "ffn2_norm": gain((L, D_MODEL)),
        "ffn2_w_gate": w((L, D_MODEL, D_FF), D_MODEL),
        "ffn2_w_up": w((L, D_MODEL, D_FF), D_MODEL),
        "ffn2_w_down": w((L, D_FF, D_MODEL), D_FF),
    }


def _fwd_reference(x, positions, ffn1_norm, ffn1_w_gate, ffn1_w_up, ffn1_w_down, mix_norm, w_in,
              swa_q_norm, swa_k_norm, swa_sinks, mla_q_lora_norm, mla_w_uq, mla_kv_lora_norm,
              mla_w_ukv, mla_q_norm, mla_k_norm, w_branch_a, w_branch_b, w_out,
              ffn2_norm, ffn2_w_gate, ffn2_w_up, ffn2_w_down):
    B, S, _ = x.shape
    half = QK_ROPE_DIM // 2
    inv_freq = ROPE_BASE ** (-jnp.arange(half, dtype=jnp.float32) / half)
    ang = positions.astype(jnp.float32)[..., None] * inv_freq
    cos = jnp.cos(ang)[:, :, None, :].astype(x.dtype)
    sin = jnp.sin(ang)[:, :, None, :].astype(x.dtype)
    k_rope_cos, k_rope_sin = cos[:, :, 0], sin[:, :, 0]

    for l in range(DEPTH):
        x = x + 0.5 * swiglu(rms_norm(x, ffn1_norm[l]), ffn1_w_gate[l], ffn1_w_up[l], ffn1_w_down[l])

        h = rms_norm(x, mix_norm[l])
        proj = h @ w_in[l]
        qa, ka, va, c_q, c_kv, k_r, g_a, g_b = jnp.split(proj, IN_OFFSETS, axis=-1)
        qa = qa.reshape(B, S, N_HEADS_A, HEAD_DIM_A)
        ka = ka.reshape(B, S, N_KV_HEADS_A, HEAD_DIM_A)
        va = va.reshape(B, S, N_KV_HEADS_A, HEAD_DIM_A)
        o_a = swa_attention(qa, ka, va, positions, swa_q_norm[l], swa_k_norm[l], swa_sinks[l])
        o_b = mla_attention(c_q, c_kv, k_r, cos, sin, mla_q_lora_norm[l], mla_w_uq[l],
                            mla_kv_lora_norm[l], mla_w_ukv[l], mla_q_norm[l], mla_k_norm[l])
        merged = jax.nn.sigmoid(g_a) * (o_a @ w_branch_a[l]) + jax.nn.sigmoid(g_b) * (o_b @ w_branch_b[l])
        x = x + merged @ w_out[l]

        x = x + 0.5 * swiglu(rms_norm(x, ffn2_norm[l]), ffn2_w_gate[l], ffn2_w_up[l], ffn2_w_down[l])
    return x


import jax as _jax
import jax.numpy as _jnp

TWIN_FORMAT = 'train_step'
FWD_PARAMS = ['x', 'positions', 'ffn1_norm', 'ffn1_w_gate', 'ffn1_w_up', 'ffn1_w_down', 'mix_norm', 'w_in', 'swa_q_norm', 'swa_k_norm', 'swa_sinks', 'mla_q_lora_norm', 'mla_w_uq', 'mla_kv_lora_norm', 'mla_w_ukv', 'mla_q_norm', 'mla_k_norm', 'w_branch_a', 'w_branch_b', 'w_out', 'ffn2_norm', 'ffn2_w_gate', 'ffn2_w_up', 'ffn2_w_down']
TWIN_WEIGHTS = ['ffn1_norm', 'ffn1_w_gate', 'ffn1_w_up', 'ffn1_w_down', 'mix_norm', 'w_in', 'swa_q_norm', 'swa_k_norm', 'swa_sinks', 'mla_q_lora_norm', 'mla_w_uq', 'mla_kv_lora_norm', 'mla_w_ukv', 'mla_q_norm', 'mla_k_norm', 'w_branch_a', 'w_branch_b', 'w_out', 'ffn2_norm', 'ffn2_w_gate', 'ffn2_w_up', 'ffn2_w_down']
TWIN_DIFF_INPUT = 'x'
TWIN_INPUTS = ['x', 'positions', 'ffn1_norm', 'ffn1_w_gate', 'ffn1_w_up', 'ffn1_w_down', 'mix_norm', 'w_in', 'swa_q_norm', 'swa_k_norm', 'swa_sinks', 'mla_q_lora_norm', 'mla_w_uq', 'mla_kv_lora_norm', 'mla_w_ukv', 'mla_q_norm', 'mla_k_norm', 'w_branch_a', 'w_branch_b', 'w_out', 'ffn2_norm', 'ffn2_w_gate', 'ffn2_w_up', 'ffn2_w_down', 'loss_target', 'm_ffn1_norm', 'm_ffn1_w_gate', 'm_ffn1_w_up', 'm_ffn1_w_down', 'm_mix_norm', 'm_w_in', 'm_swa_q_norm', 'm_swa_k_norm', 'm_swa_sinks', 'm_mla_q_lora_norm', 'm_mla_w_uq', 'm_mla_kv_lora_norm', 'm_mla_w_ukv', 'm_mla_q_norm', 'm_mla_k_norm', 'm_w_branch_a', 'm_w_branch_b', 'm_w_out', 'm_ffn2_norm', 'm_ffn2_w_gate', 'm_ffn2_w_up', 'm_ffn2_w_down', 'v_ffn1_norm', 'v_ffn1_w_gate', 'v_ffn1_w_up', 'v_ffn1_w_down', 'v_mix_norm', 'v_w_in', 'v_swa_q_norm', 'v_swa_k_norm', 'v_swa_sinks', 'v_mla_q_lora_norm', 'v_mla_w_uq', 'v_mla_kv_lora_norm', 'v_mla_w_ukv', 'v_mla_q_norm', 'v_mla_k_norm', 'v_w_branch_a', 'v_w_branch_b', 'v_w_out', 'v_ffn2_norm', 'v_ffn2_w_gate', 'v_ffn2_w_up', 'v_ffn2_w_down']
TWIN_OUTPUTS = ['loss', 'grad_x', 'grad_ffn1_norm', 'grad_ffn1_w_gate', 'grad_ffn1_w_up', 'grad_ffn1_w_down', 'grad_mix_norm', 'grad_w_in', 'grad_swa_q_norm', 'grad_swa_k_norm', 'grad_swa_sinks', 'grad_mla_q_lora_norm', 'grad_mla_w_uq', 'grad_mla_kv_lora_norm', 'grad_mla_w_ukv', 'grad_mla_q_norm', 'grad_mla_k_norm', 'grad_w_branch_a', 'grad_w_branch_b', 'grad_w_out', 'grad_ffn2_norm', 'grad_ffn2_w_gate', 'grad_ffn2_w_up', 'grad_ffn2_w_down', 'delta_ffn1_norm', 'delta_ffn1_w_gate', 'delta_ffn1_w_up', 'delta_ffn1_w_down', 'delta_mix_norm', 'delta_w_in', 'delta_swa_q_norm', 'delta_swa_k_norm', 'delta_swa_sinks', 'delta_mla_q_lora_norm', 'delta_mla_w_uq', 'delta_mla_kv_lora_norm', 'delta_mla_w_ukv', 'delta_mla_q_norm', 'delta_mla_k_norm', 'delta_w_branch_a', 'delta_w_branch_b', 'delta_w_out', 'delta_ffn2_norm', 'delta_ffn2_w_gate', 'delta_ffn2_w_up', 'delta_ffn2_w_down', 'new_m_ffn1_norm', 'new_m_ffn1_w_gate', 'new_m_ffn1_w_up', 'new_m_ffn1_w_down', 'new_m_mix_norm', 'new_m_w_in', 'new_m_swa_q_norm', 'new_m_swa_k_norm', 'new_m_swa_sinks', 'new_m_mla_q_lora_norm', 'new_m_mla_w_uq', 'new_m_mla_kv_lora_norm', 'new_m_mla_w_ukv', 'new_m_mla_q_norm', 'new_m_mla_k_norm', 'new_m_w_branch_a', 'new_m_w_branch_b', 'new_m_w_out', 'new_m_ffn2_norm', 'new_m_ffn2_w_gate', 'new_m_ffn2_w_up', 'new_m_ffn2_w_down', 'new_v_ffn1_norm', 'new_v_ffn1_w_gate', 'new_v_ffn1_w_up', 'new_v_ffn1_w_down', 'new_v_mix_norm', 'new_v_w_in', 'new_v_swa_q_norm', 'new_v_swa_k_norm', 'new_v_swa_sinks', 'new_v_mla_q_lora_norm', 'new_v_mla_w_uq', 'new_v_mla_kv_lora_norm', 'new_v_mla_w_ukv', 'new_v_mla_q_norm', 'new_v_mla_k_norm', 'new_v_w_branch_a', 'new_v_w_branch_b', 'new_v_w_out', 'new_v_ffn2_norm', 'new_v_ffn2_w_gate', 'new_v_ffn2_w_up', 'new_v_ffn2_w_down']
TWIN_LEAF_KINDS = {'loss': 'loss', 'grad_x': 'grad_x', 'grad_ffn1_norm': 'grad_w', 'grad_ffn1_w_gate': 'grad_w', 'grad_ffn1_w_up': 'grad_w', 'grad_ffn1_w_down': 'grad_w', 'grad_mix_norm': 'grad_w', 'grad_w_in': 'grad_w', 'grad_swa_q_norm': 'grad_w', 'grad_swa_k_norm': 'grad_w', 'grad_swa_sinks': 'grad_w', 'grad_mla_q_lora_norm': 'grad_w', 'grad_mla_w_uq': 'grad_w', 'grad_mla_kv_lora_norm': 'grad_w', 'grad_mla_w_ukv': 'grad_w', 'grad_mla_q_norm': 'grad_w', 'grad_mla_k_norm': 'grad_w', 'grad_w_branch_a': 'grad_w', 'grad_w_branch_b': 'grad_w', 'grad_w_out': 'grad_w', 'grad_ffn2_norm': 'grad_w', 'grad_ffn2_w_gate': 'grad_w', 'grad_ffn2_w_up': 'grad_w', 'grad_ffn2_w_down': 'grad_w', 'delta_ffn1_norm': 'delta_w', 'delta_ffn1_w_gate': 'delta_w', 'delta_ffn1_w_up': 'delta_w', 'delta_ffn1_w_down': 'delta_w', 'delta_mix_norm': 'delta_w', 'delta_w_in': 'delta_w', 'delta_swa_q_norm': 'delta_w', 'delta_swa_k_norm': 'delta_w', 'delta_swa_sinks': 'delta_w', 'delta_mla_q_lora_norm': 'delta_w', 'delta_mla_w_uq': 'delta_w', 'delta_mla_kv_lora_norm': 'delta_w', 'delta_mla_w_ukv': 'delta_w', 'delta_mla_q_norm': 'delta_w', 'delta_mla_k_norm': 'delta_w', 'delta_w_branch_a': 'delta_w', 'delta_w_branch_b': 'delta_w', 'delta_w_out': 'delta_w', 'delta_ffn2_norm': 'delta_w', 'delta_ffn2_w_gate': 'delta_w', 'delta_ffn2_w_up': 'delta_w', 'delta_ffn2_w_down': 'delta_w', 'new_m_ffn1_norm': 'new_m', 'new_m_ffn1_w_gate': 'new_m', 'new_m_ffn1_w_up': 'new_m', 'new_m_ffn1_w_down': 'new_m', 'new_m_mix_norm': 'new_m', 'new_m_w_in': 'new_m', 'new_m_swa_q_norm': 'new_m', 'new_m_swa_k_norm': 'new_m', 'new_m_swa_sinks': 'new_m', 'new_m_mla_q_lora_norm': 'new_m', 'new_m_mla_w_uq': 'new_m', 'new_m_mla_kv_lora_norm': 'new_m', 'new_m_mla_w_ukv': 'new_m', 'new_m_mla_q_norm': 'new_m', 'new_m_mla_k_norm': 'new_m', 'new_m_w_branch_a': 'new_m', 'new_m_w_branch_b': 'new_m', 'new_m_w_out': 'new_m', 'new_m_ffn2_norm': 'new_m', 'new_m_ffn2_w_gate': 'new_m', 'new_m_ffn2_w_up': 'new_m', 'new_m_ffn2_w_down': 'new_m', 'new_v_ffn1_norm': 'new_v', 'new_v_ffn1_w_gate': 'new_v', 'new_v_ffn1_w_up': 'new_v', 'new_v_ffn1_w_down': 'new_v', 'new_v_mix_norm': 'new_v', 'new_v_w_in': 'new_v', 'new_v_swa_q_norm': 'new_v', 'new_v_swa_k_norm': 'new_v', 'new_v_swa_sinks': 'new_v', 'new_v_mla_q_lora_norm': 'new_v', 'new_v_mla_w_uq': 'new_v', 'new_v_mla_kv_lora_norm': 'new_v', 'new_v_mla_w_ukv': 'new_v', 'new_v_mla_q_norm': 'new_v', 'new_v_mla_k_norm': 'new_v', 'new_v_w_branch_a': 'new_v', 'new_v_w_branch_b': 'new_v', 'new_v_w_out': 'new_v', 'new_v_ffn2_norm': 'new_v', 'new_v_ffn2_w_gate': 'new_v', 'new_v_ffn2_w_up': 'new_v', 'new_v_ffn2_w_down': 'new_v'}


def _forward(args):
    return _fwd_reference(*[args[k] for k in FWD_PARAMS])


def _output_shape():
    out = _jax.eval_shape(lambda: _forward(_fwd_setup_inputs(0)))
    return out.shape, out.dtype

N_MICROBATCH = 1
ADAM_LR = 0.001
ADAM_B1 = 0.9
ADAM_B2 = 0.999
ADAM_EPS = 1e-08
ADAM_WD = 0.01
ADAM_STEP = 10
PER_EXAMPLE_BATCH_AXIS = {'x': 0, 'positions': 0, 'loss_target': 0}
SHARED_INPUTS = []
_WEIGHT_DTYPES = {'ffn1_norm': _jnp.float32, 'ffn1_w_gate': _jnp.float32, 'ffn1_w_up': _jnp.float32, 'ffn1_w_down': _jnp.float32, 'mix_norm': _jnp.float32, 'w_in': _jnp.float32, 'swa_q_norm': _jnp.float32, 'swa_k_norm': _jnp.float32, 'swa_sinks': _jnp.float32, 'mla_q_lora_norm': _jnp.float32, 'mla_w_uq': _jnp.float32, 'mla_kv_lora_norm': _jnp.float32, 'mla_w_ukv': _jnp.float32, 'mla_q_norm': _jnp.float32, 'mla_k_norm': _jnp.float32, 'w_branch_a': _jnp.float32, 'w_branch_b': _jnp.float32, 'w_out': _jnp.float32, 'ffn2_norm': _jnp.float32, 'ffn2_w_gate': _jnp.float32, 'ffn2_w_up': _jnp.float32, 'ffn2_w_down': _jnp.float32}
MOMENT_SCALE = {'ffn1_norm': 6.167427e+00, 'ffn1_w_gate': 6.863114e-02, 'ffn1_w_up': 7.267202e-02, 'ffn1_w_down': 1.181147e-01, 'mix_norm': 6.014631e-01, 'w_in': 7.834467e-02, 'swa_q_norm': 5.340328e+00, 'swa_k_norm': 5.344671e+00, 'swa_sinks': 1.829074e+01, 'mla_q_lora_norm': 7.885552e-02, 'mla_w_uq': 4.438107e-02, 'mla_kv_lora_norm': 8.207601e-01, 'mla_w_ukv': 7.251921e-02, 'mla_q_norm': 5.353559e-01, 'mla_k_norm': 5.352562e-01, 'w_branch_a': 6.944170e-02, 'w_branch_b': 5.955295e-02, 'w_out': 8.532331e-02, 'ffn2_norm': 6.138542e+00, 'ffn2_w_gate': 6.215022e-02, 'ffn2_w_up': 7.062457e-02, 'ffn2_w_down': 1.141967e-01}


def _to_microbatches(a, axis):
    t = _jnp.moveaxis(a, axis, 0)
    t = t.reshape((N_MICROBATCH, t.shape[0] // N_MICROBATCH) + t.shape[1:])
    return _jnp.moveaxis(t, 1, axis + 1)


def setup_inputs(seed: int = 0) -> dict:
    inp = _fwd_setup_inputs(seed)
    key = _jax.random.fold_in(_jax.random.key(seed), 7919)
    shape, _ = _output_shape()
    out = dict(inp)
    out["loss_target"] = _jax.random.normal(_jax.random.fold_in(key, 0), shape, _jnp.float32)
    for i, name in enumerate(TWIN_WEIGHTS):
        w = inp[name].astype(_jnp.float32)
        if MOMENT_SCALE is None:
            s = _jnp.sqrt(_jnp.mean(_jnp.square(w)) + 1e-30)
        else:
            s = MOMENT_SCALE[name]
        km, kv = _jax.random.split(_jax.random.fold_in(key, i + 1))
        out[name] = w
        out["m_" + name] = s * _jax.random.normal(km, w.shape, _jnp.float32)
        out["v_" + name] = (s * s) * _jax.random.uniform(kv, w.shape, _jnp.float32, 0.5, 1.5)
    if N_MICROBATCH > 1:
        for name, axis in PER_EXAMPLE_BATCH_AXIS.items():
            out[name] = _to_microbatches(out[name], axis)
    return {'x': out['x'], 'positions': out['positions'], 'ffn1_norm': out['ffn1_norm'], 'ffn1_w_gate': out['ffn1_w_gate'], 'ffn1_w_up': out['ffn1_w_up'], 'ffn1_w_down': out['ffn1_w_down'], 'mix_norm': out['mix_norm'], 'w_in': out['w_in'], 'swa_q_norm': out['swa_q_norm'], 'swa_k_norm': out['swa_k_norm'], 'swa_sinks': out['swa_sinks'], 'mla_q_lora_norm': out['mla_q_lora_norm'], 'mla_w_uq': out['mla_w_uq'], 'mla_kv_lora_norm': out['mla_kv_lora_norm'], 'mla_w_ukv': out['mla_w_ukv'], 'mla_q_norm': out['mla_q_norm'], 'mla_k_norm': out['mla_k_norm'], 'w_branch_a': out['w_branch_a'], 'w_branch_b': out['w_branch_b'], 'w_out': out['w_out'], 'ffn2_norm': out['ffn2_norm'], 'ffn2_w_gate': out['ffn2_w_gate'], 'ffn2_w_up': out['ffn2_w_up'], 'ffn2_w_down': out['ffn2_w_down'], 'loss_target': out['loss_target'], 'm_ffn1_norm': out['m_ffn1_norm'], 'm_ffn1_w_gate': out['m_ffn1_w_gate'], 'm_ffn1_w_up': out['m_ffn1_w_up'], 'm_ffn1_w_down': out['m_ffn1_w_down'], 'm_mix_norm': out['m_mix_norm'], 'm_w_in': out['m_w_in'], 'm_swa_q_norm': out['m_swa_q_norm'], 'm_swa_k_norm': out['m_swa_k_norm'], 'm_swa_sinks': out['m_swa_sinks'], 'm_mla_q_lora_norm': out['m_mla_q_lora_norm'], 'm_mla_w_uq': out['m_mla_w_uq'], 'm_mla_kv_lora_norm': out['m_mla_kv_lora_norm'], 'm_mla_w_ukv': out['m_mla_w_ukv'], 'm_mla_q_norm': out['m_mla_q_norm'], 'm_mla_k_norm': out['m_mla_k_norm'], 'm_w_branch_a': out['m_w_branch_a'], 'm_w_branch_b': out['m_w_branch_b'], 'm_w_out': out['m_w_out'], 'm_ffn2_norm': out['m_ffn2_norm'], 'm_ffn2_w_gate': out['m_ffn2_w_gate'], 'm_ffn2_w_up': out['m_ffn2_w_up'], 'm_ffn2_w_down': out['m_ffn2_w_down'], 'v_ffn1_norm': out['v_ffn1_norm'], 'v_ffn1_w_gate': out['v_ffn1_w_gate'], 'v_ffn1_w_up': out['v_ffn1_w_up'], 'v_ffn1_w_down': out['v_ffn1_w_down'], 'v_mix_norm': out['v_mix_norm'], 'v_w_in': out['v_w_in'], 'v_swa_q_norm': out['v_swa_q_norm'], 'v_swa_k_norm': out['v_swa_k_norm'], 'v_swa_sinks': out['v_swa_sinks'], 'v_mla_q_lora_norm': out['v_mla_q_lora_norm'], 'v_mla_w_uq': out['v_mla_w_uq'], 'v_mla_kv_lora_norm': out['v_mla_kv_lora_norm'], 'v_mla_w_ukv': out['v_mla_w_ukv'], 'v_mla_q_norm': out['v_mla_q_norm'], 'v_mla_k_norm': out['v_mla_k_norm'], 'v_w_branch_a': out['v_w_branch_a'], 'v_w_branch_b': out['v_w_branch_b'], 'v_w_out': out['v_w_out'], 'v_ffn2_norm': out['v_ffn2_norm'], 'v_ffn2_w_gate': out['v_ffn2_w_gate'], 'v_ffn2_w_up': out['v_ffn2_w_up'], 'v_ffn2_w_down': out['v_ffn2_w_down']}


def _loss(weights, diff, rest, loss_target):
    with _jax.named_scope("forward"):
        args = {**rest, TWIN_DIFF_INPUT: diff, **{k: w.astype(_WEIGHT_DTYPES[k]) for k, w in weights.items()}}
        y = _forward(args)
    with _jax.named_scope("loss_head"):
        err = _jnp.square(y.astype(_jnp.float32) - loss_target)
        return 0.5 * _jnp.sum(_jnp.mean(err, axis=-1)) if err.ndim else 0.5 * err


def _adamw(w, g, m, v):
    m = ADAM_B1 * m + (1.0 - ADAM_B1) * g
    v = ADAM_B2 * v + (1.0 - ADAM_B2) * _jnp.square(g)
    m_hat = m / (1.0 - ADAM_B1 ** ADAM_STEP)
    v_hat = v / (1.0 - ADAM_B2 ** ADAM_STEP)
    delta = -ADAM_LR * (m_hat / (_jnp.sqrt(v_hat) + ADAM_EPS) + ADAM_WD * w)
    return delta, m, v


def reference(x, positions, ffn1_norm, ffn1_w_gate, ffn1_w_up, ffn1_w_down, mix_norm, w_in, swa_q_norm, swa_k_norm, swa_sinks, mla_q_lora_norm, mla_w_uq, mla_kv_lora_norm, mla_w_ukv, mla_q_norm, mla_k_norm, w_branch_a, w_branch_b, w_out, ffn2_norm, ffn2_w_gate, ffn2_w_up, ffn2_w_down, loss_target, m_ffn1_norm, m_ffn1_w_gate, m_ffn1_w_up, m_ffn1_w_down, m_mix_norm, m_w_in, m_swa_q_norm, m_swa_k_norm, m_swa_sinks, m_mla_q_lora_norm, m_mla_w_uq, m_mla_kv_lora_norm, m_mla_w_ukv, m_mla_q_norm, m_mla_k_norm, m_w_branch_a, m_w_branch_b, m_w_out, m_ffn2_norm, m_ffn2_w_gate, m_ffn2_w_up, m_ffn2_w_down, v_ffn1_norm, v_ffn1_w_gate, v_ffn1_w_up, v_ffn1_w_down, v_mix_norm, v_w_in, v_swa_q_norm, v_swa_k_norm, v_swa_sinks, v_mla_q_lora_norm, v_mla_w_uq, v_mla_kv_lora_norm, v_mla_w_ukv, v_mla_q_norm, v_mla_k_norm, v_w_branch_a, v_w_branch_b, v_w_out, v_ffn2_norm, v_ffn2_w_gate, v_ffn2_w_up, v_ffn2_w_down):
    given = dict(x=x, positions=positions, ffn1_norm=ffn1_norm, ffn1_w_gate=ffn1_w_gate, ffn1_w_up=ffn1_w_up, ffn1_w_down=ffn1_w_down, mix_norm=mix_norm, w_in=w_in, swa_q_norm=swa_q_norm, swa_k_norm=swa_k_norm, swa_sinks=swa_sinks, mla_q_lora_norm=mla_q_lora_norm, mla_w_uq=mla_w_uq, mla_kv_lora_norm=mla_kv_lora_norm, mla_w_ukv=mla_w_ukv, mla_q_norm=mla_q_norm, mla_k_norm=mla_k_norm, w_branch_a=w_branch_a, w_branch_b=w_branch_b, w_out=w_out, ffn2_norm=ffn2_norm, ffn2_w_gate=ffn2_w_gate, ffn2_w_up=ffn2_w_up, ffn2_w_down=ffn2_w_down, loss_target=loss_target, m_ffn1_norm=m_ffn1_norm, m_ffn1_w_gate=m_ffn1_w_gate, m_ffn1_w_up=m_ffn1_w_up, m_ffn1_w_down=m_ffn1_w_down, m_mix_norm=m_mix_norm, m_w_in=m_w_in, m_swa_q_norm=m_swa_q_norm, m_swa_k_norm=m_swa_k_norm, m_swa_sinks=m_swa_sinks, m_mla_q_lora_norm=m_mla_q_lora_norm, m_mla_w_uq=m_mla_w_uq, m_mla_kv_lora_norm=m_mla_kv_lora_norm, m_mla_w_ukv=m_mla_w_ukv, m_mla_q_norm=m_mla_q_norm, m_mla_k_norm=m_mla_k_norm, m_w_branch_a=m_w_branch_a, m_w_branch_b=m_w_branch_b, m_w_out=m_w_out, m_ffn2_norm=m_ffn2_norm, m_ffn2_w_gate=m_ffn2_w_gate, m_ffn2_w_up=m_ffn2_w_up, m_ffn2_w_down=m_ffn2_w_down, v_ffn1_norm=v_ffn1_norm, v_ffn1_w_gate=v_ffn1_w_gate, v_ffn1_w_up=v_ffn1_w_up, v_ffn1_w_down=v_ffn1_w_down, v_mix_norm=v_mix_norm, v_w_in=v_w_in, v_swa_q_norm=v_swa_q_norm, v_swa_k_norm=v_swa_k_norm, v_swa_sinks=v_swa_sinks, v_mla_q_lora_norm=v_mla_q_lora_norm, v_mla_w_uq=v_mla_w_uq, v_mla_kv_lora_norm=v_mla_kv_lora_norm, v_mla_w_ukv=v_mla_w_ukv, v_mla_q_norm=v_mla_q_norm, v_mla_k_norm=v_mla_k_norm, v_w_branch_a=v_w_branch_a, v_w_branch_b=v_w_branch_b, v_w_out=v_w_out, v_ffn2_norm=v_ffn2_norm, v_ffn2_w_gate=v_ffn2_w_gate, v_ffn2_w_up=v_ffn2_w_up, v_ffn2_w_down=v_ffn2_w_down)
    weights = {n: given[n] for n in TWIN_WEIGHTS}
    shared = {n: given[n] for n in SHARED_INPUTS}
    per_example = {n: given[n] for n in ['x', 'positions']}
    grad_fn = _jax.value_and_grad(_loss, argnums=(0, 1))

    def one_microbatch(ex, loss_target):
        ex = dict(ex)
        diff = ex.pop(TWIN_DIFF_INPUT)
        return grad_fn(weights, diff, {**shared, **ex}, loss_target)

    if N_MICROBATCH == 1:
        loss, (grad_w, grad_x) = one_microbatch(per_example, given["loss_target"])
    else:
        def body(carry, xs):
            loss_sum, grad_sum = carry
            l_k, (gw_k, gx_k) = one_microbatch(xs[0], xs[1])
            with _jax.named_scope("update"):
                return (loss_sum + l_k, _jax.tree.map(_jnp.add, grad_sum, gw_k)), gx_k

        init = (_jnp.zeros((), _jnp.float32), _jax.tree.map(_jnp.zeros_like, weights))
        (loss, grad_w), grad_x = _jax.lax.scan(body, init, (per_example, given["loss_target"]))
    with _jax.named_scope("update"):
        delta_w, new_m, new_v = {}, {}, {}
        for n in TWIN_WEIGHTS:
            delta_w[n], new_m[n], new_v[n] = _adamw(weights[n], grad_w[n], given["m_" + n], given["v_" + n])
    return (loss, grad_x, *[grad_w[n] for n in TWIN_WEIGHTS], *[delta_w[n] for n in TWIN_WEIGHTS],
            *[new_m[n] for n in TWIN_WEIGHTS], *[new_v[n] for n in TWIN_WEIGHTS])
```

```python
import functools

import numpy as np
import jax
import jax.numpy as jnp
from jax import lax
from jax.experimental import pallas as pl
from jax.experimental.pallas import tpu as pltpu

F32 = jnp.float32
BF16 = jnp.bfloat16
MESH = pl.DeviceIdType.MESH

HEAD_DIM_A = 64
N_HEADS_A = 8
N_KV_HEADS_A = 2
GROUP_A = N_HEADS_A // N_KV_HEADS_A
BLOCK = 128
N_HEADS_B = 8
Q_LORA_RANK = 256
KV_LORA_RANK = 128
QK_NOPE_DIM = 64
QK_ROPE_DIM = 32
QK_DIM_B = QK_NOPE_DIM + QK_ROPE_DIM
V_DIM_B = 64
ROPE_BASE = 10000.0
WIDTH_A = N_HEADS_A * HEAD_DIM_A
WIDTH_B = N_HEADS_B * V_DIM_B
KV_WIDTH_A = N_KV_HEADS_A * HEAD_DIM_A
EPS = 1e-6
NEG = -1e30
ADAM_LR = 0.001
ADAM_B1 = 0.9
ADAM_B2 = 0.999
ADAM_EPS = 1e-08
ADAM_WD = 0.01
ADAM_STEP = 10

N_SHARDS = 4
N_DEV = 8
LANES = 128
PACK_COLS = 1024
PACK_ROW_ALIGN = 16
VMEM_LIMIT = 48 * 1024 * 1024

PACK_NAMES = ("ffn1_w_gate", "ffn1_w_up", "ffn1_w_down", "w_in", "mla_w_uq", "mla_w_ukv",
              "w_branch_a", "w_branch_b", "w_out", "ffn2_w_gate", "ffn2_w_up", "ffn2_w_down")
SMALL_NAMES = ("ffn1_norm", "mix_norm", "swa_q_norm", "swa_k_norm", "swa_sinks", "mla_q_lora_norm",
               "mla_kv_lora_norm", "mla_q_norm", "mla_k_norm", "ffn2_norm")
WEIGHT_NAMES = ("ffn1_norm", "ffn1_w_gate", "ffn1_w_up", "ffn1_w_down", "mix_norm", "w_in", "swa_q_norm",
                "swa_k_norm", "swa_sinks", "mla_q_lora_norm", "mla_w_uq", "mla_kv_lora_norm", "mla_w_ukv",
                "mla_q_norm", "mla_k_norm", "w_branch_a", "w_branch_b", "w_out", "ffn2_norm", "ffn2_w_gate",
                "ffn2_w_up", "ffn2_w_down")


def _params(sem):
    return pltpu.CompilerParams(dimension_semantics=sem, vmem_limit_bytes=VMEM_LIMIT)


def _tile(n, want, align):
    if n <= want:
        return n
    t = (want // align) * align
    while t > align and n % t:
        t -= align
    assert t >= align and n % t == 0, (n, want, align)
    return t


def _mm_call(name, a, b, out_struct, grid, a_spec, b_spec, o_spec, dims, n_red, acc_shape, alpha=1.0, res=None):
    n_par = len(grid) - n_red

    def body(*refs):
        if res is None:
            a_ref, b_ref, o_ref, acc_ref = refs
            r_ref = None
        else:
            a_ref, b_ref, r_ref, o_ref, acc_ref = refs
        part = lax.dot_general(a_ref[...], b_ref[...], dims, preferred_element_type=F32)

        def finish(total):
            if alpha != 1.0:
                total = total * alpha
            if r_ref is not None:
                total = r_ref[...] + total
            o_ref[...] = total.astype(o_ref.dtype)

        if n_red == 0:
            finish(part)
            return
        ids = [pl.program_id(n_par + i) for i in range(n_red)]
        first = functools.reduce(jnp.logical_and, [i == 0 for i in ids])
        last = functools.reduce(jnp.logical_and, [i == grid[n_par + k] - 1 for k, i in enumerate(ids)])

        @pl.when(first)
        def _():
            acc_ref[...] = part

        @pl.when(jnp.logical_not(first))
        def _():
            acc_ref[...] += part

        @pl.when(last)
        def _():
            finish(acc_ref[...])

    in_specs = [a_spec, b_spec] + ([o_spec] if res is not None else [])
    args = (a, b) + ((res,) if res is not None else ())
    return pl.pallas_call(
        body, name=name, out_shape=out_struct, grid=grid, in_specs=in_specs, out_specs=o_spec,
        scratch_shapes=[pltpu.VMEM(acc_shape, F32)],
        compiler_params=_params(("parallel",) * n_par + ("arbitrary",) * n_red),
    )(*args)


_NN = (((1,), (0,)), ((), ()))
_NT = (((1,), (1,)), ((), ()))
_TN = (((0,), (0,)), ((), ()))


def mm_nn(name, a, b, tm=512, tn=1024, tk=1024, out_dtype=F32, alpha=1.0, res=None):
    (m, k), (_, n) = a.shape, b.shape
    tm, tn, tk = _tile(m, tm, 16), _tile(n, tn, LANES), _tile(k, tk, LANES)
    return _mm_call(name, a, b, jax.ShapeDtypeStruct((m, n), out_dtype), (m // tm, n // tn, k // tk),
                    pl.BlockSpec((tm, tk), lambda i, j, kk: (i, kk)), pl.BlockSpec((tk, tn), lambda i, j, kk: (kk, j)),
                    pl.BlockSpec((tm, tn), lambda i, j, kk: (i, j)), _NN, 1, (tm, tn), alpha, res)


def mm_nt(name, a, b, tm=512, tn=1024, tk=1024, out_dtype=F32, alpha=1.0):
    (m, n), (k, _) = a.shape, b.shape
    tm, tn, tk = _tile(m, tm, 16), _tile(k, tn, LANES), _tile(n, tk, LANES)
    return _mm_call(name, a, b, jax.ShapeDtypeStruct((m, k), out_dtype), (m // tm, k // tn, n // tk),
                    pl.BlockSpec((tm, tk), lambda i, j, kk: (i, kk)), pl.BlockSpec((tn, tk), lambda i, j, kk: (j, kk)),
                    pl.BlockSpec((tm, tn), lambda i, j, kk: (i, j)), _NT, 1, (tm, tn), alpha)


def mm_tn(name, a, b, tm=1024, tn=1024, tk=512, out_dtype=F32, alpha=1.0):
    (m, k), (_, n) = a.shape, b.shape
    tm, tn, tk = _tile(k, tm, LANES), _tile(n, tn, LANES), _tile(m, tk, 16)
    return _mm_call(name, a, b, jax.ShapeDtypeStruct((k, n), out_dtype), (k // tm, n // tn, m // tk),
                    pl.BlockSpec((tk, tm), lambda i, j, kk: (kk, i)), pl.BlockSpec((tk, tn), lambda i, j, kk: (kk, j)),
                    pl.BlockSpec((tm, tn), lambda i, j, kk: (i, j)), _TN, 1, (tm, tn), alpha)


def gmm_up(name, a, w, tm=512):
    (m, k), (g, _, n) = a.shape, w.shape
    tm = _tile(m, tm, 16)
    return _mm_call(name, a, w, jax.ShapeDtypeStruct((g, m, n), F32), (g, m // tm),
                    pl.BlockSpec((tm, k), lambda gi, i: (i, 0)), pl.BlockSpec((None, k, n), lambda gi, i: (gi, 0, 0)),
                    pl.BlockSpec((None, tm, n), lambda gi, i: (gi, i, 0)), _NN, 0, (8, LANES))


def gmm_down(name, h, w, res, alpha, tm=512):
    (g, m, n), (_, _, d) = h.shape, w.shape
    tm = _tile(m, tm, 16)
    return _mm_call(name, h, w, jax.ShapeDtypeStruct((m, d), F32), (m // tm, g),
                    pl.BlockSpec((None, tm, n), lambda i, gi: (gi, i, 0)), pl.BlockSpec((None, n, d), lambda i, gi: (gi, 0, 0)),
                    pl.BlockSpec((tm, d), lambda i, gi: (i, 0)), _NN, 1, (tm, d), alpha, res)


def gmm_nt_out(name, a, w, alpha, tm=512):
    (m, d), (g, n, _) = a.shape, w.shape
    tm = _tile(m, tm, 16)
    return _mm_call(name, a, w, jax.ShapeDtypeStruct((g, m, n), F32), (g, m // tm),
                    pl.BlockSpec((tm, d), lambda gi, i: (i, 0)), pl.BlockSpec((None, n, d), lambda gi, i: (gi, 0, 0)),
                    pl.BlockSpec((None, tm, n), lambda gi, i: (gi, i, 0)), _NT, 0, (8, LANES), alpha)


def gmm_nt_red(name, a, w, tm=512):
    (g, m, n), (_, k, _) = a.shape, w.shape
    tm = _tile(m, tm, 16)
    return _mm_call(name, a, w, jax.ShapeDtypeStruct((m, k), F32), (m // tm, g),
                    pl.BlockSpec((None, tm, n), lambda i, gi: (gi, i, 0)), pl.BlockSpec((None, k, n), lambda i, gi: (gi, 0, 0)),
                    pl.BlockSpec((tm, k), lambda i, gi: (i, 0)), _NT, 1, (tm, k))


def gmm_tn_a(name, h, b, alpha, tk=512):
    (g, m, n), (_, d) = h.shape, b.shape
    tk = _tile(m, tk, 16)
    return _mm_call(name, h, b, jax.ShapeDtypeStruct((g, n, d), F32), (g, m // tk),
                    pl.BlockSpec((None, tk, n), lambda gi, kk: (gi, kk, 0)), pl.BlockSpec((tk, d), lambda gi, kk: (kk, 0)),
                    pl.BlockSpec((None, n, d), lambda gi, kk: (gi, 0, 0)), _TN, 1, (n, d), alpha)


def gmm_tn_b(name, a, b, tk=512):
    (m, k), (g, _, n) = a.shape, b.shape
    tk = _tile(m, tk, 16)
    return _mm_call(name, a, b, jax.ShapeDtypeStruct((g, k, n), F32), (g, m // tk),
                    pl.BlockSpec((tk, k), lambda gi, kk: (kk, 0)), pl.BlockSpec((None, tk, n), lambda gi, kk: (gi, kk, 0)),
                    pl.BlockSpec((None, k, n), lambda gi, kk: (gi, 0, 0)), _TN, 1, (k, n))


def rms_fwd(name, x, gain, tm=512):
    m, d = x.shape
    tm = _tile(m, tm, 16)

    def body(x_ref, g_ref, o_ref):
        xv = x_ref[...]
        r = lax.rsqrt(jnp.mean(xv * xv, axis=-1, keepdims=True) + EPS)
        o_ref[...] = (xv * r * g_ref[...]).astype(o_ref.dtype)

    return pl.pallas_call(
        body, name=name, out_shape=jax.ShapeDtypeStruct((m, d), BF16), grid=(m // tm,),
        in_specs=[pl.BlockSpec((tm, d), lambda i: (i, 0)), pl.BlockSpec((1, d), lambda i: (0, 0))],
        out_specs=pl.BlockSpec((tm, d), lambda i: (i, 0)), compiler_params=_params(("parallel",)),
    )(x, gain)


def rms_bwd(name, x, gain, dn, dres=None, want_f32=True, want_bf16=True, tm=512):
    m, d = x.shape
    tm = _tile(m, tm, 16)
    n_out = int(want_f32) + int(want_bf16)

    def body(*refs):
        x_ref, g_ref, dn_ref = refs[:3]
        pos = 3
        r_ref = None
        if dres is not None:
            r_ref = refs[pos]
            pos += 1
        outs = refs[pos:pos + n_out]
        dg_ref = refs[pos + n_out]
        xv = x_ref[...]
        r = lax.rsqrt(jnp.mean(xv * xv, axis=-1, keepdims=True) + EPS)
        xhat = xv * r
        dnv = dn_ref[...]
        dxhat = dnv * g_ref[...]
        dx = r * (dxhat - xhat * jnp.mean(dxhat * xhat, axis=-1, keepdims=True))
        if r_ref is not None:
            dx = r_ref[...] + dx
        for o in outs:
            o[...] = dx.astype(o.dtype)
        part = jnp.sum(dnv * xhat, axis=0, keepdims=True)

        @pl.when(pl.program_id(0) == 0)
        def _():
            dg_ref[...] = part

        @pl.when(pl.program_id(0) > 0)
        def _():
            dg_ref[...] += part

    row = pl.BlockSpec((tm, d), lambda i: (i, 0))
    vec = pl.BlockSpec((1, d), lambda i: (0, 0))
    out_shape = ([jax.ShapeDtypeStruct((m, d), F32)] if want_f32 else []) + ([jax.ShapeDtypeStruct((m, d), BF16)] if want_bf16 else [])
    res = pl.pallas_call(
        body, name=name, out_shape=tuple(out_shape) + (jax.ShapeDtypeStruct((1, d), F32),), grid=(m // tm,),
        in_specs=[row, vec, row] + ([row] if dres is not None else []),
        out_specs=tuple([row] * n_out) + (vec,), compiler_params=_params(("arbitrary",)),
    )(*((x, gain, dn) + ((dres,) if dres is not None else ())))
    return res


def swiglu_fwd(name, gu, tm=512):
    _, g, m, n = gu.shape
    tm = _tile(m, tm, 16)

    def body(gu_ref, o_ref):
        gate, up = gu_ref[0], gu_ref[1]
        o_ref[...] = (gate * jax.nn.sigmoid(gate) * up).astype(o_ref.dtype)

    return pl.pallas_call(
        body, name=name, out_shape=jax.ShapeDtypeStruct((g, m, n), BF16), grid=(g, m // tm),
        in_specs=[pl.BlockSpec((2, None, tm, n), lambda gi, i: (0, gi, i, 0))],
        out_specs=pl.BlockSpec((None, tm, n), lambda gi, i: (gi, i, 0)), compiler_params=_params(("parallel", "parallel")),
    )(gu)


def swiglu_bwd(name, gu, dh, tm=512):
    _, g, m, n = gu.shape
    tm = _tile(m, tm, 16)

    def body(gu_ref, dh_ref, o_ref):
        gate, up, dhv = gu_ref[0], gu_ref[1], dh_ref[...]
        s = jax.nn.sigmoid(gate)
        o_ref[0] = (dhv * up * (s * (1.0 + gate * (1.0 - s)))).astype(o_ref.dtype)
        o_ref[1] = (dhv * (gate * s)).astype(o_ref.dtype)

    return pl.pallas_call(
        body, name=name, out_shape=jax.ShapeDtypeStruct((2, g, m, n), BF16), grid=(g, m // tm),
        in_specs=[pl.BlockSpec((2, None, tm, n), lambda gi, i: (0, gi, i, 0)), pl.BlockSpec((None, tm, n), lambda gi, i: (gi, i, 0))],
        out_specs=pl.BlockSpec((2, None, tm, n), lambda gi, i: (0, gi, i, 0)), compiler_params=_params(("parallel", "parallel")),
    )(gu, dh)


def gate_fwd(name, proj, ya, yb, off_a, off_b, tm=256):
    m, d = ya.shape
    w = proj.shape[1]
    tm = _tile(m, tm, 16)

    def body(p_ref, ya_ref, yb_ref, o_ref):
        ga, gb = p_ref[:, off_a:off_a + d], p_ref[:, off_b:off_b + d]
        o_ref[...] = (jax.nn.sigmoid(ga) * ya_ref[...] + jax.nn.sigmoid(gb) * yb_ref[...]).astype(o_ref.dtype)

    row = pl.BlockSpec((tm, d), lambda i: (i, 0))
    return pl.pallas_call(
        body, name=name, out_shape=jax.ShapeDtypeStruct((m, d), BF16), grid=(m // tm,),
        in_specs=[pl.BlockSpec((tm, w), lambda i: (i, 0)), row, row], out_specs=row, compiler_params=_params(("parallel",)),
    )(proj, ya, yb)


def gate_bwd(name, proj, ya, yb, dmerged, off_a, off_b, tm=256):
    m, d = ya.shape
    w = proj.shape[1]
    tm = _tile(m, tm, 16)

    def body(p_ref, ya_ref, yb_ref, dm_ref, dya_ref, dyb_ref, dga_ref, dgb_ref):
        sa, sb = jax.nn.sigmoid(p_ref[:, off_a:off_a + d]), jax.nn.sigmoid(p_ref[:, off_b:off_b + d])
        dm = dm_ref[...]
        dya_ref[...] = (dm * sa).astype(BF16)
        dyb_ref[...] = (dm * sb).astype(BF16)
        dga_ref[...] = (dm * ya_ref[...] * (sa * (1.0 - sa))).astype(BF16)
        dgb_ref[...] = (dm * yb_ref[...] * (sb * (1.0 - sb))).astype(BF16)

    row = pl.BlockSpec((tm, d), lambda i: (i, 0))
    o = jax.ShapeDtypeStruct((m, d), BF16)
    return pl.pallas_call(
        body, name=name, out_shape=(o, o, o, o), grid=(m // tm,),
        in_specs=[pl.BlockSpec((tm, w), lambda i: (i, 0)), row, row, row], out_specs=(row, row, row, row),
        compiler_params=_params(("parallel",)),
    )(proj, ya, yb, dmerged)


def loss_fwd_bwd(name, y, target, tm=512):
    m, d = y.shape
    tm = _tile(m, tm, 16)

    def body(y_ref, t_ref, l_ref, dy_ref, dyb_ref):
        err = y_ref[...] - t_ref[...]
        dy = err * (1.0 / d)
        dy_ref[...] = dy
        dyb_ref[...] = dy.astype(BF16)
        part = 0.5 * jnp.sum(jnp.mean(err * err, axis=-1, keepdims=True), axis=0, keepdims=True)

        @pl.when(pl.program_id(0) == 0)
        def _():
            l_ref[...] = part

        @pl.when(pl.program_id(0) > 0)
        def _():
            l_ref[...] += part

    row = pl.BlockSpec((tm, d), lambda i: (i, 0))
    return pl.pallas_call(
        body, name=name, grid=(m // tm,),
        out_shape=(jax.ShapeDtypeStruct((1, 1), F32), jax.ShapeDtypeStruct((m, d), F32), jax.ShapeDtypeStruct((m, d), BF16)),
        in_specs=[row, row], out_specs=(pl.BlockSpec((1, 1), lambda i: (0, 0)), row, row),
        compiler_params=_params(("arbitrary",)),
    )(y, target)


def adamw(name, w, g, m, v):
    r, c = w.shape
    tr = _tile(r, max(8, (2 * 1024 * 1024) // (4 * c) // 8 * 8), 8)
    c1 = np.float32(1.0 - ADAM_B1 ** ADAM_STEP)
    c2 = np.float32(1.0 - ADAM_B2 ** ADAM_STEP)

    def body(w_ref, g_ref, m_ref, v_ref, d_ref, nm_ref, nv_ref):
        gv = g_ref[...]
        nm = ADAM_B1 * m_ref[...] + (1.0 - ADAM_B1) * gv
        nv = ADAM_B2 * v_ref[...] + (1.0 - ADAM_B2) * (gv * gv)
        d_ref[...] = -ADAM_LR * ((nm / c1) / (jnp.sqrt(nv / c2) + ADAM_EPS) + ADAM_WD * w_ref[...])
        nm_ref[...] = nm
        nv_ref[...] = nv

    row = pl.BlockSpec((tr, c), lambda i: (i, 0))
    o = jax.ShapeDtypeStruct((r, c), F32)
    return pl.pallas_call(
        body, name=name, out_shape=(o, o, o), grid=(r // tr,), in_specs=[row] * 4, out_specs=(row, row, row),
        compiler_params=_params(("parallel",)),
    )(w, g, m, v)


def _rope_rot(d):
    r = np.zeros((d, d), np.float32)
    half = QK_ROPE_DIM // 2
    base = d - QK_ROPE_DIM
    for j in range(half):
        r[base + half + j, base + j] = -1.0
        r[base + j, base + half + j] = 1.0
    return r


def qk_prep_fwd(name, x, gain, rope=None):
    b, h, s, d = x.shape

    def body(*refs):
        if rope is None:
            x_ref, g_ref, o_ref = refs
        else:
            x_ref, g_ref, c_ref, s_ref, r_ref, o_ref = refs
        xv = x_ref[...]
        y = xv * lax.rsqrt(jnp.mean(xv * xv, axis=-1, keepdims=True) + EPS) * g_ref[...]
        if rope is not None:
            rot = jnp.dot(y, r_ref[...], precision=lax.Precision.HIGHEST, preferred_element_type=F32)
            y = y * c_ref[...] + rot * s_ref[...]
        o_ref[...] = y.astype(o_ref.dtype)

    xs = pl.BlockSpec((None, None, s, d), lambda bi, hi: (bi, hi, 0, 0))
    in_specs = [xs, pl.BlockSpec((1, d), lambda bi, hi: (0, 0))]
    args = [x, gain]
    if rope is not None:
        tab = pl.BlockSpec((None, s, d), lambda bi, hi: (bi, 0, 0))
        in_specs += [tab, tab, pl.BlockSpec((d, d), lambda bi, hi: (0, 0))]
        args += [rope[0], rope[1], jnp.asarray(_rope_rot(d))]
    return pl.pallas_call(
        body, name=name, out_shape=jax.ShapeDtypeStruct(x.shape, BF16), grid=(b, h), in_specs=in_specs, out_specs=xs,
        compiler_params=_params(("parallel", "parallel")),
    )(*args)


def qk_prep_bwd(name, x, gain, dxh, rope=None, head_sum=False):
    b, h, s, d = x.shape

    def body(*refs):
        if rope is None:
            x_ref, g_ref, dy_ref = refs[:3]
            outs = refs[3:]
        else:
            x_ref, g_ref, dy_ref, c_ref, s_ref, rt_ref = refs[:6]
            outs = refs[6:]
        dx_ref, dg_ref = outs[0], outs[-1]
        dy = dy_ref[...]
        if rope is not None:
            dy = dy * c_ref[...] + jnp.dot(dy * s_ref[...], rt_ref[...], precision=lax.Precision.HIGHEST, preferred_element_type=F32)
        xv = x_ref[...]
        r = lax.rsqrt(jnp.mean(xv * xv, axis=-1, keepdims=True) + EPS)
        xhat = xv * r
        dxhat = dy * g_ref[...]
        dx = r * (dxhat - xhat * jnp.mean(dxhat * xhat, axis=-1, keepdims=True))
        dx_ref[...] = dx.astype(dx_ref.dtype)
        part = jnp.sum(dy * xhat, axis=0, keepdims=True)
        first = jnp.logical_and(pl.program_id(0) == 0, pl.program_id(1) == 0)

        @pl.when(first)
        def _():
            dg_ref[...] = part

        @pl.when(jnp.logical_not(first))
        def _():
            dg_ref[...] += part

        if head_sum:
            hs_ref = outs[1]

            @pl.when(pl.program_id(1) == 0)
            def _():
                hs_ref[...] = dx

            @pl.when(pl.program_id(1) > 0)
            def _():
                hs_ref[...] += dx

    xs = pl.BlockSpec((None, None, s, d), lambda bi, hi: (bi, hi, 0, 0))
    vec = pl.BlockSpec((1, d), lambda bi, hi: (0, 0))
    tab = pl.BlockSpec((None, s, d), lambda bi, hi: (bi, 0, 0))
    in_specs = [xs, vec, xs]
    args = [x, gain, dxh]
    if rope is not None:
        in_specs += [tab, tab, pl.BlockSpec((d, d), lambda bi, hi: (0, 0))]
        args += [rope[0], rope[1], jnp.asarray(_rope_rot(d).T.copy())]
    out_shape = [jax.ShapeDtypeStruct(x.shape, BF16)]
    out_specs = [xs]
    if head_sum:
        out_shape.append(jax.ShapeDtypeStruct((b, s, d), F32))
        out_specs.append(tab)
    out_shape.append(jax.ShapeDtypeStruct((1, d), F32))
    out_specs.append(vec)
    return pl.pallas_call(
        body, name=name, out_shape=tuple(out_shape), grid=(b, h), in_specs=in_specs, out_specs=tuple(out_specs),
        compiler_params=_params(("arbitrary", "arbitrary")),
    )(*args)


def _swa_specs(s):
    nb = s // BLOCK
    q = pl.BlockSpec((None, GROUP_A, BLOCK, HEAD_DIM_A), lambda b, kv, n: (b, kv, n, 0))
    kprev = pl.BlockSpec((None, None, BLOCK, HEAD_DIM_A), lambda b, kv, n: (b, kv, jnp.maximum(n - 1, 0), 0))
    kcur = pl.BlockSpec((None, None, BLOCK, HEAD_DIM_A), lambda b, kv, n: (b, kv, n, 0))
    pcol = pl.BlockSpec((None, BLOCK, 1), lambda b, kv, n: (b, n, 0))
    prow_prev = pl.BlockSpec((None, 1, BLOCK), lambda b, kv, n: (b, 0, jnp.maximum(n - 1, 0)))
    prow_cur = pl.BlockSpec((None, 1, BLOCK), lambda b, kv, n: (b, 0, n))
    smem = pl.BlockSpec(memory_space=pltpu.SMEM)
    return nb, q, kprev, kcur, pcol, prow_prev, prow_cur, smem


def _swa_probs(q, kk, dist, valid, slope, sink):
    sc = lax.dot_general(q, kk, _NT, preferred_element_type=F32) * (HEAD_DIM_A ** -0.5)
    sc = sc - slope * dist
    sc = jnp.where(valid, sc, NEG)
    m = jnp.maximum(jnp.max(sc, axis=-1, keepdims=True), sink)
    e = jnp.exp(sc - m)
    es = jnp.exp(sink - m)
    inv = 1.0 / (jnp.sum(e, axis=-1, keepdims=True) + es)
    return e * inv, es * inv


def _swa_window(n, kp_ref, kc_ref, vp_ref, vc_ref, pc_ref, prp_ref, prc_ref):
    kk = jnp.concatenate([kp_ref[...], kc_ref[...]], axis=0)
    vv = jnp.concatenate([vp_ref[...], vc_ref[...]], axis=0)
    dist = pc_ref[...] - jnp.concatenate([prp_ref[...], prc_ref[...]], axis=1)
    qi = lax.broadcasted_iota(jnp.int32, (BLOCK, 2 * BLOCK), 0) + BLOCK
    ki = lax.broadcasted_iota(jnp.int32, (BLOCK, 2 * BLOCK), 1)
    diff = qi - ki
    valid = (diff >= 0) & (diff < BLOCK) & ((n > 0) | (ki >= BLOCK))
    return kk, vv, dist, valid


def swa_fwd(name, q, k, v, pos_col, pos_row, slopes, sinks):
    b, _, s, _ = q.shape
    nb, qs, kprev, kcur, pcol, prp, prc, smem = _swa_specs(s)

    def body(q_ref, kp_ref, kc_ref, vp_ref, vc_ref, pc_ref, prp_ref, prc_ref, sl_ref, sk_ref, o_ref):
        kv, n = pl.program_id(1), pl.program_id(2)
        kk, vv, dist, valid = _swa_window(n, kp_ref, kc_ref, vp_ref, vc_ref, pc_ref, prp_ref, prc_ref)
        for g in range(GROUP_A):
            hd = kv * GROUP_A + g
            p, _ = _swa_probs(q_ref[g], kk, dist, valid, sl_ref[hd], sk_ref[hd])
            o_ref[g] = jnp.dot(p.astype(BF16), vv, preferred_element_type=F32)

    return pl.pallas_call(
        body, name=name, out_shape=jax.ShapeDtypeStruct(q.shape, F32), grid=(b, N_KV_HEADS_A, nb),
        in_specs=[qs, kprev, kcur, kprev, kcur, pcol, prp, prc, smem, smem], out_specs=qs,
        compiler_params=_params(("parallel", "parallel", "parallel")),
    )(q, k, k, v, v, pos_col, pos_row, pos_row, slopes, sinks)


def swa_bwd(name, q, k, v, pos_col, pos_row, slopes, sinks, do):
    b, _, s, _ = q.shape
    nb, qs, kprev, kcur, pcol, prp, prc, smem = _swa_specs(s)

    def body(q_ref, kp_ref, kc_ref, vp_ref, vc_ref, pc_ref, prp_ref, prc_ref, sl_ref, sk_ref, do_ref, dq_ref, dk_ref, dv_ref, ds_ref):
        bi, kv, n = pl.program_id(0), pl.program_id(1), pl.program_id(2)
        kk, vv, dist, valid = _swa_window(n, kp_ref, kc_ref, vp_ref, vc_ref, pc_ref, prp_ref, prc_ref)

        @pl.when((bi == 0) & (kv == 0) & (n == 0))
        def _():
            ds_ref[...] = jnp.zeros_like(ds_ref)

        @pl.when(n == 0)
        def _():
            dk_ref[...] = jnp.zeros_like(dk_ref)
            dv_ref[...] = jnp.zeros_like(dv_ref)

        dkk = jnp.zeros((2 * BLOCK, HEAD_DIM_A), F32)
        dvv = jnp.zeros((2 * BLOCK, HEAD_DIM_A), F32)
        head_row = lax.broadcasted_iota(jnp.int32, (N_HEADS_A, LANES), 0)
        dsink = jnp.zeros((N_HEADS_A, LANES), F32)
        for g in range(GROUP_A):
            hd = kv * GROUP_A + g
            qg = q_ref[g]
            p, ps = _swa_probs(qg, kk, dist, valid, sl_ref[hd], sk_ref[hd])
            dob = do_ref[g].astype(BF16)
            dvv = dvv + lax.dot_general(p.astype(BF16), dob, _TN, preferred_element_type=F32)
            dp = lax.dot_general(dob, vv, _NT, preferred_element_type=F32)
            rs = jnp.sum(p * dp, axis=-1, keepdims=True)
            dsb = (p * (dp - rs) * (HEAD_DIM_A ** -0.5)).astype(BF16)
            dq_ref[g] = jnp.dot(dsb, kk, preferred_element_type=F32)
            dkk = dkk + lax.dot_general(dsb, qg, _TN, preferred_element_type=F32)
            dsink = dsink + jnp.where(head_row == hd, -jnp.sum(ps * rs), 0.0)
        ds_ref[...] += dsink

        @pl.when(n > 0)
        def _():
            start = pl.multiple_of((n - 1) * BLOCK, BLOCK)
            dk_ref[pl.ds(start, 2 * BLOCK), :] += dkk
            dv_ref[pl.ds(start, 2 * BLOCK), :] += dvv

        @pl.when(n == 0)
        def _():
            dk_ref[0:BLOCK, :] += dkk[BLOCK:, :]
            dv_ref[0:BLOCK, :] += dvv[BLOCK:, :]

    kv_full = pl.BlockSpec((None, None, s, HEAD_DIM_A), lambda bi, kv, n: (bi, kv, 0, 0))
    return pl.pallas_call(
        body, name=name, grid=(b, N_KV_HEADS_A, nb),
        out_shape=(jax.ShapeDtypeStruct(q.shape, F32), jax.ShapeDtypeStruct(k.shape, F32), jax.ShapeDtypeStruct(k.shape, F32),
                   jax.ShapeDtypeStruct((N_HEADS_A, LANES), F32)),
        in_specs=[qs, kprev, kcur, kprev, kcur, pcol, prp, prc, smem, smem, qs],
        out_specs=(qs, kv_full, kv_full, pl.BlockSpec((N_HEADS_A, LANES), lambda bi, kv, n: (0, 0))),
        compiler_params=_params(("arbitrary", "arbitrary", "arbitrary")),
    )(q, k, k, v, v, pos_col, pos_row, pos_row, slopes, sinks, do)


MLA_TQ = 256
MLA_TK = 256


def _mla_scores(q, k, i, j, tq, tk):
    sc = lax.dot_general(q, k, _NT, preferred_element_type=F32) * (QK_DIM_B ** -0.5)
    tq_idx = i * tq + lax.broadcasted_iota(jnp.int32, (tq, tk), 0)
    tk_idx = j * tk + lax.broadcasted_iota(jnp.int32, (tq, tk), 1)
    return jnp.where(tq_idx >= tk_idx, sc, NEG)


def mla_fwd(name, q, k, v):
    b, h, s, dq = q.shape
    dv = v.shape[-1]
    tq, tk = _tile(s, MLA_TQ, BLOCK), _tile(s, MLA_TK, BLOCK)
    nq, nk = s // tq, s // tk

    def body(q_ref, k_ref, v_ref, o_ref, lse_ref, m_sc, l_sc, acc_sc):
        i, j = pl.program_id(2), pl.program_id(3)

        @pl.when(j == 0)
        def _():
            m_sc[...] = jnp.full_like(m_sc, NEG)
            l_sc[...] = jnp.zeros_like(l_sc)
            acc_sc[...] = jnp.zeros_like(acc_sc)

        @pl.when(j * tk <= i * tq + (tq - 1))
        def _():
            sc = _mla_scores(q_ref[...], k_ref[...], i, j, tq, tk)
            m_old = m_sc[...]
            m_new = jnp.maximum(m_old, jnp.max(sc, axis=-1, keepdims=True))
            a = jnp.exp(m_old - m_new)
            p = jnp.exp(sc - m_new)
            l_sc[...] = a * l_sc[...] + jnp.sum(p, axis=-1, keepdims=True)
            acc_sc[...] = a * acc_sc[...] + jnp.dot(p.astype(BF16), v_ref[...], preferred_element_type=F32)
            m_sc[...] = m_new

        @pl.when(j == nk - 1)
        def _():
            o_ref[...] = acc_sc[...] / l_sc[...]
            lse_ref[...] = m_sc[...] + jnp.log(l_sc[...])

    def kv_map(bi, hi, i, j):
        return (bi, hi, jnp.minimum(j, (i * tq + tq - 1) // tk), 0)

    return pl.pallas_call(
        body, name=name, grid=(b, h, nq, nk),
        out_shape=(jax.ShapeDtypeStruct((b, h, s, dv), F32), jax.ShapeDtypeStruct((b, h, s, 1), F32)),
        in_specs=[pl.BlockSpec((None, None, tq, dq), lambda bi, hi, i, j: (bi, hi, i, 0)),
                  pl.BlockSpec((None, None, tk, dq), kv_map), pl.BlockSpec((None, None, tk, dv), kv_map)],
        out_specs=(pl.BlockSpec((None, None, tq, dv), lambda bi, hi, i, j: (bi, hi, i, 0)),
                   pl.BlockSpec((None, None, tq, 1), lambda bi, hi, i, j: (bi, hi, i, 0))),
        scratch_shapes=[pltpu.VMEM((tq, 1), F32), pltpu.VMEM((tq, 1), F32), pltpu.VMEM((tq, dv), F32)],
        compiler_params=_params(("parallel", "parallel", "parallel", "arbitrary")),
    )(q, k, v)


def _mla_ds(q, k, v, o, do, lse, i, j, tq, tk):
    sc = _mla_scores(q, k, i, j, tq, tk)
    p = jnp.exp(sc - lse)
    dob = do.astype(BF16)
    dp = lax.dot_general(dob, v, _NT, preferred_element_type=F32)
    delta = jnp.sum(o * do, axis=-1, keepdims=True)
    ds = (p * (dp - delta) * (QK_DIM_B ** -0.5)).astype(BF16)
    return p, ds, dob


def mla_bwd_dq(name, q, k, v, o, do, lse):
    b, h, s, dq = q.shape
    dv = v.shape[-1]
    tq, tk = _tile(s, MLA_TQ, BLOCK), _tile(s, MLA_TK, BLOCK)
    nq, nk = s // tq, s // tk

    def body(q_ref, k_ref, v_ref, o_ref, do_ref, lse_ref, dq_ref, acc_sc):
        i, j = pl.program_id(2), pl.program_id(3)

        @pl.when(j == 0)
        def _():
            acc_sc[...] = jnp.zeros_like(acc_sc)

        @pl.when(j * tk <= i * tq + (tq - 1))
        def _():
            _, ds, _ = _mla_ds(q_ref[...], k_ref[...], v_ref[...], o_ref[...], do_ref[...], lse_ref[...], i, j, tq, tk)
            acc_sc[...] += jnp.dot(ds, k_ref[...], preferred_element_type=F32)

        @pl.when(j == nk - 1)
        def _():
            dq_ref[...] = acc_sc[...]

    def kv_map(bi, hi, i, j):
        return (bi, hi, jnp.minimum(j, (i * tq + tq - 1) // tk), 0)

    def q_map(bi, hi, i, j):
        return (bi, hi, i, 0)

    return pl.pallas_call(
        body, name=name, grid=(b, h, nq, nk), out_shape=jax.ShapeDtypeStruct(q.shape, F32),
        in_specs=[pl.BlockSpec((None, None, tq, dq), q_map), pl.BlockSpec((None, None, tk, dq), kv_map),
                  pl.BlockSpec((None, None, tk, dv), kv_map), pl.BlockSpec((None, None, tq, dv), q_map),
                  pl.BlockSpec((None, None, tq, dv), q_map), pl.BlockSpec((None, None, tq, 1), q_map)],
        out_specs=pl.BlockSpec((None, None, tq, dq), q_map),
        scratch_shapes=[pltpu.VMEM((tq, dq), F32)],
        compiler_params=_params(("parallel", "parallel", "parallel", "arbitrary")),
    )(q, k, v, o, do, lse)


def mla_bwd_dkv(name, q, k, v, o, do, lse):
    b, h, s, dq = q.shape
    dv = v.shape[-1]
    tq, tk = _tile(s, MLA_TQ, BLOCK), _tile(s, MLA_TK, BLOCK)
    nq, nk = s // tq, s // tk

    def body(q_ref, k_ref, v_ref, o_ref, do_ref, lse_ref, dk_ref, dv_ref, dk_sc, dv_sc):
        j, i = pl.program_id(2), pl.program_id(3)

        @pl.when(i == 0)
        def _():
            dk_sc[...] = jnp.zeros_like(dk_sc)
            dv_sc[...] = jnp.zeros_like(dv_sc)

        @pl.when(j * tk <= i * tq + (tq - 1))
        def _():
            p, ds, dob = _mla_ds(q_ref[...], k_ref[...], v_ref[...], o_ref[...], do_ref[...], lse_ref[...], i, j, tq, tk)
            dv_sc[...] += lax.dot_general(p.astype(BF16), dob, _TN, preferred_element_type=F32)
            dk_sc[...] += lax.dot_general(ds, q_ref[...], _TN, preferred_element_type=F32)

        @pl.when(i == nq - 1)
        def _():
            dk_ref[...] = dk_sc[...]
            dv_ref[...] = dv_sc[...]

    def q_map(bi, hi, j, i):
        return (bi, hi, jnp.maximum(i, (j * tk) // tq), 0)

    def kv_map(bi, hi, j, i):
        return (bi, hi, j, 0)

    return pl.pallas_call(
        body, name=name, grid=(b, h, nk, nq),
        out_shape=(jax.ShapeDtypeStruct(k.shape, F32), jax.ShapeDtypeStruct(v.shape, F32)),
        in_specs=[pl.BlockSpec((None, None, tq, dq), q_map), pl.BlockSpec((None, None, tk, dq), kv_map),
                  pl.BlockSpec((None, None, tk, dv), kv_map), pl.BlockSpec((None, None, tq, dv), q_map),
                  pl.BlockSpec((None, None, tq, dv), q_map), pl.BlockSpec((None, None, tq, 1), q_map)],
        out_specs=(pl.BlockSpec((None, None, tk, dq), kv_map), pl.BlockSpec((None, None, tk, dv), kv_map)),
        scratch_shapes=[pltpu.VMEM((tk, dq), F32), pltpu.VMEM((tk, dv), F32)],
        compiler_params=_params(("parallel", "parallel", "parallel", "arbitrary")),
    )(q, k, v, o, do, lse)


_HBM = pl.BlockSpec(memory_space=pltpu.HBM)


def _place():
    x, y, c = lax.axis_index("x"), lax.axis_index("y"), lax.axis_index("c")
    chips = [(1 - x, y), (x, 1 - y), (1 - x, 1 - y)]
    return x, y, c, chips


def _remote(src, dst, send_sem, recv_sem, dev):
    return pltpu.make_async_remote_copy(src_ref=src, dst_ref=dst, send_sem=send_sem, recv_sem=recv_sem,
                                        device_id=dev, device_id_type=MESH)


def allgather_weights(pack):
    _, r, cols = pack.shape

    def body(p_ref, o_ref, send_sems, recv_sems, local_sems):
        x, y, c, chips = _place()
        s_me = 2 * x + y
        sibling = (x, y, 1 - c)
        mine = [pltpu.make_async_copy(p_ref.at[l], o_ref.at[l, s_me], local_sems.at[l]) for l in range(2)]
        for cp in mine:
            cp.start()
        first = [_remote(p_ref.at[c], o_ref.at[c, s_me], send_sems.at[j], recv_sems.at[j], (cx, cy, c))
                 for j, (cx, cy) in enumerate(chips)]
        for cp in first:
            cp.start()
        passed = []
        for j, (cx, cy) in enumerate(chips):
            blk = o_ref.at[c, 2 * cx + cy]
            _remote(blk, blk, send_sems.at[j], recv_sems.at[j], (cx, cy, c)).wait_recv()
            fwd = _remote(blk, blk, send_sems.at[3 + j], recv_sems.at[3 + j], sibling)
            fwd.start()
            passed.append(fwd)
        for j, (cx, cy) in enumerate(chips):
            blk = o_ref.at[1 - c, 2 * cx + cy]
            _remote(blk, blk, send_sems.at[3 + j], recv_sems.at[3 + j], sibling).wait_recv()
        for cp in first + passed:
            cp.wait_send()
        for cp in mine:
            cp.wait()

    return pl.pallas_call(
        body, name="allgather_weights", out_shape=jax.ShapeDtypeStruct((2, N_SHARDS, r, cols), pack.dtype),
        in_specs=[_HBM], out_specs=_HBM,
        scratch_shapes=[pltpu.SemaphoreType.DMA((6,)), pltpu.SemaphoreType.DMA((6,)), pltpu.SemaphoreType.DMA((2,))],
    )(pack)


def sibling_swap(grads):
    _, ns, r, cols = grads.shape

    def body(g_ref, o_ref, send_sem, recv_sem):
        x, y, c, _ = _place()
        cp = _remote(g_ref.at[1 - c], o_ref, send_sem, recv_sem, (x, y, 1 - c))
        cp.start()
        cp.wait()

    return pl.pallas_call(
        body, name="sibling_swap", out_shape=jax.ShapeDtypeStruct((ns, r, cols), grads.dtype),
        in_specs=[_HBM], out_specs=_HBM, scratch_shapes=[pltpu.SemaphoreType.DMA, pltpu.SemaphoreType.DMA],
    )(grads)


def add_own_slab(grads, other, c_idx, tr=256):
    _, n, cols = grads.shape
    tr = _tile(n, tr, 16)

    def body(c_ref, g_ref, o_ref, out_ref):
        out_ref[...] = (g_ref[...] + o_ref[...]).astype(out_ref.dtype)

    return pl.pallas_call(
        body, name="add_own_slab", out_shape=jax.ShapeDtypeStruct((n, cols), BF16),
        grid_spec=pltpu.PrefetchScalarGridSpec(
            num_scalar_prefetch=1, grid=(n // tr,),
            in_specs=[pl.BlockSpec((None, tr, cols), lambda i, c_ref: (c_ref[0], i, 0)), pl.BlockSpec((tr, cols), lambda i, c_ref: (i, 0))],
            out_specs=pl.BlockSpec((tr, cols), lambda i, c_ref: (i, 0))),
        compiler_params=_params(("parallel",)),
    )(c_idx, grads, other)


def scatter_to_owners(part):
    ns, r, cols = part.shape

    def body(p_ref, o_ref, send_sems, recv_sems, local_sem):
        x, y, c, chips = _place()
        s_me = 2 * x + y
        own = pltpu.make_async_copy(p_ref.at[s_me], o_ref.at[s_me], local_sem)
        own.start()
        sends = [_remote(p_ref.at[2 * cx + cy], o_ref.at[s_me], send_sems.at[j], recv_sems.at[j], (cx, cy, c))
                 for j, (cx, cy) in enumerate(chips)]
        for cp in sends:
            cp.start()
        for j, (cx, cy) in enumerate(chips):
            slot = o_ref.at[2 * cx + cy]
            _remote(slot, slot, send_sems.at[j], recv_sems.at[j], (cx, cy, c)).wait_recv()
        for cp in sends:
            cp.wait_send()
        own.wait()

    return pl.pallas_call(
        body, name="scatter_to_owners", out_shape=jax.ShapeDtypeStruct((ns, r, cols), part.dtype),
        in_specs=[_HBM], out_specs=_HBM,
        scratch_shapes=[pltpu.SemaphoreType.DMA((3,)), pltpu.SemaphoreType.DMA((3,)), pltpu.SemaphoreType.DMA],
    )(part)


def sum_slots(slots, tr=256):
    ns, r, cols = slots.shape
    tr = _tile(r, tr, 16)

    def body(s_ref, o_ref):
        total = s_ref[0].astype(F32)
        for k in range(1, ns):
            total = total + s_ref[k].astype(F32)
        o_ref[...] = total

    return pl.pallas_call(
        body, name="sum_slots", out_shape=jax.ShapeDtypeStruct((r, cols), F32), grid=(r // tr,),
        in_specs=[pl.BlockSpec((ns, tr, cols), lambda i: (0, i, 0))], out_specs=pl.BlockSpec((tr, cols), lambda i: (i, 0)),
        compiler_params=_params(("parallel",)),
    )(slots)


def sibling_share(final):
    r, cols = final.shape

    def body(f_ref, o_ref, send_sem, recv_sem, local_sem):
        x, y, c, _ = _place()
        own = pltpu.make_async_copy(f_ref, o_ref.at[c], local_sem)
        own.start()
        cp = _remote(f_ref, o_ref.at[c], send_sem, recv_sem, (x, y, 1 - c))
        cp.start()
        theirs = o_ref.at[1 - c]
        _remote(theirs, theirs, send_sem, recv_sem, (x, y, 1 - c)).wait_recv()
        cp.wait_send()
        own.wait()

    return pl.pallas_call(
        body, name="sibling_share", out_shape=jax.ShapeDtypeStruct((2, r, cols), final.dtype),
        in_specs=[_HBM], out_specs=_HBM,
        scratch_shapes=[pltpu.SemaphoreType.DMA, pltpu.SemaphoreType.DMA, pltpu.SemaphoreType.DMA],
    )(final)


def small_allreduce(vec):
    r, cols = vec.shape

    def body(v_ref, o_ref, buf, send_sems, recv_sems):
        x, y, c, _ = _place()
        me = 4 * x + 2 * y + c
        buf[me] = v_ref[...]
        peers = []
        for k in range(1, N_DEV):
            px = (1 - x) if (k & 4) else x
            py = (1 - y) if (k & 2) else y
            pc = (1 - c) if (k & 1) else c
            peers.append((px, py, pc))
        sends = [_remote(buf.at[me], buf.at[me], send_sems.at[k], recv_sems.at[k], peer) for k, peer in enumerate(peers)]
        for cp in sends:
            cp.start()
        for k, (px, py, pc) in enumerate(peers):
            slot = buf.at[4 * px + 2 * py + pc]
            _remote(slot, slot, send_sems.at[k], recv_sems.at[k], (px, py, pc)).wait_recv()
        for cp in sends:
            cp.wait_send()
        total = buf[0]
        for k in range(1, N_DEV):
            total = total + buf[k]
        o_ref[...] = total

    return pl.pallas_call(
        body, name="small_allreduce", out_shape=jax.ShapeDtypeStruct((r, cols), F32),
        in_specs=[pl.BlockSpec(memory_space=pltpu.VMEM)], out_specs=pl.BlockSpec(memory_space=pltpu.VMEM),
        scratch_shapes=[pltpu.VMEM((N_DEV, r, cols), F32), pltpu.SemaphoreType.DMA((N_DEV - 1,)), pltpu.SemaphoreType.DMA((N_DEV - 1,))],
    )(vec)


ARG_NAMES = (("x", "positions") + WEIGHT_NAMES + ("loss_target",) + tuple("m_" + n for n in WEIGHT_NAMES)
             + tuple("v_" + n for n in WEIGHT_NAMES))

_O_QA, _O_KA, _O_VA = 0, WIDTH_A, WIDTH_A + KV_WIDTH_A
_O_CQ = _O_VA + KV_WIDTH_A
_O_CKV = _O_CQ + Q_LORA_RANK
_O_GA = _O_CKV + KV_LORA_RANK


def _rows_of(n_elems):
    return -(-n_elems // PACK_COLS)


def _pack_layout(shard_shapes):
    offs, row = {}, 0
    for name in PACK_NAMES:
        offs[name] = row
        row += _rows_of(shard_shapes[name][0] * shard_shapes[name][1])
    return offs, -(-row // PACK_ROW_ALIGN) * PACK_ROW_ALIGN


def _to_rows(a):
    flat = a.reshape(-1)
    rows = _rows_of(flat.shape[0])
    if rows * PACK_COLS != flat.shape[0]:
        flat = jnp.pad(flat, (0, rows * PACK_COLS - flat.shape[0]))
    return flat.reshape(rows, PACK_COLS)


def _from_rows(slab, off, shape):
    rows = _rows_of(shape[0] * shape[1])
    return slab[off:off + rows].reshape(-1)[:shape[0] * shape[1]].reshape(shape)


def _pack_slab(pieces, rows_total, dtype):
    parts = [_to_rows(pieces[name].astype(dtype)) for name in PACK_NAMES]
    used = sum(p.shape[0] for p in parts)
    if used < rows_total:
        parts.append(jnp.zeros((rows_total - used, PACK_COLS), dtype))
    return jnp.concatenate(parts, axis=0)


def _heads(z, b, s, h, d):
    return z.reshape(b, s, h, d).transpose(0, 2, 1, 3)


def _unheads(z):
    b, h, s, d = z.shape
    return z.transpose(0, 2, 1, 3).reshape(b * s, h * d)


def _ffn_fwd(tag, x, gain, wgu, wd):
    t = x.shape[0]
    nb = rms_fwd(tag + "_rms", x, gain)
    gu = gmm_up(tag + "_up", nb, wgu)
    gu4 = gu.reshape(2, N_SHARDS, t, gu.shape[-1])
    h = swiglu_fwd(tag + "_act", gu4)
    out = gmm_down(tag + "_down", h, wd, x, 0.5)
    return out, (x, nb, gu4, h)


def _ffn_bwd(tag, saved, gain, wgu, wd, dout, doutb):
    x, nb, gu4, h = saved
    t = x.shape[0]
    dh = gmm_nt_out(tag + "_dh", doutb, wd, 0.5)
    dwd = gmm_tn_a(tag + "_dwd", h, doutb, 0.5)
    dgu = swiglu_bwd(tag + "_dact", gu4, dh).reshape(2 * N_SHARDS, t, gu4.shape[-1])
    dwgu = gmm_tn_b(tag + "_dwgu", nb, dgu)
    dn = gmm_nt_red(tag + "_dn", dgu, wgu)
    dx, dxb, dgain = rms_bwd(tag + "_drms", x, gain, dn, dres=dout)
    return dx, dxb, dwgu, dwd, dgain


def _mixer_fwd(tag, x1, sm, w, aux):
    b, s = aux["b"], aux["s"]
    d = x1.shape[1]
    o_gb, o_kr = _O_GA + d, _O_GA + 2 * d
    hb = rms_fwd(tag + "_rms", x1, sm["mix_norm"])
    proj = mm_nn(tag + "_proj", hb, w["win"], tm=512, tn=1664, tk=1024)
    qa_raw = _heads(proj[:, _O_QA:_O_KA], b, s, N_HEADS_A, HEAD_DIM_A)
    ka_raw = _heads(proj[:, _O_KA:_O_VA], b, s, N_KV_HEADS_A, HEAD_DIM_A)
    va = _heads(proj[:, _O_VA:_O_CQ], b, s, N_KV_HEADS_A, HEAD_DIM_A).astype(BF16)
    cq, ckv, kr = proj[:, _O_CQ:_O_CKV], proj[:, _O_CKV:_O_GA], proj[:, o_kr:o_kr + QK_ROPE_DIM]
    cqn = rms_fwd(tag + "_rms_cq", cq, sm["mla_q_lora_norm"], tm=1024)
    ckvn = rms_fwd(tag + "_rms_ckv", ckv, sm["mla_kv_lora_norm"], tm=1024)
    qb_raw = _heads(mm_nn(tag + "_uq", cqn, w["wuq"], tm=1024), b, s, N_HEADS_B, QK_DIM_B)
    kv = mm_nn(tag + "_ukv", ckvn, w["wukv"], tm=1024).reshape(b, s, N_HEADS_B, QK_NOPE_DIM + V_DIM_B).transpose(0, 2, 1, 3)
    vb = kv[..., QK_NOPE_DIM:].astype(BF16)
    kr_b = jnp.broadcast_to(kr.reshape(b, 1, s, QK_ROPE_DIM), (b, N_HEADS_B, s, QK_ROPE_DIM))
    kfull_raw = jnp.concatenate([kv[..., :QK_NOPE_DIM], kr_b], axis=-1)
    qah = qk_prep_fwd(tag + "_qa_norm", qa_raw, sm["swa_q_norm"])
    kah = qk_prep_fwd(tag + "_ka_norm", ka_raw, sm["swa_k_norm"])
    oa = swa_fwd(tag + "_swa", qah, kah, va, aux["pos_col"], aux["pos_row"], aux["slopes"], sm["swa_sinks"].reshape(-1))
    qbh = qk_prep_fwd(tag + "_qb_norm", qb_raw, sm["mla_q_norm"], aux["rope"])
    kbh = qk_prep_fwd(tag + "_kb_norm", kfull_raw, sm["mla_k_norm"], aux["rope"])
    ob, lse = mla_fwd(tag + "_mla", qbh, kbh, vb)
    oab, obb = _unheads(oa).astype(BF16), _unheads(ob).astype(BF16)
    ya = mm_nn(tag + "_branch_a", oab, w["wa"], tm=1024)
    yb = mm_nn(tag + "_branch_b", obb, w["wb"], tm=1024)
    mg = gate_fwd(tag + "_gate", proj, ya, yb, _O_GA, o_gb)
    x2 = mm_nn(tag + "_out", mg, w["wo"], res=x1)
    saved = dict(x1=x1, hb=hb, proj=proj, qa_raw=qa_raw, ka_raw=ka_raw, va=va, cq=cq, ckv=ckv, cqn=cqn, ckvn=ckvn,
                 qb_raw=qb_raw, kfull_raw=kfull_raw, vb=vb, qah=qah, kah=kah, qbh=qbh, kbh=kbh, ob=ob, lse=lse,
                 oab=oab, obb=obb, ya=ya, yb=yb, mg=mg)
    return x2, saved


def _mixer_bwd(tag, sv, sm, w, aux, dx2, dx2b):
    b, s = aux["b"], aux["s"]
    t, d = dx2.shape
    o_gb = _O_GA + d
    dmg = mm_nt(tag + "_d_out", dx2b, w["wo"])
    dwo = mm_tn(tag + "_dw_out", sv["mg"], dx2b)
    dya, dyb, dga, dgb = gate_bwd(tag + "_dgate", sv["proj"], sv["ya"], sv["yb"], dmg, _O_GA, o_gb)
    dwa = mm_tn(tag + "_dw_branch_a", sv["oab"], dya)
    dwb = mm_tn(tag + "_dw_branch_b", sv["obb"], dyb)
    doa = _heads(mm_nt(tag + "_d_branch_a", dya, w["wa"], tm=1024), b, s, N_HEADS_A, HEAD_DIM_A)
    dob = _heads(mm_nt(tag + "_d_branch_b", dyb, w["wb"], tm=1024), b, s, N_HEADS_B, V_DIM_B)
    sinks = sm["swa_sinks"].reshape(-1)
    dqah, dkah, dva, dsinks = swa_bwd(tag + "_dswa", sv["qah"], sv["kah"], sv["va"], aux["pos_col"], aux["pos_row"],
                                      aux["slopes"], sinks, doa)
    dqa_raw, dg_swa_q = qk_prep_bwd(tag + "_dqa_norm", sv["qa_raw"], sm["swa_q_norm"], dqah)
    dka_raw, dg_swa_k = qk_prep_bwd(tag + "_dka_norm", sv["ka_raw"], sm["swa_k_norm"], dkah)
    dqbh = mla_bwd_dq(tag + "_dmla_q", sv["qbh"], sv["kbh"], sv["vb"], sv["ob"], dob, sv["lse"])
    dkbh, dvb = mla_bwd_dkv(tag + "_dmla_kv", sv["qbh"], sv["kbh"], sv["vb"], sv["ob"], dob, sv["lse"])
    dqb_raw, dg_mla_q = qk_prep_bwd(tag + "_dqb_norm", sv["qb_raw"], sm["mla_q_norm"], dqbh, aux["rope"])
    dkfull, dkr_sum, dg_mla_k = qk_prep_bwd(tag + "_dkb_norm", sv["kfull_raw"], sm["mla_k_norm"], dkbh, aux["rope"], head_sum=True)
    dq_tok = _unheads(dqb_raw)
    dkv_tok = _unheads(jnp.concatenate([dkfull[..., :QK_NOPE_DIM], dvb.astype(BF16)], axis=-1))
    dwuq = mm_tn(tag + "_dw_uq", sv["cqn"], dq_tok, tk=1024)
    dwukv = mm_tn(tag + "_dw_ukv", sv["ckvn"], dkv_tok, tk=1024)
    dcqn = mm_nt(tag + "_d_uq", dq_tok, w["wuq"], tm=1024)
    dckvn = mm_nt(tag + "_d_ukv", dkv_tok, w["wukv"], tm=1024)
    dcq, dg_q_lora = rms_bwd(tag + "_drms_cq", sv["cq"], sm["mla_q_lora_norm"], dcqn, want_f32=False, tm=1024)
    dckv, dg_kv_lora = rms_bwd(tag + "_drms_ckv", sv["ckv"], sm["mla_kv_lora_norm"], dckvn, want_f32=False, tm=1024)
    dkr = dkr_sum[..., QK_NOPE_DIM:].reshape(t, QK_ROPE_DIM).astype(BF16)
    parts = [_unheads(dqa_raw), _unheads(dka_raw), _unheads(dva.astype(BF16)), dcq, dckv, dga, dgb, dkr]
    pad = w["win"].shape[1] - sum(p.shape[1] for p in parts)
    if pad:
        parts.append(jnp.zeros((t, pad), BF16))
    dproj = jnp.concatenate(parts, axis=1)
    dwin = mm_tn(tag + "_dw_in", sv["hb"], dproj, tm=1024, tn=1664, tk=512)
    dh = mm_nt(tag + "_d_in", dproj, w["win"], tm=512, tn=1024, tk=1664)
    dx1, dx1b, dg_mix = rms_bwd(tag + "_drms", sv["x1"], sm["mix_norm"], dh, dres=dx2)
    wgrads = dict(win=dwin, wuq=dwuq, wukv=dwukv, wa=dwa, wb=dwb, wo=dwo)
    sgrads = dict(mix_norm=dg_mix, swa_q_norm=dg_swa_q, swa_k_norm=dg_swa_k, swa_sinks=dsinks[:, 0].reshape(1, -1),
                  mla_q_lora_norm=dg_q_lora, mla_kv_lora_norm=dg_kv_lora, mla_q_norm=dg_mla_q, mla_k_norm=dg_mla_k)
    return dx1, dx1b, wgrads, sgrads


def _layer_weights(gathered_l, offs, shard_shapes, d):
    def pieces(name):
        return [_from_rows(gathered_l[s], offs[name], shard_shapes[name]) for s in range(N_SHARDS)]

    def cols(name):
        return jnp.concatenate(pieces(name), axis=1)

    win_ref = cols("w_in")
    width = win_ref.shape[1]
    parts = [win_ref[:, :_O_GA], win_ref[:, _O_GA + QK_ROPE_DIM:], win_ref[:, _O_GA:_O_GA + QK_ROPE_DIM]]
    padded = -(-width // (2 * LANES)) * (2 * LANES)
    if padded > width:
        parts.append(jnp.zeros((d, padded - width), BF16))
    return dict(
        wgu1=jnp.stack(pieces("ffn1_w_gate") + pieces("ffn1_w_up")), wd1=jnp.stack(pieces("ffn1_w_down")),
        wgu2=jnp.stack(pieces("ffn2_w_gate") + pieces("ffn2_w_up")), wd2=jnp.stack(pieces("ffn2_w_down")),
        win=jnp.concatenate(parts, axis=1), wuq=cols("mla_w_uq"), wukv=cols("mla_w_ukv"),
        wa=cols("w_branch_a"), wb=cols("w_branch_b"), wo=jnp.concatenate(pieces("w_out"), axis=0))


def _grad_pieces(g, s, shard_shapes, d):
    def col(a, name):
        c = shard_shapes[name][1]
        return a[:, s * c:(s + 1) * c]

    win = g["win"]
    width = N_SHARDS * shard_shapes["w_in"][1]
    win_ref = jnp.concatenate([win[:, :_O_GA], win[:, width - QK_ROPE_DIM:width], win[:, _O_GA:width - QK_ROPE_DIM]], axis=1)
    r_out = shard_shapes["w_out"][0]
    return dict(
        ffn1_w_gate=g["wgu1"][s], ffn1_w_up=g["wgu1"][N_SHARDS + s], ffn1_w_down=g["wd1"][s],
        ffn2_w_gate=g["wgu2"][s], ffn2_w_up=g["wgu2"][N_SHARDS + s], ffn2_w_down=g["wd2"][s],
        w_in=col(win_ref, "w_in"), mla_w_uq=col(g["wuq"], "mla_w_uq"), mla_w_ukv=col(g["wukv"], "mla_w_ukv"),
        w_branch_a=col(g["wa"], "w_branch_a"), w_branch_b=col(g["wb"], "w_branch_b"),
        w_out=g["wo"][s * r_out:(s + 1) * r_out])


def _local_step(x, positions, target, weights, small, depth):
    b, s, d = x.shape
    t = b * s
    posf = positions.astype(F32)
    half = QK_ROPE_DIM // 2
    inv_freq = ROPE_BASE ** (-jnp.arange(half, dtype=F32) / half)
    ang = posf[..., None] * inv_freq
    cos, sin = jnp.cos(ang), jnp.sin(ang)
    rope = (jnp.concatenate([jnp.ones((b, s, QK_NOPE_DIM), F32), cos, cos], axis=-1),
            jnp.concatenate([jnp.zeros((b, s, QK_NOPE_DIM), F32), sin, sin], axis=-1))
    slopes = jnp.exp2(-8.0 * (jnp.arange(N_HEADS_A, dtype=F32) + 1.0) / N_HEADS_A)
    aux = dict(b=b, s=s, pos_col=posf.reshape(b, s, 1), pos_row=posf.reshape(b, 1, s), rope=rope, slopes=slopes)

    def sm_of(l):
        return {n: small[n][l:l + 1] for n in SMALL_NAMES}

    h = x.reshape(t, d)
    saved = []
    for l in range(depth):
        sm, w = sm_of(l), weights[l]
        h, s1 = _ffn_fwd(f"l{l}_ffn1", h, sm["ffn1_norm"], w["wgu1"], w["wd1"])
        h, s2 = _mixer_fwd(f"l{l}_mix", h, sm, w, aux)
        h, s3 = _ffn_fwd(f"l{l}_ffn2", h, sm["ffn2_norm"], w["wgu2"], w["wd2"])
        saved.append((s1, s2, s3))
    loss, dh, dhb = loss_fwd_bwd("loss", h, target.reshape(t, d))
    wgrads, sgrads = [None] * depth, [None] * depth
    for l in reversed(range(depth)):
        sm, w = sm_of(l), weights[l]
        s1, s2, s3 = saved[l]
        dh, dhb, dwgu2, dwd2, dg_f2 = _ffn_bwd(f"l{l}_ffn2", s3, sm["ffn2_norm"], w["wgu2"], w["wd2"], dh, dhb)
        dh, dhb, wg, sg = _mixer_bwd(f"l{l}_mix", s2, sm, w, aux, dh, dhb)
        dh, dhb, dwgu1, dwd1, dg_f1 = _ffn_bwd(f"l{l}_ffn1", s1, sm["ffn1_norm"], w["wgu1"], w["wd1"], dh, dhb)
        wg.update(wgu1=dwgu1, wd1=dwd1, wgu2=dwgu2, wd2=dwd2)
        sg.update(ffn1_norm=dg_f1, ffn2_norm=dg_f2)
        wgrads[l], sgrads[l] = wg, sg
    return loss, dh.reshape(b, s, d), wgrads, sgrads


def kernel(x, positions, ffn1_norm, ffn1_w_gate, ffn1_w_up, ffn1_w_down, mix_norm, w_in, swa_q_norm, swa_k_norm, swa_sinks, mla_q_lora_norm, mla_w_uq, mla_kv_lora_norm, mla_w_ukv, mla_q_norm, mla_k_norm, w_branch_a, w_branch_b, w_out, ffn2_norm, ffn2_w_gate, ffn2_w_up, ffn2_w_down, loss_target, m_ffn1_norm, m_ffn1_w_gate, m_ffn1_w_up, m_ffn1_w_down, m_mix_norm, m_w_in, m_swa_q_norm, m_swa_k_norm, m_swa_sinks, m_mla_q_lora_norm, m_mla_w_uq, m_mla_kv_lora_norm, m_mla_w_ukv, m_mla_q_norm, m_mla_k_norm, m_w_branch_a, m_w_branch_b, m_w_out, m_ffn2_norm, m_ffn2_w_gate, m_ffn2_w_up, m_ffn2_w_down, v_ffn1_norm, v_ffn1_w_gate, v_ffn1_w_up, v_ffn1_w_down, v_mix_norm, v_w_in, v_swa_q_norm, v_swa_k_norm, v_swa_sinks, v_mla_q_lora_norm, v_mla_w_uq, v_mla_kv_lora_norm, v_mla_w_ukv, v_mla_q_norm, v_mla_k_norm, v_w_branch_a, v_w_branch_b, v_w_out, v_ffn2_norm, v_ffn2_w_gate, v_ffn2_w_up, v_ffn2_w_down):
    args = (x, positions, ffn1_norm, ffn1_w_gate, ffn1_w_up, ffn1_w_down, mix_norm, w_in, swa_q_norm, swa_k_norm, swa_sinks, mla_q_lora_norm, mla_w_uq, mla_kv_lora_norm, mla_w_ukv, mla_q_norm, mla_k_norm, w_branch_a, w_branch_b, w_out, ffn2_norm, ffn2_w_gate, ffn2_w_up, ffn2_w_down, loss_target, m_ffn1_norm, m_ffn1_w_gate, m_ffn1_w_up, m_ffn1_w_down, m_mix_norm, m_w_in, m_swa_q_norm, m_swa_k_norm, m_swa_sinks, m_mla_q_lora_norm, m_mla_w_uq, m_mla_kv_lora_norm, m_mla_w_ukv, m_mla_q_norm, m_mla_k_norm, m_w_branch_a, m_w_branch_b, m_w_out, m_ffn2_norm, m_ffn2_w_gate, m_ffn2_w_up, m_ffn2_w_down, v_ffn1_norm, v_ffn1_w_gate, v_ffn1_w_up, v_ffn1_w_down, v_mix_norm, v_w_in, v_swa_q_norm, v_swa_k_norm, v_swa_sinks, v_mla_q_lora_norm, v_mla_w_uq, v_mla_kv_lora_norm, v_mla_w_ukv, v_mla_q_norm, v_mla_k_norm, v_w_branch_a, v_w_branch_b, v_w_out, v_ffn2_norm, v_ffn2_w_gate, v_ffn2_w_up, v_ffn2_w_down)
    a = dict(zip(ARG_NAMES, args, strict=True))
    x = a["x"]
    depth = a["ffn1_norm"].shape[0]
    d = x.shape[-1]
    shard_shapes = {n: a[n].shape[1:] for n in PACK_NAMES}
    offs, slab_rows = _pack_layout(shard_shapes)
    assert depth == 2, "one layer slab per core of a chip"

    pack = jnp.stack([_pack_slab({n: a[n][l] for n in PACK_NAMES}, slab_rows, BF16) for l in range(depth)])
    gathered = allgather_weights(pack)
    weights = [_layer_weights(gathered[l], offs, shard_shapes, d) for l in range(depth)]
    small = {n: a[n] for n in SMALL_NAMES}

    loss, grad_x, wgrads, sgrads = _local_step(x, a["positions"], a["loss_target"], weights, small, depth)

    gpack = jnp.stack([jnp.stack([_pack_slab(_grad_pieces(wgrads[l], s, shard_shapes, d), slab_rows, F32)
                                  for s in range(N_SHARDS)]) for l in range(depth)])
    c_idx = lax.axis_index("c").astype(jnp.int32).reshape(1)
    theirs = sibling_swap(gpack)
    chip_part = add_own_slab(gpack.reshape(depth, N_SHARDS * slab_rows, PACK_COLS),
                             theirs.reshape(N_SHARDS * slab_rows, PACK_COLS), c_idx)
    slots = scatter_to_owners(chip_part.reshape(N_SHARDS, slab_rows, PACK_COLS))
    gfull = sibling_share(sum_slots(slots))
    grads = {n: jnp.stack([_from_rows(gfull[l], offs[n], shard_shapes[n]) for l in range(depth)]) for n in PACK_NAMES}

    flat = jnp.concatenate([jnp.concatenate([sgrads[l][n].reshape(-1) for l in range(depth)]) for n in SMALL_NAMES] + [loss.reshape(-1)])
    n_small = flat.shape[0]
    rows = -(-n_small // (8 * LANES)) * 8
    pad = rows * LANES - n_small

    def small_pack(v):
        return jnp.pad(v, (0, pad)).reshape(rows, LANES)

    total = small_allreduce(small_pack(flat))
    w_s, m_s, v_s = (small_pack(jnp.concatenate([a[p + n].reshape(-1) for n in SMALL_NAMES] + [jnp.zeros((1,), F32)]))
                     for p in ("", "m_", "v_"))
    d_s, nm_s, nv_s = adamw("adamw_small", w_s, total, m_s, v_s)

    def small_unpack(buf):
        out, off, flat_b = {}, 0, buf.reshape(-1)
        for n in SMALL_NAMES:
            size = a[n].shape[0] * a[n].shape[1]
            out[n] = flat_b[off:off + size].reshape(a[n].shape)
            off += size
        return out

    grads.update(small_unpack(total))
    delta, new_m, new_v = small_unpack(d_s), small_unpack(nm_s), small_unpack(nv_s)
    for n in PACK_NAMES:
        shp = a[n].shape
        two_d = (shp[0] * shp[1], shp[2])
        dn, mn, vn = adamw("adamw_" + n, a[n].reshape(two_d), grads[n].reshape(two_d), a["m_" + n].reshape(two_d), a["v_" + n].reshape(two_d))
        delta[n], new_m[n], new_v[n] = dn.reshape(shp), mn.reshape(shp), vn.reshape(shp)

    loss_out = total.reshape(-1)[n_small - 1]
    return (loss_out, grad_x, *[grads[n] for n in WEIGHT_NAMES], *[delta[n] for n in WEIGHT_NAMES],
            *[new_m[n] for n in WEIGHT_NAMES], *[new_v[n] for n in WEIGHT_NAMES])
```

```python
import functools

import numpy as np
import jax
import jax.numpy as jnp
from jax import lax
from jax.experimental import pallas as pl
from jax.experimental.pallas import tpu as pltpu

F32 = jnp.float32
BF16 = jnp.bfloat16
MESH = pl.DeviceIdType.MESH

HEAD_DIM_A = 64
N_HEADS_A = 8
N_KV_HEADS_A = 2
GROUP_A = N_HEADS_A // N_KV_HEADS_A
BLOCK = 128
N_HEADS_B = 8
Q_LORA_RANK = 256
KV_LORA_RANK = 128
QK_NOPE_DIM = 64
QK_ROPE_DIM = 32
QK_DIM_B = QK_NOPE_DIM + QK_ROPE_DIM
V_DIM_B = 64
ROPE_BASE = 10000.0
WIDTH_A = N_HEADS_A * HEAD_DIM_A
WIDTH_B = N_HEADS_B * V_DIM_B
KV_WIDTH_A = N_KV_HEADS_A * HEAD_DIM_A
EPS = 1e-6
NEG = -1e30
ADAM_LR = 0.001
ADAM_B1 = 0.9
ADAM_B2 = 0.999
ADAM_EPS = 1e-08
ADAM_WD = 0.01
ADAM_STEP = 10

N_SHARDS = 4
N_DEV = 8
LANES = 128
PACK_COLS = 1024
PACK_ROW_ALIGN = 16
VMEM_LIMIT = 48 * 1024 * 1024

PACK_NAMES = ("ffn1_w_gate", "ffn1_w_up", "ffn1_w_down", "w_in", "mla_w_uq", "mla_w_ukv",
              "w_branch_a", "w_branch_b", "w_out", "ffn2_w_gate", "ffn2_w_up", "ffn2_w_down")
SMALL_NAMES = ("ffn1_norm", "mix_norm", "swa_q_norm", "swa_k_norm", "swa_sinks", "mla_q_lora_norm",
               "mla_kv_lora_norm", "mla_q_norm", "mla_k_norm", "ffn2_norm")
WEIGHT_NAMES = ("ffn1_norm", "ffn1_w_gate", "ffn1_w_up", "ffn1_w_down", "mix_norm", "w_in", "swa_q_norm",
                "swa_k_norm", "swa_sinks", "mla_q_lora_norm", "mla_w_uq", "mla_kv_lora_norm", "mla_w_ukv",
                "mla_q_norm", "mla_k_norm", "w_branch_a", "w_branch_b", "w_out", "ffn2_norm", "ffn2_w_gate",
                "ffn2_w_up", "ffn2_w_down")


def _params(sem):
    return pltpu.CompilerParams(dimension_semantics=sem, vmem_limit_bytes=VMEM_LIMIT)


def _tile(n, want, align):
    if n <= want:
        return n
    t = (want // align) * align
    while t > align and n % t:
        t -= align
    assert t >= align and n % t == 0, (n, want, align)
    return t


def _mm_call(name, a, b, out_struct, grid, a_spec, b_spec, o_spec, dims, n_red, acc_shape, alpha=1.0, res=None):
    n_par = len(grid) - n_red

    def body(*refs):
        if res is None:
            a_ref, b_ref, o_ref, acc_ref = refs
            r_ref = None
        else:
            a_ref, b_ref, r_ref, o_ref, acc_ref = refs
        part = lax.dot_general(a_ref[...], b_ref[...], dims, preferred_element_type=F32)

        def finish(total):
            if alpha != 1.0:
                total = total * alpha
            if r_ref is not None:
                total = r_ref[...] + total
            o_ref[...] = total.astype(o_ref.dtype)

        if n_red == 0:
            finish(part)
            return
        ids = [pl.program_id(n_par + i) for i in range(n_red)]
        first = functools.reduce(jnp.logical_and, [i == 0 for i in ids])
        last = functools.reduce(jnp.logical_and, [i == grid[n_par + k] - 1 for k, i in enumerate(ids)])

        @pl.when(first)
        def _():
            acc_ref[...] = part

        @pl.when(jnp.logical_not(first))
        def _():
            acc_ref[...] += part

        @pl.when(last)
        def _():
            finish(acc_ref[...])

    in_specs = [a_spec, b_spec] + ([o_spec] if res is not None else [])
    args = (a, b) + ((res,) if res is not None else ())
    return pl.pallas_call(
        body, name=name, out_shape=out_struct, grid=grid, in_specs=in_specs, out_specs=o_spec,
        scratch_shapes=[pltpu.VMEM(acc_shape, F32)],
        compiler_params=_params(("parallel",) * n_par + ("arbitrary",) * n_red),
    )(*args)


_NN = (((1,), (0,)), ((), ()))
_NT = (((1,), (1,)), ((), ()))
_TN = (((0,), (0,)), ((), ()))


def mm_nn(name, a, b, tm=512, tn=1024, tk=1024, out_dtype=F32, alpha=1.0, res=None):
    (m, k), (_, n) = a.shape, b.shape
    tm, tn, tk = _tile(m, tm, 16), _tile(n, tn, LANES), _tile(k, tk, LANES)
    return _mm_call(name, a, b, jax.ShapeDtypeStruct((m, n), out_dtype), (m // tm, n // tn, k // tk),
                    pl.BlockSpec((tm, tk), lambda i, j, kk: (i, kk)), pl.BlockSpec((tk, tn), lambda i, j, kk: (kk, j)),
                    pl.BlockSpec((tm, tn), lambda i, j, kk: (i, j)), _NN, 1, (tm, tn), alpha, res)


def mm_nt(name, a, b, tm=512, tn=1024, tk=1024, out_dtype=F32, alpha=1.0):
    (m, n), (k, _) = a.shape, b.shape
    tm, tn, tk = _tile(m, tm, 16), _tile(k, tn, LANES), _tile(n, tk, LANES)
    return _mm_call(name, a, b, jax.ShapeDtypeStruct((m, k), out_dtype), (m // tm, k // tn, n // tk),
                    pl.BlockSpec((tm, tk), lambda i, j, kk: (i, kk)), pl.BlockSpec((tn, tk), lambda i, j, kk: (j, kk)),
                    pl.BlockSpec((tm, tn), lambda i, j, kk: (i, j)), _NT, 1, (tm, tn), alpha)


def mm_tn(name, a, b, tm=1024, tn=1024, tk=512, out_dtype=F32, alpha=1.0):
    (m, k), (_, n) = a.shape, b.shape
    tm, tn, tk = _tile(k, tm, LANES), _tile(n, tn, LANES), _tile(m, tk, 16)
    return _mm_call(name, a, b, jax.ShapeDtypeStruct((k, n), out_dtype), (k // tm, n // tn, m // tk),
                    pl.BlockSpec((tk, tm), lambda i, j, kk: (kk, i)), pl.BlockSpec((tk, tn), lambda i, j, kk: (kk, j)),
                    pl.BlockSpec((tm, tn), lambda i, j, kk: (i, j)), _TN, 1, (tm, tn), alpha)


def gmm_up(name, a, w, tm=512):
    (m, k), (g, _, n) = a.shape, w.shape
    tm = _tile(m, tm, 16)
    return _mm_call(name, a, w, jax.ShapeDtypeStruct((g, m, n), F32), (g, m // tm),
                    pl.BlockSpec((tm, k), lambda gi, i: (i, 0)), pl.BlockSpec((None, k, n), lambda gi, i: (gi, 0, 0)),
                    pl.BlockSpec((None, tm, n), lambda gi, i: (gi, i, 0)), _NN, 0, (8, LANES))


def gmm_down(name, h, w, res, alpha, tm=512):
    (g, m, n), (_, _, d) = h.shape, w.shape
    tm = _tile(m, tm, 16)
    return _mm_call(name, h, w, jax.ShapeDtypeStruct((m, d), F32), (m // tm, g),
                    pl.BlockSpec((None, tm, n), lambda i, gi: (gi, i, 0)), pl.BlockSpec((None, n, d), lambda i, gi: (gi, 0, 0)),
                    pl.BlockSpec((tm, d), lambda i, gi: (i, 0)), _NN, 1, (tm, d), alpha, res)


def gmm_nt_out(name, a, w, alpha, tm=512):
    (m, d), (g, n, _) = a.shape, w.shape
    tm = _tile(m, tm, 16)
    return _mm_call(name, a, w, jax.ShapeDtypeStruct((g, m, n), F32), (g, m // tm),
                    pl.BlockSpec((tm, d), lambda gi, i: (i, 0)), pl.BlockSpec((None, n, d), lambda gi, i: (gi, 0, 0)),
                    pl.BlockSpec((None, tm, n), lambda gi, i: (gi, i, 0)), _NT, 0, (8, LANES), alpha)


def gmm_nt_red(name, a, w, tm=512):
    (g, m, n), (_, k, _) = a.shape, w.shape
    tm = _tile(m, tm, 16)
    return _mm_call(name, a, w, jax.ShapeDtypeStruct((m, k), F32), (m // tm, g),
                    pl.BlockSpec((None, tm, n), lambda i, gi: (gi, i, 0)), pl.BlockSpec((None, k, n), lambda i, gi: (gi, 0, 0)),
                    pl.BlockSpec((tm, k), lambda i, gi: (i, 0)), _NT, 1, (tm, k))


def gmm_tn_a(name, h, b, alpha, tk=512):
    (g, m, n), (_, d) = h.shape, b.shape
    tk = _tile(m, tk, 16)
    return _mm_call(name, h, b, jax.ShapeDtypeStruct((g, n, d), F32), (g, m // tk),
                    pl.BlockSpec((None, tk, n), lambda gi, kk: (gi, kk, 0)), pl.BlockSpec((tk, d), lambda gi, kk: (kk, 0)),
                    pl.BlockSpec((None, n, d), lambda gi, kk: (gi, 0, 0)), _TN, 1, (n, d), alpha)


def gmm_tn_b(name, a, b, tk=512):
    (m, k), (g, _, n) = a.shape, b.shape
    tk = _tile(m, tk, 16)
    return _mm_call(name, a, b, jax.ShapeDtypeStruct((g, k, n), F32), (g, m // tk),
                    pl.BlockSpec((tk, k), lambda gi, kk: (kk, 0)), pl.BlockSpec((None, tk, n), lambda gi, kk: (gi, kk, 0)),
                    pl.BlockSpec((None, k, n), lambda gi, kk: (gi, 0, 0)), _TN, 1, (k, n))


def rms_fwd(name, x, gain, tm=512):
    m, d = x.shape
    tm = _tile(m, tm, 16)

    def body(x_ref, g_ref, o_ref):
        xv = x_ref[...]
        r = lax.rsqrt(jnp.mean(xv * xv, axis=-1, keepdims=True) + EPS)
        o_ref[...] = (xv * r * g_ref[...]).astype(o_ref.dtype)

    return pl.pallas_call(
        body, name=name, out_shape=jax.ShapeDtypeStruct((m, d), BF16), grid=(m // tm,),
        in_specs=[pl.BlockSpec((tm, d), lambda i: (i, 0)), pl.BlockSpec((1, d), lambda i: (0, 0))],
        out_specs=pl.BlockSpec((tm, d), lambda i: (i, 0)), compiler_params=_params(("parallel",)),
    )(x, gain)


def rms_bwd(name, x, gain, dn, dres=None, want_f32=True, want_bf16=True, tm=512):
    m, d = x.shape
    tm = _tile(m, tm, 16)
    n_out = int(want_f32) + int(want_bf16)

    def body(*refs):
        x_ref, g_ref, dn_ref = refs[:3]
        pos = 3
        r_ref = None
        if dres is not None:
            r_ref = refs[pos]
            pos += 1
        outs = refs[pos:pos + n_out]
        dg_ref = refs[pos + n_out]
        xv = x_ref[...]
        r = lax.rsqrt(jnp.mean(xv * xv, axis=-1, keepdims=True) + EPS)
        xhat = xv * r
        dnv = dn_ref[...]
        dxhat = dnv * g_ref[...]
        dx = r * (dxhat - xhat * jnp.mean(dxhat * xhat, axis=-1, keepdims=True))
        if r_ref is not None:
            dx = r_ref[...] + dx
        for o in outs:
            o[...] = dx.astype(o.dtype)
        part = jnp.sum(dnv * xhat, axis=0, keepdims=True)

        @pl.when(pl.program_id(0) == 0)
        def _():
            dg_ref[...] = part

        @pl.when(pl.program_id(0) > 0)
        def _():
            dg_ref[...] += part

    row = pl.BlockSpec((tm, d), lambda i: (i, 0))
    vec = pl.BlockSpec((1, d), lambda i: (0, 0))
    out_shape = ([jax.ShapeDtypeStruct((m, d), F32)] if want_f32 else []) + ([jax.ShapeDtypeStruct((m, d), BF16)] if want_bf16 else [])
    res = pl.pallas_call(
        body, name=name, out_shape=tuple(out_shape) + (jax.ShapeDtypeStruct((1, d), F32),), grid=(m // tm,),
        in_specs=[row, vec, row] + ([row] if dres is not None else []),
        out_specs=tuple([row] * n_out) + (vec,), compiler_params=_params(("arbitrary",)),
    )(*((x, gain, dn) + ((dres,) if dres is not None else ())))
    return res


def swiglu_fwd(name, gu, tm=512):
    _, g, m, n = gu.shape
    tm = _tile(m, tm, 16)

    def body(gu_ref, o_ref):
        gate, up = gu_ref[0], gu_ref[1]
        o_ref[...] = (gate * jax.nn.sigmoid(gate) * up).astype(o_ref.dtype)

    return pl.pallas_call(
        body, name=name, out_shape=jax.ShapeDtypeStruct((g, m, n), BF16), grid=(g, m // tm),
        in_specs=[pl.BlockSpec((2, None, tm, n), lambda gi, i: (0, gi, i, 0))],
        out_specs=pl.BlockSpec((None, tm, n), lambda gi, i: (gi, i, 0)), compiler_params=_params(("parallel", "parallel")),
    )(gu)


def swiglu_bwd(name, gu, dh, tm=512):
    _, g, m, n = gu.shape
    tm = _tile(m, tm, 16)

    def body(gu_ref, dh_ref, o_ref):
        gate, up, dhv = gu_ref[0], gu_ref[1], dh_ref[...]
        s = jax.nn.sigmoid(gate)
        o_ref[0] = (dhv * up * (s * (1.0 + gate * (1.0 - s)))).astype(o_ref.dtype)
        o_ref[1] = (dhv * (gate * s)).astype(o_ref.dtype)

    return pl.pallas_call(
        body, name=name, out_shape=jax.ShapeDtypeStruct((2, g, m, n), BF16), grid=(g, m // tm),
        in_specs=[pl.BlockSpec((2, None, tm, n), lambda gi, i: (0, gi, i, 0)), pl.BlockSpec((None, tm, n), lambda gi, i: (gi, i, 0))],
        out_specs=pl.BlockSpec((2, None, tm, n), lambda gi, i: (0, gi, i, 0)), compiler_params=_params(("parallel", "parallel")),
    )(gu, dh)


def gate_fwd(name, proj, ya, yb, off_a, off_b, tm=256):
    m, d = ya.shape
    w = proj.shape[1]
    tm = _tile(m, tm, 16)

    def body(p_ref, ya_ref, yb_ref, o_ref):
        ga, gb = p_ref[:, off_a:off_a + d], p_ref[:, off_b:off_b + d]
        o_ref[...] = (jax.nn.sigmoid(ga) * ya_ref[...] + jax.nn.sigmoid(gb) * yb_ref[...]).astype(o_ref.dtype)

    row = pl.BlockSpec((tm, d), lambda i: (i, 0))
    return pl.pallas_call(
        body, name=name, out_shape=jax.ShapeDtypeStruct((m, d), BF16), grid=(m // tm,),
        in_specs=[pl.BlockSpec((tm, w), lambda i: (i, 0)), row, row], out_specs=row, compiler_params=_params(("parallel",)),
    )(proj, ya, yb)


def gate_bwd(name, proj, ya, yb, dmerged, off_a, off_b, tm=256):
    m, d = ya.shape
    w = proj.shape[1]
    tm = _tile(m, tm, 16)

    def body(p_ref, ya_ref, yb_ref, dm_ref, dya_ref, dyb_ref, dga_ref, dgb_ref):
        sa, sb = jax.nn.sigmoid(p_ref[:, off_a:off_a + d]), jax.nn.sigmoid(p_ref[:, off_b:off_b + d])
        dm = dm_ref[...]
        dya_ref[...] = (dm * sa).astype(BF16)
        dyb_ref[...] = (dm * sb).astype(BF16)
        dga_ref[...] = (dm * ya_ref[...] * (sa * (1.0 - sa))).astype(BF16)
        dgb_ref[...] = (dm * yb_ref[...] * (sb * (1.0 - sb))).astype(BF16)

    row = pl.BlockSpec((tm, d), lambda i: (i, 0))
    o = jax.ShapeDtypeStruct((m, d), BF16)
    return pl.pallas_call(
        body, name=name, out_shape=(o, o, o, o), grid=(m // tm,),
        in_specs=[pl.BlockSpec((tm, w), lambda i: (i, 0)), row, row, row], out_specs=(row, row, row, row),
        compiler_params=_params(("parallel",)),
    )(proj, ya, yb, dmerged)


def loss_fwd_bwd(name, y, target, tm=512):
    m, d = y.shape
    tm = _tile(m, tm, 16)

    def body(y_ref, t_ref, l_ref, dy_ref, dyb_ref):
        err = y_ref[...] - t_ref[...]
        dy = err * (1.0 / d)
        dy_ref[...] = dy
        dyb_ref[...] = dy.astype(BF16)
        part = 0.5 * jnp.sum(jnp.mean(err * err, axis=-1, keepdims=True), axis=0, keepdims=True)

        @pl.when(pl.program_id(0) == 0)
        def _():
            l_ref[...] = part

        @pl.when(pl.program_id(0) > 0)
        def _():
            l_ref[...] += part

    row = pl.BlockSpec((tm, d), lambda i: (i, 0))
    return pl.pallas_call(
        body, name=name, grid=(m // tm,),
        out_shape=(jax.ShapeDtypeStruct((1, 1), F32), jax.ShapeDtypeStruct((m, d), F32), jax.ShapeDtypeStruct((m, d), BF16)),
        in_specs=[row, row], out_specs=(pl.BlockSpec((1, 1), lambda i: (0, 0)), row, row),
        compiler_params=_params(("arbitrary",)),
    )(y, target)


def adamw(name, w, g, m, v):
    r, c = w.shape
    tr = _tile(r, max(8, (2 * 1024 * 1024) // (4 * c) // 8 * 8), 8)
    c1 = np.float32(1.0 - ADAM_B1 ** ADAM_STEP)
    c2 = np.float32(1.0 - ADAM_B2 ** ADAM_STEP)

    def body(w_ref, g_ref, m_ref, v_ref, d_ref, nm_ref, nv_ref):
        gv = g_ref[...]
        nm = ADAM_B1 * m_ref[...] + (1.0 - ADAM_B1) * gv
        nv = ADAM_B2 * v_ref[...] + (1.0 - ADAM_B2) * (gv * gv)
        d_ref[...] = -ADAM_LR * ((nm / c1) / (jnp.sqrt(nv / c2) + ADAM_EPS) + ADAM_WD * w_ref[...])
        nm_ref[...] = nm
        nv_ref[...] = nv

    row = pl.BlockSpec((tr, c), lambda i: (i, 0))
    o = jax.ShapeDtypeStruct((r, c), F32)
    return pl.pallas_call(
        body, name=name, out_shape=(o, o, o), grid=(r // tr,), in_specs=[row] * 4, out_specs=(row, row, row),
        compiler_params=_params(("parallel",)),
    )(w, g, m, v)


def _rope_rot(d):
    r = np.zeros((d, d), np.float32)
    half = QK_ROPE_DIM // 2
    base = d - QK_ROPE_DIM
    for j in range(half):
        r[base + half + j, base + j] = -1.0
        r[base + j, base + half + j] = 1.0
    return r


def qk_prep_fwd(name, x, gain, rope=None):
    b, h, s, d = x.shape

    def body(*refs):
        if rope is None:
            x_ref, g_ref, o_ref = refs
        else:
            x_ref, g_ref, c_ref, s_ref, r_ref, o_ref = refs
        xv = x_ref[...]
        y = xv * lax.rsqrt(jnp.mean(xv * xv, axis=-1, keepdims=True) + EPS) * g_ref[...]
        if rope is not None:
            rot = jnp.dot(y, r_ref[...], precision=lax.Precision.HIGHEST, preferred_element_type=F32)
            y = y * c_ref[...] + rot * s_ref[...]
        o_ref[...] = y.astype(o_ref.dtype)

    xs = pl.BlockSpec((None, None, s, d), lambda bi, hi: (bi, hi, 0, 0))
    in_specs = [xs, pl.BlockSpec((1, d), lambda bi, hi: (0, 0))]
    args = [x, gain]
    if rope is not None:
        tab = pl.BlockSpec((None, s, d), lambda bi, hi: (bi, 0, 0))
        in_specs += [tab, tab, pl.BlockSpec((d, d), lambda bi, hi: (0, 0))]
        args += [rope[0], rope[1], jnp.asarray(_rope_rot(d))]
    return pl.pallas_call(
        body, name=name, out_shape=jax.ShapeDtypeStruct(x.shape, BF16), grid=(b, h), in_specs=in_specs, out_specs=xs,
        compiler_params=_params(("parallel", "parallel")),
    )(*args)


def qk_prep_bwd(name, x, gain, dxh, rope=None, head_sum=False):
    b, h, s, d = x.shape

    def body(*refs):
        if rope is None:
            x_ref, g_ref, dy_ref = refs[:3]
            outs = refs[3:]
        else:
            x_ref, g_ref, dy_ref, c_ref, s_ref, rt_ref = refs[:6]
            outs = refs[6:]
        dx_ref, dg_ref = outs[0], outs[-1]
        dy = dy_ref[...]
        if rope is not None:
            dy = dy * c_ref[...] + jnp.dot(dy * s_ref[...], rt_ref[...], precision=lax.Precision.HIGHEST, preferred_element_type=F32)
        xv = x_ref[...]
        r = lax.rsqrt(jnp.mean(xv * xv, axis=-1, keepdims=True) + EPS)
        xhat = xv * r
        dxhat = dy * g_ref[...]
        dx = r * (dxhat - xhat * jnp.mean(dxhat * xhat, axis=-1, keepdims=True))
        dx_ref[...] = dx.astype(dx_ref.dtype)
        part = jnp.sum(dy * xhat, axis=0, keepdims=True)
        first = jnp.logical_and(pl.program_id(0) == 0, pl.program_id(1) == 0)

        @pl.when(first)
        def _():
            dg_ref[...] = part

        @pl.when(jnp.logical_not(first))
        def _():
            dg_ref[...] += part

        if head_sum:
            hs_ref = outs[1]

            @pl.when(pl.program_id(1) == 0)
            def _():
                hs_ref[...] = dx

            @pl.when(pl.program_id(1) > 0)
            def _():
                hs_ref[...] += dx

    xs = pl.BlockSpec((None, None, s, d), lambda bi, hi: (bi, hi, 0, 0))
    vec = pl.BlockSpec((1, d), lambda bi, hi: (0, 0))
    tab = pl.BlockSpec((None, s, d), lambda bi, hi: (bi, 0, 0))
    in_specs = [xs, vec, xs]
    args = [x, gain, dxh]
    if rope is not None:
        in_specs += [tab, tab, pl.BlockSpec((d, d), lambda bi, hi: (0, 0))]
        args += [rope[0], rope[1], jnp.asarray(_rope_rot(d).T.copy())]
    out_shape = [jax.ShapeDtypeStruct(x.shape, BF16)]
    out_specs = [xs]
    if head_sum:
        out_shape.append(jax.ShapeDtypeStruct((b, s, d), F32))
        out_specs.append(tab)
    out_shape.append(jax.ShapeDtypeStruct((1, d), F32))
    out_specs.append(vec)
    return pl.pallas_call(
        body, name=name, out_shape=tuple(out_shape), grid=(b, h), in_specs=in_specs, out_specs=tuple(out_specs),
        compiler_params=_params(("arbitrary", "arbitrary")),
    )(*args)


def _swa_specs(s):
    nb = s // BLOCK
    q = pl.BlockSpec((None, GROUP_A, BLOCK, HEAD_DIM_A), lambda b, kv, n: (b, kv, n, 0))
    kprev = pl.BlockSpec((None, None, BLOCK, HEAD_DIM_A), lambda b, kv, n: (b, kv, jnp.maximum(n - 1, 0), 0))
    kcur = pl.BlockSpec((None, None, BLOCK, HEAD_DIM_A), lambda b, kv, n: (b, kv, n, 0))
    pcol = pl.BlockSpec((None, BLOCK, 1), lambda b, kv, n: (b, n, 0))
    prow_prev = pl.BlockSpec((None, 1, BLOCK), lambda b, kv, n: (b, 0, jnp.maximum(n - 1, 0)))
    prow_cur = pl.BlockSpec((None, 1, BLOCK), lambda b, kv, n: (b, 0, n))
    smem = pl.BlockSpec(memory_space=pltpu.SMEM)
    return nb, q, kprev, kcur, pcol, prow_prev, prow_cur, smem


def _swa_probs(q, kk, dist, valid, slope, sink):
    sc = lax.dot_general(q, kk, _NT, preferred_element_type=F32) * (HEAD_DIM_A ** -0.5)
    sc = sc - slope * dist
    sc = jnp.where(valid, sc, NEG)
    m = jnp.maximum(jnp.max(sc, axis=-1, keepdims=True), sink)
    e = jnp.exp(sc - m)
    es = jnp.exp(sink - m)
    inv = 1.0 / (jnp.sum(e, axis=-1, keepdims=True) + es)
    return e * inv, es * inv


def _swa_window(n, kp_ref, kc_ref, vp_ref, vc_ref, pc_ref, prp_ref, prc_ref):
    kk = jnp.concatenate([kp_ref[...], kc_ref[...]], axis=0)
    vv = jnp.concatenate([vp_ref[...], vc_ref[...]], axis=0)
    dist = pc_ref[...] - jnp.concatenate([prp_ref[...], prc_ref[...]], axis=1)
    qi = lax.broadcasted_iota(jnp.int32, (BLOCK, 2 * BLOCK), 0) + BLOCK
    ki = lax.broadcasted_iota(jnp.int32, (BLOCK, 2 * BLOCK), 1)
    diff = qi - ki
    valid = (diff >= 0) & (diff < BLOCK) & ((n > 0) | (ki >= BLOCK))
    return kk, vv, dist, valid


def swa_fwd(name, q, k, v, pos_col, pos_row, slopes, sinks):
    b, _, s, _ = q.shape
    nb, qs, kprev, kcur, pcol, prp, prc, smem = _swa_specs(s)

    def body(q_ref, kp_ref, kc_ref, vp_ref, vc_ref, pc_ref, prp_ref, prc_ref, sl_ref, sk_ref, o_ref):
        kv, n = pl.program_id(1), pl.program_id(2)
        kk, vv, dist, valid = _swa_window(n, kp_ref, kc_ref, vp_ref, vc_ref, pc_ref, prp_ref, prc_ref)
        for g in range(GROUP_A):
            hd = kv * GROUP_A + g
            p, _ = _swa_probs(q_ref[g], kk, dist, valid, sl_ref[hd], sk_ref[hd])
            o_ref[g] = jnp.dot(p.astype(BF16), vv, preferred_element_type=F32)

    return pl.pallas_call(
        body, name=name, out_shape=jax.ShapeDtypeStruct(q.shape, F32), grid=(b, N_KV_HEADS_A, nb),
        in_specs=[qs, kprev, kcur, kprev, kcur, pcol, prp, prc, smem, smem], out_specs=qs,
        compiler_params=_params(("parallel", "parallel", "parallel")),
    )(q, k, k, v, v, pos_col, pos_row, pos_row, slopes, sinks)


def swa_bwd(name, q, k, v, pos_col, pos_row, slopes, sinks, do):
    b, _, s, _ = q.shape
    nb, qs, kprev, kcur, pcol, prp, prc, smem = _swa_specs(s)

    def body(q_ref, kp_ref, kc_ref, vp_ref, vc_ref, pc_ref, prp_ref, prc_ref, sl_ref, sk_ref, do_ref, dq_ref, dk_ref, dv_ref, ds_ref):
        bi, kv, n = pl.program_id(0), pl.program_id(1), pl.program_id(2)
        kk, vv, dist, valid = _swa_window(n, kp_ref, kc_ref, vp_ref, vc_ref, pc_ref, prp_ref, prc_ref)

        @pl.when((bi == 0) & (kv == 0) & (n == 0))
        def _():
            ds_ref[...] = jnp.zeros_like(ds_ref)

        @pl.when(n == 0)
        def _():
            dk_ref[...] = jnp.zeros_like(dk_ref)
            dv_ref[...] = jnp.zeros_like(dv_ref)

        dkk = jnp.zeros((2 * BLOCK, HEAD_DIM_A), F32)
        dvv = jnp.zeros((2 * BLOCK, HEAD_DIM_A), F32)
        head_row = lax.broadcasted_iota(jnp.int32, (N_HEADS_A, LANES), 0)
        dsink = jnp.zeros((N_HEADS_A, LANES), F32)
        for g in range(GROUP_A):
            hd = kv * GROUP_A + g
            qg = q_ref[g]
            p, ps = _swa_probs(qg, kk, dist, valid, sl_ref[hd], sk_ref[hd])
            dob = do_ref[g].astype(BF16)
            dvv = dvv + lax.dot_general(p.astype(BF16), dob, _TN, preferred_element_type=F32)
            dp = lax.dot_general(dob, vv, _NT, preferred_element_type=F32)
            rs = jnp.sum(p * dp, axis=-1, keepdims=True)
            dsb = (p * (dp - rs) * (HEAD_DIM_A ** -0.5)).astype(BF16)
            dq_ref[g] = jnp.dot(dsb, kk, preferred_element_type=F32)
            dkk = dkk + lax.dot_general(dsb, qg, _TN, preferred_element_type=F32)
            dsink = dsink + jnp.where(head_row == hd, -jnp.sum(ps * rs), 0.0)
        ds_ref[...] += dsink

        @pl.when(n > 0)
        def _():
            start = pl.multiple_of((n - 1) * BLOCK, BLOCK)
            dk_ref[pl.ds(start, 2 * BLOCK), :] += dkk
            dv_ref[pl.ds(start, 2 * BLOCK), :] += dvv

        @pl.when(n == 0)
        def _():
            dk_ref[0:BLOCK, :] += dkk[BLOCK:, :]
            dv_ref[0:BLOCK, :] += dvv[BLOCK:, :]

    kv_full = pl.BlockSpec((None, None, s, HEAD_DIM_A), lambda bi, kv, n: (bi, kv, 0, 0))
    return pl.pallas_call(
        body, name=name, grid=(b, N_KV_HEADS_A, nb),
        out_shape=(jax.ShapeDtypeStruct(q.shape, F32), jax.ShapeDtypeStruct(k.shape, F32), jax.ShapeDtypeStruct(k.shape, F32),
                   jax.ShapeDtypeStruct((N_HEADS_A, LANES), F32)),
        in_specs=[qs, kprev, kcur, kprev, kcur, pcol, prp, prc, smem, smem, qs],
        out_specs=(qs, kv_full, kv_full, pl.BlockSpec((N_HEADS_A, LANES), lambda bi, kv, n: (0, 0))),
        compiler_params=_params(("arbitrary", "arbitrary", "arbitrary")),
    )(q, k, k, v, v, pos_col, pos_row, pos_row, slopes, sinks, do)


MLA_T = 256


def _blocks_t(z, t):
    b, h, s, d = z.shape
    return z.reshape(b, h, s // t, t, d).transpose(0, 1, 2, 4, 3)


def _unblocks_t(z):
    b, h, nb, d, t = z.shape
    return z.transpose(0, 1, 2, 4, 3).reshape(b, h, nb * t, d)


def mla_fwd(name, q, k, vt):
    b, h, s, dq = q.shape
    _, _, nb, dv, t = vt.shape
    scale = QK_DIM_B ** -0.5

    def body(q_ref, k_ref, vt_ref, o_ref, lse_ref):
        causal = lax.broadcasted_iota(jnp.int32, (t, t), 0) <= lax.broadcasted_iota(jnp.int32, (t, t), 1)
        for i in range(nb):
            q_i = q_ref[i * t:(i + 1) * t, :]

            def step(j, carry, q_i=q_i, diagonal=False):
                m, l, acc = carry
                rows = slice(j * t, (j + 1) * t) if diagonal else pl.ds(pl.multiple_of(j * t, t), t)
                st = lax.dot_general(k_ref[rows, :], q_i, _NT, preferred_element_type=F32) * scale
                if diagonal:
                    st = jnp.where(causal, st, NEG)
                m_new = jnp.maximum(m, jnp.max(st, axis=0, keepdims=True))
                a = jnp.exp(m - m_new)
                p = jnp.exp(st - m_new)
                l = a * l + jnp.sum(p, axis=0, keepdims=True)
                acc = a * acc + jnp.dot(vt_ref[j], p.astype(BF16), preferred_element_type=F32)
                return m_new, l, acc

            carry = (jnp.full((1, t), NEG, F32), jnp.zeros((1, t), F32), jnp.zeros((dv, t), F32))
            if i > 0:
                carry = lax.fori_loop(0, i, step, carry)
            m, l, acc = step(i, carry, diagonal=True)
            o_ref[i] = acc / l
            lse_ref[i] = m + jnp.log(l)

    qs = pl.BlockSpec((None, None, s, dq), lambda bi, hi: (bi, hi, 0, 0))
    ts = pl.BlockSpec((None, None, nb, dv, t), lambda bi, hi: (bi, hi, 0, 0, 0))
    ls = pl.BlockSpec((None, None, nb, 1, t), lambda bi, hi: (bi, hi, 0, 0, 0))
    return pl.pallas_call(
        body, name=name, grid=(b, h),
        out_shape=(jax.ShapeDtypeStruct((b, h, nb, dv, t), F32), jax.ShapeDtypeStruct((b, h, nb, 1, t), F32)),
        in_specs=[qs, qs, ts], out_specs=(ts, ls), compiler_params=_params(("parallel", "parallel")),
    )(q, k, vt)


def mla_bwd(name, q, k, kt, v, do, dot, ot, lse):
    b, h, s, dq = q.shape
    _, _, nb, dv, t = ot.shape
    scale = QK_DIM_B ** -0.5

    def body(q_ref, k_ref, kt_ref, v_ref, do_ref, dot_ref, ot_ref, lse_ref, dqt_ref, dk_ref, dv_ref):
        causal = lax.broadcasted_iota(jnp.int32, (t, t), 0) <= lax.broadcasted_iota(jnp.int32, (t, t), 1)
        dk_ref[...] = jnp.zeros_like(dk_ref)
        dv_ref[...] = jnp.zeros_like(dv_ref)
        for i in range(nb):
            q_i = q_ref[i * t:(i + 1) * t, :]
            do_i = do_ref[i * t:(i + 1) * t, :]
            dot_i = dot_ref[i]
            delta = jnp.sum(ot_ref[i] * dot_i, axis=0, keepdims=True)
            dot_b = dot_i.astype(BF16)
            lse_i = lse_ref[i]

            def step(j, dqt, q_i=q_i, do_i=do_i, delta=delta, dot_b=dot_b, lse_i=lse_i, diagonal=False):
                rows = slice(j * t, (j + 1) * t) if diagonal else pl.ds(pl.multiple_of(j * t, t), t)
                st = lax.dot_general(k_ref[rows, :], q_i, _NT, preferred_element_type=F32) * scale
                if diagonal:
                    st = jnp.where(causal, st, NEG)
                pt = jnp.exp(st - lse_i)
                dpt = jnp.dot(v_ref[rows, :], dot_b, preferred_element_type=F32)
                dst = (pt * (dpt - delta) * scale).astype(BF16)
                dv_ref[rows, :] += jnp.dot(pt.astype(BF16), do_i, preferred_element_type=F32)
                dk_ref[rows, :] += jnp.dot(dst, q_i, preferred_element_type=F32)
                return dqt + jnp.dot(kt_ref[j], dst, preferred_element_type=F32)

            dqt = jnp.zeros((dq, t), F32)
            if i > 0:
                dqt = lax.fori_loop(0, i, step, dqt)
            dqt_ref[i] = step(i, dqt, diagonal=True)

    def nat(d):
        return pl.BlockSpec((None, None, s, d), lambda bi, hi: (bi, hi, 0, 0))

    def blk(d):
        return pl.BlockSpec((None, None, nb, d, t), lambda bi, hi: (bi, hi, 0, 0, 0))

    return pl.pallas_call(
        body, name=name, grid=(b, h),
        out_shape=(jax.ShapeDtypeStruct((b, h, nb, dq, t), F32), jax.ShapeDtypeStruct((b, h, s, dq), F32),
                   jax.ShapeDtypeStruct((b, h, s, dv), F32)),
        in_specs=[nat(dq), nat(dq), blk(dq), nat(dv), nat(dv), blk(dv), blk(dv), blk(1)],
        out_specs=(blk(dq), nat(dq), nat(dv)), compiler_params=_params(("parallel", "parallel")),
    )(q, k, kt, v, do, dot, ot, lse)


_HBM = pl.BlockSpec(memory_space=pltpu.HBM)


def _place():
    x, y, c = lax.axis_index("x"), lax.axis_index("y"), lax.axis_index("c")
    chips = [(1 - x, y), (x, 1 - y), (1 - x, 1 - y)]
    return x, y, c, chips


def _remote(src, dst, send_sem, recv_sem, dev):
    return pltpu.make_async_remote_copy(src_ref=src, dst_ref=dst, send_sem=send_sem, recv_sem=recv_sem,
                                        device_id=dev, device_id_type=MESH)


def allgather_weights(pack):
    _, r, cols = pack.shape

    def body(p_ref, o_ref, send_sems, recv_sems):
        x, y, c, chips = _place()
        s_me = 2 * x + y
        sibling = (x, y, 1 - c)
        first = [_remote(p_ref.at[c], o_ref.at[c, s_me], send_sems.at[j], recv_sems.at[j], (cx, cy, c))
                 for j, (cx, cy) in enumerate(chips)]
        for cp in first:
            cp.start()
        passed = []
        for j, (cx, cy) in enumerate(chips):
            blk = o_ref.at[c, 2 * cx + cy]
            _remote(blk, blk, send_sems.at[j], recv_sems.at[j], (cx, cy, c)).wait_recv()
            fwd = _remote(blk, blk, send_sems.at[3 + j], recv_sems.at[3 + j], sibling)
            fwd.start()
            passed.append(fwd)
        for j, (cx, cy) in enumerate(chips):
            blk = o_ref.at[1 - c, 2 * cx + cy]
            _remote(blk, blk, send_sems.at[3 + j], recv_sems.at[3 + j], sibling).wait_recv()
        for cp in first + passed:
            cp.wait_send()

    return pl.pallas_call(
        body, name="allgather_weights", out_shape=jax.ShapeDtypeStruct((2, N_SHARDS, r, cols), pack.dtype),
        in_specs=[_HBM], out_specs=_HBM,
        scratch_shapes=[pltpu.SemaphoreType.DMA((6,)), pltpu.SemaphoreType.DMA((6,))],
    )(pack)


def sibling_swap(grads):
    _, ns, r, cols = grads.shape

    def body(g_ref, o_ref, send_sem, recv_sem):
        x, y, c, _ = _place()
        cp = _remote(g_ref.at[1 - c], o_ref, send_sem, recv_sem, (x, y, 1 - c))
        cp.start()
        cp.wait()

    return pl.pallas_call(
        body, name="sibling_swap", out_shape=jax.ShapeDtypeStruct((ns, r, cols), grads.dtype),
        in_specs=[_HBM], out_specs=_HBM, scratch_shapes=[pltpu.SemaphoreType.DMA, pltpu.SemaphoreType.DMA],
    )(grads)


def add_own_slab(grads, other, c_idx, tr=256):
    _, n, cols = grads.shape
    tr = _tile(n, tr, 16)

    def body(c_ref, g_ref, o_ref, out_ref):
        out_ref[...] = (g_ref[...] + o_ref[...]).astype(out_ref.dtype)

    return pl.pallas_call(
        body, name="add_own_slab", out_shape=jax.ShapeDtypeStruct((n, cols), BF16),
        grid_spec=pltpu.PrefetchScalarGridSpec(
            num_scalar_prefetch=1, grid=(n // tr,),
            in_specs=[pl.BlockSpec((None, tr, cols), lambda i, c_ref: (c_ref[0], i, 0)), pl.BlockSpec((tr, cols), lambda i, c_ref: (i, 0))],
            out_specs=pl.BlockSpec((tr, cols), lambda i, c_ref: (i, 0))),
        compiler_params=_params(("parallel",)),
    )(c_idx, grads, other)


def scatter_to_owners(part):
    ns, r, cols = part.shape

    def body(p_ref, o_ref, send_sems, recv_sems):
        x, y, c, chips = _place()
        s_me = 2 * x + y
        sends = [_remote(p_ref.at[2 * cx + cy], o_ref.at[s_me], send_sems.at[j], recv_sems.at[j], (cx, cy, c))
                 for j, (cx, cy) in enumerate(chips)]
        for cp in sends:
            cp.start()
        for j, (cx, cy) in enumerate(chips):
            slot = o_ref.at[2 * cx + cy]
            _remote(slot, slot, send_sems.at[j], recv_sems.at[j], (cx, cy, c)).wait_recv()
        for cp in sends:
            cp.wait_send()

    return pl.pallas_call(
        body, name="scatter_to_owners", out_shape=jax.ShapeDtypeStruct((ns, r, cols), part.dtype),
        in_specs=[_HBM], out_specs=_HBM,
        scratch_shapes=[pltpu.SemaphoreType.DMA((3,)), pltpu.SemaphoreType.DMA((3,))],
    )(part)


def sum_slots(slots, part, place, tr=256):
    ns, r, cols = slots.shape
    tr = _tile(r, tr, 16)

    def body(pl_ref, s_ref, own_ref, o_ref):
        total = own_ref[...].astype(F32)
        for k in range(1, ns):
            total = total + s_ref[(pl_ref[0] + k) % ns].astype(F32)
        o_ref[...] = total

    return pl.pallas_call(
        body, name="sum_slots", out_shape=jax.ShapeDtypeStruct((2, r, cols), F32),
        grid_spec=pltpu.PrefetchScalarGridSpec(
            num_scalar_prefetch=1, grid=(r // tr,),
            in_specs=[pl.BlockSpec((ns, tr, cols), lambda i, p: (0, i, 0)), pl.BlockSpec((None, tr, cols), lambda i, p: (p[0], i, 0))],
            out_specs=pl.BlockSpec((None, tr, cols), lambda i, p: (p[1], i, 0))),
        compiler_params=_params(("parallel",)),
    )(place, slots, part)


def sibling_share(both):
    _, r, cols = both.shape

    def body(b_ref, o_ref, send_sem, recv_sem):
        x, y, c, _ = _place()
        cp = _remote(b_ref.at[c], o_ref.at[c], send_sem, recv_sem, (x, y, 1 - c))
        cp.start()
        theirs = o_ref.at[1 - c]
        _remote(theirs, theirs, send_sem, recv_sem, (x, y, 1 - c)).wait_recv()
        cp.wait_send()

    return pl.pallas_call(
        body, name="sibling_share", out_shape=jax.ShapeDtypeStruct(both.shape, both.dtype),
        in_specs=[_HBM], out_specs=_HBM, input_output_aliases={0: 0},
        scratch_shapes=[pltpu.SemaphoreType.DMA, pltpu.SemaphoreType.DMA],
    )(both)


def small_allreduce(vec):
    r, cols = vec.shape

    def body(v_ref, o_ref, buf, send_sems, recv_sems):
        x, y, c, _ = _place()
        me = 4 * x + 2 * y + c
        buf[me] = v_ref[...]
        peers = []
        for k in range(1, N_DEV):
            px = (1 - x) if (k & 4) else x
            py = (1 - y) if (k & 2) else y
            pc = (1 - c) if (k & 1) else c
            peers.append((px, py, pc))
        sends = [_remote(buf.at[me], buf.at[me], send_sems.at[k], recv_sems.at[k], peer) for k, peer in enumerate(peers)]
        for cp in sends:
            cp.start()
        for k, (px, py, pc) in enumerate(peers):
            slot = buf.at[4 * px + 2 * py + pc]
            _remote(slot, slot, send_sems.at[k], recv_sems.at[k], (px, py, pc)).wait_recv()
        for cp in sends:
            cp.wait_send()
        total = buf[0]
        for k in range(1, N_DEV):
            total = total + buf[k]
        o_ref[...] = total

    return pl.pallas_call(
        body, name="small_allreduce", out_shape=jax.ShapeDtypeStruct((r, cols), F32),
        in_specs=[pl.BlockSpec(memory_space=pltpu.VMEM)], out_specs=pl.BlockSpec(memory_space=pltpu.VMEM),
        scratch_shapes=[pltpu.VMEM((N_DEV, r, cols), F32), pltpu.SemaphoreType.DMA((N_DEV - 1,)), pltpu.SemaphoreType.DMA((N_DEV - 1,))],
    )(vec)


ARG_NAMES = (("x", "positions") + WEIGHT_NAMES + ("loss_target",) + tuple("m_" + n for n in WEIGHT_NAMES)
             + tuple("v_" + n for n in WEIGHT_NAMES))

_O_QA, _O_KA, _O_VA = 0, WIDTH_A, WIDTH_A + KV_WIDTH_A
_O_CQ = _O_VA + KV_WIDTH_A
_O_CKV = _O_CQ + Q_LORA_RANK
_O_GA = _O_CKV + KV_LORA_RANK


def _rows_of(n_elems):
    return -(-n_elems // PACK_COLS)


def _pack_layout(shard_shapes):
    offs, row = {}, 0
    for name in PACK_NAMES:
        offs[name] = row
        row += _rows_of(shard_shapes[name][0] * shard_shapes[name][1])
    return offs, -(-row // PACK_ROW_ALIGN) * PACK_ROW_ALIGN


def _to_rows(a):
    flat = a.reshape(-1)
    rows = _rows_of(flat.shape[0])
    if rows * PACK_COLS != flat.shape[0]:
        flat = jnp.pad(flat, (0, rows * PACK_COLS - flat.shape[0]))
    return flat.reshape(rows, PACK_COLS)


def _from_rows(slab, off, shape):
    rows = _rows_of(shape[0] * shape[1])
    return slab[off:off + rows].reshape(-1)[:shape[0] * shape[1]].reshape(shape)


def _pack_slab(pieces, rows_total, dtype):
    parts = [_to_rows(pieces[name].astype(dtype)) for name in PACK_NAMES]
    used = sum(p.shape[0] for p in parts)
    if used < rows_total:
        parts.append(jnp.zeros((rows_total - used, PACK_COLS), dtype))
    return jnp.concatenate(parts, axis=0)


def _heads(z, b, s, h, d):
    return z.reshape(b, s, h, d).transpose(0, 2, 1, 3)


def _unheads(z):
    b, h, s, d = z.shape
    return z.transpose(0, 2, 1, 3).reshape(b * s, h * d)


def _ffn_fwd(tag, x, gain, wgu, wd):
    t = x.shape[0]
    nb = rms_fwd(tag + "_rms", x, gain)
    gu = gmm_up(tag + "_up", nb, wgu)
    gu4 = gu.reshape(2, N_SHARDS, t, gu.shape[-1])
    h = swiglu_fwd(tag + "_act", gu4)
    out = gmm_down(tag + "_down", h, wd, x, 0.5)
    return out, (x, nb, gu4, h)


def _ffn_bwd(tag, saved, gain, wgu, wd, dout, doutb):
    x, nb, gu4, h = saved
    t = x.shape[0]
    dh = gmm_nt_out(tag + "_dh", doutb, wd, 0.5)
    dwd = gmm_tn_a(tag + "_dwd", h, doutb, 0.5)
    dgu = swiglu_bwd(tag + "_dact", gu4, dh).reshape(2 * N_SHARDS, t, gu4.shape[-1])
    dwgu = gmm_tn_b(tag + "_dwgu", nb, dgu)
    dn = gmm_nt_red(tag + "_dn", dgu, wgu)
    dx, dxb, dgain = rms_bwd(tag + "_drms", x, gain, dn, dres=dout)
    return dx, dxb, dwgu, dwd, dgain


def _mixer_fwd(tag, x1, sm, w, aux):
    b, s = aux["b"], aux["s"]
    d = x1.shape[1]
    o_gb, o_kr = _O_GA + d, _O_GA + 2 * d
    hb = rms_fwd(tag + "_rms", x1, sm["mix_norm"])
    proj = mm_nn(tag + "_proj", hb, w["win"], tm=512, tn=1664, tk=1024)
    qa_raw = _heads(proj[:, _O_QA:_O_KA], b, s, N_HEADS_A, HEAD_DIM_A)
    ka_raw = _heads(proj[:, _O_KA:_O_VA], b, s, N_KV_HEADS_A, HEAD_DIM_A)
    va = _heads(proj[:, _O_VA:_O_CQ], b, s, N_KV_HEADS_A, HEAD_DIM_A).astype(BF16)
    cq, ckv, kr = proj[:, _O_CQ:_O_CKV], proj[:, _O_CKV:_O_GA], proj[:, o_kr:o_kr + QK_ROPE_DIM]
    cqn = rms_fwd(tag + "_rms_cq", cq, sm["mla_q_lora_norm"], tm=1024)
    ckvn = rms_fwd(tag + "_rms_ckv", ckv, sm["mla_kv_lora_norm"], tm=1024)
    qb_raw = _heads(mm_nn(tag + "_uq", cqn, w["wuq"], tm=1024), b, s, N_HEADS_B, QK_DIM_B)
    kv = mm_nn(tag + "_ukv", ckvn, w["wukv"], tm=1024).reshape(b, s, N_HEADS_B, QK_NOPE_DIM + V_DIM_B).transpose(0, 2, 1, 3)
    vb = kv[..., QK_NOPE_DIM:].astype(BF16)
    kr_b = jnp.broadcast_to(kr.reshape(b, 1, s, QK_ROPE_DIM), (b, N_HEADS_B, s, QK_ROPE_DIM))
    kfull_raw = jnp.concatenate([kv[..., :QK_NOPE_DIM], kr_b], axis=-1)
    qah = qk_prep_fwd(tag + "_qa_norm", qa_raw, sm["swa_q_norm"])
    kah = qk_prep_fwd(tag + "_ka_norm", ka_raw, sm["swa_k_norm"])
    oa = swa_fwd(tag + "_swa", qah, kah, va, aux["pos_col"], aux["pos_row"], aux["slopes"], sm["swa_sinks"].reshape(-1))
    qbh = qk_prep_fwd(tag + "_qb_norm", qb_raw, sm["mla_q_norm"], aux["rope"])
    kbh = qk_prep_fwd(tag + "_kb_norm", kfull_raw, sm["mla_k_norm"], aux["rope"])
    mla_t = MLA_T if s % MLA_T == 0 else BLOCK
    obt, lse = mla_fwd(tag + "_mla", qbh, kbh, _blocks_t(vb, mla_t))
    oab = _unheads(oa).astype(BF16)
    obb = obt.transpose(0, 2, 4, 1, 3).reshape(b * s, WIDTH_B).astype(BF16)
    ya = mm_nn(tag + "_branch_a", oab, w["wa"], tm=1024)
    yb = mm_nn(tag + "_branch_b", obb, w["wb"], tm=1024)
    mg = gate_fwd(tag + "_gate", proj, ya, yb, _O_GA, o_gb)
    x2 = mm_nn(tag + "_out", mg, w["wo"], res=x1)
    saved = dict(x1=x1, hb=hb, proj=proj, qa_raw=qa_raw, ka_raw=ka_raw, va=va, cq=cq, ckv=ckv, cqn=cqn, ckvn=ckvn,
                 qb_raw=qb_raw, kfull_raw=kfull_raw, vb=vb, qah=qah, kah=kah, qbh=qbh, kbh=kbh, obt=obt, lse=lse,
                 oab=oab, obb=obb, ya=ya, yb=yb, mg=mg)
    return x2, saved


def _mixer_bwd(tag, sv, sm, w, aux, dx2, dx2b):
    b, s = aux["b"], aux["s"]
    t, d = dx2.shape
    o_gb = _O_GA + d
    dmg = mm_nt(tag + "_d_out", dx2b, w["wo"])
    dwo = mm_tn(tag + "_dw_out", sv["mg"], dx2b)
    dya, dyb, dga, dgb = gate_bwd(tag + "_dgate", sv["proj"], sv["ya"], sv["yb"], dmg, _O_GA, o_gb)
    dwa = mm_tn(tag + "_dw_branch_a", sv["oab"], dya)
    dwb = mm_tn(tag + "_dw_branch_b", sv["obb"], dyb)
    doa = _heads(mm_nt(tag + "_d_branch_a", dya, w["wa"], tm=1024), b, s, N_HEADS_A, HEAD_DIM_A)
    dob = _heads(mm_nt(tag + "_d_branch_b", dyb, w["wb"], tm=1024), b, s, N_HEADS_B, V_DIM_B)
    mla_t = sv["obt"].shape[-1]
    sinks = sm["swa_sinks"].reshape(-1)
    dqah, dkah, dva, dsinks = swa_bwd(tag + "_dswa", sv["qah"], sv["kah"], sv["va"], aux["pos_col"], aux["pos_row"],
                                      aux["slopes"], sinks, doa)
    dqa_raw, dg_swa_q = qk_prep_bwd(tag + "_dqa_norm", sv["qa_raw"], sm["swa_q_norm"], dqah)
    dka_raw, dg_swa_k = qk_prep_bwd(tag + "_dka_norm", sv["ka_raw"], sm["swa_k_norm"], dkah)
    dqbt, dkbh, dvb = mla_bwd(tag + "_dmla", sv["qbh"], sv["kbh"], _blocks_t(sv["kbh"], mla_t), sv["vb"], dob.astype(BF16),
                              _blocks_t(dob, mla_t), sv["obt"], sv["lse"])
    dqbh = _unblocks_t(dqbt)
    dqb_raw, dg_mla_q = qk_prep_bwd(tag + "_dqb_norm", sv["qb_raw"], sm["mla_q_norm"], dqbh, aux["rope"])
    dkfull, dkr_sum, dg_mla_k = qk_prep_bwd(tag + "_dkb_norm", sv["kfull_raw"], sm["mla_k_norm"], dkbh, aux["rope"], head_sum=True)
    dq_tok = _unheads(dqb_raw)
    dkv_tok = _unheads(jnp.concatenate([dkfull[..., :QK_NOPE_DIM], dvb.astype(BF16)], axis=-1))
    dwuq = mm_tn(tag + "_dw_uq", sv["cqn"], dq_tok, tk=1024)
    dwukv = mm_tn(tag + "_dw_ukv", sv["ckvn"], dkv_tok, tk=1024)
    dcqn = mm_nt(tag + "_d_uq", dq_tok, w["wuq"], tm=1024)
    dckvn = mm_nt(tag + "_d_ukv", dkv_tok, w["wukv"], tm=1024)
    dcq, dg_q_lora = rms_bwd(tag + "_drms_cq", sv["cq"], sm["mla_q_lora_norm"], dcqn, want_f32=False, tm=1024)
    dckv, dg_kv_lora = rms_bwd(tag + "_drms_ckv", sv["ckv"], sm["mla_kv_lora_norm"], dckvn, want_f32=False, tm=1024)
    dkr = dkr_sum[..., QK_NOPE_DIM:].reshape(t, QK_ROPE_DIM).astype(BF16)
    parts = [_unheads(dqa_raw), _unheads(dka_raw), _unheads(dva.astype(BF16)), dcq, dckv, dga, dgb, dkr]
    pad = w["win"].shape[1] - sum(p.shape[1] for p in parts)
    if pad:
        parts.append(jnp.zeros((t, pad), BF16))
    dproj = jnp.concatenate(parts, axis=1)
    dwin = mm_tn(tag + "_dw_in", sv["hb"], dproj, tm=1024, tn=1664, tk=512)
    dh = mm_nt(tag + "_d_in", dproj, w["win"], tm=512, tn=1024, tk=1664)
    dx1, dx1b, dg_mix = rms_bwd(tag + "_drms", sv["x1"], sm["mix_norm"], dh, dres=dx2)
    wgrads = dict(win=dwin, wuq=dwuq, wukv=dwukv, wa=dwa, wb=dwb, wo=dwo)
    sgrads = dict(mix_norm=dg_mix, swa_q_norm=dg_swa_q, swa_k_norm=dg_swa_k, swa_sinks=dsinks[:, 0].reshape(1, -1),
                  mla_q_lora_norm=dg_q_lora, mla_kv_lora_norm=dg_kv_lora, mla_q_norm=dg_mla_q, mla_k_norm=dg_mla_k)
    return dx1, dx1b, wgrads, sgrads


def _layer_weights(gathered_l, offs, shard_shapes, d):
    def pieces(name):
        return [_from_rows(gathered_l[s], offs[name], shard_shapes[name]) for s in range(N_SHARDS)]

    def cols(name):
        return jnp.concatenate(pieces(name), axis=1)

    win_ref = cols("w_in")
    width = win_ref.shape[1]
    parts = [win_ref[:, :_O_GA], win_ref[:, _O_GA + QK_ROPE_DIM:], win_ref[:, _O_GA:_O_GA + QK_ROPE_DIM]]
    padded = -(-width // (2 * LANES)) * (2 * LANES)
    if padded > width:
        parts.append(jnp.zeros((d, padded - width), BF16))
    return dict(
        wgu1=jnp.stack(pieces("ffn1_w_gate") + pieces("ffn1_w_up")), wd1=jnp.stack(pieces("ffn1_w_down")),
        wgu2=jnp.stack(pieces("ffn2_w_gate") + pieces("ffn2_w_up")), wd2=jnp.stack(pieces("ffn2_w_down")),
        win=jnp.concatenate(parts, axis=1), wuq=cols("mla_w_uq"), wukv=cols("mla_w_ukv"),
        wa=cols("w_branch_a"), wb=cols("w_branch_b"), wo=jnp.concatenate(pieces("w_out"), axis=0))


def _grad_pieces(g, s, shard_shapes, d):
    def col(a, name):
        c = shard_shapes[name][1]
        return a[:, s * c:(s + 1) * c]

    win = g["win"]
    width = N_SHARDS * shard_shapes["w_in"][1]
    win_ref = jnp.concatenate([win[:, :_O_GA], win[:, width - QK_ROPE_DIM:width], win[:, _O_GA:width - QK_ROPE_DIM]], axis=1)
    r_out = shard_shapes["w_out"][0]
    return dict(
        ffn1_w_gate=g["wgu1"][s], ffn1_w_up=g["wgu1"][N_SHARDS + s], ffn1_w_down=g["wd1"][s],
        ffn2_w_gate=g["wgu2"][s], ffn2_w_up=g["wgu2"][N_SHARDS + s], ffn2_w_down=g["wd2"][s],
        w_in=col(win_ref, "w_in"), mla_w_uq=col(g["wuq"], "mla_w_uq"), mla_w_ukv=col(g["wukv"], "mla_w_ukv"),
        w_branch_a=col(g["wa"], "w_branch_a"), w_branch_b=col(g["wb"], "w_branch_b"),
        w_out=g["wo"][s * r_out:(s + 1) * r_out])


def _local_step(x, positions, target, weights, small, depth):
    b, s, d = x.shape
    t = b * s
    posf = positions.astype(F32)
    half = QK_ROPE_DIM // 2
    inv_freq = ROPE_BASE ** (-jnp.arange(half, dtype=F32) / half)
    ang = posf[..., None] * inv_freq
    cos, sin = jnp.cos(ang), jnp.sin(ang)
    rope = (jnp.concatenate([jnp.ones((b, s, QK_NOPE_DIM), F32), cos, cos], axis=-1),
            jnp.concatenate([jnp.zeros((b, s, QK_NOPE_DIM), F32), sin, sin], axis=-1))
    slopes = jnp.exp2(-8.0 * (jnp.arange(N_HEADS_A, dtype=F32) + 1.0) / N_HEADS_A)
    aux = dict(b=b, s=s, pos_col=posf.reshape(b, s, 1), pos_row=posf.reshape(b, 1, s), rope=rope, slopes=slopes)

    def sm_of(l):
        return {n: small[n][l:l + 1] for n in SMALL_NAMES}

    h = x.reshape(t, d)
    saved = []
    for l in range(depth):
        sm, w = sm_of(l), weights[l]
        h, s1 = _ffn_fwd(f"l{l}_ffn1", h, sm["ffn1_norm"], w["wgu1"], w["wd1"])
        h, s2 = _mixer_fwd(f"l{l}_mix", h, sm, w, aux)
        h, s3 = _ffn_fwd(f"l{l}_ffn2", h, sm["ffn2_norm"], w["wgu2"], w["wd2"])
        saved.append((s1, s2, s3))
    loss, dh, dhb = loss_fwd_bwd("loss", h, target.reshape(t, d))
    wgrads, sgrads = [None] * depth, [None] * depth
    for l in reversed(range(depth)):
        sm, w = sm_of(l), weights[l]
        s1, s2, s3 = saved[l]
        dh, dhb, dwgu2, dwd2, dg_f2 = _ffn_bwd(f"l{l}_ffn2", s3, sm["ffn2_norm"], w["wgu2"], w["wd2"], dh, dhb)
        dh, dhb, wg, sg = _mixer_bwd(f"l{l}_mix", s2, sm, w, aux, dh, dhb)
        dh, dhb, dwgu1, dwd1, dg_f1 = _ffn_bwd(f"l{l}_ffn1", s1, sm["ffn1_norm"], w["wgu1"], w["wd1"], dh, dhb)
        wg.update(wgu1=dwgu1, wd1=dwd1, wgu2=dwgu2, wd2=dwd2)
        sg.update(ffn1_norm=dg_f1, ffn2_norm=dg_f2)
        wgrads[l], sgrads[l] = wg, sg
    return loss, dh.reshape(b, s, d), wgrads, sgrads


def kernel(x, positions, ffn1_norm, ffn1_w_gate, ffn1_w_up, ffn1_w_down, mix_norm, w_in, swa_q_norm, swa_k_norm, swa_sinks, mla_q_lora_norm, mla_w_uq, mla_kv_lora_norm, mla_w_ukv, mla_q_norm, mla_k_norm, w_branch_a, w_branch_b, w_out, ffn2_norm, ffn2_w_gate, ffn2_w_up, ffn2_w_down, loss_target, m_ffn1_norm, m_ffn1_w_gate, m_ffn1_w_up, m_ffn1_w_down, m_mix_norm, m_w_in, m_swa_q_norm, m_swa_k_norm, m_swa_sinks, m_mla_q_lora_norm, m_mla_w_uq, m_mla_kv_lora_norm, m_mla_w_ukv, m_mla_q_norm, m_mla_k_norm, m_w_branch_a, m_w_branch_b, m_w_out, m_ffn2_norm, m_ffn2_w_gate, m_ffn2_w_up, m_ffn2_w_down, v_ffn1_norm, v_ffn1_w_gate, v_ffn1_w_up, v_ffn1_w_down, v_mix_norm, v_w_in, v_swa_q_norm, v_swa_k_norm, v_swa_sinks, v_mla_q_lora_norm, v_mla_w_uq, v_mla_kv_lora_norm, v_mla_w_ukv, v_mla_q_norm, v_mla_k_norm, v_w_branch_a, v_w_branch_b, v_w_out, v_ffn2_norm, v_ffn2_w_gate, v_ffn2_w_up, v_ffn2_w_down):
    args = (x, positions, ffn1_norm, ffn1_w_gate, ffn1_w_up, ffn1_w_down, mix_norm, w_in, swa_q_norm, swa_k_norm, swa_sinks, mla_q_lora_norm, mla_w_uq, mla_kv_lora_norm, mla_w_ukv, mla_q_norm, mla_k_norm, w_branch_a, w_branch_b, w_out, ffn2_norm, ffn2_w_gate, ffn2_w_up, ffn2_w_down, loss_target, m_ffn1_norm, m_ffn1_w_gate, m_ffn1_w_up, m_ffn1_w_down, m_mix_norm, m_w_in, m_swa_q_norm, m_swa_k_norm, m_swa_sinks, m_mla_q_lora_norm, m_mla_w_uq, m_mla_kv_lora_norm, m_mla_w_ukv, m_mla_q_norm, m_mla_k_norm, m_w_branch_a, m_w_branch_b, m_w_out, m_ffn2_norm, m_ffn2_w_gate, m_ffn2_w_up, m_ffn2_w_down, v_ffn1_norm, v_ffn1_w_gate, v_ffn1_w_up, v_ffn1_w_down, v_mix_norm, v_w_in, v_swa_q_norm, v_swa_k_norm, v_swa_sinks, v_mla_q_lora_norm, v_mla_w_uq, v_mla_kv_lora_norm, v_mla_w_ukv, v_mla_q_norm, v_mla_k_norm, v_w_branch_a, v_w_branch_b, v_w_out, v_ffn2_norm, v_ffn2_w_gate, v_ffn2_w_up, v_ffn2_w_down)
    a = dict(zip(ARG_NAMES, args, strict=True))
    x = a["x"]
    depth = a["ffn1_norm"].shape[0]
    d = x.shape[-1]
    shard_shapes = {n: a[n].shape[1:] for n in PACK_NAMES}
    offs, slab_rows = _pack_layout(shard_shapes)
    assert depth == 2, "one layer slab per core of a chip"

    pack = jnp.stack([_pack_slab({n: a[n][l] for n in PACK_NAMES}, slab_rows, BF16) for l in range(depth)])
    shard_me = (2 * lax.axis_index("x") + lax.axis_index("y")).astype(jnp.int32)
    core_me = lax.axis_index("c").astype(jnp.int32)
    gathered = lax.dynamic_update_slice(allgather_weights(pack), pack[:, None], (0, shard_me, 0, 0))
    weights =[_layer_weights(gathered[l], offs, shard_shapes, d) for l in range(depth)]
    small = {n: a[n] for n in SMALL_NAMES}

    loss, grad_x, wgrads, sgrads = _local_step(x, a["positions"], a["loss_target"], weights, small, depth)

    gpack = jnp.stack([jnp.stack([_pack_slab(_grad_pieces(wgrads[l], s, shard_shapes, d), slab_rows, F32)
                                  for s in range(N_SHARDS)]) for l in range(depth)])
    theirs = sibling_swap(gpack)
    chip_part = add_own_slab(gpack.reshape(depth, N_SHARDS * slab_rows, PACK_COLS),
                             theirs.reshape(N_SHARDS * slab_rows, PACK_COLS), core_me.reshape(1))
    chip_part = chip_part.reshape(N_SHARDS, slab_rows, PACK_COLS)
    slots = scatter_to_owners(chip_part)
    gfull = sibling_share(sum_slots(slots, chip_part, jnp.stack([shard_me, core_me])))
    grads = {n: jnp.stack([_from_rows(gfull[l], offs[n], shard_shapes[n]) for l in range(depth)]) for n in PACK_NAMES}

    flat = jnp.concatenate([jnp.concatenate([sgrads[l][n].reshape(-1) for l in range(depth)]) for n in SMALL_NAMES] + [loss.reshape(-1)])
    n_small = flat.shape[0]
    rows = -(-n_small // (8 * LANES)) * 8
    pad = rows * LANES - n_small

    def small_pack(v):
        return jnp.pad(v, (0, pad)).reshape(rows, LANES)

    total = small_allreduce(small_pack(flat))
    w_s, m_s, v_s = (small_pack(jnp.concatenate([a[p + n].reshape(-1) for n in SMALL_NAMES] + [jnp.zeros((1,), F32)]))
                     for p in ("", "m_", "v_"))
    d_s, nm_s, nv_s = adamw("adamw_small", w_s, total, m_s, v_s)

    def small_unpack(buf):
        out, off, flat_b = {}, 0, buf.reshape(-1)
        for n in SMALL_NAMES:
            size = a[n].shape[0] * a[n].shape[1]
            out[n] = flat_b[off:off + size].reshape(a[n].shape)
            off += size
        return out

    grads.update(small_unpack(total))
    delta, new_m, new_v = small_unpack(d_s), small_unpack(nm_s), small_unpack(nv_s)
    for n in PACK_NAMES:
        shp = a[n].shape
        two_d = (shp[0] * shp[1], shp[2])
        dn, mn, vn = adamw("adamw_" + n, a[n].reshape(two_d), grads[n].reshape(two_d), a["m_" + n].reshape(two_d), a["v_" + n].reshape(two_d))
        delta[n], new_m[n], new_v[n] = dn.reshape(shp), mn.reshape(shp), vn.reshape(shp)

    loss_out = total.reshape(-1)[n_small - 1]
    return (loss_out, grad_x, *[grads[n] for n in WEIGHT_NAMES], *[delta[n] for n in WEIGHT_NAMES],
            *[new_m[n] for n in WEIGHT_NAMES], *[new_v[n] for n in WEIGHT_NAMES])
```

```python
import functools

import numpy as np
import jax
import jax.numpy as jnp
from jax import lax
from jax.experimental import pallas as pl
from jax.experimental.pallas import tpu as pltpu

F32 = jnp.float32
BF16 = jnp.bfloat16
MESH = pl.DeviceIdType.MESH

HEAD_DIM_A = 64
N_HEADS_A = 8
N_KV_HEADS_A = 2
GROUP_A = N_HEADS_A // N_KV_HEADS_A
BLOCK = 128
N_HEADS_B = 8
Q_LORA_RANK = 256
KV_LORA_RANK = 128
QK_NOPE_DIM = 64
QK_ROPE_DIM = 32
QK_DIM_B = QK_NOPE_DIM + QK_ROPE_DIM
V_DIM_B = 64
ROPE_BASE = 10000.0
WIDTH_A = N_HEADS_A * HEAD_DIM_A
WIDTH_B = N_HEADS_B * V_DIM_B
KV_WIDTH_A = N_KV_HEADS_A * HEAD_DIM_A
EPS = 1e-6
NEG = -1e30
ADAM_LR = 0.001
ADAM_B1 = 0.9
ADAM_B2 = 0.999
ADAM_EPS = 1e-08
ADAM_WD = 0.01
ADAM_STEP = 10

N_SHARDS = 4
N_DEV = 8
LANES = 128
PACK_COLS = 1024
PACK_ROW_ALIGN = 16
VMEM_LIMIT = 48 * 1024 * 1024

PACK_NAMES = ("ffn1_w_gate", "ffn1_w_up", "ffn1_w_down", "w_in", "mla_w_uq", "mla_w_ukv",
              "w_branch_a", "w_branch_b", "w_out", "ffn2_w_gate", "ffn2_w_up", "ffn2_w_down")
COL_NAMES = ("ffn1_w_gate", "ffn1_w_up", "ffn2_w_gate", "ffn2_w_up")
DOWN_NAMES = ("ffn1_w_down", "ffn2_w_down")
MISC_NAMES = ("w_in", "mla_w_uq", "mla_w_ukv", "w_branch_a", "w_branch_b")
SMALL_NAMES = ("ffn1_norm", "mix_norm", "swa_q_norm", "swa_k_norm", "swa_sinks", "mla_q_lora_norm",
               "mla_kv_lora_norm", "mla_q_norm", "mla_k_norm", "ffn2_norm")
WEIGHT_NAMES = ("ffn1_norm", "ffn1_w_gate", "ffn1_w_up", "ffn1_w_down", "mix_norm", "w_in", "swa_q_norm",
                "swa_k_norm", "swa_sinks", "mla_q_lora_norm", "mla_w_uq", "mla_kv_lora_norm", "mla_w_ukv",
                "mla_q_norm", "mla_k_norm", "w_branch_a", "w_branch_b", "w_out", "ffn2_norm", "ffn2_w_gate",
                "ffn2_w_up", "ffn2_w_down")


def _params(sem):
    return pltpu.CompilerParams(dimension_semantics=sem, vmem_limit_bytes=VMEM_LIMIT)


def _tile(n, want, align):
    if n <= want:
        return n
    t = (want // align) * align
    while t > align and n % t:
        t -= align
    assert t >= align and n % t == 0, (n, want, align)
    return t


def _mm_call(name, a, b, out_struct, grid, a_spec, b_spec, o_spec, dims, n_red, acc_shape, alpha=1.0, res=None, into=None):
    n_par = len(grid) - n_red

    def body(*refs):
        a_ref, b_ref = refs[:2]
        r_ref = refs[2] if res is not None else None
        o_ref, acc_ref = refs[-2:]
        part = lax.dot_general(a_ref[...], b_ref[...], dims, preferred_element_type=F32)

        def finish(total):
            if alpha != 1.0:
                total = total * alpha
            if r_ref is not None:
                total = r_ref[...] + total
            o_ref[...] = total.astype(o_ref.dtype)

        if n_red == 0:
            finish(part)
            return
        ids = [pl.program_id(n_par + i) for i in range(n_red)]
        first = functools.reduce(jnp.logical_and, [i == 0 for i in ids])
        last = functools.reduce(jnp.logical_and, [i == grid[n_par + k] - 1 for k, i in enumerate(ids)])

        @pl.when(first)
        def _():
            acc_ref[...] = part

        @pl.when(jnp.logical_not(first))
        def _():
            acc_ref[...] += part

        @pl.when(last)
        def _():
            finish(acc_ref[...])

    in_specs = [a_spec, b_spec] + ([o_spec] if res is not None else [])
    args = (a, b) + ((res,) if res is not None else ())
    aliases = {}
    if into is not None:
        aliases = {len(args): 0}
        in_specs.append(pl.BlockSpec(memory_space=pl.ANY))
        args = args + (into,)
    return pl.pallas_call(
        body, name=name, out_shape=out_struct, grid=grid, in_specs=in_specs, out_specs=o_spec,
        scratch_shapes=[pltpu.VMEM(acc_shape, F32)], input_output_aliases=aliases,
        compiler_params=_params(("parallel",) * n_par + ("arbitrary",) * n_red),
    )(*args)


_NN = (((1,), (0,)), ((), ()))
_NT = (((1,), (1,)), ((), ()))
_TN = (((0,), (0,)), ((), ()))


def mm_nn(name, a, b, tm=512, tn=1024, tk=1024, out_dtype=F32, alpha=1.0, res=None):
    (m, k), (_, n) = a.shape, b.shape
    tm, tn, tk = _tile(m, tm, 16), _tile(n, tn, LANES), _tile(k, tk, LANES)
    return _mm_call(name, a, b, jax.ShapeDtypeStruct((m, n), out_dtype), (m // tm, n // tn, k // tk),
                    pl.BlockSpec((tm, tk), lambda i, j, kk: (i, kk)), pl.BlockSpec((tk, tn), lambda i, j, kk: (kk, j)),
                    pl.BlockSpec((tm, tn), lambda i, j, kk: (i, j)), _NN, 1, (tm, tn), alpha, res)


def mm_nt(name, a, b, tm=512, tn=1024, tk=1024, out_dtype=F32, alpha=1.0):
    (m, n), (k, _) = a.shape, b.shape
    tm, tn, tk = _tile(m, tm, 16), _tile(k, tn, LANES), _tile(n, tk, LANES)
    return _mm_call(name, a, b, jax.ShapeDtypeStruct((m, k), out_dtype), (m // tm, k // tn, n // tk),
                    pl.BlockSpec((tm, tk), lambda i, j, kk: (i, kk)), pl.BlockSpec((tn, tk), lambda i, j, kk: (j, kk)),
                    pl.BlockSpec((tm, tn), lambda i, j, kk: (i, j)), _NT, 1, (tm, tn), alpha)


def mm_tn(name, a, b, tm=1024, tn=1024, tk=512, out_dtype=F32, alpha=1.0):
    (m, k), (_, n) = a.shape, b.shape
    tm, tn, tk = _tile(k, tm, LANES), _tile(n, tn, LANES), _tile(m, tk, 16)
    return _mm_call(name, a, b, jax.ShapeDtypeStruct((k, n), out_dtype), (k // tm, n // tn, m // tk),
                    pl.BlockSpec((tk, tm), lambda i, j, kk: (kk, i)), pl.BlockSpec((tk, tn), lambda i, j, kk: (kk, j)),
                    pl.BlockSpec((tm, tn), lambda i, j, kk: (i, j)), _TN, 1, (tm, tn), alpha)


def gmm_up(name, a, w, l, blk, tm=512):
    (m, k), (_, ns, _, n) = a.shape, w.shape
    tm = _tile(m, tm, 16)
    return _mm_call(name, a, w, jax.ShapeDtypeStruct((2, ns, m, n), F32), (2, ns, m // tm),
                    pl.BlockSpec((tm, k), lambda j, s, i: (i, 0)), pl.BlockSpec((None, None, k, n), lambda j, s, i: (l, s, blk + j, 0)),
                    pl.BlockSpec((None, None, tm, n), lambda j, s, i: (j, s, i, 0)), _NN, 0, (8, LANES))


def gmm_up_dw(name, a, dgu, into, shape, l, blk, tk=512):
    (m, k), (_, ns, _, n) = a.shape, dgu.shape
    tk = _tile(m, tk, 16)
    return _mm_call(name, a, dgu, jax.ShapeDtypeStruct(shape, F32), (2, ns, m // tk),
                    pl.BlockSpec((tk, k), lambda j, s, kk: (kk, 0)), pl.BlockSpec((None, None, tk, n), lambda j, s, kk: (j, s, kk, 0)),
                    pl.BlockSpec((None, None, k, n), lambda j, s, kk: (l, s, blk + j, 0)), _TN, 1, (k, n), into=into)


def gmm_up_dx(name, dgu, w, k, l, blk, tm=512):
    _, ns, m, n = dgu.shape
    tm = _tile(m, tm, 16)
    return _mm_call(name, dgu, w, jax.ShapeDtypeStruct((m, k), F32), (m // tm, 2, ns),
                    pl.BlockSpec((None, None, tm, n), lambda i, j, s: (j, s, i, 0)),
                    pl.BlockSpec((None, None, k, n), lambda i, j, s: (l, s, blk + j, 0)),
                    pl.BlockSpec((tm, k), lambda i, j, s: (i, 0)), _NT, 2, (tm, k))


def gmm_down(name, h, w, l, blk, res, alpha, tm=512):
    (ns, m, n), d = h.shape, w.shape[3]
    tm = _tile(m, tm, 16)
    return _mm_call(name, h, w, jax.ShapeDtypeStruct((m, d), F32), (m // tm, ns),
                    pl.BlockSpec((None, tm, n), lambda i, s: (s, i, 0)), pl.BlockSpec((None, None, n, d), lambda i, s: (l, s, blk, 0)),
                    pl.BlockSpec((tm, d), lambda i, s: (i, 0)), _NN, 1, (tm, d), alpha, res)


def gmm_down_dh(name, a, w, n, l, blk, alpha, tm=512):
    (m, d), ns = a.shape, w.shape[1]
    tm = _tile(m, tm, 16)
    return _mm_call(name, a, w, jax.ShapeDtypeStruct((ns, m, n), F32), (ns, m // tm),
                    pl.BlockSpec((tm, d), lambda s, i: (i, 0)), pl.BlockSpec((None, None, n, d), lambda s, i: (l, s, blk, 0)),
                    pl.BlockSpec((None, tm, n), lambda s, i: (s, i, 0)), _NT, 0, (8, LANES), alpha)


def gmm_down_dw(name, h, b, into, shape, l, blk, alpha, tk=512):
    (ns, m, n), d = h.shape, b.shape[1]
    tk = _tile(m, tk, 16)
    return _mm_call(name, h, b, jax.ShapeDtypeStruct(shape, F32), (ns, m // tk),
                    pl.BlockSpec((None, tk, n), lambda s, kk: (s, kk, 0)), pl.BlockSpec((tk, d), lambda s, kk: (kk, 0)),
                    pl.BlockSpec((None, None, n, d), lambda s, kk: (l, s, blk, 0)), _TN, 1, (n, d), alpha, into=into)


def gmm_rows(name, a, w, l, res, tm=512):
    (m, _), (_, ns, r, d) = a.shape, w.shape
    tm = _tile(m, tm, 16)
    return _mm_call(name, a, w, jax.ShapeDtypeStruct((m, d), F32), (m // tm, ns),
                    pl.BlockSpec((tm, r), lambda i, s: (i, s)), pl.BlockSpec((None, None, r, d), lambda i, s: (l, s, 0, 0)),
                    pl.BlockSpec((tm, d), lambda i, s: (i, 0)), _NN, 1, (tm, d), 1.0, res)


def gmm_rows_dx(name, a, w, l, tm=512):
    (m, d), (_, ns, r, _) = a.shape, w.shape
    tm = _tile(m, tm, 16)
    return _mm_call(name, a, w, jax.ShapeDtypeStruct((m, ns * r), F32), (ns, m // tm),
                    pl.BlockSpec((tm, d), lambda s, i: (i, 0)), pl.BlockSpec((None, None, r, d), lambda s, i: (l, s, 0, 0)),
                    pl.BlockSpec((tm, r), lambda s, i: (i, s)), _NT, 0, (8, LANES))


def gmm_rows_dw(name, a, b, into, shape, l, tk=512):
    (m, _), d = a.shape, b.shape[1]
    ns, r = shape[1], shape[2]
    tk = _tile(m, tk, 16)
    return _mm_call(name, a, b, jax.ShapeDtypeStruct(shape, F32), (ns, m // tk),
                    pl.BlockSpec((tk, r), lambda s, kk: (kk, s)), pl.BlockSpec((tk, d), lambda s, kk: (kk, 0)),
                    pl.BlockSpec((None, None, r, d), lambda s, kk: (l, s, 0, 0)), _TN, 1, (r, d), into=into)


def rms_fwd(name, x, gain, tm=512):
    m, d = x.shape
    tm = _tile(m, tm, 16)

    def body(x_ref, g_ref, o_ref):
        xv = x_ref[...]
        r = lax.rsqrt(jnp.mean(xv * xv, axis=-1, keepdims=True) + EPS)
        o_ref[...] = (xv * r * g_ref[...]).astype(o_ref.dtype)

    return pl.pallas_call(
        body, name=name, out_shape=jax.ShapeDtypeStruct((m, d), BF16), grid=(m // tm,),
        in_specs=[pl.BlockSpec((tm, d), lambda i: (i, 0)), pl.BlockSpec((1, d), lambda i: (0, 0))],
        out_specs=pl.BlockSpec((tm, d), lambda i: (i, 0)), compiler_params=_params(("parallel",)),
    )(x, gain)


def rms_bwd(name, x, gain, dn, dres=None, want_f32=True, want_bf16=True, tm=512):
    m, d = x.shape
    tm = _tile(m, tm, 16)
    n_out = int(want_f32) + int(want_bf16)

    def body(*refs):
        x_ref, g_ref, dn_ref = refs[:3]
        pos = 3
        r_ref = None
        if dres is not None:
            r_ref = refs[pos]
            pos += 1
        outs = refs[pos:pos + n_out]
        dg_ref = refs[pos + n_out]
        xv = x_ref[...]
        r = lax.rsqrt(jnp.mean(xv * xv, axis=-1, keepdims=True) + EPS)
        xhat = xv * r
        dnv = dn_ref[...]
        dxhat = dnv * g_ref[...]
        dx = r * (dxhat - xhat * jnp.mean(dxhat * xhat, axis=-1, keepdims=True))
        if r_ref is not None:
            dx = r_ref[...] + dx
        for o in outs:
            o[...] = dx.astype(o.dtype)
        part = jnp.sum(dnv * xhat, axis=0, keepdims=True)

        @pl.when(pl.program_id(0) == 0)
        def _():
            dg_ref[...] = part

        @pl.when(pl.program_id(0) > 0)
        def _():
            dg_ref[...] += part

    row = pl.BlockSpec((tm, d), lambda i: (i, 0))
    vec = pl.BlockSpec((1, d), lambda i: (0, 0))
    out_shape = ([jax.ShapeDtypeStruct((m, d), F32)] if want_f32 else []) + ([jax.ShapeDtypeStruct((m, d), BF16)] if want_bf16 else [])
    res = pl.pallas_call(
        body, name=name, out_shape=tuple(out_shape) + (jax.ShapeDtypeStruct((1, d), F32),), grid=(m // tm,),
        in_specs=[row, vec, row] + ([row] if dres is not None else []),
        out_specs=tuple([row] * n_out) + (vec,), compiler_params=_params(("arbitrary",)),
    )(*((x, gain, dn) + ((dres,) if dres is not None else ())))
    return res


def swiglu_fwd(name, gu, tm=512):
    _, g, m, n = gu.shape
    tm = _tile(m, tm, 16)

    def body(gu_ref, o_ref):
        gate, up = gu_ref[0], gu_ref[1]
        o_ref[...] = (gate * jax.nn.sigmoid(gate) * up).astype(o_ref.dtype)

    return pl.pallas_call(
        body, name=name, out_shape=jax.ShapeDtypeStruct((g, m, n), BF16), grid=(g, m // tm),
        in_specs=[pl.BlockSpec((2, None, tm, n), lambda gi, i: (0, gi, i, 0))],
        out_specs=pl.BlockSpec((None, tm, n), lambda gi, i: (gi, i, 0)), compiler_params=_params(("parallel", "parallel")),
    )(gu)


def swiglu_bwd(name, gu, dh, tm=512):
    _, g, m, n = gu.shape
    tm = _tile(m, tm, 16)

    def body(gu_ref, dh_ref, o_ref):
        gate, up, dhv = gu_ref[0], gu_ref[1], dh_ref[...]
        s = jax.nn.sigmoid(gate)
        o_ref[0] = (dhv * up * (s * (1.0 + gate * (1.0 - s)))).astype(o_ref.dtype)
        o_ref[1] = (dhv * (gate * s)).astype(o_ref.dtype)

    return pl.pallas_call(
        body, name=name, out_shape=jax.ShapeDtypeStruct((2, g, m, n), BF16), grid=(g, m // tm),
        in_specs=[pl.BlockSpec((2, None, tm, n), lambda gi, i: (0, gi, i, 0)), pl.BlockSpec((None, tm, n), lambda gi, i: (gi, i, 0))],
        out_specs=pl.BlockSpec((2, None, tm, n), lambda gi, i: (0, gi, i, 0)), compiler_params=_params(("parallel", "parallel")),
    )(gu, dh)


def gate_fwd(name, proj, ya, yb, off_a, off_b, tm=256):
    m, d = ya.shape
    w = proj.shape[1]
    tm = _tile(m, tm, 16)

    def body(p_ref, ya_ref, yb_ref, o_ref):
        ga, gb = p_ref[:, off_a:off_a + d], p_ref[:, off_b:off_b + d]
        o_ref[...] = (jax.nn.sigmoid(ga) * ya_ref[...] + jax.nn.sigmoid(gb) * yb_ref[...]).astype(o_ref.dtype)

    row = pl.BlockSpec((tm, d), lambda i: (i, 0))
    return pl.pallas_call(
        body, name=name, out_shape=jax.ShapeDtypeStruct((m, d), BF16), grid=(m // tm,),
        in_specs=[pl.BlockSpec((tm, w), lambda i: (i, 0)), row, row], out_specs=row, compiler_params=_params(("parallel",)),
    )(proj, ya, yb)


def gate_bwd(name, proj, ya, yb, dmerged, off_a, off_b, tm=256):
    m, d = ya.shape
    w = proj.shape[1]
    tm = _tile(m, tm, 16)

    def body(p_ref, ya_ref, yb_ref, dm_ref, dya_ref, dyb_ref, dga_ref, dgb_ref):
        sa, sb = jax.nn.sigmoid(p_ref[:, off_a:off_a + d]), jax.nn.sigmoid(p_ref[:, off_b:off_b + d])
        dm = dm_ref[...]
        dya_ref[...] = (dm * sa).astype(BF16)
        dyb_ref[...] = (dm * sb).astype(BF16)
        dga_ref[...] = (dm * ya_ref[...] * (sa * (1.0 - sa))).astype(BF16)
        dgb_ref[...] = (dm * yb_ref[...] * (sb * (1.0 - sb))).astype(BF16)

    row = pl.BlockSpec((tm, d), lambda i: (i, 0))
    o = jax.ShapeDtypeStruct((m, d), BF16)
    return pl.pallas_call(
        body, name=name, out_shape=(o, o, o, o), grid=(m // tm,),
        in_specs=[pl.BlockSpec((tm, w), lambda i: (i, 0)), row, row, row], out_specs=(row, row, row, row),
        compiler_params=_params(("parallel",)),
    )(proj, ya, yb, dmerged)


def loss_fwd_bwd(name, y, target, tm=512):
    m, d = y.shape
    tm = _tile(m, tm, 16)

    def body(y_ref, t_ref, l_ref, dy_ref, dyb_ref):
        err = y_ref[...] - t_ref[...]
        dy = err * (1.0 / d)
        dy_ref[...] = dy
        dyb_ref[...] = dy.astype(BF16)
        part = 0.5 * jnp.sum(jnp.mean(err * err, axis=-1, keepdims=True), axis=0, keepdims=True)

        @pl.when(pl.program_id(0) == 0)
        def _():
            l_ref[...] = part

        @pl.when(pl.program_id(0) > 0)
        def _():
            l_ref[...] += part

    row = pl.BlockSpec((tm, d), lambda i: (i, 0))
    return pl.pallas_call(
        body, name=name, grid=(m // tm,),
        out_shape=(jax.ShapeDtypeStruct((1, 1), F32), jax.ShapeDtypeStruct((m, d), F32), jax.ShapeDtypeStruct((m, d), BF16)),
        in_specs=[row, row], out_specs=(pl.BlockSpec((1, 1), lambda i: (0, 0)), row, row),
        compiler_params=_params(("arbitrary",)),
    )(y, target)


def adamw(name, w, g, m, v):
    r, c = w.shape
    tr = _tile(r, max(8, (2 * 1024 * 1024) // (4 * c) // 8 * 8), 8)
    c1 = np.float32(1.0 - ADAM_B1 ** ADAM_STEP)
    c2 = np.float32(1.0 - ADAM_B2 ** ADAM_STEP)

    def body(w_ref, g_ref, m_ref, v_ref, d_ref, nm_ref, nv_ref):
        gv = g_ref[...]
        nm = ADAM_B1 * m_ref[...] + (1.0 - ADAM_B1) * gv
        nv = ADAM_B2 * v_ref[...] + (1.0 - ADAM_B2) * (gv * gv)
        d_ref[...] = -ADAM_LR * ((nm / c1) / (jnp.sqrt(nv / c2) + ADAM_EPS) + ADAM_WD * w_ref[...])
        nm_ref[...] = nm
        nv_ref[...] = nv

    row = pl.BlockSpec((tr, c), lambda i: (i, 0))
    o = jax.ShapeDtypeStruct((r, c), F32)
    return pl.pallas_call(
        body, name=name, out_shape=(o, o, o), grid=(r // tr,), in_specs=[row] * 4, out_specs=(row, row, row),
        compiler_params=_params(("parallel",)),
    )(w, g, m, v)


def _rope_rot(d):
    r = np.zeros((d, d), np.float32)
    half = QK_ROPE_DIM // 2
    base = d - QK_ROPE_DIM
    for j in range(half):
        r[base + half + j, base + j] = -1.0
        r[base + j, base + half + j] = 1.0
    return r


def qk_prep_fwd(name, x, gain, rope=None):
    b, h, s, d = x.shape

    def body(*refs):
        if rope is None:
            x_ref, g_ref, o_ref = refs
        else:
            x_ref, g_ref, c_ref, s_ref, r_ref, o_ref = refs
        xv = x_ref[...]
        y = xv * lax.rsqrt(jnp.mean(xv * xv, axis=-1, keepdims=True) + EPS) * g_ref[...]
        if rope is not None:
            rot = jnp.dot(y, r_ref[...], precision=lax.Precision.HIGHEST, preferred_element_type=F32)
            y = y * c_ref[...] + rot * s_ref[...]
        o_ref[...] = y.astype(o_ref.dtype)

    xs = pl.BlockSpec((None, None, s, d), lambda bi, hi: (bi, hi, 0, 0))
    in_specs = [xs, pl.BlockSpec((1, d), lambda bi, hi: (0, 0))]
    args = [x, gain]
    if rope is not None:
        tab = pl.BlockSpec((None, s, d), lambda bi, hi: (bi, 0, 0))
        in_specs += [tab, tab, pl.BlockSpec((d, d), lambda bi, hi: (0, 0))]
        args += [rope[0], rope[1], jnp.asarray(_rope_rot(d))]
    return pl.pallas_call(
        body, name=name, out_shape=jax.ShapeDtypeStruct(x.shape, BF16), grid=(b, h), in_specs=in_specs, out_specs=xs,
        compiler_params=_params(("parallel", "parallel")),
    )(*args)


def qk_prep_bwd(name, x, gain, dxh, rope=None, head_sum=False):
    b, h, s, d = x.shape

    def body(*refs):
        if rope is None:
            x_ref, g_ref, dy_ref = refs[:3]
            outs = refs[3:]
        else:
            x_ref, g_ref, dy_ref, c_ref, s_ref, rt_ref = refs[:6]
            outs = refs[6:]
        dx_ref, dg_ref = outs[0], outs[-1]
        dy = dy_ref[...]
        if rope is not None:
            dy = dy * c_ref[...] + jnp.dot(dy * s_ref[...], rt_ref[...], precision=lax.Precision.HIGHEST, preferred_element_type=F32)
        xv = x_ref[...]
        r = lax.rsqrt(jnp.mean(xv * xv, axis=-1, keepdims=True) + EPS)
        xhat = xv * r
        dxhat = dy * g_ref[...]
        dx = r * (dxhat - xhat * jnp.mean(dxhat * xhat, axis=-1, keepdims=True))
        dx_ref[...] = dx.astype(dx_ref.dtype)
        part = jnp.sum(dy * xhat, axis=0, keepdims=True)
        first = jnp.logical_and(pl.program_id(0) == 0, pl.program_id(1) == 0)

        @pl.when(first)
        def _():
            dg_ref[...] = part

        @pl.when(jnp.logical_not(first))
        def _():
            dg_ref[...] += part

        if head_sum:
            hs_ref = outs[1]

            @pl.when(pl.program_id(1) == 0)
            def _():
                hs_ref[...] = dx

            @pl.when(pl.program_id(1) > 0)
            def _():
                hs_ref[...] += dx

    xs = pl.BlockSpec((None, None, s, d), lambda bi, hi: (bi, hi, 0, 0))
    vec = pl.BlockSpec((1, d), lambda bi, hi: (0, 0))
    tab = pl.BlockSpec((None, s, d), lambda bi, hi: (bi, 0, 0))
    in_specs = [xs, vec, xs]
    args = [x, gain, dxh]
    if rope is not None:
        in_specs += [tab, tab, pl.BlockSpec((d, d), lambda bi, hi: (0, 0))]
        args += [rope[0], rope[1], jnp.asarray(_rope_rot(d).T.copy())]
    out_shape = [jax.ShapeDtypeStruct(x.shape, BF16)]
    out_specs = [xs]
    if head_sum:
        out_shape.append(jax.ShapeDtypeStruct((b, s, d), F32))
        out_specs.append(tab)
    out_shape.append(jax.ShapeDtypeStruct((1, d), F32))
    out_specs.append(vec)
    return pl.pallas_call(
        body, name=name, out_shape=tuple(out_shape), grid=(b, h), in_specs=in_specs, out_specs=tuple(out_specs),
        compiler_params=_params(("arbitrary", "arbitrary")),
    )(*args)


def _swa_specs(s):
    nb = s // BLOCK
    q = pl.BlockSpec((None, GROUP_A, BLOCK, HEAD_DIM_A), lambda b, kv, n: (b, kv, n, 0))
    kprev = pl.BlockSpec((None, None, BLOCK, HEAD_DIM_A), lambda b, kv, n: (b, kv, jnp.maximum(n - 1, 0), 0))
    kcur = pl.BlockSpec((None, None, BLOCK, HEAD_DIM_A), lambda b, kv, n: (b, kv, n, 0))
    pcol = pl.BlockSpec((None, BLOCK, 1), lambda b, kv, n: (b, n, 0))
    prow_prev = pl.BlockSpec((None, 1, BLOCK), lambda b, kv, n: (b, 0, jnp.maximum(n - 1, 0)))
    prow_cur = pl.BlockSpec((None, 1, BLOCK), lambda b, kv, n: (b, 0, n))
    smem = pl.BlockSpec(memory_space=pltpu.SMEM)
    return nb, q, kprev, kcur, pcol, prow_prev, prow_cur, smem


def _swa_probs(q, kk, dist, valid, slope, sink):
    sc = lax.dot_general(q, kk, _NT, preferred_element_type=F32) * (HEAD_DIM_A ** -0.5)
    sc = sc - slope * dist
    sc = jnp.where(valid, sc, NEG)
    m = jnp.maximum(jnp.max(sc, axis=-1, keepdims=True), sink)
    e = jnp.exp(sc - m)
    es = jnp.exp(sink - m)
    inv = 1.0 / (jnp.sum(e, axis=-1, keepdims=True) + es)
    return e * inv, es * inv


def _swa_window(n, kp_ref, kc_ref, vp_ref, vc_ref, pc_ref, prp_ref, prc_ref):
    kk = jnp.concatenate([kp_ref[...], kc_ref[...]], axis=0)
    vv = jnp.concatenate([vp_ref[...], vc_ref[...]], axis=0)
    dist = pc_ref[...] - jnp.concatenate([prp_ref[...], prc_ref[...]], axis=1)
    qi = lax.broadcasted_iota(jnp.int32, (BLOCK, 2 * BLOCK), 0) + BLOCK
    ki = lax.broadcasted_iota(jnp.int32, (BLOCK, 2 * BLOCK), 1)
    diff = qi - ki
    valid = (diff >= 0) & (diff < BLOCK) & ((n > 0) | (ki >= BLOCK))
    return kk, vv, dist, valid


def swa_fwd(name, q, k, v, pos_col, pos_row, slopes, sinks):
    b, _, s, _ = q.shape
    nb, qs, kprev, kcur, pcol, prp, prc, smem = _swa_specs(s)

    def body(q_ref, kp_ref, kc_ref, vp_ref, vc_ref, pc_ref, prp_ref, prc_ref, sl_ref, sk_ref, o_ref):
        kv, n = pl.program_id(1), pl.program_id(2)
        kk, vv, dist, valid = _swa_window(n, kp_ref, kc_ref, vp_ref, vc_ref, pc_ref, prp_ref, prc_ref)
        for g in range(GROUP_A):
            hd = kv * GROUP_A + g
            p, _ = _swa_probs(q_ref[g], kk, dist, valid, sl_ref[hd], sk_ref[hd])
            o_ref[g] = jnp.dot(p.astype(BF16), vv, preferred_element_type=F32)

    return pl.pallas_call(
        body, name=name, out_shape=jax.ShapeDtypeStruct(q.shape, F32), grid=(b, N_KV_HEADS_A, nb),
        in_specs=[qs, kprev, kcur, kprev, kcur, pcol, prp, prc, smem, smem], out_specs=qs,
        compiler_params=_params(("parallel", "parallel", "parallel")),
    )(q, k, k, v, v, pos_col, pos_row, pos_row, slopes, sinks)


def swa_bwd(name, q, k, v, pos_col, pos_row, slopes, sinks, do):
    b, _, s, _ = q.shape
    nb, qs, kprev, kcur, pcol, prp, prc, smem = _swa_specs(s)

    def body(q_ref, kp_ref, kc_ref, vp_ref, vc_ref, pc_ref, prp_ref, prc_ref, sl_ref, sk_ref, do_ref, dq_ref, dk_ref, dv_ref, ds_ref):
        bi, kv, n = pl.program_id(0), pl.program_id(1), pl.program_id(2)
        kk, vv, dist, valid = _swa_window(n, kp_ref, kc_ref, vp_ref, vc_ref, pc_ref, prp_ref, prc_ref)

        @pl.when((bi == 0) & (kv == 0) & (n == 0))
        def _():
            ds_ref[...] = jnp.zeros_like(ds_ref)

        @pl.when(n == 0)
        def _():
            dk_ref[...] = jnp.zeros_like(dk_ref)
            dv_ref[...] = jnp.zeros_like(dv_ref)

        dkk = jnp.zeros((2 * BLOCK, HEAD_DIM_A), F32)
        dvv = jnp.zeros((2 * BLOCK, HEAD_DIM_A), F32)
        head_row = lax.broadcasted_iota(jnp.int32, (N_HEADS_A, LANES), 0)
        dsink = jnp.zeros((N_HEADS_A, LANES), F32)
        for g in range(GROUP_A):
            hd = kv * GROUP_A + g
            qg = q_ref[g]
            p, ps = _swa_probs(qg, kk, dist, valid, sl_ref[hd], sk_ref[hd])
            dob = do_ref[g].astype(BF16)
            dvv = dvv + lax.dot_general(p.astype(BF16), dob, _TN, preferred_element_type=F32)
            dp = lax.dot_general(dob, vv, _NT, preferred_element_type=F32)
            rs = jnp.sum(p * dp, axis=-1, keepdims=True)
            dsb = (p * (dp - rs) * (HEAD_DIM_A ** -0.5)).astype(BF16)
            dq_ref[g] = jnp.dot(dsb, kk, preferred_element_type=F32)
            dkk = dkk + lax.dot_general(dsb, qg, _TN, preferred_element_type=F32)
            dsink = dsink + jnp.where(head_row == hd, -jnp.sum(ps * rs), 0.0)
        ds_ref[...] += dsink

        @pl.when(n > 0)
        def _():
            start = pl.multiple_of((n - 1) * BLOCK, BLOCK)
            dk_ref[pl.ds(start, 2 * BLOCK), :] += dkk
            dv_ref[pl.ds(start, 2 * BLOCK), :] += dvv

        @pl.when(n == 0)
        def _():
            dk_ref[0:BLOCK, :] += dkk[BLOCK:, :]
            dv_ref[0:BLOCK, :] += dvv[BLOCK:, :]

    kv_full = pl.BlockSpec((None, None, s, HEAD_DIM_A), lambda bi, kv, n: (bi, kv, 0, 0))
    return pl.pallas_call(
        body, name=name, grid=(b, N_KV_HEADS_A, nb),
        out_shape=(jax.ShapeDtypeStruct(q.shape, F32), jax.ShapeDtypeStruct(k.shape, F32), jax.ShapeDtypeStruct(k.shape, F32),
                   jax.ShapeDtypeStruct((N_HEADS_A, LANES), F32)),
        in_specs=[qs, kprev, kcur, kprev, kcur, pcol, prp, prc, smem, smem, qs],
        out_specs=(qs, kv_full, kv_full, pl.BlockSpec((N_HEADS_A, LANES), lambda bi, kv, n: (0, 0))),
        compiler_params=_params(("arbitrary", "arbitrary", "arbitrary")),
    )(q, k, k, v, v, pos_col, pos_row, pos_row, slopes, sinks, do)


MLA_T = 256


def _blocks_t(z, t):
    b, h, s, d = z.shape
    return z.reshape(b, h, s // t, t, d).transpose(0, 1, 2, 4, 3)


def _unblocks_t(z):
    b, h, nb, d, t = z.shape
    return z.transpose(0, 1, 2, 4, 3).reshape(b, h, nb * t, d)


def mla_fwd(name, q, k, vt):
    b, h, s, dq = q.shape
    _, _, nb, dv, t = vt.shape
    scale = QK_DIM_B ** -0.5

    def body(q_ref, k_ref, vt_ref, o_ref, lse_ref):
        causal = lax.broadcasted_iota(jnp.int32, (t, t), 0) <= lax.broadcasted_iota(jnp.int32, (t, t), 1)
        for i in range(nb):
            q_i = q_ref[i * t:(i + 1) * t, :]

            def step(j, carry, q_i=q_i, diagonal=False):
                m, l, acc = carry
                rows = slice(j * t, (j + 1) * t) if diagonal else pl.ds(pl.multiple_of(j * t, t), t)
                st = lax.dot_general(k_ref[rows, :], q_i, _NT, preferred_element_type=F32) * scale
                if diagonal:
                    st = jnp.where(causal, st, NEG)
                m_new = jnp.maximum(m, jnp.max(st, axis=0, keepdims=True))
                a = jnp.exp(m - m_new)
                p = jnp.exp(st - m_new)
                l = a * l + jnp.sum(p, axis=0, keepdims=True)
                acc = a * acc + jnp.dot(vt_ref[j], p.astype(BF16), preferred_element_type=F32)
                return m_new, l, acc

            carry = (jnp.full((1, t), NEG, F32), jnp.zeros((1, t), F32), jnp.zeros((dv, t), F32))
            if i > 0:
                carry = lax.fori_loop(0, i, step, carry)
            m, l, acc = step(i, carry, diagonal=True)
            o_ref[i] = acc / l
            lse_ref[i] = m + jnp.log(l)

    qs = pl.BlockSpec((None, None, s, dq), lambda bi, hi: (bi, hi, 0, 0))
    ts = pl.BlockSpec((None, None, nb, dv, t), lambda bi, hi: (bi, hi, 0, 0, 0))
    ls = pl.BlockSpec((None, None, nb, 1, t), lambda bi, hi: (bi, hi, 0, 0, 0))
    return pl.pallas_call(
        body, name=name, grid=(b, h),
        out_shape=(jax.ShapeDtypeStruct((b, h, nb, dv, t), F32), jax.ShapeDtypeStruct((b, h, nb, 1, t), F32)),
        in_specs=[qs, qs, ts], out_specs=(ts, ls), compiler_params=_params(("parallel", "parallel")),
    )(q, k, vt)


def mla_bwd(name, q, k, kt, v, do, dot, ot, lse):
    b, h, s, dq = q.shape
    _, _, nb, dv, t = ot.shape
    scale = QK_DIM_B ** -0.5

    def body(q_ref, k_ref, kt_ref, v_ref, do_ref, dot_ref, ot_ref, lse_ref, dqt_ref, dk_ref, dv_ref):
        causal = lax.broadcasted_iota(jnp.int32, (t, t), 0) <= lax.broadcasted_iota(jnp.int32, (t, t), 1)
        dk_ref[...] = jnp.zeros_like(dk_ref)
        dv_ref[...] = jnp.zeros_like(dv_ref)
        for i in range(nb):
            q_i = q_ref[i * t:(i + 1) * t, :]
            do_i = do_ref[i * t:(i + 1) * t, :]
            dot_i = dot_ref[i]
            delta = jnp.sum(ot_ref[i] * dot_i, axis=0, keepdims=True)
            dot_b = dot_i.astype(BF16)
            lse_i = lse_ref[i]

            def step(j, dqt, q_i=q_i, do_i=do_i, delta=delta, dot_b=dot_b, lse_i=lse_i, diagonal=False):
                rows = slice(j * t, (j + 1) * t) if diagonal else pl.ds(pl.multiple_of(j * t, t), t)
                st = lax.dot_general(k_ref[rows, :], q_i, _NT, preferred_element_type=F32) * scale
                if diagonal:
                    st = jnp.where(causal, st, NEG)
                pt = jnp.exp(st - lse_i)
                dpt = jnp.dot(v_ref[rows, :], dot_b, preferred_element_type=F32)
                dst = (pt * (dpt - delta) * scale).astype(BF16)
                dv_ref[rows, :] += jnp.dot(pt.astype(BF16), do_i, preferred_element_type=F32)
                dk_ref[rows, :] += jnp.dot(dst, q_i, preferred_element_type=F32)
                return dqt + jnp.dot(kt_ref[j], dst, preferred_element_type=F32)

            dqt = jnp.zeros((dq, t), F32)
            if i > 0:
                dqt = lax.fori_loop(0, i, step, dqt)
            dqt_ref[i] = step(i, dqt, diagonal=True)

    def nat(d):
        return pl.BlockSpec((None, None, s, d), lambda bi, hi: (bi, hi, 0, 0))

    def blk(d):
        return pl.BlockSpec((None, None, nb, d, t), lambda bi, hi: (bi, hi, 0, 0, 0))

    return pl.pallas_call(
        body, name=name, grid=(b, h),
        out_shape=(jax.ShapeDtypeStruct((b, h, nb, dq, t), F32), jax.ShapeDtypeStruct((b, h, s, dq), F32),
                   jax.ShapeDtypeStruct((b, h, s, dv), F32)),
        in_specs=[nat(dq), nat(dq), blk(dq), nat(dv), nat(dv), blk(dv), blk(dv), blk(1)],
        out_specs=(blk(dq), nat(dq), nat(dv)), compiler_params=_params(("parallel", "parallel")),
    )(q, k, kt, v, do, dot, ot, lse)


_HBM = pl.BlockSpec(memory_space=pltpu.HBM)


def _place():
    x, y, c = lax.axis_index("x"), lax.axis_index("y"), lax.axis_index("c")
    chips = [(1 - x, y), (x, 1 - y), (1 - x, 1 - y)]
    return x, y, c, chips


def _remote(src, dst, send_sem, recv_sem, dev):
    return pltpu.make_async_remote_copy(src_ref=src, dst_ref=dst, send_sem=send_sem, recv_sem=recv_sem,
                                        device_id=dev, device_id_type=MESH)


def allgather_weights(packs):
    nk = len(packs)

    def body(*refs):
        p_refs, o_refs = refs[:nk], refs[nk:2 * nk]
        send_sems, recv_sems = refs[2 * nk:]
        x, y, c, chips = _place()
        s_me = 2 * x + y
        sibling = (x, y, 1 - c)
        first, passed, own = [], [], []
        for k in range(nk):
            for j, (cx, cy) in enumerate(chips):
                first.append(_remote(p_refs[k].at[c], o_refs[k].at[c, s_me], send_sems.at[j * nk + k], recv_sems.at[j * nk + k], (cx, cy, c)))
            for l in range(2):
                own.append(_remote(p_refs[k].at[l], o_refs[k].at[l, s_me], send_sems.at[6 * nk + 2 * k + l],
                                   recv_sems.at[6 * nk + 2 * k + l], sibling))
        for cp in first + own:
            cp.start()
        for j, (cx, cy) in enumerate(chips):
            for k in range(nk):
                blk = o_refs[k].at[c, 2 * cx + cy]
                _remote(blk, blk, send_sems.at[j * nk + k], recv_sems.at[j * nk + k], (cx, cy, c)).wait_recv()
                fwd = _remote(blk, blk, send_sems.at[(3 + j) * nk + k], recv_sems.at[(3 + j) * nk + k], sibling)
                fwd.start()
                passed.append(fwd)
        for j, (cx, cy) in enumerate(chips):
            for k in range(nk):
                blk = o_refs[k].at[1 - c, 2 * cx + cy]
                _remote(blk, blk, send_sems.at[(3 + j) * nk + k], recv_sems.at[(3 + j) * nk + k], sibling).wait_recv()
        for k in range(nk):
            for l in range(2):
                blk = o_refs[k].at[l, s_me]
                _remote(blk, blk, send_sems.at[6 * nk + 2 * k + l], recv_sems.at[6 * nk + 2 * k + l], sibling).wait_recv()
        for cp in first + own + passed:
            cp.wait_send()

    return pl.pallas_call(
        body, name="allgather_weights",
        out_shape=tuple(jax.ShapeDtypeStruct((2, N_SHARDS) + p.shape[1:], p.dtype) for p in packs),
        in_specs=[_HBM] * nk, out_specs=tuple([_HBM] * nk),
        scratch_shapes=[pltpu.SemaphoreType.DMA((8 * nk,)), pltpu.SemaphoreType.DMA((8 * nk,))],
    )(*packs)


def sibling_swap(grads):
    nk = len(grads)

    def body(*refs):
        g_refs, o_refs = refs[:nk], refs[nk:2 * nk]
        send_sems, recv_sems = refs[2 * nk:]
        x, y, c, _ = _place()
        cps = [_remote(g_refs[k].at[1 - c], o_refs[k], send_sems.at[k], recv_sems.at[k], (x, y, 1 - c)) for k in range(nk)]
        for cp in cps:
            cp.start()
        for cp in cps:
            cp.wait()

    return pl.pallas_call(
        body, name="sibling_swap", out_shape=tuple(jax.ShapeDtypeStruct(g.shape[1:], g.dtype) for g in grads),
        in_specs=[_HBM] * nk, out_specs=tuple([_HBM] * nk),
        scratch_shapes=[pltpu.SemaphoreType.DMA((nk,)), pltpu.SemaphoreType.DMA((nk,))],
    )(*grads)


def add_own_slab(name, grads, other, c_idx, tr=256):
    _, n, cols = grads.shape
    tr = _tile(n, tr, 16)

    def body(c_ref, g_ref, o_ref, out_ref):
        out_ref[...] = (g_ref[...] + o_ref[...]).astype(out_ref.dtype)

    return pl.pallas_call(
        body, name=name, out_shape=jax.ShapeDtypeStruct((n, cols), BF16),
        grid_spec=pltpu.PrefetchScalarGridSpec(
            num_scalar_prefetch=1, grid=(n // tr,),
            in_specs=[pl.BlockSpec((None, tr, cols), lambda i, c_ref: (c_ref[0], i, 0)), pl.BlockSpec((tr, cols), lambda i, c_ref: (i, 0))],
            out_specs=pl.BlockSpec((tr, cols), lambda i, c_ref: (i, 0))),
        compiler_params=_params(("parallel",)),
    )(c_idx, grads, other)


def scatter_to_owners(parts):
    nk = len(parts)

    def body(*refs):
        p_refs, o_refs = refs[:nk], refs[nk:2 * nk]
        send_sems, recv_sems = refs[2 * nk:]
        x, y, c, chips = _place()
        s_me = 2 * x + y
        sends = [_remote(p_refs[k].at[2 * cx + cy], o_refs[k].at[s_me], send_sems.at[j * nk + k], recv_sems.at[j * nk + k], (cx, cy, c))
                 for j, (cx, cy) in enumerate(chips) for k in range(nk)]
        for cp in sends:
            cp.start()
        for j, (cx, cy) in enumerate(chips):
            for k in range(nk):
                slot = o_refs[k].at[2 * cx + cy]
                _remote(slot, slot, send_sems.at[j * nk + k], recv_sems.at[j * nk + k], (cx, cy, c)).wait_recv()
        for cp in sends:
            cp.wait_send()

    return pl.pallas_call(
        body, name="scatter_to_owners", out_shape=tuple(jax.ShapeDtypeStruct(p.shape, p.dtype) for p in parts),
        in_specs=[_HBM] * nk, out_specs=tuple([_HBM] * nk),
        scratch_shapes=[pltpu.SemaphoreType.DMA((3 * nk,)), pltpu.SemaphoreType.DMA((3 * nk,))],
    )(*parts)


def sum_slots(name, slots, part, place, tr=256):
    ns, r, cols = slots.shape
    tr = _tile(r, tr, 16)

    def body(pl_ref, s_ref, own_ref, o_ref):
        total = own_ref[...].astype(F32)
        for k in range(1, ns):
            total = total + s_ref[(pl_ref[0] + k) % ns].astype(F32)
        o_ref[...] = total

    return pl.pallas_call(
        body, name=name, out_shape=jax.ShapeDtypeStruct((2, r, cols), F32),
        grid_spec=pltpu.PrefetchScalarGridSpec(
            num_scalar_prefetch=1, grid=(r // tr,),
            in_specs=[pl.BlockSpec((ns, tr, cols), lambda i, p: (0, i, 0)), pl.BlockSpec((None, tr, cols), lambda i, p: (p[0], i, 0))],
            out_specs=pl.BlockSpec((None, tr, cols), lambda i, p: (p[1], i, 0))),
        compiler_params=_params(("parallel",)),
    )(place, slots, part)


def sibling_share(boths):
    nk = len(boths)

    def body(*refs):
        b_refs, o_refs = refs[:nk], refs[nk:2 * nk]
        send_sems, recv_sems = refs[2 * nk:]
        x, y, c, _ = _place()
        cps = [_remote(b_refs[k].at[c], o_refs[k].at[c], send_sems.at[k], recv_sems.at[k], (x, y, 1 - c)) for k in range(nk)]
        for cp in cps:
            cp.start()
        for k in range(nk):
            theirs = o_refs[k].at[1 - c]
            _remote(theirs, theirs, send_sems.at[k], recv_sems.at[k], (x, y, 1 - c)).wait_recv()
        for cp in cps:
            cp.wait_send()

    return pl.pallas_call(
        body, name="sibling_share", out_shape=tuple(jax.ShapeDtypeStruct(b.shape, b.dtype) for b in boths),
        in_specs=[_HBM] * nk, out_specs=tuple([_HBM] * nk), input_output_aliases={k: k for k in range(nk)},
        scratch_shapes=[pltpu.SemaphoreType.DMA((nk,)), pltpu.SemaphoreType.DMA((nk,))],
    )(*boths)


def small_allreduce(vec):
    r, cols = vec.shape

    def body(v_ref, o_ref, buf, send_sems, recv_sems):
        x, y, c, _ = _place()
        me = 4 * x + 2 * y + c
        buf[me] = v_ref[...]
        peers = []
        for k in range(1, N_DEV):
            px = (1 - x) if (k & 4) else x
            py = (1 - y) if (k & 2) else y
            pc = (1 - c) if (k & 1) else c
            peers.append((px, py, pc))
        sends = [_remote(buf.at[me], buf.at[me], send_sems.at[k], recv_sems.at[k], peer) for k, peer in enumerate(peers)]
        for cp in sends:
            cp.start()
        for k, (px, py, pc) in enumerate(peers):
            slot = buf.at[4 * px + 2 * py + pc]
            _remote(slot, slot, send_sems.at[k], recv_sems.at[k], (px, py, pc)).wait_recv()
        for cp in sends:
            cp.wait_send()
        total = buf[0]
        for k in range(1, N_DEV):
            total = total + buf[k]
        o_ref[...] = total

    return pl.pallas_call(
        body, name="small_allreduce", out_shape=jax.ShapeDtypeStruct((r, cols), F32),
        in_specs=[pl.BlockSpec(memory_space=pltpu.VMEM)], out_specs=pl.BlockSpec(memory_space=pltpu.VMEM),
        scratch_shapes=[pltpu.VMEM((N_DEV, r, cols), F32), pltpu.SemaphoreType.DMA((N_DEV - 1,)), pltpu.SemaphoreType.DMA((N_DEV - 1,))],
    )(vec)


ARG_NAMES = (("x", "positions") + WEIGHT_NAMES + ("loss_target",) + tuple("m_" + n for n in WEIGHT_NAMES)
             + tuple("v_" + n for n in WEIGHT_NAMES))

_O_QA, _O_KA, _O_VA = 0, WIDTH_A, WIDTH_A + KV_WIDTH_A
_O_CQ = _O_VA + KV_WIDTH_A
_O_CKV = _O_CQ + Q_LORA_RANK
_O_GA = _O_CKV + KV_LORA_RANK


def _rows_of(n_elems):
    return -(-n_elems // PACK_COLS)


def _pack_layout(shard_shapes):
    offs, row = {}, 0
    for name in MISC_NAMES:
        offs[name] = row
        row += _rows_of(shard_shapes[name][0] * shard_shapes[name][1])
    return offs, -(-row // PACK_ROW_ALIGN) * PACK_ROW_ALIGN


def _to_rows(a):
    flat = a.reshape(-1)
    rows = _rows_of(flat.shape[0])
    if rows * PACK_COLS != flat.shape[0]:
        flat = jnp.pad(flat, (0, rows * PACK_COLS - flat.shape[0]))
    return flat.reshape(rows, PACK_COLS)


def _from_rows(slab, off, shape):
    rows = _rows_of(shape[0] * shape[1])
    return slab[off:off + rows].reshape(-1)[:shape[0] * shape[1]].reshape(shape)


def _pack_slab(pieces, rows_total, dtype):
    parts = [_to_rows(pieces[name].astype(dtype)) for name in MISC_NAMES]
    used = sum(p.shape[0] for p in parts)
    if used < rows_total:
        parts.append(jnp.zeros((rows_total - used, PACK_COLS), dtype))
    return jnp.concatenate(parts, axis=0)


def _heads(z, b, s, h, d):
    return z.reshape(b, s, h, d).transpose(0, 2, 1, 3)


def _unheads(z):
    b, h, s, d = z.shape
    return z.transpose(0, 2, 1, 3).reshape(b * s, h * d)


def _ffn_fwd(tag, x, gain, wts, l, f):
    nb = rms_fwd(tag + "_rms", x, gain)
    gu = gmm_up(tag + "_up", nb, wts["col"], l, 2 * f)
    h = swiglu_fwd(tag + "_act", gu)
    out = gmm_down(tag + "_down", h, wts["down"], l, f, x, 0.5)
    return out, (x, nb, gu, h)


def _ffn_bwd(tag, saved, gain, wts, gbuf, l, f, dout, doutb):
    x, nb, gu, h = saved
    n = gu.shape[-1]
    dh = gmm_down_dh(tag + "_dh", doutb, wts["down"], n, l, f, 0.5)
    gbuf["down"] = gmm_down_dw(tag + "_dwd", h, doutb, gbuf["down"], wts["down"].shape, l, f, 0.5)
    dgu = swiglu_bwd(tag + "_dact", gu, dh)
    gbuf["col"] = gmm_up_dw(tag + "_dwgu", nb, dgu, gbuf["col"], wts["col"].shape, l, 2 * f)
    dn = gmm_up_dx(tag + "_dn", dgu, wts["col"], x.shape[1], l, 2 * f)
    dx, dxb, dgain = rms_bwd(tag + "_drms", x, gain, dn, dres=dout)
    return dx, dxb, dgain


def _mixer_fwd(tag, x1, sm, w, wts, l, aux):
    b, s = aux["b"], aux["s"]
    d = x1.shape[1]
    o_gb, o_kr = _O_GA + d, _O_GA + 2 * d
    hb = rms_fwd(tag + "_rms", x1, sm["mix_norm"])
    proj = mm_nn(tag + "_proj", hb, w["win"], tm=512, tn=1664, tk=1024)
    qa_raw = _heads(proj[:, _O_QA:_O_KA], b, s, N_HEADS_A, HEAD_DIM_A)
    ka_raw = _heads(proj[:, _O_KA:_O_VA], b, s, N_KV_HEADS_A, HEAD_DIM_A)
    va = _heads(proj[:, _O_VA:_O_CQ], b, s, N_KV_HEADS_A, HEAD_DIM_A).astype(BF16)
    cq, ckv, kr = proj[:, _O_CQ:_O_CKV], proj[:, _O_CKV:_O_GA], proj[:, o_kr:o_kr + QK_ROPE_DIM]
    cqn = rms_fwd(tag + "_rms_cq", cq, sm["mla_q_lora_norm"], tm=1024)
    ckvn = rms_fwd(tag + "_rms_ckv", ckv, sm["mla_kv_lora_norm"], tm=1024)
    qb_raw = _heads(mm_nn(tag + "_uq", cqn, w["wuq"], tm=1024), b, s, N_HEADS_B, QK_DIM_B)
    kv = mm_nn(tag + "_ukv", ckvn, w["wukv"], tm=1024).reshape(b, s, N_HEADS_B, QK_NOPE_DIM + V_DIM_B).transpose(0, 2, 1, 3)
    vb = kv[..., QK_NOPE_DIM:].astype(BF16)
    kr_b = jnp.broadcast_to(kr.reshape(b, 1, s, QK_ROPE_DIM), (b, N_HEADS_B, s, QK_ROPE_DIM))
    kfull_raw = jnp.concatenate([kv[..., :QK_NOPE_DIM], kr_b], axis=-1)
    qah = qk_prep_fwd(tag + "_qa_norm", qa_raw, sm["swa_q_norm"])
    kah = qk_prep_fwd(tag + "_ka_norm", ka_raw, sm["swa_k_norm"])
    oa = swa_fwd(tag + "_swa", qah, kah, va, aux["pos_col"], aux["pos_row"], aux["slopes"], sm["swa_sinks"].reshape(-1))
    qbh = qk_prep_fwd(tag + "_qb_norm", qb_raw, sm["mla_q_norm"], aux["rope"])
    kbh = qk_prep_fwd(tag + "_kb_norm", kfull_raw, sm["mla_k_norm"], aux["rope"])
    mla_t = MLA_T if s % MLA_T == 0 else BLOCK
    obt, lse = mla_fwd(tag + "_mla", qbh, kbh, _blocks_t(vb, mla_t))
    oab = _unheads(oa).astype(BF16)
    obb = obt.transpose(0, 2, 4, 1, 3).reshape(b * s, WIDTH_B).astype(BF16)
    ya = mm_nn(tag + "_branch_a", oab, w["wa"], tm=1024)
    yb = mm_nn(tag + "_branch_b", obb, w["wb"], tm=1024)
    mg = gate_fwd(tag + "_gate", proj, ya, yb, _O_GA, o_gb)
    x2 = gmm_rows(tag + "_out", mg, wts["wo"], l, x1)
    saved = dict(x1=x1, hb=hb, proj=proj, qa_raw=qa_raw, ka_raw=ka_raw, va=va, cq=cq, ckv=ckv, cqn=cqn, ckvn=ckvn,
                 qb_raw=qb_raw, kfull_raw=kfull_raw, vb=vb, qah=qah, kah=kah, qbh=qbh, kbh=kbh, obt=obt, lse=lse,
                 oab=oab, obb=obb, ya=ya, yb=yb, mg=mg)
    return x2, saved


def _mixer_bwd(tag, sv, sm, w, wts, gbuf, l, aux, dx2, dx2b):
    b, s = aux["b"], aux["s"]
    t, d = dx2.shape
    o_gb = _O_GA + d
    dmg = gmm_rows_dx(tag + "_d_out", dx2b, wts["wo"], l)
    gbuf["wo"] = gmm_rows_dw(tag + "_dw_out", sv["mg"], dx2b, gbuf["wo"], wts["wo"].shape, l)
    dya, dyb, dga, dgb = gate_bwd(tag + "_dgate", sv["proj"], sv["ya"], sv["yb"], dmg, _O_GA, o_gb)
    dwa = mm_tn(tag + "_dw_branch_a", sv["oab"], dya)
    dwb = mm_tn(tag + "_dw_branch_b", sv["obb"], dyb)
    doa = _heads(mm_nt(tag + "_d_branch_a", dya, w["wa"], tm=1024), b, s, N_HEADS_A, HEAD_DIM_A)
    dob = _heads(mm_nt(tag + "_d_branch_b", dyb, w["wb"], tm=1024), b, s, N_HEADS_B, V_DIM_B)
    mla_t = sv["obt"].shape[-1]
    sinks = sm["swa_sinks"].reshape(-1)
    dqah, dkah, dva, dsinks = swa_bwd(tag + "_dswa", sv["qah"], sv["kah"], sv["va"], aux["pos_col"], aux["pos_row"],
                                      aux["slopes"], sinks, doa)
    dqa_raw, dg_swa_q = qk_prep_bwd(tag + "_dqa_norm", sv["qa_raw"], sm["swa_q_norm"], dqah)
    dka_raw, dg_swa_k = qk_prep_bwd(tag + "_dka_norm", sv["ka_raw"], sm["swa_k_norm"], dkah)
    dqbt, dkbh, dvb = mla_bwd(tag + "_dmla", sv["qbh"], sv["kbh"], _blocks_t(sv["kbh"], mla_t), sv["vb"], dob.astype(BF16),
                              _blocks_t(dob, mla_t), sv["obt"], sv["lse"])
    dqbh = _unblocks_t(dqbt)
    dqb_raw, dg_mla_q = qk_prep_bwd(tag + "_dqb_norm", sv["qb_raw"], sm["mla_q_norm"], dqbh, aux["rope"])
    dkfull, dkr_sum, dg_mla_k = qk_prep_bwd(tag + "_dkb_norm", sv["kfull_raw"], sm["mla_k_norm"], dkbh, aux["rope"], head_sum=True)
    dq_tok = _unheads(dqb_raw)
    dkv_tok = _unheads(jnp.concatenate([dkfull[..., :QK_NOPE_DIM], dvb.astype(BF16)], axis=-1))
    dwuq = mm_tn(tag + "_dw_uq", sv["cqn"], dq_tok, tk=1024)
    dwukv = mm_tn(tag + "_dw_ukv", sv["ckvn"], dkv_tok, tk=1024)
    dcqn = mm_nt(tag + "_d_uq", dq_tok, w["wuq"], tm=1024)
    dckvn = mm_nt(tag + "_d_ukv", dkv_tok, w["wukv"], tm=1024)
    dcq, dg_q_lora = rms_bwd(tag + "_drms_cq", sv["cq"], sm["mla_q_lora_norm"], dcqn, want_f32=False, tm=1024)
    dckv, dg_kv_lora = rms_bwd(tag + "_drms_ckv", sv["ckv"], sm["mla_kv_lora_norm"], dckvn, want_f32=False, tm=1024)
    dkr = dkr_sum[..., QK_NOPE_DIM:].reshape(t, QK_ROPE_DIM).astype(BF16)
    parts = [_unheads(dqa_raw), _unheads(dka_raw), _unheads(dva.astype(BF16)), dcq, dckv, dga, dgb, dkr]
    pad = w["win"].shape[1] - sum(p.shape[1] for p in parts)
    if pad:
        parts.append(jnp.zeros((t, pad), BF16))
    dproj = jnp.concatenate(parts, axis=1)
    dwin = mm_tn(tag + "_dw_in", sv["hb"], dproj, tm=1024, tn=1664, tk=512)
    dh = mm_nt(tag + "_d_in", dproj, w["win"], tm=512, tn=1024, tk=1664)
    dx1, dx1b, dg_mix = rms_bwd(tag + "_drms", sv["x1"], sm["mix_norm"], dh, dres=dx2)
    wgrads = dict(win=dwin, wuq=dwuq, wukv=dwukv, wa=dwa, wb=dwb)
    sgrads = dict(mix_norm=dg_mix, swa_q_norm=dg_swa_q, swa_k_norm=dg_swa_k, swa_sinks=dsinks[:, 0].reshape(1, -1),
                  mla_q_lora_norm=dg_q_lora, mla_kv_lora_norm=dg_kv_lora, mla_q_norm=dg_mla_q, mla_k_norm=dg_mla_k)
    return dx1, dx1b, wgrads, sgrads


def _layer_weights(gathered_l, offs, shard_shapes, d):
    def pieces(name):
        return [_from_rows(gathered_l[s], offs[name], shard_shapes[name]) for s in range(N_SHARDS)]

    def cols(name):
        return jnp.concatenate(pieces(name), axis=1)

    win_ref = cols("w_in")
    width = win_ref.shape[1]
    parts = [win_ref[:, :_O_GA], win_ref[:, _O_GA + QK_ROPE_DIM:], win_ref[:, _O_GA:_O_GA + QK_ROPE_DIM]]
    padded = -(-width // (2 * LANES)) * (2 * LANES)
    if padded > width:
        parts.append(jnp.zeros((d, padded - width), BF16))
    return dict(win=jnp.concatenate(parts, axis=1), wuq=cols("mla_w_uq"), wukv=cols("mla_w_ukv"),
                wa=cols("w_branch_a"), wb=cols("w_branch_b"))


def _grad_pieces(g, s, shard_shapes, d):
    def col(a, name):
        c = shard_shapes[name][1]
        return a[:, s * c:(s + 1) * c]

    win = g["win"]
    width = N_SHARDS * shard_shapes["w_in"][1]
    win_ref = jnp.concatenate([win[:, :_O_GA], win[:, width - QK_ROPE_DIM:width], win[:, _O_GA:width - QK_ROPE_DIM]], axis=1)
    return dict(w_in=col(win_ref, "w_in"), mla_w_uq=col(g["wuq"], "mla_w_uq"), mla_w_ukv=col(g["wukv"], "mla_w_ukv"),
                w_branch_a=col(g["wa"], "w_branch_a"), w_branch_b=col(g["wb"], "w_branch_b"))


def _local_step(x, positions, target, wts, weights, small, depth):
    b, s, d = x.shape
    t = b * s
    posf = positions.astype(F32)
    half = QK_ROPE_DIM // 2
    inv_freq = ROPE_BASE ** (-jnp.arange(half, dtype=F32) / half)
    ang = posf[..., None] * inv_freq
    cos, sin = jnp.cos(ang), jnp.sin(ang)
    rope = (jnp.concatenate([jnp.ones((b, s, QK_NOPE_DIM), F32), cos, cos], axis=-1),
            jnp.concatenate([jnp.zeros((b, s, QK_NOPE_DIM), F32), sin, sin], axis=-1))
    slopes = jnp.exp2(-8.0 * (jnp.arange(N_HEADS_A, dtype=F32) + 1.0) / N_HEADS_A)
    aux = dict(b=b, s=s, pos_col=posf.reshape(b, s, 1), pos_row=posf.reshape(b, 1, s), rope=rope, slopes=slopes)

    def sm_of(l):
        return {n: small[n][l:l + 1] for n in SMALL_NAMES}

    h = x.reshape(t, d)
    saved = []
    for l in range(depth):
        sm, w = sm_of(l), weights[l]
        h, s1 = _ffn_fwd(f"l{l}_ffn1", h, sm["ffn1_norm"], wts, l, 0)
        h, s2 = _mixer_fwd(f"l{l}_mix", h, sm, w, wts, l, aux)
        h, s3 = _ffn_fwd(f"l{l}_ffn2", h, sm["ffn2_norm"], wts, l, 1)
        saved.append((s1, s2, s3))
    loss, dh, dhb = loss_fwd_bwd("loss", h, target.reshape(t, d))
    wgrads, sgrads = [None] * depth, [None] * depth
    gbuf = dict(col=None, down=None, wo=None)
    for l in reversed(range(depth)):
        sm, w = sm_of(l), weights[l]
        s1, s2, s3 = saved[l]
        dh, dhb, dg_f2 = _ffn_bwd(f"l{l}_ffn2", s3, sm["ffn2_norm"], wts, gbuf, l, 1, dh, dhb)
        dh, dhb, wg, sg = _mixer_bwd(f"l{l}_mix", s2, sm, w, wts, gbuf, l, aux, dh, dhb)
        dh, dhb, dg_f1 = _ffn_bwd(f"l{l}_ffn1", s1, sm["ffn1_norm"], wts, gbuf, l, 0, dh, dhb)
        sg.update(ffn1_norm=dg_f1, ffn2_norm=dg_f2)
        wgrads[l], sgrads[l] = wg, sg
    return loss, dh.reshape(b, s, d), gbuf, wgrads, sgrads


def kernel(x, positions, ffn1_norm, ffn1_w_gate, ffn1_w_up, ffn1_w_down, mix_norm, w_in, swa_q_norm, swa_k_norm, swa_sinks, mla_q_lora_norm, mla_w_uq, mla_kv_lora_norm, mla_w_ukv, mla_q_norm, mla_k_norm, w_branch_a, w_branch_b, w_out, ffn2_norm, ffn2_w_gate, ffn2_w_up, ffn2_w_down, loss_target, m_ffn1_norm, m_ffn1_w_gate, m_ffn1_w_up, m_ffn1_w_down, m_mix_norm, m_w_in, m_swa_q_norm, m_swa_k_norm, m_swa_sinks, m_mla_q_lora_norm, m_mla_w_uq, m_mla_kv_lora_norm, m_mla_w_ukv, m_mla_q_norm, m_mla_k_norm, m_w_branch_a, m_w_branch_b, m_w_out, m_ffn2_norm, m_ffn2_w_gate, m_ffn2_w_up, m_ffn2_w_down, v_ffn1_norm, v_ffn1_w_gate, v_ffn1_w_up, v_ffn1_w_down, v_mix_norm, v_w_in, v_swa_q_norm, v_swa_k_norm, v_swa_sinks, v_mla_q_lora_norm, v_mla_w_uq, v_mla_kv_lora_norm, v_mla_w_ukv, v_mla_q_norm, v_mla_k_norm, v_w_branch_a, v_w_branch_b, v_w_out, v_ffn2_norm, v_ffn2_w_gate, v_ffn2_w_up, v_ffn2_w_down):
    args = (x, positions, ffn1_norm, ffn1_w_gate, ffn1_w_up, ffn1_w_down, mix_norm, w_in, swa_q_norm, swa_k_norm, swa_sinks, mla_q_lora_norm, mla_w_uq, mla_kv_lora_norm, mla_w_ukv, mla_q_norm, mla_k_norm, w_branch_a, w_branch_b, w_out, ffn2_norm, ffn2_w_gate, ffn2_w_up, ffn2_w_down, loss_target, m_ffn1_norm, m_ffn1_w_gate, m_ffn1_w_up, m_ffn1_w_down, m_mix_norm, m_w_in, m_swa_q_norm, m_swa_k_norm, m_swa_sinks, m_mla_q_lora_norm, m_mla_w_uq, m_mla_kv_lora_norm, m_mla_w_ukv, m_mla_q_norm, m_mla_k_norm, m_w_branch_a, m_w_branch_b, m_w_out, m_ffn2_norm, m_ffn2_w_gate, m_ffn2_w_up, m_ffn2_w_down, v_ffn1_norm, v_ffn1_w_gate, v_ffn1_w_up, v_ffn1_w_down, v_mix_norm, v_w_in, v_swa_q_norm, v_swa_k_norm, v_swa_sinks, v_mla_q_lora_norm, v_mla_w_uq, v_mla_kv_lora_norm, v_mla_w_ukv, v_mla_q_norm, v_mla_k_norm, v_w_branch_a, v_w_branch_b, v_w_out, v_ffn2_norm, v_ffn2_w_gate, v_ffn2_w_up, v_ffn2_w_down)
    a = dict(zip(ARG_NAMES, args, strict=True))
    x = a["x"]
    depth = a["ffn1_norm"].shape[0]
    d = x.shape[-1]
    shard_shapes = {n: a[n].shape[1:] for n in PACK_NAMES}
    offs, slab_rows = _pack_layout(shard_shapes)
    assert depth == 2, "one layer slab per core of a chip"

    packs = [jnp.concatenate([a[n] for n in COL_NAMES], axis=1).astype(BF16),
             jnp.concatenate([a[n] for n in DOWN_NAMES], axis=1).astype(BF16),
             a["w_out"].astype(BF16),
             jnp.stack([_pack_slab({n: a[n][l] for n in MISC_NAMES}, slab_rows, BF16) for l in range(depth)])]
    shard_me = (2 * lax.axis_index("x") + lax.axis_index("y")).astype(jnp.int32)
    core_me = lax.axis_index("c").astype(jnp.int32)
    col, down, wo, misc = allgather_weights(packs)
    weights = [_layer_weights(misc[l], offs, shard_shapes, d) for l in range(depth)]
    small = {n: a[n] for n in SMALL_NAMES}

    loss, grad_x, gbuf, wgrads, sgrads = _local_step(x, a["positions"], a["loss_target"], dict(col=col, down=down, wo=wo),
                                                     weights, small, depth)

    gmisc = jnp.stack([jnp.stack([_pack_slab(_grad_pieces(wgrads[l], s, shard_shapes, d), slab_rows, F32)
                                  for s in range(N_SHARDS)]) for l in range(depth)])
    gbufs = [gbuf["col"], gbuf["down"], gbuf["wo"], gmisc]
    tags = ("col", "down", "wo", "misc")
    theirs = sibling_swap(gbufs)
    parts = [add_own_slab("add_own_" + tag, g.reshape(depth, -1, g.shape[-1]), o.reshape(-1, o.shape[-1]), core_me.reshape(1)).reshape(o.shape)
             for tag, g, o in zip(tags, gbufs, theirs)]
    slots = scatter_to_owners(parts)
    place = jnp.stack([shard_me, core_me])
    fcol, fdown, fwo, fmisc = sibling_share([sum_slots("sum_" + tag, sl, p, place) for tag, sl, p in zip(tags, slots, parts)])
    n_ff = shard_shapes["ffn1_w_down"][0]
    grads = {n: fcol[:, i * d:(i + 1) * d] for i, n in enumerate(COL_NAMES)}
    grads.update({n: fdown[:, i * n_ff:(i + 1) * n_ff] for i, n in enumerate(DOWN_NAMES)})
    grads["w_out"] = fwo
    grads.update({n: jnp.stack([_from_rows(fmisc[l], offs[n], shard_shapes[n]) for l in range(depth)]) for n in MISC_NAMES})

    flat = jnp.concatenate([jnp.concatenate([sgrads[l][n].reshape(-1) for l in range(depth)]) for n in SMALL_NAMES] + [loss.reshape(-1)])
    n_small = flat.shape[0]
    rows = -(-n_small // (8 * LANES)) * 8
    pad = rows * LANES - n_small

    def small_pack(v):
        return jnp.pad(v, (0, pad)).reshape(rows, LANES)

    total = small_allreduce(small_pack(flat))
    w_s, m_s, v_s = (small_pack(jnp.concatenate([a[p + n].reshape(-1) for n in SMALL_NAMES] + [jnp.zeros((1,), F32)]))
                     for p in ("", "m_", "v_"))
    d_s, nm_s, nv_s = adamw("adamw_small", w_s, total, m_s, v_s)

    def small_unpack(buf):
        out, off, flat_b = {}, 0, buf.reshape(-1)
        for n in SMALL_NAMES:
            size = a[n].shape[0] * a[n].shape[1]
            out[n] = flat_b[off:off + size].reshape(a[n].shape)
            off += size
        return out

    grads.update(small_unpack(total))
    delta, new_m, new_v = small_unpack(d_s), small_unpack(nm_s), small_unpack(nv_s)
    for n in PACK_NAMES:
        shp = a[n].shape
        two_d = (shp[0] * shp[1], shp[2])
        dn, mn, vn = adamw("adamw_" + n, a[n].reshape(two_d), grads[n].reshape(two_d), a["m_" + n].reshape(two_d), a["v_" + n].reshape(two_d))
        delta[n], new_m[n], new_v[n] = dn.reshape(shp), mn.reshape(shp), vn.reshape(shp)

    loss_out = total.reshape(-1)[n_small - 1]
    return (loss_out, grad_x, *[grads[n] for n in WEIGHT_NAMES], *[delta[n] for n in WEIGHT_NAMES],
            *[new_m[n] for n in WEIGHT_NAMES], *[new_v[n] for n in WEIGHT_NAMES])
```

```python
import functools

import numpy as np
import jax
import jax.numpy as jnp
from jax import lax
from jax.experimental import pallas as pl
from jax.experimental.pallas import tpu as pltpu

F32 = jnp.float32
BF16 = jnp.bfloat16
MESH = pl.DeviceIdType.MESH

HEAD_DIM_A = 64
N_HEADS_A = 8
N_KV_HEADS_A = 2
GROUP_A = N_HEADS_A // N_KV_HEADS_A
BLOCK = 128
N_HEADS_B = 8
Q_LORA_RANK = 256
KV_LORA_RANK = 128
QK_NOPE_DIM = 64
QK_ROPE_DIM = 32
QK_DIM_B = QK_NOPE_DIM + QK_ROPE_DIM
V_DIM_B = 64
ROPE_BASE = 10000.0
WIDTH_A = N_HEADS_A * HEAD_DIM_A
WIDTH_B = N_HEADS_B * V_DIM_B
KV_WIDTH_A = N_KV_HEADS_A * HEAD_DIM_A
EPS = 1e-6
NEG = -1e30
ADAM_LR = 0.001
ADAM_B1 = 0.9
ADAM_B2 = 0.999
ADAM_EPS = 1e-08
ADAM_WD = 0.01
ADAM_STEP = 10

N_SHARDS = 4
N_DEV = 8
LANES = 128
PACK_COLS = 1024
PACK_ROW_ALIGN = 16
VMEM_LIMIT = 48 * 1024 * 1024

PACK_NAMES = ("ffn1_w_gate", "ffn1_w_up", "ffn1_w_down", "w_in", "mla_w_uq", "mla_w_ukv",
              "w_branch_a", "w_branch_b", "w_out", "ffn2_w_gate", "ffn2_w_up", "ffn2_w_down")
COL_NAMES = ("ffn1_w_gate", "ffn1_w_up", "ffn2_w_gate", "ffn2_w_up")
DOWN_NAMES = ("ffn1_w_down", "ffn2_w_down")
MISC_NAMES = ("w_in", "mla_w_uq", "mla_w_ukv", "w_branch_a", "w_branch_b")
SMALL_NAMES = ("ffn1_norm", "mix_norm", "swa_q_norm", "swa_k_norm", "swa_sinks", "mla_q_lora_norm",
               "mla_kv_lora_norm", "mla_q_norm", "mla_k_norm", "ffn2_norm")
WEIGHT_NAMES = ("ffn1_norm", "ffn1_w_gate", "ffn1_w_up", "ffn1_w_down", "mix_norm", "w_in", "swa_q_norm",
                "swa_k_norm", "swa_sinks", "mla_q_lora_norm", "mla_w_uq", "mla_kv_lora_norm", "mla_w_ukv",
                "mla_q_norm", "mla_k_norm", "w_branch_a", "w_branch_b", "w_out", "ffn2_norm", "ffn2_w_gate",
                "ffn2_w_up", "ffn2_w_down")


def _params(sem):
    return pltpu.CompilerParams(dimension_semantics=sem, vmem_limit_bytes=VMEM_LIMIT)


def _tile(n, want, align):
    if n <= want:
        return n
    t = (want // align) * align
    while t > align and n % t:
        t -= align
    assert t >= align and n % t == 0, (n, want, align)
    return t


def _mm_call(name, a, b, out_struct, grid, a_spec, b_spec, o_spec, dims, n_red, acc_shape, alpha=1.0, res=None, into=None):
    n_par = len(grid) - n_red

    def body(*refs):
        a_ref, b_ref = refs[:2]
        r_ref = refs[2] if res is not None else None
        o_ref, acc_ref = refs[-2:]
        part = lax.dot_general(a_ref[...], b_ref[...], dims, preferred_element_type=F32)

        def finish(total):
            if alpha != 1.0:
                total = total * alpha
            if r_ref is not None:
                total = r_ref[...] + total
            o_ref[...] = total.astype(o_ref.dtype)

        if n_red == 0:
            finish(part)
            return
        ids = [pl.program_id(n_par + i) for i in range(n_red)]
        first = functools.reduce(jnp.logical_and, [i == 0 for i in ids])
        last = functools.reduce(jnp.logical_and, [i == grid[n_par + k] - 1 for k, i in enumerate(ids)])

        @pl.when(first)
        def _():
            acc_ref[...] = part

        @pl.when(jnp.logical_not(first))
        def _():
            acc_ref[...] += part

        @pl.when(last)
        def _():
            finish(acc_ref[...])

    in_specs = [a_spec, b_spec] + ([o_spec] if res is not None else [])
    args = (a, b) + ((res,) if res is not None else ())
    aliases = {}
    if into is not None:
        aliases = {len(args): 0}
        in_specs.append(pl.BlockSpec(memory_space=pl.ANY))
        args = args + (into,)
    return pl.pallas_call(
        body, name=name, out_shape=out_struct, grid=grid, in_specs=in_specs, out_specs=o_spec,
        scratch_shapes=[pltpu.VMEM(acc_shape, F32)], input_output_aliases=aliases,
        compiler_params=_params(("parallel",) * n_par + ("arbitrary",) * n_red),
    )(*args)


_NN = (((1,), (0,)), ((), ()))
_NT = (((1,), (1,)), ((), ()))
_TN = (((0,), (0,)), ((), ()))


def mm_nn(name, a, b, tm=512, tn=1024, tk=1024, out_dtype=F32, alpha=1.0, res=None):
    (m, k), (_, n) = a.shape, b.shape
    tm, tn, tk = _tile(m, tm, 16), _tile(n, tn, LANES), _tile(k, tk, LANES)
    return _mm_call(name, a, b, jax.ShapeDtypeStruct((m, n), out_dtype), (m // tm, n // tn, k // tk),
                    pl.BlockSpec((tm, tk), lambda i, j, kk: (i, kk)), pl.BlockSpec((tk, tn), lambda i, j, kk: (kk, j)),
                    pl.BlockSpec((tm, tn), lambda i, j, kk: (i, j)), _NN, 1, (tm, tn), alpha, res)


def mm_nt(name, a, b, tm=512, tn=1024, tk=1024, out_dtype=F32, alpha=1.0):
    (m, n), (k, _) = a.shape, b.shape
    tm, tn, tk = _tile(m, tm, 16), _tile(k, tn, LANES), _tile(n, tk, LANES)
    return _mm_call(name, a, b, jax.ShapeDtypeStruct((m, k), out_dtype), (m // tm, k // tn, n // tk),
                    pl.BlockSpec((tm, tk), lambda i, j, kk: (i, kk)), pl.BlockSpec((tn, tk), lambda i, j, kk: (j, kk)),
                    pl.BlockSpec((tm, tn), lambda i, j, kk: (i, j)), _NT, 1, (tm, tn), alpha)


def mm_tn(name, a, b, tm=1024, tn=1024, tk=512, out_dtype=F32, alpha=1.0):
    (m, k), (_, n) = a.shape, b.shape
    tm, tn, tk = _tile(k, tm, LANES), _tile(n, tn, LANES), _tile(m, tk, 16)
    return _mm_call(name, a, b, jax.ShapeDtypeStruct((k, n), out_dtype), (k // tm, n // tn, m // tk),
                    pl.BlockSpec((tk, tm), lambda i, j, kk: (kk, i)), pl.BlockSpec((tk, tn), lambda i, j, kk: (kk, j)),
                    pl.BlockSpec((tm, tn), lambda i, j, kk: (i, j)), _TN, 1, (tm, tn), alpha)


def ffn_up_act(name, a, w, l, blk, tm=512):
    (m, k), (_, ns, _, n) = a.shape, w.shape
    tm = _tile(m, tm, 16)

    def body(a_ref, wg_ref, wu_ref, gu_ref, h_ref):
        av = a_ref[...]
        gate = jnp.dot(av, wg_ref[...], preferred_element_type=F32)
        up = jnp.dot(av, wu_ref[...], preferred_element_type=F32)
        gu_ref[0] = gate.astype(BF16)
        gu_ref[1] = up.astype(BF16)
        h_ref[...] = (gate * jax.nn.sigmoid(gate) * up).astype(BF16)

    return pl.pallas_call(
        body, name=name, grid=(ns, m // tm),
        out_shape=(jax.ShapeDtypeStruct((2, ns, m, n), BF16), jax.ShapeDtypeStruct((ns, m, n), BF16)),
        in_specs=[pl.BlockSpec((tm, k), lambda s, i: (i, 0)), pl.BlockSpec((None, None, k, n), lambda s, i: (l, s, blk, 0)),
                  pl.BlockSpec((None, None, k, n), lambda s, i: (l, s, blk + 1, 0))],
        out_specs=(pl.BlockSpec((2, None, tm, n), lambda s, i: (0, s, i, 0)), pl.BlockSpec((None, tm, n), lambda s, i: (s, i, 0))),
        compiler_params=_params(("parallel", "parallel")),
    )(a, w, w)


def ffn_down_dact(name, a, w, gu, n, l, blk, alpha, tm=512):
    (m, d), ns = a.shape, w.shape[1]
    tm = _tile(m, tm, 16)

    def body(a_ref, w_ref, gu_ref, o_ref):
        dh = lax.dot_general(a_ref[...], w_ref[...], _NT, preferred_element_type=F32) * alpha
        gate, up = gu_ref[0].astype(F32), gu_ref[1].astype(F32)
        s = jax.nn.sigmoid(gate)
        o_ref[0] = (dh * up * (s * (1.0 + gate * (1.0 - s)))).astype(BF16)
        o_ref[1] = (dh * (gate * s)).astype(BF16)

    gu_spec = pl.BlockSpec((2, None, tm, n), lambda s, i: (0, s, i, 0))
    return pl.pallas_call(
        body, name=name, grid=(ns, m // tm), out_shape=jax.ShapeDtypeStruct((2, ns, m, n), BF16),
        in_specs=[pl.BlockSpec((tm, d), lambda s, i: (i, 0)), pl.BlockSpec((None, None, n, d), lambda s, i: (l, s, blk, 0)), gu_spec],
        out_specs=gu_spec, compiler_params=_params(("parallel", "parallel")),
    )(a, w, gu)


def gmm_up_dw(name, a, dgu, into, shape, l, blk, tk=512):
    (m, k), (_, ns, _, n) = a.shape, dgu.shape
    tk = _tile(m, tk, 16)
    return _mm_call(name, a, dgu, jax.ShapeDtypeStruct(shape, F32), (2, ns, m // tk),
                    pl.BlockSpec((tk, k), lambda j, s, kk: (kk, 0)), pl.BlockSpec((None, None, tk, n), lambda j, s, kk: (j, s, kk, 0)),
                    pl.BlockSpec((None, None, k, n), lambda j, s, kk: (l, s, blk + j, 0)), _TN, 1, (k, n), into=into)


def gmm_up_dx(name, dgu, w, k, l, blk, tm=512):
    _, ns, m, n = dgu.shape
    tm = _tile(m, tm, 16)
    return _mm_call(name, dgu, w, jax.ShapeDtypeStruct((m, k), F32), (m // tm, 2, ns),
                    pl.BlockSpec((None, None, tm, n), lambda i, j, s: (j, s, i, 0)),
                    pl.BlockSpec((None, None, k, n), lambda i, j, s: (l, s, blk + j, 0)),
                    pl.BlockSpec((tm, k), lambda i, j, s: (i, 0)), _NT, 2, (tm, k))


def gmm_down(name, h, w, l, blk, res, alpha, tm=512):
    (ns, m, n), d = h.shape, w.shape[3]
    tm = _tile(m, tm, 16)
    return _mm_call(name, h, w, jax.ShapeDtypeStruct((m, d), F32), (m // tm, ns),
                    pl.BlockSpec((None, tm, n), lambda i, s: (s, i, 0)), pl.BlockSpec((None, None, n, d), lambda i, s: (l, s, blk, 0)),
                    pl.BlockSpec((tm, d), lambda i, s: (i, 0)), _NN, 1, (tm, d), alpha, res)


def gmm_down_dw(name, h, b, into, shape, l, blk, alpha, tk=512):
    (ns, m, n), d = h.shape, b.shape[1]
    tk = _tile(m, tk, 16)
    return _mm_call(name, h, b, jax.ShapeDtypeStruct(shape, F32), (ns, m // tk),
                    pl.BlockSpec((None, tk, n), lambda s, kk: (s, kk, 0)), pl.BlockSpec((tk, d), lambda s, kk: (kk, 0)),
                    pl.BlockSpec((None, None, n, d), lambda s, kk: (l, s, blk, 0)), _TN, 1, (n, d), alpha, into=into)


def gmm_rows(name, a, w, l, res, tm=512):
    (m, _), (_, ns, r, d) = a.shape, w.shape
    tm = _tile(m, tm, 16)
    return _mm_call(name, a, w, jax.ShapeDtypeStruct((m, d), F32), (m // tm, ns),
                    pl.BlockSpec((tm, r), lambda i, s: (i, s)), pl.BlockSpec((None, None, r, d), lambda i, s: (l, s, 0, 0)),
                    pl.BlockSpec((tm, d), lambda i, s: (i, 0)), _NN, 1, (tm, d), 1.0, res)


def gmm_rows_dx(name, a, w, l, tm=512):
    (m, d), (_, ns, r, _) = a.shape, w.shape
    tm = _tile(m, tm, 16)
    return _mm_call(name, a, w, jax.ShapeDtypeStruct((m, ns * r), F32), (ns, m // tm),
                    pl.BlockSpec((tm, d), lambda s, i: (i, 0)), pl.BlockSpec((None, None, r, d), lambda s, i: (l, s, 0, 0)),
                    pl.BlockSpec((tm, r), lambda s, i: (i, s)), _NT, 0, (8, LANES))


def gmm_rows_dw(name, a, b, into, shape, l, tk=512):
    (m, _), d = a.shape, b.shape[1]
    ns, r = shape[1], shape[2]
    tk = _tile(m, tk, 16)
    return _mm_call(name, a, b, jax.ShapeDtypeStruct(shape, F32), (ns, m // tk),
                    pl.BlockSpec((tk, r), lambda s, kk: (kk, s)), pl.BlockSpec((tk, d), lambda s, kk: (kk, 0)),
                    pl.BlockSpec((None, None, r, d), lambda s, kk: (l, s, 0, 0)), _TN, 1, (r, d), into=into)


def rms_fwd(name, x, gain, tm=512):
    m, d = x.shape
    tm = _tile(m, tm, 16)

    def body(x_ref, g_ref, o_ref):
        xv = x_ref[...]
        r = lax.rsqrt(jnp.mean(xv * xv, axis=-1, keepdims=True) + EPS)
        o_ref[...] = (xv * r * g_ref[...]).astype(o_ref.dtype)

    return pl.pallas_call(
        body, name=name, out_shape=jax.ShapeDtypeStruct((m, d), BF16), grid=(m // tm,),
        in_specs=[pl.BlockSpec((tm, d), lambda i: (i, 0)), pl.BlockSpec((1, d), lambda i: (0, 0))],
        out_specs=pl.BlockSpec((tm, d), lambda i: (i, 0)), compiler_params=_params(("parallel",)),
    )(x, gain)


def rms_bwd(name, x, gain, dn, dres=None, want_f32=True, want_bf16=True, tm=512):
    m, d = x.shape
    tm = _tile(m, tm, 16)
    n_out = int(want_f32) + int(want_bf16)

    def body(*refs):
        x_ref, g_ref, dn_ref = refs[:3]
        pos = 3
        r_ref = None
        if dres is not None:
            r_ref = refs[pos]
            pos += 1
        outs = refs[pos:pos + n_out]
        dg_ref = refs[pos + n_out]
        xv = x_ref[...]
        r = lax.rsqrt(jnp.mean(xv * xv, axis=-1, keepdims=True) + EPS)
        xhat = xv * r
        dnv = dn_ref[...]
        dxhat = dnv * g_ref[...]
        dx = r * (dxhat - xhat * jnp.mean(dxhat * xhat, axis=-1, keepdims=True))
        if r_ref is not None:
            dx = r_ref[...] + dx
        for o in outs:
            o[...] = dx.astype(o.dtype)
        part = jnp.sum(dnv * xhat, axis=0, keepdims=True)

        @pl.when(pl.program_id(0) == 0)
        def _():
            dg_ref[...] = part

        @pl.when(pl.program_id(0) > 0)
        def _():
            dg_ref[...] += part

    row = pl.BlockSpec((tm, d), lambda i: (i, 0))
    vec = pl.BlockSpec((1, d), lambda i: (0, 0))
    out_shape = ([jax.ShapeDtypeStruct((m, d), F32)] if want_f32 else []) + ([jax.ShapeDtypeStruct((m, d), BF16)] if want_bf16 else [])
    res = pl.pallas_call(
        body, name=name, out_shape=tuple(out_shape) + (jax.ShapeDtypeStruct((1, d), F32),), grid=(m // tm,),
        in_specs=[row, vec, row] + ([row] if dres is not None else []),
        out_specs=tuple([row] * n_out) + (vec,), compiler_params=_params(("arbitrary",)),
    )(*((x, gain, dn) + ((dres,) if dres is not None else ())))
    return res


def gate_fwd(name, proj, ya, yb, off_a, off_b, tm=256):
    m, d = ya.shape
    w = proj.shape[1]
    tm = _tile(m, tm, 16)

    def body(p_ref, ya_ref, yb_ref, o_ref):
        ga, gb = p_ref[:, off_a:off_a + d], p_ref[:, off_b:off_b + d]
        o_ref[...] = (jax.nn.sigmoid(ga) * ya_ref[...] + jax.nn.sigmoid(gb) * yb_ref[...]).astype(o_ref.dtype)

    row = pl.BlockSpec((tm, d), lambda i: (i, 0))
    return pl.pallas_call(
        body, name=name, out_shape=jax.ShapeDtypeStruct((m, d), BF16), grid=(m // tm,),
        in_specs=[pl.BlockSpec((tm, w), lambda i: (i, 0)), row, row], out_specs=row, compiler_params=_params(("parallel",)),
    )(proj, ya, yb)


def gate_bwd(name, proj, ya, yb, dmerged, off_a, off_b, tm=256):
    m, d = ya.shape
    w = proj.shape[1]
    tm = _tile(m, tm, 16)

    def body(p_ref, ya_ref, yb_ref, dm_ref, dya_ref, dyb_ref, dga_ref, dgb_ref):
        sa, sb = jax.nn.sigmoid(p_ref[:, off_a:off_a + d]), jax.nn.sigmoid(p_ref[:, off_b:off_b + d])
        dm = dm_ref[...]
        dya_ref[...] = (dm * sa).astype(BF16)
        dyb_ref[...] = (dm * sb).astype(BF16)
        dga_ref[...] = (dm * ya_ref[...] * (sa * (1.0 - sa))).astype(BF16)
        dgb_ref[...] = (dm * yb_ref[...] * (sb * (1.0 - sb))).astype(BF16)

    row = pl.BlockSpec((tm, d), lambda i: (i, 0))
    o = jax.ShapeDtypeStruct((m, d), BF16)
    return pl.pallas_call(
        body, name=name, out_shape=(o, o, o, o), grid=(m // tm,),
        in_specs=[pl.BlockSpec((tm, w), lambda i: (i, 0)), row, row, row], out_specs=(row, row, row, row),
        compiler_params=_params(("parallel",)),
    )(proj, ya, yb, dmerged)


def loss_fwd_bwd(name, y, target, tm=512):
    m, d = y.shape
    tm = _tile(m, tm, 16)

    def body(y_ref, t_ref, l_ref, dy_ref, dyb_ref):
        err = y_ref[...] - t_ref[...]
        dy = err * (1.0 / d)
        dy_ref[...] = dy
        dyb_ref[...] = dy.astype(BF16)
        part = 0.5 * jnp.sum(jnp.mean(err * err, axis=-1, keepdims=True), axis=0, keepdims=True)

        @pl.when(pl.program_id(0) == 0)
        def _():
            l_ref[...] = part

        @pl.when(pl.program_id(0) > 0)
        def _():
            l_ref[...] += part

    row = pl.BlockSpec((tm, d), lambda i: (i, 0))
    return pl.pallas_call(
        body, name=name, grid=(m // tm,),
        out_shape=(jax.ShapeDtypeStruct((1, 1), F32), jax.ShapeDtypeStruct((m, d), F32), jax.ShapeDtypeStruct((m, d), BF16)),
        in_specs=[row, row], out_specs=(pl.BlockSpec((1, 1), lambda i: (0, 0)), row, row),
        compiler_params=_params(("arbitrary",)),
    )(y, target)


def adamw(name, w, g, m, v):
    r, c = w.shape
    tr = _tile(r, max(8, (2 * 1024 * 1024) // (4 * c) // 8 * 8), 8)
    c1 = np.float32(1.0 - ADAM_B1 ** ADAM_STEP)
    c2 = np.float32(1.0 - ADAM_B2 ** ADAM_STEP)

    def body(w_ref, g_ref, m_ref, v_ref, d_ref, nm_ref, nv_ref):
        gv = g_ref[...]
        nm = ADAM_B1 * m_ref[...] + (1.0 - ADAM_B1) * gv
        nv = ADAM_B2 * v_ref[...] + (1.0 - ADAM_B2) * (gv * gv)
        d_ref[...] = -ADAM_LR * ((nm / c1) / (jnp.sqrt(nv / c2) + ADAM_EPS) + ADAM_WD * w_ref[...])
        nm_ref[...] = nm
        nv_ref[...] = nv

    row = pl.BlockSpec((tr, c), lambda i: (i, 0))
    o = jax.ShapeDtypeStruct((r, c), F32)
    return pl.pallas_call(
        body, name=name, out_shape=(o, o, o), grid=(r // tr,), in_specs=[row] * 4, out_specs=(row, row, row),
        compiler_params=_params(("parallel",)),
    )(w, g, m, v)


def _rope_rot(d):
    r = np.zeros((d, d), np.float32)
    half = QK_ROPE_DIM // 2
    base = d - QK_ROPE_DIM
    for j in range(half):
        r[base + half + j, base + j] = -1.0
        r[base + j, base + half + j] = 1.0
    return r


def qk_prep_fwd(name, x, gain, rope=None):
    b, h, s, d = x.shape

    def body(*refs):
        if rope is None:
            x_ref, g_ref, o_ref = refs
        else:
            x_ref, g_ref, c_ref, s_ref, r_ref, o_ref = refs
        xv = x_ref[...]
        y = xv * lax.rsqrt(jnp.mean(xv * xv, axis=-1, keepdims=True) + EPS) * g_ref[...]
        if rope is not None:
            rot = jnp.dot(y, r_ref[...], precision=lax.Precision.HIGHEST, preferred_element_type=F32)
            y = y * c_ref[...] + rot * s_ref[...]
        o_ref[...] = y.astype(o_ref.dtype)

    xs = pl.BlockSpec((None, None, s, d), lambda bi, hi: (bi, hi, 0, 0))
    in_specs = [xs, pl.BlockSpec((1, d), lambda bi, hi: (0, 0))]
    args = [x, gain]
    if rope is not None:
        tab = pl.BlockSpec((None, s, d), lambda bi, hi: (bi, 0, 0))
        in_specs += [tab, tab, pl.BlockSpec((d, d), lambda bi, hi: (0, 0))]
        args += [rope[0], rope[1], jnp.asarray(_rope_rot(d))]
    return pl.pallas_call(
        body, name=name, out_shape=jax.ShapeDtypeStruct(x.shape, BF16), grid=(b, h), in_specs=in_specs, out_specs=xs,
        compiler_params=_params(("parallel", "parallel")),
    )(*args)


def qk_prep_bwd(name, x, gain, dxh, rope=None, head_sum=False):
    b, h, s, d = x.shape

    def body(*refs):
        if rope is None:
            x_ref, g_ref, dy_ref = refs[:3]
            outs = refs[3:]
        else:
            x_ref, g_ref, dy_ref, c_ref, s_ref, rt_ref = refs[:6]
            outs = refs[6:]
        dx_ref, dg_ref = outs[0], outs[-1]
        dy = dy_ref[...]
        if rope is not None:
            dy = dy * c_ref[...] + jnp.dot(dy * s_ref[...], rt_ref[...], precision=lax.Precision.HIGHEST, preferred_element_type=F32)
        xv = x_ref[...]
        r = lax.rsqrt(jnp.mean(xv * xv, axis=-1, keepdims=True) + EPS)
        xhat = xv * r
        dxhat = dy * g_ref[...]
        dx = r * (dxhat - xhat * jnp.mean(dxhat * xhat, axis=-1, keepdims=True))
        dx_ref[...] = dx.astype(dx_ref.dtype)
        part = jnp.sum(dy * xhat, axis=0, keepdims=True)
        first = jnp.logical_and(pl.program_id(0) == 0, pl.program_id(1) == 0)

        @pl.when(first)
        def _():
            dg_ref[...] = part

        @pl.when(jnp.logical_not(first))
        def _():
            dg_ref[...] += part

        if head_sum:
            hs_ref = outs[1]

            @pl.when(pl.program_id(1) == 0)
            def _():
                hs_ref[...] = dx

            @pl.when(pl.program_id(1) > 0)
            def _():
                hs_ref[...] += dx

    xs = pl.BlockSpec((None, None, s, d), lambda bi, hi: (bi, hi, 0, 0))
    vec = pl.BlockSpec((1, d), lambda bi, hi: (0, 0))
    tab = pl.BlockSpec((None, s, d), lambda bi, hi: (bi, 0, 0))
    in_specs = [xs, vec, xs]
    args = [x, gain, dxh]
    if rope is not None:
        in_specs += [tab, tab, pl.BlockSpec((d, d), lambda bi, hi: (0, 0))]
        args += [rope[0], rope[1], jnp.asarray(_rope_rot(d).T.copy())]
    out_shape = [jax.ShapeDtypeStruct(x.shape, BF16)]
    out_specs = [xs]
    if head_sum:
        out_shape.append(jax.ShapeDtypeStruct((b, s, d), F32))
        out_specs.append(tab)
    out_shape.append(jax.ShapeDtypeStruct((1, d), F32))
    out_specs.append(vec)
    return pl.pallas_call(
        body, name=name, out_shape=tuple(out_shape), grid=(b, h), in_specs=in_specs, out_specs=tuple(out_specs),
        compiler_params=_params(("arbitrary", "arbitrary")),
    )(*args)


def _swa_specs(s):
    nb = s // BLOCK
    q = pl.BlockSpec((None, GROUP_A, BLOCK, HEAD_DIM_A), lambda b, kv, n: (b, kv, n, 0))
    kprev = pl.BlockSpec((None, None, BLOCK, HEAD_DIM_A), lambda b, kv, n: (b, kv, jnp.maximum(n - 1, 0), 0))
    kcur = pl.BlockSpec((None, None, BLOCK, HEAD_DIM_A), lambda b, kv, n: (b, kv, n, 0))
    pcol = pl.BlockSpec((None, BLOCK, 1), lambda b, kv, n: (b, n, 0))
    prow_prev = pl.BlockSpec((None, 1, BLOCK), lambda b, kv, n: (b, 0, jnp.maximum(n - 1, 0)))
    prow_cur = pl.BlockSpec((None, 1, BLOCK), lambda b, kv, n: (b, 0, n))
    smem = pl.BlockSpec(memory_space=pltpu.SMEM)
    return nb, q, kprev, kcur, pcol, prow_prev, prow_cur, smem


def _swa_probs(q, kk, dist, valid, slope, sink):
    sc = lax.dot_general(q, kk, _NT, preferred_element_type=F32) * (HEAD_DIM_A ** -0.5)
    sc = sc - slope * dist
    sc = jnp.where(valid, sc, NEG)
    m = jnp.maximum(jnp.max(sc, axis=-1, keepdims=True), sink)
    e = jnp.exp(sc - m)
    es = jnp.exp(sink - m)
    inv = 1.0 / (jnp.sum(e, axis=-1, keepdims=True) + es)
    return e * inv, es * inv


def _swa_window(n, kp_ref, kc_ref, vp_ref, vc_ref, pc_ref, prp_ref, prc_ref):
    kk = jnp.concatenate([kp_ref[...], kc_ref[...]], axis=0)
    vv = jnp.concatenate([vp_ref[...], vc_ref[...]], axis=0)
    dist = pc_ref[...] - jnp.concatenate([prp_ref[...], prc_ref[...]], axis=1)
    qi = lax.broadcasted_iota(jnp.int32, (BLOCK, 2 * BLOCK), 0) + BLOCK
    ki = lax.broadcasted_iota(jnp.int32, (BLOCK, 2 * BLOCK), 1)
    diff = qi - ki
    valid = (diff >= 0) & (diff < BLOCK) & ((n > 0) | (ki >= BLOCK))
    return kk, vv, dist, valid


def swa_fwd(name, q, k, v, pos_col, pos_row, slopes, sinks):
    b, _, s, _ = q.shape
    nb, qs, kprev, kcur, pcol, prp, prc, smem = _swa_specs(s)

    def body(q_ref, kp_ref, kc_ref, vp_ref, vc_ref, pc_ref, prp_ref, prc_ref, sl_ref, sk_ref, o_ref):
        kv, n = pl.program_id(1), pl.program_id(2)
        kk, vv, dist, valid = _swa_window(n, kp_ref, kc_ref, vp_ref, vc_ref, pc_ref, prp_ref, prc_ref)
        for g in range(GROUP_A):
            hd = kv * GROUP_A + g
            p, _ = _swa_probs(q_ref[g], kk, dist, valid, sl_ref[hd], sk_ref[hd])
            o_ref[g] = jnp.dot(p.astype(BF16), vv, preferred_element_type=F32)

    return pl.pallas_call(
        body, name=name, out_shape=jax.ShapeDtypeStruct(q.shape, F32), grid=(b, N_KV_HEADS_A, nb),
        in_specs=[qs, kprev, kcur, kprev, kcur, pcol, prp, prc, smem, smem], out_specs=qs,
        compiler_params=_params(("parallel", "parallel", "parallel")),
    )(q, k, k, v, v, pos_col, pos_row, pos_row, slopes, sinks)


def swa_bwd(name, q, k, v, pos_col, pos_row, slopes, sinks, do):
    b, _, s, _ = q.shape
    nb, qs, kprev, kcur, pcol, prp, prc, smem = _swa_specs(s)

    def body(q_ref, kp_ref, kc_ref, vp_ref, vc_ref, pc_ref, prp_ref, prc_ref, sl_ref, sk_ref, do_ref, dq_ref, dk_ref, dv_ref, ds_ref):
        bi, kv, n = pl.program_id(0), pl.program_id(1), pl.program_id(2)
        kk, vv, dist, valid = _swa_window(n, kp_ref, kc_ref, vp_ref, vc_ref, pc_ref, prp_ref, prc_ref)

        @pl.when((bi == 0) & (kv == 0) & (n == 0))
        def _():
            ds_ref[...] = jnp.zeros_like(ds_ref)

        @pl.when(n == 0)
        def _():
            dk_ref[...] = jnp.zeros_like(dk_ref)
            dv_ref[...] = jnp.zeros_like(dv_ref)

        dkk = jnp.zeros((2 * BLOCK, HEAD_DIM_A), F32)
        dvv = jnp.zeros((2 * BLOCK, HEAD_DIM_A), F32)
        head_row = lax.broadcasted_iota(jnp.int32, (N_HEADS_A, LANES), 0)
        dsink = jnp.zeros((N_HEADS_A, LANES), F32)
        for g in range(GROUP_A):
            hd = kv * GROUP_A + g
            qg = q_ref[g]
            p, ps = _swa_probs(qg, kk, dist, valid, sl_ref[hd], sk_ref[hd])
            dob = do_ref[g].astype(BF16)
            dvv = dvv + lax.dot_general(p.astype(BF16), dob, _TN, preferred_element_type=F32)
            dp = lax.dot_general(dob, vv, _NT, preferred_element_type=F32)
            rs = jnp.sum(p * dp, axis=-1, keepdims=True)
            dsb = (p * (dp - rs) * (HEAD_DIM_A ** -0.5)).astype(BF16)
            dq_ref[g] = jnp.dot(dsb, kk, preferred_element_type=F32)
            dkk = dkk + lax.dot_general(dsb, qg, _TN, preferred_element_type=F32)
            dsink = dsink + jnp.where(head_row == hd, -jnp.sum(ps * rs), 0.0)
        ds_ref[...] += dsink

        @pl.when(n > 0)
        def _():
            start = pl.multiple_of((n - 1) * BLOCK, BLOCK)
            dk_ref[pl.ds(start, 2 * BLOCK), :] += dkk
            dv_ref[pl.ds(start, 2 * BLOCK), :] += dvv

        @pl.when(n == 0)
        def _():
            dk_ref[0:BLOCK, :] += dkk[BLOCK:, :]
            dv_ref[0:BLOCK, :] += dvv[BLOCK:, :]

    kv_full = pl.BlockSpec((None, None, s, HEAD_DIM_A), lambda bi, kv, n: (bi, kv, 0, 0))
    return pl.pallas_call(
        body, name=name, grid=(b, N_KV_HEADS_A, nb),
        out_shape=(jax.ShapeDtypeStruct(q.shape, F32), jax.ShapeDtypeStruct(k.shape, F32), jax.ShapeDtypeStruct(k.shape, F32),
                   jax.ShapeDtypeStruct((N_HEADS_A, LANES), F32)),
        in_specs=[qs, kprev, kcur, kprev, kcur, pcol, prp, prc, smem, smem, qs],
        out_specs=(qs, kv_full, kv_full, pl.BlockSpec((N_HEADS_A, LANES), lambda bi, kv, n: (0, 0))),
        compiler_params=_params(("arbitrary", "arbitrary", "arbitrary")),
    )(q, k, k, v, v, pos_col, pos_row, pos_row, slopes, sinks, do)


MLA_T = 256


def _blocks_t(z, t):
    b, h, s, d = z.shape
    return z.reshape(b, h, s // t, t, d).transpose(0, 1, 2, 4, 3)


def _unblocks_t(z):
    b, h, nb, d, t = z.shape
    return z.transpose(0, 1, 2, 4, 3).reshape(b, h, nb * t, d)


def mla_fwd(name, q, k, vt):
    b, h, s, dq = q.shape
    _, _, nb, dv, t = vt.shape
    scale = QK_DIM_B ** -0.5

    def body(q_ref, k_ref, vt_ref, o_ref, lse_ref):
        causal = lax.broadcasted_iota(jnp.int32, (t, t), 0) <= lax.broadcasted_iota(jnp.int32, (t, t), 1)
        for i in range(nb):
            q_i = q_ref[i * t:(i + 1) * t, :]

            def step(j, carry, q_i=q_i, diagonal=False):
                m, l, acc = carry
                rows = slice(j * t, (j + 1) * t) if diagonal else pl.ds(pl.multiple_of(j * t, t), t)
                st = lax.dot_general(k_ref[rows, :], q_i, _NT, preferred_element_type=F32) * scale
                if diagonal:
                    st = jnp.where(causal, st, NEG)
                m_new = jnp.maximum(m, jnp.max(st, axis=0, keepdims=True))
                a = jnp.exp(m - m_new)
                p = jnp.exp(st - m_new)
                l = a * l + jnp.sum(p, axis=0, keepdims=True)
                acc = a * acc + jnp.dot(vt_ref[j], p.astype(BF16), preferred_element_type=F32)
                return m_new, l, acc

            carry = (jnp.full((1, t), NEG, F32), jnp.zeros((1, t), F32), jnp.zeros((dv, t), F32))
            if i > 0:
                carry = lax.fori_loop(0, i, step, carry)
            m, l, acc = step(i, carry, diagonal=True)
            o_ref[i] = acc / l
            lse_ref[i] = m + jnp.log(l)

    qs = pl.BlockSpec((None, None, s, dq), lambda bi, hi: (bi, hi, 0, 0))
    ts = pl.BlockSpec((None, None, nb, dv, t), lambda bi, hi: (bi, hi, 0, 0, 0))
    ls = pl.BlockSpec((None, None, nb, 1, t), lambda bi, hi: (bi, hi, 0, 0, 0))
    return pl.pallas_call(
        body, name=name, grid=(b, h),
        out_shape=(jax.ShapeDtypeStruct((b, h, nb, dv, t), F32), jax.ShapeDtypeStruct((b, h, nb, 1, t), F32)),
        in_specs=[qs, qs, ts], out_specs=(ts, ls), compiler_params=_params(("parallel", "parallel")),
    )(q, k, vt)


def mla_bwd(name, q, k, kt, v, do, dot, ot, lse):
    b, h, s, dq = q.shape
    _, _, nb, dv, t = ot.shape
    scale = QK_DIM_B ** -0.5

    def body(q_ref, k_ref, kt_ref, v_ref, do_ref, dot_ref, ot_ref, lse_ref, dqt_ref, dk_ref, dv_ref):
        causal = lax.broadcasted_iota(jnp.int32, (t, t), 0) <= lax.broadcasted_iota(jnp.int32, (t, t), 1)
        dk_ref[...] = jnp.zeros_like(dk_ref)
        dv_ref[...] = jnp.zeros_like(dv_ref)
        for i in range(nb):
            q_i = q_ref[i * t:(i + 1) * t, :]
            do_i = do_ref[i * t:(i + 1) * t, :]
            dot_i = dot_ref[i]
            delta = jnp.sum(ot_ref[i] * dot_i, axis=0, keepdims=True)
            dot_b = dot_i.astype(BF16)
            lse_i = lse_ref[i]

            def step(j, dqt, q_i=q_i, do_i=do_i, delta=delta, dot_b=dot_b, lse_i=lse_i, diagonal=False):
                rows = slice(j * t, (j + 1) * t) if diagonal else pl.ds(pl.multiple_of(j * t, t), t)
                st = lax.dot_general(k_ref[rows, :], q_i, _NT, preferred_element_type=F32) * scale
                if diagonal:
                    st = jnp.where(causal, st, NEG)
                pt = jnp.exp(st - lse_i)
                dpt = jnp.dot(v_ref[rows, :], dot_b, preferred_element_type=F32)
                dst = (pt * (dpt - delta) * scale).astype(BF16)
                dv_ref[rows, :] += jnp.dot(pt.astype(BF16), do_i, preferred_element_type=F32)
                dk_ref[rows, :] += jnp.dot(dst, q_i, preferred_element_type=F32)
                return dqt + jnp.dot(kt_ref[j], dst, preferred_element_type=F32)

            dqt = jnp.zeros((dq, t), F32)
            if i > 0:
                dqt = lax.fori_loop(0, i, step, dqt)
            dqt_ref[i] = step(i, dqt, diagonal=True)

    def nat(d):
        return pl.BlockSpec((None, None, s, d), lambda bi, hi: (bi, hi, 0, 0))

    def blk(d):
        return pl.BlockSpec((None, None, nb, d, t), lambda bi, hi: (bi, hi, 0, 0, 0))

    return pl.pallas_call(
        body, name=name, grid=(b, h),
        out_shape=(jax.ShapeDtypeStruct((b, h, nb, dq, t), F32), jax.ShapeDtypeStruct((b, h, s, dq), F32),
                   jax.ShapeDtypeStruct((b, h, s, dv), F32)),
        in_specs=[nat(dq), nat(dq), blk(dq), nat(dv), nat(dv), blk(dv), blk(dv), blk(1)],
        out_specs=(blk(dq), nat(dq), nat(dv)), compiler_params=_params(("parallel", "parallel")),
    )(q, k, kt, v, do, dot, ot, lse)


_HBM = pl.BlockSpec(memory_space=pltpu.HBM)


def _place():
    x, y, c = lax.axis_index("x"), lax.axis_index("y"), lax.axis_index("c")
    chips = [(1 - x, y), (x, 1 - y), (1 - x, 1 - y)]
    return x, y, c, chips


def _remote(src, dst, send_sem, recv_sem, dev):
    return pltpu.make_async_remote_copy(src_ref=src, dst_ref=dst, send_sem=send_sem, recv_sem=recv_sem,
                                        device_id=dev, device_id_type=MESH)


def allgather_weights(packs):
    nk = len(packs)

    def body(*refs):
        p_refs, o_refs = refs[:nk], refs[nk:2 * nk]
        send_sems, recv_sems = refs[2 * nk:]
        x, y, c, chips = _place()
        s_me = 2 * x + y
        sibling = (x, y, 1 - c)
        first, passed, own = [], [], []
        for k in range(nk):
            for j, (cx, cy) in enumerate(chips):
                first.append(_remote(p_refs[k].at[c], o_refs[k].at[c, s_me], send_sems.at[j * nk + k], recv_sems.at[j * nk + k], (cx, cy, c)))
            for l in range(2):
                own.append(_remote(p_refs[k].at[l], o_refs[k].at[l, s_me], send_sems.at[6 * nk + 2 * k + l],
                                   recv_sems.at[6 * nk + 2 * k + l], sibling))
        for cp in first + own:
            cp.start()
        for j, (cx, cy) in enumerate(chips):
            for k in range(nk):
                blk = o_refs[k].at[c, 2 * cx + cy]
                _remote(blk, blk, send_sems.at[j * nk + k], recv_sems.at[j * nk + k], (cx, cy, c)).wait_recv()
                fwd = _remote(blk, blk, send_sems.at[(3 + j) * nk + k], recv_sems.at[(3 + j) * nk + k], sibling)
                fwd.start()
                passed.append(fwd)
        for j, (cx, cy) in enumerate(chips):
            for k in range(nk):
                blk = o_refs[k].at[1 - c, 2 * cx + cy]
                _remote(blk, blk, send_sems.at[(3 + j) * nk + k], recv_sems.at[(3 + j) * nk + k], sibling).wait_recv()
        for k in range(nk):
            for l in range(2):
                blk = o_refs[k].at[l, s_me]
                _remote(blk, blk, send_sems.at[6 * nk + 2 * k + l], recv_sems.at[6 * nk + 2 * k + l], sibling).wait_recv()
        for cp in first + own + passed:
            cp.wait_send()

    return pl.pallas_call(
        body, name="allgather_weights",
        out_shape=tuple(jax.ShapeDtypeStruct((2, N_SHARDS) + p.shape[1:], p.dtype) for p in packs),
        in_specs=[_HBM] * nk, out_specs=tuple([_HBM] * nk),
        scratch_shapes=[pltpu.SemaphoreType.DMA((8 * nk,)), pltpu.SemaphoreType.DMA((8 * nk,))],
    )(*packs)


def sibling_swap(grads):
    nk = len(grads)

    def body(*refs):
        g_refs, o_refs = refs[:nk], refs[nk:2 * nk]
        send_sems, recv_sems = refs[2 * nk:]
        x, y, c, _ = _place()
        cps = [_remote(g_refs[k].at[1 - c], o_refs[k], send_sems.at[k], recv_sems.at[k], (x, y, 1 - c)) for k in range(nk)]
        for cp in cps:
            cp.start()
        for cp in cps:
            cp.wait()

    return pl.pallas_call(
        body, name="sibling_swap", out_shape=tuple(jax.ShapeDtypeStruct(g.shape[1:], g.dtype) for g in grads),
        in_specs=[_HBM] * nk, out_specs=tuple([_HBM] * nk),
        scratch_shapes=[pltpu.SemaphoreType.DMA((nk,)), pltpu.SemaphoreType.DMA((nk,))],
    )(*grads)


def add_own_slab(name, grads, other, c_idx, tr=256):
    _, n, cols = grads.shape
    tr = _tile(n, tr, 16)

    def body(c_ref, g_ref, o_ref, out_ref):
        out_ref[...] = (g_ref[...] + o_ref[...]).astype(out_ref.dtype)

    return pl.pallas_call(
        body, name=name, out_shape=jax.ShapeDtypeStruct((n, cols), BF16),
        grid_spec=pltpu.PrefetchScalarGridSpec(
            num_scalar_prefetch=1, grid=(n // tr,),
            in_specs=[pl.BlockSpec((None, tr, cols), lambda i, c_ref: (c_ref[0], i, 0)), pl.BlockSpec((tr, cols), lambda i, c_ref: (i, 0))],
            out_specs=pl.BlockSpec((tr, cols), lambda i, c_ref: (i, 0))),
        compiler_params=_params(("parallel",)),
    )(c_idx, grads, other)


def scatter_to_owners(parts):
    nk = len(parts)

    def body(*refs):
        p_refs, o_refs = refs[:nk], refs[nk:2 * nk]
        send_sems, recv_sems = refs[2 * nk:]
        x, y, c, chips = _place()
        s_me = 2 * x + y
        sends = [_remote(p_refs[k].at[2 * cx + cy], o_refs[k].at[s_me], send_sems.at[j * nk + k], recv_sems.at[j * nk + k], (cx, cy, c))
                 for j, (cx, cy) in enumerate(chips) for k in range(nk)]
        for cp in sends:
            cp.start()
        for j, (cx, cy) in enumerate(chips):
            for k in range(nk):
                slot = o_refs[k].at[2 * cx + cy]
                _remote(slot, slot, send_sems.at[j * nk + k], recv_sems.at[j * nk + k], (cx, cy, c)).wait_recv()
        for cp in sends:
            cp.wait_send()

    return pl.pallas_call(
        body, name="scatter_to_owners", out_shape=tuple(jax.ShapeDtypeStruct(p.shape, p.dtype) for p in parts),
        in_specs=[_HBM] * nk, out_specs=tuple([_HBM] * nk),
        scratch_shapes=[pltpu.SemaphoreType.DMA((3 * nk,)), pltpu.SemaphoreType.DMA((3 * nk,))],
    )(*parts)


def sum_slots(name, slots, part, place, tr=256):
    ns, r, cols = slots.shape
    tr = _tile(r, tr, 16)

    def body(pl_ref, s_ref, own_ref, o_ref):
        total = own_ref[...].astype(F32)
        for k in range(1, ns):
            total = total + s_ref[(pl_ref[0] + k) % ns].astype(F32)
        o_ref[...] = total

    return pl.pallas_call(
        body, name=name, out_shape=jax.ShapeDtypeStruct((2, r, cols), F32),
        grid_spec=pltpu.PrefetchScalarGridSpec(
            num_scalar_prefetch=1, grid=(r // tr,),
            in_specs=[pl.BlockSpec((ns, tr, cols), lambda i, p: (0, i, 0)), pl.BlockSpec((None, tr, cols), lambda i, p: (p[0], i, 0))],
            out_specs=pl.BlockSpec((None, tr, cols), lambda i, p: (p[1], i, 0))),
        compiler_params=_params(("parallel",)),
    )(place, slots, part)


def sibling_share(boths):
    nk = len(boths)

    def body(*refs):
        b_refs, o_refs = refs[:nk], refs[nk:2 * nk]
        send_sems, recv_sems = refs[2 * nk:]
        x, y, c, _ = _place()
        cps = [_remote(b_refs[k].at[c], o_refs[k].at[c], send_sems.at[k], recv_sems.at[k], (x, y, 1 - c)) for k in range(nk)]
        for cp in cps:
            cp.start()
        for k in range(nk):
            theirs = o_refs[k].at[1 - c]
            _remote(theirs, theirs, send_sems.at[k], recv_sems.at[k], (x, y, 1 - c)).wait_recv()
        for cp in cps:
            cp.wait_send()

    return pl.pallas_call(
        body, name="sibling_share", out_shape=tuple(jax.ShapeDtypeStruct(b.shape, b.dtype) for b in boths),
        in_specs=[_HBM] * nk, out_specs=tuple([_HBM] * nk), input_output_aliases={k: k for k in range(nk)},
        scratch_shapes=[pltpu.SemaphoreType.DMA((nk,)), pltpu.SemaphoreType.DMA((nk,))],
    )(*boths)


def small_allreduce(vec):
    r, cols = vec.shape

    def body(v_ref, o_ref, buf, send_sems, recv_sems):
        x, y, c, _ = _place()
        me = 4 * x + 2 * y + c
        buf[me] = v_ref[...]
        peers = []
        for k in range(1, N_DEV):
            px = (1 - x) if (k & 4) else x
            py = (1 - y) if (k & 2) else y
            pc = (1 - c) if (k & 1) else c
            peers.append((px, py, pc))
        sends = [_remote(buf.at[me], buf.at[me], send_sems.at[k], recv_sems.at[k], peer) for k, peer in enumerate(peers)]
        for cp in sends:
            cp.start()
        for k, (px, py, pc) in enumerate(peers):
            slot = buf.at[4 * px + 2 * py + pc]
            _remote(slot, slot, send_sems.at[k], recv_sems.at[k], (px, py, pc)).wait_recv()
        for cp in sends:
            cp.wait_send()
        total = buf[0]
        for k in range(1, N_DEV):
            total = total + buf[k]
        o_ref[...] = total

    return pl.pallas_call(
        body, name="small_allreduce", out_shape=jax.ShapeDtypeStruct((r, cols), F32),
        in_specs=[pl.BlockSpec(memory_space=pltpu.VMEM)], out_specs=pl.BlockSpec(memory_space=pltpu.VMEM),
        scratch_shapes=[pltpu.VMEM((N_DEV, r, cols), F32), pltpu.SemaphoreType.DMA((N_DEV - 1,)), pltpu.SemaphoreType.DMA((N_DEV - 1,))],
    )(vec)


ARG_NAMES = (("x", "positions") + WEIGHT_NAMES + ("loss_target",) + tuple("m_" + n for n in WEIGHT_NAMES)
             + tuple("v_" + n for n in WEIGHT_NAMES))

_O_QA, _O_KA, _O_VA = 0, WIDTH_A, WIDTH_A + KV_WIDTH_A
_O_CQ = _O_VA + KV_WIDTH_A
_O_CKV = _O_CQ + Q_LORA_RANK
_O_GA = _O_CKV + KV_LORA_RANK


def _join_cols(shards):
    ns, r, c = shards.shape
    return shards.transpose(1, 0, 2).reshape(r, ns * c)


def _split_cols(mat):
    r, c4 = mat.shape
    return mat.reshape(r, N_SHARDS, c4 // N_SHARDS).transpose(1, 0, 2)


def _heads(z, b, s, h, d):
    return z.reshape(b, s, h, d).transpose(0, 2, 1, 3)


def _unheads(z):
    b, h, s, d = z.shape
    return z.transpose(0, 2, 1, 3).reshape(b * s, h * d)


def _ffn_fwd(tag, x, gain, wts, l, f):
    nb = rms_fwd(tag + "_rms", x, gain)
    gu, h = ffn_up_act(tag + "_up", nb, wts["col"], l, 2 * f)
    out = gmm_down(tag + "_down", h, wts["down"], l, f, x, 0.5)
    return out, (x, nb, gu, h)


def _ffn_bwd(tag, saved, gain, wts, gbuf, l, f, dout, doutb):
    x, nb, gu, h = saved
    n = gu.shape[-1]
    dgu = ffn_down_dact(tag + "_dh", doutb, wts["down"], gu, n, l, f, 0.5)
    gbuf["down"] = gmm_down_dw(tag + "_dwd", h, doutb, gbuf["down"], wts["down"].shape, l, f, 0.5)
    gbuf["col"] = gmm_up_dw(tag + "_dwgu", nb, dgu, gbuf["col"], wts["col"].shape, l, 2 * f)
    dn = gmm_up_dx(tag + "_dn", dgu, wts["col"], x.shape[1], l, 2 * f)
    dx, dxb, dgain = rms_bwd(tag + "_drms", x, gain, dn, dres=dout)
    return dx, dxb, dgain


def _mixer_fwd(tag, x1, sm, w, wts, l, aux):
    b, s = aux["b"], aux["s"]
    d = x1.shape[1]
    o_gb, o_kr = _O_GA + d, _O_GA + 2 * d
    hb = rms_fwd(tag + "_rms", x1, sm["mix_norm"])
    proj = mm_nn(tag + "_proj", hb, w["win"], tm=512, tn=1664, tk=1024)
    qa_raw = _heads(proj[:, _O_QA:_O_KA], b, s, N_HEADS_A, HEAD_DIM_A)
    ka_raw = _heads(proj[:, _O_KA:_O_VA], b, s, N_KV_HEADS_A, HEAD_DIM_A)
    va = _heads(proj[:, _O_VA:_O_CQ], b, s, N_KV_HEADS_A, HEAD_DIM_A).astype(BF16)
    cq, ckv, kr = proj[:, _O_CQ:_O_CKV], proj[:, _O_CKV:_O_GA], proj[:, o_kr:o_kr + QK_ROPE_DIM]
    cqn = rms_fwd(tag + "_rms_cq", cq, sm["mla_q_lora_norm"], tm=1024)
    ckvn = rms_fwd(tag + "_rms_ckv", ckv, sm["mla_kv_lora_norm"], tm=1024)
    qb_raw = _heads(mm_nn(tag + "_uq", cqn, w["wuq"], tm=1024), b, s, N_HEADS_B, QK_DIM_B)
    kv = mm_nn(tag + "_ukv", ckvn, w["wukv"], tm=1024).reshape(b, s, N_HEADS_B, QK_NOPE_DIM + V_DIM_B).transpose(0, 2, 1, 3)
    vb = kv[..., QK_NOPE_DIM:].astype(BF16)
    kr_b = jnp.broadcast_to(kr.reshape(b, 1, s, QK_ROPE_DIM), (b, N_HEADS_B, s, QK_ROPE_DIM))
    kfull_raw = jnp.concatenate([kv[..., :QK_NOPE_DIM], kr_b], axis=-1)
    qah = qk_prep_fwd(tag + "_qa_norm", qa_raw, sm["swa_q_norm"])
    kah = qk_prep_fwd(tag + "_ka_norm", ka_raw, sm["swa_k_norm"])
    oa = swa_fwd(tag + "_swa", qah, kah, va, aux["pos_col"], aux["pos_row"], aux["slopes"], sm["swa_sinks"].reshape(-1))
    qbh = qk_prep_fwd(tag + "_qb_norm", qb_raw, sm["mla_q_norm"], aux["rope"])
    kbh = qk_prep_fwd(tag + "_kb_norm", kfull_raw, sm["mla_k_norm"], aux["rope"])
    mla_t = MLA_T if s % MLA_T == 0 else BLOCK
    obt, lse = mla_fwd(tag + "_mla", qbh, kbh, _blocks_t(vb, mla_t))
    oab = _unheads(oa).astype(BF16)
    obb = obt.transpose(0, 2, 4, 1, 3).reshape(b * s, WIDTH_B).astype(BF16)
    ya = mm_nn(tag + "_branch_a", oab, w["wa"], tm=1024)
    yb = mm_nn(tag + "_branch_b", obb, w["wb"], tm=1024)
    mg = gate_fwd(tag + "_gate", proj, ya, yb, _O_GA, o_gb)
    x2 = gmm_rows(tag + "_out", mg, wts["wo"], l, x1)
    saved = dict(x1=x1, hb=hb, proj=proj, qa_raw=qa_raw, ka_raw=ka_raw, va=va, cq=cq, ckv=ckv, cqn=cqn, ckvn=ckvn,
                 qb_raw=qb_raw, kfull_raw=kfull_raw, vb=vb, qah=qah, kah=kah, qbh=qbh, kbh=kbh, obt=obt, lse=lse,
                 oab=oab, obb=obb, ya=ya, yb=yb, mg=mg)
    return x2, saved


def _mixer_bwd(tag, sv, sm, w, wts, gbuf, l, aux, dx2, dx2b):
    b, s = aux["b"], aux["s"]
    t, d = dx2.shape
    o_gb = _O_GA + d
    dmg = gmm_rows_dx(tag + "_d_out", dx2b, wts["wo"], l)
    gbuf["wo"] = gmm_rows_dw(tag + "_dw_out", sv["mg"], dx2b, gbuf["wo"], wts["wo"].shape, l)
    dya, dyb, dga, dgb = gate_bwd(tag + "_dgate", sv["proj"], sv["ya"], sv["yb"], dmg, _O_GA, o_gb)
    dwa = mm_tn(tag + "_dw_branch_a", sv["oab"], dya)
    dwb = mm_tn(tag + "_dw_branch_b", sv["obb"], dyb)
    doa = _heads(mm_nt(tag + "_d_branch_a", dya, w["wa"], tm=1024), b, s, N_HEADS_A, HEAD_DIM_A)
    dob = _heads(mm_nt(tag + "_d_branch_b", dyb, w["wb"], tm=1024), b, s, N_HEADS_B, V_DIM_B)
    mla_t = sv["obt"].shape[-1]
    sinks = sm["swa_sinks"].reshape(-1)
    dqah, dkah, dva, dsinks = swa_bwd(tag + "_dswa", sv["qah"], sv["kah"], sv["va"], aux["pos_col"], aux["pos_row"],
                                      aux["slopes"], sinks, doa)
    dqa_raw, dg_swa_q = qk_prep_bwd(tag + "_dqa_norm", sv["qa_raw"], sm["swa_q_norm"], dqah)
    dka_raw, dg_swa_k = qk_prep_bwd(tag + "_dka_norm", sv["ka_raw"], sm["swa_k_norm"], dkah)
    dqbt, dkbh, dvb = mla_bwd(tag + "_dmla", sv["qbh"], sv["kbh"], _blocks_t(sv["kbh"], mla_t), sv["vb"], dob.astype(BF16),
                              _blocks_t(dob, mla_t), sv["obt"], sv["lse"])
    dqbh = _unblocks_t(dqbt)
    dqb_raw, dg_mla_q = qk_prep_bwd(tag + "_dqb_norm", sv["qb_raw"], sm["mla_q_norm"], dqbh, aux["rope"])
    dkfull, dkr_sum, dg_mla_k = qk_prep_bwd(tag + "_dkb_norm", sv["kfull_raw"], sm["mla_k_norm"], dkbh, aux["rope"], head_sum=True)
    dq_tok = _unheads(dqb_raw)
    dkv_tok = _unheads(jnp.concatenate([dkfull[..., :QK_NOPE_DIM], dvb.astype(BF16)], axis=-1))
    dwuq = mm_tn(tag + "_dw_uq", sv["cqn"], dq_tok, tk=1024)
    dwukv = mm_tn(tag + "_dw_ukv", sv["ckvn"], dkv_tok, tk=1024)
    dcqn = mm_nt(tag + "_d_uq", dq_tok, w["wuq"], tm=1024)
    dckvn = mm_nt(tag + "_d_ukv", dkv_tok, w["wukv"], tm=1024)
    dcq, dg_q_lora = rms_bwd(tag + "_drms_cq", sv["cq"], sm["mla_q_lora_norm"], dcqn, want_f32=False, tm=1024)
    dckv, dg_kv_lora = rms_bwd(tag + "_drms_ckv", sv["ckv"], sm["mla_kv_lora_norm"], dckvn, want_f32=False, tm=1024)
    dkr = dkr_sum[..., QK_NOPE_DIM:].reshape(t, QK_ROPE_DIM).astype(BF16)
    parts = [_unheads(dqa_raw), _unheads(dka_raw), _unheads(dva.astype(BF16)), dcq, dckv, dga, dgb, dkr]
    pad = w["win"].shape[1] - sum(p.shape[1] for p in parts)
    if pad:
        parts.append(jnp.zeros((t, pad), BF16))
    dproj = jnp.concatenate(parts, axis=1)
    dwin = mm_tn(tag + "_dw_in", sv["hb"], dproj, tm=1024, tn=1664, tk=512)
    dh = mm_nt(tag + "_d_in", dproj, w["win"], tm=512, tn=1024, tk=1664)
    dx1, dx1b, dg_mix = rms_bwd(tag + "_drms", sv["x1"], sm["mix_norm"], dh, dres=dx2)
    wgrads = dict(win=dwin, wuq=dwuq, wukv=dwukv, wa=dwa, wb=dwb)
    sgrads = dict(mix_norm=dg_mix, swa_q_norm=dg_swa_q, swa_k_norm=dg_swa_k, swa_sinks=dsinks[:, 0].reshape(1, -1),
                  mla_q_lora_norm=dg_q_lora, mla_kv_lora_norm=dg_kv_lora, mla_q_norm=dg_mla_q, mla_k_norm=dg_mla_k)
    return dx1, dx1b, wgrads, sgrads


def _layer_weights(win4, uq4, ukv4, branch4):
    win_ref = _join_cols(win4)
    d, width = win_ref.shape
    parts = [win_ref[:, :_O_GA], win_ref[:, _O_GA + QK_ROPE_DIM:], win_ref[:, _O_GA:_O_GA + QK_ROPE_DIM]]
    padded = -(-width // (2 * LANES)) * (2 * LANES)
    if padded > width:
        parts.append(jnp.zeros((d, padded - width), BF16))
    r = branch4.shape[1] // 2
    return dict(win=jnp.concatenate(parts, axis=1), wuq=_join_cols(uq4), wukv=_join_cols(ukv4),
                wa=_join_cols(branch4[:, :r]), wb=_join_cols(branch4[:, r:]))


def _mixer_grad_shards(g, width):
    win = g["win"]
    win_ref = jnp.concatenate([win[:, :_O_GA], win[:, width - QK_ROPE_DIM:width], win[:, _O_GA:width - QK_ROPE_DIM]], axis=1)
    return (_split_cols(win_ref), _split_cols(g["wuq"]), _split_cols(g["wukv"]),
            jnp.concatenate([_split_cols(g["wa"]), _split_cols(g["wb"])], axis=1))


def _local_step(x, positions, target, wts, weights, small, depth):
    b, s, d = x.shape
    t = b * s
    posf = positions.astype(F32)
    half = QK_ROPE_DIM // 2
    inv_freq = ROPE_BASE ** (-jnp.arange(half, dtype=F32) / half)
    ang = posf[..., None] * inv_freq
    cos, sin = jnp.cos(ang), jnp.sin(ang)
    rope = (jnp.concatenate([jnp.ones((b, s, QK_NOPE_DIM), F32), cos, cos], axis=-1),
            jnp.concatenate([jnp.zeros((b, s, QK_NOPE_DIM), F32), sin, sin], axis=-1))
    slopes = jnp.exp2(-8.0 * (jnp.arange(N_HEADS_A, dtype=F32) + 1.0) / N_HEADS_A)
    aux = dict(b=b, s=s, pos_col=posf.reshape(b, s, 1), pos_row=posf.reshape(b, 1, s), rope=rope, slopes=slopes)

    def sm_of(l):
        return {n: small[n][l:l + 1] for n in SMALL_NAMES}

    h = x.reshape(t, d)
    saved = []
    for l in range(depth):
        sm, w = sm_of(l), weights[l]
        h, s1 = _ffn_fwd(f"l{l}_ffn1", h, sm["ffn1_norm"], wts, l, 0)
        h, s2 = _mixer_fwd(f"l{l}_mix", h, sm, w, wts, l, aux)
        h, s3 = _ffn_fwd(f"l{l}_ffn2", h, sm["ffn2_norm"], wts, l, 1)
        saved.append((s1, s2, s3))
    loss, dh, dhb = loss_fwd_bwd("loss", h, target.reshape(t, d))
    wgrads, sgrads = [None] * depth, [None] * depth
    gbuf = dict(col=None, down=None, wo=None)
    for l in reversed(range(depth)):
        sm, w = sm_of(l), weights[l]
        s1, s2, s3 = saved[l]
        dh, dhb, dg_f2 = _ffn_bwd(f"l{l}_ffn2", s3, sm["ffn2_norm"], wts, gbuf, l, 1, dh, dhb)
        dh, dhb, wg, sg = _mixer_bwd(f"l{l}_mix", s2, sm, w, wts, gbuf, l, aux, dh, dhb)
        dh, dhb, dg_f1 = _ffn_bwd(f"l{l}_ffn1", s1, sm["ffn1_norm"], wts, gbuf, l, 0, dh, dhb)
        sg.update(ffn1_norm=dg_f1, ffn2_norm=dg_f2)
        wgrads[l], sgrads[l] = wg, sg
    return loss, dh.reshape(b, s, d), gbuf, wgrads, sgrads


def kernel(x, positions, ffn1_norm, ffn1_w_gate, ffn1_w_up, ffn1_w_down, mix_norm, w_in, swa_q_norm, swa_k_norm, swa_sinks, mla_q_lora_norm, mla_w_uq, mla_kv_lora_norm, mla_w_ukv, mla_q_norm, mla_k_norm, w_branch_a, w_branch_b, w_out, ffn2_norm, ffn2_w_gate, ffn2_w_up, ffn2_w_down, loss_target, m_ffn1_norm, m_ffn1_w_gate, m_ffn1_w_up, m_ffn1_w_down, m_mix_norm, m_w_in, m_swa_q_norm, m_swa_k_norm, m_swa_sinks, m_mla_q_lora_norm, m_mla_w_uq, m_mla_kv_lora_norm, m_mla_w_ukv, m_mla_q_norm, m_mla_k_norm, m_w_branch_a, m_w_branch_b, m_w_out, m_ffn2_norm, m_ffn2_w_gate, m_ffn2_w_up, m_ffn2_w_down, v_ffn1_norm, v_ffn1_w_gate, v_ffn1_w_up, v_ffn1_w_down, v_mix_norm, v_w_in, v_swa_q_norm, v_swa_k_norm, v_swa_sinks, v_mla_q_lora_norm, v_mla_w_uq, v_mla_kv_lora_norm, v_mla_w_ukv, v_mla_q_norm, v_mla_k_norm, v_w_branch_a, v_w_branch_b, v_w_out, v_ffn2_norm, v_ffn2_w_gate, v_ffn2_w_up, v_ffn2_w_down):
    args = (x, positions, ffn1_norm, ffn1_w_gate, ffn1_w_up, ffn1_w_down, mix_norm, w_in, swa_q_norm, swa_k_norm, swa_sinks, mla_q_lora_norm, mla_w_uq, mla_kv_lora_norm, mla_w_ukv, mla_q_norm, mla_k_norm, w_branch_a, w_branch_b, w_out, ffn2_norm, ffn2_w_gate, ffn2_w_up, ffn2_w_down, loss_target, m_ffn1_norm, m_ffn1_w_gate, m_ffn1_w_up, m_ffn1_w_down, m_mix_norm, m_w_in, m_swa_q_norm, m_swa_k_norm, m_swa_sinks, m_mla_q_lora_norm, m_mla_w_uq, m_mla_kv_lora_norm, m_mla_w_ukv, m_mla_q_norm, m_mla_k_norm, m_w_branch_a, m_w_branch_b, m_w_out, m_ffn2_norm, m_ffn2_w_gate, m_ffn2_w_up, m_ffn2_w_down, v_ffn1_norm, v_ffn1_w_gate, v_ffn1_w_up, v_ffn1_w_down, v_mix_norm, v_w_in, v_swa_q_norm, v_swa_k_norm, v_swa_sinks, v_mla_q_lora_norm, v_mla_w_uq, v_mla_kv_lora_norm, v_mla_w_ukv, v_mla_q_norm, v_mla_k_norm, v_w_branch_a, v_w_branch_b, v_w_out, v_ffn2_norm, v_ffn2_w_gate, v_ffn2_w_up, v_ffn2_w_down)
    a = dict(zip(ARG_NAMES, args, strict=True))
    x = a["x"]
    depth = a["ffn1_norm"].shape[0]
    d = x.shape[-1]
    assert depth == 2, "one layer slab per core of a chip"

    tags = ("col", "down", "wo", "win", "uq", "ukv", "branch")
    packs = [jnp.concatenate([a[n] for n in COL_NAMES], axis=1).astype(BF16),
             jnp.concatenate([a[n] for n in DOWN_NAMES], axis=1).astype(BF16),
             a["w_out"].astype(BF16), a["w_in"].astype(BF16), a["mla_w_uq"].astype(BF16), a["mla_w_ukv"].astype(BF16),
             jnp.concatenate([a["w_branch_a"], a["w_branch_b"]], axis=1).astype(BF16)]
    shard_me = (2 * lax.axis_index("x") + lax.axis_index("y")).astype(jnp.int32)
    core_me = lax.axis_index("c").astype(jnp.int32)
    col, down, wo, win4, uq4, ukv4, branch4 = allgather_weights(packs)
    weights = [_layer_weights(win4[l], uq4[l], ukv4[l], branch4[l]) for l in range(depth)]
    small = {n: a[n] for n in SMALL_NAMES}

    loss, grad_x, gbuf, wgrads, sgrads = _local_step(x, a["positions"], a["loss_target"], dict(col=col, down=down, wo=wo),
                                                     weights, small, depth)

    width_in = N_SHARDS * a["w_in"].shape[2]
    mixer = [_mixer_grad_shards(wgrads[l], width_in) for l in range(depth)]
    gbufs = [gbuf["col"], gbuf["down"], gbuf["wo"]] + [jnp.stack([mixer[l][k] for l in range(depth)]) for k in range(4)]
    theirs = sibling_swap(gbufs)
    parts = [add_own_slab("add_own_" + tag, g.reshape(depth, -1, g.shape[-1]), o.reshape(-1, o.shape[-1]), core_me.reshape(1)).reshape(o.shape)
             for tag, g, o in zip(tags, gbufs, theirs)]
    slots = scatter_to_owners(parts)
    place = jnp.stack([shard_me, core_me])
    fcol, fdown, fwo, fwin, fuq, fukv, fbranch = sibling_share(
        [sum_slots("sum_" + tag, sl, p, place) for tag, sl, p in zip(tags, slots, parts)])
    n_ff, r_branch = a["ffn1_w_down"].shape[1], a["w_branch_a"].shape[1]
    grads = {n: fcol[:, i * d:(i + 1) * d] for i, n in enumerate(COL_NAMES)}
    grads.update({n: fdown[:, i * n_ff:(i + 1) * n_ff] for i, n in enumerate(DOWN_NAMES)})
    grads.update(w_out=fwo, w_in=fwin, mla_w_uq=fuq, mla_w_ukv=fukv, w_branch_a=fbranch[:, :r_branch], w_branch_b=fbranch[:, r_branch:])

    flat = jnp.concatenate([jnp.concatenate([sgrads[l][n].reshape(-1) for l in range(depth)]) for n in SMALL_NAMES] + [loss.reshape(-1)])
    n_small = flat.shape[0]
    rows = -(-n_small // (8 * LANES)) * 8
    pad = rows * LANES - n_small

    def small_pack(v):
        return jnp.pad(v, (0, pad)).reshape(rows, LANES)

    total = small_allreduce(small_pack(flat))
    w_s, m_s, v_s = (small_pack(jnp.concatenate([a[p + n].reshape(-1) for n in SMALL_NAMES] + [jnp.zeros((1,), F32)]))
                     for p in ("", "m_", "v_"))
    d_s, nm_s, nv_s = adamw("adamw_small", w_s, total, m_s, v_s)

    def small_unpack(buf):
        out, off, flat_b = {}, 0, buf.reshape(-1)
        for n in SMALL_NAMES:
            size = a[n].shape[0] * a[n].shape[1]
            out[n] = flat_b[off:off + size].reshape(a[n].shape)
            off += size
        return out

    grads.update(small_unpack(total))
    delta, new_m, new_v = small_unpack(d_s), small_unpack(nm_s), small_unpack(nv_s)
    for n in PACK_NAMES:
        shp = a[n].shape
        two_d = (shp[0] * shp[1], shp[2])
        dn, mn, vn = adamw("adamw_" + n, a[n].reshape(two_d), grads[n].reshape(two_d), a["m_" + n].reshape(two_d), a["v_" + n].reshape(two_d))
        delta[n], new_m[n], new_v[n] = dn.reshape(shp), mn.reshape(shp), vn.reshape(shp)

    loss_out = total.reshape(-1)[n_small - 1]
    return (loss_out, grad_x, *[grads[n] for n in WEIGHT_NAMES], *[delta[n] for n in WEIGHT_NAMES],
            *[new_m[n] for n in WEIGHT_NAMES], *[new_v[n] for n in WEIGHT_NAMES])
```

```python
import functools

import numpy as np
import jax
import jax.numpy as jnp
from jax import lax
from jax.experimental import pallas as pl
from jax.experimental.pallas import tpu as pltpu

F32 = jnp.float32
BF16 = jnp.bfloat16
MESH = pl.DeviceIdType.MESH

HEAD_DIM_A = 64
N_HEADS_A = 8
N_KV_HEADS_A = 2
GROUP_A = N_HEADS_A // N_KV_HEADS_A
BLOCK = 128
N_HEADS_B = 8
Q_LORA_RANK = 256
KV_LORA_RANK = 128
QK_NOPE_DIM = 64
QK_ROPE_DIM = 32
QK_DIM_B = QK_NOPE_DIM + QK_ROPE_DIM
V_DIM_B = 64
ROPE_BASE = 10000.0
WIDTH_A = N_HEADS_A * HEAD_DIM_A
WIDTH_B = N_HEADS_B * V_DIM_B
KV_WIDTH_A = N_KV_HEADS_A * HEAD_DIM_A
EPS = 1e-6
NEG = -1e30
ADAM_LR = 0.001
ADAM_B1 = 0.9
ADAM_B2 = 0.999
ADAM_EPS = 1e-08
ADAM_WD = 0.01
ADAM_STEP = 10

N_SHARDS = 4
N_DEV = 8
LANES = 128
VMEM_LIMIT = 48 * 1024 * 1024

PACK_NAMES = ("ffn1_w_gate", "ffn1_w_up", "ffn1_w_down", "w_in", "mla_w_uq", "mla_w_ukv",
              "w_branch_a", "w_branch_b", "w_out", "ffn2_w_gate", "ffn2_w_up", "ffn2_w_down")
COL_NAMES = ("ffn1_w_gate", "ffn1_w_up", "ffn2_w_gate", "ffn2_w_up")
DOWN_NAMES = ("ffn1_w_down", "ffn2_w_down")
SMALL_NAMES = ("ffn1_norm", "mix_norm", "swa_q_norm", "swa_k_norm", "swa_sinks", "mla_q_lora_norm",
               "mla_kv_lora_norm", "mla_q_norm", "mla_k_norm", "ffn2_norm")
WEIGHT_NAMES = ("ffn1_norm", "ffn1_w_gate", "ffn1_w_up", "ffn1_w_down", "mix_norm", "w_in", "swa_q_norm",
                "swa_k_norm", "swa_sinks", "mla_q_lora_norm", "mla_w_uq", "mla_kv_lora_norm", "mla_w_ukv",
                "mla_q_norm", "mla_k_norm", "w_branch_a", "w_branch_b", "w_out", "ffn2_norm", "ffn2_w_gate",
                "ffn2_w_up", "ffn2_w_down")


def _params(sem):
    return pltpu.CompilerParams(dimension_semantics=sem, vmem_limit_bytes=VMEM_LIMIT)


def _tile(n, want, align):
    if n <= want:
        return n
    t = (want // align) * align
    while t > align and n % t:
        t -= align
    assert t >= align and n % t == 0, (n, want, align)
    return t


def _mm_call(name, a, b, out_struct, grid, a_spec, b_spec, o_spec, dims, n_red, acc_shape, alpha=1.0, res=None, into=None):
    n_par = len(grid) - n_red

    def body(*refs):
        a_ref, b_ref = refs[:2]
        r_ref = refs[2] if res is not None else None
        o_ref, acc_ref = refs[-2:]
        part = lax.dot_general(a_ref[...], b_ref[...], dims, preferred_element_type=F32)

        def finish(total):
            if alpha != 1.0:
                total = total * alpha
            if r_ref is not None:
                total = r_ref[...] + total
            o_ref[...] = total.astype(o_ref.dtype)

        if n_red == 0:
            finish(part)
            return
        ids = [pl.program_id(n_par + i) for i in range(n_red)]
        first = functools.reduce(jnp.logical_and, [i == 0 for i in ids])
        last = functools.reduce(jnp.logical_and, [i == grid[n_par + k] - 1 for k, i in enumerate(ids)])

        @pl.when(first)
        def _():
            acc_ref[...] = part

        @pl.when(jnp.logical_not(first))
        def _():
            acc_ref[...] += part

        @pl.when(last)
        def _():
            finish(acc_ref[...])

    in_specs = [a_spec, b_spec] + ([o_spec] if res is not None else [])
    args = (a, b) + ((res,) if res is not None else ())
    aliases = {}
    if into is not None:
        aliases = {len(args): 0}
        in_specs.append(pl.BlockSpec(memory_space=pl.ANY))
        args = args + (into,)
    return pl.pallas_call(
        body, name=name, out_shape=out_struct, grid=grid, in_specs=in_specs, out_specs=o_spec,
        scratch_shapes=[pltpu.VMEM(acc_shape, F32)], input_output_aliases=aliases,
        compiler_params=_params(("parallel",) * n_par + ("arbitrary",) * n_red),
    )(*args)


_NN = (((1,), (0,)), ((), ()))
_NT = (((1,), (1,)), ((), ()))
_TN = (((0,), (0,)), ((), ()))


def mm_nn(name, a, b, tm=512, tn=1024, tk=1024, out_dtype=F32, alpha=1.0, res=None):
    (m, k), (_, n) = a.shape, b.shape
    tm, tn, tk = _tile(m, tm, 16), _tile(n, tn, LANES), _tile(k, tk, LANES)
    return _mm_call(name, a, b, jax.ShapeDtypeStruct((m, n), out_dtype), (m // tm, n // tn, k // tk),
                    pl.BlockSpec((tm, tk), lambda i, j, kk: (i, kk)), pl.BlockSpec((tk, tn), lambda i, j, kk: (kk, j)),
                    pl.BlockSpec((tm, tn), lambda i, j, kk: (i, j)), _NN, 1, (tm, tn), alpha, res)


def mm_nt(name, a, b, tm=512, tn=1024, tk=1024, out_dtype=F32, alpha=1.0):
    (m, n), (k, _) = a.shape, b.shape
    tm, tn, tk = _tile(m, tm, 16), _tile(k, tn, LANES), _tile(n, tk, LANES)
    return _mm_call(name, a, b, jax.ShapeDtypeStruct((m, k), out_dtype), (m // tm, k // tn, n // tk),
                    pl.BlockSpec((tm, tk), lambda i, j, kk: (i, kk)), pl.BlockSpec((tn, tk), lambda i, j, kk: (j, kk)),
                    pl.BlockSpec((tm, tn), lambda i, j, kk: (i, j)), _NT, 1, (tm, tn), alpha)


def mm_tn(name, a, b, tm=1024, tn=1024, tk=512, out_dtype=F32, alpha=1.0):
    (m, k), (_, n) = a.shape, b.shape
    tm, tn, tk = _tile(k, tm, LANES), _tile(n, tn, LANES), _tile(m, tk, 16)
    return _mm_call(name, a, b, jax.ShapeDtypeStruct((k, n), out_dtype), (k // tm, n // tn, m // tk),
                    pl.BlockSpec((tk, tm), lambda i, j, kk: (kk, i)), pl.BlockSpec((tk, tn), lambda i, j, kk: (kk, j)),
                    pl.BlockSpec((tm, tn), lambda i, j, kk: (i, j)), _TN, 1, (tm, tn), alpha)


def ffn_up_act(name, a, w, blk, tm=512):
    (m, k), (ns, _, n) = a.shape, w.shape
    tm = _tile(m, tm, 16)

    def body(a_ref, wg_ref, wu_ref, gu_ref, h_ref):
        av = a_ref[...]
        gate = jnp.dot(av, wg_ref[...], preferred_element_type=F32)
        up = jnp.dot(av, wu_ref[...], preferred_element_type=F32)
        gu_ref[0] = gate.astype(BF16)
        gu_ref[1] = up.astype(BF16)
        h_ref[...] = (gate * jax.nn.sigmoid(gate) * up).astype(BF16)

    return pl.pallas_call(
        body, name=name, grid=(ns, m // tm),
        out_shape=(jax.ShapeDtypeStruct((2, ns, m, n), BF16), jax.ShapeDtypeStruct((ns, m, n), BF16)),
        in_specs=[pl.BlockSpec((tm, k), lambda s, i: (i, 0)), pl.BlockSpec((None, k, n), lambda s, i: (s, blk, 0)),
                  pl.BlockSpec((None, k, n), lambda s, i: (s, blk + 1, 0))],
        out_specs=(pl.BlockSpec((2, None, tm, n), lambda s, i: (0, s, i, 0)), pl.BlockSpec((None, tm, n), lambda s, i: (s, i, 0))),
        compiler_params=_params(("parallel", "parallel")),
    )(a, w, w)


def ffn_down_dact(name, a, w, gu, n, blk, alpha, tm=512):
    (m, d), ns = a.shape, w.shape[0]
    tm = _tile(m, tm, 16)

    def body(a_ref, w_ref, gu_ref, o_ref):
        dh = lax.dot_general(a_ref[...], w_ref[...], _NT, preferred_element_type=F32) * alpha
        gate, up = gu_ref[0].astype(F32), gu_ref[1].astype(F32)
        s = jax.nn.sigmoid(gate)
        o_ref[0] = (dh * up * (s * (1.0 + gate * (1.0 - s)))).astype(BF16)
        o_ref[1] = (dh * (gate * s)).astype(BF16)

    gu_spec = pl.BlockSpec((2, None, tm, n), lambda s, i: (0, s, i, 0))
    return pl.pallas_call(
        body, name=name, grid=(ns, m // tm), out_shape=jax.ShapeDtypeStruct((2, ns, m, n), BF16),
        in_specs=[pl.BlockSpec((tm, d), lambda s, i: (i, 0)), pl.BlockSpec((None, n, d), lambda s, i: (s, blk, 0)), gu_spec],
        out_specs=gu_spec, compiler_params=_params(("parallel", "parallel")),
    )(a, w, gu)


def gmm_up_dw(name, a, dgu, into, shape, blk, tk=512):
    (m, k), (_, ns, _, n) = a.shape, dgu.shape
    tk = _tile(m, tk, 16)
    return _mm_call(name, a, dgu, jax.ShapeDtypeStruct(shape, F32), (2, ns, m // tk),
                    pl.BlockSpec((tk, k), lambda j, s, kk: (kk, 0)), pl.BlockSpec((None, None, tk, n), lambda j, s, kk: (j, s, kk, 0)),
                    pl.BlockSpec((None, k, n), lambda j, s, kk: (s, blk + j, 0)), _TN, 1, (k, n), into=into)


def gmm_up_dx(name, dgu, w, k, blk, tm=512):
    _, ns, m, n = dgu.shape
    tm = _tile(m, tm, 16)
    return _mm_call(name, dgu, w, jax.ShapeDtypeStruct((m, k), F32), (m // tm, 2, ns),
                    pl.BlockSpec((None, None, tm, n), lambda i, j, s: (j, s, i, 0)),
                    pl.BlockSpec((None, k, n), lambda i, j, s: (s, blk + j, 0)),
                    pl.BlockSpec((tm, k), lambda i, j, s: (i, 0)), _NT, 2, (tm, k))


def gmm_down(name, h, w, blk, res, alpha, tm=512):
    (ns, m, n), d = h.shape, w.shape[2]
    tm = _tile(m, tm, 16)
    return _mm_call(name, h, w, jax.ShapeDtypeStruct((m, d), F32), (m // tm, ns),
                    pl.BlockSpec((None, tm, n), lambda i, s: (s, i, 0)), pl.BlockSpec((None, n, d), lambda i, s: (s, blk, 0)),
                    pl.BlockSpec((tm, d), lambda i, s: (i, 0)), _NN, 1, (tm, d), alpha, res)


def gmm_down_dw(name, h, b, into, shape, blk, alpha, tk=512):
    (ns, m, n), d = h.shape, b.shape[1]
    tk = _tile(m, tk, 16)
    return _mm_call(name, h, b, jax.ShapeDtypeStruct(shape, F32), (ns, m // tk),
                    pl.BlockSpec((None, tk, n), lambda s, kk: (s, kk, 0)), pl.BlockSpec((tk, d), lambda s, kk: (kk, 0)),
                    pl.BlockSpec((None, n, d), lambda s, kk: (s, blk, 0)), _TN, 1, (n, d), alpha, into=into)


def gmm_rows(name, a, w, res, tm=512):
    (m, _), (ns, r, d) = a.shape, w.shape
    tm = _tile(m, tm, 16)
    return _mm_call(name, a, w, jax.ShapeDtypeStruct((m, d), F32), (m // tm, ns),
                    pl.BlockSpec((tm, r), lambda i, s: (i, s)), pl.BlockSpec((None, r, d), lambda i, s: (s, 0, 0)),
                    pl.BlockSpec((tm, d), lambda i, s: (i, 0)), _NN, 1, (tm, d), 1.0, res)


def gmm_rows_dx(name, a, w, tm=512):
    (m, d), (ns, r, _) = a.shape, w.shape
    tm = _tile(m, tm, 16)
    return _mm_call(name, a, w, jax.ShapeDtypeStruct((m, ns * r), F32), (ns, m // tm),
                    pl.BlockSpec((tm, d), lambda s, i: (i, 0)), pl.BlockSpec((None, r, d), lambda s, i: (s, 0, 0)),
                    pl.BlockSpec((tm, r), lambda s, i: (i, s)), _NT, 0, (8, LANES))


def gmm_rows_dw(name, a, b, tk=512):
    (m, da), d = a.shape, b.shape[1]
    ns, r = N_SHARDS, da // N_SHARDS
    tk = _tile(m, tk, 16)
    return _mm_call(name, a, b, jax.ShapeDtypeStruct((ns, r, d), F32), (ns, m // tk),
                    pl.BlockSpec((tk, r), lambda s, kk: (kk, s)), pl.BlockSpec((tk, d), lambda s, kk: (kk, 0)),
                    pl.BlockSpec((None, r, d), lambda s, kk: (s, 0, 0)), _TN, 1, (r, d))


def rms_fwd(name, x, gain, tm=512):
    m, d = x.shape
    tm = _tile(m, tm, 16)

    def body(x_ref, g_ref, o_ref):
        xv = x_ref[...]
        r = lax.rsqrt(jnp.mean(xv * xv, axis=-1, keepdims=True) + EPS)
        o_ref[...] = (xv * r * g_ref[...]).astype(o_ref.dtype)

    return pl.pallas_call(
        body, name=name, out_shape=jax.ShapeDtypeStruct((m, d), BF16), grid=(m // tm,),
        in_specs=[pl.BlockSpec((tm, d), lambda i: (i, 0)), pl.BlockSpec((1, d), lambda i: (0, 0))],
        out_specs=pl.BlockSpec((tm, d), lambda i: (i, 0)), compiler_params=_params(("parallel",)),
    )(x, gain)


def rms_bwd(name, x, gain, dn, dres=None, want_f32=True, want_bf16=True, tm=512):
    m, d = x.shape
    tm = _tile(m, tm, 16)
    n_out = int(want_f32) + int(want_bf16)

    def body(*refs):
        x_ref, g_ref, dn_ref = refs[:3]
        pos = 3
        r_ref = None
        if dres is not None:
            r_ref = refs[pos]
            pos += 1
        outs = refs[pos:pos + n_out]
        dg_ref = refs[pos + n_out]
        xv = x_ref[...]
        r = lax.rsqrt(jnp.mean(xv * xv, axis=-1, keepdims=True) + EPS)
        xhat = xv * r
        dnv = dn_ref[...]
        dxhat = dnv * g_ref[...]
        dx = r * (dxhat - xhat * jnp.mean(dxhat * xhat, axis=-1, keepdims=True))
        if r_ref is not None:
            dx = r_ref[...] + dx
        for o in outs:
            o[...] = dx.astype(o.dtype)
        part = jnp.sum(dnv * xhat, axis=0, keepdims=True)

        @pl.when(pl.program_id(0) == 0)
        def _():
            dg_ref[...] = part

        @pl.when(pl.program_id(0) > 0)
        def _():
            dg_ref[...] += part

    row = pl.BlockSpec((tm, d), lambda i: (i, 0))
    vec = pl.BlockSpec((1, d), lambda i: (0, 0))
    out_shape = ([jax.ShapeDtypeStruct((m, d), F32)] if want_f32 else []) + ([jax.ShapeDtypeStruct((m, d), BF16)] if want_bf16 else [])
    res = pl.pallas_call(
        body, name=name, out_shape=tuple(out_shape) + (jax.ShapeDtypeStruct((1, d), F32),), grid=(m // tm,),
        in_specs=[row, vec, row] + ([row] if dres is not None else []),
        out_specs=tuple([row] * n_out) + (vec,), compiler_params=_params(("arbitrary",)),
    )(*((x, gain, dn) + ((dres,) if dres is not None else ())))
    return res


def gate_fwd(name, proj, ya, yb, off_a, off_b, tm=256):
    m, d = ya.shape
    w = proj.shape[1]
    tm = _tile(m, tm, 16)

    def body(p_ref, ya_ref, yb_ref, o_ref):
        ga, gb = p_ref[:, off_a:off_a + d], p_ref[:, off_b:off_b + d]
        o_ref[...] = (jax.nn.sigmoid(ga) * ya_ref[...] + jax.nn.sigmoid(gb) * yb_ref[...]).astype(o_ref.dtype)

    row = pl.BlockSpec((tm, d), lambda i: (i, 0))
    return pl.pallas_call(
        body, name=name, out_shape=jax.ShapeDtypeStruct((m, d), BF16), grid=(m // tm,),
        in_specs=[pl.BlockSpec((tm, w), lambda i: (i, 0)), row, row], out_specs=row, compiler_params=_params(("parallel",)),
    )(proj, ya, yb)


def gate_bwd(name, proj, ya, yb, dmerged, off_a, off_b, tm=256):
    m, d = ya.shape
    w = proj.shape[1]
    tm = _tile(m, tm, 16)

    def body(p_ref, ya_ref, yb_ref, dm_ref, dya_ref, dyb_ref, dga_ref, dgb_ref):
        sa, sb = jax.nn.sigmoid(p_ref[:, off_a:off_a + d]), jax.nn.sigmoid(p_ref[:, off_b:off_b + d])
        dm = dm_ref[...]
        dya_ref[...] = (dm * sa).astype(BF16)
        dyb_ref[...] = (dm * sb).astype(BF16)
        dga_ref[...] = (dm * ya_ref[...] * (sa * (1.0 - sa))).astype(BF16)
        dgb_ref[...] = (dm * yb_ref[...] * (sb * (1.0 - sb))).astype(BF16)

    row = pl.BlockSpec((tm, d), lambda i: (i, 0))
    o = jax.ShapeDtypeStruct((m, d), BF16)
    return pl.pallas_call(
        body, name=name, out_shape=(o, o, o, o), grid=(m // tm,),
        in_specs=[pl.BlockSpec((tm, w), lambda i: (i, 0)), row, row, row], out_specs=(row, row, row, row),
        compiler_params=_params(("parallel",)),
    )(proj, ya, yb, dmerged)


def loss_fwd_bwd(name, y, target, tm=512):
    m, d = y.shape
    tm = _tile(m, tm, 16)

    def body(y_ref, t_ref, l_ref, dy_ref, dyb_ref):
        err = y_ref[...] - t_ref[...]
        dy = err * (1.0 / d)
        dy_ref[...] = dy
        dyb_ref[...] = dy.astype(BF16)
        part = 0.5 * jnp.sum(jnp.mean(err * err, axis=-1, keepdims=True), axis=0, keepdims=True)

        @pl.when(pl.program_id(0) == 0)
        def _():
            l_ref[...] = part

        @pl.when(pl.program_id(0) > 0)
        def _():
            l_ref[...] += part

    row = pl.BlockSpec((tm, d), lambda i: (i, 0))
    return pl.pallas_call(
        body, name=name, grid=(m // tm,),
        out_shape=(jax.ShapeDtypeStruct((1, 1), F32), jax.ShapeDtypeStruct((m, d), F32), jax.ShapeDtypeStruct((m, d), BF16)),
        in_specs=[row, row], out_specs=(pl.BlockSpec((1, 1), lambda i: (0, 0)), row, row),
        compiler_params=_params(("arbitrary",)),
    )(y, target)


def adamw(name, w, g, m, v):
    r, c = w.shape
    tr = _tile(r, max(8, (2 * 1024 * 1024) // (4 * c) // 8 * 8), 8)
    c1 = np.float32(1.0 - ADAM_B1 ** ADAM_STEP)
    c2 = np.float32(1.0 - ADAM_B2 ** ADAM_STEP)

    def body(w_ref, g_ref, m_ref, v_ref, d_ref, nm_ref, nv_ref):
        gv = g_ref[...]
        nm = ADAM_B1 * m_ref[...] + (1.0 - ADAM_B1) * gv
        nv = ADAM_B2 * v_ref[...] + (1.0 - ADAM_B2) * (gv * gv)
        d_ref[...] = -ADAM_LR * ((nm / c1) / (jnp.sqrt(nv / c2) + ADAM_EPS) + ADAM_WD * w_ref[...])
        nm_ref[...] = nm
        nv_ref[...] = nv

    row = pl.BlockSpec((tr, c), lambda i: (i, 0))
    o = jax.ShapeDtypeStruct((r, c), F32)
    return pl.pallas_call(
        body, name=name, out_shape=(o, o, o), grid=(r // tr,), in_specs=[row] * 4, out_specs=(row, row, row),
        compiler_params=_params(("parallel",)),
    )(w, g, m, v)


def _rope_rot(d):
    r = np.zeros((d, d), np.float32)
    half = QK_ROPE_DIM // 2
    base = d - QK_ROPE_DIM
    for j in range(half):
        r[base + half + j, base + j] = -1.0
        r[base + j, base + half + j] = 1.0
    return r


def qk_prep_fwd(name, x, gain, rope=None):
    b, h, s, d = x.shape

    def body(*refs):
        if rope is None:
            x_ref, g_ref, o_ref = refs
        else:
            x_ref, g_ref, c_ref, s_ref, r_ref, o_ref = refs
        xv = x_ref[...]
        y = xv * lax.rsqrt(jnp.mean(xv * xv, axis=-1, keepdims=True) + EPS) * g_ref[...]
        if rope is not None:
            rot = jnp.dot(y, r_ref[...], precision=lax.Precision.HIGHEST, preferred_element_type=F32)
            y = y * c_ref[...] + rot * s_ref[...]
        o_ref[...] = y.astype(o_ref.dtype)

    xs = pl.BlockSpec((None, None, s, d), lambda bi, hi: (bi, hi, 0, 0))
    in_specs = [xs, pl.BlockSpec((1, d), lambda bi, hi: (0, 0))]
    args = [x, gain]
    if rope is not None:
        tab = pl.BlockSpec((None, s, d), lambda bi, hi: (bi, 0, 0))
        in_specs += [tab, tab, pl.BlockSpec((d, d), lambda bi, hi: (0, 0))]
        args += [rope[0], rope[1], jnp.asarray(_rope_rot(d))]
    return pl.pallas_call(
        body, name=name, out_shape=jax.ShapeDtypeStruct(x.shape, BF16), grid=(b, h), in_specs=in_specs, out_specs=xs,
        compiler_params=_params(("parallel", "parallel")),
    )(*args)


def qk_prep_bwd(name, x, gain, dxh, rope=None, head_sum=False):
    b, h, s, d = x.shape

    def body(*refs):
        if rope is None:
            x_ref, g_ref, dy_ref = refs[:3]
            outs = refs[3:]
        else:
            x_ref, g_ref, dy_ref, c_ref, s_ref, rt_ref = refs[:6]
            outs = refs[6:]
        dx_ref, dg_ref = outs[0], outs[-1]
        dy = dy_ref[...]
        if rope is not None:
            dy = dy * c_ref[...] + jnp.dot(dy * s_ref[...], rt_ref[...], precision=lax.Precision.HIGHEST, preferred_element_type=F32)
        xv = x_ref[...]
        r = lax.rsqrt(jnp.mean(xv * xv, axis=-1, keepdims=True) + EPS)
        xhat = xv * r
        dxhat = dy * g_ref[...]
        dx = r * (dxhat - xhat * jnp.mean(dxhat * xhat, axis=-1, keepdims=True))
        dx_ref[...] = dx.astype(dx_ref.dtype)
        part = jnp.sum(dy * xhat, axis=0, keepdims=True)
        first = jnp.logical_and(pl.program_id(0) == 0, pl.program_id(1) == 0)

        @pl.when(first)
        def _():
            dg_ref[...] = part

        @pl.when(jnp.logical_not(first))
        def _():
            dg_ref[...] += part

        if head_sum:
            hs_ref = outs[1]

            @pl.when(pl.program_id(1) == 0)
            def _():
                hs_ref[...] = dx

            @pl.when(pl.program_id(1) > 0)
            def _():
                hs_ref[...] += dx

    xs = pl.BlockSpec((None, None, s, d), lambda bi, hi: (bi, hi, 0, 0))
    vec = pl.BlockSpec((1, d), lambda bi, hi: (0, 0))
    tab = pl.BlockSpec((None, s, d), lambda bi, hi: (bi, 0, 0))
    in_specs = [xs, vec, xs]
    args = [x, gain, dxh]
    if rope is not None:
        in_specs += [tab, tab, pl.BlockSpec((d, d), lambda bi, hi: (0, 0))]
        args += [rope[0], rope[1], jnp.asarray(_rope_rot(d).T.copy())]
    out_shape = [jax.ShapeDtypeStruct(x.shape, BF16)]
    out_specs = [xs]
    if head_sum:
        out_shape.append(jax.ShapeDtypeStruct((b, s, d), F32))
        out_specs.append(tab)
    out_shape.append(jax.ShapeDtypeStruct((1, d), F32))
    out_specs.append(vec)
    return pl.pallas_call(
        body, name=name, out_shape=tuple(out_shape), grid=(b, h), in_specs=in_specs, out_specs=tuple(out_specs),
        compiler_params=_params(("arbitrary", "arbitrary")),
    )(*args)


def _swa_specs(s):
    nb = s // BLOCK
    q = pl.BlockSpec((None, GROUP_A, BLOCK, HEAD_DIM_A), lambda b, kv, n: (b, kv, n, 0))
    kprev = pl.BlockSpec((None, None, BLOCK, HEAD_DIM_A), lambda b, kv, n: (b, kv, jnp.maximum(n - 1, 0), 0))
    kcur = pl.BlockSpec((None, None, BLOCK, HEAD_DIM_A), lambda b, kv, n: (b, kv, n, 0))
    pcol = pl.BlockSpec((None, BLOCK, 1), lambda b, kv, n: (b, n, 0))
    prow_prev = pl.BlockSpec((None, 1, BLOCK), lambda b, kv, n: (b, 0, jnp.maximum(n - 1, 0)))
    prow_cur = pl.BlockSpec((None, 1, BLOCK), lambda b, kv, n: (b, 0, n))
    smem = pl.BlockSpec(memory_space=pltpu.SMEM)
    return nb, q, kprev, kcur, pcol, prow_prev, prow_cur, smem


def _swa_probs(q, kk, dist, valid, slope, sink):
    sc = lax.dot_general(q, kk, _NT, preferred_element_type=F32) * (HEAD_DIM_A ** -0.5)
    sc = sc - slope * dist
    sc = jnp.where(valid, sc, NEG)
    m = jnp.maximum(jnp.max(sc, axis=-1, keepdims=True), sink)
    e = jnp.exp(sc - m)
    es = jnp.exp(sink - m)
    inv = 1.0 / (jnp.sum(e, axis=-1, keepdims=True) + es)
    return e * inv, es * inv


def _swa_window(n, kp_ref, kc_ref, vp_ref, vc_ref, pc_ref, prp_ref, prc_ref):
    kk = jnp.concatenate([kp_ref[...], kc_ref[...]], axis=0)
    vv = jnp.concatenate([vp_ref[...], vc_ref[...]], axis=0)
    dist = pc_ref[...] - jnp.concatenate([prp_ref[...], prc_ref[...]], axis=1)
    qi = lax.broadcasted_iota(jnp.int32, (BLOCK, 2 * BLOCK), 0) + BLOCK
    ki = lax.broadcasted_iota(jnp.int32, (BLOCK, 2 * BLOCK), 1)
    diff = qi - ki
    valid = (diff >= 0) & (diff < BLOCK) & ((n > 0) | (ki >= BLOCK))
    return kk, vv, dist, valid


def swa_fwd(name, q, k, v, pos_col, pos_row, slopes, sinks):
    b, _, s, _ = q.shape
    nb, qs, kprev, kcur, pcol, prp, prc, smem = _swa_specs(s)

    def body(q_ref, kp_ref, kc_ref, vp_ref, vc_ref, pc_ref, prp_ref, prc_ref, sl_ref, sk_ref, o_ref):
        kv, n = pl.program_id(1), pl.program_id(2)
        kk, vv, dist, valid = _swa_window(n, kp_ref, kc_ref, vp_ref, vc_ref, pc_ref, prp_ref, prc_ref)
        for g in range(GROUP_A):
            hd = kv * GROUP_A + g
            p, _ = _swa_probs(q_ref[g], kk, dist, valid, sl_ref[hd], sk_ref[hd])
            o_ref[g] = jnp.dot(p.astype(BF16), vv, preferred_element_type=F32)

    return pl.pallas_call(
        body, name=name, out_shape=jax.ShapeDtypeStruct(q.shape, F32), grid=(b, N_KV_HEADS_A, nb),
        in_specs=[qs, kprev, kcur, kprev, kcur, pcol, prp, prc, smem, smem], out_specs=qs,
        compiler_params=_params(("parallel", "parallel", "parallel")),
    )(q, k, k, v, v, pos_col, pos_row, pos_row, slopes, sinks)


def swa_bwd(name, q, k, v, pos_col, pos_row, slopes, sinks, do):
    b, _, s, _ = q.shape
    nb, qs, kprev, kcur, pcol, prp, prc, smem = _swa_specs(s)

    def body(q_ref, kp_ref, kc_ref, vp_ref, vc_ref, pc_ref, prp_ref, prc_ref, sl_ref, sk_ref, do_ref, dq_ref, dk_ref, dv_ref, ds_ref):
        bi, kv, n = pl.program_id(0), pl.program_id(1), pl.program_id(2)
        kk, vv, dist, valid = _swa_window(n, kp_ref, kc_ref, vp_ref, vc_ref, pc_ref, prp_ref, prc_ref)

        @pl.when((bi == 0) & (kv == 0) & (n == 0))
        def _():
            ds_ref[...] = jnp.zeros_like(ds_ref)

        @pl.when(n == 0)
        def _():
            dk_ref[...] = jnp.zeros_like(dk_ref)
            dv_ref[...] = jnp.zeros_like(dv_ref)

        dkk = jnp.zeros((2 * BLOCK, HEAD_DIM_A), F32)
        dvv = jnp.zeros((2 * BLOCK, HEAD_DIM_A), F32)
        head_row = lax.broadcasted_iota(jnp.int32, (N_HEADS_A, LANES), 0)
        dsink = jnp.zeros((N_HEADS_A, LANES), F32)
        for g in range(GROUP_A):
            hd = kv * GROUP_A + g
            qg = q_ref[g]
            p, ps = _swa_probs(qg, kk, dist, valid, sl_ref[hd], sk_ref[hd])
            dob = do_ref[g].astype(BF16)
            dvv = dvv + lax.dot_general(p.astype(BF16), dob, _TN, preferred_element_type=F32)
            dp = lax.dot_general(dob, vv, _NT, preferred_element_type=F32)
            rs = jnp.sum(p * dp, axis=-1, keepdims=True)
            dsb = (p * (dp - rs) * (HEAD_DIM_A ** -0.5)).astype(BF16)
            dq_ref[g] = jnp.dot(dsb, kk, preferred_element_type=F32)
            dkk = dkk + lax.dot_general(dsb, qg, _TN, preferred_element_type=F32)
            dsink = dsink + jnp.where(head_row == hd, -jnp.sum(ps * rs), 0.0)
        ds_ref[...] += dsink

        @pl.when(n > 0)
        def _():
            start = pl.multiple_of((n - 1) * BLOCK, BLOCK)
            dk_ref[pl.ds(start, 2 * BLOCK), :] += dkk
            dv_ref[pl.ds(start, 2 * BLOCK), :] += dvv

        @pl.when(n == 0)
        def _():
            dk_ref[0:BLOCK, :] += dkk[BLOCK:, :]
            dv_ref[0:BLOCK, :] += dvv[BLOCK:, :]

    kv_full = pl.BlockSpec((None, None, s, HEAD_DIM_A), lambda bi, kv, n: (bi, kv, 0, 0))
    return pl.pallas_call(
        body, name=name, grid=(b, N_KV_HEADS_A, nb),
        out_shape=(jax.ShapeDtypeStruct(q.shape, F32), jax.ShapeDtypeStruct(k.shape, F32), jax.ShapeDtypeStruct(k.shape, F32),
                   jax.ShapeDtypeStruct((N_HEADS_A, LANES), F32)),
        in_specs=[qs, kprev, kcur, kprev, kcur, pcol, prp, prc, smem, smem, qs],
        out_specs=(qs, kv_full, kv_full, pl.BlockSpec((N_HEADS_A, LANES), lambda bi, kv, n: (0, 0))),
        compiler_params=_params(("arbitrary", "arbitrary", "arbitrary")),
    )(q, k, k, v, v, pos_col, pos_row, pos_row, slopes, sinks, do)


MLA_T = 256


def _blocks_t(z, t):
    b, h, s, d = z.shape
    return z.reshape(b, h, s // t, t, d).transpose(0, 1, 2, 4, 3)


def _unblocks_t(z):
    b, h, nb, d, t = z.shape
    return z.transpose(0, 1, 2, 4, 3).reshape(b, h, nb * t, d)


def _hosted(plan, n_in, n_out, b, h):
    if plan is None:
        return [], [], [], [], ("parallel", "parallel"), (lambda refs: None), (lambda refs: None)
    ni, no = len(plan.ins), len(plan.out_shapes)

    def split(refs):
        return refs[n_in:n_in + ni], refs[n_in + ni + n_out:n_in + ni + n_out + no], refs[-2], refs[-1]

    def at_start(refs):
        @pl.when((pl.program_id(0) == 0) & (pl.program_id(1) == 0))
        def _():
            plan.start(*split(refs))

    def at_end(refs):
        @pl.when((pl.program_id(0) == b - 1) & (pl.program_id(1) == h - 1))
        def _():
            plan.finish(*split(refs))

    return plan.ins, [_HBM] * ni, plan.out_shapes, [_HBM] * no, ("arbitrary", "arbitrary"), at_start, at_end


def mla_fwd(name, q, k, vt, plan=None):
    b, h, s, dq = q.shape
    _, _, nb, dv, t = vt.shape
    scale = QK_DIM_B ** -0.5
    p_ins, p_in_specs, p_outs, p_out_specs, semantics, at_start, at_end = _hosted(plan, 3, 2, b, h)

    def body(*refs):
        q_ref, k_ref, vt_ref = refs[:3]
        o_ref, lse_ref = refs[3 + len(p_ins):5 + len(p_ins)]
        at_start(refs)
        causal = lax.broadcasted_iota(jnp.int32, (t, t), 0) <= lax.broadcasted_iota(jnp.int32, (t, t), 1)
        for i in range(nb):
            q_i = q_ref[i * t:(i + 1) * t, :]

            def step(j, carry, q_i=q_i, diagonal=False):
                m, l, acc = carry
                rows = slice(j * t, (j + 1) * t) if diagonal else pl.ds(pl.multiple_of(j * t, t), t)
                st = lax.dot_general(k_ref[rows, :], q_i, _NT, preferred_element_type=F32) * scale
                if diagonal:
                    st = jnp.where(causal, st, NEG)
                m_new = jnp.maximum(m, jnp.max(st, axis=0, keepdims=True))
                a = jnp.exp(m - m_new)
                p = jnp.exp(st - m_new)
                l = a * l + jnp.sum(p, axis=0, keepdims=True)
                acc = a * acc + jnp.dot(vt_ref[j], p.astype(BF16), preferred_element_type=F32)
                return m_new, l, acc

            carry = (jnp.full((1, t), NEG, F32), jnp.zeros((1, t), F32), jnp.zeros((dv, t), F32))
            if i > 0:
                carry = lax.fori_loop(0, i, step, carry)
            m, l, acc = step(i, carry, diagonal=True)
            o_ref[i] = acc / l
            lse_ref[i] = m + jnp.log(l)
        at_end(refs)

    qs = pl.BlockSpec((None, None, s, dq), lambda bi, hi: (bi, hi, 0, 0))
    ts = pl.BlockSpec((None, None, nb, dv, t), lambda bi, hi: (bi, hi, 0, 0, 0))
    ls = pl.BlockSpec((None, None, nb, 1, t), lambda bi, hi: (bi, hi, 0, 0, 0))
    res = pl.pallas_call(
        body, name=name, grid=(b, h),
        out_shape=(jax.ShapeDtypeStruct((b, h, nb, dv, t), F32), jax.ShapeDtypeStruct((b, h, nb, 1, t), F32)) + tuple(p_outs),
        in_specs=[qs, qs, ts] + p_in_specs, out_specs=(ts, ls) + tuple(p_out_specs),
        scratch_shapes=plan.scratch() if plan is not None else [], compiler_params=_params(semantics),
    )(q, k, vt, *p_ins)
    return res[0], res[1], list(res[2:])


def mla_bwd(name, q, k, kt, v, do, dot, ot, lse, plan=None):
    b, h, s, dq = q.shape
    _, _, nb, dv, t = ot.shape
    scale = QK_DIM_B ** -0.5
    p_ins, p_in_specs, p_outs, p_out_specs, semantics, at_start, at_end = _hosted(plan, 8, 3, b, h)

    def body(*refs):
        q_ref, k_ref, kt_ref, v_ref, do_ref, dot_ref, ot_ref, lse_ref = refs[:8]
        dqt_ref, dk_ref, dv_ref = refs[8 + len(p_ins):11 + len(p_ins)]
        at_start(refs)
        causal = lax.broadcasted_iota(jnp.int32, (t, t), 0) <= lax.broadcasted_iota(jnp.int32, (t, t), 1)
        dk_ref[...] = jnp.zeros_like(dk_ref)
        dv_ref[...] = jnp.zeros_like(dv_ref)
        for i in range(nb):
            q_i = q_ref[i * t:(i + 1) * t, :]
            do_i = do_ref[i * t:(i + 1) * t, :]
            dot_i = dot_ref[i]
            delta = jnp.sum(ot_ref[i] * dot_i, axis=0, keepdims=True)
            dot_b = dot_i.astype(BF16)
            lse_i = lse_ref[i]

            def step(j, dqt, q_i=q_i, do_i=do_i, delta=delta, dot_b=dot_b, lse_i=lse_i, diagonal=False):
                rows = slice(j * t, (j + 1) * t) if diagonal else pl.ds(pl.multiple_of(j * t, t), t)
                st = lax.dot_general(k_ref[rows, :], q_i, _NT, preferred_element_type=F32) * scale
                if diagonal:
                    st = jnp.where(causal, st, NEG)
                pt = jnp.exp(st - lse_i)
                dpt = jnp.dot(v_ref[rows, :], dot_b, preferred_element_type=F32)
                dst = (pt * (dpt - delta) * scale).astype(BF16)
                dv_ref[rows, :] += jnp.dot(pt.astype(BF16), do_i, preferred_element_type=F32)
                dk_ref[rows, :] += jnp.dot(dst, q_i, preferred_element_type=F32)
                return dqt + jnp.dot(kt_ref[j], dst, preferred_element_type=F32)

            dqt = jnp.zeros((dq, t), F32)
            if i > 0:
                dqt = lax.fori_loop(0, i, step, dqt)
            dqt_ref[i] = step(i, dqt, diagonal=True)
        at_end(refs)

    def nat(d):
        return pl.BlockSpec((None, None, s, d), lambda bi, hi: (bi, hi, 0, 0))

    def blk(d):
        return pl.BlockSpec((None, None, nb, d, t), lambda bi, hi: (bi, hi, 0, 0, 0))

    res = pl.pallas_call(
        body, name=name, grid=(b, h),
        out_shape=(jax.ShapeDtypeStruct((b, h, nb, dq, t), F32), jax.ShapeDtypeStruct((b, h, s, dq), F32),
                   jax.ShapeDtypeStruct((b, h, s, dv), F32)) + tuple(p_outs),
        in_specs=[nat(dq), nat(dq), blk(dq), nat(dv), nat(dv), blk(dv), blk(dv), blk(1)] + p_in_specs,
        out_specs=(blk(dq), nat(dq), nat(dv)) + tuple(p_out_specs),
        scratch_shapes=plan.scratch() if plan is not None else [], compiler_params=_params(semantics),
    )(q, k, kt, v, do, dot, ot, lse, *p_ins)
    return res[0], res[1], res[2], list(res[3:])


_HBM = pl.BlockSpec(memory_space=pltpu.HBM)


def _place():
    x, y, c = lax.axis_index("x"), lax.axis_index("y"), lax.axis_index("c")
    chips = [(1 - x, y), (x, 1 - y), (1 - x, 1 - y)]
    return x, y, c, chips


def _remote(src, dst, send_sem, recv_sem, dev):
    return pltpu.make_async_remote_copy(src_ref=src, dst_ref=dst, send_sem=send_sem, recv_sem=recv_sem,
                                        device_id=dev, device_id_type=MESH)


class CommPlan:
    def __init__(self, ins, out_shapes, n_sems, start, finish, aliases=None):
        self.ins, self.out_shapes, self.n_sems = list(ins), list(out_shapes), n_sems
        self.start, self.finish, self.aliases = start, finish, dict(aliases or {})

    def scratch(self):
        return [pltpu.SemaphoreType.DMA((self.n_sems,)), pltpu.SemaphoreType.DMA((self.n_sems,))]


def comm_call(name, plan):
    ni, no = len(plan.ins), len(plan.out_shapes)

    def body(*refs):
        ins, outs, (send_sems, recv_sems) = refs[:ni], refs[ni:ni + no], refs[ni + no:]
        plan.start(ins, outs, send_sems, recv_sems)
        plan.finish(ins, outs, send_sems, recv_sems)

    return pl.pallas_call(
        body, name=name, out_shape=tuple(plan.out_shapes), in_specs=[_HBM] * ni, out_specs=tuple([_HBM] * no),
        input_output_aliases=plan.aliases, scratch_shapes=plan.scratch(),
    )(*plan.ins)


def gather_plan(packs, l):
    nk = len(packs)

    def parts(p_refs, o_refs, ss, rs):
        x, y, c, chips = _place()
        s_me = 2 * x + y
        sibling = (x, y, 1 - c)
        first = [_remote(p_refs[k].at[l], o_refs[k].at[s_me], ss.at[j * nk + k], rs.at[j * nk + k], (cx, cy, c))
                 for j, (cx, cy) in enumerate(chips) for k in range(nk)]
        own = [_remote(p_refs[k].at[l], o_refs[k].at[s_me], ss.at[6 * nk + k], rs.at[6 * nk + k], sibling) for k in range(nk)]
        return c, chips, sibling, first, own

    def start(p_refs, o_refs, ss, rs):
        c, _, _, first, own = parts(p_refs, o_refs, ss, rs)
        for cp in own:
            cp.start()

        @pl.when(c == l)
        def _():
            for cp in first:
                cp.start()

    def finish(p_refs, o_refs, ss, rs):
        c, chips, sibling, first, own = parts(p_refs, o_refs, ss, rs)

        @pl.when(c == l)
        def _():
            passed = []
            for j, (cx, cy) in enumerate(chips):
                for k in range(nk):
                    blk = o_refs[k].at[2 * cx + cy]
                    _remote(blk, blk, ss.at[j * nk + k], rs.at[j * nk + k], (cx, cy, c)).wait_recv()
                    fwd = _remote(blk, blk, ss.at[(3 + j) * nk + k], rs.at[(3 + j) * nk + k], sibling)
                    fwd.start()
                    passed.append(fwd)
            for cp in first + passed:
                cp.wait_send()

        @pl.when(c != l)
        def _():
            for j, (cx, cy) in enumerate(chips):
                for k in range(nk):
                    blk = o_refs[k].at[2 * cx + cy]
                    _remote(blk, blk, ss.at[(3 + j) * nk + k], rs.at[(3 + j) * nk + k], sibling).wait_recv()

        for cp in own:
            cp.wait()

    outs = [jax.ShapeDtypeStruct((N_SHARDS,) + p.shape[1:], p.dtype) for p in packs]
    return CommPlan(packs, outs, 7 * nk, start, finish)


def swap_plan(grads, l):
    nk = len(grads)

    def copies(g_refs, o_refs, ss, rs):
        x, y, c, _ = _place()
        return c, [_remote(g_refs[k], o_refs[k], ss.at[k], rs.at[k], (x, y, 1 - c)) for k in range(nk)]

    def start(g_refs, o_refs, ss, rs):
        c, cps = copies(g_refs, o_refs, ss, rs)

        @pl.when(c != l)
        def _():
            for cp in cps:
                cp.start()

    def finish(g_refs, o_refs, ss, rs):
        c, cps = copies(g_refs, o_refs, ss, rs)

        @pl.when(c != l)
        def _():
            for cp in cps:
                cp.wait_send()

        @pl.when(c == l)
        def _():
            for cp in cps:
                cp.wait_recv()

    return CommPlan(grads, [jax.ShapeDtypeStruct(g.shape, g.dtype) for g in grads], nk, start, finish)


def scatter_plan(parts, l):
    nk = len(parts)

    def copies(p_refs, o_refs, ss, rs):
        x, y, c, chips = _place()
        s_me = 2 * x + y
        sends = [_remote(p_refs[k].at[2 * cx + cy], o_refs[k].at[s_me], ss.at[j * nk + k], rs.at[j * nk + k], (cx, cy, c))
                 for j, (cx, cy) in enumerate(chips) for k in range(nk)]
        return c, chips, sends

    def start(p_refs, o_refs, ss, rs):
        c, _, sends = copies(p_refs, o_refs, ss, rs)

        @pl.when(c == l)
        def _():
            for cp in sends:
                cp.start()

    def finish(p_refs, o_refs, ss, rs):
        c, chips, sends = copies(p_refs, o_refs, ss, rs)

        @pl.when(c == l)
        def _():
            for j, (cx, cy) in enumerate(chips):
                for k in range(nk):
                    slot = o_refs[k].at[2 * cx + cy]
                    _remote(slot, slot, ss.at[j * nk + k], rs.at[j * nk + k], (cx, cy, c)).wait_recv()
            for cp in sends:
                cp.wait_send()

    return CommPlan(parts, [jax.ShapeDtypeStruct(p.shape, p.dtype) for p in parts], 3 * nk, start, finish)


def share_plan(finals):
    flat = [(l, f) for l, fs in enumerate(finals) for f in fs]

    def copies(f_refs, o_refs, ss, rs):
        x, y, c, _ = _place()
        return c, [_remote(f_refs[i], o_refs[i], ss.at[i], rs.at[i], (x, y, 1 - c)) for i in range(len(flat))]

    def start(f_refs, o_refs, ss, rs):
        c, cps = copies(f_refs, o_refs, ss, rs)
        for l in range(len(finals)):
            @pl.when(c == l)
            def _(l=l):
                for cp, (lf, _) in zip(cps, flat):
                    if lf == l:
                        cp.start()

    def finish(f_refs, o_refs, ss, rs):
        c, cps = copies(f_refs, o_refs, ss, rs)
        for l in range(len(finals)):
            @pl.when(c == l)
            def _(l=l):
                for cp, (lf, _) in zip(cps, flat):
                    if lf == l:
                        cp.wait_send()

            @pl.when(c != l)
            def _(l=l):
                for cp, (lf, _) in zip(cps, flat):
                    if lf == l:
                        cp.wait_recv()

    arrays = [f for _, f in flat]
    return CommPlan(arrays, [jax.ShapeDtypeStruct(f.shape, f.dtype) for f in arrays], len(flat), start, finish,
                    aliases={i: i for i in range(len(flat))})


def add_pair(name, own, other, tr=256):
    n, cols = own.shape
    tr = _tile(n, tr, 16)

    def body(g_ref, o_ref, out_ref):
        out_ref[...] = (g_ref[...] + o_ref[...]).astype(out_ref.dtype)

    row = pl.BlockSpec((tr, cols), lambda i: (i, 0))
    return pl.pallas_call(
        body, name=name, out_shape=jax.ShapeDtypeStruct((n, cols), BF16), grid=(n // tr,), in_specs=[row, row], out_specs=row,
        compiler_params=_params(("parallel",)),
    )(own, other)


def sum_slots(name, slots, part, shard, tr=256):
    ns, r, cols = slots.shape
    tr = _tile(r, tr, 16)

    def body(s_idx, s_ref, own_ref, o_ref):
        total = own_ref[...].astype(F32)
        for k in range(1, ns):
            total = total + s_ref[(s_idx[0] + k) % ns].astype(F32)
        o_ref[...] = total

    return pl.pallas_call(
        body, name=name, out_shape=jax.ShapeDtypeStruct((r, cols), F32),
        grid_spec=pltpu.PrefetchScalarGridSpec(
            num_scalar_prefetch=1, grid=(r // tr,),
            in_specs=[pl.BlockSpec((ns, tr, cols), lambda i, p: (0, i, 0)), pl.BlockSpec((None, tr, cols), lambda i, p: (p[0], i, 0))],
            out_specs=pl.BlockSpec((tr, cols), lambda i, p: (i, 0))),
        compiler_params=_params(("parallel",)),
    )(shard, slots, part)


def small_allreduce(vec):
    r, cols = vec.shape

    def body(v_ref, o_ref, buf, send_sems, recv_sems):
        x, y, c, _ = _place()
        me = 4 * x + 2 * y + c
        buf[me] = v_ref[...]
        peers = []
        for k in range(1, N_DEV):
            px = (1 - x) if (k & 4) else x
            py = (1 - y) if (k & 2) else y
            pc = (1 - c) if (k & 1) else c
            peers.append((px, py, pc))
        sends = [_remote(buf.at[me], buf.at[me], send_sems.at[k], recv_sems.at[k], peer) for k, peer in enumerate(peers)]
        for cp in sends:
            cp.start()
        for k, (px, py, pc) in enumerate(peers):
            slot = buf.at[4 * px + 2 * py + pc]
            _remote(slot, slot, send_sems.at[k], recv_sems.at[k], (px, py, pc)).wait_recv()
        for cp in sends:
            cp.wait_send()
        total = buf[0]
        for k in range(1, N_DEV):
            total = total + buf[k]
        o_ref[...] = total

    return pl.pallas_call(
        body, name="small_allreduce", out_shape=jax.ShapeDtypeStruct((r, cols), F32),
        in_specs=[pl.BlockSpec(memory_space=pltpu.VMEM)], out_specs=pl.BlockSpec(memory_space=pltpu.VMEM),
        scratch_shapes=[pltpu.VMEM((N_DEV, r, cols), F32), pltpu.SemaphoreType.DMA((N_DEV - 1,)), pltpu.SemaphoreType.DMA((N_DEV - 1,))],
    )(vec)


ARG_NAMES = (("x", "positions") + WEIGHT_NAMES + ("loss_target",) + tuple("m_" + n for n in WEIGHT_NAMES)
             + tuple("v_" + n for n in WEIGHT_NAMES))

_O_QA, _O_KA, _O_VA = 0, WIDTH_A, WIDTH_A + KV_WIDTH_A
_O_CQ = _O_VA + KV_WIDTH_A
_O_CKV = _O_CQ + Q_LORA_RANK
_O_GA = _O_CKV + KV_LORA_RANK


def _join_cols(shards):
    ns, r, c = shards.shape
    return shards.transpose(1, 0, 2).reshape(r, ns * c)


def _split_cols(mat):
    r, c4 = mat.shape
    return mat.reshape(r, N_SHARDS, c4 // N_SHARDS).transpose(1, 0, 2)


def _heads(z, b, s, h, d):
    return z.reshape(b, s, h, d).transpose(0, 2, 1, 3)


def _unheads(z):
    b, h, s, d = z.shape
    return z.transpose(0, 2, 1, 3).reshape(b * s, h * d)


def _ffn_fwd(tag, x, gain, wts, f):
    nb = rms_fwd(tag + "_rms", x, gain)
    gu, h = ffn_up_act(tag + "_up", nb, wts["col"], 2 * f)
    out = gmm_down(tag + "_down", h, wts["down"], f, x, 0.5)
    return out, (x, nb, gu, h)


def _ffn_bwd(tag, saved, gain, wts, gbuf, f, dout, doutb):
    x, nb, gu, h = saved
    n = gu.shape[-1]
    dgu = ffn_down_dact(tag + "_dh", doutb, wts["down"], gu, n, f, 0.5)
    gbuf["down"] = gmm_down_dw(tag + "_dwd", h, doutb, gbuf["down"], wts["down"].shape, f, 0.5)
    gbuf["col"] = gmm_up_dw(tag + "_dwgu", nb, dgu, gbuf["col"], wts["col"].shape, 2 * f)
    dn = gmm_up_dx(tag + "_dn", dgu, wts["col"], x.shape[1], 2 * f)
    dx, dxb, dgain = rms_bwd(tag + "_drms", x, gain, dn, dres=dout)
    return dx, dxb, dgain


def _mixer_fwd(tag, x1, sm, w, wts, aux, plan=None):
    b, s = aux["b"], aux["s"]
    d = x1.shape[1]
    o_gb, o_kr = _O_GA + d, _O_GA + 2 * d
    hb = rms_fwd(tag + "_rms", x1, sm["mix_norm"])
    proj = mm_nn(tag + "_proj", hb, w["win"], tm=512, tn=1664, tk=1024)
    qa_raw = _heads(proj[:, _O_QA:_O_KA], b, s, N_HEADS_A, HEAD_DIM_A)
    ka_raw = _heads(proj[:, _O_KA:_O_VA], b, s, N_KV_HEADS_A, HEAD_DIM_A)
    va = _heads(proj[:, _O_VA:_O_CQ], b, s, N_KV_HEADS_A, HEAD_DIM_A).astype(BF16)
    cq, ckv, kr = proj[:, _O_CQ:_O_CKV], proj[:, _O_CKV:_O_GA], proj[:, o_kr:o_kr + QK_ROPE_DIM]
    cqn = rms_fwd(tag + "_rms_cq", cq, sm["mla_q_lora_norm"], tm=1024)
    ckvn = rms_fwd(tag + "_rms_ckv", ckv, sm["mla_kv_lora_norm"], tm=1024)
    qb_raw = _heads(mm_nn(tag + "_uq", cqn, w["wuq"], tm=1024), b, s, N_HEADS_B, QK_DIM_B)
    kv = mm_nn(tag + "_ukv", ckvn, w["wukv"], tm=1024).reshape(b, s, N_HEADS_B, QK_NOPE_DIM + V_DIM_B).transpose(0, 2, 1, 3)
    vb = kv[..., QK_NOPE_DIM:].astype(BF16)
    kr_b = jnp.broadcast_to(kr.reshape(b, 1, s, QK_ROPE_DIM), (b, N_HEADS_B, s, QK_ROPE_DIM))
    kfull_raw = jnp.concatenate([kv[..., :QK_NOPE_DIM], kr_b], axis=-1)
    qah = qk_prep_fwd(tag + "_qa_norm", qa_raw, sm["swa_q_norm"])
    kah = qk_prep_fwd(tag + "_ka_norm", ka_raw, sm["swa_k_norm"])
    oa = swa_fwd(tag + "_swa", qah, kah, va, aux["pos_col"], aux["pos_row"], aux["slopes"], sm["swa_sinks"].reshape(-1))
    qbh = qk_prep_fwd(tag + "_qb_norm", qb_raw, sm["mla_q_norm"], aux["rope"])
    kbh = qk_prep_fwd(tag + "_kb_norm", kfull_raw, sm["mla_k_norm"], aux["rope"])
    mla_t = MLA_T if s % MLA_T == 0 else BLOCK
    obt, lse, carried = mla_fwd(tag + "_mla", qbh, kbh, _blocks_t(vb, mla_t), plan)
    oab = _unheads(oa).astype(BF16)
    obb = obt.transpose(0, 2, 4, 1, 3).reshape(b * s, WIDTH_B).astype(BF16)
    ya = mm_nn(tag + "_branch_a", oab, w["wa"], tm=1024)
    yb = mm_nn(tag + "_branch_b", obb, w["wb"], tm=1024)
    mg = gate_fwd(tag + "_gate", proj, ya, yb, _O_GA, o_gb)
    x2 = gmm_rows(tag + "_out", mg, wts["wo"], x1)
    saved = dict(x1=x1, hb=hb, proj=proj, qa_raw=qa_raw, ka_raw=ka_raw, va=va, cq=cq, ckv=ckv, cqn=cqn, ckvn=ckvn,
                 qb_raw=qb_raw, kfull_raw=kfull_raw, vb=vb, qah=qah, kah=kah, qbh=qbh, kbh=kbh, obt=obt, lse=lse,
                 oab=oab, obb=obb, ya=ya, yb=yb, mg=mg)
    return x2, saved, carried


def _mixer_bwd(tag, sv, sm, w, wts, gbuf, aux, dx2, dx2b, plan=None):
    b, s = aux["b"], aux["s"]
    t, d = dx2.shape
    o_gb = _O_GA + d
    dmg = gmm_rows_dx(tag + "_d_out", dx2b, wts["wo"])
    gbuf["wo"] = gmm_rows_dw(tag + "_dw_out", sv["mg"], dx2b)
    dya, dyb, dga, dgb = gate_bwd(tag + "_dgate", sv["proj"], sv["ya"], sv["yb"], dmg, _O_GA, o_gb)
    dwa = mm_tn(tag + "_dw_branch_a", sv["oab"], dya)
    dwb = mm_tn(tag + "_dw_branch_b", sv["obb"], dyb)
    doa = _heads(mm_nt(tag + "_d_branch_a", dya, w["wa"], tm=1024), b, s, N_HEADS_A, HEAD_DIM_A)
    dob = _heads(mm_nt(tag + "_d_branch_b", dyb, w["wb"], tm=1024), b, s, N_HEADS_B, V_DIM_B)
    mla_t = sv["obt"].shape[-1]
    sinks = sm["swa_sinks"].reshape(-1)
    dqah, dkah, dva, dsinks = swa_bwd(tag + "_dswa", sv["qah"], sv["kah"], sv["va"], aux["pos_col"], aux["pos_row"],
                                      aux["slopes"], sinks, doa)
    dqa_raw, dg_swa_q = qk_prep_bwd(tag + "_dqa_norm", sv["qa_raw"], sm["swa_q_norm"], dqah)
    dka_raw, dg_swa_k = qk_prep_bwd(tag + "_dka_norm", sv["ka_raw"], sm["swa_k_norm"], dkah)
    dqbt, dkbh, dvb, carried = mla_bwd(tag + "_dmla", sv["qbh"], sv["kbh"], _blocks_t(sv["kbh"], mla_t), sv["vb"], dob.astype(BF16),
                                       _blocks_t(dob, mla_t), sv["obt"], sv["lse"], plan)
    dqbh = _unblocks_t(dqbt)
    dqb_raw, dg_mla_q = qk_prep_bwd(tag + "_dqb_norm", sv["qb_raw"], sm["mla_q_norm"], dqbh, aux["rope"])
    dkfull, dkr_sum, dg_mla_k = qk_prep_bwd(tag + "_dkb_norm", sv["kfull_raw"], sm["mla_k_norm"], dkbh, aux["rope"], head_sum=True)
    dq_tok = _unheads(dqb_raw)
    dkv_tok = _unheads(jnp.concatenate([dkfull[..., :QK_NOPE_DIM], dvb.astype(BF16)], axis=-1))
    dwuq = mm_tn(tag + "_dw_uq", sv["cqn"], dq_tok, tk=1024)
    dwukv = mm_tn(tag + "_dw_ukv", sv["ckvn"], dkv_tok, tk=1024)
    dcqn = mm_nt(tag + "_d_uq", dq_tok, w["wuq"], tm=1024)
    dckvn = mm_nt(tag + "_d_ukv", dkv_tok, w["wukv"], tm=1024)
    dcq, dg_q_lora = rms_bwd(tag + "_drms_cq", sv["cq"], sm["mla_q_lora_norm"], dcqn, want_f32=False, tm=1024)
    dckv, dg_kv_lora = rms_bwd(tag + "_drms_ckv", sv["ckv"], sm["mla_kv_lora_norm"], dckvn, want_f32=False, tm=1024)
    dkr = dkr_sum[..., QK_NOPE_DIM:].reshape(t, QK_ROPE_DIM).astype(BF16)
    parts = [_unheads(dqa_raw), _unheads(dka_raw), _unheads(dva.astype(BF16)), dcq, dckv, dga, dgb, dkr]
    pad = w["win"].shape[1] - sum(p.shape[1] for p in parts)
    if pad:
        parts.append(jnp.zeros((t, pad), BF16))
    dproj = jnp.concatenate(parts, axis=1)
    dwin = mm_tn(tag + "_dw_in", sv["hb"], dproj, tm=1024, tn=1664, tk=512)
    dh = mm_nt(tag + "_d_in", dproj, w["win"], tm=512, tn=1024, tk=1664)
    dx1, dx1b, dg_mix = rms_bwd(tag + "_drms", sv["x1"], sm["mix_norm"], dh, dres=dx2)
    wgrads = dict(win=dwin, wuq=dwuq, wukv=dwukv, wa=dwa, wb=dwb)
    sgrads = dict(mix_norm=dg_mix, swa_q_norm=dg_swa_q, swa_k_norm=dg_swa_k, swa_sinks=dsinks[:, 0].reshape(1, -1),
                  mla_q_lora_norm=dg_q_lora, mla_kv_lora_norm=dg_kv_lora, mla_q_norm=dg_mla_q, mla_k_norm=dg_mla_k)
    return dx1, dx1b, wgrads, sgrads, carried


def _layer_weights(win4, uq4, ukv4, branch4):
    win_ref = _join_cols(win4)
    d, width = win_ref.shape
    parts = [win_ref[:, :_O_GA], win_ref[:, _O_GA + QK_ROPE_DIM:], win_ref[:, _O_GA:_O_GA + QK_ROPE_DIM]]
    padded = -(-width // (2 * LANES)) * (2 * LANES)
    if padded > width:
        parts.append(jnp.zeros((d, padded - width), BF16))
    r = branch4.shape[1] // 2
    return dict(win=jnp.concatenate(parts, axis=1), wuq=_join_cols(uq4), wukv=_join_cols(ukv4),
                wa=_join_cols(branch4[:, :r]), wb=_join_cols(branch4[:, r:]))


def _mixer_grad_shards(g, width):
    win = g["win"]
    win_ref = jnp.concatenate([win[:, :_O_GA], win[:, width - QK_ROPE_DIM:width], win[:, _O_GA:width - QK_ROPE_DIM]], axis=1)
    return (_split_cols(win_ref), _split_cols(g["wuq"]), _split_cols(g["wukv"]),
            jnp.concatenate([_split_cols(g["wa"]), _split_cols(g["wb"])], axis=1))


BUFFER_TAGS = ("col", "down", "wo", "win", "uq", "ukv", "branch")


def _layer_buffers(gathered):
    col, down, wo, win4, uq4, ukv4, branch4 = gathered
    return dict(col=col, down=down, wo=wo), _layer_weights(win4, uq4, ukv4, branch4)


def _train_step(x, positions, target, packs, small, shard):
    depth = 2
    b, s, d = x.shape
    t = b * s
    posf = positions.astype(F32)
    half = QK_ROPE_DIM // 2
    inv_freq = ROPE_BASE ** (-jnp.arange(half, dtype=F32) / half)
    ang = posf[..., None] * inv_freq
    cos, sin = jnp.cos(ang), jnp.sin(ang)
    rope = (jnp.concatenate([jnp.ones((b, s, QK_NOPE_DIM), F32), cos, cos], axis=-1),
            jnp.concatenate([jnp.zeros((b, s, QK_NOPE_DIM), F32), sin, sin], axis=-1))
    slopes = jnp.exp2(-8.0 * (jnp.arange(N_HEADS_A, dtype=F32) + 1.0) / N_HEADS_A)
    aux = dict(b=b, s=s, pos_col=posf.reshape(b, s, 1), pos_row=posf.reshape(b, 1, s), rope=rope, slopes=slopes)

    def sm_of(l):
        return {n: small[n][l:l + 1] for n in SMALL_NAMES}

    width_in = N_SHARDS * packs[BUFFER_TAGS.index("win")].shape[2]
    wts, weights, saved, sgrads, parts, slots = [None] * depth, [None] * depth, [None] * depth, [None] * depth, [None] * depth, [None] * depth

    wts[0], weights[0] = _layer_buffers(comm_call("gather_l0", gather_plan(packs, 0)))
    h = x.reshape(t, d)
    for l in range(depth):
        sm = sm_of(l)
        h, s1 = _ffn_fwd(f"l{l}_ffn1", h, sm["ffn1_norm"], wts[l], 0)
        h, s2, carried = _mixer_fwd(f"l{l}_mix", h, sm, weights[l], wts[l], aux, gather_plan(packs, 1) if l == 0 else None)
        if l == 0:
            wts[1], weights[1] = _layer_buffers(carried)
        h, s3 = _ffn_fwd(f"l{l}_ffn2", h, sm["ffn2_norm"], wts[l], 1)
        saved[l] = (s1, s2, s3)
    loss, dh, dhb = loss_fwd_bwd("loss", h, target.reshape(t, d))

    for l in reversed(range(depth)):
        sm = sm_of(l)
        s1, s2, s3 = saved[l]
        gbuf = dict(col=None, down=None, wo=None)
        dh, dhb, dg_f2 = _ffn_bwd(f"l{l}_ffn2", s3, sm["ffn2_norm"], wts[l], gbuf, 1, dh, dhb)
        plan = scatter_plan(parts[1], 1) if l == 0 else None
        dh, dhb, wg, sg, carried = _mixer_bwd(f"l{l}_mix", s2, sm, weights[l], wts[l], gbuf, aux, dh, dhb, plan)
        if l == 0:
            slots[1] = carried
        dh, dhb, dg_f1 = _ffn_bwd(f"l{l}_ffn1", s1, sm["ffn1_norm"], wts[l], gbuf, 0, dh, dhb)
        sg.update(ffn1_norm=dg_f1, ffn2_norm=dg_f2)
        sgrads[l] = sg
        mine = [gbuf["col"], gbuf["down"], gbuf["wo"]] + list(_mixer_grad_shards(wg, width_in))
        theirs = comm_call(f"swap_l{l}", swap_plan(mine, l))
        parts[l] = [add_pair(f"l{l}_add_{tag}", g.reshape(-1, g.shape[-1]), o.reshape(-1, o.shape[-1])).reshape(g.shape)
                    for tag, g, o in zip(BUFFER_TAGS, mine, theirs)]
    slots[0] = comm_call("scatter_l0", scatter_plan(parts[0], 0))
    finals = [[sum_slots(f"l{l}_sum_{tag}", sl, p, shard) for tag, sl, p in zip(BUFFER_TAGS, slots[l], parts[l])] for l in range(depth)]
    shared = comm_call("share", share_plan(finals))
    nk = len(BUFFER_TAGS)
    return loss, dh.reshape(b, s, d), [shared[l * nk:(l + 1) * nk] for l in range(depth)], sgrads


def kernel(x, positions, ffn1_norm, ffn1_w_gate, ffn1_w_up, ffn1_w_down, mix_norm, w_in, swa_q_norm, swa_k_norm, swa_sinks, mla_q_lora_norm, mla_w_uq, mla_kv_lora_norm, mla_w_ukv, mla_q_norm, mla_k_norm, w_branch_a, w_branch_b, w_out, ffn2_norm, ffn2_w_gate, ffn2_w_up, ffn2_w_down, loss_target, m_ffn1_norm, m_ffn1_w_gate, m_ffn1_w_up, m_ffn1_w_down, m_mix_norm, m_w_in, m_swa_q_norm, m_swa_k_norm, m_swa_sinks, m_mla_q_lora_norm, m_mla_w_uq, m_mla_kv_lora_norm, m_mla_w_ukv, m_mla_q_norm, m_mla_k_norm, m_w_branch_a, m_w_branch_b, m_w_out, m_ffn2_norm, m_ffn2_w_gate, m_ffn2_w_up, m_ffn2_w_down, v_ffn1_norm, v_ffn1_w_gate, v_ffn1_w_up, v_ffn1_w_down, v_mix_norm, v_w_in, v_swa_q_norm, v_swa_k_norm, v_swa_sinks, v_mla_q_lora_norm, v_mla_w_uq, v_mla_kv_lora_norm, v_mla_w_ukv, v_mla_q_norm, v_mla_k_norm, v_w_branch_a, v_w_branch_b, v_w_out, v_ffn2_norm, v_ffn2_w_gate, v_ffn2_w_up, v_ffn2_w_down):
    args = (x, positions, ffn1_norm, ffn1_w_gate, ffn1_w_up, ffn1_w_down, mix_norm, w_in, swa_q_norm, swa_k_norm, swa_sinks, mla_q_lora_norm, mla_w_uq, mla_kv_lora_norm, mla_w_ukv, mla_q_norm, mla_k_norm, w_branch_a, w_branch_b, w_out, ffn2_norm, ffn2_w_gate, ffn2_w_up, ffn2_w_down, loss_target, m_ffn1_norm, m_ffn1_w_gate, m_ffn1_w_up, m_ffn1_w_down, m_mix_norm, m_w_in, m_swa_q_norm, m_swa_k_norm, m_swa_sinks, m_mla_q_lora_norm, m_mla_w_uq, m_mla_kv_lora_norm, m_mla_w_ukv, m_mla_q_norm, m_mla_k_norm, m_w_branch_a, m_w_branch_b, m_w_out, m_ffn2_norm, m_ffn2_w_gate, m_ffn2_w_up, m_ffn2_w_down, v_ffn1_norm, v_ffn1_w_gate, v_ffn1_w_up, v_ffn1_w_down, v_mix_norm, v_w_in, v_swa_q_norm, v_swa_k_norm, v_swa_sinks, v_mla_q_lora_norm, v_mla_w_uq, v_mla_kv_lora_norm, v_mla_w_ukv, v_mla_q_norm, v_mla_k_norm, v_w_branch_a, v_w_branch_b, v_w_out, v_ffn2_norm, v_ffn2_w_gate, v_ffn2_w_up, v_ffn2_w_down)
    a = dict(zip(ARG_NAMES, args, strict=True))
    x = a["x"]
    depth = a["ffn1_norm"].shape[0]
    d = x.shape[-1]
    assert depth == 2, "one layer slab per core of a chip"

    packs = [jnp.concatenate([a[n] for n in COL_NAMES], axis=1).astype(BF16),
             jnp.concatenate([a[n] for n in DOWN_NAMES], axis=1).astype(BF16),
             a["w_out"].astype(BF16), a["w_in"].astype(BF16), a["mla_w_uq"].astype(BF16), a["mla_w_ukv"].astype(BF16),
             jnp.concatenate([a["w_branch_a"], a["w_branch_b"]], axis=1).astype(BF16)]
    shard_me = (2 * lax.axis_index("x") + lax.axis_index("y")).astype(jnp.int32).reshape(1)
    small = {n: a[n] for n in SMALL_NAMES}

    loss, grad_x, summed, sgrads = _train_step(x, a["positions"], a["loss_target"], packs, small, shard_me)

    fcol, fdown, fwo, fwin, fuq, fukv, fbranch = [jnp.stack([summed[l][k] for l in range(depth)]) for k in range(len(BUFFER_TAGS))]
    n_ff, r_branch = a["ffn1_w_down"].shape[1], a["w_branch_a"].shape[1]
    grads = {n: fcol[:, i * d:(i + 1) * d] for i, n in enumerate(COL_NAMES)}
    grads.update({n: fdown[:, i * n_ff:(i + 1) * n_ff] for i, n in enumerate(DOWN_NAMES)})
    grads.update(w_out=fwo, w_in=fwin, mla_w_uq=fuq, mla_w_ukv=fukv, w_branch_a=fbranch[:, :r_branch], w_branch_b=fbranch[:, r_branch:])

    flat = jnp.concatenate([jnp.concatenate([sgrads[l][n].reshape(-1) for l in range(depth)]) for n in SMALL_NAMES] + [loss.reshape(-1)])
    n_small = flat.shape[0]
    rows = -(-n_small // (8 * LANES)) * 8
    pad = rows * LANES - n_small

    def small_pack(v):
        return jnp.pad(v, (0, pad)).reshape(rows, LANES)

    total = small_allreduce(small_pack(flat))
    w_s, m_s, v_s = (small_pack(jnp.concatenate([a[p + n].reshape(-1) for n in SMALL_NAMES] + [jnp.zeros((1,), F32)]))
                     for p in ("", "m_", "v_"))
    d_s, nm_s, nv_s = adamw("adamw_small", w_s, total, m_s, v_s)

    def small_unpack(buf):
        out, off, flat_b = {}, 0, buf.reshape(-1)
        for n in SMALL_NAMES:
            size = a[n].shape[0] * a[n].shape[1]
            out[n] = flat_b[off:off + size].reshape(a[n].shape)
            off += size
        return out

    grads.update(small_unpack(total))
    delta, new_m, new_v = small_unpack(d_s), small_unpack(nm_s), small_unpack(nv_s)
    for n in PACK_NAMES:
        shp = a[n].shape
        two_d = (shp[0] * shp[1], shp[2])
        dn, mn, vn = adamw("adamw_" + n, a[n].reshape(two_d), grads[n].reshape(two_d), a["m_" + n].reshape(two_d), a["v_" + n].reshape(two_d))
        delta[n], new_m[n], new_v[n] = dn.reshape(shp), mn.reshape(shp), vn.reshape(shp)

    loss_out = total.reshape(-1)[n_small - 1]
    return (loss_out, grad_x, *[grads[n] for n in WEIGHT_NAMES], *[delta[n] for n in WEIGHT_NAMES],
            *[new_m[n] for n in WEIGHT_NAMES], *[new_v[n] for n in WEIGHT_NAMES])
```

```python
import functools

import numpy as np
import jax
import jax.numpy as jnp
from jax import lax
from jax.experimental import pallas as pl
from jax.experimental.pallas import tpu as pltpu

F32 = jnp.float32
BF16 = jnp.bfloat16
MESH = pl.DeviceIdType.MESH

HEAD_DIM_A = 64
N_HEADS_A = 8
N_KV_HEADS_A = 2
GROUP_A = N_HEADS_A // N_KV_HEADS_A
BLOCK = 128
N_HEADS_B = 8
Q_LORA_RANK = 256
KV_LORA_RANK = 128
QK_NOPE_DIM = 64
QK_ROPE_DIM = 32
QK_DIM_B = QK_NOPE_DIM + QK_ROPE_DIM
V_DIM_B = 64
ROPE_BASE = 10000.0
WIDTH_A = N_HEADS_A * HEAD_DIM_A
WIDTH_B = N_HEADS_B * V_DIM_B
KV_WIDTH_A = N_KV_HEADS_A * HEAD_DIM_A
EPS = 1e-6
NEG = -1e30
ADAM_LR = 0.001
ADAM_B1 = 0.9
ADAM_B2 = 0.999
ADAM_EPS = 1e-08
ADAM_WD = 0.01
ADAM_STEP = 10

N_SHARDS = 4
N_DEV = 8
LANES = 128
VMEM_LIMIT = 48 * 1024 * 1024

PACK_NAMES = ("ffn1_w_gate", "ffn1_w_up", "ffn1_w_down", "w_in", "mla_w_uq", "mla_w_ukv",
              "w_branch_a", "w_branch_b", "w_out", "ffn2_w_gate", "ffn2_w_up", "ffn2_w_down")
COL_NAMES = ("ffn1_w_gate", "ffn1_w_up", "ffn2_w_gate", "ffn2_w_up")
DOWN_NAMES = ("ffn1_w_down", "ffn2_w_down")
SMALL_NAMES = ("ffn1_norm", "mix_norm", "swa_q_norm", "swa_k_norm", "swa_sinks", "mla_q_lora_norm",
               "mla_kv_lora_norm", "mla_q_norm", "mla_k_norm", "ffn2_norm")
WEIGHT_NAMES = ("ffn1_norm", "ffn1_w_gate", "ffn1_w_up", "ffn1_w_down", "mix_norm", "w_in", "swa_q_norm",
                "swa_k_norm", "swa_sinks", "mla_q_lora_norm", "mla_w_uq", "mla_kv_lora_norm", "mla_w_ukv",
                "mla_q_norm", "mla_k_norm", "w_branch_a", "w_branch_b", "w_out", "ffn2_norm", "ffn2_w_gate",
                "ffn2_w_up", "ffn2_w_down")


def _params(sem):
    return pltpu.CompilerParams(dimension_semantics=sem, vmem_limit_bytes=VMEM_LIMIT)


def _tile(n, want, align):
    if n <= want:
        return n
    t = (want // align) * align
    while t > align and n % t:
        t -= align
    assert t >= align and n % t == 0, (n, want, align)
    return t


def _mm_call(name, a, b, out_struct, grid, a_spec, b_spec, o_spec, dims, n_red, acc_shape, alpha=1.0, res=None, into=None):
    n_par = len(grid) - n_red

    def body(*refs):
        a_ref, b_ref = refs[:2]
        r_ref = refs[2] if res is not None else None
        o_ref, acc_ref = refs[-2:]
        part = lax.dot_general(a_ref[...], b_ref[...], dims, preferred_element_type=F32)

        def finish(total):
            if alpha != 1.0:
                total = total * alpha
            if r_ref is not None:
                total = r_ref[...] + total
            o_ref[...] = total.astype(o_ref.dtype)

        if n_red == 0:
            finish(part)
            return
        ids = [pl.program_id(n_par + i) for i in range(n_red)]
        first = functools.reduce(jnp.logical_and, [i == 0 for i in ids])
        last = functools.reduce(jnp.logical_and, [i == grid[n_par + k] - 1 for k, i in enumerate(ids)])

        @pl.when(first)
        def _():
            acc_ref[...] = part

        @pl.when(jnp.logical_not(first))
        def _():
            acc_ref[...] += part

        @pl.when(last)
        def _():
            finish(acc_ref[...])

    in_specs = [a_spec, b_spec] + ([o_spec] if res is not None else [])
    args = (a, b) + ((res,) if res is not None else ())
    aliases = {}
    if into is not None:
        aliases = {len(args): 0}
        in_specs.append(pl.BlockSpec(memory_space=pl.ANY))
        args = args + (into,)
    return pl.pallas_call(
        body, name=name, out_shape=out_struct, grid=grid, in_specs=in_specs, out_specs=o_spec,
        scratch_shapes=[pltpu.VMEM(acc_shape, F32)], input_output_aliases=aliases,
        compiler_params=_params(("parallel",) * n_par + ("arbitrary",) * n_red),
    )(*args)


_NN = (((1,), (0,)), ((), ()))
_NT = (((1,), (1,)), ((), ()))
_TN = (((0,), (0,)), ((), ()))


def mm_nn(name, a, b, tm=512, tn=1024, tk=1024, out_dtype=F32, alpha=1.0, res=None):
    (m, k), (_, n) = a.shape, b.shape
    tm, tn, tk = _tile(m, tm, 16), _tile(n, tn, LANES), _tile(k, tk, LANES)
    return _mm_call(name, a, b, jax.ShapeDtypeStruct((m, n), out_dtype), (m // tm, n // tn, k // tk),
                    pl.BlockSpec((tm, tk), lambda i, j, kk: (i, kk)), pl.BlockSpec((tk, tn), lambda i, j, kk: (kk, j)),
                    pl.BlockSpec((tm, tn), lambda i, j, kk: (i, j)), _NN, 1, (tm, tn), alpha, res)


def mm_nt(name, a, b, tm=512, tn=1024, tk=1024, out_dtype=F32, alpha=1.0):
    (m, n), (k, _) = a.shape, b.shape
    tm, tn, tk = _tile(m, tm, 16), _tile(k, tn, LANES), _tile(n, tk, LANES)
    return _mm_call(name, a, b, jax.ShapeDtypeStruct((m, k), out_dtype), (m // tm, k // tn, n // tk),
                    pl.BlockSpec((tm, tk), lambda i, j, kk: (i, kk)), pl.BlockSpec((tn, tk), lambda i, j, kk: (j, kk)),
                    pl.BlockSpec((tm, tn), lambda i, j, kk: (i, j)), _NT, 1, (tm, tn), alpha)


def mm_tn(name, a, b, tm=1024, tn=1024, tk=512, out_dtype=F32, alpha=1.0):
    (m, k), (_, n) = a.shape, b.shape
    tm, tn, tk = _tile(k, tm, LANES), _tile(n, tn, LANES), _tile(m, tk, 16)
    return _mm_call(name, a, b, jax.ShapeDtypeStruct((k, n), out_dtype), (k // tm, n // tn, m // tk),
                    pl.BlockSpec((tk, tm), lambda i, j, kk: (kk, i)), pl.BlockSpec((tk, tn), lambda i, j, kk: (kk, j)),
                    pl.BlockSpec((tm, tn), lambda i, j, kk: (i, j)), _TN, 1, (tm, tn), alpha)


def ffn_up_act(name, a, w, blk, tm=512):
    (m, k), (ns, _, n) = a.shape, w.shape
    tm = _tile(m, tm, 16)

    def body(a_ref, wg_ref, wu_ref, gu_ref, h_ref):
        av = a_ref[...]
        gate = jnp.dot(av, wg_ref[...], preferred_element_type=F32)
        up = jnp.dot(av, wu_ref[...], preferred_element_type=F32)
        gu_ref[0] = gate.astype(BF16)
        gu_ref[1] = up.astype(BF16)
        h_ref[...] = (gate * jax.nn.sigmoid(gate) * up).astype(BF16)

    return pl.pallas_call(
        body, name=name, grid=(ns, m // tm),
        out_shape=(jax.ShapeDtypeStruct((2, ns, m, n), BF16), jax.ShapeDtypeStruct((ns, m, n), BF16)),
        in_specs=[pl.BlockSpec((tm, k), lambda s, i: (i, 0)), pl.BlockSpec((None, k, n), lambda s, i: (s, blk, 0)),
                  pl.BlockSpec((None, k, n), lambda s, i: (s, blk + 1, 0))],
        out_specs=(pl.BlockSpec((2, None, tm, n), lambda s, i: (0, s, i, 0)), pl.BlockSpec((None, tm, n), lambda s, i: (s, i, 0))),
        compiler_params=_params(("parallel", "parallel")),
    )(a, w, w)


def ffn_down_dact(name, a, w, gu, n, blk, alpha, tm=512):
    (m, d), ns = a.shape, w.shape[0]
    tm = _tile(m, tm, 16)

    def body(a_ref, w_ref, gu_ref, o_ref):
        dh = lax.dot_general(a_ref[...], w_ref[...], _NT, preferred_element_type=F32) * alpha
        gate, up = gu_ref[0].astype(F32), gu_ref[1].astype(F32)
        s = jax.nn.sigmoid(gate)
        o_ref[0] = (dh * up * (s * (1.0 + gate * (1.0 - s)))).astype(BF16)
        o_ref[1] = (dh * (gate * s)).astype(BF16)

    gu_spec = pl.BlockSpec((2, None, tm, n), lambda s, i: (0, s, i, 0))
    return pl.pallas_call(
        body, name=name, grid=(ns, m // tm), out_shape=jax.ShapeDtypeStruct((2, ns, m, n), BF16),
        in_specs=[pl.BlockSpec((tm, d), lambda s, i: (i, 0)), pl.BlockSpec((None, n, d), lambda s, i: (s, blk, 0)), gu_spec],
        out_specs=gu_spec, compiler_params=_params(("parallel", "parallel")),
    )(a, w, gu)


def gmm_up_dw(name, a, dgu, into, shape, blk, tk=512):
    (m, k), (_, ns, _, n) = a.shape, dgu.shape
    tk = _tile(m, tk, 16)
    return _mm_call(name, a, dgu, jax.ShapeDtypeStruct(shape, BF16), (2, ns, m // tk),
                    pl.BlockSpec((tk, k), lambda j, s, kk: (kk, 0)), pl.BlockSpec((None, None, tk, n), lambda j, s, kk: (j, s, kk, 0)),
                    pl.BlockSpec((None, k, n), lambda j, s, kk: (s, blk + j, 0)), _TN, 1, (k, n), into=into)


def gmm_up_dx(name, dgu, w, k, blk, tm=512):
    _, ns, m, n = dgu.shape
    tm = _tile(m, tm, 16)
    return _mm_call(name, dgu, w, jax.ShapeDtypeStruct((m, k), F32), (m // tm, 2, ns),
                    pl.BlockSpec((None, None, tm, n), lambda i, j, s: (j, s, i, 0)),
                    pl.BlockSpec((None, k, n), lambda i, j, s: (s, blk + j, 0)),
                    pl.BlockSpec((tm, k), lambda i, j, s: (i, 0)), _NT, 2, (tm, k))


def gmm_down(name, h, w, blk, res, alpha, tm=512):
    (ns, m, n), d = h.shape, w.shape[2]
    tm = _tile(m, tm, 16)
    return _mm_call(name, h, w, jax.ShapeDtypeStruct((m, d), F32), (m // tm, ns),
                    pl.BlockSpec((None, tm, n), lambda i, s: (s, i, 0)), pl.BlockSpec((None, n, d), lambda i, s: (s, blk, 0)),
                    pl.BlockSpec((tm, d), lambda i, s: (i, 0)), _NN, 1, (tm, d), alpha, res)


def gmm_down_dw(name, h, b, into, shape, blk, alpha, tk=512):
    (ns, m, n), d = h.shape, b.shape[1]
    tk = _tile(m, tk, 16)
    return _mm_call(name, h, b, jax.ShapeDtypeStruct(shape, BF16), (ns, m // tk),
                    pl.BlockSpec((None, tk, n), lambda s, kk: (s, kk, 0)), pl.BlockSpec((tk, d), lambda s, kk: (kk, 0)),
                    pl.BlockSpec((None, n, d), lambda s, kk: (s, blk, 0)), _TN, 1, (n, d), alpha, into=into)


def gmm_rows(name, a, w, res, tm=512):
    (m, _), (ns, r, d) = a.shape, w.shape
    tm = _tile(m, tm, 16)
    return _mm_call(name, a, w, jax.ShapeDtypeStruct((m, d), F32), (m // tm, ns),
                    pl.BlockSpec((tm, r), lambda i, s: (i, s)), pl.BlockSpec((None, r, d), lambda i, s: (s, 0, 0)),
                    pl.BlockSpec((tm, d), lambda i, s: (i, 0)), _NN, 1, (tm, d), 1.0, res)


def gmm_rows_dx(name, a, w, tm=512):
    (m, d), (ns, r, _) = a.shape, w.shape
    tm = _tile(m, tm, 16)
    return _mm_call(name, a, w, jax.ShapeDtypeStruct((m, ns * r), F32), (ns, m // tm),
                    pl.BlockSpec((tm, d), lambda s, i: (i, 0)), pl.BlockSpec((None, r, d), lambda s, i: (s, 0, 0)),
                    pl.BlockSpec((tm, r), lambda s, i: (i, s)), _NT, 0, (8, LANES))


def gmm_rows_dw(name, a, b, tk=512):
    (m, da), d = a.shape, b.shape[1]
    ns, r = N_SHARDS, da // N_SHARDS
    tk = _tile(m, tk, 16)
    return _mm_call(name, a, b, jax.ShapeDtypeStruct((ns, r, d), BF16), (ns, m // tk),
                    pl.BlockSpec((tk, r), lambda s, kk: (kk, s)), pl.BlockSpec((tk, d), lambda s, kk: (kk, 0)),
                    pl.BlockSpec((None, r, d), lambda s, kk: (s, 0, 0)), _TN, 1, (r, d))


def rms_fwd(name, x, gain, tm=512):
    m, d = x.shape
    tm = _tile(m, tm, 16)

    def body(x_ref, g_ref, o_ref):
        xv = x_ref[...]
        r = lax.rsqrt(jnp.mean(xv * xv, axis=-1, keepdims=True) + EPS)
        o_ref[...] = (xv * r * g_ref[...]).astype(o_ref.dtype)

    return pl.pallas_call(
        body, name=name, out_shape=jax.ShapeDtypeStruct((m, d), BF16), grid=(m // tm,),
        in_specs=[pl.BlockSpec((tm, d), lambda i: (i, 0)), pl.BlockSpec((1, d), lambda i: (0, 0))],
        out_specs=pl.BlockSpec((tm, d), lambda i: (i, 0)), compiler_params=_params(("parallel",)),
    )(x, gain)


def rms_bwd(name, x, gain, dn, dres=None, want_f32=True, want_bf16=True, tm=512):
    m, d = x.shape
    tm = _tile(m, tm, 16)
    n_out = int(want_f32) + int(want_bf16)

    def body(*refs):
        x_ref, g_ref, dn_ref = refs[:3]
        pos = 3
        r_ref = None
        if dres is not None:
            r_ref = refs[pos]
            pos += 1
        outs = refs[pos:pos + n_out]
        dg_ref = refs[pos + n_out]
        xv = x_ref[...]
        r = lax.rsqrt(jnp.mean(xv * xv, axis=-1, keepdims=True) + EPS)
        xhat = xv * r
        dnv = dn_ref[...]
        dxhat = dnv * g_ref[...]
        dx = r * (dxhat - xhat * jnp.mean(dxhat * xhat, axis=-1, keepdims=True))
        if r_ref is not None:
            dx = r_ref[...] + dx
        for o in outs:
            o[...] = dx.astype(o.dtype)
        part = jnp.sum(dnv * xhat, axis=0, keepdims=True)

        @pl.when(pl.program_id(0) == 0)
        def _():
            dg_ref[...] = part

        @pl.when(pl.program_id(0) > 0)
        def _():
            dg_ref[...] += part

    row = pl.BlockSpec((tm, d), lambda i: (i, 0))
    vec = pl.BlockSpec((1, d), lambda i: (0, 0))
    out_shape = ([jax.ShapeDtypeStruct((m, d), F32)] if want_f32 else []) + ([jax.ShapeDtypeStruct((m, d), BF16)] if want_bf16 else [])
    res = pl.pallas_call(
        body, name=name, out_shape=tuple(out_shape) + (jax.ShapeDtypeStruct((1, d), F32),), grid=(m // tm,),
        in_specs=[row, vec, row] + ([row] if dres is not None else []),
        out_specs=tuple([row] * n_out) + (vec,), compiler_params=_params(("arbitrary",)),
    )(*((x, gain, dn) + ((dres,) if dres is not None else ())))
    return res


def gate_fwd(name, proj, ya, yb, off_a, off_b, tm=256):
    m, d = ya.shape
    w = proj.shape[1]
    tm = _tile(m, tm, 16)

    def body(p_ref, ya_ref, yb_ref, o_ref):
        ga, gb = p_ref[:, off_a:off_a + d], p_ref[:, off_b:off_b + d]
        o_ref[...] = (jax.nn.sigmoid(ga) * ya_ref[...] + jax.nn.sigmoid(gb) * yb_ref[...]).astype(o_ref.dtype)

    row = pl.BlockSpec((tm, d), lambda i: (i, 0))
    return pl.pallas_call(
        body, name=name, out_shape=jax.ShapeDtypeStruct((m, d), BF16), grid=(m // tm,),
        in_specs=[pl.BlockSpec((tm, w), lambda i: (i, 0)), row, row], out_specs=row, compiler_params=_params(("parallel",)),
    )(proj, ya, yb)


def gate_bwd(name, proj, ya, yb, dmerged, off_a, off_b, tm=256):
    m, d = ya.shape
    w = proj.shape[1]
    tm = _tile(m, tm, 16)

    def body(p_ref, ya_ref, yb_ref, dm_ref, dya_ref, dyb_ref, dga_ref, dgb_ref):
        sa, sb = jax.nn.sigmoid(p_ref[:, off_a:off_a + d]), jax.nn.sigmoid(p_ref[:, off_b:off_b + d])
        dm = dm_ref[...]
        dya_ref[...] = (dm * sa).astype(BF16)
        dyb_ref[...] = (dm * sb).astype(BF16)
        dga_ref[...] = (dm * ya_ref[...] * (sa * (1.0 - sa))).astype(BF16)
        dgb_ref[...] = (dm * yb_ref[...] * (sb * (1.0 - sb))).astype(BF16)

    row = pl.BlockSpec((tm, d), lambda i: (i, 0))
    o = jax.ShapeDtypeStruct((m, d), BF16)
    return pl.pallas_call(
        body, name=name, out_shape=(o, o, o, o), grid=(m // tm,),
        in_specs=[pl.BlockSpec((tm, w), lambda i: (i, 0)), row, row, row], out_specs=(row, row, row, row),
        compiler_params=_params(("parallel",)),
    )(proj, ya, yb, dmerged)


def loss_fwd_bwd(name, y, target, tm=512):
    m, d = y.shape
    tm = _tile(m, tm, 16)

    def body(y_ref, t_ref, l_ref, dy_ref, dyb_ref):
        err = y_ref[...] - t_ref[...]
        dy = err * (1.0 / d)
        dy_ref[...] = dy
        dyb_ref[...] = dy.astype(BF16)
        part = 0.5 * jnp.sum(jnp.mean(err * err, axis=-1, keepdims=True), axis=0, keepdims=True)

        @pl.when(pl.program_id(0) == 0)
        def _():
            l_ref[...] = part

        @pl.when(pl.program_id(0) > 0)
        def _():
            l_ref[...] += part

    row = pl.BlockSpec((tm, d), lambda i: (i, 0))
    return pl.pallas_call(
        body, name=name, grid=(m // tm,),
        out_shape=(jax.ShapeDtypeStruct((1, 1), F32), jax.ShapeDtypeStruct((m, d), F32), jax.ShapeDtypeStruct((m, d), BF16)),
        in_specs=[row, row], out_specs=(pl.BlockSpec((1, 1), lambda i: (0, 0)), row, row),
        compiler_params=_params(("arbitrary",)),
    )(y, target)


def adamw(name, w, g, m, v):
    r, c = w.shape
    tr = _tile(r, max(8, (2 * 1024 * 1024) // (4 * c) // 8 * 8), 8)
    c1 = np.float32(1.0 - ADAM_B1 ** ADAM_STEP)
    c2 = np.float32(1.0 - ADAM_B2 ** ADAM_STEP)

    def body(w_ref, g_ref, m_ref, v_ref, d_ref, nm_ref, nv_ref):
        gv = g_ref[...]
        nm = ADAM_B1 * m_ref[...] + (1.0 - ADAM_B1) * gv
        nv = ADAM_B2 * v_ref[...] + (1.0 - ADAM_B2) * (gv * gv)
        d_ref[...] = -ADAM_LR * ((nm / c1) / (jnp.sqrt(nv / c2) + ADAM_EPS) + ADAM_WD * w_ref[...])
        nm_ref[...] = nm
        nv_ref[...] = nv

    row = pl.BlockSpec((tr, c), lambda i: (i, 0))
    o = jax.ShapeDtypeStruct((r, c), F32)
    return pl.pallas_call(
        body, name=name, out_shape=(o, o, o), grid=(r // tr,), in_specs=[row] * 4, out_specs=(row, row, row),
        compiler_params=_params(("parallel",)),
    )(w, g, m, v)


def _rope_rot(d):
    r = np.zeros((d, d), np.float32)
    half = QK_ROPE_DIM // 2
    base = d - QK_ROPE_DIM
    for j in range(half):
        r[base + half + j, base + j] = -1.0
        r[base + j, base + half + j] = 1.0
    return r


def qk_prep_fwd(name, x, gain, rope=None):
    b, h, s, d = x.shape

    def body(*refs):
        if rope is None:
            x_ref, g_ref, o_ref = refs
        else:
            x_ref, g_ref, c_ref, s_ref, r_ref, o_ref = refs
        xv = x_ref[...]
        y = xv * lax.rsqrt(jnp.mean(xv * xv, axis=-1, keepdims=True) + EPS) * g_ref[...]
        if rope is not None:
            rot = jnp.dot(y, r_ref[...], precision=lax.Precision.HIGHEST, preferred_element_type=F32)
            y = y * c_ref[...] + rot * s_ref[...]
        o_ref[...] = y.astype(o_ref.dtype)

    xs = pl.BlockSpec((None, None, s, d), lambda bi, hi: (bi, hi, 0, 0))
    in_specs = [xs, pl.BlockSpec((1, d), lambda bi, hi: (0, 0))]
    args = [x, gain]
    if rope is not None:
        tab = pl.BlockSpec((None, s, d), lambda bi, hi: (bi, 0, 0))
        in_specs += [tab, tab, pl.BlockSpec((d, d), lambda bi, hi: (0, 0))]
        args += [rope[0], rope[1], jnp.asarray(_rope_rot(d))]
    return pl.pallas_call(
        body, name=name, out_shape=jax.ShapeDtypeStruct(x.shape, BF16), grid=(b, h), in_specs=in_specs, out_specs=xs,
        compiler_params=_params(("parallel", "parallel")),
    )(*args)


def qk_prep_bwd(name, x, gain, dxh, rope=None, head_sum=False):
    b, h, s, d = x.shape

    def body(*refs):
        if rope is None:
            x_ref, g_ref, dy_ref = refs[:3]
            outs = refs[3:]
        else:
            x_ref, g_ref, dy_ref, c_ref, s_ref, rt_ref = refs[:6]
            outs = refs[6:]
        dx_ref, dg_ref = outs[0], outs[-1]
        dy = dy_ref[...]
        if rope is not None:
            dy = dy * c_ref[...] + jnp.dot(dy * s_ref[...], rt_ref[...], precision=lax.Precision.HIGHEST, preferred_element_type=F32)
        xv = x_ref[...]
        r = lax.rsqrt(jnp.mean(xv * xv, axis=-1, keepdims=True) + EPS)
        xhat = xv * r
        dxhat = dy * g_ref[...]
        dx = r * (dxhat - xhat * jnp.mean(dxhat * xhat, axis=-1, keepdims=True))
        dx_ref[...] = dx.astype(dx_ref.dtype)
        part = jnp.sum(dy * xhat, axis=0, keepdims=True)
        first = jnp.logical_and(pl.program_id(0) == 0, pl.program_id(1) == 0)

        @pl.when(first)
        def _():
            dg_ref[...] = part

        @pl.when(jnp.logical_not(first))
        def _():
            dg_ref[...] += part

        if head_sum:
            hs_ref = outs[1]

            @pl.when(pl.program_id(1) == 0)
            def _():
                hs_ref[...] = dx

            @pl.when(pl.program_id(1) > 0)
            def _():
                hs_ref[...] += dx

    xs = pl.BlockSpec((None, None, s, d), lambda bi, hi: (bi, hi, 0, 0))
    vec = pl.BlockSpec((1, d), lambda bi, hi: (0, 0))
    tab = pl.BlockSpec((None, s, d), lambda bi, hi: (bi, 0, 0))
    in_specs = [xs, vec, xs]
    args = [x, gain, dxh]
    if rope is not None:
        in_specs += [tab, tab, pl.BlockSpec((d, d), lambda bi, hi: (0, 0))]
        args += [rope[0], rope[1], jnp.asarray(_rope_rot(d).T.copy())]
    out_shape = [jax.ShapeDtypeStruct(x.shape, BF16)]
    out_specs = [xs]
    if head_sum:
        out_shape.append(jax.ShapeDtypeStruct((b, s, d), F32))
        out_specs.append(tab)
    out_shape.append(jax.ShapeDtypeStruct((1, d), F32))
    out_specs.append(vec)
    return pl.pallas_call(
        body, name=name, out_shape=tuple(out_shape), grid=(b, h), in_specs=in_specs, out_specs=tuple(out_specs),
        compiler_params=_params(("arbitrary", "arbitrary")),
    )(*args)


def _swa_specs(s):
    nb = s // BLOCK
    q = pl.BlockSpec((None, GROUP_A, BLOCK, HEAD_DIM_A), lambda b, kv, n: (b, kv, n, 0))
    kprev = pl.BlockSpec((None, None, BLOCK, HEAD_DIM_A), lambda b, kv, n: (b, kv, jnp.maximum(n - 1, 0), 0))
    kcur = pl.BlockSpec((None, None, BLOCK, HEAD_DIM_A), lambda b, kv, n: (b, kv, n, 0))
    pcol = pl.BlockSpec((None, BLOCK, 1), lambda b, kv, n: (b, n, 0))
    prow_prev = pl.BlockSpec((None, 1, BLOCK), lambda b, kv, n: (b, 0, jnp.maximum(n - 1, 0)))
    prow_cur = pl.BlockSpec((None, 1, BLOCK), lambda b, kv, n: (b, 0, n))
    smem = pl.BlockSpec(memory_space=pltpu.SMEM)
    return nb, q, kprev, kcur, pcol, prow_prev, prow_cur, smem


def _swa_probs(q, kk, dist, valid, slope, sink):
    sc = lax.dot_general(q, kk, _NT, preferred_element_type=F32) * (HEAD_DIM_A ** -0.5)
    sc = sc - slope * dist
    sc = jnp.where(valid, sc, NEG)
    m = jnp.maximum(jnp.max(sc, axis=-1, keepdims=True), sink)
    e = jnp.exp(sc - m)
    es = jnp.exp(sink - m)
    inv = 1.0 / (jnp.sum(e, axis=-1, keepdims=True) + es)
    return e * inv, es * inv


def _swa_window(n, kp_ref, kc_ref, vp_ref, vc_ref, pc_ref, prp_ref, prc_ref):
    kk = jnp.concatenate([kp_ref[...], kc_ref[...]], axis=0)
    vv = jnp.concatenate([vp_ref[...], vc_ref[...]], axis=0)
    dist = pc_ref[...] - jnp.concatenate([prp_ref[...], prc_ref[...]], axis=1)
    qi = lax.broadcasted_iota(jnp.int32, (BLOCK, 2 * BLOCK), 0) + BLOCK
    ki = lax.broadcasted_iota(jnp.int32, (BLOCK, 2 * BLOCK), 1)
    diff = qi - ki
    valid = (diff >= 0) & (diff < BLOCK) & ((n > 0) | (ki >= BLOCK))
    return kk, vv, dist, valid


def swa_fwd(name, q, k, v, pos_col, pos_row, slopes, sinks):
    b, _, s, _ = q.shape
    nb, qs, kprev, kcur, pcol, prp, prc, smem = _swa_specs(s)

    def body(q_ref, kp_ref, kc_ref, vp_ref, vc_ref, pc_ref, prp_ref, prc_ref, sl_ref, sk_ref, o_ref):
        kv, n = pl.program_id(1), pl.program_id(2)
        kk, vv, dist, valid = _swa_window(n, kp_ref, kc_ref, vp_ref, vc_ref, pc_ref, prp_ref, prc_ref)
        for g in range(GROUP_A):
            hd = kv * GROUP_A + g
            p, _ = _swa_probs(q_ref[g], kk, dist, valid, sl_ref[hd], sk_ref[hd])
            o_ref[g] = jnp.dot(p.astype(BF16), vv, preferred_element_type=F32)

    return pl.pallas_call(
        body, name=name, out_shape=jax.ShapeDtypeStruct(q.shape, F32), grid=(b, N_KV_HEADS_A, nb),
        in_specs=[qs, kprev, kcur, kprev, kcur, pcol, prp, prc, smem, smem], out_specs=qs,
        compiler_params=_params(("parallel", "parallel", "parallel")),
    )(q, k, k, v, v, pos_col, pos_row, pos_row, slopes, sinks)


def swa_bwd(name, q, k, v, pos_col, pos_row, slopes, sinks, do):
    b, _, s, _ = q.shape
    nb, qs, kprev, kcur, pcol, prp, prc, smem = _swa_specs(s)

    def body(q_ref, kp_ref, kc_ref, vp_ref, vc_ref, pc_ref, prp_ref, prc_ref, sl_ref, sk_ref, do_ref, dq_ref, dk_ref, dv_ref, ds_ref):
        bi, kv, n = pl.program_id(0), pl.program_id(1), pl.program_id(2)
        kk, vv, dist, valid = _swa_window(n, kp_ref, kc_ref, vp_ref, vc_ref, pc_ref, prp_ref, prc_ref)

        @pl.when((bi == 0) & (kv == 0) & (n == 0))
        def _():
            ds_ref[...] = jnp.zeros_like(ds_ref)

        @pl.when(n == 0)
        def _():
            dk_ref[...] = jnp.zeros_like(dk_ref)
            dv_ref[...] = jnp.zeros_like(dv_ref)

        dkk = jnp.zeros((2 * BLOCK, HEAD_DIM_A), F32)
        dvv = jnp.zeros((2 * BLOCK, HEAD_DIM_A), F32)
        head_row = lax.broadcasted_iota(jnp.int32, (N_HEADS_A, LANES), 0)
        dsink = jnp.zeros((N_HEADS_A, LANES), F32)
        for g in range(GROUP_A):
            hd = kv * GROUP_A + g
            qg = q_ref[g]
            p, ps = _swa_probs(qg, kk, dist, valid, sl_ref[hd], sk_ref[hd])
            dob = do_ref[g].astype(BF16)
            dvv = dvv + lax.dot_general(p.astype(BF16), dob, _TN, preferred_element_type=F32)
            dp = lax.dot_general(dob, vv, _NT, preferred_element_type=F32)
            rs = jnp.sum(p * dp, axis=-1, keepdims=True)
            dsb = (p * (dp - rs) * (HEAD_DIM_A ** -0.5)).astype(BF16)
            dq_ref[g] = jnp.dot(dsb, kk, preferred_element_type=F32)
            dkk = dkk + lax.dot_general(dsb, qg, _TN, preferred_element_type=F32)
            dsink = dsink + jnp.where(head_row == hd, -jnp.sum(ps * rs), 0.0)
        ds_ref[...] += dsink

        @pl.when(n > 0)
        def _():
            start = pl.multiple_of((n - 1) * BLOCK, BLOCK)
            dk_ref[pl.ds(start, 2 * BLOCK), :] += dkk
            dv_ref[pl.ds(start, 2 * BLOCK), :] += dvv

        @pl.when(n == 0)
        def _():
            dk_ref[0:BLOCK, :] += dkk[BLOCK:, :]
            dv_ref[0:BLOCK, :] += dvv[BLOCK:, :]

    kv_full = pl.BlockSpec((None, None, s, HEAD_DIM_A), lambda bi, kv, n: (bi, kv, 0, 0))
    return pl.pallas_call(
        body, name=name, grid=(b, N_KV_HEADS_A, nb),
        out_shape=(jax.ShapeDtypeStruct(q.shape, F32), jax.ShapeDtypeStruct(k.shape, F32), jax.ShapeDtypeStruct(k.shape, F32),
                   jax.ShapeDtypeStruct((N_HEADS_A, LANES), F32)),
        in_specs=[qs, kprev, kcur, kprev, kcur, pcol, prp, prc, smem, smem, qs],
        out_specs=(qs, kv_full, kv_full, pl.BlockSpec((N_HEADS_A, LANES), lambda bi, kv, n: (0, 0))),
        compiler_params=_params(("arbitrary", "arbitrary", "arbitrary")),
    )(q, k, k, v, v, pos_col, pos_row, pos_row, slopes, sinks, do)


MLA_T = 256


def _blocks_t(z, t):
    b, h, s, d = z.shape
    return z.reshape(b, h, s // t, t, d).transpose(0, 1, 2, 4, 3)


def _unblocks_t(z):
    b, h, nb, d, t = z.shape
    return z.transpose(0, 1, 2, 4, 3).reshape(b, h, nb * t, d)


def _hosted(plan, n_in, n_out, b, h):
    if plan is None:
        return [], [], [], [], ("parallel", "parallel"), (lambda refs: None), (lambda refs: None)
    ni, no = len(plan.ins), len(plan.out_shapes)

    def split(refs):
        return refs[n_in:n_in + ni], refs[n_in + ni + n_out:n_in + ni + n_out + no], refs[-2], refs[-1]

    def at_start(refs):
        @pl.when((pl.program_id(0) == 0) & (pl.program_id(1) == 0))
        def _():
            plan.start(*split(refs))

    def at_end(refs):
        @pl.when((pl.program_id(0) == b - 1) & (pl.program_id(1) == h - 1))
        def _():
            plan.finish(*split(refs))

    return plan.ins, [_HBM] * ni, plan.out_shapes, [_HBM] * no, ("arbitrary", "arbitrary"), at_start, at_end


def mla_fwd(name, q, k, vt, plan=None):
    b, h, s, dq = q.shape
    _, _, nb, dv, t = vt.shape
    scale = QK_DIM_B ** -0.5
    p_ins, p_in_specs, p_outs, p_out_specs, semantics, at_start, at_end = _hosted(plan, 3, 2, b, h)

    def body(*refs):
        q_ref, k_ref, vt_ref = refs[:3]
        o_ref, lse_ref = refs[3 + len(p_ins):5 + len(p_ins)]
        at_start(refs)
        causal = lax.broadcasted_iota(jnp.int32, (t, t), 0) <= lax.broadcasted_iota(jnp.int32, (t, t), 1)
        for i in range(nb):
            q_i = q_ref[i * t:(i + 1) * t, :]

            def step(j, carry, q_i=q_i, diagonal=False):
                m, l, acc = carry
                rows = slice(j * t, (j + 1) * t) if diagonal else pl.ds(pl.multiple_of(j * t, t), t)
                st = lax.dot_general(k_ref[rows, :], q_i, _NT, preferred_element_type=F32) * scale
                if diagonal:
                    st = jnp.where(causal, st, NEG)
                m_new = jnp.maximum(m, jnp.max(st, axis=0, keepdims=True))
                a = jnp.exp(m - m_new)
                p = jnp.exp(st - m_new)
                l = a * l + jnp.sum(p, axis=0, keepdims=True)
                acc = a * acc + jnp.dot(vt_ref[j], p.astype(BF16), preferred_element_type=F32)
                return m_new, l, acc

            carry = (jnp.full((1, t), NEG, F32), jnp.zeros((1, t), F32), jnp.zeros((dv, t), F32))
            if i > 0:
                carry = lax.fori_loop(0, i, step, carry)
            m, l, acc = step(i, carry, diagonal=True)
            o_ref[i] = acc / l
            lse_ref[i] = m + jnp.log(l)
        at_end(refs)

    qs = pl.BlockSpec((None, None, s, dq), lambda bi, hi: (bi, hi, 0, 0))
    ts = pl.BlockSpec((None, None, nb, dv, t), lambda bi, hi: (bi, hi, 0, 0, 0))
    ls = pl.BlockSpec((None, None, nb, 1, t), lambda bi, hi: (bi, hi, 0, 0, 0))
    res = pl.pallas_call(
        body, name=name, grid=(b, h),
        out_shape=(jax.ShapeDtypeStruct((b, h, nb, dv, t), F32), jax.ShapeDtypeStruct((b, h, nb, 1, t), F32)) + tuple(p_outs),
        in_specs=[qs, qs, ts] + p_in_specs, out_specs=(ts, ls) + tuple(p_out_specs),
        scratch_shapes=plan.scratch() if plan is not None else [], compiler_params=_params(semantics),
    )(q, k, vt, *p_ins)
    return res[0], res[1], list(res[2:])


def mla_bwd(name, q, k, kt, v, do, dot, ot, lse, plan=None):
    b, h, s, dq = q.shape
    _, _, nb, dv, t = ot.shape
    scale = QK_DIM_B ** -0.5
    p_ins, p_in_specs, p_outs, p_out_specs, semantics, at_start, at_end = _hosted(plan, 8, 3, b, h)

    def body(*refs):
        q_ref, k_ref, kt_ref, v_ref, do_ref, dot_ref, ot_ref, lse_ref = refs[:8]
        dqt_ref, dk_ref, dv_ref = refs[8 + len(p_ins):11 + len(p_ins)]
        at_start(refs)
        causal = lax.broadcasted_iota(jnp.int32, (t, t), 0) <= lax.broadcasted_iota(jnp.int32, (t, t), 1)
        dk_ref[...] = jnp.zeros_like(dk_ref)
        dv_ref[...] = jnp.zeros_like(dv_ref)
        for i in range(nb):
            q_i = q_ref[i * t:(i + 1) * t, :]
            do_i = do_ref[i * t:(i + 1) * t, :]
            dot_i = dot_ref[i]
            delta = jnp.sum(ot_ref[i] * dot_i, axis=0, keepdims=True)
            dot_b = dot_i.astype(BF16)
            lse_i = lse_ref[i]

            def step(j, dqt, q_i=q_i, do_i=do_i, delta=delta, dot_b=dot_b, lse_i=lse_i, diagonal=False):
                rows = slice(j * t, (j + 1) * t) if diagonal else pl.ds(pl.multiple_of(j * t, t), t)
                st = lax.dot_general(k_ref[rows, :], q_i, _NT, preferred_element_type=F32) * scale
                if diagonal:
                    st = jnp.where(causal, st, NEG)
                pt = jnp.exp(st - lse_i)
                dpt = jnp.dot(v_ref[rows, :], dot_b, preferred_element_type=F32)
                dst = (pt * (dpt - delta) * scale).astype(BF16)
                dv_ref[rows, :] += jnp.dot(pt.astype(BF16), do_i, preferred_element_type=F32)
                dk_ref[rows, :] += jnp.dot(dst, q_i, preferred_element_type=F32)
                return dqt + jnp.dot(kt_ref[j], dst, preferred_element_type=F32)

            dqt = jnp.zeros((dq, t), F32)
            if i > 0:
                dqt = lax.fori_loop(0, i, step, dqt)
            dqt_ref[i] = step(i, dqt, diagonal=True)
        at_end(refs)

    def nat(d):
        return pl.BlockSpec((None, None, s, d), lambda bi, hi: (bi, hi, 0, 0))

    def blk(d):
        return pl.BlockSpec((None, None, nb, d, t), lambda bi, hi: (bi, hi, 0, 0, 0))

    res = pl.pallas_call(
        body, name=name, grid=(b, h),
        out_shape=(jax.ShapeDtypeStruct((b, h, nb, dq, t), F32), jax.ShapeDtypeStruct((b, h, s, dq), F32),
                   jax.ShapeDtypeStruct((b, h, s, dv), F32)) + tuple(p_outs),
        in_specs=[nat(dq), nat(dq), blk(dq), nat(dv), nat(dv), blk(dv), blk(dv), blk(1)] + p_in_specs,
        out_specs=(blk(dq), nat(dq), nat(dv)) + tuple(p_out_specs),
        scratch_shapes=plan.scratch() if plan is not None else [], compiler_params=_params(semantics),
    )(q, k, kt, v, do, dot, ot, lse, *p_ins)
    return res[0], res[1], res[2], list(res[3:])


_HBM = pl.BlockSpec(memory_space=pltpu.HBM)


def _place():
    x, y, c = lax.axis_index("x"), lax.axis_index("y"), lax.axis_index("c")
    chips = [(1 - x, y), (x, 1 - y), (1 - x, 1 - y)]
    return x, y, c, chips


def _remote(src, dst, send_sem, recv_sem, dev):
    return pltpu.make_async_remote_copy(src_ref=src, dst_ref=dst, send_sem=send_sem, recv_sem=recv_sem,
                                        device_id=dev, device_id_type=MESH)


class CommPlan:
    def __init__(self, ins, out_shapes, n_sems, start, finish, aliases=None):
        self.ins, self.out_shapes, self.n_sems = list(ins), list(out_shapes), n_sems
        self.start, self.finish, self.aliases = start, finish, dict(aliases or {})

    def scratch(self):
        return [pltpu.SemaphoreType.DMA((self.n_sems,)), pltpu.SemaphoreType.DMA((self.n_sems,))]


def comm_call(name, plan):
    ni, no = len(plan.ins), len(plan.out_shapes)

    def body(*refs):
        ins, outs, (send_sems, recv_sems) = refs[:ni], refs[ni:ni + no], refs[ni + no:]
        plan.start(ins, outs, send_sems, recv_sems)
        plan.finish(ins, outs, send_sems, recv_sems)

    return pl.pallas_call(
        body, name=name, out_shape=tuple(plan.out_shapes), in_specs=[_HBM] * ni, out_specs=tuple([_HBM] * no),
        input_output_aliases=plan.aliases, scratch_shapes=plan.scratch(),
    )(*plan.ins)


def gather_plan(packs, l):
    nk = len(packs)

    def parts(p_refs, o_refs, ss, rs):
        x, y, c, chips = _place()
        s_me = 2 * x + y
        sibling = (x, y, 1 - c)
        first = [_remote(p_refs[k].at[l], o_refs[k].at[s_me], ss.at[j * nk + k], rs.at[j * nk + k], (cx, cy, c))
                 for j, (cx, cy) in enumerate(chips) for k in range(nk)]
        own = [_remote(p_refs[k].at[l], o_refs[k].at[s_me], ss.at[6 * nk + k], rs.at[6 * nk + k], sibling) for k in range(nk)]
        return c, chips, sibling, first, own

    def start(p_refs, o_refs, ss, rs):
        c, _, _, first, own = parts(p_refs, o_refs, ss, rs)
        for cp in own:
            cp.start()

        @pl.when(c == l)
        def _():
            for cp in first:
                cp.start()

    def finish(p_refs, o_refs, ss, rs):
        c, chips, sibling, first, own = parts(p_refs, o_refs, ss, rs)

        @pl.when(c == l)
        def _():
            passed = []
            for j, (cx, cy) in enumerate(chips):
                for k in range(nk):
                    blk = o_refs[k].at[2 * cx + cy]
                    _remote(blk, blk, ss.at[j * nk + k], rs.at[j * nk + k], (cx, cy, c)).wait_recv()
                    fwd = _remote(blk, blk, ss.at[(3 + j) * nk + k], rs.at[(3 + j) * nk + k], sibling)
                    fwd.start()
                    passed.append(fwd)
            for cp in first + passed:
                cp.wait_send()

        @pl.when(c != l)
        def _():
            for j, (cx, cy) in enumerate(chips):
                for k in range(nk):
                    blk = o_refs[k].at[2 * cx + cy]
                    _remote(blk, blk, ss.at[(3 + j) * nk + k], rs.at[(3 + j) * nk + k], sibling).wait_recv()

        for cp in own:
            cp.wait()

    outs = [jax.ShapeDtypeStruct((N_SHARDS,) + p.shape[1:], p.dtype) for p in packs]
    return CommPlan(packs, outs, 7 * nk, start, finish)


def swap_plan(grads, l):
    nk = len(grads)

    def copies(g_refs, o_refs, ss, rs):
        x, y, c, _ = _place()
        return c, [_remote(g_refs[k], o_refs[k], ss.at[k], rs.at[k], (x, y, 1 - c)) for k in range(nk)]

    def start(g_refs, o_refs, ss, rs):
        c, cps = copies(g_refs, o_refs, ss, rs)

        @pl.when(c != l)
        def _():
            for cp in cps:
                cp.start()

    def finish(g_refs, o_refs, ss, rs):
        c, cps = copies(g_refs, o_refs, ss, rs)

        @pl.when(c != l)
        def _():
            for cp in cps:
                cp.wait_send()

        @pl.when(c == l)
        def _():
            for cp in cps:
                cp.wait_recv()

    return CommPlan(grads, [jax.ShapeDtypeStruct(g.shape, g.dtype) for g in grads], nk, start, finish)


def scatter_plan(parts, l):
    nk = len(parts)

    def copies(p_refs, o_refs, ss, rs):
        x, y, c, chips = _place()
        s_me = 2 * x + y
        sends = [_remote(p_refs[k].at[2 * cx + cy], o_refs[k].at[s_me], ss.at[j * nk + k], rs.at[j * nk + k], (cx, cy, c))
                 for j, (cx, cy) in enumerate(chips) for k in range(nk)]
        return c, chips, sends

    def start(p_refs, o_refs, ss, rs):
        c, _, sends = copies(p_refs, o_refs, ss, rs)

        @pl.when(c == l)
        def _():
            for cp in sends:
                cp.start()

    def finish(p_refs, o_refs, ss, rs):
        c, chips, sends = copies(p_refs, o_refs, ss, rs)

        @pl.when(c == l)
        def _():
            for j, (cx, cy) in enumerate(chips):
                for k in range(nk):
                    slot = o_refs[k].at[2 * cx + cy]
                    _remote(slot, slot, ss.at[j * nk + k], rs.at[j * nk + k], (cx, cy, c)).wait_recv()
            for cp in sends:
                cp.wait_send()

    return CommPlan(parts, [jax.ShapeDtypeStruct(p.shape, p.dtype) for p in parts], 3 * nk, start, finish)


def share_plan(finals):
    flat = [(l, f) for l, fs in enumerate(finals) for f in fs]

    def copies(f_refs, o_refs, ss, rs):
        x, y, c, _ = _place()
        return c, [_remote(f_refs[i], o_refs[i], ss.at[i], rs.at[i], (x, y, 1 - c)) for i in range(len(flat))]

    def start(f_refs, o_refs, ss, rs):
        c, cps = copies(f_refs, o_refs, ss, rs)
        for l in range(len(finals)):
            @pl.when(c == l)
            def _(l=l):
                for cp, (lf, _) in zip(cps, flat):
                    if lf == l:
                        cp.start()

    def finish(f_refs, o_refs, ss, rs):
        c, cps = copies(f_refs, o_refs, ss, rs)
        for l in range(len(finals)):
            @pl.when(c == l)
            def _(l=l):
                for cp, (lf, _) in zip(cps, flat):
                    if lf == l:
                        cp.wait_send()

            @pl.when(c != l)
            def _(l=l):
                for cp, (lf, _) in zip(cps, flat):
                    if lf == l:
                        cp.wait_recv()

    arrays = [f for _, f in flat]
    return CommPlan(arrays, [jax.ShapeDtypeStruct(f.shape, f.dtype) for f in arrays], len(flat), start, finish,
                    aliases={i: i for i in range(len(flat))})


def add_pair(name, own, other, tr=256):
    n, cols = own.shape
    tr = _tile(n, tr, 16)

    def body(g_ref, o_ref, out_ref):
        out_ref[...] = (g_ref[...].astype(F32) + o_ref[...].astype(F32)).astype(out_ref.dtype)

    row = pl.BlockSpec((tr, cols), lambda i: (i, 0))
    return pl.pallas_call(
        body, name=name, out_shape=jax.ShapeDtypeStruct((n, cols), BF16), grid=(n // tr,), in_specs=[row, row], out_specs=row,
        compiler_params=_params(("parallel",)),
    )(own, other)


def sum_slots(name, slots, part, shard, tr=256):
    ns, r, cols = slots.shape
    tr = _tile(r, tr, 16)

    def body(s_idx, s_ref, own_ref, o_ref):
        total = own_ref[...].astype(F32)
        for k in range(1, ns):
            total = total + s_ref[(s_idx[0] + k) % ns].astype(F32)
        o_ref[...] = total

    return pl.pallas_call(
        body, name=name, out_shape=jax.ShapeDtypeStruct((r, cols), F32),
        grid_spec=pltpu.PrefetchScalarGridSpec(
            num_scalar_prefetch=1, grid=(r // tr,),
            in_specs=[pl.BlockSpec((ns, tr, cols), lambda i, p: (0, i, 0)), pl.BlockSpec((None, tr, cols), lambda i, p: (p[0], i, 0))],
            out_specs=pl.BlockSpec((tr, cols), lambda i, p: (i, 0))),
        compiler_params=_params(("parallel",)),
    )(shard, slots, part)


def small_allreduce(vec):
    r, cols = vec.shape

    def body(v_ref, o_ref, buf, send_sems, recv_sems):
        x, y, c, _ = _place()
        me = 4 * x + 2 * y + c
        buf[me] = v_ref[...]
        peers = []
        for k in range(1, N_DEV):
            px = (1 - x) if (k & 4) else x
            py = (1 - y) if (k & 2) else y
            pc = (1 - c) if (k & 1) else c
            peers.append((px, py, pc))
        sends = [_remote(buf.at[me], buf.at[me], send_sems.at[k], recv_sems.at[k], peer) for k, peer in enumerate(peers)]
        for cp in sends:
            cp.start()
        for k, (px, py, pc) in enumerate(peers):
            slot = buf.at[4 * px + 2 * py + pc]
            _remote(slot, slot, send_sems.at[k], recv_sems.at[k], (px, py, pc)).wait_recv()
        for cp in sends:
            cp.wait_send()
        total = buf[0]
        for k in range(1, N_DEV):
            total = total + buf[k]
        o_ref[...] = total

    return pl.pallas_call(
        body, name="small_allreduce", out_shape=jax.ShapeDtypeStruct((r, cols), F32),
        in_specs=[pl.BlockSpec(memory_space=pltpu.VMEM)], out_specs=pl.BlockSpec(memory_space=pltpu.VMEM),
        scratch_shapes=[pltpu.VMEM((N_DEV, r, cols), F32), pltpu.SemaphoreType.DMA((N_DEV - 1,)), pltpu.SemaphoreType.DMA((N_DEV - 1,))],
    )(vec)


ARG_NAMES = (("x", "positions") + WEIGHT_NAMES + ("loss_target",) + tuple("m_" + n for n in WEIGHT_NAMES)
             + tuple("v_" + n for n in WEIGHT_NAMES))

_O_QA, _O_KA, _O_VA = 0, WIDTH_A, WIDTH_A + KV_WIDTH_A
_O_CQ = _O_VA + KV_WIDTH_A
_O_CKV = _O_CQ + Q_LORA_RANK
_O_GA = _O_CKV + KV_LORA_RANK


def _join_cols(shards):
    ns, r, c = shards.shape
    return shards.transpose(1, 0, 2).reshape(r, ns * c)


def _split_cols(mat):
    r, c4 = mat.shape
    return mat.reshape(r, N_SHARDS, c4 // N_SHARDS).transpose(1, 0, 2)


def _heads(z, b, s, h, d):
    return z.reshape(b, s, h, d).transpose(0, 2, 1, 3)


def _unheads(z):
    b, h, s, d = z.shape
    return z.transpose(0, 2, 1, 3).reshape(b * s, h * d)


def _ffn_fwd(tag, x, gain, wts, f):
    nb = rms_fwd(tag + "_rms", x, gain)
    gu, h = ffn_up_act(tag + "_up", nb, wts["col"], 2 * f)
    out = gmm_down(tag + "_down", h, wts["down"], f, x, 0.5)
    return out, (x, nb, gu, h)


def _ffn_bwd(tag, saved, gain, wts, gbuf, f, dout, doutb):
    x, nb, gu, h = saved
    n = gu.shape[-1]
    dgu = ffn_down_dact(tag + "_dh", doutb, wts["down"], gu, n, f, 0.5)
    gbuf["down"] = gmm_down_dw(tag + "_dwd", h, doutb, gbuf["down"], wts["down"].shape, f, 0.5)
    gbuf["col"] = gmm_up_dw(tag + "_dwgu", nb, dgu, gbuf["col"], wts["col"].shape, 2 * f)
    dn = gmm_up_dx(tag + "_dn", dgu, wts["col"], x.shape[1], 2 * f)
    dx, dxb, dgain = rms_bwd(tag + "_drms", x, gain, dn, dres=dout)
    return dx, dxb, dgain


def _mixer_fwd(tag, x1, sm, w, wts, aux, plan=None):
    b, s = aux["b"], aux["s"]
    d = x1.shape[1]
    o_gb, o_kr = _O_GA + d, _O_GA + 2 * d
    hb = rms_fwd(tag + "_rms", x1, sm["mix_norm"])
    proj = mm_nn(tag + "_proj", hb, w["win"], tm=512, tn=1664, tk=1024)
    qa_raw = _heads(proj[:, _O_QA:_O_KA], b, s, N_HEADS_A, HEAD_DIM_A)
    ka_raw = _heads(proj[:, _O_KA:_O_VA], b, s, N_KV_HEADS_A, HEAD_DIM_A)
    va = _heads(proj[:, _O_VA:_O_CQ], b, s, N_KV_HEADS_A, HEAD_DIM_A).astype(BF16)
    cq, ckv, kr = proj[:, _O_CQ:_O_CKV], proj[:, _O_CKV:_O_GA], proj[:, o_kr:o_kr + QK_ROPE_DIM]
    cqn = rms_fwd(tag + "_rms_cq", cq, sm["mla_q_lora_norm"], tm=1024)
    ckvn = rms_fwd(tag + "_rms_ckv", ckv, sm["mla_kv_lora_norm"], tm=1024)
    qb_raw = _heads(mm_nn(tag + "_uq", cqn, w["wuq"], tm=1024), b, s, N_HEADS_B, QK_DIM_B)
    kv = mm_nn(tag + "_ukv", ckvn, w["wukv"], tm=1024).reshape(b, s, N_HEADS_B, QK_NOPE_DIM + V_DIM_B).transpose(0, 2, 1, 3)
    vb = kv[..., QK_NOPE_DIM:].astype(BF16)
    kr_b = jnp.broadcast_to(kr.reshape(b, 1, s, QK_ROPE_DIM), (b, N_HEADS_B, s, QK_ROPE_DIM))
    kfull_raw = jnp.concatenate([kv[..., :QK_NOPE_DIM], kr_b], axis=-1)
    qah = qk_prep_fwd(tag + "_qa_norm", qa_raw, sm["swa_q_norm"])
    kah = qk_prep_fwd(tag + "_ka_norm", ka_raw, sm["swa_k_norm"])
    oa = swa_fwd(tag + "_swa", qah, kah, va, aux["pos_col"], aux["pos_row"], aux["slopes"], sm["swa_sinks"].reshape(-1))
    qbh = qk_prep_fwd(tag + "_qb_norm", qb_raw, sm["mla_q_norm"], aux["rope"])
    kbh = qk_prep_fwd(tag + "_kb_norm", kfull_raw, sm["mla_k_norm"], aux["rope"])
    mla_t = MLA_T if s % MLA_T == 0 else BLOCK
    obt, lse, carried = mla_fwd(tag + "_mla", qbh, kbh, _blocks_t(vb, mla_t), plan)
    oab = _unheads(oa).astype(BF16)
    obb = obt.transpose(0, 2, 4, 1, 3).reshape(b * s, WIDTH_B).astype(BF16)
    ya = mm_nn(tag + "_branch_a", oab, w["wa"], tm=1024)
    yb = mm_nn(tag + "_branch_b", obb, w["wb"], tm=1024)
    mg = gate_fwd(tag + "_gate", proj, ya, yb, _O_GA, o_gb)
    x2 = gmm_rows(tag + "_out", mg, wts["wo"], x1)
    saved = dict(x1=x1, hb=hb, proj=proj, qa_raw=qa_raw, ka_raw=ka_raw, va=va, cq=cq, ckv=ckv, cqn=cqn, ckvn=ckvn,
                 qb_raw=qb_raw, kfull_raw=kfull_raw, vb=vb, qah=qah, kah=kah, qbh=qbh, kbh=kbh, obt=obt, lse=lse,
                 oab=oab, obb=obb, ya=ya, yb=yb, mg=mg)
    return x2, saved, carried


def _mixer_bwd(tag, sv, sm, w, wts, gbuf, aux, dx2, dx2b, plan=None):
    b, s = aux["b"], aux["s"]
    t, d = dx2.shape
    o_gb = _O_GA + d
    dmg = gmm_rows_dx(tag + "_d_out", dx2b, wts["wo"])
    gbuf["wo"] = gmm_rows_dw(tag + "_dw_out", sv["mg"], dx2b)
    dya, dyb, dga, dgb = gate_bwd(tag + "_dgate", sv["proj"], sv["ya"], sv["yb"], dmg, _O_GA, o_gb)
    dwa = mm_tn(tag + "_dw_branch_a", sv["oab"], dya, out_dtype=BF16)
    dwb = mm_tn(tag + "_dw_branch_b", sv["obb"], dyb, out_dtype=BF16)
    doa = _heads(mm_nt(tag + "_d_branch_a", dya, w["wa"], tm=1024), b, s, N_HEADS_A, HEAD_DIM_A)
    dob = _heads(mm_nt(tag + "_d_branch_b", dyb, w["wb"], tm=1024), b, s, N_HEADS_B, V_DIM_B)
    mla_t = sv["obt"].shape[-1]
    sinks = sm["swa_sinks"].reshape(-1)
    dqah, dkah, dva, dsinks = swa_bwd(tag + "_dswa", sv["qah"], sv["kah"], sv["va"], aux["pos_col"], aux["pos_row"],
                                      aux["slopes"], sinks, doa)
    dqa_raw, dg_swa_q = qk_prep_bwd(tag + "_dqa_norm", sv["qa_raw"], sm["swa_q_norm"], dqah)
    dka_raw, dg_swa_k = qk_prep_bwd(tag + "_dka_norm", sv["ka_raw"], sm["swa_k_norm"], dkah)
    dqbt, dkbh, dvb, carried = mla_bwd(tag + "_dmla", sv["qbh"], sv["kbh"], _blocks_t(sv["kbh"], mla_t), sv["vb"], dob.astype(BF16),
                                       _blocks_t(dob, mla_t), sv["obt"], sv["lse"], plan)
    dqbh = _unblocks_t(dqbt)
    dqb_raw, dg_mla_q = qk_prep_bwd(tag + "_dqb_norm", sv["qb_raw"], sm["mla_q_norm"], dqbh, aux["rope"])
    dkfull, dkr_sum, dg_mla_k = qk_prep_bwd(tag + "_dkb_norm", sv["kfull_raw"], sm["mla_k_norm"], dkbh, aux["rope"], head_sum=True)
    dq_tok = _unheads(dqb_raw)
    dkv_tok = _unheads(jnp.concatenate([dkfull[..., :QK_NOPE_DIM], dvb.astype(BF16)], axis=-1))
    dwuq = mm_tn(tag + "_dw_uq", sv["cqn"], dq_tok, tk=1024, out_dtype=BF16)
    dwukv = mm_tn(tag + "_dw_ukv", sv["ckvn"], dkv_tok, tk=1024, out_dtype=BF16)
    dcqn = mm_nt(tag + "_d_uq", dq_tok, w["wuq"], tm=1024)
    dckvn = mm_nt(tag + "_d_ukv", dkv_tok, w["wukv"], tm=1024)
    dcq, dg_q_lora = rms_bwd(tag + "_drms_cq", sv["cq"], sm["mla_q_lora_norm"], dcqn, want_f32=False, tm=1024)
    dckv, dg_kv_lora = rms_bwd(tag + "_drms_ckv", sv["ckv"], sm["mla_kv_lora_norm"], dckvn, want_f32=False, tm=1024)
    dkr = dkr_sum[..., QK_NOPE_DIM:].reshape(t, QK_ROPE_DIM).astype(BF16)
    parts = [_unheads(dqa_raw), _unheads(dka_raw), _unheads(dva.astype(BF16)), dcq, dckv, dga, dgb, dkr]
    pad = w["win"].shape[1] - sum(p.shape[1] for p in parts)
    if pad:
        parts.append(jnp.zeros((t, pad), BF16))
    dproj = jnp.concatenate(parts, axis=1)
    dwin = mm_tn(tag + "_dw_in", sv["hb"], dproj, tm=1024, tn=1664, tk=512, out_dtype=BF16)
    dh = mm_nt(tag + "_d_in", dproj, w["win"], tm=512, tn=1024, tk=1664)
    dx1, dx1b, dg_mix = rms_bwd(tag + "_drms", sv["x1"], sm["mix_norm"], dh, dres=dx2)
    wgrads = dict(win=dwin, wuq=dwuq, wukv=dwukv, wa=dwa, wb=dwb)
    sgrads = dict(mix_norm=dg_mix, swa_q_norm=dg_swa_q, swa_k_norm=dg_swa_k, swa_sinks=dsinks[:, 0].reshape(1, -1),
                  mla_q_lora_norm=dg_q_lora, mla_kv_lora_norm=dg_kv_lora, mla_q_norm=dg_mla_q, mla_k_norm=dg_mla_k)
    return dx1, dx1b, wgrads, sgrads, carried


def _layer_weights(win4, uq4, ukv4, branch4):
    win_ref = _join_cols(win4)
    d, width = win_ref.shape
    parts = [win_ref[:, :_O_GA], win_ref[:, _O_GA + QK_ROPE_DIM:], win_ref[:, _O_GA:_O_GA + QK_ROPE_DIM]]
    padded = -(-width // (2 * LANES)) * (2 * LANES)
    if padded > width:
        parts.append(jnp.zeros((d, padded - width), BF16))
    r = branch4.shape[1] // 2
    return dict(win=jnp.concatenate(parts, axis=1), wuq=_join_cols(uq4), wukv=_join_cols(ukv4),
                wa=_join_cols(branch4[:, :r]), wb=_join_cols(branch4[:, r:]))


def _mixer_grad_shards(g, width):
    win = g["win"]
    win_ref = jnp.concatenate([win[:, :_O_GA], win[:, width - QK_ROPE_DIM:width], win[:, _O_GA:width - QK_ROPE_DIM]], axis=1)
    return (_split_cols(win_ref), _split_cols(g["wuq"]), _split_cols(g["wukv"]),
            jnp.concatenate([_split_cols(g["wa"]), _split_cols(g["wb"])], axis=1))


BUFFER_TAGS = ("col", "down", "wo", "win", "uq", "ukv", "branch")


def _layer_buffers(gathered):
    col, down, wo, win4, uq4, ukv4, branch4 = gathered
    return dict(col=col, down=down, wo=wo), _layer_weights(win4, uq4, ukv4, branch4)


def _train_step(x, positions, target, packs, small, shard):
    depth = 2
    b, s, d = x.shape
    t = b * s
    posf = positions.astype(F32)
    half = QK_ROPE_DIM // 2
    inv_freq = ROPE_BASE ** (-jnp.arange(half, dtype=F32) / half)
    ang = posf[..., None] * inv_freq
    cos, sin = jnp.cos(ang), jnp.sin(ang)
    rope = (jnp.concatenate([jnp.ones((b, s, QK_NOPE_DIM), F32), cos, cos], axis=-1),
            jnp.concatenate([jnp.zeros((b, s, QK_NOPE_DIM), F32), sin, sin], axis=-1))
    slopes = jnp.exp2(-8.0 * (jnp.arange(N_HEADS_A, dtype=F32) + 1.0) / N_HEADS_A)
    aux = dict(b=b, s=s, pos_col=posf.reshape(b, s, 1), pos_row=posf.reshape(b, 1, s), rope=rope, slopes=slopes)

    def sm_of(l):
        return {n: small[n][l:l + 1] for n in SMALL_NAMES}

    width_in = N_SHARDS * packs[BUFFER_TAGS.index("win")].shape[2]
    wts, weights, saved, sgrads, parts, slots = [None] * depth, [None] * depth, [None] * depth, [None] * depth, [None] * depth, [None] * depth

    wts[0], weights[0] = _layer_buffers(comm_call("gather_l0", gather_plan(packs, 0)))
    h = x.reshape(t, d)
    for l in range(depth):
        sm = sm_of(l)
        h, s1 = _ffn_fwd(f"l{l}_ffn1", h, sm["ffn1_norm"], wts[l], 0)
        h, s2, carried = _mixer_fwd(f"l{l}_mix", h, sm, weights[l], wts[l], aux, gather_plan(packs, 1) if l == 0 else None)
        if l == 0:
            wts[1], weights[1] = _layer_buffers(carried)
        h, s3 = _ffn_fwd(f"l{l}_ffn2", h, sm["ffn2_norm"], wts[l], 1)
        saved[l] = (s1, s2, s3)
    loss, dh, dhb = loss_fwd_bwd("loss", h, target.reshape(t, d))

    for l in reversed(range(depth)):
        sm = sm_of(l)
        s1, s2, s3 = saved[l]
        gbuf = dict(col=None, down=None, wo=None)
        dh, dhb, dg_f2 = _ffn_bwd(f"l{l}_ffn2", s3, sm["ffn2_norm"], wts[l], gbuf, 1, dh, dhb)
        plan = scatter_plan(parts[1], 1) if l == 0 else None
        dh, dhb, wg, sg, carried = _mixer_bwd(f"l{l}_mix", s2, sm, weights[l], wts[l], gbuf, aux, dh, dhb, plan)
        if l == 0:
            slots[1] = carried
        dh, dhb, dg_f1 = _ffn_bwd(f"l{l}_ffn1", s1, sm["ffn1_norm"], wts[l], gbuf, 0, dh, dhb)
        sg.update(ffn1_norm=dg_f1, ffn2_norm=dg_f2)
        sgrads[l] = sg
        mine = [gbuf["col"], gbuf["down"], gbuf["wo"]] + list(_mixer_grad_shards(wg, width_in))
        theirs = comm_call(f"swap_l{l}", swap_plan(mine, l))
        parts[l] = [add_pair(f"l{l}_add_{tag}", g.reshape(-1, g.shape[-1]), o.reshape(-1, o.shape[-1])).reshape(g.shape)
                    for tag, g, o in zip(BUFFER_TAGS, mine, theirs)]
    slots[0] = comm_call("scatter_l0", scatter_plan(parts[0], 0))
    finals = [[sum_slots(f"l{l}_sum_{tag}", sl, p, shard) for tag, sl, p in zip(BUFFER_TAGS, slots[l], parts[l])] for l in range(depth)]
    shared = comm_call("share", share_plan(finals))
    nk = len(BUFFER_TAGS)
    return loss, dh.reshape(b, s, d), [shared[l * nk:(l + 1) * nk] for l in range(depth)], sgrads


def kernel(x, positions, ffn1_norm, ffn1_w_gate, ffn1_w_up, ffn1_w_down, mix_norm, w_in, swa_q_norm, swa_k_norm, swa_sinks, mla_q_lora_norm, mla_w_uq, mla_kv_lora_norm, mla_w_ukv, mla_q_norm, mla_k_norm, w_branch_a, w_branch_b, w_out, ffn2_norm, ffn2_w_gate, ffn2_w_up, ffn2_w_down, loss_target, m_ffn1_norm, m_ffn1_w_gate, m_ffn1_w_up, m_ffn1_w_down, m_mix_norm, m_w_in, m_swa_q_norm, m_swa_k_norm, m_swa_sinks, m_mla_q_lora_norm, m_mla_w_uq, m_mla_kv_lora_norm, m_mla_w_ukv, m_mla_q_norm, m_mla_k_norm, m_w_branch_a, m_w_branch_b, m_w_out, m_ffn2_norm, m_ffn2_w_gate, m_ffn2_w_up, m_ffn2_w_down, v_ffn1_norm, v_ffn1_w_gate, v_ffn1_w_up, v_ffn1_w_down, v_mix_norm, v_w_in, v_swa_q_norm, v_swa_k_norm, v_swa_sinks, v_mla_q_lora_norm, v_mla_w_uq, v_mla_kv_lora_norm, v_mla_w_ukv, v_mla_q_norm, v_mla_k_norm, v_w_branch_a, v_w_branch_b, v_w_out, v_ffn2_norm, v_ffn2_w_gate, v_ffn2_w_up, v_ffn2_w_down):
    args = (x, positions, ffn1_norm, ffn1_w_gate, ffn1_w_up, ffn1_w_down, mix_norm, w_in, swa_q_norm, swa_k_norm, swa_sinks, mla_q_lora_norm, mla_w_uq, mla_kv_lora_norm, mla_w_ukv, mla_q_norm, mla_k_norm, w_branch_a, w_branch_b, w_out, ffn2_norm, ffn2_w_gate, ffn2_w_up, ffn2_w_down, loss_target, m_ffn1_norm, m_ffn1_w_gate, m_ffn1_w_up, m_ffn1_w_down, m_mix_norm, m_w_in, m_swa_q_norm, m_swa_k_norm, m_swa_sinks, m_mla_q_lora_norm, m_mla_w_uq, m_mla_kv_lora_norm, m_mla_w_ukv, m_mla_q_norm, m_mla_k_norm, m_w_branch_a, m_w_branch_b, m_w_out, m_ffn2_norm, m_ffn2_w_gate, m_ffn2_w_up, m_ffn2_w_down, v_ffn1_norm, v_ffn1_w_gate, v_ffn1_w_up, v_ffn1_w_down, v_mix_norm, v_w_in, v_swa_q_norm, v_swa_k_norm, v_swa_sinks, v_mla_q_lora_norm, v_mla_w_uq, v_mla_kv_lora_norm, v_mla_w_ukv, v_mla_q_norm, v_mla_k_norm, v_w_branch_a, v_w_branch_b, v_w_out, v_ffn2_norm, v_ffn2_w_gate, v_ffn2_w_up, v_ffn2_w_down)
    a = dict(zip(ARG_NAMES, args, strict=True))
    x = a["x"]
    depth = a["ffn1_norm"].shape[0]
    d = x.shape[-1]
    assert depth == 2, "one layer slab per core of a chip"

    packs = [jnp.concatenate([a[n] for n in COL_NAMES], axis=1).astype(BF16),
             jnp.concatenate([a[n] for n in DOWN_NAMES], axis=1).astype(BF16),
             a["w_out"].astype(BF16), a["w_in"].astype(BF16), a["mla_w_uq"].astype(BF16), a["mla_w_ukv"].astype(BF16),
             jnp.concatenate([a["w_branch_a"], a["w_branch_b"]], axis=1).astype(BF16)]
    shard_me = (2 * lax.axis_index("x") + lax.axis_index("y")).astype(jnp.int32).reshape(1)
    small = {n: a[n] for n in SMALL_NAMES}

    loss, grad_x, summed, sgrads = _train_step(x, a["positions"], a["loss_target"], packs, small, shard_me)

    fcol, fdown, fwo, fwin, fuq, fukv, fbranch = [jnp.stack([summed[l][k] for l in range(depth)]) for k in range(len(BUFFER_TAGS))]
    n_ff, r_branch = a["ffn1_w_down"].shape[1], a["w_branch_a"].shape[1]
    grads = {n: fcol[:, i * d:(i + 1) * d] for i, n in enumerate(COL_NAMES)}
    grads.update({n: fdown[:, i * n_ff:(i + 1) * n_ff] for i, n in enumerate(DOWN_NAMES)})
    grads.update(w_out=fwo, w_in=fwin, mla_w_uq=fuq, mla_w_ukv=fukv, w_branch_a=fbranch[:, :r_branch], w_branch_b=fbranch[:, r_branch:])

    flat = jnp.concatenate([jnp.concatenate([sgrads[l][n].reshape(-1) for l in range(depth)]) for n in SMALL_NAMES] + [loss.reshape(-1)])
    n_small = flat.shape[0]
    rows = -(-n_small // (8 * LANES)) * 8
    pad = rows * LANES - n_small

    def small_pack(v):
        return jnp.pad(v, (0, pad)).reshape(rows, LANES)

    total = small_allreduce(small_pack(flat))
    w_s, m_s, v_s = (small_pack(jnp.concatenate([a[p + n].reshape(-1) for n in SMALL_NAMES] + [jnp.zeros((1,), F32)]))
                     for p in ("", "m_", "v_"))
    d_s, nm_s, nv_s = adamw("adamw_small", w_s, total, m_s, v_s)

    def small_unpack(buf):
        out, off, flat_b = {}, 0, buf.reshape(-1)
        for n in SMALL_NAMES:
            size = a[n].shape[0] * a[n].shape[1]
            out[n] = flat_b[off:off + size].reshape(a[n].shape)
            off += size
        return out

    grads.update(small_unpack(total))
    delta, new_m, new_v = small_unpack(d_s), small_unpack(nm_s), small_unpack(nv_s)
    for n in PACK_NAMES:
        shp = a[n].shape
        two_d = (shp[0] * shp[1], shp[2])
        dn, mn, vn = adamw("adamw_" + n, a[n].reshape(two_d), grads[n].reshape(two_d), a["m_" + n].reshape(two_d), a["v_" + n].reshape(two_d))
        delta[n], new_m[n], new_v[n] = dn.reshape(shp), mn.reshape(shp), vn.reshape(shp)

    loss_out = total.reshape(-1)[n_small - 1]
    return (loss_out, grad_x, *[grads[n] for n in WEIGHT_NAMES], *[delta[n] for n in WEIGHT_NAMES],
            *[new_m[n] for n in WEIGHT_NAMES], *[new_v[n] for n in WEIGHT_NAMES])
```

```python
import functools

import numpy as np
import jax
import jax.numpy as jnp
from jax import lax
from jax.experimental import pallas as pl
from jax.experimental.pallas import tpu as pltpu

F32 = jnp.float32
BF16 = jnp.bfloat16
MESH = pl.DeviceIdType.MESH

HEAD_DIM_A = 64
N_HEADS_A = 8
N_KV_HEADS_A = 2
GROUP_A = N_HEADS_A // N_KV_HEADS_A
BLOCK = 128
N_HEADS_B = 8
Q_LORA_RANK = 256
KV_LORA_RANK = 128
QK_NOPE_DIM = 64
QK_ROPE_DIM = 32
QK_DIM_B = QK_NOPE_DIM + QK_ROPE_DIM
V_DIM_B = 64
ROPE_BASE = 10000.0
WIDTH_A = N_HEADS_A * HEAD_DIM_A
WIDTH_B = N_HEADS_B * V_DIM_B
KV_WIDTH_A = N_KV_HEADS_A * HEAD_DIM_A
EPS = 1e-6
NEG = -1e30
ADAM_LR = 0.001
ADAM_B1 = 0.9
ADAM_B2 = 0.999
ADAM_EPS = 1e-08
ADAM_WD = 0.01
ADAM_STEP = 10

N_SHARDS = 4
N_DEV = 8
LANES = 128
VMEM_LIMIT = 48 * 1024 * 1024

PACK_NAMES = ("ffn1_w_gate", "ffn1_w_up", "ffn1_w_down", "w_in", "mla_w_uq", "mla_w_ukv",
              "w_branch_a", "w_branch_b", "w_out", "ffn2_w_gate", "ffn2_w_up", "ffn2_w_down")
COL_NAMES = ("ffn1_w_gate", "ffn1_w_up", "ffn2_w_gate", "ffn2_w_up")
DOWN_NAMES = ("ffn1_w_down", "ffn2_w_down")
SMALL_NAMES = ("ffn1_norm", "mix_norm", "swa_q_norm", "swa_k_norm", "swa_sinks", "mla_q_lora_norm",
               "mla_kv_lora_norm", "mla_q_norm", "mla_k_norm", "ffn2_norm")
WEIGHT_NAMES = ("ffn1_norm", "ffn1_w_gate", "ffn1_w_up", "ffn1_w_down", "mix_norm", "w_in", "swa_q_norm",
                "swa_k_norm", "swa_sinks", "mla_q_lora_norm", "mla_w_uq", "mla_kv_lora_norm", "mla_w_ukv",
                "mla_q_norm", "mla_k_norm", "w_branch_a", "w_branch_b", "w_out", "ffn2_norm", "ffn2_w_gate",
                "ffn2_w_up", "ffn2_w_down")


def _params(sem):
    return pltpu.CompilerParams(dimension_semantics=sem, vmem_limit_bytes=VMEM_LIMIT)


def _tile(n, want, align):
    if n <= want:
        return n
    t = (want // align) * align
    while t > align and n % t:
        t -= align
    assert t >= align and n % t == 0, (n, want, align)
    return t


def _mm_call(name, a, b, out_struct, grid, a_spec, b_spec, o_spec, dims, n_red, acc_shape, alpha=1.0, res=None, into=None):
    n_par = len(grid) - n_red

    def body(*refs):
        a_ref, b_ref = refs[:2]
        r_ref = refs[2] if res is not None else None
        o_ref, acc_ref = refs[-2:]
        part = lax.dot_general(a_ref[...], b_ref[...], dims, preferred_element_type=F32)

        def finish(total):
            if alpha != 1.0:
                total = total * alpha
            if r_ref is not None:
                total = r_ref[...] + total
            o_ref[...] = total.astype(o_ref.dtype)

        if n_red == 0:
            finish(part)
            return
        ids = [pl.program_id(n_par + i) for i in range(n_red)]
        first = functools.reduce(jnp.logical_and, [i == 0 for i in ids])
        last = functools.reduce(jnp.logical_and, [i == grid[n_par + k] - 1 for k, i in enumerate(ids)])

        @pl.when(first)
        def _():
            acc_ref[...] = part

        @pl.when(jnp.logical_not(first))
        def _():
            acc_ref[...] += part

        @pl.when(last)
        def _():
            finish(acc_ref[...])

    in_specs = [a_spec, b_spec] + ([o_spec] if res is not None else [])
    args = (a, b) + ((res,) if res is not None else ())
    aliases = {}
    if into is not None:
        aliases = {len(args): 0}
        in_specs.append(pl.BlockSpec(memory_space=pl.ANY))
        args = args + (into,)
    return pl.pallas_call(
        body, name=name, out_shape=out_struct, grid=grid, in_specs=in_specs, out_specs=o_spec,
        scratch_shapes=[pltpu.VMEM(acc_shape, F32)], input_output_aliases=aliases,
        compiler_params=_params(("parallel",) * n_par + ("arbitrary",) * n_red),
    )(*args)


_NN = (((1,), (0,)), ((), ()))
_NT = (((1,), (1,)), ((), ()))
_TN = (((0,), (0,)), ((), ()))


def mm_nn(name, a, b, tm=512, tn=1024, tk=1024, out_dtype=F32, alpha=1.0, res=None):
    (m, k), (_, n) = a.shape, b.shape
    tm, tn, tk = _tile(m, tm, 16), _tile(n, tn, LANES), _tile(k, tk, LANES)
    return _mm_call(name, a, b, jax.ShapeDtypeStruct((m, n), out_dtype), (m // tm, n // tn, k // tk),
                    pl.BlockSpec((tm, tk), lambda i, j, kk: (i, kk)), pl.BlockSpec((tk, tn), lambda i, j, kk: (kk, j)),
                    pl.BlockSpec((tm, tn), lambda i, j, kk: (i, j)), _NN, 1, (tm, tn), alpha, res)


def mm_nt(name, a, b, tm=512, tn=1024, tk=1024, out_dtype=F32, alpha=1.0, res=None):
    (m, n), (k, _) = a.shape, b.shape
    tm, tn, tk = _tile(m, tm, 16), _tile(k, tn, LANES), _tile(n, tk, LANES)
    return _mm_call(name, a, b, jax.ShapeDtypeStruct((m, k), out_dtype), (m // tm, k // tn, n // tk),
                    pl.BlockSpec((tm, tk), lambda i, j, kk: (i, kk)), pl.BlockSpec((tn, tk), lambda i, j, kk: (j, kk)),
                    pl.BlockSpec((tm, tn), lambda i, j, kk: (i, j)), _NT, 1, (tm, tn), alpha, res)


def mm_tn(name, a, b, tm=1024, tn=1024, tk=512, out_dtype=F32, alpha=1.0):
    (m, k), (_, n) = a.shape, b.shape
    tm, tn, tk = _tile(k, tm, LANES), _tile(n, tn, LANES), _tile(m, tk, 16)
    return _mm_call(name, a, b, jax.ShapeDtypeStruct((k, n), out_dtype), (k // tm, n // tn, m // tk),
                    pl.BlockSpec((tk, tm), lambda i, j, kk: (kk, i)), pl.BlockSpec((tk, tn), lambda i, j, kk: (kk, j)),
                    pl.BlockSpec((tm, tn), lambda i, j, kk: (i, j)), _TN, 1, (tm, tn), alpha)


def ffn_up_act(name, a, w, blk, tm=512):
    (m, k), (ns, _, n) = a.shape, w.shape
    tm = _tile(m, tm, 16)

    def body(a_ref, wg_ref, wu_ref, gu_ref, h_ref):
        av = a_ref[...]
        gate = jnp.dot(av, wg_ref[...], preferred_element_type=F32)
        up = jnp.dot(av, wu_ref[...], preferred_element_type=F32)
        gu_ref[0] = gate.astype(BF16)
        gu_ref[1] = up.astype(BF16)
        h_ref[...] = (gate * jax.nn.sigmoid(gate) * up).astype(BF16)

    return pl.pallas_call(
        body, name=name, grid=(ns, m // tm),
        out_shape=(jax.ShapeDtypeStruct((2, ns, m, n), BF16), jax.ShapeDtypeStruct((ns, m, n), BF16)),
        in_specs=[pl.BlockSpec((tm, k), lambda s, i: (i, 0)), pl.BlockSpec((None, k, n), lambda s, i: (s, blk, 0)),
                  pl.BlockSpec((None, k, n), lambda s, i: (s, blk + 1, 0))],
        out_specs=(pl.BlockSpec((2, None, tm, n), lambda s, i: (0, s, i, 0)), pl.BlockSpec((None, tm, n), lambda s, i: (s, i, 0))),
        compiler_params=_params(("parallel", "parallel")),
    )(a, w, w)


def ffn_down_dact(name, a, w, gu, n, blk, alpha, tm=512):
    (m, d), ns = a.shape, w.shape[0]
    tm = _tile(m, tm, 16)

    def body(a_ref, w_ref, gu_ref, o_ref):
        dh = lax.dot_general(a_ref[...], w_ref[...], _NT, preferred_element_type=F32) * alpha
        gate, up = gu_ref[0].astype(F32), gu_ref[1].astype(F32)
        s = jax.nn.sigmoid(gate)
        o_ref[0] = (dh * up * (s * (1.0 + gate * (1.0 - s)))).astype(BF16)
        o_ref[1] = (dh * (gate * s)).astype(BF16)

    gu_spec = pl.BlockSpec((2, None, tm, n), lambda s, i: (0, s, i, 0))
    return pl.pallas_call(
        body, name=name, grid=(ns, m // tm), out_shape=jax.ShapeDtypeStruct((2, ns, m, n), BF16),
        in_specs=[pl.BlockSpec((tm, d), lambda s, i: (i, 0)), pl.BlockSpec((None, n, d), lambda s, i: (s, blk, 0)), gu_spec],
        out_specs=gu_spec, compiler_params=_params(("parallel", "parallel")),
    )(a, w, gu)


def gmm_up_dw(name, a, dgu, into, shape, blk, tk=512):
    (m, k), (_, ns, _, n) = a.shape, dgu.shape
    tk = _tile(m, tk, 16)
    return _mm_call(name, a, dgu, jax.ShapeDtypeStruct(shape, BF16), (2, ns, m // tk),
                    pl.BlockSpec((tk, k), lambda j, s, kk: (kk, 0)), pl.BlockSpec((None, None, tk, n), lambda j, s, kk: (j, s, kk, 0)),
                    pl.BlockSpec((None, k, n), lambda j, s, kk: (s, blk + j, 0)), _TN, 1, (k, n), into=into)


def gmm_up_dx(name, dgu, w, k, blk, tm=512):
    _, ns, m, n = dgu.shape
    tm = _tile(m, tm, 16)
    return _mm_call(name, dgu, w, jax.ShapeDtypeStruct((m, k), F32), (m // tm, 2, ns),
                    pl.BlockSpec((None, None, tm, n), lambda i, j, s: (j, s, i, 0)),
                    pl.BlockSpec((None, k, n), lambda i, j, s: (s, blk + j, 0)),
                    pl.BlockSpec((tm, k), lambda i, j, s: (i, 0)), _NT, 2, (tm, k))


def gmm_down(name, h, w, blk, res, alpha, tm=512):
    (ns, m, n), d = h.shape, w.shape[2]
    tm = _tile(m, tm, 16)
    return _mm_call(name, h, w, jax.ShapeDtypeStruct((m, d), F32), (m // tm, ns),
                    pl.BlockSpec((None, tm, n), lambda i, s: (s, i, 0)), pl.BlockSpec((None, n, d), lambda i, s: (s, blk, 0)),
                    pl.BlockSpec((tm, d), lambda i, s: (i, 0)), _NN, 1, (tm, d), alpha, res)


def gmm_down_dw(name, h, b, into, shape, blk, alpha, tk=512):
    (ns, m, n), d = h.shape, b.shape[1]
    tk = _tile(m, tk, 16)
    return _mm_call(name, h, b, jax.ShapeDtypeStruct(shape, BF16), (ns, m // tk),
                    pl.BlockSpec((None, tk, n), lambda s, kk: (s, kk, 0)), pl.BlockSpec((tk, d), lambda s, kk: (kk, 0)),
                    pl.BlockSpec((None, n, d), lambda s, kk: (s, blk, 0)), _TN, 1, (n, d), alpha, into=into)


def gmm_rows(name, a, w, res, tm=512):
    (m, _), (ns, r, d) = a.shape, w.shape
    tm = _tile(m, tm, 16)
    return _mm_call(name, a, w, jax.ShapeDtypeStruct((m, d), F32), (m // tm, ns),
                    pl.BlockSpec((tm, r), lambda i, s: (i, s)), pl.BlockSpec((None, r, d), lambda i, s: (s, 0, 0)),
                    pl.BlockSpec((tm, d), lambda i, s: (i, 0)), _NN, 1, (tm, d), 1.0, res)


def gmm_rows_dx(name, a, w, tm=512):
    (m, d), (ns, r, _) = a.shape, w.shape
    tm = _tile(m, tm, 16)
    return _mm_call(name, a, w, jax.ShapeDtypeStruct((m, ns * r), F32), (ns, m // tm),
                    pl.BlockSpec((tm, d), lambda s, i: (i, 0)), pl.BlockSpec((None, r, d), lambda s, i: (s, 0, 0)),
                    pl.BlockSpec((tm, r), lambda s, i: (i, s)), _NT, 0, (8, LANES))


def gmm_rows_dw(name, a, b, tk=512):
    (m, da), d = a.shape, b.shape[1]
    ns, r = N_SHARDS, da // N_SHARDS
    tk = _tile(m, tk, 16)
    return _mm_call(name, a, b, jax.ShapeDtypeStruct((ns, r, d), BF16), (ns, m // tk),
                    pl.BlockSpec((tk, r), lambda s, kk: (kk, s)), pl.BlockSpec((tk, d), lambda s, kk: (kk, 0)),
                    pl.BlockSpec((None, r, d), lambda s, kk: (s, 0, 0)), _TN, 1, (r, d))


def rms_fwd(name, x, gain, tm=512, col_blk=0):
    m, d = x.shape[0], gain.shape[1]
    tm = _tile(m, tm, 16)

    def body(x_ref, g_ref, o_ref):
        xv = x_ref[...]
        r = lax.rsqrt(jnp.mean(xv * xv, axis=-1, keepdims=True) + EPS)
        o_ref[...] = (xv * r * g_ref[...]).astype(o_ref.dtype)

    return pl.pallas_call(
        body, name=name, out_shape=jax.ShapeDtypeStruct((m, d), BF16), grid=(m // tm,),
        in_specs=[pl.BlockSpec((tm, d), lambda i: (i, col_blk)), pl.BlockSpec((1, d), lambda i: (0, 0))],
        out_specs=pl.BlockSpec((tm, d), lambda i: (i, 0)), compiler_params=_params(("parallel",)),
    )(x, gain)


def rms_bwd(name, x, gain, dn, dres=None, want_f32=True, want_bf16=True, tm=512, col_blk=0):
    m, d = x.shape[0], gain.shape[1]
    tm = _tile(m, tm, 16)
    n_out = int(want_f32) + int(want_bf16)

    def body(*refs):
        x_ref, g_ref, dn_ref = refs[:3]
        pos = 3
        r_ref = None
        if dres is not None:
            r_ref = refs[pos]
            pos += 1
        outs = refs[pos:pos + n_out]
        dg_ref = refs[pos + n_out]
        xv = x_ref[...]
        r = lax.rsqrt(jnp.mean(xv * xv, axis=-1, keepdims=True) + EPS)
        xhat = xv * r
        dnv = dn_ref[...]
        dxhat = dnv * g_ref[...]
        dx = r * (dxhat - xhat * jnp.mean(dxhat * xhat, axis=-1, keepdims=True))
        if r_ref is not None:
            dx = r_ref[...] + dx
        for o in outs:
            o[...] = dx.astype(o.dtype)
        part = jnp.sum(dnv * xhat, axis=0, keepdims=True)

        @pl.when(pl.program_id(0) == 0)
        def _():
            dg_ref[...] = part

        @pl.when(pl.program_id(0) > 0)
        def _():
            dg_ref[...] += part

    row = pl.BlockSpec((tm, d), lambda i: (i, 0))
    vec = pl.BlockSpec((1, d), lambda i: (0, 0))
    out_shape = ([jax.ShapeDtypeStruct((m, d), F32)] if want_f32 else []) + ([jax.ShapeDtypeStruct((m, d), BF16)] if want_bf16 else [])
    res = pl.pallas_call(
        body, name=name, out_shape=tuple(out_shape) + (jax.ShapeDtypeStruct((1, d), F32),), grid=(m // tm,),
        in_specs=[pl.BlockSpec((tm, d), lambda i: (i, col_blk)), vec, row] + ([row] if dres is not None else []),
        out_specs=tuple([row] * n_out) + (vec,), compiler_params=_params(("arbitrary",)),
    )(*((x, gain, dn) + ((dres,) if dres is not None else ())))
    return res


def gate_fwd(name, proj, ya, yb, off_a, off_b, tm=256):
    m, d = ya.shape
    w = proj.shape[1]
    tm = _tile(m, tm, 16)

    def body(p_ref, ya_ref, yb_ref, o_ref):
        ga, gb = p_ref[:, off_a:off_a + d], p_ref[:, off_b:off_b + d]
        o_ref[...] = (jax.nn.sigmoid(ga) * ya_ref[...] + jax.nn.sigmoid(gb) * yb_ref[...]).astype(o_ref.dtype)

    row = pl.BlockSpec((tm, d), lambda i: (i, 0))
    return pl.pallas_call(
        body, name=name, out_shape=jax.ShapeDtypeStruct((m, d), BF16), grid=(m // tm,),
        in_specs=[pl.BlockSpec((tm, w), lambda i: (i, 0)), row, row], out_specs=row, compiler_params=_params(("parallel",)),
    )(proj, ya, yb)


def gate_bwd(name, proj, ya, yb, dmerged, off_a, off_b, tm=256):
    m, d = ya.shape
    w = proj.shape[1]
    tm = _tile(m, tm, 16)

    def body(p_ref, ya_ref, yb_ref, dm_ref, dya_ref, dyb_ref, dga_ref, dgb_ref):
        sa, sb = jax.nn.sigmoid(p_ref[:, off_a:off_a + d]), jax.nn.sigmoid(p_ref[:, off_b:off_b + d])
        dm = dm_ref[...]
        dya_ref[...] = (dm * sa).astype(BF16)
        dyb_ref[...] = (dm * sb).astype(BF16)
        dga_ref[...] = (dm * ya_ref[...] * (sa * (1.0 - sa))).astype(BF16)
        dgb_ref[...] = (dm * yb_ref[...] * (sb * (1.0 - sb))).astype(BF16)

    row = pl.BlockSpec((tm, d), lambda i: (i, 0))
    o = jax.ShapeDtypeStruct((m, d), BF16)
    return pl.pallas_call(
        body, name=name, out_shape=(o, o, o, o), grid=(m // tm,),
        in_specs=[pl.BlockSpec((tm, w), lambda i: (i, 0)), row, row, row], out_specs=(row, row, row, row),
        compiler_params=_params(("parallel",)),
    )(proj, ya, yb, dmerged)


def loss_fwd_bwd(name, y, target, tm=512):
    m, d = y.shape
    tm = _tile(m, tm, 16)

    def body(y_ref, t_ref, l_ref, dy_ref, dyb_ref):
        err = y_ref[...] - t_ref[...]
        dy = err * (1.0 / d)
        dy_ref[...] = dy
        dyb_ref[...] = dy.astype(BF16)
        part = 0.5 * jnp.sum(jnp.mean(err * err, axis=-1, keepdims=True), axis=0, keepdims=True)

        @pl.when(pl.program_id(0) == 0)
        def _():
            l_ref[...] = part

        @pl.when(pl.program_id(0) > 0)
        def _():
            l_ref[...] += part

    row = pl.BlockSpec((tm, d), lambda i: (i, 0))
    return pl.pallas_call(
        body, name=name, grid=(m // tm,),
        out_shape=(jax.ShapeDtypeStruct((1, 1), F32), jax.ShapeDtypeStruct((m, d), F32), jax.ShapeDtypeStruct((m, d), BF16)),
        in_specs=[row, row], out_specs=(pl.BlockSpec((1, 1), lambda i: (0, 0)), row, row),
        compiler_params=_params(("arbitrary",)),
    )(y, target)


def adamw(name, w, g, m, v):
    r, c = w.shape
    tr = _tile(r, max(8, (2 * 1024 * 1024) // (4 * c) // 8 * 8), 8)
    c1 = np.float32(1.0 - ADAM_B1 ** ADAM_STEP)
    c2 = np.float32(1.0 - ADAM_B2 ** ADAM_STEP)

    def body(w_ref, g_ref, m_ref, v_ref, d_ref, nm_ref, nv_ref):
        gv = g_ref[...]
        nm = ADAM_B1 * m_ref[...] + (1.0 - ADAM_B1) * gv
        nv = ADAM_B2 * v_ref[...] + (1.0 - ADAM_B2) * (gv * gv)
        d_ref[...] = -ADAM_LR * ((nm / c1) / (jnp.sqrt(nv / c2) + ADAM_EPS) + ADAM_WD * w_ref[...])
        nm_ref[...] = nm
        nv_ref[...] = nv

    row = pl.BlockSpec((tr, c), lambda i: (i, 0))
    o = jax.ShapeDtypeStruct((r, c), F32)
    return pl.pallas_call(
        body, name=name, out_shape=(o, o, o), grid=(r // tr,), in_specs=[row] * 4, out_specs=(row, row, row),
        compiler_params=_params(("parallel",)),
    )(w, g, m, v)


HEAD_PAD = LANES


def _rope_rot():
    r = np.zeros((HEAD_PAD, HEAD_PAD), np.float32)
    half = QK_ROPE_DIM // 2
    for j in range(half):
        r[QK_NOPE_DIM + half + j, QK_NOPE_DIM + j] = -1.0
        r[QK_NOPE_DIM + j, QK_NOPE_DIM + half + j] = 1.0
    return r


def _head_spec(s, blk0):
    return pl.BlockSpec((None, s, HEAD_PAD), lambda bi, hi: (bi, 0, blk0 + hi))


def qk_prep_fwd(name, x, blk0, n_heads, d_real, gain, rope=None, extra=None):
    b, s, _ = x.shape

    def body(*refs):
        x_ref, g_ref = refs[0], refs[1]
        pos = 2
        xv = x_ref[...]
        if extra is not None:
            xv = xv + refs[pos][...]
            pos += 1
        y = xv * lax.rsqrt(jnp.sum(xv * xv, axis=-1, keepdims=True) * (1.0 / d_real) + EPS) * g_ref[...]
        if rope is not None:
            c_ref, s_ref, r_ref = refs[pos:pos + 3]
            rot = jnp.dot(y, r_ref[...], precision=lax.Precision.HIGHEST, preferred_element_type=F32)
            y = y * c_ref[...] + rot * s_ref[...]
        refs[-1][...] = y.astype(BF16)

    vec = pl.BlockSpec((1, HEAD_PAD), lambda bi, hi: (0, 0))
    tab = pl.BlockSpec((None, s, HEAD_PAD), lambda bi, hi: (bi, 0, 0))
    in_specs, args = [_head_spec(s, blk0), vec], [x, gain]
    if extra is not None:
        e_blk = extra[1]
        in_specs.append(pl.BlockSpec((None, s, HEAD_PAD), lambda bi, hi: (bi, 0, e_blk)))
        args.append(extra[0])
    if rope is not None:
        in_specs += [tab, tab, pl.BlockSpec((HEAD_PAD, HEAD_PAD), lambda bi, hi: (0, 0))]
        args += [rope[0], rope[1], jnp.asarray(_rope_rot())]
    return pl.pallas_call(
        body, name=name, out_shape=jax.ShapeDtypeStruct((b, s, n_heads * HEAD_PAD), BF16), grid=(b, n_heads),
        in_specs=in_specs, out_specs=_head_spec(s, 0), compiler_params=_params(("parallel", "parallel")),
    )(*args)


def qk_prep_bwd(name, x, blk0, n_heads, d_real, gain, dxh, rope=None, extra=None, head_sum=False):
    b, s, _ = x.shape

    def body(*refs):
        x_ref, g_ref, dy_ref = refs[:3]
        pos = 3
        xv = x_ref[...]
        if extra is not None:
            xv = xv + refs[pos][...]
            pos += 1
        dy = dy_ref[...]
        if rope is not None:
            c_ref, s_ref, rt_ref = refs[pos:pos + 3]
            pos += 3
            dy = dy * c_ref[...] + jnp.dot(dy * s_ref[...], rt_ref[...], precision=lax.Precision.HIGHEST, preferred_element_type=F32)
        outs = refs[pos:]
        dx_ref, dg_ref = outs[0], outs[-1]
        r = lax.rsqrt(jnp.sum(xv * xv, axis=-1, keepdims=True) * (1.0 / d_real) + EPS)
        xhat = xv * r
        dxhat = dy * g_ref[...]
        dx = r * (dxhat - xhat * (jnp.sum(dxhat * xhat, axis=-1, keepdims=True) * (1.0 / d_real)))
        dx_ref[...] = dx.astype(dx_ref.dtype)
        part = jnp.sum(dy * xhat, axis=0, keepdims=True)
        first = jnp.logical_and(pl.program_id(0) == 0, pl.program_id(1) == 0)

        @pl.when(first)
        def _():
            dg_ref[...] = part

        @pl.when(jnp.logical_not(first))
        def _():
            dg_ref[...] += part

        if head_sum:
            hs_ref = outs[1]

            @pl.when(pl.program_id(1) == 0)
            def _():
                hs_ref[...] = dx

            @pl.when(pl.program_id(1) > 0)
            def _():
                hs_ref[...] += dx

    vec = pl.BlockSpec((1, HEAD_PAD), lambda bi, hi: (0, 0))
    tab = pl.BlockSpec((None, s, HEAD_PAD), lambda bi, hi: (bi, 0, 0))
    in_specs, args = [_head_spec(s, blk0), vec, _head_spec(s, 0)], [x, gain, dxh]
    if extra is not None:
        e_blk = extra[1]
        in_specs.append(pl.BlockSpec((None, s, HEAD_PAD), lambda bi, hi: (bi, 0, e_blk)))
        args.append(extra[0])
    if rope is not None:
        in_specs += [tab, tab, pl.BlockSpec((HEAD_PAD, HEAD_PAD), lambda bi, hi: (0, 0))]
        args += [rope[0], rope[1], jnp.asarray(_rope_rot().T.copy())]
    out_shape, out_specs = [jax.ShapeDtypeStruct((b, s, n_heads * HEAD_PAD), BF16)], [_head_spec(s, 0)]
    if head_sum:
        out_shape.append(jax.ShapeDtypeStruct((b, s, HEAD_PAD), F32))
        out_specs.append(tab)
    out_shape.append(jax.ShapeDtypeStruct((1, HEAD_PAD), F32))
    out_specs.append(vec)
    return pl.pallas_call(
        body, name=name, out_shape=tuple(out_shape), grid=(b, n_heads), in_specs=in_specs, out_specs=tuple(out_specs),
        compiler_params=_params(("arbitrary", "arbitrary")),
    )(*args)


def _swa_specs(v_blk0):
    q = pl.BlockSpec((None, BLOCK, GROUP_A * HEAD_PAD), lambda b, kv, n: (b, n, kv))
    kprev = pl.BlockSpec((None, BLOCK, HEAD_PAD), lambda b, kv, n: (b, jnp.maximum(n - 1, 0), kv))
    kcur = pl.BlockSpec((None, BLOCK, HEAD_PAD), lambda b, kv, n: (b, n, kv))
    vprev = pl.BlockSpec((None, BLOCK, HEAD_PAD), lambda b, kv, n: (b, jnp.maximum(n - 1, 0), v_blk0 + kv))
    vcur = pl.BlockSpec((None, BLOCK, HEAD_PAD), lambda b, kv, n: (b, n, v_blk0 + kv))
    pcol = pl.BlockSpec((None, BLOCK, 1), lambda b, kv, n: (b, n, 0))
    prow_prev = pl.BlockSpec((None, 1, BLOCK), lambda b, kv, n: (b, 0, jnp.maximum(n - 1, 0)))
    prow_cur = pl.BlockSpec((None, 1, BLOCK), lambda b, kv, n: (b, 0, n))
    smem = pl.BlockSpec(memory_space=pltpu.SMEM)
    return q, kprev, kcur, vprev, vcur, pcol, prow_prev, prow_cur, smem


def _swa_probs(q, kk, dist, valid, slope, sink):
    sc = lax.dot_general(q, kk, _NT, preferred_element_type=F32) * (HEAD_DIM_A ** -0.5)
    sc = sc - slope * dist
    sc = jnp.where(valid, sc, NEG)
    m = jnp.maximum(jnp.max(sc, axis=-1, keepdims=True), sink)
    e = jnp.exp(sc - m)
    es = jnp.exp(sink - m)
    inv = 1.0 / (jnp.sum(e, axis=-1, keepdims=True) + es)
    return e * inv, es * inv


def _swa_window(n, kp_ref, kc_ref, vp_ref, vc_ref, pc_ref, prp_ref, prc_ref):
    kk = jnp.concatenate([kp_ref[...], kc_ref[...]], axis=0)
    vv = jnp.concatenate([vp_ref[...], vc_ref[...]], axis=0).astype(BF16)
    dist = pc_ref[...] - jnp.concatenate([prp_ref[...], prc_ref[...]], axis=1)
    qi = lax.broadcasted_iota(jnp.int32, (BLOCK, 2 * BLOCK), 0) + BLOCK
    ki = lax.broadcasted_iota(jnp.int32, (BLOCK, 2 * BLOCK), 1)
    diff = qi - ki
    valid = (diff >= 0) & (diff < BLOCK) & ((n > 0) | (ki >= BLOCK))
    return kk, vv, dist, valid


def swa_fwd(name, q, k, vsrc, v_blk0, pos_col, pos_row, slopes, sinks):
    b, s, _ = q.shape
    qs, kprev, kcur, vprev, vcur, pcol, prp, prc, smem = _swa_specs(v_blk0)

    def body(q_ref, kp_ref, kc_ref, vp_ref, vc_ref, pc_ref, prp_ref, prc_ref, sl_ref, sk_ref, o_ref):
        kv, n = pl.program_id(1), pl.program_id(2)
        kk, vv, dist, valid = _swa_window(n, kp_ref, kc_ref, vp_ref, vc_ref, pc_ref, prp_ref, prc_ref)
        for g in range(GROUP_A):
            hd = kv * GROUP_A + g
            lanes = slice(g * HEAD_PAD, (g + 1) * HEAD_PAD)
            p, _ = _swa_probs(q_ref[:, lanes], kk, dist, valid, sl_ref[hd], sk_ref[hd])
            o_ref[:, lanes] = jnp.dot(p.astype(BF16), vv, preferred_element_type=F32).astype(BF16)

    return pl.pallas_call(
        body, name=name, out_shape=jax.ShapeDtypeStruct(q.shape, BF16), grid=(b, N_KV_HEADS_A, s // BLOCK),
        in_specs=[qs, kprev, kcur, vprev, vcur, pcol, prp, prc, smem, smem], out_specs=qs,
        compiler_params=_params(("parallel", "parallel", "parallel")),
    )(q, k, k, vsrc, vsrc, pos_col, pos_row, pos_row, slopes, sinks)


def swa_bwd(name, q, k, vsrc, v_blk0, pos_col, pos_row, slopes, sinks, do):
    b, s, _ = q.shape
    qs, kprev, kcur, vprev, vcur, pcol, prp, prc, smem = _swa_specs(v_blk0)

    def body(q_ref, kp_ref, kc_ref, vp_ref, vc_ref, pc_ref, prp_ref, prc_ref, sl_ref, sk_ref, do_ref, dq_ref, dk_ref, dv_ref, ds_ref):
        bi, kv, n = pl.program_id(0), pl.program_id(1), pl.program_id(2)
        kk, vv, dist, valid = _swa_window(n, kp_ref, kc_ref, vp_ref, vc_ref, pc_ref, prp_ref, prc_ref)

        @pl.when((bi == 0) & (kv == 0) & (n == 0))
        def _():
            ds_ref[...] = jnp.zeros_like(ds_ref)

        @pl.when(n == 0)
        def _():
            dk_ref[...] = jnp.zeros_like(dk_ref)
            dv_ref[...] = jnp.zeros_like(dv_ref)

        dkk = jnp.zeros((2 * BLOCK, HEAD_PAD), F32)
        dvv = jnp.zeros((2 * BLOCK, HEAD_PAD), F32)
        head_row = lax.broadcasted_iota(jnp.int32, (N_HEADS_A, LANES), 0)
        dsink = jnp.zeros((N_HEADS_A, LANES), F32)
        for g in range(GROUP_A):
            hd = kv * GROUP_A + g
            lanes = slice(g * HEAD_PAD, (g + 1) * HEAD_PAD)
            qg = q_ref[:, lanes]
            p, ps = _swa_probs(qg, kk, dist, valid, sl_ref[hd], sk_ref[hd])
            dob = do_ref[:, lanes].astype(BF16)
            dvv = dvv + lax.dot_general(p.astype(BF16), dob, _TN, preferred_element_type=F32)
            dp = lax.dot_general(dob, vv, _NT, preferred_element_type=F32)
            rs = jnp.sum(p * dp, axis=-1, keepdims=True)
            dsb = (p * (dp - rs) * (HEAD_DIM_A ** -0.5)).astype(BF16)
            dq_ref[:, lanes] = jnp.dot(dsb, kk, preferred_element_type=F32)
            dkk = dkk + lax.dot_general(dsb, qg, _TN, preferred_element_type=F32)
            dsink = dsink + jnp.where(head_row == hd, -jnp.sum(ps * rs), 0.0)
        ds_ref[...] += dsink

        @pl.when(n > 0)
        def _():
            start = pl.multiple_of((n - 1) * BLOCK, BLOCK)
            dk_ref[pl.ds(start, 2 * BLOCK), :] += dkk
            dv_ref[pl.ds(start, 2 * BLOCK), :] += dvv

        @pl.when(n == 0)
        def _():
            dk_ref[0:BLOCK, :] += dkk[BLOCK:, :]
            dv_ref[0:BLOCK, :] += dvv[BLOCK:, :]

    kv_full = pl.BlockSpec((None, s, HEAD_PAD), lambda bi, kv, n: (bi, 0, kv))
    kv_shape = jax.ShapeDtypeStruct(k.shape, F32)
    return pl.pallas_call(
        body, name=name, grid=(b, N_KV_HEADS_A, s // BLOCK),
        out_shape=(jax.ShapeDtypeStruct(q.shape, F32), kv_shape, kv_shape, jax.ShapeDtypeStruct((N_HEADS_A, LANES), F32)),
        in_specs=[qs, kprev, kcur, vprev, vcur, pcol, prp, prc, smem, smem, qs],
        out_specs=(qs, kv_full, kv_full, pl.BlockSpec((N_HEADS_A, LANES), lambda bi, kv, n: (0, 0))),
        compiler_params=_params(("arbitrary", "arbitrary", "arbitrary")),
    )(q, k, k, vsrc, vsrc, pos_col, pos_row, pos_row, slopes, sinks, do)


MLA_T = 256


def _hosted(plan, n_in, n_out, b, h):
    if plan is None:
        return [], [], [], [], ("parallel", "parallel"), (lambda refs: None), (lambda refs: None)
    ni, no = len(plan.ins), len(plan.out_shapes)

    def split(refs):
        return refs[n_in:n_in + ni], refs[n_in + ni + n_out:n_in + ni + n_out + no], refs[-2], refs[-1]

    def at_start(refs):
        @pl.when((pl.program_id(0) == 0) & (pl.program_id(1) == 0))
        def _():
            plan.start(*split(refs))

    def at_end(refs):
        @pl.when((pl.program_id(0) == b - 1) & (pl.program_id(1) == h - 1))
        def _():
            plan.finish(*split(refs))

    return plan.ins, [_HBM] * ni, plan.out_shapes, [_HBM] * no, ("arbitrary", "arbitrary"), at_start, at_end


def mla_fwd(name, q, k, v, plan=None):
    b, s, w = q.shape
    h = w // HEAD_PAD
    t = MLA_T if s % MLA_T == 0 else BLOCK
    nb = s // t
    scale = QK_DIM_B ** -0.5
    p_ins, p_in_specs, p_outs, p_out_specs, semantics, at_start, at_end = _hosted(plan, 3, 3, b, h)

    def body(*refs):
        q_ref, k_ref, v_ref = refs[:3]
        o_ref, ob_ref, lse_ref = refs[3 + len(p_ins):6 + len(p_ins)]
        at_start(refs)
        causal = lax.broadcasted_iota(jnp.int32, (t, t), 0) <= lax.broadcasted_iota(jnp.int32, (t, t), 1)
        for i in range(nb):
            q_i = q_ref[i * t:(i + 1) * t, :]

            def step(j, carry, q_i=q_i, diagonal=False):
                m, l, acc = carry
                rows = slice(j * t, (j + 1) * t) if diagonal else pl.ds(pl.multiple_of(j * t, t), t)
                st = lax.dot_general(k_ref[rows, :], q_i, _NT, preferred_element_type=F32) * scale
                if diagonal:
                    st = jnp.where(causal, st, NEG)
                m_new = jnp.maximum(m, jnp.max(st, axis=0, keepdims=True))
                a = jnp.exp(m - m_new)
                p = jnp.exp(st - m_new)
                l = a * l + jnp.sum(p, axis=0, keepdims=True)
                acc = a * acc + lax.dot_general(v_ref[rows, :], p.astype(BF16), _TN, preferred_element_type=F32)
                return m_new, l, acc

            carry = (jnp.full((1, t), NEG, F32), jnp.zeros((1, t), F32), jnp.zeros((HEAD_PAD, t), F32))
            if i > 0:
                carry = lax.fori_loop(0, i, step, carry)
            m, l, acc = step(i, carry, diagonal=True)
            o_i = (acc / l).T
            o_ref[i * t:(i + 1) * t, :] = o_i
            ob_ref[i * t:(i + 1) * t, :] = o_i.astype(BF16)
            lse_ref[i] = m + jnp.log(l)
        at_end(refs)

    hs = _head_spec(s, 0)
    ls = pl.BlockSpec((None, None, nb, 1, t), lambda bi, hi: (bi, hi, 0, 0, 0))
    res = pl.pallas_call(
        body, name=name, grid=(b, h),
        out_shape=(jax.ShapeDtypeStruct((b, s, w), F32), jax.ShapeDtypeStruct((b, s, w), BF16),
                   jax.ShapeDtypeStruct((b, h, nb, 1, t), F32)) + tuple(p_outs),
        in_specs=[hs, hs, hs] + p_in_specs, out_specs=(hs, hs, ls) + tuple(p_out_specs),
        scratch_shapes=plan.scratch() if plan is not None else [], compiler_params=_params(semantics),
    )(q, k, v, *p_ins)
    return res[0], res[1], res[2], list(res[3:])


def mla_bwd(name, q, k, v, o, do, lse, plan=None):
    b, s, w = q.shape
    h = w // HEAD_PAD
    _, _, nb, _, t = lse.shape
    scale = QK_DIM_B ** -0.5
    p_ins, p_in_specs, p_outs, p_out_specs, semantics, at_start, at_end = _hosted(plan, 6, 3, b, h)

    def body(*refs):
        q_ref, k_ref, v_ref, o_ref, do_ref, lse_ref = refs[:6]
        dq_ref, dk_ref, dv_ref = refs[6 + len(p_ins):9 + len(p_ins)]
        dv_acc = refs[9 + len(p_ins) + len(p_outs)]
        at_start(refs)
        causal = lax.broadcasted_iota(jnp.int32, (t, t), 0) <= lax.broadcasted_iota(jnp.int32, (t, t), 1)
        dk_ref[...] = jnp.zeros_like(dk_ref)
        dv_acc[...] = jnp.zeros_like(dv_acc)
        for i in range(nb):
            q_i = q_ref[i * t:(i + 1) * t, :]
            do_i = do_ref[i * t:(i + 1) * t, :]
            delta = jnp.sum((o_ref[i * t:(i + 1) * t, :] * do_i).T, axis=0, keepdims=True)
            do_b = do_i.astype(BF16)
            lse_i = lse_ref[i]

            def step(j, dqt, q_i=q_i, do_b=do_b, delta=delta, lse_i=lse_i, diagonal=False):
                rows = slice(j * t, (j + 1) * t) if diagonal else pl.ds(pl.multiple_of(j * t, t), t)
                k_j = k_ref[rows, :]
                st = lax.dot_general(k_j, q_i, _NT, preferred_element_type=F32) * scale
                if diagonal:
                    st = jnp.where(causal, st, NEG)
                pt = jnp.exp(st - lse_i)
                dpt = lax.dot_general(v_ref[rows, :], do_b, _NT, preferred_element_type=F32)
                dst = (pt * (dpt - delta) * scale).astype(BF16)
                dv_acc[rows, :] += jnp.dot(pt.astype(BF16), do_b, preferred_element_type=F32)
                dk_ref[rows, :] += jnp.dot(dst, q_i, preferred_element_type=F32)
                return dqt + lax.dot_general(k_j, dst, _TN, preferred_element_type=F32)

            dqt = jnp.zeros((HEAD_PAD, t), F32)
            if i > 0:
                dqt = lax.fori_loop(0, i, step, dqt)
            dq_ref[i * t:(i + 1) * t, :] = step(i, dqt, diagonal=True).T
        dv_ref[...] = dv_acc[...].astype(BF16)
        at_end(refs)

    hs = _head_spec(s, 0)
    ls = pl.BlockSpec((None, None, nb, 1, t), lambda bi, hi: (bi, hi, 0, 0, 0))
    res = pl.pallas_call(
        body, name=name, grid=(b, h),
        out_shape=(jax.ShapeDtypeStruct((b, s, w), F32), jax.ShapeDtypeStruct((b, s, w), F32),
                   jax.ShapeDtypeStruct((b, s, w), BF16)) + tuple(p_outs),
        in_specs=[hs, hs, hs, hs, hs, ls] + p_in_specs, out_specs=(hs, hs, hs) + tuple(p_out_specs),
        scratch_shapes=[pltpu.VMEM((s, HEAD_PAD), F32)] + (plan.scratch() if plan is not None else []),
        compiler_params=_params(semantics),
    )(q, k, v, o, do, lse, *p_ins)
    return res[0], res[1], res[2], list(res[3:])


_HBM = pl.BlockSpec(memory_space=pltpu.HBM)


def _place():
    x, y, c = lax.axis_index("x"), lax.axis_index("y"), lax.axis_index("c")
    chips = [(1 - x, y), (x, 1 - y), (1 - x, 1 - y)]
    return x, y, c, chips


def _remote(src, dst, send_sem, recv_sem, dev):
    return pltpu.make_async_remote_copy(src_ref=src, dst_ref=dst, send_sem=send_sem, recv_sem=recv_sem,
                                        device_id=dev, device_id_type=MESH)


class CommPlan:
    def __init__(self, ins, out_shapes, n_sems, start, finish, aliases=None):
        self.ins, self.out_shapes, self.n_sems = list(ins), list(out_shapes), n_sems
        self.start, self.finish, self.aliases = start, finish, dict(aliases or {})

    def scratch(self):
        return [pltpu.SemaphoreType.DMA((self.n_sems,)), pltpu.SemaphoreType.DMA((self.n_sems,))]


def comm_call(name, plan):
    ni, no = len(plan.ins), len(plan.out_shapes)

    def body(*refs):
        ins, outs, (send_sems, recv_sems) = refs[:ni], refs[ni:ni + no], refs[ni + no:]
        plan.start(ins, outs, send_sems, recv_sems)
        plan.finish(ins, outs, send_sems, recv_sems)

    return pl.pallas_call(
        body, name=name, out_shape=tuple(plan.out_shapes), in_specs=[_HBM] * ni, out_specs=tuple([_HBM] * no),
        input_output_aliases=plan.aliases, scratch_shapes=plan.scratch(),
    )(*plan.ins)


def gather_plan(packs, l):
    nk = len(packs)

    def parts(p_refs, o_refs, ss, rs):
        x, y, c, chips = _place()
        s_me = 2 * x + y
        sibling = (x, y, 1 - c)
        first = [_remote(p_refs[k].at[l], o_refs[k].at[s_me], ss.at[j * nk + k], rs.at[j * nk + k], (cx, cy, c))
                 for j, (cx, cy) in enumerate(chips) for k in range(nk)]
        own = [_remote(p_refs[k].at[l], o_refs[k].at[s_me], ss.at[6 * nk + k], rs.at[6 * nk + k], sibling) for k in range(nk)]
        return c, chips, sibling, first, own

    def start(p_refs, o_refs, ss, rs):
        c, _, _, first, own = parts(p_refs, o_refs, ss, rs)
        for cp in own:
            cp.start()

        @pl.when(c == l)
        def _():
            for cp in first:
                cp.start()

    def finish(p_refs, o_refs, ss, rs):
        c, chips, sibling, first, own = parts(p_refs, o_refs, ss, rs)

        @pl.when(c == l)
        def _():
            passed = []
            for j, (cx, cy) in enumerate(chips):
                for k in range(nk):
                    blk = o_refs[k].at[2 * cx + cy]
                    _remote(blk, blk, ss.at[j * nk + k], rs.at[j * nk + k], (cx, cy, c)).wait_recv()
                    fwd = _remote(blk, blk, ss.at[(3 + j) * nk + k], rs.at[(3 + j) * nk + k], sibling)
                    fwd.start()
                    passed.append(fwd)
            for cp in first + passed:
                cp.wait_send()

        @pl.when(c != l)
        def _():
            for j, (cx, cy) in enumerate(chips):
                for k in range(nk):
                    blk = o_refs[k].at[2 * cx + cy]
                    _remote(blk, blk, ss.at[(3 + j) * nk + k], rs.at[(3 + j) * nk + k], sibling).wait_recv()

        for cp in own:
            cp.wait()

    outs = [jax.ShapeDtypeStruct((N_SHARDS,) + p.shape[1:], p.dtype) for p in packs]
    return CommPlan(packs, outs, 7 * nk, start, finish)


def swap_plan(grads, l):
    nk = len(grads)

    def copies(g_refs, o_refs, ss, rs):
        x, y, c, _ = _place()
        return c, [_remote(g_refs[k], o_refs[k], ss.at[k], rs.at[k], (x, y, 1 - c)) for k in range(nk)]

    def start(g_refs, o_refs, ss, rs):
        c, cps = copies(g_refs, o_refs, ss, rs)

        @pl.when(c != l)
        def _():
            for cp in cps:
                cp.start()

    def finish(g_refs, o_refs, ss, rs):
        c, cps = copies(g_refs, o_refs, ss, rs)

        @pl.when(c != l)
        def _():
            for cp in cps:
                cp.wait_send()

        @pl.when(c == l)
        def _():
            for cp in cps:
                cp.wait_recv()

    return CommPlan(grads, [jax.ShapeDtypeStruct(g.shape, g.dtype) for g in grads], nk, start, finish)


def scatter_plan(parts, l):
    nk = len(parts)

    def copies(p_refs, o_refs, ss, rs):
        x, y, c, chips = _place()
        s_me = 2 * x + y
        sends = [_remote(p_refs[k].at[2 * cx + cy], o_refs[k].at[s_me], ss.at[j * nk + k], rs.at[j * nk + k], (cx, cy, c))
                 for j, (cx, cy) in enumerate(chips) for k in range(nk)]
        return c, chips, sends

    def start(p_refs, o_refs, ss, rs):
        c, _, sends = copies(p_refs, o_refs, ss, rs)

        @pl.when(c == l)
        def _():
            for cp in sends:
                cp.start()

    def finish(p_refs, o_refs, ss, rs):
        c, chips, sends = copies(p_refs, o_refs, ss, rs)

        @pl.when(c == l)
        def _():
            for j, (cx, cy) in enumerate(chips):
                for k in range(nk):
                    slot = o_refs[k].at[2 * cx + cy]
                    _remote(slot, slot, ss.at[j * nk + k], rs.at[j * nk + k], (cx, cy, c)).wait_recv()
            for cp in sends:
                cp.wait_send()

    return CommPlan(parts, [jax.ShapeDtypeStruct(p.shape, p.dtype) for p in parts], 3 * nk, start, finish)


def share_plan(finals):
    flat = [(l, f) for l, fs in enumerate(finals) for f in fs]

    def copies(f_refs, o_refs, ss, rs):
        x, y, c, _ = _place()
        return c, [_remote(f_refs[i], o_refs[i], ss.at[i], rs.at[i], (x, y, 1 - c)) for i in range(len(flat))]

    def start(f_refs, o_refs, ss, rs):
        c, cps = copies(f_refs, o_refs, ss, rs)
        for l in range(len(finals)):
            @pl.when(c == l)
            def _(l=l):
                for cp, (lf, _) in zip(cps, flat):
                    if lf == l:
                        cp.start()

    def finish(f_refs, o_refs, ss, rs):
        c, cps = copies(f_refs, o_refs, ss, rs)
        for l in range(len(finals)):
            @pl.when(c == l)
            def _(l=l):
                for cp, (lf, _) in zip(cps, flat):
                    if lf == l:
                        cp.wait_send()

            @pl.when(c != l)
            def _(l=l):
                for cp, (lf, _) in zip(cps, flat):
                    if lf == l:
                        cp.wait_recv()

    arrays = [f for _, f in flat]
    return CommPlan(arrays, [jax.ShapeDtypeStruct(f.shape, f.dtype) for f in arrays], len(flat), start, finish,
                    aliases={i: i for i in range(len(flat))})


def add_pair(name, own, other, tr=256):
    n, cols = own.shape
    tr = _tile(n, tr, 16)

    def body(g_ref, o_ref, out_ref):
        out_ref[...] = (g_ref[...].astype(F32) + o_ref[...].astype(F32)).astype(out_ref.dtype)

    row = pl.BlockSpec((tr, cols), lambda i: (i, 0))
    return pl.pallas_call(
        body, name=name, out_shape=jax.ShapeDtypeStruct((n, cols), BF16), grid=(n // tr,), in_specs=[row, row], out_specs=row,
        compiler_params=_params(("parallel",)),
    )(own, other)


def sum_slots(name, slots, part, shard, tr=256):
    ns, r, cols = slots.shape
    tr = _tile(r, tr, 16)

    def body(s_idx, s_ref, own_ref, o_ref):
        total = own_ref[...].astype(F32)
        for k in range(1, ns):
            total = total + s_ref[(s_idx[0] + k) % ns].astype(F32)
        o_ref[...] = total

    return pl.pallas_call(
        body, name=name, out_shape=jax.ShapeDtypeStruct((r, cols), F32),
        grid_spec=pltpu.PrefetchScalarGridSpec(
            num_scalar_prefetch=1, grid=(r // tr,),
            in_specs=[pl.BlockSpec((ns, tr, cols), lambda i, p: (0, i, 0)), pl.BlockSpec((None, tr, cols), lambda i, p: (p[0], i, 0))],
            out_specs=pl.BlockSpec((tr, cols), lambda i, p: (i, 0))),
        compiler_params=_params(("parallel",)),
    )(shard, slots, part)


def small_allreduce(vec):
    r, cols = vec.shape

    def body(v_ref, o_ref, buf, send_sems, recv_sems):
        x, y, c, _ = _place()
        me = 4 * x + 2 * y + c
        buf[me] = v_ref[...]
        peers = []
        for k in range(1, N_DEV):
            px = (1 - x) if (k & 4) else x
            py = (1 - y) if (k & 2) else y
            pc = (1 - c) if (k & 1) else c
            peers.append((px, py, pc))
        sends = [_remote(buf.at[me], buf.at[me], send_sems.at[k], recv_sems.at[k], peer) for k, peer in enumerate(peers)]
        for cp in sends:
            cp.start()
        for k, (px, py, pc) in enumerate(peers):
            slot = buf.at[4 * px + 2 * py + pc]
            _remote(slot, slot, send_sems.at[k], recv_sems.at[k], (px, py, pc)).wait_recv()
        for cp in sends:
            cp.wait_send()
        total = buf[0]
        for k in range(1, N_DEV):
            total = total + buf[k]
        o_ref[...] = total

    return pl.pallas_call(
        body, name="small_allreduce", out_shape=jax.ShapeDtypeStruct((r, cols), F32),
        in_specs=[pl.BlockSpec(memory_space=pltpu.VMEM)], out_specs=pl.BlockSpec(memory_space=pltpu.VMEM),
        scratch_shapes=[pltpu.VMEM((N_DEV, r, cols), F32), pltpu.SemaphoreType.DMA((N_DEV - 1,)), pltpu.SemaphoreType.DMA((N_DEV - 1,))],
    )(vec)


ARG_NAMES = (("x", "positions") + WEIGHT_NAMES + ("loss_target",) + tuple("m_" + n for n in WEIGHT_NAMES)
             + tuple("v_" + n for n in WEIGHT_NAMES))

_C_KA = N_HEADS_A * HEAD_PAD
_C_VA = _C_KA + N_KV_HEADS_A * HEAD_PAD
_C_CQ = _C_VA + N_KV_HEADS_A * HEAD_PAD
_C_CKV = _C_CQ + Q_LORA_RANK
_C_KR = _C_CKV + KV_LORA_RANK
_C_GA = _C_KR + HEAD_PAD
_R_KA = WIDTH_A
_R_VA = _R_KA + KV_WIDTH_A
_R_CQ = _R_VA + KV_WIDTH_A
_R_KR = _R_CQ + Q_LORA_RANK + KV_LORA_RANK
_R_GA = _R_KR + QK_ROPE_DIM


def _pad_head_cols(w, n_heads):
    r, dh = w.shape[0], w.shape[1] // n_heads
    return jnp.pad(w.reshape(r, n_heads, dh), ((0, 0), (0, 0), (0, HEAD_PAD - dh))).reshape(r, n_heads * HEAD_PAD)


def _unpad_head_cols(w, n_heads, dh):
    r = w.shape[0]
    return w.reshape(r, n_heads, HEAD_PAD)[:, :, :dh].reshape(r, n_heads * dh)


def _pad_head_rows(w, n_heads):
    dh, c = w.shape[0] // n_heads, w.shape[1]
    return jnp.pad(w.reshape(n_heads, dh, c), ((0, 0), (0, HEAD_PAD - dh), (0, 0))).reshape(n_heads * HEAD_PAD, c)


def _unpad_head_rows(w, n_heads, dh):
    c = w.shape[1]
    return w.reshape(n_heads, HEAD_PAD, c)[:, :dh].reshape(n_heads * dh, c)


def _join_cols(shards):
    ns, r, c = shards.shape
    return shards.transpose(1, 0, 2).reshape(r, ns * c)


def _split_cols(mat):
    r, c4 = mat.shape
    return mat.reshape(r, N_SHARDS, c4 // N_SHARDS).transpose(1, 0, 2)


def _ffn_fwd(tag, x, gain, wts, f):
    nb = rms_fwd(tag + "_rms", x, gain)
    gu, h = ffn_up_act(tag + "_up", nb, wts["col"], 2 * f)
    out = gmm_down(tag + "_down", h, wts["down"], f, x, 0.5)
    return out, (x, nb, gu, h)


def _ffn_bwd(tag, saved, gain, wts, gbuf, f, dout, doutb):
    x, nb, gu, h = saved
    n = gu.shape[-1]
    dgu = ffn_down_dact(tag + "_dh", doutb, wts["down"], gu, n, f, 0.5)
    gbuf["down"] = gmm_down_dw(tag + "_dwd", h, doutb, gbuf["down"], wts["down"].shape, f, 0.5)
    gbuf["col"] = gmm_up_dw(tag + "_dwgu", nb, dgu, gbuf["col"], wts["col"].shape, 2 * f)
    dn = gmm_up_dx(tag + "_dn", dgu, wts["col"], x.shape[1], 2 * f)
    dx, dxb, dgain = rms_bwd(tag + "_drms", x, gain, dn, dres=dout)
    return dx, dxb, dgain


def _pad_gain(g):
    return jnp.pad(g, ((0, 0), (0, HEAD_PAD - g.shape[1])))


def _mixer_fwd(tag, x1, sm, w, wts, aux, plan=None):
    b, s = aux["b"], aux["s"]
    t, d = x1.shape
    hb = rms_fwd(tag + "_rms", x1, sm["mix_norm"])
    proj = mm_nn(tag + "_proj", hb, w["win"], tm=512, tn=2048, tk=1024)
    proj3 = proj.reshape(b, s, proj.shape[1])
    gains = {n: _pad_gain(sm[n]) for n in ("swa_q_norm", "swa_k_norm", "mla_q_norm", "mla_k_norm")}
    cqn = rms_fwd(tag + "_rms_cq", proj, sm["mla_q_lora_norm"], tm=1024, col_blk=_C_CQ // Q_LORA_RANK)
    ckvn = rms_fwd(tag + "_rms_ckv", proj, sm["mla_kv_lora_norm"], tm=1024, col_blk=_C_CKV // KV_LORA_RANK)
    qb_raw = mm_nn(tag + "_uq", cqn, w["wuq"], tm=1024).reshape(b, s, N_HEADS_B * HEAD_PAD)
    kb_raw = mm_nn(tag + "_uk", ckvn, w["wk"], tm=1024).reshape(b, s, N_HEADS_B * HEAD_PAD)
    vb = mm_nn(tag + "_uv", ckvn, w["wv"], tm=1024, out_dtype=BF16).reshape(b, s, N_HEADS_B * HEAD_PAD)
    qah = qk_prep_fwd(tag + "_qa_norm", proj3, 0, N_HEADS_A, HEAD_DIM_A, gains["swa_q_norm"])
    kah = qk_prep_fwd(tag + "_ka_norm", proj3, _C_KA // HEAD_PAD, N_KV_HEADS_A, HEAD_DIM_A, gains["swa_k_norm"])
    sinks = sm["swa_sinks"].reshape(-1)
    oab = swa_fwd(tag + "_swa", qah, kah, proj3, _C_VA // HEAD_PAD, aux["pos_col"], aux["pos_row"], aux["slopes"], sinks)
    kr = (proj3, _C_KR // HEAD_PAD)
    qbh = qk_prep_fwd(tag + "_qb_norm", qb_raw, 0, N_HEADS_B, QK_DIM_B, gains["mla_q_norm"], aux["rope"])
    kbh = qk_prep_fwd(tag + "_kb_norm", kb_raw, 0, N_HEADS_B, QK_DIM_B, gains["mla_k_norm"], aux["rope"], extra=kr)
    ob, obb, lse, carried = mla_fwd(tag + "_mla", qbh, kbh, vb, plan)
    oab2, obb2 = oab.reshape(t, -1), obb.reshape(t, -1)
    ya = mm_nn(tag + "_branch_a", oab2, w["wa"], tm=1024)
    yb = mm_nn(tag + "_branch_b", obb2, w["wb"], tm=1024)
    mg = gate_fwd(tag + "_gate", proj, ya, yb, _C_GA, _C_GA + d)
    x2 = gmm_rows(tag + "_out", mg, wts["wo"], x1)
    saved = dict(x1=x1, hb=hb, proj=proj, cqn=cqn, ckvn=ckvn, qb_raw=qb_raw, kb_raw=kb_raw, vb=vb, qah=qah, kah=kah,
                 qbh=qbh, kbh=kbh, ob=ob, lse=lse, oab2=oab2, obb2=obb2, ya=ya, yb=yb, mg=mg, gains=gains)
    return x2, saved, carried


def _mixer_bwd(tag, sv, sm, w, wts, gbuf, aux, dx2, dx2b, plan=None):
    b, s = aux["b"], aux["s"]
    t, d = dx2.shape
    proj, gains = sv["proj"], sv["gains"]
    proj3 = proj.reshape(b, s, proj.shape[1])
    kr = (proj3, _C_KR // HEAD_PAD)
    dmg = gmm_rows_dx(tag + "_d_out", dx2b, wts["wo"])
    gbuf["wo"] = gmm_rows_dw(tag + "_dw_out", sv["mg"], dx2b)
    dya, dyb, dga, dgb = gate_bwd(tag + "_dgate", proj, sv["ya"], sv["yb"], dmg, _C_GA, _C_GA + d)
    dwa = mm_tn(tag + "_dw_branch_a", sv["oab2"], dya, out_dtype=BF16)
    dwb = mm_tn(tag + "_dw_branch_b", sv["obb2"], dyb, out_dtype=BF16)
    doa = mm_nt(tag + "_d_branch_a", dya, w["wa"], tm=1024).reshape(b, s, -1)
    dob = mm_nt(tag + "_d_branch_b", dyb, w["wb"], tm=1024).reshape(b, s, -1)
    sinks = sm["swa_sinks"].reshape(-1)
    dqah, dkah, dva, dsinks = swa_bwd(tag + "_dswa", sv["qah"], sv["kah"], proj3, _C_VA // HEAD_PAD, aux["pos_col"], aux["pos_row"],
                                      aux["slopes"], sinks, doa)
    dqa_raw, dg_swa_q = qk_prep_bwd(tag + "_dqa_norm", proj3, 0, N_HEADS_A, HEAD_DIM_A, gains["swa_q_norm"], dqah)
    dka_raw, dg_swa_k = qk_prep_bwd(tag + "_dka_norm", proj3, _C_KA // HEAD_PAD, N_KV_HEADS_A, HEAD_DIM_A, gains["swa_k_norm"], dkah)
    dqbh, dkbh, dvb, carried = mla_bwd(tag + "_dmla", sv["qbh"], sv["kbh"], sv["vb"], sv["ob"], dob, sv["lse"], plan)
    dqb_raw, dg_mla_q = qk_prep_bwd(tag + "_dqb_norm", sv["qb_raw"], 0, N_HEADS_B, QK_DIM_B, gains["mla_q_norm"], dqbh, aux["rope"])
    dkb_raw, dkr_sum, dg_mla_k = qk_prep_bwd(tag + "_dkb_norm", sv["kb_raw"], 0, N_HEADS_B, QK_DIM_B, gains["mla_k_norm"], dkbh,
                                             aux["rope"], extra=kr, head_sum=True)
    dq_tok, dk_tok, dv_tok = dqb_raw.reshape(t, -1), dkb_raw.reshape(t, -1), dvb.reshape(t, -1)
    dwuq = mm_tn(tag + "_dw_uq", sv["cqn"], dq_tok, tk=1024, out_dtype=BF16)
    dwk = mm_tn(tag + "_dw_uk", sv["ckvn"], dk_tok, tk=1024, out_dtype=BF16)
    dwv = mm_tn(tag + "_dw_uv", sv["ckvn"], dv_tok, tk=1024, out_dtype=BF16)
    dcqn = mm_nt(tag + "_d_uq", dq_tok, w["wuq"], tm=1024)
    dckvn = mm_nt(tag + "_d_uv", dv_tok, w["wv"], tm=1024, res=mm_nt(tag + "_d_uk", dk_tok, w["wk"], tm=1024))
    dcq, dg_q_lora = rms_bwd(tag + "_drms_cq", proj, sm["mla_q_lora_norm"], dcqn, want_f32=False, tm=1024, col_blk=_C_CQ // Q_LORA_RANK)
    dckv, dg_kv_lora = rms_bwd(tag + "_drms_ckv", proj, sm["mla_kv_lora_norm"], dckvn, want_f32=False, tm=1024,
                               col_blk=_C_CKV // KV_LORA_RANK)
    dproj = jnp.concatenate([dqa_raw.reshape(t, -1), dka_raw.reshape(t, -1), dva.reshape(t, -1).astype(BF16), dcq, dckv,
                             dkr_sum.reshape(t, HEAD_PAD).astype(BF16), dga, dgb], axis=1)
    dwin = mm_tn(tag + "_dw_in", sv["hb"], dproj, tm=1024, tn=2048, tk=512, out_dtype=BF16)
    dh = mm_nt(tag + "_d_in", dproj, w["win"], tm=512, tn=1024, tk=2048)
    dx1, dx1b, dg_mix = rms_bwd(tag + "_drms", sv["x1"], sm["mix_norm"], dh, dres=dx2)
    wgrads = dict(win=dwin, wuq=dwuq, wk=dwk, wv=dwv, wa=dwa, wb=dwb)
    sgrads = dict(mix_norm=dg_mix, swa_q_norm=dg_swa_q[:, :HEAD_DIM_A], swa_k_norm=dg_swa_k[:, :HEAD_DIM_A],
                  swa_sinks=dsinks[:, 0].reshape(1, -1), mla_q_lora_norm=dg_q_lora, mla_kv_lora_norm=dg_kv_lora,
                  mla_q_norm=dg_mla_q[:, :QK_DIM_B], mla_k_norm=dg_mla_k[:, :QK_DIM_B])
    return dx1, dx1b, wgrads, sgrads, carried


def _layer_weights(win4, uq4, ukv4, branch4):
    wr = _join_cols(win4)
    kr = jnp.pad(wr[:, _R_KR:_R_GA], ((0, 0), (QK_NOPE_DIM, HEAD_PAD - QK_DIM_B)))
    win = jnp.concatenate([_pad_head_cols(wr[:, :_R_KA], N_HEADS_A), _pad_head_cols(wr[:, _R_KA:_R_VA], N_KV_HEADS_A),
                           _pad_head_cols(wr[:, _R_VA:_R_CQ], N_KV_HEADS_A), wr[:, _R_CQ:_R_KR], kr, wr[:, _R_GA:]], axis=1)
    ukv = _join_cols(ukv4)
    ukv3 = ukv.reshape(ukv.shape[0], N_HEADS_B, QK_NOPE_DIM + V_DIM_B)
    r = branch4.shape[1] // 2
    return dict(win=win, wuq=_pad_head_cols(_join_cols(uq4), N_HEADS_B),
                wk=_pad_head_cols(ukv3[:, :, :QK_NOPE_DIM].reshape(ukv.shape[0], -1), N_HEADS_B),
                wv=_pad_head_cols(ukv3[:, :, QK_NOPE_DIM:].reshape(ukv.shape[0], -1), N_HEADS_B),
                wa=_pad_head_rows(_join_cols(branch4[:, :r]), N_HEADS_A), wb=_pad_head_rows(_join_cols(branch4[:, r:]), N_HEADS_B))


def _mixer_grad_shards(g):
    dw = g["win"]
    win_ref = jnp.concatenate([_unpad_head_cols(dw[:, :_C_KA], N_HEADS_A, HEAD_DIM_A),
                               _unpad_head_cols(dw[:, _C_KA:_C_VA], N_KV_HEADS_A, HEAD_DIM_A),
                               _unpad_head_cols(dw[:, _C_VA:_C_CQ], N_KV_HEADS_A, HEAD_DIM_A), dw[:, _C_CQ:_C_KR],
                               dw[:, _C_KR + QK_NOPE_DIM:_C_KR + QK_DIM_B], dw[:, _C_GA:]], axis=1)
    rk = g["wk"].shape[0]
    ukv = jnp.concatenate([g["wk"].reshape(rk, N_HEADS_B, HEAD_PAD)[:, :, :QK_NOPE_DIM],
                           g["wv"].reshape(rk, N_HEADS_B, HEAD_PAD)[:, :, :V_DIM_B]], axis=2).reshape(rk, -1)
    return (_split_cols(win_ref), _split_cols(_unpad_head_cols(g["wuq"], N_HEADS_B, QK_DIM_B)), _split_cols(ukv),
            jnp.concatenate([_split_cols(_unpad_head_rows(g["wa"], N_HEADS_A, HEAD_DIM_A)),
                             _split_cols(_unpad_head_rows(g["wb"], N_HEADS_B, V_DIM_B))], axis=1))


BUFFER_TAGS = ("col", "down", "wo", "win", "uq", "ukv", "branch")


def _layer_buffers(gathered):
    col, down, wo, win4, uq4, ukv4, branch4 = gathered
    return dict(col=col, down=down, wo=wo), _layer_weights(win4, uq4, ukv4, branch4)


def _train_step(x, positions, target, packs, small, shard):
    depth = 2
    b, s, d = x.shape
    t = b * s
    posf = positions.astype(F32)
    half = QK_ROPE_DIM // 2
    inv_freq = ROPE_BASE ** (-jnp.arange(half, dtype=F32) / half)
    ang = posf[..., None] * inv_freq
    cos, sin = jnp.cos(ang), jnp.sin(ang)
    tail = HEAD_PAD - QK_DIM_B
    rope = (jnp.concatenate([jnp.ones((b, s, QK_NOPE_DIM), F32), cos, cos, jnp.ones((b, s, tail), F32)], axis=-1),
            jnp.concatenate([jnp.zeros((b, s, QK_NOPE_DIM), F32), sin, sin, jnp.zeros((b, s, tail), F32)], axis=-1))
    slopes = jnp.exp2(-8.0 * (jnp.arange(N_HEADS_A, dtype=F32) + 1.0) / N_HEADS_A)
    aux = dict(b=b, s=s, pos_col=posf.reshape(b, s, 1), pos_row=posf.reshape(b, 1, s), rope=rope, slopes=slopes)

    def sm_of(l):
        return {n: small[n][l:l + 1] for n in SMALL_NAMES}

    wts, weights, saved, sgrads, parts, slots = [None] * depth, [None] * depth, [None] * depth, [None] * depth, [None] * depth, [None] * depth

    wts[0], weights[0] = _layer_buffers(comm_call("gather_l0", gather_plan(packs, 0)))
    h = x.reshape(t, d)
    for l in range(depth):
        sm = sm_of(l)
        h, s1 = _ffn_fwd(f"l{l}_ffn1", h, sm["ffn1_norm"], wts[l], 0)
        h, s2, carried = _mixer_fwd(f"l{l}_mix", h, sm, weights[l], wts[l], aux, gather_plan(packs, 1) if l == 0 else None)
        if l == 0:
            wts[1], weights[1] = _layer_buffers(carried)
        h, s3 = _ffn_fwd(f"l{l}_ffn2", h, sm["ffn2_norm"], wts[l], 1)
        saved[l] = (s1, s2, s3)
    loss, dh, dhb = loss_fwd_bwd("loss", h, target.reshape(t, d))

    for l in reversed(range(depth)):
        sm = sm_of(l)
        s1, s2, s3 = saved[l]
        gbuf = dict(col=None, down=None, wo=None)
        dh, dhb, dg_f2 = _ffn_bwd(f"l{l}_ffn2", s3, sm["ffn2_norm"], wts[l], gbuf, 1, dh, dhb)
        plan = scatter_plan(parts[1], 1) if l == 0 else None
        dh, dhb, wg, sg, carried = _mixer_bwd(f"l{l}_mix", s2, sm, weights[l], wts[l], gbuf, aux, dh, dhb, plan)
        if l == 0:
            slots[1] = carried
        dh, dhb, dg_f1 = _ffn_bwd(f"l{l}_ffn1", s1, sm["ffn1_norm"], wts[l], gbuf, 0, dh, dhb)
        sg.update(ffn1_norm=dg_f1, ffn2_norm=dg_f2)
        sgrads[l] = sg
        mine = [gbuf["col"], gbuf["down"], gbuf["wo"]] + list(_mixer_grad_shards(wg))
        theirs = comm_call(f"swap_l{l}", swap_plan(mine, l))
        parts[l] = [add_pair(f"l{l}_add_{tag}", g.reshape(-1, g.shape[-1]), o.reshape(-1, o.shape[-1])).reshape(g.shape)
                    for tag, g, o in zip(BUFFER_TAGS, mine, theirs)]
    slots[0] = comm_call("scatter_l0", scatter_plan(parts[0], 0))
    finals = [[sum_slots(f"l{l}_sum_{tag}", sl, p, shard) for tag, sl, p in zip(BUFFER_TAGS, slots[l], parts[l])] for l in range(depth)]
    shared = comm_call("share", share_plan(finals))
    nk = len(BUFFER_TAGS)
    return loss, dh.reshape(b, s, d), [shared[l * nk:(l + 1) * nk] for l in range(depth)], sgrads


def kernel(x, positions, ffn1_norm, ffn1_w_gate, ffn1_w_up, ffn1_w_down, mix_norm, w_in, swa_q_norm, swa_k_norm, swa_sinks, mla_q_lora_norm, mla_w_uq, mla_kv_lora_norm, mla_w_ukv, mla_q_norm, mla_k_norm, w_branch_a, w_branch_b, w_out, ffn2_norm, ffn2_w_gate, ffn2_w_up, ffn2_w_down, loss_target, m_ffn1_norm, m_ffn1_w_gate, m_ffn1_w_up, m_ffn1_w_down, m_mix_norm, m_w_in, m_swa_q_norm, m_swa_k_norm, m_swa_sinks, m_mla_q_lora_norm, m_mla_w_uq, m_mla_kv_lora_norm, m_mla_w_ukv, m_mla_q_norm, m_mla_k_norm, m_w_branch_a, m_w_branch_b, m_w_out, m_ffn2_norm, m_ffn2_w_gate, m_ffn2_w_up, m_ffn2_w_down, v_ffn1_norm, v_ffn1_w_gate, v_ffn1_w_up, v_ffn1_w_down, v_mix_norm, v_w_in, v_swa_q_norm, v_swa_k_norm, v_swa_sinks, v_mla_q_lora_norm, v_mla_w_uq, v_mla_kv_lora_norm, v_mla_w_ukv, v_mla_q_norm, v_mla_k_norm, v_w_branch_a, v_w_branch_b, v_w_out, v_ffn2_norm, v_ffn2_w_gate, v_ffn2_w_up, v_ffn2_w_down):
    args = (x, positions, ffn1_norm, ffn1_w_gate, ffn1_w_up, ffn1_w_down, mix_norm, w_in, swa_q_norm, swa_k_norm, swa_sinks, mla_q_lora_norm, mla_w_uq, mla_kv_lora_norm, mla_w_ukv, mla_q_norm, mla_k_norm, w_branch_a, w_branch_b, w_out, ffn2_norm, ffn2_w_gate, ffn2_w_up, ffn2_w_down, loss_target, m_ffn1_norm, m_ffn1_w_gate, m_ffn1_w_up, m_ffn1_w_down, m_mix_norm, m_w_in, m_swa_q_norm, m_swa_k_norm, m_swa_sinks, m_mla_q_lora_norm, m_mla_w_uq, m_mla_kv_lora_norm, m_mla_w_ukv, m_mla_q_norm, m_mla_k_norm, m_w_branch_a, m_w_branch_b, m_w_out, m_ffn2_norm, m_ffn2_w_gate, m_ffn2_w_up, m_ffn2_w_down, v_ffn1_norm, v_ffn1_w_gate, v_ffn1_w_up, v_ffn1_w_down, v_mix_norm, v_w_in, v_swa_q_norm, v_swa_k_norm, v_swa_sinks, v_mla_q_lora_norm, v_mla_w_uq, v_mla_kv_lora_norm, v_mla_w_ukv, v_mla_q_norm, v_mla_k_norm, v_w_branch_a, v_w_branch_b, v_w_out, v_ffn2_norm, v_ffn2_w_gate, v_ffn2_w_up, v_ffn2_w_down)
    a = dict(zip(ARG_NAMES, args, strict=True))
    x = a["x"]
    depth = a["ffn1_norm"].shape[0]
    d = x.shape[-1]
    assert depth == 2, "one layer slab per core of a chip"

    packs = [jnp.concatenate([a[n] for n in COL_NAMES], axis=1).astype(BF16),
             jnp.concatenate([a[n] for n in DOWN_NAMES], axis=1).astype(BF16),
             a["w_out"].astype(BF16), a["w_in"].astype(BF16), a["mla_w_uq"].astype(BF16), a["mla_w_ukv"].astype(BF16),
             jnp.concatenate([a["w_branch_a"], a["w_branch_b"]], axis=1).astype(BF16)]
    shard_me = (2 * lax.axis_index("x") + lax.axis_index("y")).astype(jnp.int32).reshape(1)
    small = {n: a[n] for n in SMALL_NAMES}

    loss, grad_x, summed, sgrads = _train_step(x, a["positions"], a["loss_target"], packs, small, shard_me)

    fcol, fdown, fwo, fwin, fuq, fukv, fbranch = [jnp.stack([summed[l][k] for l in range(depth)]) for k in range(len(BUFFER_TAGS))]
    n_ff, r_branch = a["ffn1_w_down"].shape[1], a["w_branch_a"].shape[1]
    grads = {n: fcol[:, i * d:(i + 1) * d] for i, n in enumerate(COL_NAMES)}
    grads.update({n: fdown[:, i * n_ff:(i + 1) * n_ff] for i, n in enumerate(DOWN_NAMES)})
    grads.update(w_out=fwo, w_in=fwin, mla_w_uq=fuq, mla_w_ukv=fukv, w_branch_a=fbranch[:, :r_branch], w_branch_b=fbranch[:, r_branch:])

    flat = jnp.concatenate([jnp.concatenate([sgrads[l][n].reshape(-1) for l in range(depth)]) for n in SMALL_NAMES] + [loss.reshape(-1)])
    n_small = flat.shape[0]
    rows = -(-n_small // (8 * LANES)) * 8
    pad = rows * LANES - n_small

    def small_pack(v):
        return jnp.pad(v, (0, pad)).reshape(rows, LANES)

    total = small_allreduce(small_pack(flat))
    w_s, m_s, v_s = (small_pack(jnp.concatenate([a[p + n].reshape(-1) for n in SMALL_NAMES] + [jnp.zeros((1,), F32)]))
                     for p in ("", "m_", "v_"))
    d_s, nm_s, nv_s = adamw("adamw_small", w_s, total, m_s, v_s)

    def small_unpack(buf):
        out, off, flat_b = {}, 0, buf.reshape(-1)
        for n in SMALL_NAMES:
            size = a[n].shape[0] * a[n].shape[1]
            out[n] = flat_b[off:off + size].reshape(a[n].shape)
            off += size
        return out

    grads.update(small_unpack(total))
    delta, new_m, new_v = small_unpack(d_s), small_unpack(nm_s), small_unpack(nv_s)
    for n in PACK_NAMES:
        shp = a[n].shape
        two_d = (shp[0] * shp[1], shp[2])
        dn, mn, vn = adamw("adamw_" + n, a[n].reshape(two_d), grads[n].reshape(two_d), a["m_" + n].reshape(two_d), a["v_" + n].reshape(two_d))
        delta[n], new_m[n], new_v[n] = dn.reshape(shp), mn.reshape(shp), vn.reshape(shp)

    loss_out = total.reshape(-1)[n_small - 1]
    return (loss_out, grad_x, *[grads[n] for n in WEIGHT_NAMES], *[delta[n] for n in WEIGHT_NAMES],
            *[new_m[n] for n in WEIGHT_NAMES], *[new_v[n] for n in WEIGHT_NAMES])
```

```python
import functools

import numpy as np
import jax
import jax.numpy as jnp
from jax import lax
from jax.experimental import pallas as pl
from jax.experimental.pallas import tpu as pltpu

F32 = jnp.float32
BF16 = jnp.bfloat16
MESH = pl.DeviceIdType.MESH

HEAD_DIM_A = 64
N_HEADS_A = 8
N_KV_HEADS_A = 2
GROUP_A = N_HEADS_A // N_KV_HEADS_A
BLOCK = 128
N_HEADS_B = 8
Q_LORA_RANK = 256
KV_LORA_RANK = 128
QK_NOPE_DIM = 64
QK_ROPE_DIM = 32
QK_DIM_B = QK_NOPE_DIM + QK_ROPE_DIM
V_DIM_B = 64
ROPE_BASE = 10000.0
WIDTH_A = N_HEADS_A * HEAD_DIM_A
WIDTH_B = N_HEADS_B * V_DIM_B
KV_WIDTH_A = N_KV_HEADS_A * HEAD_DIM_A
EPS = 1e-6
NEG = -1e30
ADAM_LR = 0.001
ADAM_B1 = 0.9
ADAM_B2 = 0.999
ADAM_EPS = 1e-08
ADAM_WD = 0.01
ADAM_STEP = 10

N_SHARDS = 4
N_DEV = 8
LANES = 128
VMEM_LIMIT = 48 * 1024 * 1024

PACK_NAMES = ("ffn1_w_gate", "ffn1_w_up", "ffn1_w_down", "w_in", "mla_w_uq", "mla_w_ukv",
              "w_branch_a", "w_branch_b", "w_out", "ffn2_w_gate", "ffn2_w_up", "ffn2_w_down")
COL_NAMES = ("ffn1_w_gate", "ffn1_w_up", "ffn2_w_gate", "ffn2_w_up")
DOWN_NAMES = ("ffn1_w_down", "ffn2_w_down")
SMALL_NAMES = ("ffn1_norm", "mix_norm", "swa_q_norm", "swa_k_norm", "swa_sinks", "mla_q_lora_norm",
               "mla_kv_lora_norm", "mla_q_norm", "mla_k_norm", "ffn2_norm")
WEIGHT_NAMES = ("ffn1_norm", "ffn1_w_gate", "ffn1_w_up", "ffn1_w_down", "mix_norm", "w_in", "swa_q_norm",
                "swa_k_norm", "swa_sinks", "mla_q_lora_norm", "mla_w_uq", "mla_kv_lora_norm", "mla_w_ukv",
                "mla_q_norm", "mla_k_norm", "w_branch_a", "w_branch_b", "w_out", "ffn2_norm", "ffn2_w_gate",
                "ffn2_w_up", "ffn2_w_down")


def _params(sem):
    return pltpu.CompilerParams(dimension_semantics=sem, vmem_limit_bytes=VMEM_LIMIT)


def _tile(n, want, align):
    if n <= want:
        return n
    t = (want // align) * align
    while t > align and n % t:
        t -= align
    assert t >= align and n % t == 0, (n, want, align)
    return t


def _mm_call(name, a, b, out_struct, grid, a_spec, b_spec, o_spec, dims, n_red, acc_shape, alpha=1.0, res=None, into=None,
             inner=0, a_cols=0):
    if n_red and all(g == 1 for g in grid[len(grid) - n_red:]):
        n_red = 0
    n_par = len(grid) - n_red

    def body(*refs):
        a_ref, b_ref = refs[:2]
        r_ref = refs[2] if res is not None else None
        o_ref, acc_ref = refs[-2:]
        if inner:
            def a_of(s):
                return a_ref[:, s * a_cols:(s + 1) * a_cols] if a_cols else a_ref[s]

            part = lax.dot_general(a_of(0), b_ref[0], dims, preferred_element_type=F32)
            for s in range(1, inner):
                part = part + lax.dot_general(a_of(s), b_ref[s], dims, preferred_element_type=F32)
        else:
            part = lax.dot_general(a_ref[...], b_ref[...], dims, preferred_element_type=F32)

        def finish(total):
            if alpha != 1.0:
                total = total * alpha
            if r_ref is not None:
                total = r_ref[...] + total
            o_ref[...] = total.astype(o_ref.dtype)

        if n_red == 0:
            finish(part)
            return
        ids = [pl.program_id(n_par + i) for i in range(n_red)]
        first = functools.reduce(jnp.logical_and, [i == 0 for i in ids])
        last = functools.reduce(jnp.logical_and, [i == grid[n_par + k] - 1 for k, i in enumerate(ids)])

        @pl.when(first)
        def _():
            acc_ref[...] = part

        @pl.when(jnp.logical_not(first))
        def _():
            acc_ref[...] += part

        @pl.when(last)
        def _():
            finish(acc_ref[...])

    in_specs = [a_spec, b_spec] + ([o_spec] if res is not None else [])
    args = (a, b) + ((res,) if res is not None else ())
    aliases = {}
    if into is not None:
        aliases = {len(args): 0}
        in_specs.append(pl.BlockSpec(memory_space=pl.ANY))
        args = args + (into,)
    return pl.pallas_call(
        body, name=name, out_shape=out_struct, grid=grid, in_specs=in_specs, out_specs=o_spec,
        scratch_shapes=[pltpu.VMEM(acc_shape, F32)], input_output_aliases=aliases,
        compiler_params=_params(("parallel",) * n_par + ("arbitrary",) * n_red),
    )(*args)


_NN = (((1,), (0,)), ((), ()))
_NT = (((1,), (1,)), ((), ()))
_TN = (((0,), (0,)), ((), ()))


def mm_nn(name, a, b, tm=512, tn=1024, tk=1024, out_dtype=F32, alpha=1.0, res=None):
    (m, k), (_, n) = a.shape, b.shape
    tm, tn, tk = _tile(m, tm, 16), _tile(n, tn, LANES), _tile(k, tk, LANES)
    return _mm_call(name, a, b, jax.ShapeDtypeStruct((m, n), out_dtype), (m // tm, n // tn, k // tk),
                    pl.BlockSpec((tm, tk), lambda i, j, kk: (i, kk)), pl.BlockSpec((tk, tn), lambda i, j, kk: (kk, j)),
                    pl.BlockSpec((tm, tn), lambda i, j, kk: (i, j)), _NN, 1, (tm, tn), alpha, res)


def mm_nt(name, a, b, tm=512, tn=1024, tk=1024, out_dtype=F32, alpha=1.0, res=None):
    (m, n), (k, _) = a.shape, b.shape
    tm, tn, tk = _tile(m, tm, 16), _tile(k, tn, LANES), _tile(n, tk, LANES)
    return _mm_call(name, a, b, jax.ShapeDtypeStruct((m, k), out_dtype), (m // tm, k // tn, n // tk),
                    pl.BlockSpec((tm, tk), lambda i, j, kk: (i, kk)), pl.BlockSpec((tn, tk), lambda i, j, kk: (j, kk)),
                    pl.BlockSpec((tm, tn), lambda i, j, kk: (i, j)), _NT, 1, (tm, tn), alpha, res)


def mm_tn(name, a, b, tm=1024, tn=1024, tk=1024, out_dtype=F32, alpha=1.0):
    (m, k), (_, n) = a.shape, b.shape
    tm, tn, tk = _tile(k, tm, LANES), _tile(n, tn, LANES), _tile(m, tk, 16)
    return _mm_call(name, a, b, jax.ShapeDtypeStruct((k, n), out_dtype), (k // tm, n // tn, m // tk),
                    pl.BlockSpec((tk, tm), lambda i, j, kk: (kk, i)), pl.BlockSpec((tk, tn), lambda i, j, kk: (kk, j)),
                    pl.BlockSpec((tm, tn), lambda i, j, kk: (i, j)), _TN, 1, (tm, tn), alpha)


def ffn_up_act(name, a, w, blk, tm=512):
    (m, k), (ns, _, n) = a.shape, w.shape
    tm = _tile(m, tm, 16)

    def body(a_ref, wg_ref, wu_ref, gu_ref, h_ref):
        av = a_ref[...]
        gate = jnp.dot(av, wg_ref[...], preferred_element_type=F32)
        up = jnp.dot(av, wu_ref[...], preferred_element_type=F32)
        gu_ref[0] = gate.astype(BF16)
        gu_ref[1] = up.astype(BF16)
        h_ref[...] = (gate * jax.nn.sigmoid(gate) * up).astype(BF16)

    return pl.pallas_call(
        body, name=name, grid=(ns, m // tm),
        out_shape=(jax.ShapeDtypeStruct((2, ns, m, n), BF16), jax.ShapeDtypeStruct((ns, m, n), BF16)),
        in_specs=[pl.BlockSpec((tm, k), lambda s, i: (i, 0)), pl.BlockSpec((None, k, n), lambda s, i: (s, blk, 0)),
                  pl.BlockSpec((None, k, n), lambda s, i: (s, blk + 1, 0))],
        out_specs=(pl.BlockSpec((2, None, tm, n), lambda s, i: (0, s, i, 0)), pl.BlockSpec((None, tm, n), lambda s, i: (s, i, 0))),
        compiler_params=_params(("parallel", "parallel")),
    )(a, w, w)


def ffn_down_dact(name, a, w, gu, n, blk, alpha, tm=512):
    (m, d), ns = a.shape, w.shape[0]
    tm = _tile(m, tm, 16)

    def body(a_ref, w_ref, gu_ref, o_ref):
        dh = lax.dot_general(a_ref[...], w_ref[...], _NT, preferred_element_type=F32) * alpha
        gate, up = gu_ref[0].astype(F32), gu_ref[1].astype(F32)
        s = jax.nn.sigmoid(gate)
        o_ref[0] = (dh * up * (s * (1.0 + gate * (1.0 - s)))).astype(BF16)
        o_ref[1] = (dh * (gate * s)).astype(BF16)

    gu_spec = pl.BlockSpec((2, None, tm, n), lambda s, i: (0, s, i, 0))
    return pl.pallas_call(
        body, name=name, grid=(ns, m // tm), out_shape=jax.ShapeDtypeStruct((2, ns, m, n), BF16),
        in_specs=[pl.BlockSpec((tm, d), lambda s, i: (i, 0)), pl.BlockSpec((None, n, d), lambda s, i: (s, blk, 0)), gu_spec],
        out_specs=gu_spec, compiler_params=_params(("parallel", "parallel")),
    )(a, w, gu)


def gmm_up_dw(name, a, dgu, into, shape, blk, tk=2048):
    (m, k), (_, ns, _, n) = a.shape, dgu.shape
    tk = _tile(m, tk, 16)
    return _mm_call(name, a, dgu, jax.ShapeDtypeStruct(shape, BF16), (2, ns, m // tk),
                    pl.BlockSpec((tk, k), lambda j, s, kk: (kk, 0)), pl.BlockSpec((None, None, tk, n), lambda j, s, kk: (j, s, kk, 0)),
                    pl.BlockSpec((None, k, n), lambda j, s, kk: (s, blk + j, 0)), _TN, 1, (k, n), into=into)


def gmm_up_dx(name, dgu, w, k, blk, tm=512):
    _, ns, m, n = dgu.shape
    tm = _tile(m, tm, 16)
    return _mm_call(name, dgu, w, jax.ShapeDtypeStruct((m, k), F32), (m // tm, 2),
                    pl.BlockSpec((None, ns, tm, n), lambda i, j: (j, 0, i, 0)),
                    pl.BlockSpec((ns, k, n), lambda i, j: (0, blk + j, 0)),
                    pl.BlockSpec((tm, k), lambda i, j: (i, 0)), _NT, 1, (tm, k), inner=ns)


def gmm_down(name, h, w, blk, res, alpha, tm=512):
    (ns, m, n), d = h.shape, w.shape[2]
    tm = _tile(m, tm, 16)
    return _mm_call(name, h, w, jax.ShapeDtypeStruct((m, d), F32), (m // tm,),
                    pl.BlockSpec((ns, tm, n), lambda i: (0, i, 0)), pl.BlockSpec((ns, n, d), lambda i: (0, blk, 0)),
                    pl.BlockSpec((tm, d), lambda i: (i, 0)), _NN, 0, (8, LANES), alpha, res, inner=ns)


def gmm_down_dw(name, h, b, into, shape, blk, alpha, tk=2048):
    (ns, m, n), d = h.shape, b.shape[1]
    tk = _tile(m, tk, 16)
    return _mm_call(name, h, b, jax.ShapeDtypeStruct(shape, BF16), (ns, m // tk),
                    pl.BlockSpec((None, tk, n), lambda s, kk: (s, kk, 0)), pl.BlockSpec((tk, d), lambda s, kk: (kk, 0)),
                    pl.BlockSpec((None, n, d), lambda s, kk: (s, blk, 0)), _TN, 1, (n, d), alpha, into=into)


def gmm_rows(name, a, w, res, tm=512):
    (m, _), (ns, r, d) = a.shape, w.shape
    tm = _tile(m, tm, 16)
    return _mm_call(name, a, w, jax.ShapeDtypeStruct((m, d), F32), (m // tm,),
                    pl.BlockSpec((tm, ns * r), lambda i: (i, 0)), pl.BlockSpec((ns, r, d), lambda i: (0, 0, 0)),
                    pl.BlockSpec((tm, d), lambda i: (i, 0)), _NN, 0, (8, LANES), 1.0, res, inner=ns, a_cols=r)


def gmm_rows_dx(name, a, w, tm=512):
    (m, d), (ns, r, _) = a.shape, w.shape
    tm = _tile(m, tm, 16)
    return _mm_call(name, a, w, jax.ShapeDtypeStruct((m, ns * r), F32), (ns, m // tm),
                    pl.BlockSpec((tm, d), lambda s, i: (i, 0)), pl.BlockSpec((None, r, d), lambda s, i: (s, 0, 0)),
                    pl.BlockSpec((tm, r), lambda s, i: (i, s)), _NT, 0, (8, LANES))


def gmm_rows_dw(name, a, b, tk=2048):
    (m, da), d = a.shape, b.shape[1]
    ns, r = N_SHARDS, da // N_SHARDS
    tk = _tile(m, tk, 16)
    return _mm_call(name, a, b, jax.ShapeDtypeStruct((ns, r, d), BF16), (ns, m // tk),
                    pl.BlockSpec((tk, r), lambda s, kk: (kk, s)), pl.BlockSpec((tk, d), lambda s, kk: (kk, 0)),
                    pl.BlockSpec((None, r, d), lambda s, kk: (s, 0, 0)), _TN, 1, (r, d))


def rms_fwd(name, x, gain, tm=512, col_blk=0):
    m, d = x.shape[0], gain.shape[1]
    tm = _tile(m, tm, 16)

    def body(x_ref, g_ref, o_ref):
        xv = x_ref[...]
        r = lax.rsqrt(jnp.mean(xv * xv, axis=-1, keepdims=True) + EPS)
        o_ref[...] = (xv * r * g_ref[...]).astype(o_ref.dtype)

    return pl.pallas_call(
        body, name=name, out_shape=jax.ShapeDtypeStruct((m, d), BF16), grid=(m // tm,),
        in_specs=[pl.BlockSpec((tm, d), lambda i: (i, col_blk)), pl.BlockSpec((1, d), lambda i: (0, 0))],
        out_specs=pl.BlockSpec((tm, d), lambda i: (i, 0)), compiler_params=_params(("parallel",)),
    )(x, gain)


def rms_bwd(name, x, gain, dn, dres=None, want_f32=True, want_bf16=True, tm=512, col_blk=0):
    m, d = x.shape[0], gain.shape[1]
    tm = _tile(m, tm, 16)
    n_out = int(want_f32) + int(want_bf16)

    def body(*refs):
        x_ref, g_ref, dn_ref = refs[:3]
        pos = 3
        r_ref = None
        if dres is not None:
            r_ref = refs[pos]
            pos += 1
        outs = refs[pos:pos + n_out]
        dg_ref = refs[pos + n_out]
        xv = x_ref[...]
        r = lax.rsqrt(jnp.mean(xv * xv, axis=-1, keepdims=True) + EPS)
        xhat = xv * r
        dnv = dn_ref[...]
        dxhat = dnv * g_ref[...]
        dx = r * (dxhat - xhat * jnp.mean(dxhat * xhat, axis=-1, keepdims=True))
        if r_ref is not None:
            dx = r_ref[...] + dx
        for o in outs:
            o[...] = dx.astype(o.dtype)
        part = jnp.sum(dnv * xhat, axis=0, keepdims=True)

        @pl.when(pl.program_id(0) == 0)
        def _():
            dg_ref[...] = part

        @pl.when(pl.program_id(0) > 0)
        def _():
            dg_ref[...] += part

    row = pl.BlockSpec((tm, d), lambda i: (i, 0))
    vec = pl.BlockSpec((1, d), lambda i: (0, 0))
    out_shape = ([jax.ShapeDtypeStruct((m, d), F32)] if want_f32 else []) + ([jax.ShapeDtypeStruct((m, d), BF16)] if want_bf16 else [])
    res = pl.pallas_call(
        body, name=name, out_shape=tuple(out_shape) + (jax.ShapeDtypeStruct((1, d), F32),), grid=(m // tm,),
        in_specs=[pl.BlockSpec((tm, d), lambda i: (i, col_blk)), vec, row] + ([row] if dres is not None else []),
        out_specs=tuple([row] * n_out) + (vec,), compiler_params=_params(("arbitrary",)),
    )(*((x, gain, dn) + ((dres,) if dres is not None else ())))
    return res


def gate_fwd(name, proj, ya, yb, off_a, off_b, tm=256):
    m, d = ya.shape
    w = proj.shape[1]
    tm = _tile(m, tm, 16)

    def body(p_ref, ya_ref, yb_ref, o_ref):
        ga, gb = p_ref[:, off_a:off_a + d], p_ref[:, off_b:off_b + d]
        o_ref[...] = (jax.nn.sigmoid(ga) * ya_ref[...] + jax.nn.sigmoid(gb) * yb_ref[...]).astype(o_ref.dtype)

    row = pl.BlockSpec((tm, d), lambda i: (i, 0))
    return pl.pallas_call(
        body, name=name, out_shape=jax.ShapeDtypeStruct((m, d), BF16), grid=(m // tm,),
        in_specs=[pl.BlockSpec((tm, w), lambda i: (i, 0)), row, row], out_specs=row, compiler_params=_params(("parallel",)),
    )(proj, ya, yb)


def gate_bwd(name, proj, ya, yb, dmerged, off_a, off_b, tm=256):
    m, d = ya.shape
    w = proj.shape[1]
    tm = _tile(m, tm, 16)

    def body(p_ref, ya_ref, yb_ref, dm_ref, dya_ref, dyb_ref, dga_ref, dgb_ref):
        sa, sb = jax.nn.sigmoid(p_ref[:, off_a:off_a + d]), jax.nn.sigmoid(p_ref[:, off_b:off_b + d])
        dm = dm_ref[...]
        dya_ref[...] = (dm * sa).astype(BF16)
        dyb_ref[...] = (dm * sb).astype(BF16)
        dga_ref[...] = (dm * ya_ref[...] * (sa * (1.0 - sa))).astype(BF16)
        dgb_ref[...] = (dm * yb_ref[...] * (sb * (1.0 - sb))).astype(BF16)

    row = pl.BlockSpec((tm, d), lambda i: (i, 0))
    o = jax.ShapeDtypeStruct((m, d), BF16)
    return pl.pallas_call(
        body, name=name, out_shape=(o, o, o, o), grid=(m // tm,),
        in_specs=[pl.BlockSpec((tm, w), lambda i: (i, 0)), row, row, row], out_specs=(row, row, row, row),
        compiler_params=_params(("parallel",)),
    )(proj, ya, yb, dmerged)


def loss_fwd_bwd(name, y, target, tm=512):
    m, d = y.shape
    tm = _tile(m, tm, 16)

    def body(y_ref, t_ref, l_ref, dy_ref, dyb_ref):
        err = y_ref[...] - t_ref[...]
        dy = err * (1.0 / d)
        dy_ref[...] = dy
        dyb_ref[...] = dy.astype(BF16)
        part = 0.5 * jnp.sum(jnp.mean(err * err, axis=-1, keepdims=True), axis=0, keepdims=True)

        @pl.when(pl.program_id(0) == 0)
        def _():
            l_ref[...] = part

        @pl.when(pl.program_id(0) > 0)
        def _():
            l_ref[...] += part

    row = pl.BlockSpec((tm, d), lambda i: (i, 0))
    return pl.pallas_call(
        body, name=name, grid=(m // tm,),
        out_shape=(jax.ShapeDtypeStruct((1, 1), F32), jax.ShapeDtypeStruct((m, d), F32), jax.ShapeDtypeStruct((m, d), BF16)),
        in_specs=[row, row], out_specs=(pl.BlockSpec((1, 1), lambda i: (0, 0)), row, row),
        compiler_params=_params(("arbitrary",)),
    )(y, target)


def adamw(name, w, g, m, v):
    r, c = w.shape
    tr = _tile(r, max(8, (2 * 1024 * 1024) // (4 * c) // 8 * 8), 8)
    c1 = np.float32(1.0 - ADAM_B1 ** ADAM_STEP)
    c2 = np.float32(1.0 - ADAM_B2 ** ADAM_STEP)

    def body(w_ref, g_ref, m_ref, v_ref, d_ref, nm_ref, nv_ref):
        gv = g_ref[...]
        nm = ADAM_B1 * m_ref[...] + (1.0 - ADAM_B1) * gv
        nv = ADAM_B2 * v_ref[...] + (1.0 - ADAM_B2) * (gv * gv)
        d_ref[...] = -ADAM_LR * ((nm / c1) / (jnp.sqrt(nv / c2) + ADAM_EPS) + ADAM_WD * w_ref[...])
        nm_ref[...] = nm
        nv_ref[...] = nv

    row = pl.BlockSpec((tr, c), lambda i: (i, 0))
    o = jax.ShapeDtypeStruct((r, c), F32)
    return pl.pallas_call(
        body, name=name, out_shape=(o, o, o), grid=(r // tr,), in_specs=[row] * 4, out_specs=(row, row, row),
        compiler_params=_params(("parallel",)),
    )(w, g, m, v)


HEAD_PAD = LANES


def _rope_rot():
    r = np.zeros((HEAD_PAD, HEAD_PAD), np.float32)
    half = QK_ROPE_DIM // 2
    for j in range(half):
        r[QK_NOPE_DIM + half + j, QK_NOPE_DIM + j] = -1.0
        r[QK_NOPE_DIM + j, QK_NOPE_DIM + half + j] = 1.0
    return r


def _head_spec(s, blk0):
    return pl.BlockSpec((None, s, HEAD_PAD), lambda bi, hi: (bi, 0, blk0 + hi))


def qk_prep_fwd(name, x, blk0, n_heads, d_real, gain, rope=None, extra=None):
    b, s, _ = x.shape

    def body(*refs):
        x_ref, g_ref = refs[0], refs[1]
        pos = 2
        xv = x_ref[...]
        if extra is not None:
            xv = xv + refs[pos][...]
            pos += 1
        y = xv * lax.rsqrt(jnp.sum(xv * xv, axis=-1, keepdims=True) * (1.0 / d_real) + EPS) * g_ref[...]
        if rope is not None:
            c_ref, s_ref, r_ref = refs[pos:pos + 3]
            rot = jnp.dot(y, r_ref[...], precision=lax.Precision.HIGHEST, preferred_element_type=F32)
            y = y * c_ref[...] + rot * s_ref[...]
        refs[-1][...] = y.astype(BF16)

    vec = pl.BlockSpec((1, HEAD_PAD), lambda bi, hi: (0, 0))
    tab = pl.BlockSpec((None, s, HEAD_PAD), lambda bi, hi: (bi, 0, 0))
    in_specs, args = [_head_spec(s, blk0), vec], [x, gain]
    if extra is not None:
        e_blk = extra[1]
        in_specs.append(pl.BlockSpec((None, s, HEAD_PAD), lambda bi, hi: (bi, 0, e_blk)))
        args.append(extra[0])
    if rope is not None:
        in_specs += [tab, tab, pl.BlockSpec((HEAD_PAD, HEAD_PAD), lambda bi, hi: (0, 0))]
        args += [rope[0], rope[1], jnp.asarray(_rope_rot())]
    return pl.pallas_call(
        body, name=name, out_shape=jax.ShapeDtypeStruct((b, s, n_heads * HEAD_PAD), BF16), grid=(b, n_heads),
        in_specs=in_specs, out_specs=_head_spec(s, 0), compiler_params=_params(("parallel", "parallel")),
    )(*args)


def qk_prep_bwd(name, x, blk0, n_heads, d_real, gain, dxh, rope=None, extra=None, head_sum=False):
    b, s, _ = x.shape

    def body(*refs):
        x_ref, g_ref, dy_ref = refs[:3]
        pos = 3
        xv = x_ref[...]
        if extra is not None:
            xv = xv + refs[pos][...]
            pos += 1
        dy = dy_ref[...]
        if rope is not None:
            c_ref, s_ref, rt_ref = refs[pos:pos + 3]
            pos += 3
            dy = dy * c_ref[...] + jnp.dot(dy * s_ref[...], rt_ref[...], precision=lax.Precision.HIGHEST, preferred_element_type=F32)
        outs = refs[pos:]
        dx_ref, dg_ref = outs[0], outs[-1]
        r = lax.rsqrt(jnp.sum(xv * xv, axis=-1, keepdims=True) * (1.0 / d_real) + EPS)
        xhat = xv * r
        dxhat = dy * g_ref[...]
        dx = r * (dxhat - xhat * (jnp.sum(dxhat * xhat, axis=-1, keepdims=True) * (1.0 / d_real)))
        dx_ref[...] = dx.astype(dx_ref.dtype)
        part = jnp.sum(dy * xhat, axis=0, keepdims=True)
        first = jnp.logical_and(pl.program_id(0) == 0, pl.program_id(1) == 0)

        @pl.when(first)
        def _():
            dg_ref[...] = part

        @pl.when(jnp.logical_not(first))
        def _():
            dg_ref[...] += part

        if head_sum:
            hs_ref = outs[1]

            @pl.when(pl.program_id(1) == 0)
            def _():
                hs_ref[...] = dx

            @pl.when(pl.program_id(1) > 0)
            def _():
                hs_ref[...] += dx

    vec = pl.BlockSpec((1, HEAD_PAD), lambda bi, hi: (0, 0))
    tab = pl.BlockSpec((None, s, HEAD_PAD), lambda bi, hi: (bi, 0, 0))
    in_specs, args = [_head_spec(s, blk0), vec, _head_spec(s, 0)], [x, gain, dxh]
    if extra is not None:
        e_blk = extra[1]
        in_specs.append(pl.BlockSpec((None, s, HEAD_PAD), lambda bi, hi: (bi, 0, e_blk)))
        args.append(extra[0])
    if rope is not None:
        in_specs += [tab, tab, pl.BlockSpec((HEAD_PAD, HEAD_PAD), lambda bi, hi: (0, 0))]
        args += [rope[0], rope[1], jnp.asarray(_rope_rot().T.copy())]
    out_shape, out_specs = [jax.ShapeDtypeStruct((b, s, n_heads * HEAD_PAD), BF16)], [_head_spec(s, 0)]
    if head_sum:
        out_shape.append(jax.ShapeDtypeStruct((b, s, HEAD_PAD), F32))
        out_specs.append(tab)
    out_shape.append(jax.ShapeDtypeStruct((1, HEAD_PAD), F32))
    out_specs.append(vec)
    return pl.pallas_call(
        body, name=name, out_shape=tuple(out_shape), grid=(b, n_heads), in_specs=in_specs, out_specs=tuple(out_specs),
        compiler_params=_params(("arbitrary", "arbitrary")),
    )(*args)


def _swa_specs(v_blk0):
    q = pl.BlockSpec((None, BLOCK, GROUP_A * HEAD_PAD), lambda b, kv, n: (b, n, kv))
    kprev = pl.BlockSpec((None, BLOCK, HEAD_PAD), lambda b, kv, n: (b, jnp.maximum(n - 1, 0), kv))
    kcur = pl.BlockSpec((None, BLOCK, HEAD_PAD), lambda b, kv, n: (b, n, kv))
    vprev = pl.BlockSpec((None, BLOCK, HEAD_PAD), lambda b, kv, n: (b, jnp.maximum(n - 1, 0), v_blk0 + kv))
    vcur = pl.BlockSpec((None, BLOCK, HEAD_PAD), lambda b, kv, n: (b, n, v_blk0 + kv))
    pcol = pl.BlockSpec((None, BLOCK, 1), lambda b, kv, n: (b, n, 0))
    prow_prev = pl.BlockSpec((None, 1, BLOCK), lambda b, kv, n: (b, 0, jnp.maximum(n - 1, 0)))
    prow_cur = pl.BlockSpec((None, 1, BLOCK), lambda b, kv, n: (b, 0, n))
    smem = pl.BlockSpec(memory_space=pltpu.SMEM)
    return q, kprev, kcur, vprev, vcur, pcol, prow_prev, prow_cur, smem


def _swa_probs(q, kk, dist, valid, slope, sink):
    sc = lax.dot_general(q, kk, _NT, preferred_element_type=F32) * (HEAD_DIM_A ** -0.5)
    sc = sc - slope * dist
    sc = jnp.where(valid, sc, NEG)
    m = jnp.maximum(jnp.max(sc, axis=-1, keepdims=True), sink)
    e = jnp.exp(sc - m)
    es = jnp.exp(sink - m)
    inv = 1.0 / (jnp.sum(e, axis=-1, keepdims=True) + es)
    return e * inv, es * inv


def _swa_window(n, kp_ref, kc_ref, vp_ref, vc_ref, pc_ref, prp_ref, prc_ref):
    kk = jnp.concatenate([kp_ref[...], kc_ref[...]], axis=0)
    vv = jnp.concatenate([vp_ref[...], vc_ref[...]], axis=0).astype(BF16)
    dist = pc_ref[...] - jnp.concatenate([prp_ref[...], prc_ref[...]], axis=1)
    qi = lax.broadcasted_iota(jnp.int32, (BLOCK, 2 * BLOCK), 0) + BLOCK
    ki = lax.broadcasted_iota(jnp.int32, (BLOCK, 2 * BLOCK), 1)
    diff = qi - ki
    valid = (diff >= 0) & (diff < BLOCK) & ((n > 0) | (ki >= BLOCK))
    return kk, vv, dist, valid


def swa_fwd(name, q, k, vsrc, v_blk0, pos_col, pos_row, slopes, sinks):
    b, s, _ = q.shape
    qs, kprev, kcur, vprev, vcur, pcol, prp, prc, smem = _swa_specs(v_blk0)

    def body(q_ref, kp_ref, kc_ref, vp_ref, vc_ref, pc_ref, prp_ref, prc_ref, sl_ref, sk_ref, o_ref):
        kv, n = pl.program_id(1), pl.program_id(2)
        kk, vv, dist, valid = _swa_window(n, kp_ref, kc_ref, vp_ref, vc_ref, pc_ref, prp_ref, prc_ref)
        for g in range(GROUP_A):
            hd = kv * GROUP_A + g
            lanes = slice(g * HEAD_PAD, (g + 1) * HEAD_PAD)
            p, _ = _swa_probs(q_ref[:, lanes], kk, dist, valid, sl_ref[hd], sk_ref[hd])
            o_ref[:, lanes] = jnp.dot(p.astype(BF16), vv, preferred_element_type=F32).astype(BF16)

    return pl.pallas_call(
        body, name=name, out_shape=jax.ShapeDtypeStruct(q.shape, BF16), grid=(b, N_KV_HEADS_A, s // BLOCK),
        in_specs=[qs, kprev, kcur, vprev, vcur, pcol, prp, prc, smem, smem], out_specs=qs,
        compiler_params=_params(("parallel", "parallel", "parallel")),
    )(q, k, k, vsrc, vsrc, pos_col, pos_row, pos_row, slopes, sinks)


def swa_bwd(name, q, k, vsrc, v_blk0, pos_col, pos_row, slopes, sinks, do):
    b, s, _ = q.shape
    qs, kprev, kcur, vprev, vcur, pcol, prp, prc, smem = _swa_specs(v_blk0)

    def body(q_ref, kp_ref, kc_ref, vp_ref, vc_ref, pc_ref, prp_ref, prc_ref, sl_ref, sk_ref, do_ref, dq_ref, dk_ref, dv_ref, ds_ref):
        bi, kv, n = pl.program_id(0), pl.program_id(1), pl.program_id(2)
        kk, vv, dist, valid = _swa_window(n, kp_ref, kc_ref, vp_ref, vc_ref, pc_ref, prp_ref, prc_ref)

        @pl.when((bi == 0) & (kv == 0) & (n == 0))
        def _():
            ds_ref[...] = jnp.zeros_like(ds_ref)

        @pl.when(n == 0)
        def _():
            dk_ref[...] = jnp.zeros_like(dk_ref)
            dv_ref[...] = jnp.zeros_like(dv_ref)

        dkk = jnp.zeros((2 * BLOCK, HEAD_PAD), F32)
        dvv = jnp.zeros((2 * BLOCK, HEAD_PAD), F32)
        head_row = lax.broadcasted_iota(jnp.int32, (N_HEADS_A, LANES), 0)
        dsink = jnp.zeros((N_HEADS_A, LANES), F32)
        for g in range(GROUP_A):
            hd = kv * GROUP_A + g
            lanes = slice(g * HEAD_PAD, (g + 1) * HEAD_PAD)
            qg = q_ref[:, lanes]
            p, ps = _swa_probs(qg, kk, dist, valid, sl_ref[hd], sk_ref[hd])
            dob = do_ref[:, lanes].astype(BF16)
            dvv = dvv + lax.dot_general(p.astype(BF16), dob, _TN, preferred_element_type=F32)
            dp = lax.dot_general(dob, vv, _NT, preferred_element_type=F32)
            rs = jnp.sum(p * dp, axis=-1, keepdims=True)
            dsb = (p * (dp - rs) * (HEAD_DIM_A ** -0.5)).astype(BF16)
            dq_ref[:, lanes] = jnp.dot(dsb, kk, preferred_element_type=F32)
            dkk = dkk + lax.dot_general(dsb, qg, _TN, preferred_element_type=F32)
            dsink = dsink + jnp.where(head_row == hd, -jnp.sum(ps * rs), 0.0)
        ds_ref[...] += dsink

        @pl.when(n > 0)
        def _():
            start = pl.multiple_of((n - 1) * BLOCK, BLOCK)
            dk_ref[pl.ds(start, 2 * BLOCK), :] += dkk
            dv_ref[pl.ds(start, 2 * BLOCK), :] += dvv

        @pl.when(n == 0)
        def _():
            dk_ref[0:BLOCK, :] += dkk[BLOCK:, :]
            dv_ref[0:BLOCK, :] += dvv[BLOCK:, :]

    kv_full = pl.BlockSpec((None, s, HEAD_PAD), lambda bi, kv, n: (bi, 0, kv))
    kv_shape = jax.ShapeDtypeStruct(k.shape, F32)
    return pl.pallas_call(
        body, name=name, grid=(b, N_KV_HEADS_A, s // BLOCK),
        out_shape=(jax.ShapeDtypeStruct(q.shape, F32), kv_shape, kv_shape, jax.ShapeDtypeStruct((N_HEADS_A, LANES), F32)),
        in_specs=[qs, kprev, kcur, vprev, vcur, pcol, prp, prc, smem, smem, qs],
        out_specs=(qs, kv_full, kv_full, pl.BlockSpec((N_HEADS_A, LANES), lambda bi, kv, n: (0, 0))),
        compiler_params=_params(("arbitrary", "arbitrary", "arbitrary")),
    )(q, k, k, vsrc, vsrc, pos_col, pos_row, pos_row, slopes, sinks, do)


MLA_T = 256


def _hosted(plan, n_in, n_out, b, h):
    if plan is None:
        return [], [], [], [], ("parallel", "parallel"), (lambda refs: None), (lambda refs: None)
    ni, no = len(plan.ins), len(plan.out_shapes)

    def split(refs):
        return refs[n_in:n_in + ni], refs[n_in + ni + n_out:n_in + ni + n_out + no], refs[-2], refs[-1]

    def at_start(refs):
        @pl.when((pl.program_id(0) == 0) & (pl.program_id(1) == 0))
        def _():
            plan.start(*split(refs))

    def at_end(refs):
        @pl.when((pl.program_id(0) == b - 1) & (pl.program_id(1) == h - 1))
        def _():
            plan.finish(*split(refs))

    return plan.ins, [_HBM] * ni, plan.out_shapes, [_HBM] * no, ("arbitrary", "arbitrary"), at_start, at_end


def mla_fwd(name, q, k, v, plan=None):
    b, s, w = q.shape
    h = w // HEAD_PAD
    t = MLA_T if s % MLA_T == 0 else BLOCK
    nb = s // t
    scale = QK_DIM_B ** -0.5
    p_ins, p_in_specs, p_outs, p_out_specs, semantics, at_start, at_end = _hosted(plan, 3, 3, b, h)

    def body(*refs):
        q_ref, k_ref, v_ref = refs[:3]
        o_ref, ob_ref, lse_ref = refs[3 + len(p_ins):6 + len(p_ins)]
        at_start(refs)
        causal = lax.broadcasted_iota(jnp.int32, (t, t), 0) <= lax.broadcasted_iota(jnp.int32, (t, t), 1)
        for i in range(nb):
            q_i = q_ref[i * t:(i + 1) * t, :]

            def step(j, carry, q_i=q_i, diagonal=False):
                m, l, acc = carry
                rows = slice(j * t, (j + 1) * t) if diagonal else pl.ds(pl.multiple_of(j * t, t), t)
                st = lax.dot_general(k_ref[rows, :], q_i, _NT, preferred_element_type=F32) * scale
                if diagonal:
                    st = jnp.where(causal, st, NEG)
                m_new = jnp.maximum(m, jnp.max(st, axis=0, keepdims=True))
                a = jnp.exp(m - m_new)
                p = jnp.exp(st - m_new)
                l = a * l + jnp.sum(p, axis=0, keepdims=True)
                acc = a * acc + lax.dot_general(v_ref[rows, :], p.astype(BF16), _TN, preferred_element_type=F32)
                return m_new, l, acc

            carry = (jnp.full((1, t), NEG, F32), jnp.zeros((1, t), F32), jnp.zeros((HEAD_PAD, t), F32))
            if i > 0:
                carry = lax.fori_loop(0, i, step, carry)
            m, l, acc = step(i, carry, diagonal=True)
            o_i = (acc / l).T
            o_ref[i * t:(i + 1) * t, :] = o_i
            ob_ref[i * t:(i + 1) * t, :] = o_i.astype(BF16)
            lse_ref[i] = m + jnp.log(l)
        at_end(refs)

    hs = _head_spec(s, 0)
    ls = pl.BlockSpec((None, None, nb, 1, t), lambda bi, hi: (bi, hi, 0, 0, 0))
    res = pl.pallas_call(
        body, name=name, grid=(b, h),
        out_shape=(jax.ShapeDtypeStruct((b, s, w), F32), jax.ShapeDtypeStruct((b, s, w), BF16),
                   jax.ShapeDtypeStruct((b, h, nb, 1, t), F32)) + tuple(p_outs),
        in_specs=[hs, hs, hs] + p_in_specs, out_specs=(hs, hs, ls) + tuple(p_out_specs),
        scratch_shapes=plan.scratch() if plan is not None else [], compiler_params=_params(semantics),
    )(q, k, v, *p_ins)
    return res[0], res[1], res[2], list(res[3:])


def mla_bwd(name, q, k, v, o, do, lse, plan=None):
    b, s, w = q.shape
    h = w // HEAD_PAD
    _, _, nb, _, t = lse.shape
    scale = QK_DIM_B ** -0.5
    p_ins, p_in_specs, p_outs, p_out_specs, semantics, at_start, at_end = _hosted(plan, 6, 3, b, h)

    def body(*refs):
        q_ref, k_ref, v_ref, o_ref, do_ref, lse_ref = refs[:6]
        dq_ref, dk_ref, dv_ref = refs[6 + len(p_ins):9 + len(p_ins)]
        dv_acc = refs[9 + len(p_ins) + len(p_outs)]
        at_start(refs)
        causal = lax.broadcasted_iota(jnp.int32, (t, t), 0) <= lax.broadcasted_iota(jnp.int32, (t, t), 1)
        dk_ref[...] = jnp.zeros_like(dk_ref)
        dv_acc[...] = jnp.zeros_like(dv_acc)
        for i in range(nb):
            q_i = q_ref[i * t:(i + 1) * t, :]
            do_i = do_ref[i * t:(i + 1) * t, :]
            delta = jnp.sum((o_ref[i * t:(i + 1) * t, :] * do_i).T, axis=0, keepdims=True)
            do_b = do_i.astype(BF16)
            lse_i = lse_ref[i]

            def step(j, dqt, q_i=q_i, do_b=do_b, delta=delta, lse_i=lse_i, diagonal=False):
                rows = slice(j * t, (j + 1) * t) if diagonal else pl.ds(pl.multiple_of(j * t, t), t)
                k_j = k_ref[rows, :]
                st = lax.dot_general(k_j, q_i, _NT, preferred_element_type=F32) * scale
                if diagonal:
                    st = jnp.where(causal, st, NEG)
                pt = jnp.exp(st - lse_i)
                dpt = lax.dot_general(v_ref[rows, :], do_b, _NT, preferred_element_type=F32)
                dst = (pt * (dpt - delta) * scale).astype(BF16)
                dv_acc[rows, :] += jnp.dot(pt.astype(BF16), do_b, preferred_element_type=F32)
                dk_ref[rows, :] += jnp.dot(dst, q_i, preferred_element_type=F32)
                return dqt + lax.dot_general(k_j, dst, _TN, preferred_element_type=F32)

            dqt = jnp.zeros((HEAD_PAD, t), F32)
            if i > 0:
                dqt = lax.fori_loop(0, i, step, dqt)
            dq_ref[i * t:(i + 1) * t, :] = step(i, dqt, diagonal=True).T
        dv_ref[...] = dv_acc[...].astype(BF16)
        at_end(refs)

    hs = _head_spec(s, 0)
    ls = pl.BlockSpec((None, None, nb, 1, t), lambda bi, hi: (bi, hi, 0, 0, 0))
    res = pl.pallas_call(
        body, name=name, grid=(b, h),
        out_shape=(jax.ShapeDtypeStruct((b, s, w), F32), jax.ShapeDtypeStruct((b, s, w), F32),
                   jax.ShapeDtypeStruct((b, s, w), BF16)) + tuple(p_outs),
        in_specs=[hs, hs, hs, hs, hs, ls] + p_in_specs, out_specs=(hs, hs, hs) + tuple(p_out_specs),
        scratch_shapes=[pltpu.VMEM((s, HEAD_PAD), F32)] + (plan.scratch() if plan is not None else []),
        compiler_params=_params(semantics),
    )(q, k, v, o, do, lse, *p_ins)
    return res[0], res[1], res[2], list(res[3:])


_HBM = pl.BlockSpec(memory_space=pltpu.HBM)


def _place():
    x, y, c = lax.axis_index("x"), lax.axis_index("y"), lax.axis_index("c")
    chips = [(1 - x, y), (x, 1 - y), (1 - x, 1 - y)]
    return x, y, c, chips


def _remote(src, dst, send_sem, recv_sem, dev):
    return pltpu.make_async_remote_copy(src_ref=src, dst_ref=dst, send_sem=send_sem, recv_sem=recv_sem,
                                        device_id=dev, device_id_type=MESH)


class CommPlan:
    def __init__(self, ins, out_shapes, n_sems, start, finish, aliases=None):
        self.ins, self.out_shapes, self.n_sems = list(ins), list(out_shapes), n_sems
        self.start, self.finish, self.aliases = start, finish, dict(aliases or {})

    def scratch(self):
        return [pltpu.SemaphoreType.DMA((self.n_sems,)), pltpu.SemaphoreType.DMA((self.n_sems,))]


def comm_call(name, plan):
    ni, no = len(plan.ins), len(plan.out_shapes)

    def body(*refs):
        ins, outs, (send_sems, recv_sems) = refs[:ni], refs[ni:ni + no], refs[ni + no:]
        plan.start(ins, outs, send_sems, recv_sems)
        plan.finish(ins, outs, send_sems, recv_sems)

    return pl.pallas_call(
        body, name=name, out_shape=tuple(plan.out_shapes), in_specs=[_HBM] * ni, out_specs=tuple([_HBM] * no),
        input_output_aliases=plan.aliases, scratch_shapes=plan.scratch(),
    )(*plan.ins)


def gather_plan(packs, l):
    nk = len(packs)

    def parts(p_refs, o_refs, ss, rs):
        x, y, c, chips = _place()
        s_me = 2 * x + y
        sibling = (x, y, 1 - c)
        first = [_remote(p_refs[k].at[l], o_refs[k].at[s_me], ss.at[j * nk + k], rs.at[j * nk + k], (cx, cy, c))
                 for j, (cx, cy) in enumerate(chips) for k in range(nk)]
        own = [_remote(p_refs[k].at[l], o_refs[k].at[s_me], ss.at[6 * nk + k], rs.at[6 * nk + k], sibling) for k in range(nk)]
        return c, chips, sibling, first, own

    def start(p_refs, o_refs, ss, rs):
        c, _, _, first, own = parts(p_refs, o_refs, ss, rs)
        for cp in own:
            cp.start()

        @pl.when(c == l)
        def _():
            for cp in first:
                cp.start()

    def finish(p_refs, o_refs, ss, rs):
        c, chips, sibling, first, own = parts(p_refs, o_refs, ss, rs)

        @pl.when(c == l)
        def _():
            passed = []
            for j, (cx, cy) in enumerate(chips):
                for k in range(nk):
                    blk = o_refs[k].at[2 * cx + cy]
                    _remote(blk, blk, ss.at[j * nk + k], rs.at[j * nk + k], (cx, cy, c)).wait_recv()
                    fwd = _remote(blk, blk, ss.at[(3 + j) * nk + k], rs.at[(3 + j) * nk + k], sibling)
                    fwd.start()
                    passed.append(fwd)
            for cp in first + passed:
                cp.wait_send()

        @pl.when(c != l)
        def _():
            for j, (cx, cy) in enumerate(chips):
                for k in range(nk):
                    blk = o_refs[k].at[2 * cx + cy]
                    _remote(blk, blk, ss.at[(3 + j) * nk + k], rs.at[(3 + j) * nk + k], sibling).wait_recv()

        for cp in own:
            cp.wait()

    outs = [jax.ShapeDtypeStruct((N_SHARDS,) + p.shape[1:], p.dtype) for p in packs]
    return CommPlan(packs, outs, 7 * nk, start, finish)


def swap_plan(grads, l):
    nk = len(grads)

    def copies(g_refs, o_refs, ss, rs):
        x, y, c, _ = _place()
        return c, [_remote(g_refs[k], o_refs[k], ss.at[k], rs.at[k], (x, y, 1 - c)) for k in range(nk)]

    def start(g_refs, o_refs, ss, rs):
        c, cps = copies(g_refs, o_refs, ss, rs)

        @pl.when(c != l)
        def _():
            for cp in cps:
                cp.start()

    def finish(g_refs, o_refs, ss, rs):
        c, cps = copies(g_refs, o_refs, ss, rs)

        @pl.when(c != l)
        def _():
            for cp in cps:
                cp.wait_send()

        @pl.when(c == l)
        def _():
            for cp in cps:
                cp.wait_recv()

    return CommPlan(grads, [jax.ShapeDtypeStruct(g.shape, g.dtype) for g in grads], nk, start, finish)


def scatter_plan(parts, l):
    nk = len(parts)

    def copies(p_refs, o_refs, ss, rs):
        x, y, c, chips = _place()
        s_me = 2 * x + y
        sends = [_remote(p_refs[k].at[2 * cx + cy], o_refs[k].at[s_me], ss.at[j * nk + k], rs.at[j * nk + k], (cx, cy, c))
                 for j, (cx, cy) in enumerate(chips) for k in range(nk)]
        return c, chips, sends

    def start(p_refs, o_refs, ss, rs):
        c, _, sends = copies(p_refs, o_refs, ss, rs)

        @pl.when(c == l)
        def _():
            for cp in sends:
                cp.start()

    def finish(p_refs, o_refs, ss, rs):
        c, chips, sends = copies(p_refs, o_refs, ss, rs)

        @pl.when(c == l)
        def _():
            for j, (cx, cy) in enumerate(chips):
                for k in range(nk):
                    slot = o_refs[k].at[2 * cx + cy]
                    _remote(slot, slot, ss.at[j * nk + k], rs.at[j * nk + k], (cx, cy, c)).wait_recv()
            for cp in sends:
                cp.wait_send()

    return CommPlan(parts, [jax.ShapeDtypeStruct(p.shape, p.dtype) for p in parts], 3 * nk, start, finish)


def share_plan(finals):
    flat = [(l, f) for l, fs in enumerate(finals) for f in fs]

    def copies(f_refs, o_refs, ss, rs):
        x, y, c, _ = _place()
        return c, [_remote(f_refs[i], o_refs[i], ss.at[i], rs.at[i], (x, y, 1 - c)) for i in range(len(flat))]

    def start(f_refs, o_refs, ss, rs):
        c, cps = copies(f_refs, o_refs, ss, rs)
        for l in range(len(finals)):
            @pl.when(c == l)
            def _(l=l):
                for cp, (lf, _) in zip(cps, flat):
                    if lf == l:
                        cp.start()

    def finish(f_refs, o_refs, ss, rs):
        c, cps = copies(f_refs, o_refs, ss, rs)
        for l in range(len(finals)):
            @pl.when(c == l)
            def _(l=l):
                for cp, (lf, _) in zip(cps, flat):
                    if lf == l:
                        cp.wait_send()

            @pl.when(c != l)
            def _(l=l):
                for cp, (lf, _) in zip(cps, flat):
                    if lf == l:
                        cp.wait_recv()

    arrays = [f for _, f in flat]
    return CommPlan(arrays, [jax.ShapeDtypeStruct(f.shape, f.dtype) for f in arrays], len(flat), start, finish,
                    aliases={i: i for i in range(len(flat))})


def add_pair(name, own, other, tr=256):
    n, cols = own.shape
    tr = _tile(n, tr, 16)

    def body(g_ref, o_ref, out_ref):
        out_ref[...] = (g_ref[...].astype(F32) + o_ref[...].astype(F32)).astype(out_ref.dtype)

    row = pl.BlockSpec((tr, cols), lambda i: (i, 0))
    return pl.pallas_call(
        body, name=name, out_shape=jax.ShapeDtypeStruct((n, cols), BF16), grid=(n // tr,), in_specs=[row, row], out_specs=row,
        compiler_params=_params(("parallel",)),
    )(own, other)


def sum_slots(name, slots, part, shard, tr=256):
    ns, r, cols = slots.shape
    tr = _tile(r, tr, 16)

    def body(s_idx, s_ref, own_ref, o_ref):
        total = own_ref[...].astype(F32)
        for k in range(1, ns):
            total = total + s_ref[(s_idx[0] + k) % ns].astype(F32)
        o_ref[...] = total

    return pl.pallas_call(
        body, name=name, out_shape=jax.ShapeDtypeStruct((r, cols), F32),
        grid_spec=pltpu.PrefetchScalarGridSpec(
            num_scalar_prefetch=1, grid=(r // tr,),
            in_specs=[pl.BlockSpec((ns, tr, cols), lambda i, p: (0, i, 0)), pl.BlockSpec((None, tr, cols), lambda i, p: (p[0], i, 0))],
            out_specs=pl.BlockSpec((tr, cols), lambda i, p: (i, 0))),
        compiler_params=_params(("parallel",)),
    )(shard, slots, part)


def small_allreduce(vec):
    r, cols = vec.shape

    def body(v_ref, o_ref, buf, send_sems, recv_sems):
        x, y, c, _ = _place()
        me = 4 * x + 2 * y + c
        buf[me] = v_ref[...]
        peers = []
        for k in range(1, N_DEV):
            px = (1 - x) if (k & 4) else x
            py = (1 - y) if (k & 2) else y
            pc = (1 - c) if (k & 1) else c
            peers.append((px, py, pc))
        sends = [_remote(buf.at[me], buf.at[me], send_sems.at[k], recv_sems.at[k], peer) for k, peer in enumerate(peers)]
        for cp in sends:
            cp.start()
        for k, (px, py, pc) in enumerate(peers):
            slot = buf.at[4 * px + 2 * py + pc]
            _remote(slot, slot, send_sems.at[k], recv_sems.at[k], (px, py, pc)).wait_recv()
        for cp in sends:
            cp.wait_send()
        total = buf[0]
        for k in range(1, N_DEV):
            total = total + buf[k]
        o_ref[...] = total

    return pl.pallas_call(
        body, name="small_allreduce", out_shape=jax.ShapeDtypeStruct((r, cols), F32),
        in_specs=[pl.BlockSpec(memory_space=pltpu.VMEM)], out_specs=pl.BlockSpec(memory_space=pltpu.VMEM),
        scratch_shapes=[pltpu.VMEM((N_DEV, r, cols), F32), pltpu.SemaphoreType.DMA((N_DEV - 1,)), pltpu.SemaphoreType.DMA((N_DEV - 1,))],
    )(vec)


ARG_NAMES = (("x", "positions") + WEIGHT_NAMES + ("loss_target",) + tuple("m_" + n for n in WEIGHT_NAMES)
             + tuple("v_" + n for n in WEIGHT_NAMES))

_C_KA = N_HEADS_A * HEAD_PAD
_C_VA = _C_KA + N_KV_HEADS_A * HEAD_PAD
_C_CQ = _C_VA + N_KV_HEADS_A * HEAD_PAD
_C_CKV = _C_CQ + Q_LORA_RANK
_C_KR = _C_CKV + KV_LORA_RANK
_C_GA = _C_KR + HEAD_PAD
_R_KA = WIDTH_A
_R_VA = _R_KA + KV_WIDTH_A
_R_CQ = _R_VA + KV_WIDTH_A
_R_KR = _R_CQ + Q_LORA_RANK + KV_LORA_RANK
_R_GA = _R_KR + QK_ROPE_DIM


def _pad_head_cols(w, n_heads):
    r, dh = w.shape[0], w.shape[1] // n_heads
    return jnp.pad(w.reshape(r, n_heads, dh), ((0, 0), (0, 0), (0, HEAD_PAD - dh))).reshape(r, n_heads * HEAD_PAD)


def _unpad_head_cols(w, n_heads, dh):
    r = w.shape[0]
    return w.reshape(r, n_heads, HEAD_PAD)[:, :, :dh].reshape(r, n_heads * dh)


def _pad_head_rows(w, n_heads):
    dh, c = w.shape[0] // n_heads, w.shape[1]
    return jnp.pad(w.reshape(n_heads, dh, c), ((0, 0), (0, HEAD_PAD - dh), (0, 0))).reshape(n_heads * HEAD_PAD, c)


def _unpad_head_rows(w, n_heads, dh):
    c = w.shape[1]
    return w.reshape(n_heads, HEAD_PAD, c)[:, :dh].reshape(n_heads * dh, c)


def _join_cols(shards):
    ns, r, c = shards.shape
    return shards.transpose(1, 0, 2).reshape(r, ns * c)


def _split_cols(mat):
    r, c4 = mat.shape
    return mat.reshape(r, N_SHARDS, c4 // N_SHARDS).transpose(1, 0, 2)


def _ffn_fwd(tag, x, gain, wts, f):
    nb = rms_fwd(tag + "_rms", x, gain)
    gu, h = ffn_up_act(tag + "_up", nb, wts["col"], 2 * f)
    out = gmm_down(tag + "_down", h, wts["down"], f, x, 0.5)
    return out, (x, nb, gu, h)


def _ffn_bwd(tag, saved, gain, wts, gbuf, f, dout, doutb):
    x, nb, gu, h = saved
    n = gu.shape[-1]
    dgu = ffn_down_dact(tag + "_dh", doutb, wts["down"], gu, n, f, 0.5)
    gbuf["down"] = gmm_down_dw(tag + "_dwd", h, doutb, gbuf["down"], wts["down"].shape, f, 0.5)
    gbuf["col"] = gmm_up_dw(tag + "_dwgu", nb, dgu, gbuf["col"], wts["col"].shape, 2 * f)
    dn = gmm_up_dx(tag + "_dn", dgu, wts["col"], x.shape[1], 2 * f)
    dx, dxb, dgain = rms_bwd(tag + "_drms", x, gain, dn, dres=dout)
    return dx, dxb, dgain


def _pad_gain(g):
    return jnp.pad(g, ((0, 0), (0, HEAD_PAD - g.shape[1])))


def _mixer_fwd(tag, x1, sm, w, wts, aux, plan=None):
    b, s = aux["b"], aux["s"]
    t, d = x1.shape
    hb = rms_fwd(tag + "_rms", x1, sm["mix_norm"])
    proj = mm_nn(tag + "_proj", hb, w["win"], tm=512, tn=2048, tk=1024)
    proj3 = proj.reshape(b, s, proj.shape[1])
    gains = {n: _pad_gain(sm[n]) for n in ("swa_q_norm", "swa_k_norm", "mla_q_norm", "mla_k_norm")}
    cqn = rms_fwd(tag + "_rms_cq", proj, sm["mla_q_lora_norm"], tm=1024, col_blk=_C_CQ // Q_LORA_RANK)
    ckvn = rms_fwd(tag + "_rms_ckv", proj, sm["mla_kv_lora_norm"], tm=1024, col_blk=_C_CKV // KV_LORA_RANK)
    qb_raw = mm_nn(tag + "_uq", cqn, w["wuq"], tm=1024).reshape(b, s, N_HEADS_B * HEAD_PAD)
    kb_raw = mm_nn(tag + "_uk", ckvn, w["wk"], tm=1024).reshape(b, s, N_HEADS_B * HEAD_PAD)
    vb = mm_nn(tag + "_uv", ckvn, w["wv"], tm=1024, out_dtype=BF16).reshape(b, s, N_HEADS_B * HEAD_PAD)
    qah = qk_prep_fwd(tag + "_qa_norm", proj3, 0, N_HEADS_A, HEAD_DIM_A, gains["swa_q_norm"])
    kah = qk_prep_fwd(tag + "_ka_norm", proj3, _C_KA // HEAD_PAD, N_KV_HEADS_A, HEAD_DIM_A, gains["swa_k_norm"])
    sinks = sm["swa_sinks"].reshape(-1)
    oab = swa_fwd(tag + "_swa", qah, kah, proj3, _C_VA // HEAD_PAD, aux["pos_col"], aux["pos_row"], aux["slopes"], sinks)
    kr = (proj3, _C_KR // HEAD_PAD)
    qbh = qk_prep_fwd(tag + "_qb_norm", qb_raw, 0, N_HEADS_B, QK_DIM_B, gains["mla_q_norm"], aux["rope"])
    kbh = qk_prep_fwd(tag + "_kb_norm", kb_raw, 0, N_HEADS_B, QK_DIM_B, gains["mla_k_norm"], aux["rope"], extra=kr)
    ob, obb, lse, carried = mla_fwd(tag + "_mla", qbh, kbh, vb, plan)
    oab2, obb2 = oab.reshape(t, -1), obb.reshape(t, -1)
    ya = mm_nn(tag + "_branch_a", oab2, w["wa"], tm=1024)
    yb = mm_nn(tag + "_branch_b", obb2, w["wb"], tm=1024)
    mg = gate_fwd(tag + "_gate", proj, ya, yb, _C_GA, _C_GA + d)
    x2 = gmm_rows(tag + "_out", mg, wts["wo"], x1)
    saved = dict(x1=x1, hb=hb, proj=proj, cqn=cqn, ckvn=ckvn, qb_raw=qb_raw, kb_raw=kb_raw, vb=vb, qah=qah, kah=kah,
                 qbh=qbh, kbh=kbh, ob=ob, lse=lse, oab2=oab2, obb2=obb2, ya=ya, yb=yb, mg=mg, gains=gains)
    return x2, saved, carried


def _mixer_bwd(tag, sv, sm, w, wts, gbuf, aux, dx2, dx2b, plan=None):
    b, s = aux["b"], aux["s"]
    t, d = dx2.shape
    proj, gains = sv["proj"], sv["gains"]
    proj3 = proj.reshape(b, s, proj.shape[1])
    kr = (proj3, _C_KR // HEAD_PAD)
    dmg = gmm_rows_dx(tag + "_d_out", dx2b, wts["wo"])
    gbuf["wo"] = gmm_rows_dw(tag + "_dw_out", sv["mg"], dx2b)
    dya, dyb, dga, dgb = gate_bwd(tag + "_dgate", proj, sv["ya"], sv["yb"], dmg, _C_GA, _C_GA + d)
    dwa = mm_tn(tag + "_dw_branch_a", sv["oab2"], dya, out_dtype=BF16)
    dwb = mm_tn(tag + "_dw_branch_b", sv["obb2"], dyb, out_dtype=BF16)
    doa = mm_nt(tag + "_d_branch_a", dya, w["wa"], tm=1024).reshape(b, s, -1)
    dob = mm_nt(tag + "_d_branch_b", dyb, w["wb"], tm=1024).reshape(b, s, -1)
    sinks = sm["swa_sinks"].reshape(-1)
    dqah, dkah, dva, dsinks = swa_bwd(tag + "_dswa", sv["qah"], sv["kah"], proj3, _C_VA // HEAD_PAD, aux["pos_col"], aux["pos_row"],
                                      aux["slopes"], sinks, doa)
    dqa_raw, dg_swa_q = qk_prep_bwd(tag + "_dqa_norm", proj3, 0, N_HEADS_A, HEAD_DIM_A, gains["swa_q_norm"], dqah)
    dka_raw, dg_swa_k = qk_prep_bwd(tag + "_dka_norm", proj3, _C_KA // HEAD_PAD, N_KV_HEADS_A, HEAD_DIM_A, gains["swa_k_norm"], dkah)
    dqbh, dkbh, dvb, carried = mla_bwd(tag + "_dmla", sv["qbh"], sv["kbh"], sv["vb"], sv["ob"], dob, sv["lse"], plan)
    dqb_raw, dg_mla_q = qk_prep_bwd(tag + "_dqb_norm", sv["qb_raw"], 0, N_HEADS_B, QK_DIM_B, gains["mla_q_norm"], dqbh, aux["rope"])
    dkb_raw, dkr_sum, dg_mla_k = qk_prep_bwd(tag + "_dkb_norm", sv["kb_raw"], 0, N_HEADS_B, QK_DIM_B, gains["mla_k_norm"], dkbh,
                                             aux["rope"], extra=kr, head_sum=True)
    dq_tok, dk_tok, dv_tok = dqb_raw.reshape(t, -1), dkb_raw.reshape(t, -1), dvb.reshape(t, -1)
    dwuq = mm_tn(tag + "_dw_uq", sv["cqn"], dq_tok, tk=1024, out_dtype=BF16)
    dwk = mm_tn(tag + "_dw_uk", sv["ckvn"], dk_tok, tk=1024, out_dtype=BF16)
    dwv = mm_tn(tag + "_dw_uv", sv["ckvn"], dv_tok, tk=1024, out_dtype=BF16)
    dcqn = mm_nt(tag + "_d_uq", dq_tok, w["wuq"], tm=1024)
    dckvn = mm_nt(tag + "_d_uv", dv_tok, w["wv"], tm=1024, res=mm_nt(tag + "_d_uk", dk_tok, w["wk"], tm=1024))
    dcq, dg_q_lora = rms_bwd(tag + "_drms_cq", proj, sm["mla_q_lora_norm"], dcqn, want_f32=False, tm=1024, col_blk=_C_CQ // Q_LORA_RANK)
    dckv, dg_kv_lora = rms_bwd(tag + "_drms_ckv", proj, sm["mla_kv_lora_norm"], dckvn, want_f32=False, tm=1024,
                               col_blk=_C_CKV // KV_LORA_RANK)
    dproj = jnp.concatenate([dqa_raw.reshape(t, -1), dka_raw.reshape(t, -1), dva.reshape(t, -1).astype(BF16), dcq, dckv,
                             dkr_sum.reshape(t, HEAD_PAD).astype(BF16), dga, dgb], axis=1)
    dwin = mm_tn(tag + "_dw_in", sv["hb"], dproj, tm=1024, tn=2048, tk=1024, out_dtype=BF16)
    dh = mm_nt(tag + "_d_in", dproj, w["win"], tm=512, tn=1024, tk=2048)
    dx1, dx1b, dg_mix = rms_bwd(tag + "_drms", sv["x1"], sm["mix_norm"], dh, dres=dx2)
    wgrads = dict(win=dwin, wuq=dwuq, wk=dwk, wv=dwv, wa=dwa, wb=dwb)
    sgrads = dict(mix_norm=dg_mix, swa_q_norm=dg_swa_q[:, :HEAD_DIM_A], swa_k_norm=dg_swa_k[:, :HEAD_DIM_A],
                  swa_sinks=dsinks[:, 0].reshape(1, -1), mla_q_lora_norm=dg_q_lora, mla_kv_lora_norm=dg_kv_lora,
                  mla_q_norm=dg_mla_q[:, :QK_DIM_B], mla_k_norm=dg_mla_k[:, :QK_DIM_B])
    return dx1, dx1b, wgrads, sgrads, carried


def _layer_weights(win4, uq4, ukv4, branch4):
    wr = _join_cols(win4)
    kr = jnp.pad(wr[:, _R_KR:_R_GA], ((0, 0), (QK_NOPE_DIM, HEAD_PAD - QK_DIM_B)))
    win = jnp.concatenate([_pad_head_cols(wr[:, :_R_KA], N_HEADS_A), _pad_head_cols(wr[:, _R_KA:_R_VA], N_KV_HEADS_A),
                           _pad_head_cols(wr[:, _R_VA:_R_CQ], N_KV_HEADS_A), wr[:, _R_CQ:_R_KR], kr, wr[:, _R_GA:]], axis=1)
    ukv = _join_cols(ukv4)
    ukv3 = ukv.reshape(ukv.shape[0], N_HEADS_B, QK_NOPE_DIM + V_DIM_B)
    r = branch4.shape[1] // 2
    return dict(win=win, wuq=_pad_head_cols(_join_cols(uq4), N_HEADS_B),
                wk=_pad_head_cols(ukv3[:, :, :QK_NOPE_DIM].reshape(ukv.shape[0], -1), N_HEADS_B),
                wv=_pad_head_cols(ukv3[:, :, QK_NOPE_DIM:].reshape(ukv.shape[0], -1), N_HEADS_B),
                wa=_pad_head_rows(_join_cols(branch4[:, :r]), N_HEADS_A), wb=_pad_head_rows(_join_cols(branch4[:, r:]), N_HEADS_B))


def _mixer_grad_shards(g):
    dw = g["win"]
    win_ref = jnp.concatenate([_unpad_head_cols(dw[:, :_C_KA], N_HEADS_A, HEAD_DIM_A),
                               _unpad_head_cols(dw[:, _C_KA:_C_VA], N_KV_HEADS_A, HEAD_DIM_A),
                               _unpad_head_cols(dw[:, _C_VA:_C_CQ], N_KV_HEADS_A, HEAD_DIM_A), dw[:, _C_CQ:_C_KR],
                               dw[:, _C_KR + QK_NOPE_DIM:_C_KR + QK_DIM_B], dw[:, _C_GA:]], axis=1)
    rk = g["wk"].shape[0]
    ukv = jnp.concatenate([g["wk"].reshape(rk, N_HEADS_B, HEAD_PAD)[:, :, :QK_NOPE_DIM],
                           g["wv"].reshape(rk, N_HEADS_B, HEAD_PAD)[:, :, :V_DIM_B]], axis=2).reshape(rk, -1)
    return (_split_cols(win_ref), _split_cols(_unpad_head_cols(g["wuq"], N_HEADS_B, QK_DIM_B)), _split_cols(ukv),
            jnp.concatenate([_split_cols(_unpad_head_rows(g["wa"], N_HEADS_A, HEAD_DIM_A)),
                             _split_cols(_unpad_head_rows(g["wb"], N_HEADS_B, V_DIM_B))], axis=1))


BUFFER_TAGS = ("col", "down", "wo", "win", "uq", "ukv", "branch")


def _layer_buffers(gathered):
    col, down, wo, win4, uq4, ukv4, branch4 = gathered
    return dict(col=col, down=down, wo=wo), _layer_weights(win4, uq4, ukv4, branch4)


def _train_step(x, positions, target, packs, small, shard):
    depth = 2
    b, s, d = x.shape
    t = b * s
    posf = positions.astype(F32)
    half = QK_ROPE_DIM // 2
    inv_freq = ROPE_BASE ** (-jnp.arange(half, dtype=F32) / half)
    ang = posf[..., None] * inv_freq
    cos, sin = jnp.cos(ang), jnp.sin(ang)
    tail = HEAD_PAD - QK_DIM_B
    rope = (jnp.concatenate([jnp.ones((b, s, QK_NOPE_DIM), F32), cos, cos, jnp.ones((b, s, tail), F32)], axis=-1),
            jnp.concatenate([jnp.zeros((b, s, QK_NOPE_DIM), F32), sin, sin, jnp.zeros((b, s, tail), F32)], axis=-1))
    slopes = jnp.exp2(-8.0 * (jnp.arange(N_HEADS_A, dtype=F32) + 1.0) / N_HEADS_A)
    aux = dict(b=b, s=s, pos_col=posf.reshape(b, s, 1), pos_row=posf.reshape(b, 1, s), rope=rope, slopes=slopes)

    def sm_of(l):
        return {n: small[n][l:l + 1] for n in SMALL_NAMES}

    wts, weights, saved, sgrads, parts, slots = [None] * depth, [None] * depth, [None] * depth, [None] * depth, [None] * depth, [None] * depth

    wts[0], weights[0] = _layer_buffers(comm_call("gather_l0", gather_plan(packs, 0)))
    h = x.reshape(t, d)
    for l in range(depth):
        sm = sm_of(l)
        h, s1 = _ffn_fwd(f"l{l}_ffn1", h, sm["ffn1_norm"], wts[l], 0)
        h, s2, carried = _mixer_fwd(f"l{l}_mix", h, sm, weights[l], wts[l], aux, gather_plan(packs, 1) if l == 0 else None)
        if l == 0:
            wts[1], weights[1] = _layer_buffers(carried)
        h, s3 = _ffn_fwd(f"l{l}_ffn2", h, sm["ffn2_norm"], wts[l], 1)
        saved[l] = (s1, s2, s3)
    loss, dh, dhb = loss_fwd_bwd("loss", h, target.reshape(t, d))

    for l in reversed(range(depth)):
        sm = sm_of(l)
        s1, s2, s3 = saved[l]
        gbuf = dict(col=None, down=None, wo=None)
        dh, dhb, dg_f2 = _ffn_bwd(f"l{l}_ffn2", s3, sm["ffn2_norm"], wts[l], gbuf, 1, dh, dhb)
        plan = scatter_plan(parts[1], 1) if l == 0 else None
        dh, dhb, wg, sg, carried = _mixer_bwd(f"l{l}_mix", s2, sm, weights[l], wts[l], gbuf, aux, dh, dhb, plan)
        if l == 0:
            slots[1] = carried
        dh, dhb, dg_f1 = _ffn_bwd(f"l{l}_ffn1", s1, sm["ffn1_norm"], wts[l], gbuf, 0, dh, dhb)
        sg.update(ffn1_norm=dg_f1, ffn2_norm=dg_f2)
        sgrads[l] = sg
        mine = [gbuf["col"], gbuf["down"], gbuf["wo"]] + list(_mixer_grad_shards(wg))
        theirs = comm_call(f"swap_l{l}", swap_plan(mine, l))
        parts[l] = [add_pair(f"l{l}_add_{tag}", g.reshape(-1, g.shape[-1]), o.reshape(-1, o.shape[-1])).reshape(g.shape)
                    for tag, g, o in zip(BUFFER_TAGS, mine, theirs)]
    slots[0] = comm_call("scatter_l0", scatter_plan(parts[0], 0))
    finals = [[sum_slots(f"l{l}_sum_{tag}", sl, p, shard) for tag, sl, p in zip(BUFFER_TAGS, slots[l], parts[l])] for l in range(depth)]
    shared = comm_call("share", share_plan(finals))
    nk = len(BUFFER_TAGS)
    return loss, dh.reshape(b, s, d), [shared[l * nk:(l + 1) * nk] for l in range(depth)], sgrads


def kernel(x, positions, ffn1_norm, ffn1_w_gate, ffn1_w_up, ffn1_w_down, mix_norm, w_in, swa_q_norm, swa_k_norm, swa_sinks, mla_q_lora_norm, mla_w_uq, mla_kv_lora_norm, mla_w_ukv, mla_q_norm, mla_k_norm, w_branch_a, w_branch_b, w_out, ffn2_norm, ffn2_w_gate, ffn2_w_up, ffn2_w_down, loss_target, m_ffn1_norm, m_ffn1_w_gate, m_ffn1_w_up, m_ffn1_w_down, m_mix_norm, m_w_in, m_swa_q_norm, m_swa_k_norm, m_swa_sinks, m_mla_q_lora_norm, m_mla_w_uq, m_mla_kv_lora_norm, m_mla_w_ukv, m_mla_q_norm, m_mla_k_norm, m_w_branch_a, m_w_branch_b, m_w_out, m_ffn2_norm, m_ffn2_w_gate, m_ffn2_w_up, m_ffn2_w_down, v_ffn1_norm, v_ffn1_w_gate, v_ffn1_w_up, v_ffn1_w_down, v_mix_norm, v_w_in, v_swa_q_norm, v_swa_k_norm, v_swa_sinks, v_mla_q_lora_norm, v_mla_w_uq, v_mla_kv_lora_norm, v_mla_w_ukv, v_mla_q_norm, v_mla_k_norm, v_w_branch_a, v_w_branch_b, v_w_out, v_ffn2_norm, v_ffn2_w_gate, v_ffn2_w_up, v_ffn2_w_down):
    args = (x, positions, ffn1_norm, ffn1_w_gate, ffn1_w_up, ffn1_w_down, mix_norm, w_in, swa_q_norm, swa_k_norm, swa_sinks, mla_q_lora_norm, mla_w_uq, mla_kv_lora_norm, mla_w_ukv, mla_q_norm, mla_k_norm, w_branch_a, w_branch_b, w_out, ffn2_norm, ffn2_w_gate, ffn2_w_up, ffn2_w_down, loss_target, m_ffn1_norm, m_ffn1_w_gate, m_ffn1_w_up, m_ffn1_w_down, m_mix_norm, m_w_in, m_swa_q_norm, m_swa_k_norm, m_swa_sinks, m_mla_q_lora_norm, m_mla_w_uq, m_mla_kv_lora_norm, m_mla_w_ukv, m_mla_q_norm, m_mla_k_norm, m_w_branch_a, m_w_branch_b, m_w_out, m_ffn2_norm, m_ffn2_w_gate, m_ffn2_w_up, m_ffn2_w_down, v_ffn1_norm, v_ffn1_w_gate, v_ffn1_w_up, v_ffn1_w_down, v_mix_norm, v_w_in, v_swa_q_norm, v_swa_k_norm, v_swa_sinks, v_mla_q_lora_norm, v_mla_w_uq, v_mla_kv_lora_norm, v_mla_w_ukv, v_mla_q_norm, v_mla_k_norm, v_w_branch_a, v_w_branch_b, v_w_out, v_ffn2_norm, v_ffn2_w_gate, v_ffn2_w_up, v_ffn2_w_down)
    a = dict(zip(ARG_NAMES, args, strict=True))
    x = a["x"]
    depth = a["ffn1_norm"].shape[0]
    d = x.shape[-1]
    assert depth == 2, "one layer slab per core of a chip"

    packs = [jnp.concatenate([a[n] for n in COL_NAMES], axis=1).astype(BF16),
             jnp.concatenate([a[n] for n in DOWN_NAMES], axis=1).astype(BF16),
             a["w_out"].astype(BF16), a["w_in"].astype(BF16), a["mla_w_uq"].astype(BF16), a["mla_w_ukv"].astype(BF16),
             jnp.concatenate([a["w_branch_a"], a["w_branch_b"]], axis=1).astype(BF16)]
    shard_me = (2 * lax.axis_index("x") + lax.axis_index("y")).astype(jnp.int32).reshape(1)
    small = {n: a[n] for n in SMALL_NAMES}

    loss, grad_x, summed, sgrads = _train_step(x, a["positions"], a["loss_target"], packs, small, shard_me)

    fcol, fdown, fwo, fwin, fuq, fukv, fbranch = [jnp.stack([summed[l][k] for l in range(depth)]) for k in range(len(BUFFER_TAGS))]
    n_ff, r_branch = a["ffn1_w_down"].shape[1], a["w_branch_a"].shape[1]
    grads = {n: fcol[:, i * d:(i + 1) * d] for i, n in enumerate(COL_NAMES)}
    grads.update({n: fdown[:, i * n_ff:(i + 1) * n_ff] for i, n in enumerate(DOWN_NAMES)})
    grads.update(w_out=fwo, w_in=fwin, mla_w_uq=fuq, mla_w_ukv=fukv, w_branch_a=fbranch[:, :r_branch], w_branch_b=fbranch[:, r_branch:])

    flat = jnp.concatenate([jnp.concatenate([sgrads[l][n].reshape(-1) for l in range(depth)]) for n in SMALL_NAMES] + [loss.reshape(-1)])
    n_small = flat.shape[0]
    rows = -(-n_small // (8 * LANES)) * 8
    pad = rows * LANES - n_small

    def small_pack(v):
        return jnp.pad(v, (0, pad)).reshape(rows, LANES)

    total = small_allreduce(small_pack(flat))
    w_s, m_s, v_s = (small_pack(jnp.concatenate([a[p + n].reshape(-1) for n in SMALL_NAMES] + [jnp.zeros((1,), F32)]))
                     for p in ("", "m_", "v_"))
    d_s, nm_s, nv_s = adamw("adamw_small", w_s, total, m_s, v_s)

    def small_unpack(buf):
        out, off, flat_b = {}, 0, buf.reshape(-1)
        for n in SMALL_NAMES:
            size = a[n].shape[0] * a[n].shape[1]
            out[n] = flat_b[off:off + size].reshape(a[n].shape)
            off += size
        return out

    grads.update(small_unpack(total))
    delta, new_m, new_v = small_unpack(d_s), small_unpack(nm_s), small_unpack(nv_s)
    for n in PACK_NAMES:
        shp = a[n].shape
        two_d = (shp[0] * shp[1], shp[2])
        dn, mn, vn = adamw("adamw_" + n, a[n].reshape(two_d), grads[n].reshape(two_d), a["m_" + n].reshape(two_d), a["v_" + n].reshape(two_d))
        delta[n], new_m[n], new_v[n] = dn.reshape(shp), mn.reshape(shp), vn.reshape(shp)

    loss_out = total.reshape(-1)[n_small - 1]
    return (loss_out, grad_x, *[grads[n] for n in WEIGHT_NAMES], *[delta[n] for n in WEIGHT_NAMES],
            *[new_m[n] for n in WEIGHT_NAMES], *[new_v[n] for n in WEIGHT_NAMES])
```

```python
import functools

import numpy as np
import jax
import jax.numpy as jnp
from jax import lax
from jax.experimental import pallas as pl
from jax.experimental.pallas import tpu as pltpu

F32 = jnp.float32
BF16 = jnp.bfloat16
MESH = pl.DeviceIdType.MESH

HEAD_DIM_A = 64
N_HEADS_A = 8
N_KV_HEADS_A = 2
GROUP_A = N_HEADS_A // N_KV_HEADS_A
BLOCK = 128
N_HEADS_B = 8
Q_LORA_RANK = 256
KV_LORA_RANK = 128
QK_NOPE_DIM = 64
QK_ROPE_DIM = 32
QK_DIM_B = QK_NOPE_DIM + QK_ROPE_DIM
V_DIM_B = 64
ROPE_BASE = 10000.0
WIDTH_A = N_HEADS_A * HEAD_DIM_A
WIDTH_B = N_HEADS_B * V_DIM_B
KV_WIDTH_A = N_KV_HEADS_A * HEAD_DIM_A
EPS = 1e-6
NEG = -1e30
ADAM_LR = 0.001
ADAM_B1 = 0.9
ADAM_B2 = 0.999
ADAM_EPS = 1e-08
ADAM_WD = 0.01
ADAM_STEP = 10

N_SHARDS = 4
N_DEV = 8
LANES = 128
VMEM_LIMIT = 48 * 1024 * 1024

PACK_NAMES = ("ffn1_w_gate", "ffn1_w_up", "ffn1_w_down", "w_in", "mla_w_uq", "mla_w_ukv",
              "w_branch_a", "w_branch_b", "w_out", "ffn2_w_gate", "ffn2_w_up", "ffn2_w_down")
COL_NAMES = ("ffn1_w_gate", "ffn1_w_up", "ffn2_w_gate", "ffn2_w_up")
DOWN_NAMES = ("ffn1_w_down", "ffn2_w_down")
SMALL_NAMES = ("ffn1_norm", "mix_norm", "swa_q_norm", "swa_k_norm", "swa_sinks", "mla_q_lora_norm",
               "mla_kv_lora_norm", "mla_q_norm", "mla_k_norm", "ffn2_norm")
WEIGHT_NAMES = ("ffn1_norm", "ffn1_w_gate", "ffn1_w_up", "ffn1_w_down", "mix_norm", "w_in", "swa_q_norm",
                "swa_k_norm", "swa_sinks", "mla_q_lora_norm", "mla_w_uq", "mla_kv_lora_norm", "mla_w_ukv",
                "mla_q_norm", "mla_k_norm", "w_branch_a", "w_branch_b", "w_out", "ffn2_norm", "ffn2_w_gate",
                "ffn2_w_up", "ffn2_w_down")


def _params(sem):
    return pltpu.CompilerParams(dimension_semantics=sem, vmem_limit_bytes=VMEM_LIMIT)


def _tile(n, want, align):
    if n <= want:
        return n
    t = (want // align) * align
    while t > align and n % t:
        t -= align
    assert t >= align and n % t == 0, (n, want, align)
    return t


def _mm_call(name, a, b, out_struct, grid, a_spec, b_spec, o_spec, dims, n_red, acc_shape, alpha=1.0, res=None, into=None,
             inner=0, a_cols=0):
    if n_red and all(g == 1 for g in grid[len(grid) - n_red:]):
        n_red = 0
    n_par = len(grid) - n_red

    def body(*refs):
        a_ref, b_ref = refs[:2]
        r_ref = refs[2] if res is not None else None
        o_ref, acc_ref = refs[-2:]
        if inner:
            def a_of(s):
                return a_ref[:, s * a_cols:(s + 1) * a_cols] if a_cols else a_ref[s]

            part = lax.dot_general(a_of(0), b_ref[0], dims, preferred_element_type=F32)
            for s in range(1, inner):
                part = part + lax.dot_general(a_of(s), b_ref[s], dims, preferred_element_type=F32)
        else:
            part = lax.dot_general(a_ref[...], b_ref[...], dims, preferred_element_type=F32)

        def finish(total):
            if alpha != 1.0:
                total = total * alpha
            if r_ref is not None:
                total = r_ref[...] + total
            o_ref[...] = total.astype(o_ref.dtype)

        if n_red == 0:
            finish(part)
            return
        ids = [pl.program_id(n_par + i) for i in range(n_red)]
        first = functools.reduce(jnp.logical_and, [i == 0 for i in ids])
        last = functools.reduce(jnp.logical_and, [i == grid[n_par + k] - 1 for k, i in enumerate(ids)])

        @pl.when(first)
        def _():
            acc_ref[...] = part

        @pl.when(jnp.logical_not(first))
        def _():
            acc_ref[...] += part

        @pl.when(last)
        def _():
            finish(acc_ref[...])

    in_specs = [a_spec, b_spec] + ([o_spec] if res is not None else [])
    args = (a, b) + ((res,) if res is not None else ())
    aliases = {}
    if into is not None:
        aliases = {len(args): 0}
        in_specs.append(pl.BlockSpec(memory_space=pl.ANY))
        args = args + (into,)
    return pl.pallas_call(
        body, name=name, out_shape=out_struct, grid=grid, in_specs=in_specs, out_specs=o_spec,
        scratch_shapes=[pltpu.VMEM(acc_shape, F32)], input_output_aliases=aliases,
        compiler_params=_params(("parallel",) * n_par + ("arbitrary",) * n_red),
    )(*args)


_NN = (((1,), (0,)), ((), ()))
_NT = (((1,), (1,)), ((), ()))
_TN = (((0,), (0,)), ((), ()))


def mm_nn(name, a, b, tm=512, tn=1024, tk=1024, out_dtype=F32, alpha=1.0, res=None):
    (m, k), (_, n) = a.shape, b.shape
    tm, tn, tk = _tile(m, tm, 16), _tile(n, tn, LANES), _tile(k, tk, LANES)
    return _mm_call(name, a, b, jax.ShapeDtypeStruct((m, n), out_dtype), (m // tm, n // tn, k // tk),
                    pl.BlockSpec((tm, tk), lambda i, j, kk: (i, kk)), pl.BlockSpec((tk, tn), lambda i, j, kk: (kk, j)),
                    pl.BlockSpec((tm, tn), lambda i, j, kk: (i, j)), _NN, 1, (tm, tn), alpha, res)


def mm_nt(name, a, b, tm=512, tn=1024, tk=1024, out_dtype=F32, alpha=1.0, res=None):
    (m, n), (k, _) = a.shape, b.shape
    tm, tn, tk = _tile(m, tm, 16), _tile(k, tn, LANES), _tile(n, tk, LANES)
    return _mm_call(name, a, b, jax.ShapeDtypeStruct((m, k), out_dtype), (m // tm, k // tn, n // tk),
                    pl.BlockSpec((tm, tk), lambda i, j, kk: (i, kk)), pl.BlockSpec((tn, tk), lambda i, j, kk: (j, kk)),
                    pl.BlockSpec((tm, tn), lambda i, j, kk: (i, j)), _NT, 1, (tm, tn), alpha, res)


def mm_tn(name, a, b, tm=1024, tn=1024, tk=1024, out_dtype=F32, alpha=1.0):
    (m, k), (_, n) = a.shape, b.shape
    tm, tn, tk = _tile(k, tm, LANES), _tile(n, tn, LANES), _tile(m, tk, 16)
    return _mm_call(name, a, b, jax.ShapeDtypeStruct((k, n), out_dtype), (k // tm, n // tn, m // tk),
                    pl.BlockSpec((tk, tm), lambda i, j, kk: (kk, i)), pl.BlockSpec((tk, tn), lambda i, j, kk: (kk, j)),
                    pl.BlockSpec((tm, tn), lambda i, j, kk: (i, j)), _TN, 1, (tm, tn), alpha)


def ffn_up_act(name, a, w, blk, tm=512):
    (m, k), (ns, _, n) = a.shape, w.shape
    tm = _tile(m, tm, 16)

    def body(a_ref, wg_ref, wu_ref, gu_ref, h_ref):
        av = a_ref[...]
        gate = jnp.dot(av, wg_ref[...], preferred_element_type=F32)
        up = jnp.dot(av, wu_ref[...], preferred_element_type=F32)
        gu_ref[0] = gate.astype(BF16)
        gu_ref[1] = up.astype(BF16)
        h_ref[...] = (gate * jax.nn.sigmoid(gate) * up).astype(BF16)

    return pl.pallas_call(
        body, name=name, grid=(ns, m // tm),
        out_shape=(jax.ShapeDtypeStruct((2, ns, m, n), BF16), jax.ShapeDtypeStruct((ns, m, n), BF16)),
        in_specs=[pl.BlockSpec((tm, k), lambda s, i: (i, 0)), pl.BlockSpec((None, k, n), lambda s, i: (s, blk, 0)),
                  pl.BlockSpec((None, k, n), lambda s, i: (s, blk + 1, 0))],
        out_specs=(pl.BlockSpec((2, None, tm, n), lambda s, i: (0, s, i, 0)), pl.BlockSpec((None, tm, n), lambda s, i: (s, i, 0))),
        compiler_params=_params(("parallel", "parallel")),
    )(a, w, w)


def ffn_down_dact(name, a, w, gu, n, blk, alpha, tm=512):
    (m, d), ns = a.shape, w.shape[0]
    tm = _tile(m, tm, 16)

    def body(a_ref, w_ref, gu_ref, o_ref):
        dh = lax.dot_general(a_ref[...], w_ref[...], _NT, preferred_element_type=F32) * alpha
        gate, up = gu_ref[0].astype(F32), gu_ref[1].astype(F32)
        s = jax.nn.sigmoid(gate)
        o_ref[0] = (dh * up * (s * (1.0 + gate * (1.0 - s)))).astype(BF16)
        o_ref[1] = (dh * (gate * s)).astype(BF16)

    gu_spec = pl.BlockSpec((2, None, tm, n), lambda s, i: (0, s, i, 0))
    return pl.pallas_call(
        body, name=name, grid=(ns, m // tm), out_shape=jax.ShapeDtypeStruct((2, ns, m, n), BF16),
        in_specs=[pl.BlockSpec((tm, d), lambda s, i: (i, 0)), pl.BlockSpec((None, n, d), lambda s, i: (s, blk, 0)), gu_spec],
        out_specs=gu_spec, compiler_params=_params(("parallel", "parallel")),
    )(a, w, gu)


def gmm_up_dw(name, a, dgu, into, shape, blk, tk=2048):
    (m, k), (_, ns, _, n) = a.shape, dgu.shape
    tk = _tile(m, tk, 16)
    return _mm_call(name, a, dgu, jax.ShapeDtypeStruct(shape, BF16), (2, ns, m // tk),
                    pl.BlockSpec((tk, k), lambda j, s, kk: (kk, 0)), pl.BlockSpec((None, None, tk, n), lambda j, s, kk: (j, s, kk, 0)),
                    pl.BlockSpec((None, k, n), lambda j, s, kk: (s, blk + j, 0)), _TN, 1, (k, n), into=into)


def gmm_up_dx(name, dgu, w, k, blk, tm=512):
    _, ns, m, n = dgu.shape
    tm = _tile(m, tm, 16)
    return _mm_call(name, dgu, w, jax.ShapeDtypeStruct((m, k), F32), (m // tm, 2),
                    pl.BlockSpec((None, ns, tm, n), lambda i, j: (j, 0, i, 0)),
                    pl.BlockSpec((ns, k, n), lambda i, j: (0, blk + j, 0)),
                    pl.BlockSpec((tm, k), lambda i, j: (i, 0)), _NT, 1, (tm, k), inner=ns)


def gmm_down(name, h, w, blk, res, alpha, tm=512):
    (ns, m, n), d = h.shape, w.shape[2]
    tm = _tile(m, tm, 16)
    return _mm_call(name, h, w, jax.ShapeDtypeStruct((m, d), F32), (m // tm,),
                    pl.BlockSpec((ns, tm, n), lambda i: (0, i, 0)), pl.BlockSpec((ns, n, d), lambda i: (0, blk, 0)),
                    pl.BlockSpec((tm, d), lambda i: (i, 0)), _NN, 0, (8, LANES), alpha, res, inner=ns)


def gmm_down_dw(name, h, b, into, shape, blk, alpha, tk=2048):
    (ns, m, n), d = h.shape, b.shape[1]
    tk = _tile(m, tk, 16)
    return _mm_call(name, h, b, jax.ShapeDtypeStruct(shape, BF16), (ns, m // tk),
                    pl.BlockSpec((None, tk, n), lambda s, kk: (s, kk, 0)), pl.BlockSpec((tk, d), lambda s, kk: (kk, 0)),
                    pl.BlockSpec((None, n, d), lambda s, kk: (s, blk, 0)), _TN, 1, (n, d), alpha, into=into)


def gmm_rows(name, a, w, res, tm=512):
    (m, _), (ns, r, d) = a.shape, w.shape
    tm = _tile(m, tm, 16)
    return _mm_call(name, a, w, jax.ShapeDtypeStruct((m, d), F32), (m // tm,),
                    pl.BlockSpec((tm, ns * r), lambda i: (i, 0)), pl.BlockSpec((ns, r, d), lambda i: (0, 0, 0)),
                    pl.BlockSpec((tm, d), lambda i: (i, 0)), _NN, 0, (8, LANES), 1.0, res, inner=ns, a_cols=r)


def gmm_rows_dx(name, a, w, tm=512):
    (m, d), (ns, r, _) = a.shape, w.shape
    tm = _tile(m, tm, 16)
    return _mm_call(name, a, w, jax.ShapeDtypeStruct((m, ns * r), F32), (ns, m // tm),
                    pl.BlockSpec((tm, d), lambda s, i: (i, 0)), pl.BlockSpec((None, r, d), lambda s, i: (s, 0, 0)),
                    pl.BlockSpec((tm, r), lambda s, i: (i, s)), _NT, 0, (8, LANES))


def gmm_rows_dw(name, a, b, tk=2048):
    (m, da), d = a.shape, b.shape[1]
    ns, r = N_SHARDS, da // N_SHARDS
    tk = _tile(m, tk, 16)
    return _mm_call(name, a, b, jax.ShapeDtypeStruct((ns, r, d), BF16), (ns, m // tk),
                    pl.BlockSpec((tk, r), lambda s, kk: (kk, s)), pl.BlockSpec((tk, d), lambda s, kk: (kk, 0)),
                    pl.BlockSpec((None, r, d), lambda s, kk: (s, 0, 0)), _TN, 1, (r, d))


def rms_fwd(name, x, gain, tm=512, col_blk=0):
    m, d = x.shape[0], gain.shape[1]
    tm = _tile(m, tm, 16)

    def body(x_ref, g_ref, o_ref):
        xv = x_ref[...]
        r = lax.rsqrt(jnp.mean(xv * xv, axis=-1, keepdims=True) + EPS)
        o_ref[...] = (xv * r * g_ref[...]).astype(o_ref.dtype)

    return pl.pallas_call(
        body, name=name, out_shape=jax.ShapeDtypeStruct((m, d), BF16), grid=(m // tm,),
        in_specs=[pl.BlockSpec((tm, d), lambda i: (i, col_blk)), pl.BlockSpec((1, d), lambda i: (0, 0))],
        out_specs=pl.BlockSpec((tm, d), lambda i: (i, 0)), compiler_params=_params(("parallel",)),
    )(x, gain)


def rms_bwd(name, x, gain, dn, dres=None, want_f32=True, want_bf16=True, tm=512, col_blk=0):
    m, d = x.shape[0], gain.shape[1]
    tm = _tile(m, tm, 16)
    n_out = int(want_f32) + int(want_bf16)

    def body(*refs):
        x_ref, g_ref, dn_ref = refs[:3]
        pos = 3
        r_ref = None
        if dres is not None:
            r_ref = refs[pos]
            pos += 1
        outs = refs[pos:pos + n_out]
        dg_ref = refs[pos + n_out]
        xv = x_ref[...]
        r = lax.rsqrt(jnp.mean(xv * xv, axis=-1, keepdims=True) + EPS)
        xhat = xv * r
        dnv = dn_ref[...]
        dxhat = dnv * g_ref[...]
        dx = r * (dxhat - xhat * jnp.mean(dxhat * xhat, axis=-1, keepdims=True))
        if r_ref is not None:
            dx = r_ref[...] + dx
        for o in outs:
            o[...] = dx.astype(o.dtype)
        part = jnp.sum(dnv * xhat, axis=0, keepdims=True)

        @pl.when(pl.program_id(0) == 0)
        def _():
            dg_ref[...] = part

        @pl.when(pl.program_id(0) > 0)
        def _():
            dg_ref[...] += part

    row = pl.BlockSpec((tm, d), lambda i: (i, 0))
    vec = pl.BlockSpec((1, d), lambda i: (0, 0))
    out_shape = ([jax.ShapeDtypeStruct((m, d), F32)] if want_f32 else []) + ([jax.ShapeDtypeStruct((m, d), BF16)] if want_bf16 else [])
    res = pl.pallas_call(
        body, name=name, out_shape=tuple(out_shape) + (jax.ShapeDtypeStruct((1, d), F32),), grid=(m // tm,),
        in_specs=[pl.BlockSpec((tm, d), lambda i: (i, col_blk)), vec, row] + ([row] if dres is not None else []),
        out_specs=tuple([row] * n_out) + (vec,), compiler_params=_params(("arbitrary",)),
    )(*((x, gain, dn) + ((dres,) if dres is not None else ())))
    return res


def gate_fwd(name, proj, ya, yb, off_a, off_b, tm=256):
    m, d = ya.shape
    w = proj.shape[1]
    tm = _tile(m, tm, 16)

    def body(p_ref, ya_ref, yb_ref, o_ref):
        ga, gb = p_ref[:, off_a:off_a + d], p_ref[:, off_b:off_b + d]
        o_ref[...] = (jax.nn.sigmoid(ga) * ya_ref[...] + jax.nn.sigmoid(gb) * yb_ref[...]).astype(o_ref.dtype)

    row = pl.BlockSpec((tm, d), lambda i: (i, 0))
    return pl.pallas_call(
        body, name=name, out_shape=jax.ShapeDtypeStruct((m, d), BF16), grid=(m // tm,),
        in_specs=[pl.BlockSpec((tm, w), lambda i: (i, 0)), row, row], out_specs=row, compiler_params=_params(("parallel",)),
    )(proj, ya, yb)


def gate_bwd(name, proj, ya, yb, dmerged, off_a, off_b, tm=256):
    m, d = ya.shape
    w = proj.shape[1]
    tm = _tile(m, tm, 16)

    def body(p_ref, ya_ref, yb_ref, dm_ref, dya_ref, dyb_ref, dga_ref, dgb_ref):
        sa, sb = jax.nn.sigmoid(p_ref[:, off_a:off_a + d]), jax.nn.sigmoid(p_ref[:, off_b:off_b + d])
        dm = dm_ref[...]
        dya_ref[...] = (dm * sa).astype(BF16)
        dyb_ref[...] = (dm * sb).astype(BF16)
        dga_ref[...] = (dm * ya_ref[...] * (sa * (1.0 - sa))).astype(BF16)
        dgb_ref[...] = (dm * yb_ref[...] * (sb * (1.0 - sb))).astype(BF16)

    row = pl.BlockSpec((tm, d), lambda i: (i, 0))
    o = jax.ShapeDtypeStruct((m, d), BF16)
    return pl.pallas_call(
        body, name=name, out_shape=(o, o, o, o), grid=(m // tm,),
        in_specs=[pl.BlockSpec((tm, w), lambda i: (i, 0)), row, row, row], out_specs=(row, row, row, row),
        compiler_params=_params(("parallel",)),
    )(proj, ya, yb, dmerged)


def loss_fwd_bwd(name, y, target, tm=512):
    m, d = y.shape
    tm = _tile(m, tm, 16)

    def body(y_ref, t_ref, l_ref, dy_ref, dyb_ref):
        err = y_ref[...] - t_ref[...]
        dy = err * (1.0 / d)
        dy_ref[...] = dy
        dyb_ref[...] = dy.astype(BF16)
        part = 0.5 * jnp.sum(jnp.mean(err * err, axis=-1, keepdims=True), axis=0, keepdims=True)

        @pl.when(pl.program_id(0) == 0)
        def _():
            l_ref[...] = part

        @pl.when(pl.program_id(0) > 0)
        def _():
            l_ref[...] += part

    row = pl.BlockSpec((tm, d), lambda i: (i, 0))
    return pl.pallas_call(
        body, name=name, grid=(m // tm,),
        out_shape=(jax.ShapeDtypeStruct((1, 1), F32), jax.ShapeDtypeStruct((m, d), F32), jax.ShapeDtypeStruct((m, d), BF16)),
        in_specs=[row, row], out_specs=(pl.BlockSpec((1, 1), lambda i: (0, 0)), row, row),
        compiler_params=_params(("arbitrary",)),
    )(y, target)


def adamw(name, w, g, m, v):
    r, c = w.shape
    tr = _tile(r, max(8, (2 * 1024 * 1024) // (4 * c) // 8 * 8), 8)
    c1 = np.float32(1.0 - ADAM_B1 ** ADAM_STEP)
    c2 = np.float32(1.0 - ADAM_B2 ** ADAM_STEP)

    def body(w_ref, g_ref, m_ref, v_ref, d_ref, nm_ref, nv_ref):
        gv = g_ref[...]
        nm = ADAM_B1 * m_ref[...] + (1.0 - ADAM_B1) * gv
        nv = ADAM_B2 * v_ref[...] + (1.0 - ADAM_B2) * (gv * gv)
        d_ref[...] = -ADAM_LR * ((nm / c1) / (jnp.sqrt(nv / c2) + ADAM_EPS) + ADAM_WD * w_ref[...])
        nm_ref[...] = nm
        nv_ref[...] = nv

    row = pl.BlockSpec((tr, c), lambda i: (i, 0))
    o = jax.ShapeDtypeStruct((r, c), F32)
    return pl.pallas_call(
        body, name=name, out_shape=(o, o, o), grid=(r // tr,), in_specs=[row] * 4, out_specs=(row, row, row),
        compiler_params=_params(("parallel",)),
    )(w, g, m, v)


HEAD_PAD = LANES


def _rope_rot():
    r = np.zeros((HEAD_PAD, HEAD_PAD), np.float32)
    half = QK_ROPE_DIM // 2
    for j in range(half):
        r[QK_NOPE_DIM + half + j, QK_NOPE_DIM + j] = -1.0
        r[QK_NOPE_DIM + j, QK_NOPE_DIM + half + j] = 1.0
    return r


def _head_spec(s, blk0):
    return pl.BlockSpec((None, s, HEAD_PAD), lambda bi, hi: (bi, 0, blk0 + hi))


def qk_prep_fwd(name, x, blk0, n_heads, d_real, gain, rope=None, extra=None):
    b, s, _ = x.shape

    def body(*refs):
        x_ref, g_ref = refs[0], refs[1]
        pos = 2
        xv = x_ref[...]
        if extra is not None:
            xv = xv + refs[pos][...]
            pos += 1
        y = xv * lax.rsqrt(jnp.sum(xv * xv, axis=-1, keepdims=True) * (1.0 / d_real) + EPS) * g_ref[...]
        if rope is not None:
            c_ref, s_ref, r_ref = refs[pos:pos + 3]
            rot = jnp.dot(y, r_ref[...], precision=lax.Precision.HIGHEST, preferred_element_type=F32)
            y = y * c_ref[...] + rot * s_ref[...]
        refs[-1][...] = y.astype(BF16)

    vec = pl.BlockSpec((1, HEAD_PAD), lambda bi, hi: (0, 0))
    tab = pl.BlockSpec((None, s, HEAD_PAD), lambda bi, hi: (bi, 0, 0))
    in_specs, args = [_head_spec(s, blk0), vec], [x, gain]
    if extra is not None:
        e_blk = extra[1]
        in_specs.append(pl.BlockSpec((None, s, HEAD_PAD), lambda bi, hi: (bi, 0, e_blk)))
        args.append(extra[0])
    if rope is not None:
        in_specs += [tab, tab, pl.BlockSpec((HEAD_PAD, HEAD_PAD), lambda bi, hi: (0, 0))]
        args += [rope[0], rope[1], jnp.asarray(_rope_rot())]
    return pl.pallas_call(
        body, name=name, out_shape=jax.ShapeDtypeStruct((b, s, n_heads * HEAD_PAD), BF16), grid=(b, n_heads),
        in_specs=in_specs, out_specs=_head_spec(s, 0), compiler_params=_params(("parallel", "parallel")),
    )(*args)


def qk_prep_bwd(name, x, blk0, n_heads, d_real, gain, dxh, rope=None, extra=None, head_sum=False):
    b, s, _ = x.shape

    def body(*refs):
        x_ref, g_ref, dy_ref = refs[:3]
        pos = 3
        xv = x_ref[...]
        if extra is not None:
            xv = xv + refs[pos][...]
            pos += 1
        dy = dy_ref[...]
        if rope is not None:
            c_ref, s_ref, rt_ref = refs[pos:pos + 3]
            pos += 3
            dy = dy * c_ref[...] + jnp.dot(dy * s_ref[...], rt_ref[...], precision=lax.Precision.HIGHEST, preferred_element_type=F32)
        outs = refs[pos:]
        dx_ref, dg_ref = outs[0], outs[-1]
        r = lax.rsqrt(jnp.sum(xv * xv, axis=-1, keepdims=True) * (1.0 / d_real) + EPS)
        xhat = xv * r
        dxhat = dy * g_ref[...]
        dx = r * (dxhat - xhat * (jnp.sum(dxhat * xhat, axis=-1, keepdims=True) * (1.0 / d_real)))
        dx_ref[...] = dx.astype(dx_ref.dtype)
        part = jnp.sum(dy * xhat, axis=0, keepdims=True)
        first = jnp.logical_and(pl.program_id(0) == 0, pl.program_id(1) == 0)

        @pl.when(first)
        def _():
            dg_ref[...] = part

        @pl.when(jnp.logical_not(first))
        def _():
            dg_ref[...] += part

        if head_sum:
            hs_ref = outs[1]

            @pl.when(pl.program_id(1) == 0)
            def _():
                hs_ref[...] = dx

            @pl.when(pl.program_id(1) > 0)
            def _():
                hs_ref[...] += dx

    vec = pl.BlockSpec((1, HEAD_PAD), lambda bi, hi: (0, 0))
    tab = pl.BlockSpec((None, s, HEAD_PAD), lambda bi, hi: (bi, 0, 0))
    in_specs, args = [_head_spec(s, blk0), vec, _head_spec(s, 0)], [x, gain, dxh]
    if extra is not None:
        e_blk = extra[1]
        in_specs.append(pl.BlockSpec((None, s, HEAD_PAD), lambda bi, hi: (bi, 0, e_blk)))
        args.append(extra[0])
    if rope is not None:
        in_specs += [tab, tab, pl.BlockSpec((HEAD_PAD, HEAD_PAD), lambda bi, hi: (0, 0))]
        args += [rope[0], rope[1], jnp.asarray(_rope_rot().T.copy())]
    out_shape, out_specs = [jax.ShapeDtypeStruct((b, s, n_heads * HEAD_PAD), BF16)], [_head_spec(s, 0)]
    if head_sum:
        out_shape.append(jax.ShapeDtypeStruct((b, s, HEAD_PAD), F32))
        out_specs.append(tab)
    out_shape.append(jax.ShapeDtypeStruct((1, HEAD_PAD), F32))
    out_specs.append(vec)
    return pl.pallas_call(
        body, name=name, out_shape=tuple(out_shape), grid=(b, n_heads), in_specs=in_specs, out_specs=tuple(out_specs),
        compiler_params=_params(("arbitrary", "arbitrary")),
    )(*args)


def _swa_specs(v_blk0):
    q = pl.BlockSpec((None, BLOCK, GROUP_A * HEAD_PAD), lambda b, kv, n: (b, n, kv))
    kprev = pl.BlockSpec((None, BLOCK, HEAD_PAD), lambda b, kv, n: (b, jnp.maximum(n - 1, 0), kv))
    kcur = pl.BlockSpec((None, BLOCK, HEAD_PAD), lambda b, kv, n: (b, n, kv))
    vprev = pl.BlockSpec((None, BLOCK, HEAD_PAD), lambda b, kv, n: (b, jnp.maximum(n - 1, 0), v_blk0 + kv))
    vcur = pl.BlockSpec((None, BLOCK, HEAD_PAD), lambda b, kv, n: (b, n, v_blk0 + kv))
    pcol = pl.BlockSpec((None, BLOCK, 1), lambda b, kv, n: (b, n, 0))
    prow_prev = pl.BlockSpec((None, 1, BLOCK), lambda b, kv, n: (b, 0, jnp.maximum(n - 1, 0)))
    prow_cur = pl.BlockSpec((None, 1, BLOCK), lambda b, kv, n: (b, 0, n))
    smem = pl.BlockSpec(memory_space=pltpu.SMEM)
    return q, kprev, kcur, vprev, vcur, pcol, prow_prev, prow_cur, smem


def _swa_probs(q, kk, dist, valid, slope, sink):
    sc = lax.dot_general(q, kk, _NT, preferred_element_type=F32) * (HEAD_DIM_A ** -0.5)
    sc = sc - slope * dist
    sc = jnp.where(valid, sc, NEG)
    m = jnp.maximum(jnp.max(sc, axis=-1, keepdims=True), sink)
    e = jnp.exp(sc - m)
    es = jnp.exp(sink - m)
    inv = 1.0 / (jnp.sum(e, axis=-1, keepdims=True) + es)
    return e * inv, es * inv


def _swa_window(n, kp_ref, kc_ref, vp_ref, vc_ref, pc_ref, prp_ref, prc_ref):
    kk = jnp.concatenate([kp_ref[...], kc_ref[...]], axis=0)
    vv = jnp.concatenate([vp_ref[...], vc_ref[...]], axis=0).astype(BF16)
    dist = pc_ref[...] - jnp.concatenate([prp_ref[...], prc_ref[...]], axis=1)
    qi = lax.broadcasted_iota(jnp.int32, (BLOCK, 2 * BLOCK), 0) + BLOCK
    ki = lax.broadcasted_iota(jnp.int32, (BLOCK, 2 * BLOCK), 1)
    diff = qi - ki
    valid = (diff >= 0) & (diff < BLOCK) & ((n > 0) | (ki >= BLOCK))
    return kk, vv, dist, valid


def swa_fwd(name, q, k, vsrc, v_blk0, pos_col, pos_row, slopes, sinks):
    b, s, _ = q.shape
    qs, kprev, kcur, vprev, vcur, pcol, prp, prc, smem = _swa_specs(v_blk0)

    def body(q_ref, kp_ref, kc_ref, vp_ref, vc_ref, pc_ref, prp_ref, prc_ref, sl_ref, sk_ref, o_ref):
        kv, n = pl.program_id(1), pl.program_id(2)
        kk, vv, dist, valid = _swa_window(n, kp_ref, kc_ref, vp_ref, vc_ref, pc_ref, prp_ref, prc_ref)
        for g in range(GROUP_A):
            hd = kv * GROUP_A + g
            lanes = slice(g * HEAD_PAD, (g + 1) * HEAD_PAD)
            p, _ = _swa_probs(q_ref[:, lanes], kk, dist, valid, sl_ref[hd], sk_ref[hd])
            o_ref[:, lanes] = jnp.dot(p.astype(BF16), vv, preferred_element_type=F32).astype(BF16)

    return pl.pallas_call(
        body, name=name, out_shape=jax.ShapeDtypeStruct(q.shape, BF16), grid=(b, N_KV_HEADS_A, s // BLOCK),
        in_specs=[qs, kprev, kcur, vprev, vcur, pcol, prp, prc, smem, smem], out_specs=qs,
        compiler_params=_params(("parallel", "parallel", "parallel")),
    )(q, k, k, vsrc, vsrc, pos_col, pos_row, pos_row, slopes, sinks)


def swa_bwd(name, q, k, vsrc, v_blk0, pos_col, pos_row, slopes, sinks, do):
    b, s, _ = q.shape
    qs, kprev, kcur, vprev, vcur, pcol, prp, prc, smem = _swa_specs(v_blk0)

    def body(q_ref, kp_ref, kc_ref, vp_ref, vc_ref, pc_ref, prp_ref, prc_ref, sl_ref, sk_ref, do_ref, dq_ref, dk_ref, dv_ref, ds_ref):
        bi, kv, n = pl.program_id(0), pl.program_id(1), pl.program_id(2)
        kk, vv, dist, valid = _swa_window(n, kp_ref, kc_ref, vp_ref, vc_ref, pc_ref, prp_ref, prc_ref)

        @pl.when((bi == 0) & (kv == 0) & (n == 0))
        def _():
            ds_ref[...] = jnp.zeros_like(ds_ref)

        @pl.when(n == 0)
        def _():
            dk_ref[...] = jnp.zeros_like(dk_ref)
            dv_ref[...] = jnp.zeros_like(dv_ref)

        dkk = jnp.zeros((2 * BLOCK, HEAD_PAD), F32)
        dvv = jnp.zeros((2 * BLOCK, HEAD_PAD), F32)
        head_row = lax.broadcasted_iota(jnp.int32, (N_HEADS_A, LANES), 0)
        dsink = jnp.zeros((N_HEADS_A, LANES), F32)
        for g in range(GROUP_A):
            hd = kv * GROUP_A + g
            lanes = slice(g * HEAD_PAD, (g + 1) * HEAD_PAD)
            qg = q_ref[:, lanes]
            p, ps = _swa_probs(qg, kk, dist, valid, sl_ref[hd], sk_ref[hd])
            dob = do_ref[:, lanes].astype(BF16)
            dvv = dvv + lax.dot_general(p.astype(BF16), dob, _TN, preferred_element_type=F32)
            dp = lax.dot_general(dob, vv, _NT, preferred_element_type=F32)
            rs = jnp.sum(p * dp, axis=-1, keepdims=True)
            dsb = (p * (dp - rs) * (HEAD_DIM_A ** -0.5)).astype(BF16)
            dq_ref[:, lanes] = jnp.dot(dsb, kk, preferred_element_type=F32)
            dkk = dkk + lax.dot_general(dsb, qg, _TN, preferred_element_type=F32)
            dsink = dsink + jnp.where(head_row == hd, -jnp.sum(ps * rs), 0.0)
        ds_ref[...] += dsink

        @pl.when(n > 0)
        def _():
            start = pl.multiple_of((n - 1) * BLOCK, BLOCK)
            dk_ref[pl.ds(start, 2 * BLOCK), :] += dkk
            dv_ref[pl.ds(start, 2 * BLOCK), :] += dvv

        @pl.when(n == 0)
        def _():
            dk_ref[0:BLOCK, :] += dkk[BLOCK:, :]
            dv_ref[0:BLOCK, :] += dvv[BLOCK:, :]

    kv_full = pl.BlockSpec((None, s, HEAD_PAD), lambda bi, kv, n: (bi, 0, kv))
    kv_shape = jax.ShapeDtypeStruct(k.shape, F32)
    return pl.pallas_call(
        body, name=name, grid=(b, N_KV_HEADS_A, s // BLOCK),
        out_shape=(jax.ShapeDtypeStruct(q.shape, F32), kv_shape, kv_shape, jax.ShapeDtypeStruct((N_HEADS_A, LANES), F32)),
        in_specs=[qs, kprev, kcur, vprev, vcur, pcol, prp, prc, smem, smem, qs],
        out_specs=(qs, kv_full, kv_full, pl.BlockSpec((N_HEADS_A, LANES), lambda bi, kv, n: (0, 0))),
        compiler_params=_params(("arbitrary", "arbitrary", "arbitrary")),
    )(q, k, k, vsrc, vsrc, pos_col, pos_row, pos_row, slopes, sinks, do)


MLA_T = 512


def _hosted(plan, n_in, n_out, b, h):
    if plan is None:
        return [], [], [], [], ("parallel", "parallel"), (lambda refs: None), (lambda refs: None)
    ni, no = len(plan.ins), len(plan.out_shapes)

    def split(refs):
        return refs[n_in:n_in + ni], refs[n_in + ni + n_out:n_in + ni + n_out + no], refs[-2], refs[-1]

    def at_start(refs):
        @pl.when((pl.program_id(0) == 0) & (pl.program_id(1) == 0))
        def _():
            plan.start(*split(refs))

    def at_end(refs):
        @pl.when((pl.program_id(0) == b - 1) & (pl.program_id(1) == h - 1))
        def _():
            plan.finish(*split(refs))

    return plan.ins, [_HBM] * ni, plan.out_shapes, [_HBM] * no, ("arbitrary", "arbitrary"), at_start, at_end


def mla_fwd(name, q, k, v, plan=None):
    b, s, w = q.shape
    h = w // HEAD_PAD
    t = MLA_T if s % MLA_T == 0 else BLOCK
    nb = s // t
    scale = QK_DIM_B ** -0.5
    p_ins, p_in_specs, p_outs, p_out_specs, semantics, at_start, at_end = _hosted(plan, 3, 3, b, h)

    def body(*refs):
        q_ref, k_ref, v_ref = refs[:3]
        o_ref, ob_ref, lse_ref = refs[3 + len(p_ins):6 + len(p_ins)]
        at_start(refs)
        causal = lax.broadcasted_iota(jnp.int32, (t, t), 0) <= lax.broadcasted_iota(jnp.int32, (t, t), 1)
        for i in range(nb):
            q_i = q_ref[i * t:(i + 1) * t, :]

            def step(j, carry, q_i=q_i, diagonal=False):
                m, l, acc = carry
                rows = slice(j * t, (j + 1) * t) if diagonal else pl.ds(pl.multiple_of(j * t, t), t)
                st = lax.dot_general(k_ref[rows, :], q_i, _NT, preferred_element_type=F32) * scale
                if diagonal:
                    st = jnp.where(causal, st, NEG)
                m_new = jnp.maximum(m, jnp.max(st, axis=0, keepdims=True))
                a = jnp.exp(m - m_new)
                p = jnp.exp(st - m_new)
                l = a * l + jnp.sum(p, axis=0, keepdims=True)
                acc = a * acc + lax.dot_general(v_ref[rows, :], p.astype(BF16), _TN, preferred_element_type=F32)
                return m_new, l, acc

            carry = (jnp.full((1, t), NEG, F32), jnp.zeros((1, t), F32), jnp.zeros((HEAD_PAD, t), F32))
            if i > 0:
                carry = lax.fori_loop(0, i, step, carry)
            m, l, acc = step(i, carry, diagonal=True)
            o_i = (acc / l).T
            o_ref[i * t:(i + 1) * t, :] = o_i
            ob_ref[i * t:(i + 1) * t, :] = o_i.astype(BF16)
            lse_ref[i] = m + jnp.log(l)
        at_end(refs)

    hs = _head_spec(s, 0)
    ls = pl.BlockSpec((None, None, nb, 1, t), lambda bi, hi: (bi, hi, 0, 0, 0))
    res = pl.pallas_call(
        body, name=name, grid=(b, h),
        out_shape=(jax.ShapeDtypeStruct((b, s, w), F32), jax.ShapeDtypeStruct((b, s, w), BF16),
                   jax.ShapeDtypeStruct((b, h, nb, 1, t), F32)) + tuple(p_outs),
        in_specs=[hs, hs, hs] + p_in_specs, out_specs=(hs, hs, ls) + tuple(p_out_specs),
        scratch_shapes=plan.scratch() if plan is not None else [], compiler_params=_params(semantics),
    )(q, k, v, *p_ins)
    return res[0], res[1], res[2], list(res[3:])


def mla_bwd(name, q, k, v, o, do, lse, plan=None):
    b, s, w = q.shape
    h = w // HEAD_PAD
    _, _, nb, _, t = lse.shape
    scale = QK_DIM_B ** -0.5
    p_ins, p_in_specs, p_outs, p_out_specs, semantics, at_start, at_end = _hosted(plan, 6, 3, b, h)

    def body(*refs):
        q_ref, k_ref, v_ref, o_ref, do_ref, lse_ref = refs[:6]
        dq_ref, dk_ref, dv_ref = refs[6 + len(p_ins):9 + len(p_ins)]
        dv_acc = refs[9 + len(p_ins) + len(p_outs)]
        at_start(refs)
        causal = lax.broadcasted_iota(jnp.int32, (t, t), 0) <= lax.broadcasted_iota(jnp.int32, (t, t), 1)
        dk_ref[...] = jnp.zeros_like(dk_ref)
        dv_acc[...] = jnp.zeros_like(dv_acc)
        for i in range(nb):
            q_i = q_ref[i * t:(i + 1) * t, :]
            do_i = do_ref[i * t:(i + 1) * t, :]
            delta = jnp.sum((o_ref[i * t:(i + 1) * t, :] * do_i).T, axis=0, keepdims=True)
            do_b = do_i.astype(BF16)
            lse_i = lse_ref[i]

            def step(j, dqt, q_i=q_i, do_b=do_b, delta=delta, lse_i=lse_i, diagonal=False):
                rows = slice(j * t, (j + 1) * t) if diagonal else pl.ds(pl.multiple_of(j * t, t), t)
                k_j = k_ref[rows, :]
                st = lax.dot_general(k_j, q_i, _NT, preferred_element_type=F32) * scale
                if diagonal:
                    st = jnp.where(causal, st, NEG)
                pt = jnp.exp(st - lse_i)
                dpt = lax.dot_general(v_ref[rows, :], do_b, _NT, preferred_element_type=F32)
                dst = (pt * (dpt - delta) * scale).astype(BF16)
                dv_acc[rows, :] += jnp.dot(pt.astype(BF16), do_b, preferred_element_type=F32)
                dk_ref[rows, :] += jnp.dot(dst, q_i, preferred_element_type=F32)
                return dqt + lax.dot_general(k_j, dst, _TN, preferred_element_type=F32)

            dqt = jnp.zeros((HEAD_PAD, t), F32)
            if i > 0:
                dqt = lax.fori_loop(0, i, step, dqt)
            dq_ref[i * t:(i + 1) * t, :] = step(i, dqt, diagonal=True).T
        dv_ref[...] = dv_acc[...].astype(BF16)
        at_end(refs)

    hs = _head_spec(s, 0)
    ls = pl.BlockSpec((None, None, nb, 1, t), lambda bi, hi: (bi, hi, 0, 0, 0))
    res = pl.pallas_call(
        body, name=name, grid=(b, h),
        out_shape=(jax.ShapeDtypeStruct((b, s, w), F32), jax.ShapeDtypeStruct((b, s, w), F32),
                   jax.ShapeDtypeStruct((b, s, w), BF16)) + tuple(p_outs),
        in_specs=[hs, hs, hs, hs, hs, ls] + p_in_specs, out_specs=(hs, hs, hs) + tuple(p_out_specs),
        scratch_shapes=[pltpu.VMEM((s, HEAD_PAD), F32)] + (plan.scratch() if plan is not None else []),
        compiler_params=_params(semantics),
    )(q, k, v, o, do, lse, *p_ins)
    return res[0], res[1], res[2], list(res[3:])


_HBM = pl.BlockSpec(memory_space=pltpu.HBM)


def _place():
    x, y, c = lax.axis_index("x"), lax.axis_index("y"), lax.axis_index("c")
    chips = [(1 - x, y), (x, 1 - y), (1 - x, 1 - y)]
    return x, y, c, chips


def _remote(src, dst, send_sem, recv_sem, dev):
    return pltpu.make_async_remote_copy(src_ref=src, dst_ref=dst, send_sem=send_sem, recv_sem=recv_sem,
                                        device_id=dev, device_id_type=MESH)


class CommPlan:
    def __init__(self, ins, out_shapes, n_sems, start, finish, aliases=None):
        self.ins, self.out_shapes, self.n_sems = list(ins), list(out_shapes), n_sems
        self.start, self.finish, self.aliases = start, finish, dict(aliases or {})

    def scratch(self):
        return [pltpu.SemaphoreType.DMA((self.n_sems,)), pltpu.SemaphoreType.DMA((self.n_sems,))]


def comm_call(name, plan):
    ni, no = len(plan.ins), len(plan.out_shapes)

    def body(*refs):
        ins, outs, (send_sems, recv_sems) = refs[:ni], refs[ni:ni + no], refs[ni + no:]
        plan.start(ins, outs, send_sems, recv_sems)
        plan.finish(ins, outs, send_sems, recv_sems)

    return pl.pallas_call(
        body, name=name, out_shape=tuple(plan.out_shapes), in_specs=[_HBM] * ni, out_specs=tuple([_HBM] * no),
        input_output_aliases=plan.aliases, scratch_shapes=plan.scratch(),
    )(*plan.ins)


def gather_plan(packs, l):
    nk = len(packs)

    def parts(p_refs, o_refs, ss, rs):
        x, y, c, chips = _place()
        s_me = 2 * x + y
        sibling = (x, y, 1 - c)
        first = [_remote(p_refs[k].at[l], o_refs[k].at[s_me], ss.at[j * nk + k], rs.at[j * nk + k], (cx, cy, c))
                 for j, (cx, cy) in enumerate(chips) for k in range(nk)]
        own = [_remote(p_refs[k].at[l], o_refs[k].at[s_me], ss.at[6 * nk + k], rs.at[6 * nk + k], sibling) for k in range(nk)]
        return c, chips, sibling, first, own

    def start(p_refs, o_refs, ss, rs):
        c, _, _, first, own = parts(p_refs, o_refs, ss, rs)
        for cp in own:
            cp.start()

        @pl.when(c == l)
        def _():
            for cp in first:
                cp.start()

    def finish(p_refs, o_refs, ss, rs):
        c, chips, sibling, first, own = parts(p_refs, o_refs, ss, rs)

        @pl.when(c == l)
        def _():
            passed = []
            for j, (cx, cy) in enumerate(chips):
                for k in range(nk):
                    blk = o_refs[k].at[2 * cx + cy]
                    _remote(blk, blk, ss.at[j * nk + k], rs.at[j * nk + k], (cx, cy, c)).wait_recv()
                    fwd = _remote(blk, blk, ss.at[(3 + j) * nk + k], rs.at[(3 + j) * nk + k], sibling)
                    fwd.start()
                    passed.append(fwd)
            for cp in first + passed:
                cp.wait_send()

        @pl.when(c != l)
        def _():
            for j, (cx, cy) in enumerate(chips):
                for k in range(nk):
                    blk = o_refs[k].at[2 * cx + cy]
                    _remote(blk, blk, ss.at[(3 + j) * nk + k], rs.at[(3 + j) * nk + k], sibling).wait_recv()

        for cp in own:
            cp.wait()

    outs = [jax.ShapeDtypeStruct((N_SHARDS,) + p.shape[1:], p.dtype) for p in packs]
    return CommPlan(packs, outs, 7 * nk, start, finish)


def swap_plan(grads, l):
    nk = len(grads)

    def copies(g_refs, o_refs, ss, rs):
        x, y, c, _ = _place()
        return c, [_remote(g_refs[k], o_refs[k], ss.at[k], rs.at[k], (x, y, 1 - c)) for k in range(nk)]

    def start(g_refs, o_refs, ss, rs):
        c, cps = copies(g_refs, o_refs, ss, rs)

        @pl.when(c != l)
        def _():
            for cp in cps:
                cp.start()

    def finish(g_refs, o_refs, ss, rs):
        c, cps = copies(g_refs, o_refs, ss, rs)

        @pl.when(c != l)
        def _():
            for cp in cps:
                cp.wait_send()

        @pl.when(c == l)
        def _():
            for cp in cps:
                cp.wait_recv()

    return CommPlan(grads, [jax.ShapeDtypeStruct(g.shape, g.dtype) for g in grads], nk, start, finish)


def scatter_plan(parts, l):
    nk = len(parts)

    def copies(p_refs, o_refs, ss, rs):
        x, y, c, chips = _place()
        s_me = 2 * x + y
        sends = [_remote(p_refs[k].at[2 * cx + cy], o_refs[k].at[s_me], ss.at[j * nk + k], rs.at[j * nk + k], (cx, cy, c))
                 for j, (cx, cy) in enumerate(chips) for k in range(nk)]
        return c, chips, sends

    def start(p_refs, o_refs, ss, rs):
        c, _, sends = copies(p_refs, o_refs, ss, rs)

        @pl.when(c == l)
        def _():
            for cp in sends:
                cp.start()

    def finish(p_refs, o_refs, ss, rs):
        c, chips, sends = copies(p_refs, o_refs, ss, rs)

        @pl.when(c == l)
        def _():
            for j, (cx, cy) in enumerate(chips):
                for k in range(nk):
                    slot = o_refs[k].at[2 * cx + cy]
                    _remote(slot, slot, ss.at[j * nk + k], rs.at[j * nk + k], (cx, cy, c)).wait_recv()
            for cp in sends:
                cp.wait_send()

    return CommPlan(parts, [jax.ShapeDtypeStruct(p.shape, p.dtype) for p in parts], 3 * nk, start, finish)


def share_plan(finals):
    flat = [(l, f) for l, fs in enumerate(finals) for f in fs]

    def copies(f_refs, o_refs, ss, rs):
        x, y, c, _ = _place()
        return c, [_remote(f_refs[i], o_refs[i], ss.at[i], rs.at[i], (x, y, 1 - c)) for i in range(len(flat))]

    def start(f_refs, o_refs, ss, rs):
        c, cps = copies(f_refs, o_refs, ss, rs)
        for l in range(len(finals)):
            @pl.when(c == l)
            def _(l=l):
                for cp, (lf, _) in zip(cps, flat):
                    if lf == l:
                        cp.start()

    def finish(f_refs, o_refs, ss, rs):
        c, cps = copies(f_refs, o_refs, ss, rs)
        for l in range(len(finals)):
            @pl.when(c == l)
            def _(l=l):
                for cp, (lf, _) in zip(cps, flat):
                    if lf == l:
                        cp.wait_send()

            @pl.when(c != l)
            def _(l=l):
                for cp, (lf, _) in zip(cps, flat):
                    if lf == l:
                        cp.wait_recv()

    arrays = [f for _, f in flat]
    return CommPlan(arrays, [jax.ShapeDtypeStruct(f.shape, f.dtype) for f in arrays], len(flat), start, finish,
                    aliases={i: i for i in range(len(flat))})


def add_pair(name, own, other, tr=256):
    n, cols = own.shape
    tr = _tile(n, tr, 16)

    def body(g_ref, o_ref, out_ref):
        out_ref[...] = (g_ref[...].astype(F32) + o_ref[...].astype(F32)).astype(out_ref.dtype)

    row = pl.BlockSpec((tr, cols), lambda i: (i, 0))
    return pl.pallas_call(
        body, name=name, out_shape=jax.ShapeDtypeStruct((n, cols), BF16), grid=(n // tr,), in_specs=[row, row], out_specs=row,
        compiler_params=_params(("parallel",)),
    )(own, other)


def sum_slots(name, slots, part, shard, tr=256):
    ns, r, cols = slots.shape
    tr = _tile(r, tr, 16)

    def body(s_idx, s_ref, own_ref, o_ref):
        total = own_ref[...].astype(F32)
        for k in range(1, ns):
            total = total + s_ref[(s_idx[0] + k) % ns].astype(F32)
        o_ref[...] = total

    return pl.pallas_call(
        body, name=name, out_shape=jax.ShapeDtypeStruct((r, cols), F32),
        grid_spec=pltpu.PrefetchScalarGridSpec(
            num_scalar_prefetch=1, grid=(r // tr,),
            in_specs=[pl.BlockSpec((ns, tr, cols), lambda i, p: (0, i, 0)), pl.BlockSpec((None, tr, cols), lambda i, p: (p[0], i, 0))],
            out_specs=pl.BlockSpec((tr, cols), lambda i, p: (i, 0))),
        compiler_params=_params(("parallel",)),
    )(shard, slots, part)


def small_allreduce(vec):
    r, cols = vec.shape

    def body(v_ref, o_ref, buf, send_sems, recv_sems):
        x, y, c, _ = _place()
        me = 4 * x + 2 * y + c
        buf[me] = v_ref[...]
        peers = []
        for k in range(1, N_DEV):
            px = (1 - x) if (k & 4) else x
            py = (1 - y) if (k & 2) else y
            pc = (1 - c) if (k & 1) else c
            peers.append((px, py, pc))
        sends = [_remote(buf.at[me], buf.at[me], send_sems.at[k], recv_sems.at[k], peer) for k, peer in enumerate(peers)]
        for cp in sends:
            cp.start()
        for k, (px, py, pc) in enumerate(peers):
            slot = buf.at[4 * px + 2 * py + pc]
            _remote(slot, slot, send_sems.at[k], recv_sems.at[k], (px, py, pc)).wait_recv()
        for cp in sends:
            cp.wait_send()
        total = buf[0]
        for k in range(1, N_DEV):
            total = total + buf[k]
        o_ref[...] = total

    return pl.pallas_call(
        body, name="small_allreduce", out_shape=jax.ShapeDtypeStruct((r, cols), F32),
        in_specs=[pl.BlockSpec(memory_space=pltpu.VMEM)], out_specs=pl.BlockSpec(memory_space=pltpu.VMEM),
        scratch_shapes=[pltpu.VMEM((N_DEV, r, cols), F32), pltpu.SemaphoreType.DMA((N_DEV - 1,)), pltpu.SemaphoreType.DMA((N_DEV - 1,))],
    )(vec)


ARG_NAMES = (("x", "positions") + WEIGHT_NAMES + ("loss_target",) + tuple("m_" + n for n in WEIGHT_NAMES)
             + tuple("v_" + n for n in WEIGHT_NAMES))

_C_KA = N_HEADS_A * HEAD_PAD
_C_VA = _C_KA + N_KV_HEADS_A * HEAD_PAD
_C_CQ = _C_VA + N_KV_HEADS_A * HEAD_PAD
_C_CKV = _C_CQ + Q_LORA_RANK
_C_KR = _C_CKV + KV_LORA_RANK
_C_GA = _C_KR + HEAD_PAD
_R_KA = WIDTH_A
_R_VA = _R_KA + KV_WIDTH_A
_R_CQ = _R_VA + KV_WIDTH_A
_R_KR = _R_CQ + Q_LORA_RANK + KV_LORA_RANK
_R_GA = _R_KR + QK_ROPE_DIM


def _pad_head_cols(w, n_heads):
    r, dh = w.shape[0], w.shape[1] // n_heads
    return jnp.pad(w.reshape(r, n_heads, dh), ((0, 0), (0, 0), (0, HEAD_PAD - dh))).reshape(r, n_heads * HEAD_PAD)


def _unpad_head_cols(w, n_heads, dh):
    r = w.shape[0]
    return w.reshape(r, n_heads, HEAD_PAD)[:, :, :dh].reshape(r, n_heads * dh)


def _pad_head_rows(w, n_heads):
    dh, c = w.shape[0] // n_heads, w.shape[1]
    return jnp.pad(w.reshape(n_heads, dh, c), ((0, 0), (0, HEAD_PAD - dh), (0, 0))).reshape(n_heads * HEAD_PAD, c)


def _unpad_head_rows(w, n_heads, dh):
    c = w.shape[1]
    return w.reshape(n_heads, HEAD_PAD, c)[:, :dh].reshape(n_heads * dh, c)


def _join_cols(shards):
    ns, r, c = shards.shape
    return shards.transpose(1, 0, 2).reshape(r, ns * c)


def _split_cols(mat):
    r, c4 = mat.shape
    return mat.reshape(r, N_SHARDS, c4 // N_SHARDS).transpose(1, 0, 2)


def _ffn_fwd(tag, x, gain, wts, f):
    nb = rms_fwd(tag + "_rms", x, gain)
    gu, h = ffn_up_act(tag + "_up", nb, wts["col"], 2 * f)
    out = gmm_down(tag + "_down", h, wts["down"], f, x, 0.5)
    return out, (x, nb, gu, h)


def _ffn_bwd(tag, saved, gain, wts, gbuf, f, dout, doutb):
    x, nb, gu, h = saved
    n = gu.shape[-1]
    dgu = ffn_down_dact(tag + "_dh", doutb, wts["down"], gu, n, f, 0.5)
    gbuf["down"] = gmm_down_dw(tag + "_dwd", h, doutb, gbuf["down"], wts["down"].shape, f, 0.5)
    gbuf["col"] = gmm_up_dw(tag + "_dwgu", nb, dgu, gbuf["col"], wts["col"].shape, 2 * f)
    dn = gmm_up_dx(tag + "_dn", dgu, wts["col"], x.shape[1], 2 * f)
    dx, dxb, dgain = rms_bwd(tag + "_drms", x, gain, dn, dres=dout)
    return dx, dxb, dgain


def _pad_gain(g):
    return jnp.pad(g, ((0, 0), (0, HEAD_PAD - g.shape[1])))


def _mixer_fwd(tag, x1, sm, w, wts, aux, plan=None):
    b, s = aux["b"], aux["s"]
    t, d = x1.shape
    hb = rms_fwd(tag + "_rms", x1, sm["mix_norm"])
    proj = mm_nn(tag + "_proj", hb, w["win"], tm=512, tn=2048, tk=1024)
    proj3 = proj.reshape(b, s, proj.shape[1])
    gains = {n: _pad_gain(sm[n]) for n in ("swa_q_norm", "swa_k_norm", "mla_q_norm", "mla_k_norm")}
    cqn = rms_fwd(tag + "_rms_cq", proj, sm["mla_q_lora_norm"], tm=1024, col_blk=_C_CQ // Q_LORA_RANK)
    ckvn = rms_fwd(tag + "_rms_ckv", proj, sm["mla_kv_lora_norm"], tm=1024, col_blk=_C_CKV // KV_LORA_RANK)
    qb_raw = mm_nn(tag + "_uq", cqn, w["wuq"], tm=1024).reshape(b, s, N_HEADS_B * HEAD_PAD)
    kb_raw = mm_nn(tag + "_uk", ckvn, w["wk"], tm=1024).reshape(b, s, N_HEADS_B * HEAD_PAD)
    vb = mm_nn(tag + "_uv", ckvn, w["wv"], tm=1024, out_dtype=BF16).reshape(b, s, N_HEADS_B * HEAD_PAD)
    qah = qk_prep_fwd(tag + "_qa_norm", proj3, 0, N_HEADS_A, HEAD_DIM_A, gains["swa_q_norm"])
    kah = qk_prep_fwd(tag + "_ka_norm", proj3, _C_KA // HEAD_PAD, N_KV_HEADS_A, HEAD_DIM_A, gains["swa_k_norm"])
    sinks = sm["swa_sinks"].reshape(-1)
    oab = swa_fwd(tag + "_swa", qah, kah, proj3, _C_VA // HEAD_PAD, aux["pos_col"], aux["pos_row"], aux["slopes"], sinks)
    kr = (proj3, _C_KR // HEAD_PAD)
    qbh = qk_prep_fwd(tag + "_qb_norm", qb_raw, 0, N_HEADS_B, QK_DIM_B, gains["mla_q_norm"], aux["rope"])
    kbh = qk_prep_fwd(tag + "_kb_norm", kb_raw, 0, N_HEADS_B, QK_DIM_B, gains["mla_k_norm"], aux["rope"], extra=kr)
    ob, obb, lse, carried = mla_fwd(tag + "_mla", qbh, kbh, vb, plan)
    oab2, obb2 = oab.reshape(t, -1), obb.reshape(t, -1)
    ya = mm_nn(tag + "_branch_a", oab2, w["wa"], tm=1024)
    yb = mm_nn(tag + "_branch_b", obb2, w["wb"], tm=1024)
    mg = gate_fwd(tag + "_gate", proj, ya, yb, _C_GA, _C_GA + d)
    x2 = gmm_rows(tag + "_out", mg, wts["wo"], x1)
    saved = dict(x1=x1, hb=hb, proj=proj, cqn=cqn, ckvn=ckvn, qb_raw=qb_raw, kb_raw=kb_raw, vb=vb, qah=qah, kah=kah,
                 qbh=qbh, kbh=kbh, ob=ob, lse=lse, oab2=oab2, obb2=obb2, ya=ya, yb=yb, mg=mg, gains=gains)
    return x2, saved, carried


def _mixer_bwd(tag, sv, sm, w, wts, gbuf, aux, dx2, dx2b, plan=None):
    b, s = aux["b"], aux["s"]
    t, d = dx2.shape
    proj, gains = sv["proj"], sv["gains"]
    proj3 = proj.reshape(b, s, proj.shape[1])
    kr = (proj3, _C_KR // HEAD_PAD)
    dmg = gmm_rows_dx(tag + "_d_out", dx2b, wts["wo"])
    gbuf["wo"] = gmm_rows_dw(tag + "_dw_out", sv["mg"], dx2b)
    dya, dyb, dga, dgb = gate_bwd(tag + "_dgate", proj, sv["ya"], sv["yb"], dmg, _C_GA, _C_GA + d)
    dwa = mm_tn(tag + "_dw_branch_a", sv["oab2"], dya, out_dtype=BF16)
    dwb = mm_tn(tag + "_dw_branch_b", sv["obb2"], dyb, out_dtype=BF16)
    doa = mm_nt(tag + "_d_branch_a", dya, w["wa"], tm=1024).reshape(b, s, -1)
    dob = mm_nt(tag + "_d_branch_b", dyb, w["wb"], tm=1024).reshape(b, s, -1)
    sinks = sm["swa_sinks"].reshape(-1)
    dqah, dkah, dva, dsinks = swa_bwd(tag + "_dswa", sv["qah"], sv["kah"], proj3, _C_VA // HEAD_PAD, aux["pos_col"], aux["pos_row"],
                                      aux["slopes"], sinks, doa)
    dqa_raw, dg_swa_q = qk_prep_bwd(tag + "_dqa_norm", proj3, 0, N_HEADS_A, HEAD_DIM_A, gains["swa_q_norm"], dqah)
    dka_raw, dg_swa_k = qk_prep_bwd(tag + "_dka_norm", proj3, _C_KA // HEAD_PAD, N_KV_HEADS_A, HEAD_DIM_A, gains["swa_k_norm"], dkah)
    dqbh, dkbh, dvb, carried = mla_bwd(tag + "_dmla", sv["qbh"], sv["kbh"], sv["vb"], sv["ob"], dob, sv["lse"], plan)
    dqb_raw, dg_mla_q = qk_prep_bwd(tag + "_dqb_norm", sv["qb_raw"], 0, N_HEADS_B, QK_DIM_B, gains["mla_q_norm"], dqbh, aux["rope"])
    dkb_raw, dkr_sum, dg_mla_k = qk_prep_bwd(tag + "_dkb_norm", sv["kb_raw"], 0, N_HEADS_B, QK_DIM_B, gains["mla_k_norm"], dkbh,
                                             aux["rope"], extra=kr, head_sum=True)
    dq_tok, dk_tok, dv_tok = dqb_raw.reshape(t, -1), dkb_raw.reshape(t, -1), dvb.reshape(t, -1)
    dwuq = mm_tn(tag + "_dw_uq", sv["cqn"], dq_tok, tk=1024, out_dtype=BF16)
    dwk = mm_tn(tag + "_dw_uk", sv["ckvn"], dk_tok, tk=1024, out_dtype=BF16)
    dwv = mm_tn(tag + "_dw_uv", sv["ckvn"], dv_tok, tk=1024, out_dtype=BF16)
    dcqn = mm_nt(tag + "_d_uq", dq_tok, w["wuq"], tm=1024)
    dckvn = mm_nt(tag + "_d_uv", dv_tok, w["wv"], tm=1024, res=mm_nt(tag + "_d_uk", dk_tok, w["wk"], tm=1024))
    dcq, dg_q_lora = rms_bwd(tag + "_drms_cq", proj, sm["mla_q_lora_norm"], dcqn, want_f32=False, tm=1024, col_blk=_C_CQ // Q_LORA_RANK)
    dckv, dg_kv_lora = rms_bwd(tag + "_drms_ckv", proj, sm["mla_kv_lora_norm"], dckvn, want_f32=False, tm=1024,
                               col_blk=_C_CKV // KV_LORA_RANK)
    dproj = jnp.concatenate([dqa_raw.reshape(t, -1), dka_raw.reshape(t, -1), dva.reshape(t, -1).astype(BF16), dcq, dckv,
                             dkr_sum.reshape(t, HEAD_PAD).astype(BF16), dga, dgb], axis=1)
    dwin = mm_tn(tag + "_dw_in", sv["hb"], dproj, tm=1024, tn=2048, tk=1024, out_dtype=BF16)
    dh = mm_nt(tag + "_d_in", dproj, w["win"], tm=512, tn=1024, tk=2048)
    dx1, dx1b, dg_mix = rms_bwd(tag + "_drms", sv["x1"], sm["mix_norm"], dh, dres=dx2)
    wgrads = dict(win=dwin, wuq=dwuq, wk=dwk, wv=dwv, wa=dwa, wb=dwb)
    sgrads = dict(mix_norm=dg_mix, swa_q_norm=dg_swa_q[:, :HEAD_DIM_A], swa_k_norm=dg_swa_k[:, :HEAD_DIM_A],
                  swa_sinks=dsinks[:, 0].reshape(1, -1), mla_q_lora_norm=dg_q_lora, mla_kv_lora_norm=dg_kv_lora,
                  mla_q_norm=dg_mla_q[:, :QK_DIM_B], mla_k_norm=dg_mla_k[:, :QK_DIM_B])
    return dx1, dx1b, wgrads, sgrads, carried


def _layer_weights(win4, uq4, ukv4, branch4):
    wr = _join_cols(win4)
    kr = jnp.pad(wr[:, _R_KR:_R_GA], ((0, 0), (QK_NOPE_DIM, HEAD_PAD - QK_DIM_B)))
    win = jnp.concatenate([_pad_head_cols(wr[:, :_R_KA], N_HEADS_A), _pad_head_cols(wr[:, _R_KA:_R_VA], N_KV_HEADS_A),
                           _pad_head_cols(wr[:, _R_VA:_R_CQ], N_KV_HEADS_A), wr[:, _R_CQ:_R_KR], kr, wr[:, _R_GA:]], axis=1)
    ukv = _join_cols(ukv4)
    ukv3 = ukv.reshape(ukv.shape[0], N_HEADS_B, QK_NOPE_DIM + V_DIM_B)
    r = branch4.shape[1] // 2
    return dict(win=win, wuq=_pad_head_cols(_join_cols(uq4), N_HEADS_B),
                wk=_pad_head_cols(ukv3[:, :, :QK_NOPE_DIM].reshape(ukv.shape[0], -1), N_HEADS_B),
                wv=_pad_head_cols(ukv3[:, :, QK_NOPE_DIM:].reshape(ukv.shape[0], -1), N_HEADS_B),
                wa=_pad_head_rows(_join_cols(branch4[:, :r]), N_HEADS_A), wb=_pad_head_rows(_join_cols(branch4[:, r:]), N_HEADS_B))


def _mixer_grad_shards(g):
    dw = g["win"]
    win_ref = jnp.concatenate([_unpad_head_cols(dw[:, :_C_KA], N_HEADS_A, HEAD_DIM_A),
                               _unpad_head_cols(dw[:, _C_KA:_C_VA], N_KV_HEADS_A, HEAD_DIM_A),
                               _unpad_head_cols(dw[:, _C_VA:_C_CQ], N_KV_HEADS_A, HEAD_DIM_A), dw[:, _C_CQ:_C_KR],
                               dw[:, _C_KR + QK_NOPE_DIM:_C_KR + QK_DIM_B], dw[:, _C_GA:]], axis=1)
    rk = g["wk"].shape[0]
    ukv = jnp.concatenate([g["wk"].reshape(rk, N_HEADS_B, HEAD_PAD)[:, :, :QK_NOPE_DIM],
                           g["wv"].reshape(rk, N_HEADS_B, HEAD_PAD)[:, :, :V_DIM_B]], axis=2).reshape(rk, -1)
    return (_split_cols(win_ref), _split_cols(_unpad_head_cols(g["wuq"], N_HEADS_B, QK_DIM_B)), _split_cols(ukv),
            jnp.concatenate([_split_cols(_unpad_head_rows(g["wa"], N_HEADS_A, HEAD_DIM_A)),
                             _split_cols(_unpad_head_rows(g["wb"], N_HEADS_B, V_DIM_B))], axis=1))


BUFFER_TAGS = ("col", "down", "wo", "win", "uq", "ukv", "branch")


def _layer_buffers(gathered):
    col, down, wo, win4, uq4, ukv4, branch4 = gathered
    return dict(col=col, down=down, wo=wo), _layer_weights(win4, uq4, ukv4, branch4)


def _train_step(x, positions, target, packs, small, shard):
    depth = 2
    b, s, d = x.shape
    t = b * s
    posf = positions.astype(F32)
    half = QK_ROPE_DIM // 2
    inv_freq = ROPE_BASE ** (-jnp.arange(half, dtype=F32) / half)
    ang = posf[..., None] * inv_freq
    cos, sin = jnp.cos(ang), jnp.sin(ang)
    tail = HEAD_PAD - QK_DIM_B
    rope = (jnp.concatenate([jnp.ones((b, s, QK_NOPE_DIM), F32), cos, cos, jnp.ones((b, s, tail), F32)], axis=-1),
            jnp.concatenate([jnp.zeros((b, s, QK_NOPE_DIM), F32), sin, sin, jnp.zeros((b, s, tail), F32)], axis=-1))
    slopes = jnp.exp2(-8.0 * (jnp.arange(N_HEADS_A, dtype=F32) + 1.0) / N_HEADS_A)
    aux = dict(b=b, s=s, pos_col=posf.reshape(b, s, 1), pos_row=posf.reshape(b, 1, s), rope=rope, slopes=slopes)

    def sm_of(l):
        return {n: small[n][l:l + 1] for n in SMALL_NAMES}

    wts, weights, saved, sgrads, parts, slots = [None] * depth, [None] * depth, [None] * depth, [None] * depth, [None] * depth, [None] * depth

    wts[0], weights[0] = _layer_buffers(comm_call("gather_l0", gather_plan(packs, 0)))
    h = x.reshape(t, d)
    for l in range(depth):
        sm = sm_of(l)
        h, s1 = _ffn_fwd(f"l{l}_ffn1", h, sm["ffn1_norm"], wts[l], 0)
        h, s2, carried = _mixer_fwd(f"l{l}_mix", h, sm, weights[l], wts[l], aux, gather_plan(packs, 1) if l == 0 else None)
        if l == 0:
            wts[1], weights[1] = _layer_buffers(carried)
        h, s3 = _ffn_fwd(f"l{l}_ffn2", h, sm["ffn2_norm"], wts[l], 1)
        saved[l] = (s1, s2, s3)
    loss, dh, dhb = loss_fwd_bwd("loss", h, target.reshape(t, d))

    for l in reversed(range(depth)):
        sm = sm_of(l)
        s1, s2, s3 = saved[l]
        gbuf = dict(col=None, down=None, wo=None)
        dh, dhb, dg_f2 = _ffn_bwd(f"l{l}_ffn2", s3, sm["ffn2_norm"], wts[l], gbuf, 1, dh, dhb)
        plan = scatter_plan(parts[1], 1) if l == 0 else None
        dh, dhb, wg, sg, carried = _mixer_bwd(f"l{l}_mix", s2, sm, weights[l], wts[l], gbuf, aux, dh, dhb, plan)
        if l == 0:
            slots[1] = carried
        dh, dhb, dg_f1 = _ffn_bwd(f"l{l}_ffn1", s1, sm["ffn1_norm"], wts[l], gbuf, 0, dh, dhb)
        sg.update(ffn1_norm=dg_f1, ffn2_norm=dg_f2)
        sgrads[l] = sg
        mine = [gbuf["col"], gbuf["down"], gbuf["wo"]] + list(_mixer_grad_shards(wg))
        theirs = comm_call(f"swap_l{l}", swap_plan(mine, l))
        parts[l] = [add_pair(f"l{l}_add_{tag}", g.reshape(-1, g.shape[-1]), o.reshape(-1, o.shape[-1])).reshape(g.shape)
                    for tag, g, o in zip(BUFFER_TAGS, mine, theirs)]
    slots[0] = comm_call("scatter_l0", scatter_plan(parts[0], 0))
    finals = [[sum_slots(f"l{l}_sum_{tag}", sl, p, shard) for tag, sl, p in zip(BUFFER_TAGS, slots[l], parts[l])] for l in range(depth)]
    shared = comm_call("share", share_plan(finals))
    nk = len(BUFFER_TAGS)
    return loss, dh.reshape(b, s, d), [shared[l * nk:(l + 1) * nk] for l in range(depth)], sgrads


def kernel(x, positions, ffn1_norm, ffn1_w_gate, ffn1_w_up, ffn1_w_down, mix_norm, w_in, swa_q_norm, swa_k_norm, swa_sinks, mla_q_lora_norm, mla_w_uq, mla_kv_lora_norm, mla_w_ukv, mla_q_norm, mla_k_norm, w_branch_a, w_branch_b, w_out, ffn2_norm, ffn2_w_gate, ffn2_w_up, ffn2_w_down, loss_target, m_ffn1_norm, m_ffn1_w_gate, m_ffn1_w_up, m_ffn1_w_down, m_mix_norm, m_w_in, m_swa_q_norm, m_swa_k_norm, m_swa_sinks, m_mla_q_lora_norm, m_mla_w_uq, m_mla_kv_lora_norm, m_mla_w_ukv, m_mla_q_norm, m_mla_k_norm, m_w_branch_a, m_w_branch_b, m_w_out, m_ffn2_norm, m_ffn2_w_gate, m_ffn2_w_up, m_ffn2_w_down, v_ffn1_norm, v_ffn1_w_gate, v_ffn1_w_up, v_ffn1_w_down, v_mix_norm, v_w_in, v_swa_q_norm, v_swa_k_norm, v_swa_sinks, v_mla_q_lora_norm, v_mla_w_uq, v_mla_kv_lora_norm, v_mla_w_ukv, v_mla_q_norm, v_mla_k_norm, v_w_branch_a, v_w_branch_b, v_w_out, v_ffn2_norm, v_ffn2_w_gate, v_ffn2_w_up, v_ffn2_w_down):
    args = (x, positions, ffn1_norm, ffn1_w_gate, ffn1_w_up, ffn1_w_down, mix_norm, w_in, swa_q_norm, swa_k_norm, swa_sinks, mla_q_lora_norm, mla_w_uq, mla_kv_lora_norm, mla_w_ukv, mla_q_norm, mla_k_norm, w_branch_a, w_branch_b, w_out, ffn2_norm, ffn2_w_gate, ffn2_w_up, ffn2_w_down, loss_target, m_ffn1_norm, m_ffn1_w_gate, m_ffn1_w_up, m_ffn1_w_down, m_mix_norm, m_w_in, m_swa_q_norm, m_swa_k_norm, m_swa_sinks, m_mla_q_lora_norm, m_mla_w_uq, m_mla_kv_lora_norm, m_mla_w_ukv, m_mla_q_norm, m_mla_k_norm, m_w_branch_a, m_w_branch_b, m_w_out, m_ffn2_norm, m_ffn2_w_gate, m_ffn2_w_up, m_ffn2_w_down, v_ffn1_norm, v_ffn1_w_gate, v_ffn1_w_up, v_ffn1_w_down, v_mix_norm, v_w_in, v_swa_q_norm, v_swa_k_norm, v_swa_sinks, v_mla_q_lora_norm, v_mla_w_uq, v_mla_kv_lora_norm, v_mla_w_ukv, v_mla_q_norm, v_mla_k_norm, v_w_branch_a, v_w_branch_b, v_w_out, v_ffn2_norm, v_ffn2_w_gate, v_ffn2_w_up, v_ffn2_w_down)
    a = dict(zip(ARG_NAMES, args, strict=True))
    x = a["x"]
    depth = a["ffn1_norm"].shape[0]
    d = x.shape[-1]
    assert depth == 2, "one layer slab per core of a chip"

    packs = [jnp.concatenate([a[n] for n in COL_NAMES], axis=1).astype(BF16),
             jnp.concatenate([a[n] for n in DOWN_NAMES], axis=1).astype(BF16),
             a["w_out"].astype(BF16), a["w_in"].astype(BF16), a["mla_w_uq"].astype(BF16), a["mla_w_ukv"].astype(BF16),
             jnp.concatenate([a["w_branch_a"], a["w_branch_b"]], axis=1).astype(BF16)]
    shard_me = (2 * lax.axis_index("x") + lax.axis_index("y")).astype(jnp.int32).reshape(1)
    small = {n: a[n] for n in SMALL_NAMES}

    loss, grad_x, summed, sgrads = _train_step(x, a["positions"], a["loss_target"], packs, small, shard_me)

    fcol, fdown, fwo, fwin, fuq, fukv, fbranch = [jnp.stack([summed[l][k] for l in range(depth)]) for k in range(len(BUFFER_TAGS))]
    n_ff, r_branch = a["ffn1_w_down"].shape[1], a["w_branch_a"].shape[1]
    grads = {n: fcol[:, i * d:(i + 1) * d] for i, n in enumerate(COL_NAMES)}
    grads.update({n: fdown[:, i * n_ff:(i + 1) * n_ff] for i, n in enumerate(DOWN_NAMES)})
    grads.update(w_out=fwo, w_in=fwin, mla_w_uq=fuq, mla_w_ukv=fukv, w_branch_a=fbranch[:, :r_branch], w_branch_b=fbranch[:, r_branch:])

    flat = jnp.concatenate([jnp.concatenate([sgrads[l][n].reshape(-1) for l in range(depth)]) for n in SMALL_NAMES] + [loss.reshape(-1)])
    n_small = flat.shape[0]
    rows = -(-n_small // (8 * LANES)) * 8
    pad = rows * LANES - n_small

    def small_pack(v):
        return jnp.pad(v, (0, pad)).reshape(rows, LANES)

    total = small_allreduce(small_pack(flat))
    w_s, m_s, v_s = (small_pack(jnp.concatenate([a[p + n].reshape(-1) for n in SMALL_NAMES] + [jnp.zeros((1,), F32)]))
                     for p in ("", "m_", "v_"))
    d_s, nm_s, nv_s = adamw("adamw_small", w_s, total, m_s, v_s)

    def small_unpack(buf):
        out, off, flat_b = {}, 0, buf.reshape(-1)
        for n in SMALL_NAMES:
            size = a[n].shape[0] * a[n].shape[1]
            out[n] = flat_b[off:off + size].reshape(a[n].shape)
            off += size
        return out

    grads.update(small_unpack(total))
    delta, new_m, new_v = small_unpack(d_s), small_unpack(nm_s), small_unpack(nv_s)
    for n in PACK_NAMES:
        shp = a[n].shape
        two_d = (shp[0] * shp[1], shp[2])
        dn, mn, vn = adamw("adamw_" + n, a[n].reshape(two_d), grads[n].reshape(two_d), a["m_" + n].reshape(two_d), a["v_" + n].reshape(two_d))
        delta[n], new_m[n], new_v[n] = dn.reshape(shp), mn.reshape(shp), vn.reshape(shp)

    loss_out = total.reshape(-1)[n_small - 1]
    return (loss_out, grad_x, *[grads[n] for n in WEIGHT_NAMES], *[delta[n] for n in WEIGHT_NAMES],
            *[new_m[n] for n in WEIGHT_NAMES], *[new_v[n] for n in WEIGHT_NAMES])
```

```python
import functools

import numpy as np
import jax
import jax.numpy as jnp
from jax import lax
from jax.experimental import pallas as pl
from jax.experimental.pallas import tpu as pltpu

F32 = jnp.float32
BF16 = jnp.bfloat16
MESH = pl.DeviceIdType.MESH

HEAD_DIM_A = 64
N_HEADS_A = 8
N_KV_HEADS_A = 2
GROUP_A = N_HEADS_A // N_KV_HEADS_A
BLOCK = 128
N_HEADS_B = 8
Q_LORA_RANK = 256
KV_LORA_RANK = 128
QK_NOPE_DIM = 64
QK_ROPE_DIM = 32
QK_DIM_B = QK_NOPE_DIM + QK_ROPE_DIM
V_DIM_B = 64
ROPE_BASE = 10000.0
WIDTH_A = N_HEADS_A * HEAD_DIM_A
WIDTH_B = N_HEADS_B * V_DIM_B
KV_WIDTH_A = N_KV_HEADS_A * HEAD_DIM_A
EPS = 1e-6
NEG = -1e30
ADAM_LR = 0.001
ADAM_B1 = 0.9
ADAM_B2 = 0.999
ADAM_EPS = 1e-08
ADAM_WD = 0.01
ADAM_STEP = 10

N_SHARDS = 4
N_DEV = 8
LANES = 128
VMEM_LIMIT = 48 * 1024 * 1024

PACK_NAMES = ("ffn1_w_gate", "ffn1_w_up", "ffn1_w_down", "w_in", "mla_w_uq", "mla_w_ukv",
              "w_branch_a", "w_branch_b", "w_out", "ffn2_w_gate", "ffn2_w_up", "ffn2_w_down")
SMALL_NAMES = ("ffn1_norm", "mix_norm", "swa_q_norm", "swa_k_norm", "swa_sinks", "mla_q_lora_norm",
               "mla_kv_lora_norm", "mla_q_norm", "mla_k_norm", "ffn2_norm")
WEIGHT_NAMES = ("ffn1_norm", "ffn1_w_gate", "ffn1_w_up", "ffn1_w_down", "mix_norm", "w_in", "swa_q_norm",
                "swa_k_norm", "swa_sinks", "mla_q_lora_norm", "mla_w_uq", "mla_kv_lora_norm", "mla_w_ukv",
                "mla_q_norm", "mla_k_norm", "w_branch_a", "w_branch_b", "w_out", "ffn2_norm", "ffn2_w_gate",
                "ffn2_w_up", "ffn2_w_down")


def _params(sem):
    return pltpu.CompilerParams(dimension_semantics=sem, vmem_limit_bytes=VMEM_LIMIT)


def _tile(n, want, align):
    if n <= want:
        return n
    t = (want // align) * align
    while t > align and n % t:
        t -= align
    assert t >= align and n % t == 0, (n, want, align)
    return t


def _mm_call(name, a, b, out_struct, grid, a_spec, b_spec, o_spec, dims, n_red, acc_shape, alpha=1.0, res=None, into=None,
             inner=0, a_cols=0, plan=None):
    if n_red and all(g == 1 for g in grid[len(grid) - n_red:]):
        n_red = 0
    n_par = len(grid) - n_red
    n_in = 2 + int(res is not None) + int(into is not None)
    p_ins, p_in_specs, p_outs, p_out_specs, p_scratch, semantics, at_start, at_end = _hosted(
        plan, n_in, 1, grid, ("parallel",) * n_par + ("arbitrary",) * n_red)

    def body(*refs):
        at_start(refs)
        compute(refs)
        at_end(refs)

    def compute(refs):
        a_ref, b_ref = refs[:2]
        r_ref = refs[2] if res is not None else None
        o_ref, acc_ref = refs[n_in + len(p_ins)], refs[n_in + len(p_ins) + 1 + len(p_outs)]
        if inner:
            def a_of(s):
                return a_ref[:, s * a_cols:(s + 1) * a_cols] if a_cols else a_ref[s]

            part = lax.dot_general(a_of(0), b_ref[0], dims, preferred_element_type=F32)
            for s in range(1, inner):
                part = part + lax.dot_general(a_of(s), b_ref[s], dims, preferred_element_type=F32)
        else:
            part = lax.dot_general(a_ref[...], b_ref[...], dims, preferred_element_type=F32)

        def finish(total):
            if alpha != 1.0:
                total = total * alpha
            if r_ref is not None:
                total = r_ref[...] + total
            o_ref[...] = total.astype(o_ref.dtype)

        if n_red == 0:
            finish(part)
            return
        ids = [pl.program_id(n_par + i) for i in range(n_red)]
        first = functools.reduce(jnp.logical_and, [i == 0 for i in ids])
        last = functools.reduce(jnp.logical_and, [i == grid[n_par + k] - 1 for k, i in enumerate(ids)])

        @pl.when(first)
        def _():
            acc_ref[...] = part

        @pl.when(jnp.logical_not(first))
        def _():
            acc_ref[...] += part

        @pl.when(last)
        def _():
            finish(acc_ref[...])

    in_specs = [a_spec, b_spec] + ([o_spec] if res is not None else [])
    args = (a, b) + ((res,) if res is not None else ())
    aliases = {}
    if into is not None:
        aliases = {len(args): 0}
        in_specs.append(pl.BlockSpec(memory_space=pl.ANY))
        args = args + (into,)
    if plan is None:
        return pl.pallas_call(
            body, name=name, out_shape=out_struct, grid=grid, in_specs=in_specs, out_specs=o_spec,
            scratch_shapes=[pltpu.VMEM(acc_shape, F32)], input_output_aliases=aliases, compiler_params=_params(semantics),
        )(*args)
    res_all = pl.pallas_call(
        body, name=name, out_shape=(out_struct,) + tuple(p_outs), grid=grid, in_specs=in_specs + p_in_specs,
        out_specs=(o_spec,) + tuple(p_out_specs), scratch_shapes=[pltpu.VMEM(acc_shape, F32)] + p_scratch,
        input_output_aliases=aliases, compiler_params=_params(semantics),
    )(*args, *p_ins)
    return res_all[0], list(res_all[1:])


_NN = (((1,), (0,)), ((), ()))
_NT = (((1,), (1,)), ((), ()))
_TN = (((0,), (0,)), ((), ()))


def mm_nn(name, a, b, tm=512, tn=1024, tk=1024, out_dtype=F32, alpha=1.0, res=None, plan=None):
    (m, k), (_, n) = a.shape, b.shape
    tm, tn, tk = _tile(m, tm, 16), _tile(n, tn, LANES), _tile(k, tk, LANES)
    return _mm_call(name, a, b, jax.ShapeDtypeStruct((m, n), out_dtype), (m // tm, n // tn, k // tk),
                    pl.BlockSpec((tm, tk), lambda i, j, kk: (i, kk)), pl.BlockSpec((tk, tn), lambda i, j, kk: (kk, j)),
                    pl.BlockSpec((tm, tn), lambda i, j, kk: (i, j)), _NN, 1, (tm, tn), alpha, res, plan=plan)


def mm_nt(name, a, b, tm=512, tn=1024, tk=1024, out_dtype=F32, alpha=1.0, res=None):
    (m, n), (k, _) = a.shape, b.shape
    tm, tn, tk = _tile(m, tm, 16), _tile(k, tn, LANES), _tile(n, tk, LANES)
    return _mm_call(name, a, b, jax.ShapeDtypeStruct((m, k), out_dtype), (m // tm, k // tn, n // tk),
                    pl.BlockSpec((tm, tk), lambda i, j, kk: (i, kk)), pl.BlockSpec((tn, tk), lambda i, j, kk: (j, kk)),
                    pl.BlockSpec((tm, tn), lambda i, j, kk: (i, j)), _NT, 1, (tm, tn), alpha, res)


def mm_tn(name, a, b, tm=1024, tn=1024, tk=1024, out_dtype=F32, alpha=1.0):
    (m, k), (_, n) = a.shape, b.shape
    tm, tn, tk = _tile(k, tm, LANES), _tile(n, tn, LANES), _tile(m, tk, 16)
    return _mm_call(name, a, b, jax.ShapeDtypeStruct((k, n), out_dtype), (k // tm, n // tn, m // tk),
                    pl.BlockSpec((tk, tm), lambda i, j, kk: (kk, i)), pl.BlockSpec((tk, tn), lambda i, j, kk: (kk, j)),
                    pl.BlockSpec((tm, tn), lambda i, j, kk: (i, j)), _TN, 1, (tm, tn), alpha)


def ffn_up_act(name, a, w, blk, tm=512, plan=None):
    (m, k), (ns, _, n) = a.shape, w.shape
    tm = _tile(m, tm, 16)
    grid = (ns, m // tm)
    p_ins, p_in_specs, p_outs, p_out_specs, p_scratch, semantics, at_start, at_end = _hosted(plan, 3, 2, grid, ("parallel", "parallel"))

    def body(*refs):
        a_ref, wg_ref, wu_ref = refs[:3]
        gu_ref, h_ref = refs[3 + len(p_ins):5 + len(p_ins)]
        at_start(refs)
        av = a_ref[...]
        gate = jnp.dot(av, wg_ref[...], preferred_element_type=F32)
        up = jnp.dot(av, wu_ref[...], preferred_element_type=F32)
        gu_ref[0] = gate.astype(BF16)
        gu_ref[1] = up.astype(BF16)
        h_ref[...] = (gate * jax.nn.sigmoid(gate) * up).astype(BF16)
        at_end(refs)

    res = pl.pallas_call(
        body, name=name, grid=grid,
        out_shape=(jax.ShapeDtypeStruct((2, ns, m, n), BF16), jax.ShapeDtypeStruct((ns, m, n), BF16)) + tuple(p_outs),
        in_specs=[pl.BlockSpec((tm, k), lambda s, i: (i, 0)), pl.BlockSpec((None, k, n), lambda s, i: (s, blk, 0)),
                  pl.BlockSpec((None, k, n), lambda s, i: (s, blk + 1, 0))] + p_in_specs,
        out_specs=(pl.BlockSpec((2, None, tm, n), lambda s, i: (0, s, i, 0)), pl.BlockSpec((None, tm, n), lambda s, i: (s, i, 0)))
        + tuple(p_out_specs),
        scratch_shapes=p_scratch, compiler_params=_params(semantics),
    )(a, w, w, *p_ins)
    return res[0], res[1], list(res[2:])


def ffn_down_dact(name, a, w, gu, n, blk, alpha, tm=512, plan=None):
    (m, d), ns = a.shape, w.shape[0]
    tm = _tile(m, tm, 16)
    grid = (ns, m // tm)
    p_ins, p_in_specs, p_outs, p_out_specs, p_scratch, semantics, at_start, at_end = _hosted(plan, 3, 1, grid, ("parallel", "parallel"))

    def body(*refs):
        a_ref, w_ref, gu_ref = refs[:3]
        o_ref = refs[3 + len(p_ins)]
        at_start(refs)
        dh = lax.dot_general(a_ref[...], w_ref[...], _NT, preferred_element_type=F32) * alpha
        gate, up = gu_ref[0].astype(F32), gu_ref[1].astype(F32)
        s = jax.nn.sigmoid(gate)
        o_ref[0] = (dh * up * (s * (1.0 + gate * (1.0 - s)))).astype(BF16)
        o_ref[1] = (dh * (gate * s)).astype(BF16)
        at_end(refs)

    gu_spec = pl.BlockSpec((2, None, tm, n), lambda s, i: (0, s, i, 0))
    res = pl.pallas_call(
        body, name=name, grid=grid, out_shape=(jax.ShapeDtypeStruct((2, ns, m, n), BF16),) + tuple(p_outs),
        in_specs=[pl.BlockSpec((tm, d), lambda s, i: (i, 0)), pl.BlockSpec((None, n, d), lambda s, i: (s, blk, 0)), gu_spec] + p_in_specs,
        out_specs=(gu_spec,) + tuple(p_out_specs), scratch_shapes=p_scratch, compiler_params=_params(semantics),
    )(a, w, gu, *p_ins)
    return res[0], list(res[1:])


def gmm_up_dw(name, a, dgu, tk=2048, plan=None):
    (m, k), (_, ns, _, n) = a.shape, dgu.shape
    tk = _tile(m, tk, 16)
    return _mm_call(name, a, dgu, jax.ShapeDtypeStruct((ns, 2 * k, n), BF16), (2, ns, m // tk),
                    pl.BlockSpec((tk, k), lambda j, s, kk: (kk, 0)), pl.BlockSpec((None, None, tk, n), lambda j, s, kk: (j, s, kk, 0)),
                    pl.BlockSpec((None, k, n), lambda j, s, kk: (s, j, 0)), _TN, 1, (k, n), plan=plan)


def gmm_up_dx(name, dgu, w, k, blk, tm=512):
    _, ns, m, n = dgu.shape
    tm = _tile(m, tm, 16)
    return _mm_call(name, dgu, w, jax.ShapeDtypeStruct((m, k), F32), (m // tm, 2),
                    pl.BlockSpec((None, ns, tm, n), lambda i, j: (j, 0, i, 0)),
                    pl.BlockSpec((ns, k, n), lambda i, j: (0, blk + j, 0)),
                    pl.BlockSpec((tm, k), lambda i, j: (i, 0)), _NT, 1, (tm, k), inner=ns)


def gmm_down(name, h, w, blk, res, alpha, tm=512):
    (ns, m, n), d = h.shape, w.shape[2]
    tm = _tile(m, tm, 16)
    return _mm_call(name, h, w, jax.ShapeDtypeStruct((m, d), F32), (m // tm,),
                    pl.BlockSpec((ns, tm, n), lambda i: (0, i, 0)), pl.BlockSpec((ns, n, d), lambda i: (0, blk, 0)),
                    pl.BlockSpec((tm, d), lambda i: (i, 0)), _NN, 0, (8, LANES), alpha, res, inner=ns)


def gmm_down_dw(name, h, b, alpha, tk=2048):
    (ns, m, n), d = h.shape, b.shape[1]
    tk = _tile(m, tk, 16)
    return _mm_call(name, h, b, jax.ShapeDtypeStruct((ns, n, d), BF16), (ns, m // tk),
                    pl.BlockSpec((None, tk, n), lambda s, kk: (s, kk, 0)), pl.BlockSpec((tk, d), lambda s, kk: (kk, 0)),
                    pl.BlockSpec((None, n, d), lambda s, kk: (s, 0, 0)), _TN, 1, (n, d), alpha)


def gmm_rows(name, a, w, res, tm=512):
    (m, _), (ns, r, d) = a.shape, w.shape
    tm = _tile(m, tm, 16)
    return _mm_call(name, a, w, jax.ShapeDtypeStruct((m, d), F32), (m // tm,),
                    pl.BlockSpec((tm, ns * r), lambda i: (i, 0)), pl.BlockSpec((ns, r, d), lambda i: (0, 0, 0)),
                    pl.BlockSpec((tm, d), lambda i: (i, 0)), _NN, 0, (8, LANES), 1.0, res, inner=ns, a_cols=r)


def gmm_rows_dx(name, a, w, tm=512):
    (m, d), (ns, r, _) = a.shape, w.shape
    tm = _tile(m, tm, 16)
    return _mm_call(name, a, w, jax.ShapeDtypeStruct((m, ns * r), F32), (ns, m // tm),
                    pl.BlockSpec((tm, d), lambda s, i: (i, 0)), pl.BlockSpec((None, r, d), lambda s, i: (s, 0, 0)),
                    pl.BlockSpec((tm, r), lambda s, i: (i, s)), _NT, 0, (8, LANES))


def gmm_rows_dw(name, a, b, tk=2048):
    (m, da), d = a.shape, b.shape[1]
    ns, r = N_SHARDS, da // N_SHARDS
    tk = _tile(m, tk, 16)
    return _mm_call(name, a, b, jax.ShapeDtypeStruct((ns, r, d), BF16), (ns, m // tk),
                    pl.BlockSpec((tk, r), lambda s, kk: (kk, s)), pl.BlockSpec((tk, d), lambda s, kk: (kk, 0)),
                    pl.BlockSpec((None, r, d), lambda s, kk: (s, 0, 0)), _TN, 1, (r, d))


def rms_fwd(name, x, gain, tm=512, col_blk=0):
    m, d = x.shape[0], gain.shape[1]
    tm = _tile(m, tm, 16)

    def body(x_ref, g_ref, o_ref):
        xv = x_ref[...]
        r = lax.rsqrt(jnp.mean(xv * xv, axis=-1, keepdims=True) + EPS)
        o_ref[...] = (xv * r * g_ref[...]).astype(o_ref.dtype)

    return pl.pallas_call(
        body, name=name, out_shape=jax.ShapeDtypeStruct((m, d), BF16), grid=(m // tm,),
        in_specs=[pl.BlockSpec((tm, d), lambda i: (i, col_blk)), pl.BlockSpec((1, d), lambda i: (0, 0))],
        out_specs=pl.BlockSpec((tm, d), lambda i: (i, 0)), compiler_params=_params(("parallel",)),
    )(x, gain)


def rms_bwd(name, x, gain, dn, dres=None, want_f32=True, want_bf16=True, tm=512, col_blk=0):
    m, d = x.shape[0], gain.shape[1]
    tm = _tile(m, tm, 16)
    n_out = int(want_f32) + int(want_bf16)

    def body(*refs):
        x_ref, g_ref, dn_ref = refs[:3]
        pos = 3
        r_ref = None
        if dres is not None:
            r_ref = refs[pos]
            pos += 1
        outs = refs[pos:pos + n_out]
        dg_ref = refs[pos + n_out]
        xv = x_ref[...]
        r = lax.rsqrt(jnp.mean(xv * xv, axis=-1, keepdims=True) + EPS)
        xhat = xv * r
        dnv = dn_ref[...]
        dxhat = dnv * g_ref[...]
        dx = r * (dxhat - xhat * jnp.mean(dxhat * xhat, axis=-1, keepdims=True))
        if r_ref is not None:
            dx = r_ref[...] + dx
        for o in outs:
            o[...] = dx.astype(o.dtype)
        part = jnp.sum(dnv * xhat, axis=0, keepdims=True)

        @pl.when(pl.program_id(0) == 0)
        def _():
            dg_ref[...] = part

        @pl.when(pl.program_id(0) > 0)
        def _():
            dg_ref[...] += part

    row = pl.BlockSpec((tm, d), lambda i: (i, 0))
    vec = pl.BlockSpec((1, d), lambda i: (0, 0))
    out_shape = ([jax.ShapeDtypeStruct((m, d), F32)] if want_f32 else []) + ([jax.ShapeDtypeStruct((m, d), BF16)] if want_bf16 else [])
    res = pl.pallas_call(
        body, name=name, out_shape=tuple(out_shape) + (jax.ShapeDtypeStruct((1, d), F32),), grid=(m // tm,),
        in_specs=[pl.BlockSpec((tm, d), lambda i: (i, col_blk)), vec, row] + ([row] if dres is not None else []),
        out_specs=tuple([row] * n_out) + (vec,), compiler_params=_params(("arbitrary",)),
    )(*((x, gain, dn) + ((dres,) if dres is not None else ())))
    return res


def gate_fwd(name, proj, ya, yb, off_a, off_b, tm=256):
    m, d = ya.shape
    w = proj.shape[1]
    tm = _tile(m, tm, 16)

    def body(p_ref, ya_ref, yb_ref, o_ref):
        ga, gb = p_ref[:, off_a:off_a + d], p_ref[:, off_b:off_b + d]
        o_ref[...] = (jax.nn.sigmoid(ga) * ya_ref[...] + jax.nn.sigmoid(gb) * yb_ref[...]).astype(o_ref.dtype)

    row = pl.BlockSpec((tm, d), lambda i: (i, 0))
    return pl.pallas_call(
        body, name=name, out_shape=jax.ShapeDtypeStruct((m, d), BF16), grid=(m // tm,),
        in_specs=[pl.BlockSpec((tm, w), lambda i: (i, 0)), row, row], out_specs=row, compiler_params=_params(("parallel",)),
    )(proj, ya, yb)


def gate_bwd(name, proj, ya, yb, dmerged, off_a, off_b, tm=256):
    m, d = ya.shape
    w = proj.shape[1]
    tm = _tile(m, tm, 16)

    def body(p_ref, ya_ref, yb_ref, dm_ref, dya_ref, dyb_ref, dga_ref, dgb_ref):
        sa, sb = jax.nn.sigmoid(p_ref[:, off_a:off_a + d]), jax.nn.sigmoid(p_ref[:, off_b:off_b + d])
        dm = dm_ref[...]
        dya_ref[...] = (dm * sa).astype(BF16)
        dyb_ref[...] = (dm * sb).astype(BF16)
        dga_ref[...] = (dm * ya_ref[...] * (sa * (1.0 - sa))).astype(BF16)
        dgb_ref[...] = (dm * yb_ref[...] * (sb * (1.0 - sb))).astype(BF16)

    row = pl.BlockSpec((tm, d), lambda i: (i, 0))
    o = jax.ShapeDtypeStruct((m, d), BF16)
    return pl.pallas_call(
        body, name=name, out_shape=(o, o, o, o), grid=(m // tm,),
        in_specs=[pl.BlockSpec((tm, w), lambda i: (i, 0)), row, row, row], out_specs=(row, row, row, row),
        compiler_params=_params(("parallel",)),
    )(proj, ya, yb, dmerged)


def loss_fwd_bwd(name, y, target, tm=512):
    m, d = y.shape
    tm = _tile(m, tm, 16)

    def body(y_ref, t_ref, l_ref, dy_ref, dyb_ref):
        err = y_ref[...] - t_ref[...]
        dy = err * (1.0 / d)
        dy_ref[...] = dy
        dyb_ref[...] = dy.astype(BF16)
        part = 0.5 * jnp.sum(jnp.mean(err * err, axis=-1, keepdims=True), axis=0, keepdims=True)

        @pl.when(pl.program_id(0) == 0)
        def _():
            l_ref[...] = part

        @pl.when(pl.program_id(0) > 0)
        def _():
            l_ref[...] += part

    row = pl.BlockSpec((tm, d), lambda i: (i, 0))
    return pl.pallas_call(
        body, name=name, grid=(m // tm,),
        out_shape=(jax.ShapeDtypeStruct((1, 1), F32), jax.ShapeDtypeStruct((m, d), F32), jax.ShapeDtypeStruct((m, d), BF16)),
        in_specs=[row, row], out_specs=(pl.BlockSpec((1, 1), lambda i: (0, 0)), row, row),
        compiler_params=_params(("arbitrary",)),
    )(y, target)


def adamw(name, w, g, m, v):
    r, c = w.shape
    tr = _tile(r, max(8, (2 * 1024 * 1024) // (4 * c) // 8 * 8), 8)
    c1 = np.float32(1.0 - ADAM_B1 ** ADAM_STEP)
    c2 = np.float32(1.0 - ADAM_B2 ** ADAM_STEP)

    def body(w_ref, g_ref, m_ref, v_ref, d_ref, nm_ref, nv_ref):
        gv = g_ref[...]
        nm = ADAM_B1 * m_ref[...] + (1.0 - ADAM_B1) * gv
        nv = ADAM_B2 * v_ref[...] + (1.0 - ADAM_B2) * (gv * gv)
        d_ref[...] = -ADAM_LR * ((nm / c1) / (jnp.sqrt(nv / c2) + ADAM_EPS) + ADAM_WD * w_ref[...])
        nm_ref[...] = nm
        nv_ref[...] = nv

    row = pl.BlockSpec((tr, c), lambda i: (i, 0))
    o = jax.ShapeDtypeStruct((r, c), F32)
    return pl.pallas_call(
        body, name=name, out_shape=(o, o, o), grid=(r // tr,), in_specs=[row] * 4, out_specs=(row, row, row),
        compiler_params=_params(("parallel",)),
    )(w, g, m, v)


HEAD_PAD = LANES


def _rope_rot():
    r = np.zeros((HEAD_PAD, HEAD_PAD), np.float32)
    half = QK_ROPE_DIM // 2
    for j in range(half):
        r[QK_NOPE_DIM + half + j, QK_NOPE_DIM + j] = -1.0
        r[QK_NOPE_DIM + j, QK_NOPE_DIM + half + j] = 1.0
    return r


def _head_spec(s, blk0):
    return pl.BlockSpec((None, s, HEAD_PAD), lambda bi, hi: (bi, 0, blk0 + hi))


def qk_prep_fwd(name, x, blk0, n_heads, d_real, gain, rope=None, extra=None):
    b, s, _ = x.shape

    def body(*refs):
        x_ref, g_ref = refs[0], refs[1]
        pos = 2
        xv = x_ref[...]
        if extra is not None:
            xv = xv + refs[pos][...]
            pos += 1
        y = xv * lax.rsqrt(jnp.sum(xv * xv, axis=-1, keepdims=True) * (1.0 / d_real) + EPS) * g_ref[...]
        if rope is not None:
            c_ref, s_ref, r_ref = refs[pos:pos + 3]
            rot = jnp.dot(y, r_ref[...], precision=lax.Precision.HIGHEST, preferred_element_type=F32)
            y = y * c_ref[...] + rot * s_ref[...]
        refs[-1][...] = y.astype(BF16)

    vec = pl.BlockSpec((1, HEAD_PAD), lambda bi, hi: (0, 0))
    tab = pl.BlockSpec((None, s, HEAD_PAD), lambda bi, hi: (bi, 0, 0))
    in_specs, args = [_head_spec(s, blk0), vec], [x, gain]
    if extra is not None:
        e_blk = extra[1]
        in_specs.append(pl.BlockSpec((None, s, HEAD_PAD), lambda bi, hi: (bi, 0, e_blk)))
        args.append(extra[0])
    if rope is not None:
        in_specs += [tab, tab, pl.BlockSpec((HEAD_PAD, HEAD_PAD), lambda bi, hi: (0, 0))]
        args += [rope[0], rope[1], jnp.asarray(_rope_rot())]
    return pl.pallas_call(
        body, name=name, out_shape=jax.ShapeDtypeStruct((b, s, n_heads * HEAD_PAD), BF16), grid=(b, n_heads),
        in_specs=in_specs, out_specs=_head_spec(s, 0), compiler_params=_params(("parallel", "parallel")),
    )(*args)


def qk_prep_bwd(name, x, blk0, n_heads, d_real, gain, dxh, rope=None, extra=None, head_sum=False):
    b, s, _ = x.shape

    def body(*refs):
        x_ref, g_ref, dy_ref = refs[:3]
        pos = 3
        xv = x_ref[...]
        if extra is not None:
            xv = xv + refs[pos][...]
            pos += 1
        dy = dy_ref[...]
        if rope is not None:
            c_ref, s_ref, rt_ref = refs[pos:pos + 3]
            pos += 3
            dy = dy * c_ref[...] + jnp.dot(dy * s_ref[...], rt_ref[...], precision=lax.Precision.HIGHEST, preferred_element_type=F32)
        outs = refs[pos:]
        dx_ref, dg_ref = outs[0], outs[-1]
        r = lax.rsqrt(jnp.sum(xv * xv, axis=-1, keepdims=True) * (1.0 / d_real) + EPS)
        xhat = xv * r
        dxhat = dy * g_ref[...]
        dx = r * (dxhat - xhat * (jnp.sum(dxhat * xhat, axis=-1, keepdims=True) * (1.0 / d_real)))
        dx_ref[...] = dx.astype(dx_ref.dtype)
        part = jnp.sum(dy * xhat, axis=0, keepdims=True)
        first = jnp.logical_and(pl.program_id(0) == 0, pl.program_id(1) == 0)

        @pl.when(first)
        def _():
            dg_ref[...] = part

        @pl.when(jnp.logical_not(first))
        def _():
            dg_ref[...] += part

        if head_sum:
            hs_ref = outs[1]

            @pl.when(pl.program_id(1) == 0)
            def _():
                hs_ref[...] = dx

            @pl.when(pl.program_id(1) > 0)
            def _():
                hs_ref[...] += dx

    vec = pl.BlockSpec((1, HEAD_PAD), lambda bi, hi: (0, 0))
    tab = pl.BlockSpec((None, s, HEAD_PAD), lambda bi, hi: (bi, 0, 0))
    in_specs, args = [_head_spec(s, blk0), vec, _head_spec(s, 0)], [x, gain, dxh]
    if extra is not None:
        e_blk = extra[1]
        in_specs.append(pl.BlockSpec((None, s, HEAD_PAD), lambda bi, hi: (bi, 0, e_blk)))
        args.append(extra[0])
    if rope is not None:
        in_specs += [tab, tab, pl.BlockSpec((HEAD_PAD, HEAD_PAD), lambda bi, hi: (0, 0))]
        args += [rope[0], rope[1], jnp.asarray(_rope_rot().T.copy())]
    out_shape, out_specs = [jax.ShapeDtypeStruct((b, s, n_heads * HEAD_PAD), BF16)], [_head_spec(s, 0)]
    if head_sum:
        out_shape.append(jax.ShapeDtypeStruct((b, s, HEAD_PAD), F32))
        out_specs.append(tab)
    out_shape.append(jax.ShapeDtypeStruct((1, HEAD_PAD), F32))
    out_specs.append(vec)
    return pl.pallas_call(
        body, name=name, out_shape=tuple(out_shape), grid=(b, n_heads), in_specs=in_specs, out_specs=tuple(out_specs),
        compiler_params=_params(("arbitrary", "arbitrary")),
    )(*args)


def _swa_specs(v_blk0):
    q = pl.BlockSpec((None, BLOCK, GROUP_A * HEAD_PAD), lambda b, kv, n: (b, n, kv))
    kprev = pl.BlockSpec((None, BLOCK, HEAD_PAD), lambda b, kv, n: (b, jnp.maximum(n - 1, 0), kv))
    kcur = pl.BlockSpec((None, BLOCK, HEAD_PAD), lambda b, kv, n: (b, n, kv))
    vprev = pl.BlockSpec((None, BLOCK, HEAD_PAD), lambda b, kv, n: (b, jnp.maximum(n - 1, 0), v_blk0 + kv))
    vcur = pl.BlockSpec((None, BLOCK, HEAD_PAD), lambda b, kv, n: (b, n, v_blk0 + kv))
    pcol = pl.BlockSpec((None, BLOCK, 1), lambda b, kv, n: (b, n, 0))
    prow_prev = pl.BlockSpec((None, 1, BLOCK), lambda b, kv, n: (b, 0, jnp.maximum(n - 1, 0)))
    prow_cur = pl.BlockSpec((None, 1, BLOCK), lambda b, kv, n: (b, 0, n))
    smem = pl.BlockSpec(memory_space=pltpu.SMEM)
    return q, kprev, kcur, vprev, vcur, pcol, prow_prev, prow_cur, smem


def _swa_probs(q, kk, dist, valid, slope, sink):
    sc = lax.dot_general(q, kk, _NT, preferred_element_type=F32) * (HEAD_DIM_A ** -0.5)
    sc = sc - slope * dist
    sc = jnp.where(valid, sc, NEG)
    m = jnp.maximum(jnp.max(sc, axis=-1, keepdims=True), sink)
    e = jnp.exp(sc - m)
    es = jnp.exp(sink - m)
    inv = 1.0 / (jnp.sum(e, axis=-1, keepdims=True) + es)
    return e * inv, es * inv


def _swa_window(n, kp_ref, kc_ref, vp_ref, vc_ref, pc_ref, prp_ref, prc_ref):
    kk = jnp.concatenate([kp_ref[...], kc_ref[...]], axis=0)
    vv = jnp.concatenate([vp_ref[...], vc_ref[...]], axis=0).astype(BF16)
    dist = pc_ref[...] - jnp.concatenate([prp_ref[...], prc_ref[...]], axis=1)
    qi = lax.broadcasted_iota(jnp.int32, (BLOCK, 2 * BLOCK), 0) + BLOCK
    ki = lax.broadcasted_iota(jnp.int32, (BLOCK, 2 * BLOCK), 1)
    diff = qi - ki
    valid = (diff >= 0) & (diff < BLOCK) & ((n > 0) | (ki >= BLOCK))
    return kk, vv, dist, valid


def swa_fwd(name, q, k, vsrc, v_blk0, pos_col, pos_row, slopes, sinks, plan=None):
    b, s, _ = q.shape
    qs, kprev, kcur, vprev, vcur, pcol, prp, prc, smem = _swa_specs(v_blk0)
    grid = (b, N_KV_HEADS_A, s // BLOCK)
    p_ins, p_in_specs, p_outs, p_out_specs, p_scratch, semantics, at_start, at_end = _hosted(plan, 10, 1, grid, ("parallel",) * 3)

    def body(*refs):
        q_ref, kp_ref, kc_ref, vp_ref, vc_ref, pc_ref, prp_ref, prc_ref, sl_ref, sk_ref = refs[:10]
        o_ref = refs[10 + len(p_ins)]
        at_start(refs)
        kv, n = pl.program_id(1), pl.program_id(2)
        kk, vv, dist, valid = _swa_window(n, kp_ref, kc_ref, vp_ref, vc_ref, pc_ref, prp_ref, prc_ref)
        for g in range(GROUP_A):
            hd = kv * GROUP_A + g
            lanes = slice(g * HEAD_PAD, (g + 1) * HEAD_PAD)
            p, _ = _swa_probs(q_ref[:, lanes], kk, dist, valid, sl_ref[hd], sk_ref[hd])
            o_ref[:, lanes] = jnp.dot(p.astype(BF16), vv, preferred_element_type=F32).astype(BF16)
        at_end(refs)

    res = pl.pallas_call(
        body, name=name, out_shape=(jax.ShapeDtypeStruct(q.shape, BF16),) + tuple(p_outs), grid=grid,
        in_specs=[qs, kprev, kcur, vprev, vcur, pcol, prp, prc, smem, smem] + p_in_specs, out_specs=(qs,) + tuple(p_out_specs),
        scratch_shapes=p_scratch, compiler_params=_params(semantics),
    )(q, k, k, vsrc, vsrc, pos_col, pos_row, pos_row, slopes, sinks, *p_ins)
    return res[0], list(res[1:])


def swa_bwd(name, q, k, vsrc, v_blk0, pos_col, pos_row, slopes, sinks, do, plan=None):
    b, s, _ = q.shape
    qs, kprev, kcur, vprev, vcur, pcol, prp, prc, smem = _swa_specs(v_blk0)
    grid = (b, N_KV_HEADS_A, s // BLOCK)
    p_ins, p_in_specs, p_outs, p_out_specs, p_scratch, semantics, at_start, at_end = _hosted(plan, 11, 4, grid, ("arbitrary",) * 3)

    def body(*refs):
        at_start(refs)
        compute(*refs[:11], *refs[11 + len(p_ins):15 + len(p_ins)])
        at_end(refs)

    def compute(q_ref, kp_ref, kc_ref, vp_ref, vc_ref, pc_ref, prp_ref, prc_ref, sl_ref, sk_ref, do_ref, dq_ref, dk_ref, dv_ref, ds_ref):
        bi, kv, n = pl.program_id(0), pl.program_id(1), pl.program_id(2)
        kk, vv, dist, valid = _swa_window(n, kp_ref, kc_ref, vp_ref, vc_ref, pc_ref, prp_ref, prc_ref)

        @pl.when((bi == 0) & (kv == 0) & (n == 0))
        def _():
            ds_ref[...] = jnp.zeros_like(ds_ref)

        @pl.when(n == 0)
        def _():
            dk_ref[...] = jnp.zeros_like(dk_ref)
            dv_ref[...] = jnp.zeros_like(dv_ref)

        dkk = jnp.zeros((2 * BLOCK, HEAD_PAD), F32)
        dvv = jnp.zeros((2 * BLOCK, HEAD_PAD), F32)
        head_row = lax.broadcasted_iota(jnp.int32, (N_HEADS_A, LANES), 0)
        dsink = jnp.zeros((N_HEADS_A, LANES), F32)
        for g in range(GROUP_A):
            hd = kv * GROUP_A + g
            lanes = slice(g * HEAD_PAD, (g + 1) * HEAD_PAD)
            qg = q_ref[:, lanes]
            p, ps = _swa_probs(qg, kk, dist, valid, sl_ref[hd], sk_ref[hd])
            dob = do_ref[:, lanes].astype(BF16)
            dvv = dvv + lax.dot_general(p.astype(BF16), dob, _TN, preferred_element_type=F32)
            dp = lax.dot_general(dob, vv, _NT, preferred_element_type=F32)
            rs = jnp.sum(p * dp, axis=-1, keepdims=True)
            dsb = (p * (dp - rs) * (HEAD_DIM_A ** -0.5)).astype(BF16)
            dq_ref[:, lanes] = jnp.dot(dsb, kk, preferred_element_type=F32)
            dkk = dkk + lax.dot_general(dsb, qg, _TN, preferred_element_type=F32)
            dsink = dsink + jnp.where(head_row == hd, -jnp.sum(ps * rs), 0.0)
        ds_ref[...] += dsink

        @pl.when(n > 0)
        def _():
            start = pl.multiple_of((n - 1) * BLOCK, BLOCK)
            dk_ref[pl.ds(start, 2 * BLOCK), :] += dkk
            dv_ref[pl.ds(start, 2 * BLOCK), :] += dvv

        @pl.when(n == 0)
        def _():
            dk_ref[0:BLOCK, :] += dkk[BLOCK:, :]
            dv_ref[0:BLOCK, :] += dvv[BLOCK:, :]

    kv_full = pl.BlockSpec((None, s, HEAD_PAD), lambda bi, kv, n: (bi, 0, kv))
    kv_shape = jax.ShapeDtypeStruct(k.shape, F32)
    res = pl.pallas_call(
        body, name=name, grid=grid,
        out_shape=(jax.ShapeDtypeStruct(q.shape, F32), kv_shape, kv_shape, jax.ShapeDtypeStruct((N_HEADS_A, LANES), F32)) + tuple(p_outs),
        in_specs=[qs, kprev, kcur, vprev, vcur, pcol, prp, prc, smem, smem, qs] + p_in_specs,
        out_specs=(qs, kv_full, kv_full, pl.BlockSpec((N_HEADS_A, LANES), lambda bi, kv, n: (0, 0))) + tuple(p_out_specs),
        scratch_shapes=p_scratch, compiler_params=_params(semantics),
    )(q, k, k, vsrc, vsrc, pos_col, pos_row, pos_row, slopes, sinks, do, *p_ins)
    return res[0], res[1], res[2], res[3], list(res[4:])


MLA_T = 512


def _hosted(plan, n_in, n_out, grid, semantics):
    if plan is None:
        return [], [], [], [], [], semantics, (lambda refs: None), (lambda refs: None)
    ni, no = len(plan.ins), len(plan.out_shapes)

    def split(refs):
        return refs[n_in:n_in + ni], refs[n_in + ni + n_out:n_in + ni + n_out + no], refs[-2], refs[-1]

    def at_start(refs):
        @pl.when(functools.reduce(jnp.logical_and, [pl.program_id(ax) == 0 for ax in range(len(grid))]))
        def _():
            plan.start(*split(refs))

    def at_end(refs):
        @pl.when(functools.reduce(jnp.logical_and, [pl.program_id(ax) == g - 1 for ax, g in enumerate(grid)]))
        def _():
            plan.finish(*split(refs))

    return plan.ins, [_HBM] * ni, plan.out_shapes, [_HBM] * no, plan.scratch(), ("arbitrary",) * len(grid), at_start, at_end


def mla_fwd(name, q, k, v, plan=None):
    b, s, w = q.shape
    h = w // HEAD_PAD
    t = MLA_T if s % MLA_T == 0 else BLOCK
    nb = s // t
    scale = QK_DIM_B ** -0.5
    p_ins, p_in_specs, p_outs, p_out_specs, p_scratch, semantics, at_start, at_end = _hosted(
        plan, 3, 3, (b, h), ("parallel", "parallel"))

    def body(*refs):
        q_ref, k_ref, v_ref = refs[:3]
        o_ref, ob_ref, lse_ref = refs[3 + len(p_ins):6 + len(p_ins)]
        at_start(refs)
        causal = lax.broadcasted_iota(jnp.int32, (t, t), 0) <= lax.broadcasted_iota(jnp.int32, (t, t), 1)
        for i in range(nb):
            q_i = q_ref[i * t:(i + 1) * t, :]

            def step(j, carry, q_i=q_i, diagonal=False):
                m, l, acc = carry
                rows = slice(j * t, (j + 1) * t) if diagonal else pl.ds(pl.multiple_of(j * t, t), t)
                st = lax.dot_general(k_ref[rows, :], q_i, _NT, preferred_element_type=F32) * scale
                if diagonal:
                    st = jnp.where(causal, st, NEG)
                m_new = jnp.maximum(m, jnp.max(st, axis=0, keepdims=True))
                a = jnp.exp(m - m_new)
                p = jnp.exp(st - m_new)
                l = a * l + jnp.sum(p, axis=0, keepdims=True)
                acc = a * acc + lax.dot_general(v_ref[rows, :], p.astype(BF16), _TN, preferred_element_type=F32)
                return m_new, l, acc

            carry = (jnp.full((1, t), NEG, F32), jnp.zeros((1, t), F32), jnp.zeros((HEAD_PAD, t), F32))
            if i > 0:
                carry = lax.fori_loop(0, i, step, carry)
            m, l, acc = step(i, carry, diagonal=True)
            o_i = (acc / l).T
            o_ref[i * t:(i + 1) * t, :] = o_i
            ob_ref[i * t:(i + 1) * t, :] = o_i.astype(BF16)
            lse_ref[i] = m + jnp.log(l)
        at_end(refs)

    hs = _head_spec(s, 0)
    ls = pl.BlockSpec((None, None, nb, 1, t), lambda bi, hi: (bi, hi, 0, 0, 0))
    res = pl.pallas_call(
        body, name=name, grid=(b, h),
        out_shape=(jax.ShapeDtypeStruct((b, s, w), F32), jax.ShapeDtypeStruct((b, s, w), BF16),
                   jax.ShapeDtypeStruct((b, h, nb, 1, t), F32)) + tuple(p_outs),
        in_specs=[hs, hs, hs] + p_in_specs, out_specs=(hs, hs, ls) + tuple(p_out_specs),
        scratch_shapes=p_scratch, compiler_params=_params(semantics),
    )(q, k, v, *p_ins)
    return res[0], res[1], res[2], list(res[3:])


def mla_bwd(name, q, k, v, o, do, lse, plan=None):
    b, s, w = q.shape
    h = w // HEAD_PAD
    _, _, nb, _, t = lse.shape
    scale = QK_DIM_B ** -0.5
    p_ins, p_in_specs, p_outs, p_out_specs, p_scratch, semantics, at_start, at_end = _hosted(
        plan, 6, 3, (b, h), ("parallel", "parallel"))

    def body(*refs):
        q_ref, k_ref, v_ref, o_ref, do_ref, lse_ref = refs[:6]
        dq_ref, dk_ref, dv_ref = refs[6 + len(p_ins):9 + len(p_ins)]
        dv_acc = refs[9 + len(p_ins) + len(p_outs)]
        at_start(refs)
        causal = lax.broadcasted_iota(jnp.int32, (t, t), 0) <= lax.broadcasted_iota(jnp.int32, (t, t), 1)
        dk_ref[...] = jnp.zeros_like(dk_ref)
        dv_acc[...] = jnp.zeros_like(dv_acc)
        for i in range(nb):
            q_i = q_ref[i * t:(i + 1) * t, :]
            do_i = do_ref[i * t:(i + 1) * t, :]
            delta = jnp.sum((o_ref[i * t:(i + 1) * t, :] * do_i).T, axis=0, keepdims=True)
            do_b = do_i.astype(BF16)
            lse_i = lse_ref[i]

            def step(j, dqt, q_i=q_i, do_b=do_b, delta=delta, lse_i=lse_i, diagonal=False):
                rows = slice(j * t, (j + 1) * t) if diagonal else pl.ds(pl.multiple_of(j * t, t), t)
                k_j = k_ref[rows, :]
                st = lax.dot_general(k_j, q_i, _NT, preferred_element_type=F32) * scale
                if diagonal:
                    st = jnp.where(causal, st, NEG)
                pt = jnp.exp(st - lse_i)
                dpt = lax.dot_general(v_ref[rows, :], do_b, _NT, preferred_element_type=F32)
                dst = (pt * (dpt - delta) * scale).astype(BF16)
                dv_acc[rows, :] += jnp.dot(pt.astype(BF16), do_b, preferred_element_type=F32)
                dk_ref[rows, :] += jnp.dot(dst, q_i, preferred_element_type=F32)
                return dqt + lax.dot_general(k_j, dst, _TN, preferred_element_type=F32)

            dqt = jnp.zeros((HEAD_PAD, t), F32)
            if i > 0:
                dqt = lax.fori_loop(0, i, step, dqt)
            dq_ref[i * t:(i + 1) * t, :] = step(i, dqt, diagonal=True).T
        dv_ref[...] = dv_acc[...].astype(BF16)
        at_end(refs)

    hs = _head_spec(s, 0)
    ls = pl.BlockSpec((None, None, nb, 1, t), lambda bi, hi: (bi, hi, 0, 0, 0))
    res = pl.pallas_call(
        body, name=name, grid=(b, h),
        out_shape=(jax.ShapeDtypeStruct((b, s, w), F32), jax.ShapeDtypeStruct((b, s, w), F32),
                   jax.ShapeDtypeStruct((b, s, w), BF16)) + tuple(p_outs),
        in_specs=[hs, hs, hs, hs, hs, ls] + p_in_specs, out_specs=(hs, hs, hs) + tuple(p_out_specs),
        scratch_shapes=[pltpu.VMEM((s, HEAD_PAD), F32)] + p_scratch,
        compiler_params=_params(semantics),
    )(q, k, v, o, do, lse, *p_ins)
    return res[0], res[1], res[2], list(res[3:])


_HBM = pl.BlockSpec(memory_space=pltpu.HBM)


def _place():
    x, y, c = lax.axis_index("x"), lax.axis_index("y"), lax.axis_index("c")
    chips = [(1 - x, y), (x, 1 - y), (1 - x, 1 - y)]
    return x, y, c, chips


def _remote(src, dst, send_sem, recv_sem, dev):
    return pltpu.make_async_remote_copy(src_ref=src, dst_ref=dst, send_sem=send_sem, recv_sem=recv_sem,
                                        device_id=dev, device_id_type=MESH)


class CommPlan:
    def __init__(self, ins, out_shapes, n_sems, start, finish, aliases=None):
        self.ins, self.out_shapes, self.n_sems = list(ins), list(out_shapes), n_sems
        self.start, self.finish, self.aliases = start, finish, dict(aliases or {})

    def scratch(self):
        return [pltpu.SemaphoreType.DMA((self.n_sems,)), pltpu.SemaphoreType.DMA((self.n_sems,))]


def comm_call(name, plan):
    ni, no = len(plan.ins), len(plan.out_shapes)

    def body(*refs):
        ins, outs, (send_sems, recv_sems) = refs[:ni], refs[ni:ni + no], refs[ni + no:]
        plan.start(ins, outs, send_sems, recv_sems)
        plan.finish(ins, outs, send_sems, recv_sems)

    return pl.pallas_call(
        body, name=name, out_shape=tuple(plan.out_shapes), in_specs=[_HBM] * ni, out_specs=tuple([_HBM] * no),
        input_output_aliases=plan.aliases, scratch_shapes=plan.scratch(),
    )(*plan.ins)


def gather_plan(packs, l):
    nk = len(packs)

    def parts(p_refs, o_refs, ss, rs):
        x, y, c, chips = _place()
        s_me = 2 * x + y
        sibling = (x, y, 1 - c)
        first = [_remote(p_refs[k].at[l], o_refs[k].at[s_me], ss.at[j * nk + k], rs.at[j * nk + k], (cx, cy, c))
                 for j, (cx, cy) in enumerate(chips) for k in range(nk)]
        own = [_remote(p_refs[k].at[l], o_refs[k].at[s_me], ss.at[6 * nk + k], rs.at[6 * nk + k], sibling) for k in range(nk)]
        return c, chips, sibling, first, own

    def start(p_refs, o_refs, ss, rs):
        c, _, _, first, own = parts(p_refs, o_refs, ss, rs)
        for cp in own:
            cp.start()

        @pl.when(c == l)
        def _():
            for cp in first:
                cp.start()

    def finish(p_refs, o_refs, ss, rs):
        c, chips, sibling, first, own = parts(p_refs, o_refs, ss, rs)

        @pl.when(c == l)
        def _():
            passed = []
            for j, (cx, cy) in enumerate(chips):
                for k in range(nk):
                    blk = o_refs[k].at[2 * cx + cy]
                    _remote(blk, blk, ss.at[j * nk + k], rs.at[j * nk + k], (cx, cy, c)).wait_recv()
                    fwd = _remote(blk, blk, ss.at[(3 + j) * nk + k], rs.at[(3 + j) * nk + k], sibling)
                    fwd.start()
                    passed.append(fwd)
            for cp in first + passed:
                cp.wait_send()

        @pl.when(c != l)
        def _():
            for j, (cx, cy) in enumerate(chips):
                for k in range(nk):
                    blk = o_refs[k].at[2 * cx + cy]
                    _remote(blk, blk, ss.at[(3 + j) * nk + k], rs.at[(3 + j) * nk + k], sibling).wait_recv()

        for cp in own:
            cp.wait()

    outs = [jax.ShapeDtypeStruct((N_SHARDS,) + p.shape[1:], p.dtype) for p in packs]
    return CommPlan(packs, outs, 7 * nk, start, finish)


def swap_plan(grads, l):
    nk = len(grads)

    def copies(g_refs, o_refs, ss, rs):
        x, y, c, _ = _place()
        return c, [_remote(g_refs[k], o_refs[k], ss.at[k], rs.at[k], (x, y, 1 - c)) for k in range(nk)]

    def start(g_refs, o_refs, ss, rs):
        c, cps = copies(g_refs, o_refs, ss, rs)

        @pl.when(c != l)
        def _():
            for cp in cps:
                cp.start()

    def finish(g_refs, o_refs, ss, rs):
        c, cps = copies(g_refs, o_refs, ss, rs)

        @pl.when(c != l)
        def _():
            for cp in cps:
                cp.wait_send()

        @pl.when(c == l)
        def _():
            for cp in cps:
                cp.wait_recv()

    return CommPlan(grads, [jax.ShapeDtypeStruct(g.shape, g.dtype) for g in grads], nk, start, finish)


def scatter_plan(parts, l):
    nk = len(parts)

    def copies(p_refs, o_refs, ss, rs):
        x, y, c, chips = _place()
        s_me = 2 * x + y
        sends = [_remote(p_refs[k].at[2 * cx + cy], o_refs[k].at[s_me], ss.at[j * nk + k], rs.at[j * nk + k], (cx, cy, c))
                 for j, (cx, cy) in enumerate(chips) for k in range(nk)]
        return c, chips, sends

    def start(p_refs, o_refs, ss, rs):
        c, _, sends = copies(p_refs, o_refs, ss, rs)

        @pl.when(c == l)
        def _():
            for cp in sends:
                cp.start()

    def finish(p_refs, o_refs, ss, rs):
        c, chips, sends = copies(p_refs, o_refs, ss, rs)

        @pl.when(c == l)
        def _():
            for j, (cx, cy) in enumerate(chips):
                for k in range(nk):
                    slot = o_refs[k].at[2 * cx + cy]
                    _remote(slot, slot, ss.at[j * nk + k], rs.at[j * nk + k], (cx, cy, c)).wait_recv()
            for cp in sends:
                cp.wait_send()

    return CommPlan(parts, [jax.ShapeDtypeStruct(p.shape, p.dtype) for p in parts], 3 * nk, start, finish)


def share_plan(finals):
    flat = [(l, f) for l, fs in enumerate(finals) for f in fs]

    def copies(f_refs, o_refs, ss, rs):
        x, y, c, _ = _place()
        return c, [_remote(f_refs[i], o_refs[i], ss.at[i], rs.at[i], (x, y, 1 - c)) for i in range(len(flat))]

    def start(f_refs, o_refs, ss, rs):
        c, cps = copies(f_refs, o_refs, ss, rs)
        for l in range(len(finals)):
            @pl.when(c == l)
            def _(l=l):
                for cp, (lf, _) in zip(cps, flat):
                    if lf == l:
                        cp.start()

    def finish(f_refs, o_refs, ss, rs):
        c, cps = copies(f_refs, o_refs, ss, rs)
        for l in range(len(finals)):
            @pl.when(c == l)
            def _(l=l):
                for cp, (lf, _) in zip(cps, flat):
                    if lf == l:
                        cp.wait_send()

            @pl.when(c != l)
            def _(l=l):
                for cp, (lf, _) in zip(cps, flat):
                    if lf == l:
                        cp.wait_recv()

    arrays = [f for _, f in flat]
    return CommPlan(arrays, [jax.ShapeDtypeStruct(f.shape, f.dtype) for f in arrays], len(flat), start, finish,
                    aliases={i: i for i in range(len(flat))})


def add_pair(name, own, other, tr=256):
    n, cols = own.shape
    tr = _tile(n, tr, 16)

    def body(g_ref, o_ref, out_ref):
        out_ref[...] = (g_ref[...].astype(F32) + o_ref[...].astype(F32)).astype(out_ref.dtype)

    row = pl.BlockSpec((tr, cols), lambda i: (i, 0))
    return pl.pallas_call(
        body, name=name, out_shape=jax.ShapeDtypeStruct((n, cols), BF16), grid=(n // tr,), in_specs=[row, row], out_specs=row,
        compiler_params=_params(("parallel",)),
    )(own, other)


def sum_slots(name, slots, part, shard, tr=256):
    ns, r, cols = slots.shape
    tr = _tile(r, tr, 16)

    def body(s_idx, s_ref, own_ref, o_ref):
        total = own_ref[...].astype(F32)
        for k in range(1, ns):
            total = total + s_ref[(s_idx[0] + k) % ns].astype(F32)
        o_ref[...] = total

    return pl.pallas_call(
        body, name=name, out_shape=jax.ShapeDtypeStruct((r, cols), F32),
        grid_spec=pltpu.PrefetchScalarGridSpec(
            num_scalar_prefetch=1, grid=(r // tr,),
            in_specs=[pl.BlockSpec((ns, tr, cols), lambda i, p: (0, i, 0)), pl.BlockSpec((None, tr, cols), lambda i, p: (p[0], i, 0))],
            out_specs=pl.BlockSpec((tr, cols), lambda i, p: (i, 0))),
        compiler_params=_params(("parallel",)),
    )(shard, slots, part)


def small_allreduce(vec):
    r, cols = vec.shape

    def body(v_ref, o_ref, buf, send_sems, recv_sems):
        x, y, c, _ = _place()
        me = 4 * x + 2 * y + c
        buf[me] = v_ref[...]
        peers = []
        for k in range(1, N_DEV):
            px = (1 - x) if (k & 4) else x
            py = (1 - y) if (k & 2) else y
            pc = (1 - c) if (k & 1) else c
            peers.append((px, py, pc))
        sends = [_remote(buf.at[me], buf.at[me], send_sems.at[k], recv_sems.at[k], peer) for k, peer in enumerate(peers)]
        for cp in sends:
            cp.start()
        for k, (px, py, pc) in enumerate(peers):
            slot = buf.at[4 * px + 2 * py + pc]
            _remote(slot, slot, send_sems.at[k], recv_sems.at[k], (px, py, pc)).wait_recv()
        for cp in sends:
            cp.wait_send()
        total = buf[0]
        for k in range(1, N_DEV):
            total = total + buf[k]
        o_ref[...] = total

    return pl.pallas_call(
        body, name="small_allreduce", out_shape=jax.ShapeDtypeStruct((r, cols), F32),
        in_specs=[pl.BlockSpec(memory_space=pltpu.VMEM)], out_specs=pl.BlockSpec(memory_space=pltpu.VMEM),
        scratch_shapes=[pltpu.VMEM((N_DEV, r, cols), F32), pltpu.SemaphoreType.DMA((N_DEV - 1,)), pltpu.SemaphoreType.DMA((N_DEV - 1,))],
    )(vec)


ARG_NAMES = (("x", "positions") + WEIGHT_NAMES + ("loss_target",) + tuple("m_" + n for n in WEIGHT_NAMES)
             + tuple("v_" + n for n in WEIGHT_NAMES))

_C_KA = N_HEADS_A * HEAD_PAD
_C_VA = _C_KA + N_KV_HEADS_A * HEAD_PAD
_C_CQ = _C_VA + N_KV_HEADS_A * HEAD_PAD
_C_CKV = _C_CQ + Q_LORA_RANK
_C_KR = _C_CKV + KV_LORA_RANK
_C_GA = _C_KR + HEAD_PAD
_R_KA = WIDTH_A
_R_VA = _R_KA + KV_WIDTH_A
_R_CQ = _R_VA + KV_WIDTH_A
_R_KR = _R_CQ + Q_LORA_RANK + KV_LORA_RANK
_R_GA = _R_KR + QK_ROPE_DIM


def _pad_head_cols(w, n_heads):
    r, dh = w.shape[0], w.shape[1] // n_heads
    return jnp.pad(w.reshape(r, n_heads, dh), ((0, 0), (0, 0), (0, HEAD_PAD - dh))).reshape(r, n_heads * HEAD_PAD)


def _unpad_head_cols(w, n_heads, dh):
    r = w.shape[0]
    return w.reshape(r, n_heads, HEAD_PAD)[:, :, :dh].reshape(r, n_heads * dh)


def _pad_head_rows(w, n_heads):
    dh, c = w.shape[0] // n_heads, w.shape[1]
    return jnp.pad(w.reshape(n_heads, dh, c), ((0, 0), (0, HEAD_PAD - dh), (0, 0))).reshape(n_heads * HEAD_PAD, c)


def _unpad_head_rows(w, n_heads, dh):
    c = w.shape[1]
    return w.reshape(n_heads, HEAD_PAD, c)[:, :dh].reshape(n_heads * dh, c)


def _join_cols(shards):
    ns, r, c = shards.shape
    return shards.transpose(1, 0, 2).reshape(r, ns * c)


def _split_cols(mat):
    r, c4 = mat.shape
    return mat.reshape(r, N_SHARDS, c4 // N_SHARDS).transpose(1, 0, 2)


def _ffn_fwd(tag, x, gain, wcol, wdown, plan=None):
    nb = rms_fwd(tag + "_rms", x, gain)
    gu, h, carried = ffn_up_act(tag + "_up", nb, wcol, 0, plan=plan)
    out = gmm_down(tag + "_down", h, wdown, 0, x, 0.5)
    return out, (x, nb, gu, h), carried


def _ffn_bwd(tag, saved, gain, wcol, wdown, dout, doutb, plan_dh=None, plan_dw=None):
    x, nb, gu, h = saved
    dgu, carried_dh = ffn_down_dact(tag + "_dh", doutb, wdown, gu, gu.shape[-1], 0, 0.5, plan=plan_dh)
    gdown = gmm_down_dw(tag + "_dwd", h, doutb, 0.5)
    gcol, carried_dw = gmm_up_dw(tag + "_dwgu", nb, dgu, plan=plan_dw) if plan_dw is not None else (gmm_up_dw(tag + "_dwgu", nb, dgu), [])
    dn = gmm_up_dx(tag + "_dn", dgu, wcol, x.shape[1], 0)
    dx, dxb, dgain = rms_bwd(tag + "_drms", x, gain, dn, dres=dout)
    return dx, dxb, dgain, gcol, gdown, carried_dh, carried_dw


def _pad_gain(g):
    return jnp.pad(g, ((0, 0), (0, HEAD_PAD - g.shape[1])))


def _mixer_fwd(tag, x1, sm, w, wo, aux, plans=None):
    plans = plans or {}
    b, s = aux["b"], aux["s"]
    t, d = x1.shape
    hb = rms_fwd(tag + "_rms", x1, sm["mix_norm"])
    carried = {}
    if plans.get("proj") is not None:
        proj, carried["proj"] = mm_nn(tag + "_proj", hb, w["win"], tm=512, tn=2048, tk=1024, plan=plans["proj"])
    else:
        proj = mm_nn(tag + "_proj", hb, w["win"], tm=512, tn=2048, tk=1024)
    proj3 = proj.reshape(b, s, proj.shape[1])
    gains = {n: _pad_gain(sm[n]) for n in ("swa_q_norm", "swa_k_norm", "mla_q_norm", "mla_k_norm")}
    cqn = rms_fwd(tag + "_rms_cq", proj, sm["mla_q_lora_norm"], tm=1024, col_blk=_C_CQ // Q_LORA_RANK)
    ckvn = rms_fwd(tag + "_rms_ckv", proj, sm["mla_kv_lora_norm"], tm=1024, col_blk=_C_CKV // KV_LORA_RANK)
    qb_raw = mm_nn(tag + "_uq", cqn, w["wuq"], tm=1024).reshape(b, s, N_HEADS_B * HEAD_PAD)
    kb_raw = mm_nn(tag + "_uk", ckvn, w["wk"], tm=1024).reshape(b, s, N_HEADS_B * HEAD_PAD)
    vb = mm_nn(tag + "_uv", ckvn, w["wv"], tm=1024, out_dtype=BF16).reshape(b, s, N_HEADS_B * HEAD_PAD)
    qah = qk_prep_fwd(tag + "_qa_norm", proj3, 0, N_HEADS_A, HEAD_DIM_A, gains["swa_q_norm"])
    kah = qk_prep_fwd(tag + "_ka_norm", proj3, _C_KA // HEAD_PAD, N_KV_HEADS_A, HEAD_DIM_A, gains["swa_k_norm"])
    sinks = sm["swa_sinks"].reshape(-1)
    oab, carried["swa"] = swa_fwd(tag + "_swa", qah, kah, proj3, _C_VA // HEAD_PAD, aux["pos_col"], aux["pos_row"], aux["slopes"],
                                  sinks, plan=plans.get("swa"))
    kr = (proj3, _C_KR // HEAD_PAD)
    qbh = qk_prep_fwd(tag + "_qb_norm", qb_raw, 0, N_HEADS_B, QK_DIM_B, gains["mla_q_norm"], aux["rope"])
    kbh = qk_prep_fwd(tag + "_kb_norm", kb_raw, 0, N_HEADS_B, QK_DIM_B, gains["mla_k_norm"], aux["rope"], extra=kr)
    ob, obb, lse, carried["mla"] = mla_fwd(tag + "_mla", qbh, kbh, vb, plans.get("mla"))
    oab2, obb2 = oab.reshape(t, -1), obb.reshape(t, -1)
    ya = mm_nn(tag + "_branch_a", oab2, w["wa"], tm=1024)
    yb = mm_nn(tag + "_branch_b", obb2, w["wb"], tm=1024)
    mg = gate_fwd(tag + "_gate", proj, ya, yb, _C_GA, _C_GA + d)
    x2 = gmm_rows(tag + "_out", mg, wo, x1)
    saved = dict(x1=x1, hb=hb, proj=proj, cqn=cqn, ckvn=ckvn, qb_raw=qb_raw, kb_raw=kb_raw, vb=vb, qah=qah, kah=kah,
                 qbh=qbh, kbh=kbh, ob=ob, lse=lse, oab2=oab2, obb2=obb2, ya=ya, yb=yb, mg=mg, gains=gains)
    return x2, saved, carried


def _mixer_bwd(tag, sv, sm, w, wo, aux, dx2, dx2b, plans=None):
    plans = plans or {}
    carried = {}
    b, s = aux["b"], aux["s"]
    t, d = dx2.shape
    proj, gains = sv["proj"], sv["gains"]
    proj3 = proj.reshape(b, s, proj.shape[1])
    kr = (proj3, _C_KR // HEAD_PAD)
    dmg = gmm_rows_dx(tag + "_d_out", dx2b, wo)
    dwo = gmm_rows_dw(tag + "_dw_out", sv["mg"], dx2b)
    dya, dyb, dga, dgb = gate_bwd(tag + "_dgate", proj, sv["ya"], sv["yb"], dmg, _C_GA, _C_GA + d)
    dwa = mm_tn(tag + "_dw_branch_a", sv["oab2"], dya, out_dtype=BF16)
    dwb = mm_tn(tag + "_dw_branch_b", sv["obb2"], dyb, out_dtype=BF16)
    doa = mm_nt(tag + "_d_branch_a", dya, w["wa"], tm=1024).reshape(b, s, -1)
    dob = mm_nt(tag + "_d_branch_b", dyb, w["wb"], tm=1024).reshape(b, s, -1)
    sinks = sm["swa_sinks"].reshape(-1)
    dqah, dkah, dva, dsinks, carried["dswa"] = swa_bwd(tag + "_dswa", sv["qah"], sv["kah"], proj3, _C_VA // HEAD_PAD, aux["pos_col"],
                                                       aux["pos_row"], aux["slopes"], sinks, doa, plan=plans.get("dswa"))
    dqa_raw, dg_swa_q = qk_prep_bwd(tag + "_dqa_norm", proj3, 0, N_HEADS_A, HEAD_DIM_A, gains["swa_q_norm"], dqah)
    dka_raw, dg_swa_k = qk_prep_bwd(tag + "_dka_norm", proj3, _C_KA // HEAD_PAD, N_KV_HEADS_A, HEAD_DIM_A, gains["swa_k_norm"], dkah)
    dqbh, dkbh, dvb, carried["dmla"] = mla_bwd(tag + "_dmla", sv["qbh"], sv["kbh"], sv["vb"], sv["ob"], dob, sv["lse"], plans.get("dmla"))
    dqb_raw, dg_mla_q = qk_prep_bwd(tag + "_dqb_norm", sv["qb_raw"], 0, N_HEADS_B, QK_DIM_B, gains["mla_q_norm"], dqbh, aux["rope"])
    dkb_raw, dkr_sum, dg_mla_k = qk_prep_bwd(tag + "_dkb_norm", sv["kb_raw"], 0, N_HEADS_B, QK_DIM_B, gains["mla_k_norm"], dkbh,
                                             aux["rope"], extra=kr, head_sum=True)
    dq_tok, dk_tok, dv_tok = dqb_raw.reshape(t, -1), dkb_raw.reshape(t, -1), dvb.reshape(t, -1)
    dwuq = mm_tn(tag + "_dw_uq", sv["cqn"], dq_tok, tk=1024, out_dtype=BF16)
    dwk = mm_tn(tag + "_dw_uk", sv["ckvn"], dk_tok, tk=1024, out_dtype=BF16)
    dwv = mm_tn(tag + "_dw_uv", sv["ckvn"], dv_tok, tk=1024, out_dtype=BF16)
    dcqn = mm_nt(tag + "_d_uq", dq_tok, w["wuq"], tm=1024)
    dckvn = mm_nt(tag + "_d_uv", dv_tok, w["wv"], tm=1024, res=mm_nt(tag + "_d_uk", dk_tok, w["wk"], tm=1024))
    dcq, dg_q_lora = rms_bwd(tag + "_drms_cq", proj, sm["mla_q_lora_norm"], dcqn, want_f32=False, tm=1024, col_blk=_C_CQ // Q_LORA_RANK)
    dckv, dg_kv_lora = rms_bwd(tag + "_drms_ckv", proj, sm["mla_kv_lora_norm"], dckvn, want_f32=False, tm=1024,
                               col_blk=_C_CKV // KV_LORA_RANK)
    dproj = jnp.concatenate([dqa_raw.reshape(t, -1), dka_raw.reshape(t, -1), dva.reshape(t, -1).astype(BF16), dcq, dckv,
                             dkr_sum.reshape(t, HEAD_PAD).astype(BF16), dga, dgb], axis=1)
    dwin = mm_tn(tag + "_dw_in", sv["hb"], dproj, tm=1024, tn=2048, tk=1024, out_dtype=BF16)
    dh = mm_nt(tag + "_d_in", dproj, w["win"], tm=512, tn=1024, tk=2048)
    dx1, dx1b, dg_mix = rms_bwd(tag + "_drms", sv["x1"], sm["mix_norm"], dh, dres=dx2)
    wgrads = dict(win=dwin, wuq=dwuq, wk=dwk, wv=dwv, wa=dwa, wb=dwb, wo=dwo)
    sgrads = dict(mix_norm=dg_mix, swa_q_norm=dg_swa_q[:, :HEAD_DIM_A], swa_k_norm=dg_swa_k[:, :HEAD_DIM_A],
                  swa_sinks=dsinks[:, 0].reshape(1, -1), mla_q_lora_norm=dg_q_lora, mla_kv_lora_norm=dg_kv_lora,
                  mla_q_norm=dg_mla_q[:, :QK_DIM_B], mla_k_norm=dg_mla_k[:, :QK_DIM_B])
    return dx1, dx1b, wgrads, sgrads, carried


def _layer_weights(win4, uq4, ukv4, branch4):
    wr = _join_cols(win4)
    kr = jnp.pad(wr[:, _R_KR:_R_GA], ((0, 0), (QK_NOPE_DIM, HEAD_PAD - QK_DIM_B)))
    win = jnp.concatenate([_pad_head_cols(wr[:, :_R_KA], N_HEADS_A), _pad_head_cols(wr[:, _R_KA:_R_VA], N_KV_HEADS_A),
                           _pad_head_cols(wr[:, _R_VA:_R_CQ], N_KV_HEADS_A), wr[:, _R_CQ:_R_KR], kr, wr[:, _R_GA:]], axis=1)
    ukv = _join_cols(ukv4)
    ukv3 = ukv.reshape(ukv.shape[0], N_HEADS_B, QK_NOPE_DIM + V_DIM_B)
    r = branch4.shape[1] // 2
    return dict(win=win, wuq=_pad_head_cols(_join_cols(uq4), N_HEADS_B),
                wk=_pad_head_cols(ukv3[:, :, :QK_NOPE_DIM].reshape(ukv.shape[0], -1), N_HEADS_B),
                wv=_pad_head_cols(ukv3[:, :, QK_NOPE_DIM:].reshape(ukv.shape[0], -1), N_HEADS_B),
                wa=_pad_head_rows(_join_cols(branch4[:, :r]), N_HEADS_A), wb=_pad_head_rows(_join_cols(branch4[:, r:]), N_HEADS_B))


def _mixer_grad_shards(g):
    dw = g["win"]
    win_ref = jnp.concatenate([_unpad_head_cols(dw[:, :_C_KA], N_HEADS_A, HEAD_DIM_A),
                               _unpad_head_cols(dw[:, _C_KA:_C_VA], N_KV_HEADS_A, HEAD_DIM_A),
                               _unpad_head_cols(dw[:, _C_VA:_C_CQ], N_KV_HEADS_A, HEAD_DIM_A), dw[:, _C_CQ:_C_KR],
                               dw[:, _C_KR + QK_NOPE_DIM:_C_KR + QK_DIM_B], dw[:, _C_GA:]], axis=1)
    rk = g["wk"].shape[0]
    ukv = jnp.concatenate([g["wk"].reshape(rk, N_HEADS_B, HEAD_PAD)[:, :, :QK_NOPE_DIM],
                           g["wv"].reshape(rk, N_HEADS_B, HEAD_PAD)[:, :, :V_DIM_B]], axis=2).reshape(rk, -1)
    return (_split_cols(win_ref), _split_cols(_unpad_head_cols(g["wuq"], N_HEADS_B, QK_DIM_B)), _split_cols(ukv),
            jnp.concatenate([_split_cols(_unpad_head_rows(g["wa"], N_HEADS_A, HEAD_DIM_A)),
                             _split_cols(_unpad_head_rows(g["wb"], N_HEADS_B, V_DIM_B))], axis=1))


BUFFER_TAGS = ("col1", "col2", "down1", "down2", "wo", "win", "uq", "ukv", "branch")
GATHER_CARRIERS = (("ffn1_up", ("col1",)), ("proj", ("down1", "wo")), ("swa", ("col2",)),
                   ("mla", ("down2", "win", "uq", "ukv", "branch")))
SCATTER_CARRIERS = (("ffn2_dh", ("col1",)), ("ffn2_dwgu", ("col2",)), ("dswa", ("down1", "down2", "wo")),
                    ("dmla", ("win", "uq", "ukv", "branch")))


def _pick(by_tag, tags):
    return [by_tag[t] for t in tags]


def _train_step(x, positions, target, packs, small, shard):
    depth = 2
    b, s, d = x.shape
    t = b * s
    posf = positions.astype(F32)
    half = QK_ROPE_DIM // 2
    inv_freq = ROPE_BASE ** (-jnp.arange(half, dtype=F32) / half)
    ang = posf[..., None] * inv_freq
    cos, sin = jnp.cos(ang), jnp.sin(ang)
    tail = HEAD_PAD - QK_DIM_B
    rope = (jnp.concatenate([jnp.ones((b, s, QK_NOPE_DIM), F32), cos, cos, jnp.ones((b, s, tail), F32)], axis=-1),
            jnp.concatenate([jnp.zeros((b, s, QK_NOPE_DIM), F32), sin, sin, jnp.zeros((b, s, tail), F32)], axis=-1))
    slopes = jnp.exp2(-8.0 * (jnp.arange(N_HEADS_A, dtype=F32) + 1.0) / N_HEADS_A)
    aux = dict(b=b, s=s, pos_col=posf.reshape(b, s, 1), pos_row=posf.reshape(b, 1, s), rope=rope, slopes=slopes)

    def sm_of(l):
        return {n: small[n][l:l + 1] for n in SMALL_NAMES}

    by_tag = dict(zip(BUFFER_TAGS, packs))
    wts, saved, sgrads, parts, slots = [None] * depth, [None] * depth, [None] * depth, [None] * depth, [None] * depth

    wts[0] = dict(zip(BUFFER_TAGS, comm_call("gather_l0", gather_plan(packs, 0))))
    gplans = {kern: gather_plan(_pick(by_tag, tags), 1) for kern, tags in GATHER_CARRIERS}
    h = x.reshape(t, d)
    for l in range(depth):
        sm, wl = sm_of(l), wts[l]
        first = l == 0
        weights = _layer_weights(wl["win"], wl["uq"], wl["ukv"], wl["branch"])
        h, s1, c_up = _ffn_fwd(f"l{l}_ffn1", h, sm["ffn1_norm"], wl["col1"], wl["down1"], gplans["ffn1_up"] if first else None)
        h, s2, c_mix = _mixer_fwd(f"l{l}_mix", h, sm, weights, wl["wo"], aux, gplans if first else None)
        if first:
            got = dict(c_mix, ffn1_up=c_up)
            wts[1] = {tag: arr for kern, tags in GATHER_CARRIERS for tag, arr in zip(tags, got[kern])}
        h, s3, _ = _ffn_fwd(f"l{l}_ffn2", h, sm["ffn2_norm"], wl["col2"], wl["down2"])
        saved[l] = (s1, s2, s3, weights)
    loss, dh, dhb = loss_fwd_bwd("loss", h, target.reshape(t, d))

    for l in reversed(range(depth)):
        sm, wl = sm_of(l), wts[l]
        s1, s2, s3, weights = saved[l]
        last = l == 0
        splans = {kern: scatter_plan(_pick(parts[1], tags), 1) for kern, tags in SCATTER_CARRIERS} if last else {}
        dh, dhb, dg_f2, gcol2, gdown2, c_dh, c_dw = _ffn_bwd(f"l{l}_ffn2", s3, sm["ffn2_norm"], wl["col2"], wl["down2"], dh, dhb,
                                                             splans.get("ffn2_dh"), splans.get("ffn2_dwgu"))
        dh, dhb, wg, sg, c_mix = _mixer_bwd(f"l{l}_mix", s2, sm, weights, wl["wo"], aux, dh, dhb, splans)
        if last:
            got = dict(c_mix, ffn2_dh=c_dh, ffn2_dwgu=c_dw)
            slots[1] = {tag: arr for kern, tags in SCATTER_CARRIERS for tag, arr in zip(tags, got[kern])}
        dh, dhb, dg_f1, gcol1, gdown1, _, _ = _ffn_bwd(f"l{l}_ffn1", s1, sm["ffn1_norm"], wl["col1"], wl["down1"], dh, dhb)
        sg.update(ffn1_norm=dg_f1, ffn2_norm=dg_f2)
        sgrads[l] = sg
        gwin, guq, gukv, gbranch = _mixer_grad_shards(wg)
        mine = [gcol1, gcol2, gdown1, gdown2, wg["wo"], gwin, guq, gukv, gbranch]
        theirs = comm_call(f"swap_l{l}", swap_plan(mine, l))
        parts[l] = {tag: add_pair(f"l{l}_add_{tag}", g.reshape(-1, g.shape[-1]), o.reshape(-1, o.shape[-1])).reshape(g.shape)
                    for tag, g, o in zip(BUFFER_TAGS, mine, theirs)}
    slots[0] = dict(zip(BUFFER_TAGS, comm_call("scatter_l0", scatter_plan(_pick(parts[0], BUFFER_TAGS), 0))))
    finals = [[sum_slots(f"l{l}_sum_{tag}", slots[l][tag], parts[l][tag], shard) for tag in BUFFER_TAGS] for l in range(depth)]
    shared = comm_call("share", share_plan(finals))
    nk = len(BUFFER_TAGS)
    return loss, dh.reshape(b, s, d), [dict(zip(BUFFER_TAGS, shared[l * nk:(l + 1) * nk])) for l in range(depth)], sgrads


def kernel(x, positions, ffn1_norm, ffn1_w_gate, ffn1_w_up, ffn1_w_down, mix_norm, w_in, swa_q_norm, swa_k_norm, swa_sinks, mla_q_lora_norm, mla_w_uq, mla_kv_lora_norm, mla_w_ukv, mla_q_norm, mla_k_norm, w_branch_a, w_branch_b, w_out, ffn2_norm, ffn2_w_gate, ffn2_w_up, ffn2_w_down, loss_target, m_ffn1_norm, m_ffn1_w_gate, m_ffn1_w_up, m_ffn1_w_down, m_mix_norm, m_w_in, m_swa_q_norm, m_swa_k_norm, m_swa_sinks, m_mla_q_lora_norm, m_mla_w_uq, m_mla_kv_lora_norm, m_mla_w_ukv, m_mla_q_norm, m_mla_k_norm, m_w_branch_a, m_w_branch_b, m_w_out, m_ffn2_norm, m_ffn2_w_gate, m_ffn2_w_up, m_ffn2_w_down, v_ffn1_norm, v_ffn1_w_gate, v_ffn1_w_up, v_ffn1_w_down, v_mix_norm, v_w_in, v_swa_q_norm, v_swa_k_norm, v_swa_sinks, v_mla_q_lora_norm, v_mla_w_uq, v_mla_kv_lora_norm, v_mla_w_ukv, v_mla_q_norm, v_mla_k_norm, v_w_branch_a, v_w_branch_b, v_w_out, v_ffn2_norm, v_ffn2_w_gate, v_ffn2_w_up, v_ffn2_w_down):
    args = (x, positions, ffn1_norm, ffn1_w_gate, ffn1_w_up, ffn1_w_down, mix_norm, w_in, swa_q_norm, swa_k_norm, swa_sinks, mla_q_lora_norm, mla_w_uq, mla_kv_lora_norm, mla_w_ukv, mla_q_norm, mla_k_norm, w_branch_a, w_branch_b, w_out, ffn2_norm, ffn2_w_gate, ffn2_w_up, ffn2_w_down, loss_target, m_ffn1_norm, m_ffn1_w_gate, m_ffn1_w_up, m_ffn1_w_down, m_mix_norm, m_w_in, m_swa_q_norm, m_swa_k_norm, m_swa_sinks, m_mla_q_lora_norm, m_mla_w_uq, m_mla_kv_lora_norm, m_mla_w_ukv, m_mla_q_norm, m_mla_k_norm, m_w_branch_a, m_w_branch_b, m_w_out, m_ffn2_norm, m_ffn2_w_gate, m_ffn2_w_up, m_ffn2_w_down, v_ffn1_norm, v_ffn1_w_gate, v_ffn1_w_up, v_ffn1_w_down, v_mix_norm, v_w_in, v_swa_q_norm, v_swa_k_norm, v_swa_sinks, v_mla_q_lora_norm, v_mla_w_uq, v_mla_kv_lora_norm, v_mla_w_ukv, v_mla_q_norm, v_mla_k_norm, v_w_branch_a, v_w_branch_b, v_w_out, v_ffn2_norm, v_ffn2_w_gate, v_ffn2_w_up, v_ffn2_w_down)
    a = dict(zip(ARG_NAMES, args, strict=True))
    x = a["x"]
    depth = a["ffn1_norm"].shape[0]
    d = x.shape[-1]
    assert depth == 2, "one layer slab per core of a chip"

    def bf16_rows(*names):
        return jnp.concatenate([a[n] for n in names], axis=1).astype(BF16) if len(names) > 1 else a[names[0]].astype(BF16)

    packs = [bf16_rows("ffn1_w_gate", "ffn1_w_up"), bf16_rows("ffn2_w_gate", "ffn2_w_up"), bf16_rows("ffn1_w_down"),
             bf16_rows("ffn2_w_down"), bf16_rows("w_out"), bf16_rows("w_in"), bf16_rows("mla_w_uq"), bf16_rows("mla_w_ukv"),
             bf16_rows("w_branch_a", "w_branch_b")]
    shard_me = (2 * lax.axis_index("x") + lax.axis_index("y")).astype(jnp.int32).reshape(1)
    small = {n: a[n] for n in SMALL_NAMES}

    loss, grad_x, summed, sgrads = _train_step(x, a["positions"], a["loss_target"], packs, small, shard_me)

    full = {tag: jnp.stack([summed[l][tag] for l in range(depth)]) for tag in BUFFER_TAGS}
    r_branch = a["w_branch_a"].shape[1]
    grads = dict(ffn1_w_gate=full["col1"][:, :d], ffn1_w_up=full["col1"][:, d:], ffn2_w_gate=full["col2"][:, :d],
                 ffn2_w_up=full["col2"][:, d:], ffn1_w_down=full["down1"], ffn2_w_down=full["down2"], w_out=full["wo"],
                 w_in=full["win"], mla_w_uq=full["uq"], mla_w_ukv=full["ukv"], w_branch_a=full["branch"][:, :r_branch],
                 w_branch_b=full["branch"][:, r_branch:])

    flat = jnp.concatenate([jnp.concatenate([sgrads[l][n].reshape(-1) for l in range(depth)]) for n in SMALL_NAMES] + [loss.reshape(-1)])
    n_small = flat.shape[0]
    rows = -(-n_small // (8 * LANES)) * 8
    pad = rows * LANES - n_small

    def small_pack(v):
        return jnp.pad(v, (0, pad)).reshape(rows, LANES)

    total = small_allreduce(small_pack(flat))
    w_s, m_s, v_s = (small_pack(jnp.concatenate([a[p + n].reshape(-1) for n in SMALL_NAMES] + [jnp.zeros((1,), F32)]))
                     for p in ("", "m_", "v_"))
    d_s, nm_s, nv_s = adamw("adamw_small", w_s, total, m_s, v_s)

    def small_unpack(buf):
        out, off, flat_b = {}, 0, buf.reshape(-1)
        for n in SMALL_NAMES:
            size = a[n].shape[0] * a[n].shape[1]
            out[n] = flat_b[off:off + size].reshape(a[n].shape)
            off += size
        return out

    grads.update(small_unpack(total))
    delta, new_m, new_v = small_unpack(d_s), small_unpack(nm_s), small_unpack(nv_s)
    for n in PACK_NAMES:
        shp = a[n].shape
        two_d = (shp[0] * shp[1], shp[2])
        dn, mn, vn = adamw("adamw_" + n, a[n].reshape(two_d), grads[n].reshape(two_d), a["m_" + n].reshape(two_d), a["v_" + n].reshape(two_d))
        delta[n], new_m[n], new_v[n] = dn.reshape(shp), mn.reshape(shp), vn.reshape(shp)

    loss_out = total.reshape(-1)[n_small - 1]
    return (loss_out, grad_x, *[grads[n] for n in WEIGHT_NAMES], *[delta[n] for n in WEIGHT_NAMES],
            *[new_m[n] for n in WEIGHT_NAMES], *[new_v[n] for n in WEIGHT_NAMES])
```

```python
import functools

import numpy as np
import jax
import jax.numpy as jnp
from jax import lax
from jax.experimental import pallas as pl
from jax.experimental.pallas import tpu as pltpu

F32 = jnp.float32
BF16 = jnp.bfloat16
MESH = pl.DeviceIdType.MESH

HEAD_DIM_A = 64
N_HEADS_A = 8
N_KV_HEADS_A = 2
GROUP_A = N_HEADS_A // N_KV_HEADS_A
BLOCK = 128
N_HEADS_B = 8
Q_LORA_RANK = 256
KV_LORA_RANK = 128
QK_NOPE_DIM = 64
QK_ROPE_DIM = 32
QK_DIM_B = QK_NOPE_DIM + QK_ROPE_DIM
V_DIM_B = 64
ROPE_BASE = 10000.0
WIDTH_A = N_HEADS_A * HEAD_DIM_A
WIDTH_B = N_HEADS_B * V_DIM_B
KV_WIDTH_A = N_KV_HEADS_A * HEAD_DIM_A
EPS = 1e-6
NEG = -1e30
ADAM_LR = 0.001
ADAM_B1 = 0.9
ADAM_B2 = 0.999
ADAM_EPS = 1e-08
ADAM_WD = 0.01
ADAM_STEP = 10

N_SHARDS = 4
N_DEV = 8
LANES = 128
VMEM_LIMIT = 48 * 1024 * 1024

PACK_NAMES = ("ffn1_w_gate", "ffn1_w_up", "ffn1_w_down", "w_in", "mla_w_uq", "mla_w_ukv",
              "w_branch_a", "w_branch_b", "w_out", "ffn2_w_gate", "ffn2_w_up", "ffn2_w_down")
SMALL_NAMES = ("ffn1_norm", "mix_norm", "swa_q_norm", "swa_k_norm", "swa_sinks", "mla_q_lora_norm",
               "mla_kv_lora_norm", "mla_q_norm", "mla_k_norm", "ffn2_norm")
WEIGHT_NAMES = ("ffn1_norm", "ffn1_w_gate", "ffn1_w_up", "ffn1_w_down", "mix_norm", "w_in", "swa_q_norm",
                "swa_k_norm", "swa_sinks", "mla_q_lora_norm", "mla_w_uq", "mla_kv_lora_norm", "mla_w_ukv",
                "mla_q_norm", "mla_k_norm", "w_branch_a", "w_branch_b", "w_out", "ffn2_norm", "ffn2_w_gate",
                "ffn2_w_up", "ffn2_w_down")


def _params(sem):
    return pltpu.CompilerParams(dimension_semantics=sem, vmem_limit_bytes=VMEM_LIMIT)


def _tile(n, want, align):
    if n <= want:
        return n
    t = (want // align) * align
    while t > align and n % t:
        t -= align
    assert t >= align and n % t == 0, (n, want, align)
    return t


def _mm_call(name, a, b, out_struct, grid, a_spec, b_spec, o_spec, dims, n_red, acc_shape, alpha=1.0, res=None, into=None,
             inner=0, a_cols=0, plan=None):
    if n_red and all(g == 1 for g in grid[len(grid) - n_red:]):
        n_red = 0
    n_par = len(grid) - n_red
    n_in = 2 + int(res is not None) + int(into is not None)
    p_ins, p_in_specs, p_outs, p_out_specs, p_scratch, semantics, at_start, at_end = _hosted(
        plan, n_in, 1, grid, ("parallel",) * n_par + ("arbitrary",) * n_red)

    def body(*refs):
        at_start(refs)
        compute(refs)
        at_end(refs)

    def compute(refs):
        a_ref, b_ref = refs[:2]
        r_ref = refs[2] if res is not None else None
        o_ref, acc_ref = refs[n_in + len(p_ins)], refs[n_in + len(p_ins) + 1 + len(p_outs)]
        if inner:
            def a_of(s):
                return a_ref[:, s * a_cols:(s + 1) * a_cols] if a_cols else a_ref[s]

            part = lax.dot_general(a_of(0), b_ref[0], dims, preferred_element_type=F32)
            for s in range(1, inner):
                part = part + lax.dot_general(a_of(s), b_ref[s], dims, preferred_element_type=F32)
        else:
            part = lax.dot_general(a_ref[...], b_ref[...], dims, preferred_element_type=F32)

        def finish(total):
            if alpha != 1.0:
                total = total * alpha
            if r_ref is not None:
                total = r_ref[...] + total
            o_ref[...] = total.astype(o_ref.dtype)

        if n_red == 0:
            finish(part)
            return
        ids = [pl.program_id(n_par + i) for i in range(n_red)]
        first = functools.reduce(jnp.logical_and, [i == 0 for i in ids])
        last = functools.reduce(jnp.logical_and, [i == grid[n_par + k] - 1 for k, i in enumerate(ids)])

        @pl.when(first)
        def _():
            acc_ref[...] = part

        @pl.when(jnp.logical_not(first))
        def _():
            acc_ref[...] += part

        @pl.when(last)
        def _():
            finish(acc_ref[...])

    in_specs = [a_spec, b_spec] + ([o_spec] if res is not None else [])
    args = (a, b) + ((res,) if res is not None else ())
    aliases = {}
    if into is not None:
        aliases = {len(args): 0}
        in_specs.append(pl.BlockSpec(memory_space=pl.ANY))
        args = args + (into,)
    if plan is None:
        return pl.pallas_call(
            body, name=name, out_shape=out_struct, grid=grid, in_specs=in_specs, out_specs=o_spec,
            scratch_shapes=[pltpu.VMEM(acc_shape, F32)], input_output_aliases=aliases, compiler_params=_params(semantics),
        )(*args)
    res_all = pl.pallas_call(
        body, name=name, out_shape=(out_struct,) + tuple(p_outs), grid=grid, in_specs=in_specs + p_in_specs,
        out_specs=(o_spec,) + tuple(p_out_specs), scratch_shapes=[pltpu.VMEM(acc_shape, F32)] + p_scratch,
        input_output_aliases=aliases, compiler_params=_params(semantics),
    )(*args, *p_ins)
    return res_all[0], list(res_all[1:])


_NN = (((1,), (0,)), ((), ()))
_NT = (((1,), (1,)), ((), ()))
_TN = (((0,), (0,)), ((), ()))


def mm_nn(name, a, b, tm=512, tn=1024, tk=1024, out_dtype=F32, alpha=1.0, res=None, plan=None):
    (m, k), (_, n) = a.shape, b.shape
    tm, tn, tk = _tile(m, tm, 16), _tile(n, tn, LANES), _tile(k, tk, LANES)
    return _mm_call(name, a, b, jax.ShapeDtypeStruct((m, n), out_dtype), (m // tm, n // tn, k // tk),
                    pl.BlockSpec((tm, tk), lambda i, j, kk: (i, kk)), pl.BlockSpec((tk, tn), lambda i, j, kk: (kk, j)),
                    pl.BlockSpec((tm, tn), lambda i, j, kk: (i, j)), _NN, 1, (tm, tn), alpha, res, plan=plan)


def mm_nt(name, a, b, tm=512, tn=1024, tk=1024, out_dtype=F32, alpha=1.0, res=None):
    (m, n), (k, _) = a.shape, b.shape
    tm, tn, tk = _tile(m, tm, 16), _tile(k, tn, LANES), _tile(n, tk, LANES)
    return _mm_call(name, a, b, jax.ShapeDtypeStruct((m, k), out_dtype), (m // tm, k // tn, n // tk),
                    pl.BlockSpec((tm, tk), lambda i, j, kk: (i, kk)), pl.BlockSpec((tn, tk), lambda i, j, kk: (j, kk)),
                    pl.BlockSpec((tm, tn), lambda i, j, kk: (i, j)), _NT, 1, (tm, tn), alpha, res)


def mm_tn(name, a, b, tm=1024, tn=1024, tk=1024, out_dtype=F32, alpha=1.0):
    (m, k), (_, n) = a.shape, b.shape
    tm, tn, tk = _tile(k, tm, LANES), _tile(n, tn, LANES), _tile(m, tk, 16)
    return _mm_call(name, a, b, jax.ShapeDtypeStruct((k, n), out_dtype), (k // tm, n // tn, m // tk),
                    pl.BlockSpec((tk, tm), lambda i, j, kk: (kk, i)), pl.BlockSpec((tk, tn), lambda i, j, kk: (kk, j)),
                    pl.BlockSpec((tm, tn), lambda i, j, kk: (i, j)), _TN, 1, (tm, tn), alpha)


def ffn_up_act(name, a, w, blk, tm=512, plan=None):
    (m, k), (ns, _, n) = a.shape, w.shape
    tm = _tile(m, tm, 16)
    grid = (ns, m // tm)
    p_ins, p_in_specs, p_outs, p_out_specs, p_scratch, semantics, at_start, at_end = _hosted(plan, 3, 2, grid, ("parallel", "parallel"))

    def body(*refs):
        a_ref, wg_ref, wu_ref = refs[:3]
        gu_ref, h_ref = refs[3 + len(p_ins):5 + len(p_ins)]
        at_start(refs)
        av = a_ref[...]
        gate = jnp.dot(av, wg_ref[...], preferred_element_type=F32)
        up = jnp.dot(av, wu_ref[...], preferred_element_type=F32)
        gu_ref[0] = gate.astype(BF16)
        gu_ref[1] = up.astype(BF16)
        h_ref[...] = (gate * jax.nn.sigmoid(gate) * up).astype(BF16)
        at_end(refs)

    res = pl.pallas_call(
        body, name=name, grid=grid,
        out_shape=(jax.ShapeDtypeStruct((2, ns, m, n), BF16), jax.ShapeDtypeStruct((ns, m, n), BF16)) + tuple(p_outs),
        in_specs=[pl.BlockSpec((tm, k), lambda s, i: (i, 0)), pl.BlockSpec((None, k, n), lambda s, i: (s, blk, 0)),
                  pl.BlockSpec((None, k, n), lambda s, i: (s, blk + 1, 0))] + p_in_specs,
        out_specs=(pl.BlockSpec((2, None, tm, n), lambda s, i: (0, s, i, 0)), pl.BlockSpec((None, tm, n), lambda s, i: (s, i, 0)))
        + tuple(p_out_specs),
        scratch_shapes=p_scratch, compiler_params=_params(semantics),
    )(a, w, w, *p_ins)
    return res[0], res[1], list(res[2:])


def ffn_down_dact(name, a, w, gu, n, blk, alpha, tm=512, plan=None):
    (m, d), ns = a.shape, w.shape[0]
    tm = _tile(m, tm, 16)
    grid = (ns, m // tm)
    p_ins, p_in_specs, p_outs, p_out_specs, p_scratch, semantics, at_start, at_end = _hosted(plan, 3, 1, grid, ("parallel", "parallel"))

    def body(*refs):
        a_ref, w_ref, gu_ref = refs[:3]
        o_ref = refs[3 + len(p_ins)]
        at_start(refs)
        dh = lax.dot_general(a_ref[...], w_ref[...], _NT, preferred_element_type=F32) * alpha
        gate, up = gu_ref[0].astype(F32), gu_ref[1].astype(F32)
        s = jax.nn.sigmoid(gate)
        o_ref[0] = (dh * up * (s * (1.0 + gate * (1.0 - s)))).astype(BF16)
        o_ref[1] = (dh * (gate * s)).astype(BF16)
        at_end(refs)

    gu_spec = pl.BlockSpec((2, None, tm, n), lambda s, i: (0, s, i, 0))
    res = pl.pallas_call(
        body, name=name, grid=grid, out_shape=(jax.ShapeDtypeStruct((2, ns, m, n), BF16),) + tuple(p_outs),
        in_specs=[pl.BlockSpec((tm, d), lambda s, i: (i, 0)), pl.BlockSpec((None, n, d), lambda s, i: (s, blk, 0)), gu_spec] + p_in_specs,
        out_specs=(gu_spec,) + tuple(p_out_specs), scratch_shapes=p_scratch, compiler_params=_params(semantics),
    )(a, w, gu, *p_ins)
    return res[0], list(res[1:])


def gmm_up_dw(name, a, dgu, tk=2048, plan=None):
    (m, k), (_, ns, _, n) = a.shape, dgu.shape
    tk = _tile(m, tk, 16)
    return _mm_call(name, a, dgu, jax.ShapeDtypeStruct((ns, 2 * k, n), BF16), (2, ns, m // tk),
                    pl.BlockSpec((tk, k), lambda j, s, kk: (kk, 0)), pl.BlockSpec((None, None, tk, n), lambda j, s, kk: (j, s, kk, 0)),
                    pl.BlockSpec((None, k, n), lambda j, s, kk: (s, j, 0)), _TN, 1, (k, n), plan=plan)


def gmm_up_dx(name, dgu, w, k, blk, tm=512, plan=None):
    _, ns, m, n = dgu.shape
    tm = _tile(m, tm, 16)
    return _mm_call(name, dgu, w, jax.ShapeDtypeStruct((m, k), F32), (m // tm, 2),
                    pl.BlockSpec((None, ns, tm, n), lambda i, j: (j, 0, i, 0)),
                    pl.BlockSpec((ns, k, n), lambda i, j: (0, blk + j, 0)),
                    pl.BlockSpec((tm, k), lambda i, j: (i, 0)), _NT, 1, (tm, k), inner=ns, plan=plan)


def gmm_down(name, h, w, blk, res, alpha, tm=512):
    (ns, m, n), d = h.shape, w.shape[2]
    tm = _tile(m, tm, 16)
    return _mm_call(name, h, w, jax.ShapeDtypeStruct((m, d), F32), (m // tm,),
                    pl.BlockSpec((ns, tm, n), lambda i: (0, i, 0)), pl.BlockSpec((ns, n, d), lambda i: (0, blk, 0)),
                    pl.BlockSpec((tm, d), lambda i: (i, 0)), _NN, 0, (8, LANES), alpha, res, inner=ns)


def gmm_down_dw(name, h, b, alpha, tk=2048):
    (ns, m, n), d = h.shape, b.shape[1]
    tk = _tile(m, tk, 16)
    return _mm_call(name, h, b, jax.ShapeDtypeStruct((ns, n, d), BF16), (ns, m // tk),
                    pl.BlockSpec((None, tk, n), lambda s, kk: (s, kk, 0)), pl.BlockSpec((tk, d), lambda s, kk: (kk, 0)),
                    pl.BlockSpec((None, n, d), lambda s, kk: (s, 0, 0)), _TN, 1, (n, d), alpha)


def gmm_rows(name, a, w, res, tm=512):
    (m, _), (ns, r, d) = a.shape, w.shape
    tm = _tile(m, tm, 16)
    return _mm_call(name, a, w, jax.ShapeDtypeStruct((m, d), F32), (m // tm,),
                    pl.BlockSpec((tm, ns * r), lambda i: (i, 0)), pl.BlockSpec((ns, r, d), lambda i: (0, 0, 0)),
                    pl.BlockSpec((tm, d), lambda i: (i, 0)), _NN, 0, (8, LANES), 1.0, res, inner=ns, a_cols=r)


def gmm_rows_dx(name, a, w, tm=512):
    (m, d), (ns, r, _) = a.shape, w.shape
    tm = _tile(m, tm, 16)
    return _mm_call(name, a, w, jax.ShapeDtypeStruct((m, ns * r), F32), (ns, m // tm),
                    pl.BlockSpec((tm, d), lambda s, i: (i, 0)), pl.BlockSpec((None, r, d), lambda s, i: (s, 0, 0)),
                    pl.BlockSpec((tm, r), lambda s, i: (i, s)), _NT, 0, (8, LANES))


def gmm_rows_dw(name, a, b, tk=2048):
    (m, da), d = a.shape, b.shape[1]
    ns, r = N_SHARDS, da // N_SHARDS
    tk = _tile(m, tk, 16)
    return _mm_call(name, a, b, jax.ShapeDtypeStruct((ns, r, d), BF16), (ns, m // tk),
                    pl.BlockSpec((tk, r), lambda s, kk: (kk, s)), pl.BlockSpec((tk, d), lambda s, kk: (kk, 0)),
                    pl.BlockSpec((None, r, d), lambda s, kk: (s, 0, 0)), _TN, 1, (r, d))


def rms_fwd(name, x, gain, tm=512, col_blk=0):
    m, d = x.shape[0], gain.shape[1]
    tm = _tile(m, tm, 16)

    def body(x_ref, g_ref, o_ref):
        xv = x_ref[...]
        r = lax.rsqrt(jnp.mean(xv * xv, axis=-1, keepdims=True) + EPS)
        o_ref[...] = (xv * r * g_ref[...]).astype(o_ref.dtype)

    return pl.pallas_call(
        body, name=name, out_shape=jax.ShapeDtypeStruct((m, d), BF16), grid=(m // tm,),
        in_specs=[pl.BlockSpec((tm, d), lambda i: (i, col_blk)), pl.BlockSpec((1, d), lambda i: (0, 0))],
        out_specs=pl.BlockSpec((tm, d), lambda i: (i, 0)), compiler_params=_params(("parallel",)),
    )(x, gain)


def rms_bwd(name, x, gain, dn, dres=None, want_f32=True, want_bf16=True, tm=512, col_blk=0):
    m, d = x.shape[0], gain.shape[1]
    tm = _tile(m, tm, 16)
    n_out = int(want_f32) + int(want_bf16)

    def body(*refs):
        x_ref, g_ref, dn_ref = refs[:3]
        pos = 3
        r_ref = None
        if dres is not None:
            r_ref = refs[pos]
            pos += 1
        outs = refs[pos:pos + n_out]
        dg_ref = refs[pos + n_out]
        xv = x_ref[...]
        r = lax.rsqrt(jnp.mean(xv * xv, axis=-1, keepdims=True) + EPS)
        xhat = xv * r
        dnv = dn_ref[...]
        dxhat = dnv * g_ref[...]
        dx = r * (dxhat - xhat * jnp.mean(dxhat * xhat, axis=-1, keepdims=True))
        if r_ref is not None:
            dx = r_ref[...] + dx
        for o in outs:
            o[...] = dx.astype(o.dtype)
        part = jnp.sum(dnv * xhat, axis=0, keepdims=True)

        @pl.when(pl.program_id(0) == 0)
        def _():
            dg_ref[...] = part

        @pl.when(pl.program_id(0) > 0)
        def _():
            dg_ref[...] += part

    row = pl.BlockSpec((tm, d), lambda i: (i, 0))
    vec = pl.BlockSpec((1, d), lambda i: (0, 0))
    out_shape = ([jax.ShapeDtypeStruct((m, d), F32)] if want_f32 else []) + ([jax.ShapeDtypeStruct((m, d), BF16)] if want_bf16 else [])
    res = pl.pallas_call(
        body, name=name, out_shape=tuple(out_shape) + (jax.ShapeDtypeStruct((1, d), F32),), grid=(m // tm,),
        in_specs=[pl.BlockSpec((tm, d), lambda i: (i, col_blk)), vec, row] + ([row] if dres is not None else []),
        out_specs=tuple([row] * n_out) + (vec,), compiler_params=_params(("arbitrary",)),
    )(*((x, gain, dn) + ((dres,) if dres is not None else ())))
    return res


def gate_fwd(name, proj, ya, yb, off_a, off_b, tm=256):
    m, d = ya.shape
    w = proj.shape[1]
    tm = _tile(m, tm, 16)

    def body(p_ref, ya_ref, yb_ref, o_ref):
        ga, gb = p_ref[:, off_a:off_a + d], p_ref[:, off_b:off_b + d]
        o_ref[...] = (jax.nn.sigmoid(ga) * ya_ref[...] + jax.nn.sigmoid(gb) * yb_ref[...]).astype(o_ref.dtype)

    row = pl.BlockSpec((tm, d), lambda i: (i, 0))
    return pl.pallas_call(
        body, name=name, out_shape=jax.ShapeDtypeStruct((m, d), BF16), grid=(m // tm,),
        in_specs=[pl.BlockSpec((tm, w), lambda i: (i, 0)), row, row], out_specs=row, compiler_params=_params(("parallel",)),
    )(proj, ya, yb)


def gate_bwd(name, proj, ya, yb, dmerged, off_a, off_b, tm=256):
    m, d = ya.shape
    w = proj.shape[1]
    tm = _tile(m, tm, 16)

    def body(p_ref, ya_ref, yb_ref, dm_ref, dya_ref, dyb_ref, dga_ref, dgb_ref):
        sa, sb = jax.nn.sigmoid(p_ref[:, off_a:off_a + d]), jax.nn.sigmoid(p_ref[:, off_b:off_b + d])
        dm = dm_ref[...]
        dya_ref[...] = (dm * sa).astype(BF16)
        dyb_ref[...] = (dm * sb).astype(BF16)
        dga_ref[...] = (dm * ya_ref[...] * (sa * (1.0 - sa))).astype(BF16)
        dgb_ref[...] = (dm * yb_ref[...] * (sb * (1.0 - sb))).astype(BF16)

    row = pl.BlockSpec((tm, d), lambda i: (i, 0))
    o = jax.ShapeDtypeStruct((m, d), BF16)
    return pl.pallas_call(
        body, name=name, out_shape=(o, o, o, o), grid=(m // tm,),
        in_specs=[pl.BlockSpec((tm, w), lambda i: (i, 0)), row, row, row], out_specs=(row, row, row, row),
        compiler_params=_params(("parallel",)),
    )(proj, ya, yb, dmerged)


def loss_fwd_bwd(name, y, target, tm=512):
    m, d = y.shape
    tm = _tile(m, tm, 16)

    def body(y_ref, t_ref, l_ref, dy_ref, dyb_ref):
        err = y_ref[...] - t_ref[...]
        dy = err * (1.0 / d)
        dy_ref[...] = dy
        dyb_ref[...] = dy.astype(BF16)
        part = 0.5 * jnp.sum(jnp.mean(err * err, axis=-1, keepdims=True), axis=0, keepdims=True)

        @pl.when(pl.program_id(0) == 0)
        def _():
            l_ref[...] = part

        @pl.when(pl.program_id(0) > 0)
        def _():
            l_ref[...] += part

    row = pl.BlockSpec((tm, d), lambda i: (i, 0))
    return pl.pallas_call(
        body, name=name, grid=(m // tm,),
        out_shape=(jax.ShapeDtypeStruct((1, 1), F32), jax.ShapeDtypeStruct((m, d), F32), jax.ShapeDtypeStruct((m, d), BF16)),
        in_specs=[row, row], out_specs=(pl.BlockSpec((1, 1), lambda i: (0, 0)), row, row),
        compiler_params=_params(("arbitrary",)),
    )(y, target)


def adamw(name, w, g, m, v):
    r, c = w.shape
    tr = _tile(r, max(8, (2 * 1024 * 1024) // (4 * c) // 8 * 8), 8)
    c1 = np.float32(1.0 - ADAM_B1 ** ADAM_STEP)
    c2 = np.float32(1.0 - ADAM_B2 ** ADAM_STEP)

    def body(w_ref, g_ref, m_ref, v_ref, d_ref, nm_ref, nv_ref):
        gv = g_ref[...]
        nm = ADAM_B1 * m_ref[...] + (1.0 - ADAM_B1) * gv
        nv = ADAM_B2 * v_ref[...] + (1.0 - ADAM_B2) * (gv * gv)
        d_ref[...] = -ADAM_LR * ((nm / c1) / (jnp.sqrt(nv / c2) + ADAM_EPS) + ADAM_WD * w_ref[...])
        nm_ref[...] = nm
        nv_ref[...] = nv

    row = pl.BlockSpec((tr, c), lambda i: (i, 0))
    o = jax.ShapeDtypeStruct((r, c), F32)
    return pl.pallas_call(
        body, name=name, out_shape=(o, o, o), grid=(r // tr,), in_specs=[row] * 4, out_specs=(row, row, row),
        compiler_params=_params(("parallel",)),
    )(w, g, m, v)


HEAD_PAD = LANES


def _rope_rot():
    r = np.zeros((HEAD_PAD, HEAD_PAD), np.float32)
    half = QK_ROPE_DIM // 2
    for j in range(half):
        r[QK_NOPE_DIM + half + j, QK_NOPE_DIM + j] = -1.0
        r[QK_NOPE_DIM + j, QK_NOPE_DIM + half + j] = 1.0
    return r


def _head_spec(s, blk0):
    return pl.BlockSpec((None, s, HEAD_PAD), lambda bi, hi: (bi, 0, blk0 + hi))


def qk_prep_fwd(name, x, blk0, n_heads, d_real, gain, rope=None, extra=None):
    b, s, _ = x.shape

    def body(*refs):
        x_ref, g_ref = refs[0], refs[1]
        pos = 2
        xv = x_ref[...]
        if extra is not None:
            xv = xv + refs[pos][...]
            pos += 1
        y = xv * lax.rsqrt(jnp.sum(xv * xv, axis=-1, keepdims=True) * (1.0 / d_real) + EPS) * g_ref[...]
        if rope is not None:
            c_ref, s_ref, r_ref = refs[pos:pos + 3]
            rot = jnp.dot(y, r_ref[...], precision=lax.Precision.HIGHEST, preferred_element_type=F32)
            y = y * c_ref[...] + rot * s_ref[...]
        refs[-1][...] = y.astype(BF16)

    vec = pl.BlockSpec((1, HEAD_PAD), lambda bi, hi: (0, 0))
    tab = pl.BlockSpec((None, s, HEAD_PAD), lambda bi, hi: (bi, 0, 0))
    in_specs, args = [_head_spec(s, blk0), vec], [x, gain]
    if extra is not None:
        e_blk = extra[1]
        in_specs.append(pl.BlockSpec((None, s, HEAD_PAD), lambda bi, hi: (bi, 0, e_blk)))
        args.append(extra[0])
    if rope is not None:
        in_specs += [tab, tab, pl.BlockSpec((HEAD_PAD, HEAD_PAD), lambda bi, hi: (0, 0))]
        args += [rope[0], rope[1], jnp.asarray(_rope_rot())]
    return pl.pallas_call(
        body, name=name, out_shape=jax.ShapeDtypeStruct((b, s, n_heads * HEAD_PAD), BF16), grid=(b, n_heads),
        in_specs=in_specs, out_specs=_head_spec(s, 0), compiler_params=_params(("parallel", "parallel")),
    )(*args)


def qk_prep_bwd(name, x, blk0, n_heads, d_real, gain, dxh, rope=None, extra=None, head_sum=False):
    b, s, _ = x.shape

    def body(*refs):
        x_ref, g_ref, dy_ref = refs[:3]
        pos = 3
        xv = x_ref[...]
        if extra is not None:
            xv = xv + refs[pos][...]
            pos += 1
        dy = dy_ref[...]
        if rope is not None:
            c_ref, s_ref, rt_ref = refs[pos:pos + 3]
            pos += 3
            dy = dy * c_ref[...] + jnp.dot(dy * s_ref[...], rt_ref[...], precision=lax.Precision.HIGHEST, preferred_element_type=F32)
        outs = refs[pos:]
        dx_ref, dg_ref = outs[0], outs[-1]
        r = lax.rsqrt(jnp.sum(xv * xv, axis=-1, keepdims=True) * (1.0 / d_real) + EPS)
        xhat = xv * r
        dxhat = dy * g_ref[...]
        dx = r * (dxhat - xhat * (jnp.sum(dxhat * xhat, axis=-1, keepdims=True) * (1.0 / d_real)))
        dx_ref[...] = dx.astype(dx_ref.dtype)
        part = jnp.sum(dy * xhat, axis=0, keepdims=True)
        first = jnp.logical_and(pl.program_id(0) == 0, pl.program_id(1) == 0)

        @pl.when(first)
        def _():
            dg_ref[...] = part

        @pl.when(jnp.logical_not(first))
        def _():
            dg_ref[...] += part

        if head_sum:
            hs_ref = outs[1]

            @pl.when(pl.program_id(1) == 0)
            def _():
                hs_ref[...] = dx

            @pl.when(pl.program_id(1) > 0)
            def _():
                hs_ref[...] += dx

    vec = pl.BlockSpec((1, HEAD_PAD), lambda bi, hi: (0, 0))
    tab = pl.BlockSpec((None, s, HEAD_PAD), lambda bi, hi: (bi, 0, 0))
    in_specs, args = [_head_spec(s, blk0), vec, _head_spec(s, 0)], [x, gain, dxh]
    if extra is not None:
        e_blk = extra[1]
        in_specs.append(pl.BlockSpec((None, s, HEAD_PAD), lambda bi, hi: (bi, 0, e_blk)))
        args.append(extra[0])
    if rope is not None:
        in_specs += [tab, tab, pl.BlockSpec((HEAD_PAD, HEAD_PAD), lambda bi, hi: (0, 0))]
        args += [rope[0], rope[1], jnp.asarray(_rope_rot().T.copy())]
    out_shape, out_specs = [jax.ShapeDtypeStruct((b, s, n_heads * HEAD_PAD), BF16)], [_head_spec(s, 0)]
    if head_sum:
        out_shape.append(jax.ShapeDtypeStruct((b, s, HEAD_PAD), F32))
        out_specs.append(tab)
    out_shape.append(jax.ShapeDtypeStruct((1, HEAD_PAD), F32))
    out_specs.append(vec)
    return pl.pallas_call(
        body, name=name, out_shape=tuple(out_shape), grid=(b, n_heads), in_specs=in_specs, out_specs=tuple(out_specs),
        compiler_params=_params(("arbitrary", "arbitrary")),
    )(*args)


def _swa_specs(v_blk0):
    q = pl.BlockSpec((None, BLOCK, GROUP_A * HEAD_PAD), lambda b, kv, n: (b, n, kv))
    kprev = pl.BlockSpec((None, BLOCK, HEAD_PAD), lambda b, kv, n: (b, jnp.maximum(n - 1, 0), kv))
    kcur = pl.BlockSpec((None, BLOCK, HEAD_PAD), lambda b, kv, n: (b, n, kv))
    vprev = pl.BlockSpec((None, BLOCK, HEAD_PAD), lambda b, kv, n: (b, jnp.maximum(n - 1, 0), v_blk0 + kv))
    vcur = pl.BlockSpec((None, BLOCK, HEAD_PAD), lambda b, kv, n: (b, n, v_blk0 + kv))
    pcol = pl.BlockSpec((None, BLOCK, 1), lambda b, kv, n: (b, n, 0))
    prow_prev = pl.BlockSpec((None, 1, BLOCK), lambda b, kv, n: (b, 0, jnp.maximum(n - 1, 0)))
    prow_cur = pl.BlockSpec((None, 1, BLOCK), lambda b, kv, n: (b, 0, n))
    smem = pl.BlockSpec(memory_space=pltpu.SMEM)
    return q, kprev, kcur, vprev, vcur, pcol, prow_prev, prow_cur, smem


def _swa_probs(q, kk, dist, valid, slope, sink):
    sc = lax.dot_general(q, kk, _NT, preferred_element_type=F32) * (HEAD_DIM_A ** -0.5)
    sc = sc - slope * dist
    sc = jnp.where(valid, sc, NEG)
    m = jnp.maximum(jnp.max(sc, axis=-1, keepdims=True), sink)
    e = jnp.exp(sc - m)
    es = jnp.exp(sink - m)
    inv = 1.0 / (jnp.sum(e, axis=-1, keepdims=True) + es)
    return e * inv, es * inv


def _swa_window(n, kp_ref, kc_ref, vp_ref, vc_ref, pc_ref, prp_ref, prc_ref):
    kk = jnp.concatenate([kp_ref[...], kc_ref[...]], axis=0)
    vv = jnp.concatenate([vp_ref[...], vc_ref[...]], axis=0).astype(BF16)
    dist = pc_ref[...] - jnp.concatenate([prp_ref[...], prc_ref[...]], axis=1)
    qi = lax.broadcasted_iota(jnp.int32, (BLOCK, 2 * BLOCK), 0) + BLOCK
    ki = lax.broadcasted_iota(jnp.int32, (BLOCK, 2 * BLOCK), 1)
    diff = qi - ki
    valid = (diff >= 0) & (diff < BLOCK) & ((n > 0) | (ki >= BLOCK))
    return kk, vv, dist, valid


def swa_fwd(name, q, k, vsrc, v_blk0, pos_col, pos_row, slopes, sinks, plan=None):
    b, s, _ = q.shape
    qs, kprev, kcur, vprev, vcur, pcol, prp, prc, smem = _swa_specs(v_blk0)
    grid = (b, N_KV_HEADS_A, s // BLOCK)
    p_ins, p_in_specs, p_outs, p_out_specs, p_scratch, semantics, at_start, at_end = _hosted(plan, 10, 1, grid, ("parallel",) * 3)

    def body(*refs):
        q_ref, kp_ref, kc_ref, vp_ref, vc_ref, pc_ref, prp_ref, prc_ref, sl_ref, sk_ref = refs[:10]
        o_ref = refs[10 + len(p_ins)]
        at_start(refs)
        kv, n = pl.program_id(1), pl.program_id(2)
        kk, vv, dist, valid = _swa_window(n, kp_ref, kc_ref, vp_ref, vc_ref, pc_ref, prp_ref, prc_ref)
        for g in range(GROUP_A):
            hd = kv * GROUP_A + g
            lanes = slice(g * HEAD_PAD, (g + 1) * HEAD_PAD)
            p, _ = _swa_probs(q_ref[:, lanes], kk, dist, valid, sl_ref[hd], sk_ref[hd])
            o_ref[:, lanes] = jnp.dot(p.astype(BF16), vv, preferred_element_type=F32).astype(BF16)
        at_end(refs)

    res = pl.pallas_call(
        body, name=name, out_shape=(jax.ShapeDtypeStruct(q.shape, BF16),) + tuple(p_outs), grid=grid,
        in_specs=[qs, kprev, kcur, vprev, vcur, pcol, prp, prc, smem, smem] + p_in_specs, out_specs=(qs,) + tuple(p_out_specs),
        scratch_shapes=p_scratch, compiler_params=_params(semantics),
    )(q, k, k, vsrc, vsrc, pos_col, pos_row, pos_row, slopes, sinks, *p_ins)
    return res[0], list(res[1:])


def swa_bwd(name, q, k, vsrc, v_blk0, pos_col, pos_row, slopes, sinks, do, plan=None):
    b, s, _ = q.shape
    qs, kprev, kcur, vprev, vcur, pcol, prp, prc, smem = _swa_specs(v_blk0)
    grid = (b, N_KV_HEADS_A, s // BLOCK)
    p_ins, p_in_specs, p_outs, p_out_specs, p_scratch, semantics, at_start, at_end = _hosted(plan, 11, 4, grid, ("arbitrary",) * 3)

    def body(*refs):
        at_start(refs)
        compute(*refs[:11], *refs[11 + len(p_ins):15 + len(p_ins)])
        at_end(refs)

    def compute(q_ref, kp_ref, kc_ref, vp_ref, vc_ref, pc_ref, prp_ref, prc_ref, sl_ref, sk_ref, do_ref, dq_ref, dk_ref, dv_ref, ds_ref):
        bi, kv, n = pl.program_id(0), pl.program_id(1), pl.program_id(2)
        kk, vv, dist, valid = _swa_window(n, kp_ref, kc_ref, vp_ref, vc_ref, pc_ref, prp_ref, prc_ref)

        @pl.when((bi == 0) & (kv == 0) & (n == 0))
        def _():
            ds_ref[...] = jnp.zeros_like(ds_ref)

        @pl.when(n == 0)
        def _():
            dk_ref[...] = jnp.zeros_like(dk_ref)
            dv_ref[...] = jnp.zeros_like(dv_ref)

        dkk = jnp.zeros((2 * BLOCK, HEAD_PAD), F32)
        dvv = jnp.zeros((2 * BLOCK, HEAD_PAD), F32)
        head_row = lax.broadcasted_iota(jnp.int32, (N_HEADS_A, LANES), 0)
        dsink = jnp.zeros((N_HEADS_A, LANES), F32)
        for g in range(GROUP_A):
            hd = kv * GROUP_A + g
            lanes = slice(g * HEAD_PAD, (g + 1) * HEAD_PAD)
            qg = q_ref[:, lanes]
            p, ps = _swa_probs(qg, kk, dist, valid, sl_ref[hd], sk_ref[hd])
            dob = do_ref[:, lanes].astype(BF16)
            dvv = dvv + lax.dot_general(p.astype(BF16), dob, _TN, preferred_element_type=F32)
            dp = lax.dot_general(dob, vv, _NT, preferred_element_type=F32)
            rs = jnp.sum(p * dp, axis=-1, keepdims=True)
            dsb = (p * (dp - rs) * (HEAD_DIM_A ** -0.5)).astype(BF16)
            dq_ref[:, lanes] = jnp.dot(dsb, kk, preferred_element_type=F32)
            dkk = dkk + lax.dot_general(dsb, qg, _TN, preferred_element_type=F32)
            dsink = dsink + jnp.where(head_row == hd, -jnp.sum(ps * rs), 0.0)
        ds_ref[...] += dsink

        @pl.when(n > 0)
        def _():
            start = pl.multiple_of((n - 1) * BLOCK, BLOCK)
            dk_ref[pl.ds(start, 2 * BLOCK), :] += dkk
            dv_ref[pl.ds(start, 2 * BLOCK), :] += dvv

        @pl.when(n == 0)
        def _():
            dk_ref[0:BLOCK, :] += dkk[BLOCK:, :]
            dv_ref[0:BLOCK, :] += dvv[BLOCK:, :]

    kv_full = pl.BlockSpec((None, s, HEAD_PAD), lambda bi, kv, n: (bi, 0, kv))
    kv_shape = jax.ShapeDtypeStruct(k.shape, F32)
    res = pl.pallas_call(
        body, name=name, grid=grid,
        out_shape=(jax.ShapeDtypeStruct(q.shape, F32), kv_shape, kv_shape, jax.ShapeDtypeStruct((N_HEADS_A, LANES), F32)) + tuple(p_outs),
        in_specs=[qs, kprev, kcur, vprev, vcur, pcol, prp, prc, smem, smem, qs] + p_in_specs,
        out_specs=(qs, kv_full, kv_full, pl.BlockSpec((N_HEADS_A, LANES), lambda bi, kv, n: (0, 0))) + tuple(p_out_specs),
        scratch_shapes=p_scratch, compiler_params=_params(semantics),
    )(q, k, k, vsrc, vsrc, pos_col, pos_row, pos_row, slopes, sinks, do, *p_ins)
    return res[0], res[1], res[2], res[3], list(res[4:])


MLA_T = 512


def _hosted(plan, n_in, n_out, grid, semantics):
    if plan is None:
        return [], [], [], [], [], semantics, (lambda refs: None), (lambda refs: None)
    ni, no = len(plan.ins), len(plan.out_shapes)

    def split(refs):
        return refs[n_in:n_in + ni], refs[n_in + ni + n_out:n_in + ni + n_out + no], refs[-2], refs[-1]

    def at_start(refs):
        @pl.when(functools.reduce(jnp.logical_and, [pl.program_id(ax) == 0 for ax in range(len(grid))]))
        def _():
            plan.start(*split(refs))

    def at_end(refs):
        @pl.when(functools.reduce(jnp.logical_and, [pl.program_id(ax) == g - 1 for ax, g in enumerate(grid)]))
        def _():
            plan.finish(*split(refs))

    return plan.ins, [_HBM] * ni, plan.out_shapes, [_HBM] * no, plan.scratch(), ("arbitrary",) * len(grid), at_start, at_end


def mla_fwd(name, q, k, v, plan=None):
    b, s, w = q.shape
    h = w // HEAD_PAD
    t = MLA_T if s % MLA_T == 0 else BLOCK
    nb = s // t
    scale = QK_DIM_B ** -0.5
    p_ins, p_in_specs, p_outs, p_out_specs, p_scratch, semantics, at_start, at_end = _hosted(
        plan, 3, 3, (b, h), ("parallel", "parallel"))

    def body(*refs):
        q_ref, k_ref, v_ref = refs[:3]
        o_ref, ob_ref, lse_ref = refs[3 + len(p_ins):6 + len(p_ins)]
        at_start(refs)
        causal = lax.broadcasted_iota(jnp.int32, (t, t), 0) <= lax.broadcasted_iota(jnp.int32, (t, t), 1)
        for i in range(nb):
            q_i = q_ref[i * t:(i + 1) * t, :]

            def step(j, carry, q_i=q_i, diagonal=False):
                m, l, acc = carry
                rows = slice(j * t, (j + 1) * t) if diagonal else pl.ds(pl.multiple_of(j * t, t), t)
                st = lax.dot_general(k_ref[rows, :], q_i, _NT, preferred_element_type=F32) * scale
                if diagonal:
                    st = jnp.where(causal, st, NEG)
                m_new = jnp.maximum(m, jnp.max(st, axis=0, keepdims=True))
                a = jnp.exp(m - m_new)
                p = jnp.exp(st - m_new)
                l = a * l + jnp.sum(p, axis=0, keepdims=True)
                acc = a * acc + lax.dot_general(v_ref[rows, :], p.astype(BF16), _TN, preferred_element_type=F32)
                return m_new, l, acc

            carry = (jnp.full((1, t), NEG, F32), jnp.zeros((1, t), F32), jnp.zeros((HEAD_PAD, t), F32))
            if i > 0:
                carry = lax.fori_loop(0, i, step, carry)
            m, l, acc = step(i, carry, diagonal=True)
            o_i = (acc / l).T
            o_ref[i * t:(i + 1) * t, :] = o_i
            ob_ref[i * t:(i + 1) * t, :] = o_i.astype(BF16)
            lse_ref[i] = m + jnp.log(l)
        at_end(refs)

    hs = _head_spec(s, 0)
    ls = pl.BlockSpec((None, None, nb, 1, t), lambda bi, hi: (bi, hi, 0, 0, 0))
    res = pl.pallas_call(
        body, name=name, grid=(b, h),
        out_shape=(jax.ShapeDtypeStruct((b, s, w), F32), jax.ShapeDtypeStruct((b, s, w), BF16),
                   jax.ShapeDtypeStruct((b, h, nb, 1, t), F32)) + tuple(p_outs),
        in_specs=[hs, hs, hs] + p_in_specs, out_specs=(hs, hs, ls) + tuple(p_out_specs),
        scratch_shapes=p_scratch, compiler_params=_params(semantics),
    )(q, k, v, *p_ins)
    return res[0], res[1], res[2], list(res[3:])


def mla_bwd(name, q, k, v, o, do, lse, plan=None):
    b, s, w = q.shape
    h = w // HEAD_PAD
    _, _, nb, _, t = lse.shape
    scale = QK_DIM_B ** -0.5
    p_ins, p_in_specs, p_outs, p_out_specs, p_scratch, semantics, at_start, at_end = _hosted(
        plan, 6, 3, (b, h), ("parallel", "parallel"))

    def body(*refs):
        q_ref, k_ref, v_ref, o_ref, do_ref, lse_ref = refs[:6]
        dq_ref, dk_ref, dv_ref = refs[6 + len(p_ins):9 + len(p_ins)]
        dv_acc = refs[9 + len(p_ins) + len(p_outs)]
        at_start(refs)
        causal = lax.broadcasted_iota(jnp.int32, (t, t), 0) <= lax.broadcasted_iota(jnp.int32, (t, t), 1)
        dk_ref[...] = jnp.zeros_like(dk_ref)
        dv_acc[...] = jnp.zeros_like(dv_acc)
        for i in range(nb):
            q_i = q_ref[i * t:(i + 1) * t, :]
            do_i = do_ref[i * t:(i + 1) * t, :]
            delta = jnp.sum((o_ref[i * t:(i + 1) * t, :] * do_i).T, axis=0, keepdims=True)
            do_b = do_i.astype(BF16)
            lse_i = lse_ref[i]

            def step(j, dqt, q_i=q_i, do_b=do_b, delta=delta, lse_i=lse_i, diagonal=False):
                rows = slice(j * t, (j + 1) * t) if diagonal else pl.ds(pl.multiple_of(j * t, t), t)
                k_j = k_ref[rows, :]
                st = lax.dot_general(k_j, q_i, _NT, preferred_element_type=F32) * scale
                if diagonal:
                    st = jnp.where(causal, st, NEG)
                pt = jnp.exp(st - lse_i)
                dpt = lax.dot_general(v_ref[rows, :], do_b, _NT, preferred_element_type=F32)
                dst = (pt * (dpt - delta) * scale).astype(BF16)
                dv_acc[rows, :] += jnp.dot(pt.astype(BF16), do_b, preferred_element_type=F32)
                dk_ref[rows, :] += jnp.dot(dst, q_i, preferred_element_type=F32)
                return dqt + lax.dot_general(k_j, dst, _TN, preferred_element_type=F32)

            dqt = jnp.zeros((HEAD_PAD, t), F32)
            if i > 0:
                dqt = lax.fori_loop(0, i, step, dqt)
            dq_ref[i * t:(i + 1) * t, :] = step(i, dqt, diagonal=True).T
        dv_ref[...] = dv_acc[...].astype(BF16)
        at_end(refs)

    hs = _head_spec(s, 0)
    ls = pl.BlockSpec((None, None, nb, 1, t), lambda bi, hi: (bi, hi, 0, 0, 0))
    res = pl.pallas_call(
        body, name=name, grid=(b, h),
        out_shape=(jax.ShapeDtypeStruct((b, s, w), F32), jax.ShapeDtypeStruct((b, s, w), F32),
                   jax.ShapeDtypeStruct((b, s, w), BF16)) + tuple(p_outs),
        in_specs=[hs, hs, hs, hs, hs, ls] + p_in_specs, out_specs=(hs, hs, hs) + tuple(p_out_specs),
        scratch_shapes=[pltpu.VMEM((s, HEAD_PAD), F32)] + p_scratch,
        compiler_params=_params(semantics),
    )(q, k, v, o, do, lse, *p_ins)
    return res[0], res[1], res[2], list(res[3:])


_HBM = pl.BlockSpec(memory_space=pltpu.HBM)


def _place():
    x, y, c = lax.axis_index("x"), lax.axis_index("y"), lax.axis_index("c")
    chips = [(1 - x, y), (x, 1 - y), (1 - x, 1 - y)]
    return x, y, c, chips


def _remote(src, dst, send_sem, recv_sem, dev):
    return pltpu.make_async_remote_copy(src_ref=src, dst_ref=dst, send_sem=send_sem, recv_sem=recv_sem,
                                        device_id=dev, device_id_type=MESH)


class CommPlan:
    def __init__(self, ins, out_shapes, n_sems, start, finish, aliases=None):
        self.ins, self.out_shapes, self.n_sems = list(ins), list(out_shapes), n_sems
        self.start, self.finish, self.aliases = start, finish, dict(aliases or {})

    def scratch(self):
        return [pltpu.SemaphoreType.DMA((self.n_sems,)), pltpu.SemaphoreType.DMA((self.n_sems,))]


def comm_call(name, plan):
    ni, no = len(plan.ins), len(plan.out_shapes)

    def body(*refs):
        ins, outs, (send_sems, recv_sems) = refs[:ni], refs[ni:ni + no], refs[ni + no:]
        plan.start(ins, outs, send_sems, recv_sems)
        plan.finish(ins, outs, send_sems, recv_sems)

    return pl.pallas_call(
        body, name=name, out_shape=tuple(plan.out_shapes), in_specs=[_HBM] * ni, out_specs=tuple([_HBM] * no),
        input_output_aliases=plan.aliases, scratch_shapes=plan.scratch(),
    )(*plan.ins)


def gather_plan(packs, l):
    nk = len(packs)

    def parts(p_refs, o_refs, ss, rs):
        x, y, c, chips = _place()
        s_me = 2 * x + y
        sibling = (x, y, 1 - c)
        first = [_remote(p_refs[k].at[l], o_refs[k].at[s_me], ss.at[j * nk + k], rs.at[j * nk + k], (cx, cy, c))
                 for j, (cx, cy) in enumerate(chips) for k in range(nk)]
        own = [_remote(p_refs[k].at[l], o_refs[k].at[s_me], ss.at[6 * nk + k], rs.at[6 * nk + k], sibling) for k in range(nk)]
        return c, chips, sibling, first, own

    def start(p_refs, o_refs, ss, rs):
        c, _, _, first, own = parts(p_refs, o_refs, ss, rs)
        for cp in own:
            cp.start()

        @pl.when(c == l)
        def _():
            for cp in first:
                cp.start()

    def finish(p_refs, o_refs, ss, rs):
        c, chips, sibling, first, own = parts(p_refs, o_refs, ss, rs)

        @pl.when(c == l)
        def _():
            passed = []
            for j, (cx, cy) in enumerate(chips):
                for k in range(nk):
                    blk = o_refs[k].at[2 * cx + cy]
                    _remote(blk, blk, ss.at[j * nk + k], rs.at[j * nk + k], (cx, cy, c)).wait_recv()
                    fwd = _remote(blk, blk, ss.at[(3 + j) * nk + k], rs.at[(3 + j) * nk + k], sibling)
                    fwd.start()
                    passed.append(fwd)
            for cp in first + passed:
                cp.wait_send()

        @pl.when(c != l)
        def _():
            for j, (cx, cy) in enumerate(chips):
                for k in range(nk):
                    blk = o_refs[k].at[2 * cx + cy]
                    _remote(blk, blk, ss.at[(3 + j) * nk + k], rs.at[(3 + j) * nk + k], sibling).wait_recv()

        for cp in own:
            cp.wait()

    outs = [jax.ShapeDtypeStruct((N_SHARDS,) + p.shape[1:], p.dtype) for p in packs]
    return CommPlan(packs, outs, 7 * nk, start, finish)


def swap_plan(grads, l):
    nk = len(grads)

    def copies(g_refs, o_refs, ss, rs):
        x, y, c, _ = _place()
        return c, [_remote(g_refs[k], o_refs[k], ss.at[k], rs.at[k], (x, y, 1 - c)) for k in range(nk)]

    def start(g_refs, o_refs, ss, rs):
        c, cps = copies(g_refs, o_refs, ss, rs)

        @pl.when(c != l)
        def _():
            for cp in cps:
                cp.start()

    def finish(g_refs, o_refs, ss, rs):
        c, cps = copies(g_refs, o_refs, ss, rs)

        @pl.when(c != l)
        def _():
            for cp in cps:
                cp.wait_send()

        @pl.when(c == l)
        def _():
            for cp in cps:
                cp.wait_recv()

    return CommPlan(grads, [jax.ShapeDtypeStruct(g.shape, g.dtype) for g in grads], nk, start, finish)


def scatter_plan(parts, l):
    nk = len(parts)

    def copies(p_refs, o_refs, ss, rs):
        x, y, c, chips = _place()
        s_me = 2 * x + y
        sends = [_remote(p_refs[k].at[2 * cx + cy], o_refs[k].at[s_me], ss.at[j * nk + k], rs.at[j * nk + k], (cx, cy, c))
                 for j, (cx, cy) in enumerate(chips) for k in range(nk)]
        return c, chips, sends

    def start(p_refs, o_refs, ss, rs):
        c, _, sends = copies(p_refs, o_refs, ss, rs)

        @pl.when(c == l)
        def _():
            for cp in sends:
                cp.start()

    def finish(p_refs, o_refs, ss, rs):
        c, chips, sends = copies(p_refs, o_refs, ss, rs)

        @pl.when(c == l)
        def _():
            for j, (cx, cy) in enumerate(chips):
                for k in range(nk):
                    slot = o_refs[k].at[2 * cx + cy]
                    _remote(slot, slot, ss.at[j * nk + k], rs.at[j * nk + k], (cx, cy, c)).wait_recv()
            for cp in sends:
                cp.wait_send()

    return CommPlan(parts, [jax.ShapeDtypeStruct(p.shape, p.dtype) for p in parts], 3 * nk, start, finish)


def share_plan(finals):
    flat = [(l, f) for l, fs in enumerate(finals) for f in fs]

    def copies(f_refs, o_refs, ss, rs):
        x, y, c, _ = _place()
        return c, [_remote(f_refs[i], o_refs[i], ss.at[i], rs.at[i], (x, y, 1 - c)) for i in range(len(flat))]

    def start(f_refs, o_refs, ss, rs):
        c, cps = copies(f_refs, o_refs, ss, rs)
        for l in range(len(finals)):
            @pl.when(c == l)
            def _(l=l):
                for cp, (lf, _) in zip(cps, flat):
                    if lf == l:
                        cp.start()

    def finish(f_refs, o_refs, ss, rs):
        c, cps = copies(f_refs, o_refs, ss, rs)
        for l in range(len(finals)):
            @pl.when(c == l)
            def _(l=l):
                for cp, (lf, _) in zip(cps, flat):
                    if lf == l:
                        cp.wait_send()

            @pl.when(c != l)
            def _(l=l):
                for cp, (lf, _) in zip(cps, flat):
                    if lf == l:
                        cp.wait_recv()

    arrays = [f for _, f in flat]
    return CommPlan(arrays, [jax.ShapeDtypeStruct(f.shape, f.dtype) for f in arrays], len(flat), start, finish,
                    aliases={i: i for i in range(len(flat))})


def add_pair(name, own, other, tr=256):
    n, cols = own.shape
    tr = _tile(n, tr, 16)

    def body(g_ref, o_ref, out_ref):
        out_ref[...] = (g_ref[...].astype(F32) + o_ref[...].astype(F32)).astype(out_ref.dtype)

    row = pl.BlockSpec((tr, cols), lambda i: (i, 0))
    return pl.pallas_call(
        body, name=name, out_shape=jax.ShapeDtypeStruct((n, cols), BF16), grid=(n // tr,), in_specs=[row, row], out_specs=row,
        compiler_params=_params(("parallel",)),
    )(own, other)


def sum_slots(name, slots, part, shard, tr=256):
    ns, r, cols = slots.shape
    tr = _tile(r, tr, 16)

    def body(s_idx, s_ref, own_ref, o_ref):
        total = own_ref[...].astype(F32)
        for k in range(1, ns):
            total = total + s_ref[(s_idx[0] + k) % ns].astype(F32)
        o_ref[...] = total

    return pl.pallas_call(
        body, name=name, out_shape=jax.ShapeDtypeStruct((r, cols), F32),
        grid_spec=pltpu.PrefetchScalarGridSpec(
            num_scalar_prefetch=1, grid=(r // tr,),
            in_specs=[pl.BlockSpec((ns, tr, cols), lambda i, p: (0, i, 0)), pl.BlockSpec((None, tr, cols), lambda i, p: (p[0], i, 0))],
            out_specs=pl.BlockSpec((tr, cols), lambda i, p: (i, 0))),
        compiler_params=_params(("parallel",)),
    )(shard, slots, part)


def small_allreduce(vec):
    r, cols = vec.shape

    def body(v_ref, o_ref, buf, send_sems, recv_sems):
        x, y, c, _ = _place()
        me = 4 * x + 2 * y + c
        buf[me] = v_ref[...]
        peers = []
        for k in range(1, N_DEV):
            px = (1 - x) if (k & 4) else x
            py = (1 - y) if (k & 2) else y
            pc = (1 - c) if (k & 1) else c
            peers.append((px, py, pc))
        sends = [_remote(buf.at[me], buf.at[me], send_sems.at[k], recv_sems.at[k], peer) for k, peer in enumerate(peers)]
        for cp in sends:
            cp.start()
        for k, (px, py, pc) in enumerate(peers):
            slot = buf.at[4 * px + 2 * py + pc]
            _remote(slot, slot, send_sems.at[k], recv_sems.at[k], (px, py, pc)).wait_recv()
        for cp in sends:
            cp.wait_send()
        total = buf[0]
        for k in range(1, N_DEV):
            total = total + buf[k]
        o_ref[...] = total

    return pl.pallas_call(
        body, name="small_allreduce", out_shape=jax.ShapeDtypeStruct((r, cols), F32),
        in_specs=[pl.BlockSpec(memory_space=pltpu.VMEM)], out_specs=pl.BlockSpec(memory_space=pltpu.VMEM),
        scratch_shapes=[pltpu.VMEM((N_DEV, r, cols), F32), pltpu.SemaphoreType.DMA((N_DEV - 1,)), pltpu.SemaphoreType.DMA((N_DEV - 1,))],
    )(vec)


ARG_NAMES = (("x", "positions") + WEIGHT_NAMES + ("loss_target",) + tuple("m_" + n for n in WEIGHT_NAMES)
             + tuple("v_" + n for n in WEIGHT_NAMES))

_C_KA = N_HEADS_A * HEAD_PAD
_C_VA = _C_KA + N_KV_HEADS_A * HEAD_PAD
_C_CQ = _C_VA + N_KV_HEADS_A * HEAD_PAD
_C_CKV = _C_CQ + Q_LORA_RANK
_C_KR = _C_CKV + KV_LORA_RANK
_C_GA = _C_KR + HEAD_PAD
_R_KA = WIDTH_A
_R_VA = _R_KA + KV_WIDTH_A
_R_CQ = _R_VA + KV_WIDTH_A
_R_KR = _R_CQ + Q_LORA_RANK + KV_LORA_RANK
_R_GA = _R_KR + QK_ROPE_DIM


def _pad_head_cols(w, n_heads):
    r, dh = w.shape[0], w.shape[1] // n_heads
    return jnp.pad(w.reshape(r, n_heads, dh), ((0, 0), (0, 0), (0, HEAD_PAD - dh))).reshape(r, n_heads * HEAD_PAD)


def _unpad_head_cols(w, n_heads, dh):
    r = w.shape[0]
    return w.reshape(r, n_heads, HEAD_PAD)[:, :, :dh].reshape(r, n_heads * dh)


def _pad_head_rows(w, n_heads):
    dh, c = w.shape[0] // n_heads, w.shape[1]
    return jnp.pad(w.reshape(n_heads, dh, c), ((0, 0), (0, HEAD_PAD - dh), (0, 0))).reshape(n_heads * HEAD_PAD, c)


def _unpad_head_rows(w, n_heads, dh):
    c = w.shape[1]
    return w.reshape(n_heads, HEAD_PAD, c)[:, :dh].reshape(n_heads * dh, c)


def _join_cols(shards):
    ns, r, c = shards.shape
    return shards.transpose(1, 0, 2).reshape(r, ns * c)


def _split_cols(mat):
    r, c4 = mat.shape
    return mat.reshape(r, N_SHARDS, c4 // N_SHARDS).transpose(1, 0, 2)


def _ffn_fwd(tag, x, gain, wcol, wdown, plan=None):
    nb = rms_fwd(tag + "_rms", x, gain)
    gu, h, carried = ffn_up_act(tag + "_up", nb, wcol, 0, plan=plan)
    out = gmm_down(tag + "_down", h, wdown, 0, x, 0.5)
    return out, (x, nb, gu, h), carried


def _ffn_bwd(tag, saved, gain, wcol, wdown, dout, doutb, plans=None):
    plans = plans or {}
    x, nb, gu, h = saved
    carried = {}
    dgu, carried["dh"] = ffn_down_dact(tag + "_dh", doutb, wdown, gu, gu.shape[-1], 0, 0.5, plan=plans.get("dh"))
    gdown = gmm_down_dw(tag + "_dwd", h, doutb, 0.5)
    if plans.get("dwgu") is not None:
        gcol, carried["dwgu"] = gmm_up_dw(tag + "_dwgu", nb, dgu, plan=plans["dwgu"])
    else:
        gcol = gmm_up_dw(tag + "_dwgu", nb, dgu)
    if plans.get("dn") is not None:
        dn, carried["dn"] = gmm_up_dx(tag + "_dn", dgu, wcol, x.shape[1], 0, plan=plans["dn"])
    else:
        dn = gmm_up_dx(tag + "_dn", dgu, wcol, x.shape[1], 0)
    dx, dxb, dgain = rms_bwd(tag + "_drms", x, gain, dn, dres=dout)
    return dx, dxb, dgain, gcol, gdown, carried


def _pad_gain(g):
    return jnp.pad(g, ((0, 0), (0, HEAD_PAD - g.shape[1])))


def _mixer_fwd(tag, x1, sm, w, wo, aux, plans=None):
    plans = plans or {}
    b, s = aux["b"], aux["s"]
    t, d = x1.shape
    hb = rms_fwd(tag + "_rms", x1, sm["mix_norm"])
    carried = {}
    if plans.get("proj") is not None:
        proj, carried["proj"] = mm_nn(tag + "_proj", hb, w["win"], tm=512, tn=2048, tk=1024, plan=plans["proj"])
    else:
        proj = mm_nn(tag + "_proj", hb, w["win"], tm=512, tn=2048, tk=1024)
    proj3 = proj.reshape(b, s, proj.shape[1])
    gains = {n: _pad_gain(sm[n]) for n in ("swa_q_norm", "swa_k_norm", "mla_q_norm", "mla_k_norm")}
    cqn = rms_fwd(tag + "_rms_cq", proj, sm["mla_q_lora_norm"], tm=1024, col_blk=_C_CQ // Q_LORA_RANK)
    ckvn = rms_fwd(tag + "_rms_ckv", proj, sm["mla_kv_lora_norm"], tm=1024, col_blk=_C_CKV // KV_LORA_RANK)
    qb_raw = mm_nn(tag + "_uq", cqn, w["wuq"], tm=1024).reshape(b, s, N_HEADS_B * HEAD_PAD)
    kb_raw = mm_nn(tag + "_uk", ckvn, w["wk"], tm=1024).reshape(b, s, N_HEADS_B * HEAD_PAD)
    vb = mm_nn(tag + "_uv", ckvn, w["wv"], tm=1024, out_dtype=BF16).reshape(b, s, N_HEADS_B * HEAD_PAD)
    qah = qk_prep_fwd(tag + "_qa_norm", proj3, 0, N_HEADS_A, HEAD_DIM_A, gains["swa_q_norm"])
    kah = qk_prep_fwd(tag + "_ka_norm", proj3, _C_KA // HEAD_PAD, N_KV_HEADS_A, HEAD_DIM_A, gains["swa_k_norm"])
    sinks = sm["swa_sinks"].reshape(-1)
    oab, carried["swa"] = swa_fwd(tag + "_swa", qah, kah, proj3, _C_VA // HEAD_PAD, aux["pos_col"], aux["pos_row"], aux["slopes"],
                                  sinks, plan=plans.get("swa"))
    kr = (proj3, _C_KR // HEAD_PAD)
    qbh = qk_prep_fwd(tag + "_qb_norm", qb_raw, 0, N_HEADS_B, QK_DIM_B, gains["mla_q_norm"], aux["rope"])
    kbh = qk_prep_fwd(tag + "_kb_norm", kb_raw, 0, N_HEADS_B, QK_DIM_B, gains["mla_k_norm"], aux["rope"], extra=kr)
    ob, obb, lse, carried["mla"] = mla_fwd(tag + "_mla", qbh, kbh, vb, plans.get("mla"))
    oab2, obb2 = oab.reshape(t, -1), obb.reshape(t, -1)
    ya = mm_nn(tag + "_branch_a", oab2, w["wa"], tm=1024)
    yb = mm_nn(tag + "_branch_b", obb2, w["wb"], tm=1024)
    mg = gate_fwd(tag + "_gate", proj, ya, yb, _C_GA, _C_GA + d)
    x2 = gmm_rows(tag + "_out", mg, wo, x1)
    saved = dict(x1=x1, hb=hb, proj=proj, cqn=cqn, ckvn=ckvn, qb_raw=qb_raw, kb_raw=kb_raw, vb=vb, qah=qah, kah=kah,
                 qbh=qbh, kbh=kbh, ob=ob, lse=lse, oab2=oab2, obb2=obb2, ya=ya, yb=yb, mg=mg, gains=gains)
    return x2, saved, carried


def _mixer_bwd(tag, sv, sm, w, wo, aux, dx2, dx2b, plans=None):
    plans = plans or {}
    carried = {}
    b, s = aux["b"], aux["s"]
    t, d = dx2.shape
    proj, gains = sv["proj"], sv["gains"]
    proj3 = proj.reshape(b, s, proj.shape[1])
    kr = (proj3, _C_KR // HEAD_PAD)
    dmg = gmm_rows_dx(tag + "_d_out", dx2b, wo)
    dwo = gmm_rows_dw(tag + "_dw_out", sv["mg"], dx2b)
    dya, dyb, dga, dgb = gate_bwd(tag + "_dgate", proj, sv["ya"], sv["yb"], dmg, _C_GA, _C_GA + d)
    dwa = mm_tn(tag + "_dw_branch_a", sv["oab2"], dya, out_dtype=BF16)
    dwb = mm_tn(tag + "_dw_branch_b", sv["obb2"], dyb, out_dtype=BF16)
    doa = mm_nt(tag + "_d_branch_a", dya, w["wa"], tm=1024).reshape(b, s, -1)
    dob = mm_nt(tag + "_d_branch_b", dyb, w["wb"], tm=1024).reshape(b, s, -1)
    sinks = sm["swa_sinks"].reshape(-1)
    dqah, dkah, dva, dsinks, carried["dswa"] = swa_bwd(tag + "_dswa", sv["qah"], sv["kah"], proj3, _C_VA // HEAD_PAD, aux["pos_col"],
                                                       aux["pos_row"], aux["slopes"], sinks, doa, plan=plans.get("dswa"))
    dqa_raw, dg_swa_q = qk_prep_bwd(tag + "_dqa_norm", proj3, 0, N_HEADS_A, HEAD_DIM_A, gains["swa_q_norm"], dqah)
    dka_raw, dg_swa_k = qk_prep_bwd(tag + "_dka_norm", proj3, _C_KA // HEAD_PAD, N_KV_HEADS_A, HEAD_DIM_A, gains["swa_k_norm"], dkah)
    dqbh, dkbh, dvb, carried["dmla"] = mla_bwd(tag + "_dmla", sv["qbh"], sv["kbh"], sv["vb"], sv["ob"], dob, sv["lse"], plans.get("dmla"))
    dqb_raw, dg_mla_q = qk_prep_bwd(tag + "_dqb_norm", sv["qb_raw"], 0, N_HEADS_B, QK_DIM_B, gains["mla_q_norm"], dqbh, aux["rope"])
    dkb_raw, dkr_sum, dg_mla_k = qk_prep_bwd(tag + "_dkb_norm", sv["kb_raw"], 0, N_HEADS_B, QK_DIM_B, gains["mla_k_norm"], dkbh,
                                             aux["rope"], extra=kr, head_sum=True)
    dq_tok, dk_tok, dv_tok = dqb_raw.reshape(t, -1), dkb_raw.reshape(t, -1), dvb.reshape(t, -1)
    dwuq = mm_tn(tag + "_dw_uq", sv["cqn"], dq_tok, tk=1024, out_dtype=BF16)
    dwk = mm_tn(tag + "_dw_uk", sv["ckvn"], dk_tok, tk=1024, out_dtype=BF16)
    dwv = mm_tn(tag + "_dw_uv", sv["ckvn"], dv_tok, tk=1024, out_dtype=BF16)
    dcqn = mm_nt(tag + "_d_uq", dq_tok, w["wuq"], tm=1024)
    dckvn = mm_nt(tag + "_d_uv", dv_tok, w["wv"], tm=1024, res=mm_nt(tag + "_d_uk", dk_tok, w["wk"], tm=1024))
    dcq, dg_q_lora = rms_bwd(tag + "_drms_cq", proj, sm["mla_q_lora_norm"], dcqn, want_f32=False, tm=1024, col_blk=_C_CQ // Q_LORA_RANK)
    dckv, dg_kv_lora = rms_bwd(tag + "_drms_ckv", proj, sm["mla_kv_lora_norm"], dckvn, want_f32=False, tm=1024,
                               col_blk=_C_CKV // KV_LORA_RANK)
    dproj = jnp.concatenate([dqa_raw.reshape(t, -1), dka_raw.reshape(t, -1), dva.reshape(t, -1).astype(BF16), dcq, dckv,
                             dkr_sum.reshape(t, HEAD_PAD).astype(BF16), dga, dgb], axis=1)
    dwin = mm_tn(tag + "_dw_in", sv["hb"], dproj, tm=1024, tn=2048, tk=1024, out_dtype=BF16)
    dh = mm_nt(tag + "_d_in", dproj, w["win"], tm=512, tn=1024, tk=2048)
    dx1, dx1b, dg_mix = rms_bwd(tag + "_drms", sv["x1"], sm["mix_norm"], dh, dres=dx2)
    wgrads = dict(win=dwin, wuq=dwuq, wk=dwk, wv=dwv, wa=dwa, wb=dwb, wo=dwo)
    sgrads = dict(mix_norm=dg_mix, swa_q_norm=dg_swa_q[:, :HEAD_DIM_A], swa_k_norm=dg_swa_k[:, :HEAD_DIM_A],
                  swa_sinks=dsinks[:, 0].reshape(1, -1), mla_q_lora_norm=dg_q_lora, mla_kv_lora_norm=dg_kv_lora,
                  mla_q_norm=dg_mla_q[:, :QK_DIM_B], mla_k_norm=dg_mla_k[:, :QK_DIM_B])
    return dx1, dx1b, wgrads, sgrads, carried


def _layer_weights(win4, uq4, ukv4, branch4):
    wr = _join_cols(win4)
    kr = jnp.pad(wr[:, _R_KR:_R_GA], ((0, 0), (QK_NOPE_DIM, HEAD_PAD - QK_DIM_B)))
    win = jnp.concatenate([_pad_head_cols(wr[:, :_R_KA], N_HEADS_A), _pad_head_cols(wr[:, _R_KA:_R_VA], N_KV_HEADS_A),
                           _pad_head_cols(wr[:, _R_VA:_R_CQ], N_KV_HEADS_A), wr[:, _R_CQ:_R_KR], kr, wr[:, _R_GA:]], axis=1)
    ukv = _join_cols(ukv4)
    ukv3 = ukv.reshape(ukv.shape[0], N_HEADS_B, QK_NOPE_DIM + V_DIM_B)
    r = branch4.shape[1] // 2
    return dict(win=win, wuq=_pad_head_cols(_join_cols(uq4), N_HEADS_B),
                wk=_pad_head_cols(ukv3[:, :, :QK_NOPE_DIM].reshape(ukv.shape[0], -1), N_HEADS_B),
                wv=_pad_head_cols(ukv3[:, :, QK_NOPE_DIM:].reshape(ukv.shape[0], -1), N_HEADS_B),
                wa=_pad_head_rows(_join_cols(branch4[:, :r]), N_HEADS_A), wb=_pad_head_rows(_join_cols(branch4[:, r:]), N_HEADS_B))


def _mixer_grad_shards(g):
    dw = g["win"]
    win_ref = jnp.concatenate([_unpad_head_cols(dw[:, :_C_KA], N_HEADS_A, HEAD_DIM_A),
                               _unpad_head_cols(dw[:, _C_KA:_C_VA], N_KV_HEADS_A, HEAD_DIM_A),
                               _unpad_head_cols(dw[:, _C_VA:_C_CQ], N_KV_HEADS_A, HEAD_DIM_A), dw[:, _C_CQ:_C_KR],
                               dw[:, _C_KR + QK_NOPE_DIM:_C_KR + QK_DIM_B], dw[:, _C_GA:]], axis=1)
    rk = g["wk"].shape[0]
    ukv = jnp.concatenate([g["wk"].reshape(rk, N_HEADS_B, HEAD_PAD)[:, :, :QK_NOPE_DIM],
                           g["wv"].reshape(rk, N_HEADS_B, HEAD_PAD)[:, :, :V_DIM_B]], axis=2).reshape(rk, -1)
    return (_split_cols(win_ref), _split_cols(_unpad_head_cols(g["wuq"], N_HEADS_B, QK_DIM_B)), _split_cols(ukv),
            jnp.concatenate([_split_cols(_unpad_head_rows(g["wa"], N_HEADS_A, HEAD_DIM_A)),
                             _split_cols(_unpad_head_rows(g["wb"], N_HEADS_B, V_DIM_B))], axis=1))


BUFFER_TAGS = ("col1", "col2", "down1", "down2", "wo", "win", "uq", "ukv", "branch")
FIRST_GATHER = ("col1", "down1")
GATHER_CARRIERS = {(0, "ffn1_up"): (0, ("wo", "win", "uq", "ukv", "branch")), (0, "proj"): (1, ("col1",)),
                   (0, "swa"): (1, ("down1", "wo", "uq", "ukv")), (0, "mla"): (0, ("col2", "down2")),
                   (0, "ffn2_up"): (1, ("win", "branch")), (1, "ffn1_up"): (1, ("col2",)), (1, "swa"): (1, ("down2",))}
SCATTER_CARRIERS = {(0, "ffn2_dh"): (1, ("col1",)), (0, "ffn2_dwgu"): (1, ("col2",)), (0, "ffn2_dn"): (1, ("down1", "down2", "wo")),
                    (0, "dswa"): (1, ("win", "uq", "ukv", "branch")), (0, "dmla"): (0, ("col2", "down2")),
                    (0, "ffn1_dh"): (0, ("wo", "win", "uq", "ukv", "branch"))}
LAST_SCATTER = ("col1", "down1")


def _pick(by_tag, tags):
    return [by_tag[t] for t in tags]


def _train_step(x, positions, target, packs, small, shard):
    depth = 2
    b, s, d = x.shape
    t = b * s
    posf = positions.astype(F32)
    half = QK_ROPE_DIM // 2
    inv_freq = ROPE_BASE ** (-jnp.arange(half, dtype=F32) / half)
    ang = posf[..., None] * inv_freq
    cos, sin = jnp.cos(ang), jnp.sin(ang)
    tail = HEAD_PAD - QK_DIM_B
    rope = (jnp.concatenate([jnp.ones((b, s, QK_NOPE_DIM), F32), cos, cos, jnp.ones((b, s, tail), F32)], axis=-1),
            jnp.concatenate([jnp.zeros((b, s, QK_NOPE_DIM), F32), sin, sin, jnp.zeros((b, s, tail), F32)], axis=-1))
    slopes = jnp.exp2(-8.0 * (jnp.arange(N_HEADS_A, dtype=F32) + 1.0) / N_HEADS_A)
    aux = dict(b=b, s=s, pos_col=posf.reshape(b, s, 1), pos_row=posf.reshape(b, 1, s), rope=rope, slopes=slopes)

    def sm_of(l):
        return {n: small[n][l:l + 1] for n in SMALL_NAMES}

    by_tag = dict(zip(BUFFER_TAGS, packs))
    wts, parts, slots = [{} for _ in range(depth)], [{} for _ in range(depth)], [{} for _ in range(depth)]
    saved, sgrads = [None] * depth, [None] * depth

    def gather_piece(l, kern):
        if (l, kern) not in GATHER_CARRIERS:
            return None
        data_l, tags = GATHER_CARRIERS[(l, kern)]
        return gather_plan(_pick(by_tag, tags), data_l)

    def gathered(l, kern, arrays):
        if (l, kern) in GATHER_CARRIERS:
            data_l, tags = GATHER_CARRIERS[(l, kern)]
            wts[data_l].update(zip(tags, arrays))

    def scatter_piece(l, kern):
        if (l, kern) not in SCATTER_CARRIERS:
            return None
        data_l, tags = SCATTER_CARRIERS[(l, kern)]
        return scatter_plan(_pick(parts[data_l], tags), data_l)

    def scattered(l, kern, arrays):
        if (l, kern) in SCATTER_CARRIERS:
            data_l, tags = SCATTER_CARRIERS[(l, kern)]
            slots[data_l].update(zip(tags, arrays))

    def chip_sums(l, tags, mine):
        theirs = comm_call(f"swap_l{l}_{tags[0]}", swap_plan(_pick(mine, tags), l))
        for tag, o in zip(tags, theirs):
            g = mine[tag]
            parts[l][tag] = add_pair(f"l{l}_add_{tag}", g.reshape(-1, g.shape[-1]), o.reshape(-1, o.shape[-1])).reshape(g.shape)

    wts[0].update(zip(FIRST_GATHER, comm_call("gather_first", gather_plan(_pick(by_tag, FIRST_GATHER), 0))))
    h = x.reshape(t, d)
    for l in range(depth):
        sm, wl = sm_of(l), wts[l]
        h, s1, got = _ffn_fwd(f"l{l}_ffn1", h, sm["ffn1_norm"], wl["col1"], wl["down1"], gather_piece(l, "ffn1_up"))
        gathered(l, "ffn1_up", got)
        weights = _layer_weights(wl["win"], wl["uq"], wl["ukv"], wl["branch"])
        h, s2, got = _mixer_fwd(f"l{l}_mix", h, sm, weights, wl["wo"], aux, {k: gather_piece(l, k) for k in ("proj", "swa", "mla")})
        for kern, arrays in got.items():
            gathered(l, kern, arrays)
        h, s3, got = _ffn_fwd(f"l{l}_ffn2", h, sm["ffn2_norm"], wl["col2"], wl["down2"], gather_piece(l, "ffn2_up"))
        gathered(l, "ffn2_up", got)
        saved[l] = (s1, s2, s3, weights)
    loss, dh, dhb = loss_fwd_bwd("loss", h, target.reshape(t, d))

    for l in reversed(range(depth)):
        sm, wl = sm_of(l), wts[l]
        s1, s2, s3, weights = saved[l]
        staged = l == 0
        dh, dhb, dg_f2, gcol2, gdown2, got = _ffn_bwd(f"l{l}_ffn2", s3, sm["ffn2_norm"], wl["col2"], wl["down2"], dh, dhb,
                                                      {k: scatter_piece(l, "ffn2_" + k) for k in ("dh", "dwgu", "dn")})
        for kern, arrays in got.items():
            scattered(l, "ffn2_" + kern, arrays)
        mine = dict(col2=gcol2, down2=gdown2)
        if staged:
            chip_sums(l, ("col2", "down2"), mine)
        dh, dhb, wg, sg, got = _mixer_bwd(f"l{l}_mix", s2, sm, weights, wl["wo"], aux, dh, dhb,
                                          {k: scatter_piece(l, k) for k in ("dswa", "dmla")})
        for kern, arrays in got.items():
            scattered(l, kern, arrays)
        gwin, guq, gukv, gbranch = _mixer_grad_shards(wg)
        mine.update(wo=wg["wo"], win=gwin, uq=guq, ukv=gukv, branch=gbranch)
        if staged:
            chip_sums(l, ("wo", "win", "uq", "ukv", "branch"), mine)
        dh, dhb, dg_f1, gcol1, gdown1, got = _ffn_bwd(f"l{l}_ffn1", s1, sm["ffn1_norm"], wl["col1"], wl["down1"], dh, dhb,
                                                      {k: scatter_piece(l, "ffn1_" + k) for k in ("dh", "dwgu", "dn")})
        for kern, arrays in got.items():
            scattered(l, "ffn1_" + kern, arrays)
        mine.update(col1=gcol1, down1=gdown1)
        chip_sums(l, LAST_SCATTER if staged else BUFFER_TAGS, mine)
        sg.update(ffn1_norm=dg_f1, ffn2_norm=dg_f2)
        sgrads[l] = sg
    slots[0].update(zip(LAST_SCATTER, comm_call("scatter_last", scatter_plan(_pick(parts[0], LAST_SCATTER), 0))))
    finals = [[sum_slots(f"l{l}_sum_{tag}", slots[l][tag], parts[l][tag], shard) for tag in BUFFER_TAGS] for l in range(depth)]
    shared = comm_call("share", share_plan(finals))
    nk = len(BUFFER_TAGS)
    return loss, dh.reshape(b, s, d), [dict(zip(BUFFER_TAGS, shared[l * nk:(l + 1) * nk])) for l in range(depth)], sgrads


def kernel(x, positions, ffn1_norm, ffn1_w_gate, ffn1_w_up, ffn1_w_down, mix_norm, w_in, swa_q_norm, swa_k_norm, swa_sinks, mla_q_lora_norm, mla_w_uq, mla_kv_lora_norm, mla_w_ukv, mla_q_norm, mla_k_norm, w_branch_a, w_branch_b, w_out, ffn2_norm, ffn2_w_gate, ffn2_w_up, ffn2_w_down, loss_target, m_ffn1_norm, m_ffn1_w_gate, m_ffn1_w_up, m_ffn1_w_down, m_mix_norm, m_w_in, m_swa_q_norm, m_swa_k_norm, m_swa_sinks, m_mla_q_lora_norm, m_mla_w_uq, m_mla_kv_lora_norm, m_mla_w_ukv, m_mla_q_norm, m_mla_k_norm, m_w_branch_a, m_w_branch_b, m_w_out, m_ffn2_norm, m_ffn2_w_gate, m_ffn2_w_up, m_ffn2_w_down, v_ffn1_norm, v_ffn1_w_gate, v_ffn1_w_up, v_ffn1_w_down, v_mix_norm, v_w_in, v_swa_q_norm, v_swa_k_norm, v_swa_sinks, v_mla_q_lora_norm, v_mla_w_uq, v_mla_kv_lora_norm, v_mla_w_ukv, v_mla_q_norm, v_mla_k_norm, v_w_branch_a, v_w_branch_b, v_w_out, v_ffn2_norm, v_ffn2_w_gate, v_ffn2_w_up, v_ffn2_w_down):
    args = (x, positions, ffn1_norm, ffn1_w_gate, ffn1_w_up, ffn1_w_down, mix_norm, w_in, swa_q_norm, swa_k_norm, swa_sinks, mla_q_lora_norm, mla_w_uq, mla_kv_lora_norm, mla_w_ukv, mla_q_norm, mla_k_norm, w_branch_a, w_branch_b, w_out, ffn2_norm, ffn2_w_gate, ffn2_w_up, ffn2_w_down, loss_target, m_ffn1_norm, m_ffn1_w_gate, m_ffn1_w_up, m_ffn1_w_down, m_mix_norm, m_w_in, m_swa_q_norm, m_swa_k_norm, m_swa_sinks, m_mla_q_lora_norm, m_mla_w_uq, m_mla_kv_lora_norm, m_mla_w_ukv, m_mla_q_norm, m_mla_k_norm, m_w_branch_a, m_w_branch_b, m_w_out, m_ffn2_norm, m_ffn2_w_gate, m_ffn2_w_up, m_ffn2_w_down, v_ffn1_norm, v_ffn1_w_gate, v_ffn1_w_up, v_ffn1_w_down, v_mix_norm, v_w_in, v_swa_q_norm, v_swa_k_norm, v_swa_sinks, v_mla_q_lora_norm, v_mla_w_uq, v_mla_kv_lora_norm, v_mla_w_ukv, v_mla_q_norm, v_mla_k_norm, v_w_branch_a, v_w_branch_b, v_w_out, v_ffn2_norm, v_ffn2_w_gate, v_ffn2_w_up, v_ffn2_w_down)
    a = dict(zip(ARG_NAMES, args, strict=True))
    x = a["x"]
    depth = a["ffn1_norm"].shape[0]
    d = x.shape[-1]
    assert depth == 2, "one layer slab per core of a chip"

    def bf16_rows(*names):
        return jnp.concatenate([a[n] for n in names], axis=1).astype(BF16) if len(names) > 1 else a[names[0]].astype(BF16)

    packs = [bf16_rows("ffn1_w_gate", "ffn1_w_up"), bf16_rows("ffn2_w_gate", "ffn2_w_up"), bf16_rows("ffn1_w_down"),
             bf16_rows("ffn2_w_down"), bf16_rows("w_out"), bf16_rows("w_in"), bf16_rows("mla_w_uq"), bf16_rows("mla_w_ukv"),
             bf16_rows("w_branch_a", "w_branch_b")]
    shard_me = (2 * lax.axis_index("x") + lax.axis_index("y")).astype(jnp.int32).reshape(1)
    small = {n: a[n] for n in SMALL_NAMES}

    loss, grad_x, summed, sgrads = _train_step(x, a["positions"], a["loss_target"], packs, small, shard_me)

    full = {tag: jnp.stack([summed[l][tag] for l in range(depth)]) for tag in BUFFER_TAGS}
    r_branch = a["w_branch_a"].shape[1]
    grads = dict(ffn1_w_gate=full["col1"][:, :d], ffn1_w_up=full["col1"][:, d:], ffn2_w_gate=full["col2"][:, :d],
                 ffn2_w_up=full["col2"][:, d:], ffn1_w_down=full["down1"], ffn2_w_down=full["down2"], w_out=full["wo"],
                 w_in=full["win"], mla_w_uq=full["uq"], mla_w_ukv=full["ukv"], w_branch_a=full["branch"][:, :r_branch],
                 w_branch_b=full["branch"][:, r_branch:])

    flat = jnp.concatenate([jnp.concatenate([sgrads[l][n].reshape(-1) for l in range(depth)]) for n in SMALL_NAMES] + [loss.reshape(-1)])
    n_small = flat.shape[0]
    rows = -(-n_small // (8 * LANES)) * 8
    pad = rows * LANES - n_small

    def small_pack(v):
        return jnp.pad(v, (0, pad)).reshape(rows, LANES)

    total = small_allreduce(small_pack(flat))
    w_s, m_s, v_s = (small_pack(jnp.concatenate([a[p + n].reshape(-1) for n in SMALL_NAMES] + [jnp.zeros((1,), F32)]))
                     for p in ("", "m_", "v_"))
    d_s, nm_s, nv_s = adamw("adamw_small", w_s, total, m_s, v_s)

    def small_unpack(buf):
        out, off, flat_b = {}, 0, buf.reshape(-1)
        for n in SMALL_NAMES:
            size = a[n].shape[0] * a[n].shape[1]
            out[n] = flat_b[off:off + size].reshape(a[n].shape)
            off += size
        return out

    grads.update(small_unpack(total))
    delta, new_m, new_v = small_unpack(d_s), small_unpack(nm_s), small_unpack(nv_s)
    for n in PACK_NAMES:
        shp = a[n].shape
        two_d = (shp[0] * shp[1], shp[2])
        dn, mn, vn = adamw("adamw_" + n, a[n].reshape(two_d), grads[n].reshape(two_d), a["m_" + n].reshape(two_d), a["v_" + n].reshape(two_d))
        delta[n], new_m[n], new_v[n] = dn.reshape(shp), mn.reshape(shp), vn.reshape(shp)

    loss_out = total.reshape(-1)[n_small - 1]
    return (loss_out, grad_x, *[grads[n] for n in WEIGHT_NAMES], *[delta[n] for n in WEIGHT_NAMES],
            *[new_m[n] for n in WEIGHT_NAMES], *[new_v[n] for n in WEIGHT_NAMES])
```

```python
import functools

import numpy as np
import jax
import jax.numpy as jnp
from jax import lax
from jax.experimental import pallas as pl
from jax.experimental.pallas import tpu as pltpu

F32 = jnp.float32
BF16 = jnp.bfloat16
MESH = pl.DeviceIdType.MESH

HEAD_DIM_A = 64
N_HEADS_A = 8
N_KV_HEADS_A = 2
GROUP_A = N_HEADS_A // N_KV_HEADS_A
BLOCK = 128
N_HEADS_B = 8
Q_LORA_RANK = 256
KV_LORA_RANK = 128
QK_NOPE_DIM = 64
QK_ROPE_DIM = 32
QK_DIM_B = QK_NOPE_DIM + QK_ROPE_DIM
V_DIM_B = 64
ROPE_BASE = 10000.0
WIDTH_A = N_HEADS_A * HEAD_DIM_A
WIDTH_B = N_HEADS_B * V_DIM_B
KV_WIDTH_A = N_KV_HEADS_A * HEAD_DIM_A
EPS = 1e-6
NEG = -1e30
ADAM_LR = 0.001
ADAM_B1 = 0.9
ADAM_B2 = 0.999
ADAM_EPS = 1e-08
ADAM_WD = 0.01
ADAM_STEP = 10

N_SHARDS = 4
N_DEV = 8
LANES = 128
VMEM_LIMIT = 48 * 1024 * 1024

PACK_NAMES = ("ffn1_w_gate", "ffn1_w_up", "ffn1_w_down", "w_in", "mla_w_uq", "mla_w_ukv",
              "w_branch_a", "w_branch_b", "w_out", "ffn2_w_gate", "ffn2_w_up", "ffn2_w_down")
SMALL_NAMES = ("ffn1_norm", "mix_norm", "swa_q_norm", "swa_k_norm", "swa_sinks", "mla_q_lora_norm",
               "mla_kv_lora_norm", "mla_q_norm", "mla_k_norm", "ffn2_norm")
WEIGHT_NAMES = ("ffn1_norm", "ffn1_w_gate", "ffn1_w_up", "ffn1_w_down", "mix_norm", "w_in", "swa_q_norm",
                "swa_k_norm", "swa_sinks", "mla_q_lora_norm", "mla_w_uq", "mla_kv_lora_norm", "mla_w_ukv",
                "mla_q_norm", "mla_k_norm", "w_branch_a", "w_branch_b", "w_out", "ffn2_norm", "ffn2_w_gate",
                "ffn2_w_up", "ffn2_w_down")


def _params(sem):
    return pltpu.CompilerParams(dimension_semantics=sem, vmem_limit_bytes=VMEM_LIMIT)


def _tile(n, want, align):
    if n <= want:
        return n
    t = (want // align) * align
    while t > align and n % t:
        t -= align
    assert t >= align and n % t == 0, (n, want, align)
    return t


def _mm_call(name, a, b, out_struct, grid, a_spec, b_spec, o_spec, dims, n_red, acc_shape, alpha=1.0, res=None, into=None,
             inner=0, a_cols=0, plan=None):
    if n_red and all(g == 1 for g in grid[len(grid) - n_red:]):
        n_red = 0
    n_par = len(grid) - n_red
    n_in = 2 + int(res is not None) + int(into is not None)
    p_ins, p_in_specs, p_outs, p_out_specs, p_scratch, semantics, at_start, at_end = _hosted(
        plan, n_in, 1, grid, ("parallel",) * n_par + ("arbitrary",) * n_red)

    def body(*refs):
        at_start(refs)
        compute(refs)
        at_end(refs)

    def compute(refs):
        a_ref, b_ref = refs[:2]
        r_ref = refs[2] if res is not None else None
        o_ref, acc_ref = refs[n_in + len(p_ins)], refs[n_in + len(p_ins) + 1 + len(p_outs)]
        if inner:
            def a_of(s):
                return a_ref[:, s * a_cols:(s + 1) * a_cols] if a_cols else a_ref[s]

            part = lax.dot_general(a_of(0), b_ref[0], dims, preferred_element_type=F32)
            for s in range(1, inner):
                part = part + lax.dot_general(a_of(s), b_ref[s], dims, preferred_element_type=F32)
        else:
            part = lax.dot_general(a_ref[...], b_ref[...], dims, preferred_element_type=F32)

        def finish(total):
            if alpha != 1.0:
                total = total * alpha
            if r_ref is not None:
                total = r_ref[...] + total
            o_ref[...] = total.astype(o_ref.dtype)

        if n_red == 0:
            finish(part)
            return
        ids = [pl.program_id(n_par + i) for i in range(n_red)]
        first = functools.reduce(jnp.logical_and, [i == 0 for i in ids])
        last = functools.reduce(jnp.logical_and, [i == grid[n_par + k] - 1 for k, i in enumerate(ids)])

        @pl.when(first)
        def _():
            acc_ref[...] = part

        @pl.when(jnp.logical_not(first))
        def _():
            acc_ref[...] += part

        @pl.when(last)
        def _():
            finish(acc_ref[...])

    in_specs = [a_spec, b_spec] + ([o_spec] if res is not None else [])
    args = (a, b) + ((res,) if res is not None else ())
    aliases = {}
    if into is not None:
        aliases = {len(args): 0}
        in_specs.append(pl.BlockSpec(memory_space=pl.ANY))
        args = args + (into,)
    if plan is None:
        return pl.pallas_call(
            body, name=name, out_shape=out_struct, grid=grid, in_specs=in_specs, out_specs=o_spec,
            scratch_shapes=[pltpu.VMEM(acc_shape, F32)], input_output_aliases=aliases, compiler_params=_params(semantics),
        )(*args)
    res_all = pl.pallas_call(
        body, name=name, out_shape=(out_struct,) + tuple(p_outs), grid=grid, in_specs=in_specs + p_in_specs,
        out_specs=(o_spec,) + tuple(p_out_specs), scratch_shapes=[pltpu.VMEM(acc_shape, F32)] + p_scratch,
        input_output_aliases=aliases, compiler_params=_params(semantics),
    )(*args, *p_ins)
    return res_all[0], list(res_all[1:])


_NN = (((1,), (0,)), ((), ()))
_NT = (((1,), (1,)), ((), ()))
_TN = (((0,), (0,)), ((), ()))


def mm_nn(name, a, b, tm=512, tn=1024, tk=1024, out_dtype=F32, alpha=1.0, res=None, plan=None):
    (m, k), (_, n) = a.shape, b.shape
    tm, tn, tk = _tile(m, tm, 16), _tile(n, tn, LANES), _tile(k, tk, LANES)
    return _mm_call(name, a, b, jax.ShapeDtypeStruct((m, n), out_dtype), (m // tm, n // tn, k // tk),
                    pl.BlockSpec((tm, tk), lambda i, j, kk: (i, kk)), pl.BlockSpec((tk, tn), lambda i, j, kk: (kk, j)),
                    pl.BlockSpec((tm, tn), lambda i, j, kk: (i, j)), _NN, 1, (tm, tn), alpha, res, plan=plan)


def mm_nt(name, a, b, tm=512, tn=1024, tk=1024, out_dtype=F32, alpha=1.0, res=None):
    (m, n), (k, _) = a.shape, b.shape
    tm, tn, tk = _tile(m, tm, 16), _tile(k, tn, LANES), _tile(n, tk, LANES)
    return _mm_call(name, a, b, jax.ShapeDtypeStruct((m, k), out_dtype), (m // tm, k // tn, n // tk),
                    pl.BlockSpec((tm, tk), lambda i, j, kk: (i, kk)), pl.BlockSpec((tn, tk), lambda i, j, kk: (j, kk)),
                    pl.BlockSpec((tm, tn), lambda i, j, kk: (i, j)), _NT, 1, (tm, tn), alpha, res)


def mm_tn(name, a, b, tm=1024, tn=1024, tk=1024, out_dtype=F32, alpha=1.0):
    (m, k), (_, n) = a.shape, b.shape
    tm, tn, tk = _tile(k, tm, LANES), _tile(n, tn, LANES), _tile(m, tk, 16)
    return _mm_call(name, a, b, jax.ShapeDtypeStruct((k, n), out_dtype), (k // tm, n // tn, m // tk),
                    pl.BlockSpec((tk, tm), lambda i, j, kk: (kk, i)), pl.BlockSpec((tk, tn), lambda i, j, kk: (kk, j)),
                    pl.BlockSpec((tm, tn), lambda i, j, kk: (i, j)), _TN, 1, (tm, tn), alpha)


def ffn_up_act(name, a, w, blk, tm=1024, plan=None):
    (m, k), (ns, _, n) = a.shape, w.shape
    tm = _tile(m, tm, 16)
    grid = (ns, m // tm)
    p_ins, p_in_specs, p_outs, p_out_specs, p_scratch, semantics, at_start, at_end = _hosted(plan, 3, 2, grid, ("parallel", "parallel"))

    def body(*refs):
        a_ref, wg_ref, wu_ref = refs[:3]
        gu_ref, h_ref = refs[3 + len(p_ins):5 + len(p_ins)]
        at_start(refs)
        av = a_ref[...]
        gate = jnp.dot(av, wg_ref[...], preferred_element_type=F32)
        up = jnp.dot(av, wu_ref[...], preferred_element_type=F32)
        gu_ref[0] = gate.astype(BF16)
        gu_ref[1] = up.astype(BF16)
        h_ref[...] = (gate * jax.nn.sigmoid(gate) * up).astype(BF16)
        at_end(refs)

    res = pl.pallas_call(
        body, name=name, grid=grid,
        out_shape=(jax.ShapeDtypeStruct((2, ns, m, n), BF16), jax.ShapeDtypeStruct((ns, m, n), BF16)) + tuple(p_outs),
        in_specs=[pl.BlockSpec((tm, k), lambda s, i: (i, 0)), pl.BlockSpec((None, k, n), lambda s, i: (s, blk, 0)),
                  pl.BlockSpec((None, k, n), lambda s, i: (s, blk + 1, 0))] + p_in_specs,
        out_specs=(pl.BlockSpec((2, None, tm, n), lambda s, i: (0, s, i, 0)), pl.BlockSpec((None, tm, n), lambda s, i: (s, i, 0)))
        + tuple(p_out_specs),
        scratch_shapes=p_scratch, compiler_params=_params(semantics),
    )(a, w, w, *p_ins)
    return res[0], res[1], list(res[2:])


def ffn_down_dact(name, a, w, gu, n, blk, alpha, tm=1024, plan=None):
    (m, d), ns = a.shape, w.shape[0]
    tm = _tile(m, tm, 16)
    grid = (ns, m // tm)
    p_ins, p_in_specs, p_outs, p_out_specs, p_scratch, semantics, at_start, at_end = _hosted(plan, 3, 1, grid, ("parallel", "parallel"))

    def body(*refs):
        a_ref, w_ref, gu_ref = refs[:3]
        o_ref = refs[3 + len(p_ins)]
        at_start(refs)
        dh = lax.dot_general(a_ref[...], w_ref[...], _NT, preferred_element_type=F32) * alpha
        gate, up = gu_ref[0].astype(F32), gu_ref[1].astype(F32)
        s = jax.nn.sigmoid(gate)
        o_ref[0] = (dh * up * (s * (1.0 + gate * (1.0 - s)))).astype(BF16)
        o_ref[1] = (dh * (gate * s)).astype(BF16)
        at_end(refs)

    gu_spec = pl.BlockSpec((2, None, tm, n), lambda s, i: (0, s, i, 0))
    res = pl.pallas_call(
        body, name=name, grid=grid, out_shape=(jax.ShapeDtypeStruct((2, ns, m, n), BF16),) + tuple(p_outs),
        in_specs=[pl.BlockSpec((tm, d), lambda s, i: (i, 0)), pl.BlockSpec((None, n, d), lambda s, i: (s, blk, 0)), gu_spec] + p_in_specs,
        out_specs=(gu_spec,) + tuple(p_out_specs), scratch_shapes=p_scratch, compiler_params=_params(semantics),
    )(a, w, gu, *p_ins)
    return res[0], list(res[1:])


def gmm_up_dw(name, a, dgu, tk=2048, plan=None):
    (m, k), (_, ns, _, n) = a.shape, dgu.shape
    tk = _tile(m, tk, 16)
    return _mm_call(name, a, dgu, jax.ShapeDtypeStruct((ns, 2 * k, n), BF16), (2, ns, m // tk),
                    pl.BlockSpec((tk, k), lambda j, s, kk: (kk, 0)), pl.BlockSpec((None, None, tk, n), lambda j, s, kk: (j, s, kk, 0)),
                    pl.BlockSpec((None, k, n), lambda j, s, kk: (s, j, 0)), _TN, 1, (k, n), plan=plan)


def gmm_up_dx(name, dgu, w, k, blk, tm=512, plan=None):
    _, ns, m, n = dgu.shape
    tm = _tile(m, tm, 16)
    return _mm_call(name, dgu, w, jax.ShapeDtypeStruct((m, k), F32), (m // tm, 2),
                    pl.BlockSpec((None, ns, tm, n), lambda i, j: (j, 0, i, 0)),
                    pl.BlockSpec((ns, k, n), lambda i, j: (0, blk + j, 0)),
                    pl.BlockSpec((tm, k), lambda i, j: (i, 0)), _NT, 1, (tm, k), inner=ns, plan=plan)


def gmm_down(name, h, w, blk, res, alpha, tm=512):
    (ns, m, n), d = h.shape, w.shape[2]
    tm = _tile(m, tm, 16)
    return _mm_call(name, h, w, jax.ShapeDtypeStruct((m, d), F32), (m // tm,),
                    pl.BlockSpec((ns, tm, n), lambda i: (0, i, 0)), pl.BlockSpec((ns, n, d), lambda i: (0, blk, 0)),
                    pl.BlockSpec((tm, d), lambda i: (i, 0)), _NN, 0, (8, LANES), alpha, res, inner=ns)


def gmm_down_dw(name, h, b, alpha, tk=2048):
    (ns, m, n), d = h.shape, b.shape[1]
    tk = _tile(m, tk, 16)
    return _mm_call(name, h, b, jax.ShapeDtypeStruct((ns, n, d), BF16), (ns, m // tk),
                    pl.BlockSpec((None, tk, n), lambda s, kk: (s, kk, 0)), pl.BlockSpec((tk, d), lambda s, kk: (kk, 0)),
                    pl.BlockSpec((None, n, d), lambda s, kk: (s, 0, 0)), _TN, 1, (n, d), alpha)


def gmm_rows(name, a, w, res, tm=512):
    (m, _), (ns, r, d) = a.shape, w.shape
    tm = _tile(m, tm, 16)
    return _mm_call(name, a, w, jax.ShapeDtypeStruct((m, d), F32), (m // tm,),
                    pl.BlockSpec((tm, ns * r), lambda i: (i, 0)), pl.BlockSpec((ns, r, d), lambda i: (0, 0, 0)),
                    pl.BlockSpec((tm, d), lambda i: (i, 0)), _NN, 0, (8, LANES), 1.0, res, inner=ns, a_cols=r)


def gmm_rows_dx(name, a, w, tm=512):
    (m, d), (ns, r, _) = a.shape, w.shape
    tm = _tile(m, tm, 16)
    return _mm_call(name, a, w, jax.ShapeDtypeStruct((m, ns * r), F32), (ns, m // tm),
                    pl.BlockSpec((tm, d), lambda s, i: (i, 0)), pl.BlockSpec((None, r, d), lambda s, i: (s, 0, 0)),
                    pl.BlockSpec((tm, r), lambda s, i: (i, s)), _NT, 0, (8, LANES))


def gmm_rows_dw(name, a, b, tk=2048):
    (m, da), d = a.shape, b.shape[1]
    ns, r = N_SHARDS, da // N_SHARDS
    tk = _tile(m, tk, 16)
    return _mm_call(name, a, b, jax.ShapeDtypeStruct((ns, r, d), BF16), (ns, m // tk),
                    pl.BlockSpec((tk, r), lambda s, kk: (kk, s)), pl.BlockSpec((tk, d), lambda s, kk: (kk, 0)),
                    pl.BlockSpec((None, r, d), lambda s, kk: (s, 0, 0)), _TN, 1, (r, d))


def rms_fwd(name, x, gain, tm=512, col_blk=0):
    m, d = x.shape[0], gain.shape[1]
    tm = _tile(m, tm, 16)

    def body(x_ref, g_ref, o_ref):
        xv = x_ref[...]
        r = lax.rsqrt(jnp.mean(xv * xv, axis=-1, keepdims=True) + EPS)
        o_ref[...] = (xv * r * g_ref[...]).astype(o_ref.dtype)

    return pl.pallas_call(
        body, name=name, out_shape=jax.ShapeDtypeStruct((m, d), BF16), grid=(m // tm,),
        in_specs=[pl.BlockSpec((tm, d), lambda i: (i, col_blk)), pl.BlockSpec((1, d), lambda i: (0, 0))],
        out_specs=pl.BlockSpec((tm, d), lambda i: (i, 0)), compiler_params=_params(("parallel",)),
    )(x, gain)


def rms_bwd(name, x, gain, dn, dres=None, want_f32=True, want_bf16=True, tm=512, col_blk=0):
    m, d = x.shape[0], gain.shape[1]
    tm = _tile(m, tm, 16)
    n_out = int(want_f32) + int(want_bf16)

    def body(*refs):
        x_ref, g_ref, dn_ref = refs[:3]
        pos = 3
        r_ref = None
        if dres is not None:
            r_ref = refs[pos]
            pos += 1
        outs = refs[pos:pos + n_out]
        dg_ref = refs[pos + n_out]
        xv = x_ref[...]
        r = lax.rsqrt(jnp.mean(xv * xv, axis=-1, keepdims=True) + EPS)
        xhat = xv * r
        dnv = dn_ref[...]
        dxhat = dnv * g_ref[...]
        dx = r * (dxhat - xhat * jnp.mean(dxhat * xhat, axis=-1, keepdims=True))
        if r_ref is not None:
            dx = r_ref[...] + dx
        for o in outs:
            o[...] = dx.astype(o.dtype)
        part = jnp.sum(dnv * xhat, axis=0, keepdims=True)

        @pl.when(pl.program_id(0) == 0)
        def _():
            dg_ref[...] = part

        @pl.when(pl.program_id(0) > 0)
        def _():
            dg_ref[...] += part

    row = pl.BlockSpec((tm, d), lambda i: (i, 0))
    vec = pl.BlockSpec((1, d), lambda i: (0, 0))
    out_shape = ([jax.ShapeDtypeStruct((m, d), F32)] if want_f32 else []) + ([jax.ShapeDtypeStruct((m, d), BF16)] if want_bf16 else [])
    res = pl.pallas_call(
        body, name=name, out_shape=tuple(out_shape) + (jax.ShapeDtypeStruct((1, d), F32),), grid=(m // tm,),
        in_specs=[pl.BlockSpec((tm, d), lambda i: (i, col_blk)), vec, row] + ([row] if dres is not None else []),
        out_specs=tuple([row] * n_out) + (vec,), compiler_params=_params(("arbitrary",)),
    )(*((x, gain, dn) + ((dres,) if dres is not None else ())))
    return res


def gate_fwd(name, proj, ya, yb, off_a, off_b, tm=256):
    m, d = ya.shape
    w = proj.shape[1]
    tm = _tile(m, tm, 16)

    def body(p_ref, ya_ref, yb_ref, o_ref):
        ga, gb = p_ref[:, off_a:off_a + d], p_ref[:, off_b:off_b + d]
        o_ref[...] = (jax.nn.sigmoid(ga) * ya_ref[...] + jax.nn.sigmoid(gb) * yb_ref[...]).astype(o_ref.dtype)

    row = pl.BlockSpec((tm, d), lambda i: (i, 0))
    return pl.pallas_call(
        body, name=name, out_shape=jax.ShapeDtypeStruct((m, d), BF16), grid=(m // tm,),
        in_specs=[pl.BlockSpec((tm, w), lambda i: (i, 0)), row, row], out_specs=row, compiler_params=_params(("parallel",)),
    )(proj, ya, yb)


def gate_bwd(name, proj, ya, yb, dmerged, off_a, off_b, tm=256):
    m, d = ya.shape
    w = proj.shape[1]
    tm = _tile(m, tm, 16)

    def body(p_ref, ya_ref, yb_ref, dm_ref, dya_ref, dyb_ref, dga_ref, dgb_ref):
        sa, sb = jax.nn.sigmoid(p_ref[:, off_a:off_a + d]), jax.nn.sigmoid(p_ref[:, off_b:off_b + d])
        dm = dm_ref[...]
        dya_ref[...] = (dm * sa).astype(BF16)
        dyb_ref[...] = (dm * sb).astype(BF16)
        dga_ref[...] = (dm * ya_ref[...] * (sa * (1.0 - sa))).astype(BF16)
        dgb_ref[...] = (dm * yb_ref[...] * (sb * (1.0 - sb))).astype(BF16)

    row = pl.BlockSpec((tm, d), lambda i: (i, 0))
    o = jax.ShapeDtypeStruct((m, d), BF16)
    return pl.pallas_call(
        body, name=name, out_shape=(o, o, o, o), grid=(m // tm,),
        in_specs=[pl.BlockSpec((tm, w), lambda i: (i, 0)), row, row, row], out_specs=(row, row, row, row),
        compiler_params=_params(("parallel",)),
    )(proj, ya, yb, dmerged)


def loss_fwd_bwd(name, y, target, tm=512):
    m, d = y.shape
    tm = _tile(m, tm, 16)

    def body(y_ref, t_ref, l_ref, dy_ref, dyb_ref):
        err = y_ref[...] - t_ref[...]
        dy = err * (1.0 / d)
        dy_ref[...] = dy
        dyb_ref[...] = dy.astype(BF16)
        part = 0.5 * jnp.sum(jnp.mean(err * err, axis=-1, keepdims=True), axis=0, keepdims=True)

        @pl.when(pl.program_id(0) == 0)
        def _():
            l_ref[...] = part

        @pl.when(pl.program_id(0) > 0)
        def _():
            l_ref[...] += part

    row = pl.BlockSpec((tm, d), lambda i: (i, 0))
    return pl.pallas_call(
        body, name=name, grid=(m // tm,),
        out_shape=(jax.ShapeDtypeStruct((1, 1), F32), jax.ShapeDtypeStruct((m, d), F32), jax.ShapeDtypeStruct((m, d), BF16)),
        in_specs=[row, row], out_specs=(pl.BlockSpec((1, 1), lambda i: (0, 0)), row, row),
        compiler_params=_params(("arbitrary",)),
    )(y, target)


def adamw(name, w, g, m, v):
    r, c = w.shape
    tr = _tile(r, max(8, (2 * 1024 * 1024) // (4 * c) // 8 * 8), 8)
    c1 = np.float32(1.0 - ADAM_B1 ** ADAM_STEP)
    c2 = np.float32(1.0 - ADAM_B2 ** ADAM_STEP)

    def body(w_ref, g_ref, m_ref, v_ref, d_ref, nm_ref, nv_ref):
        gv = g_ref[...]
        nm = ADAM_B1 * m_ref[...] + (1.0 - ADAM_B1) * gv
        nv = ADAM_B2 * v_ref[...] + (1.0 - ADAM_B2) * (gv * gv)
        d_ref[...] = -ADAM_LR * ((nm / c1) / (jnp.sqrt(nv / c2) + ADAM_EPS) + ADAM_WD * w_ref[...])
        nm_ref[...] = nm
        nv_ref[...] = nv

    row = pl.BlockSpec((tr, c), lambda i: (i, 0))
    o = jax.ShapeDtypeStruct((r, c), F32)
    return pl.pallas_call(
        body, name=name, out_shape=(o, o, o), grid=(r // tr,), in_specs=[row] * 4, out_specs=(row, row, row),
        compiler_params=_params(("parallel",)),
    )(w, g, m, v)


HEAD_PAD = LANES


def _rope_rot():
    r = np.zeros((HEAD_PAD, HEAD_PAD), np.float32)
    half = QK_ROPE_DIM // 2
    for j in range(half):
        r[QK_NOPE_DIM + half + j, QK_NOPE_DIM + j] = -1.0
        r[QK_NOPE_DIM + j, QK_NOPE_DIM + half + j] = 1.0
    return r


def _head_spec(s, blk0):
    return pl.BlockSpec((None, s, HEAD_PAD), lambda bi, hi: (bi, 0, blk0 + hi))


def qk_prep_fwd(name, x, blk0, n_heads, d_real, gain, rope=None, extra=None):
    b, s, _ = x.shape

    def body(*refs):
        x_ref, g_ref = refs[0], refs[1]
        pos = 2
        xv = x_ref[...]
        if extra is not None:
            xv = xv + refs[pos][...]
            pos += 1
        y = xv * lax.rsqrt(jnp.sum(xv * xv, axis=-1, keepdims=True) * (1.0 / d_real) + EPS) * g_ref[...]
        if rope is not None:
            c_ref, s_ref, r_ref = refs[pos:pos + 3]
            rot = jnp.dot(y, r_ref[...], precision=lax.Precision.HIGHEST, preferred_element_type=F32)
            y = y * c_ref[...] + rot * s_ref[...]
        refs[-1][...] = y.astype(BF16)

    vec = pl.BlockSpec((1, HEAD_PAD), lambda bi, hi: (0, 0))
    tab = pl.BlockSpec((None, s, HEAD_PAD), lambda bi, hi: (bi, 0, 0))
    in_specs, args = [_head_spec(s, blk0), vec], [x, gain]
    if extra is not None:
        e_blk = extra[1]
        in_specs.append(pl.BlockSpec((None, s, HEAD_PAD), lambda bi, hi: (bi, 0, e_blk)))
        args.append(extra[0])
    if rope is not None:
        in_specs += [tab, tab, pl.BlockSpec((HEAD_PAD, HEAD_PAD), lambda bi, hi: (0, 0))]
        args += [rope[0], rope[1], jnp.asarray(_rope_rot())]
    return pl.pallas_call(
        body, name=name, out_shape=jax.ShapeDtypeStruct((b, s, n_heads * HEAD_PAD), BF16), grid=(b, n_heads),
        in_specs=in_specs, out_specs=_head_spec(s, 0), compiler_params=_params(("parallel", "parallel")),
    )(*args)


def qk_prep_bwd(name, x, blk0, n_heads, d_real, gain, dxh, rope=None, extra=None, head_sum=False):
    b, s, _ = x.shape

    def body(*refs):
        x_ref, g_ref, dy_ref = refs[:3]
        pos = 3
        xv = x_ref[...]
        if extra is not None:
            xv = xv + refs[pos][...]
            pos += 1
        dy = dy_ref[...]
        if rope is not None:
            c_ref, s_ref, rt_ref = refs[pos:pos + 3]
            pos += 3
            dy = dy * c_ref[...] + jnp.dot(dy * s_ref[...], rt_ref[...], precision=lax.Precision.HIGHEST, preferred_element_type=F32)
        outs = refs[pos:]
        dx_ref, dg_ref = outs[0], outs[-1]
        r = lax.rsqrt(jnp.sum(xv * xv, axis=-1, keepdims=True) * (1.0 / d_real) + EPS)
        xhat = xv * r
        dxhat = dy * g_ref[...]
        dx = r * (dxhat - xhat * (jnp.sum(dxhat * xhat, axis=-1, keepdims=True) * (1.0 / d_real)))
        dx_ref[...] = dx.astype(dx_ref.dtype)
        part = jnp.sum(dy * xhat, axis=0, keepdims=True)
        first = jnp.logical_and(pl.program_id(0) == 0, pl.program_id(1) == 0)

        @pl.when(first)
        def _():
            dg_ref[...] = part

        @pl.when(jnp.logical_not(first))
        def _():
            dg_ref[...] += part

        if head_sum:
            hs_ref = outs[1]

            @pl.when(pl.program_id(1) == 0)
            def _():
                hs_ref[...] = dx

            @pl.when(pl.program_id(1) > 0)
            def _():
                hs_ref[...] += dx

    vec = pl.BlockSpec((1, HEAD_PAD), lambda bi, hi: (0, 0))
    tab = pl.BlockSpec((None, s, HEAD_PAD), lambda bi, hi: (bi, 0, 0))
    in_specs, args = [_head_spec(s, blk0), vec, _head_spec(s, 0)], [x, gain, dxh]
    if extra is not None:
        e_blk = extra[1]
        in_specs.append(pl.BlockSpec((None, s, HEAD_PAD), lambda bi, hi: (bi, 0, e_blk)))
        args.append(extra[0])
    if rope is not None:
        in_specs += [tab, tab, pl.BlockSpec((HEAD_PAD, HEAD_PAD), lambda bi, hi: (0, 0))]
        args += [rope[0], rope[1], jnp.asarray(_rope_rot().T.copy())]
    out_shape, out_specs = [jax.ShapeDtypeStruct((b, s, n_heads * HEAD_PAD), BF16)], [_head_spec(s, 0)]
    if head_sum:
        out_shape.append(jax.ShapeDtypeStruct((b, s, HEAD_PAD), F32))
        out_specs.append(tab)
    out_shape.append(jax.ShapeDtypeStruct((1, HEAD_PAD), F32))
    out_specs.append(vec)
    return pl.pallas_call(
        body, name=name, out_shape=tuple(out_shape), grid=(b, n_heads), in_specs=in_specs, out_specs=tuple(out_specs),
        compiler_params=_params(("arbitrary", "arbitrary")),
    )(*args)


def _swa_specs(v_blk0):
    q = pl.BlockSpec((None, BLOCK, GROUP_A * HEAD_PAD), lambda b, kv, n: (b, n, kv))
    kprev = pl.BlockSpec((None, BLOCK, HEAD_PAD), lambda b, kv, n: (b, jnp.maximum(n - 1, 0), kv))
    kcur = pl.BlockSpec((None, BLOCK, HEAD_PAD), lambda b, kv, n: (b, n, kv))
    vprev = pl.BlockSpec((None, BLOCK, HEAD_PAD), lambda b, kv, n: (b, jnp.maximum(n - 1, 0), v_blk0 + kv))
    vcur = pl.BlockSpec((None, BLOCK, HEAD_PAD), lambda b, kv, n: (b, n, v_blk0 + kv))
    pcol = pl.BlockSpec((None, BLOCK, 1), lambda b, kv, n: (b, n, 0))
    prow_prev = pl.BlockSpec((None, 1, BLOCK), lambda b, kv, n: (b, 0, jnp.maximum(n - 1, 0)))
    prow_cur = pl.BlockSpec((None, 1, BLOCK), lambda b, kv, n: (b, 0, n))
    smem = pl.BlockSpec(memory_space=pltpu.SMEM)
    return q, kprev, kcur, vprev, vcur, pcol, prow_prev, prow_cur, smem


def _swa_probs(q, kk, dist, valid, slope, sink):
    sc = lax.dot_general(q, kk, _NT, preferred_element_type=F32) * (HEAD_DIM_A ** -0.5)
    sc = sc - slope * dist
    sc = jnp.where(valid, sc, NEG)
    m = jnp.maximum(jnp.max(sc, axis=-1, keepdims=True), sink)
    e = jnp.exp(sc - m)
    es = jnp.exp(sink - m)
    inv = 1.0 / (jnp.sum(e, axis=-1, keepdims=True) + es)
    return e * inv, es * inv


def _swa_window(n, kp_ref, kc_ref, vp_ref, vc_ref, pc_ref, prp_ref, prc_ref):
    kk = jnp.concatenate([kp_ref[...], kc_ref[...]], axis=0)
    vv = jnp.concatenate([vp_ref[...], vc_ref[...]], axis=0).astype(BF16)
    dist = pc_ref[...] - jnp.concatenate([prp_ref[...], prc_ref[...]], axis=1)
    qi = lax.broadcasted_iota(jnp.int32, (BLOCK, 2 * BLOCK), 0) + BLOCK
    ki = lax.broadcasted_iota(jnp.int32, (BLOCK, 2 * BLOCK), 1)
    diff = qi - ki
    valid = (diff >= 0) & (diff < BLOCK) & ((n > 0) | (ki >= BLOCK))
    return kk, vv, dist, valid


def swa_fwd(name, q, k, vsrc, v_blk0, pos_col, pos_row, slopes, sinks, plan=None):
    b, s, _ = q.shape
    qs, kprev, kcur, vprev, vcur, pcol, prp, prc, smem = _swa_specs(v_blk0)
    grid = (b, N_KV_HEADS_A, s // BLOCK)
    p_ins, p_in_specs, p_outs, p_out_specs, p_scratch, semantics, at_start, at_end = _hosted(plan, 10, 1, grid, ("parallel",) * 3)

    def body(*refs):
        q_ref, kp_ref, kc_ref, vp_ref, vc_ref, pc_ref, prp_ref, prc_ref, sl_ref, sk_ref = refs[:10]
        o_ref = refs[10 + len(p_ins)]
        at_start(refs)
        kv, n = pl.program_id(1), pl.program_id(2)
        kk, vv, dist, valid = _swa_window(n, kp_ref, kc_ref, vp_ref, vc_ref, pc_ref, prp_ref, prc_ref)
        for g in range(GROUP_A):
            hd = kv * GROUP_A + g
            lanes = slice(g * HEAD_PAD, (g + 1) * HEAD_PAD)
            p, _ = _swa_probs(q_ref[:, lanes], kk, dist, valid, sl_ref[hd], sk_ref[hd])
            o_ref[:, lanes] = jnp.dot(p.astype(BF16), vv, preferred_element_type=F32).astype(BF16)
        at_end(refs)

    res = pl.pallas_call(
        body, name=name, out_shape=(jax.ShapeDtypeStruct(q.shape, BF16),) + tuple(p_outs), grid=grid,
        in_specs=[qs, kprev, kcur, vprev, vcur, pcol, prp, prc, smem, smem] + p_in_specs, out_specs=(qs,) + tuple(p_out_specs),
        scratch_shapes=p_scratch, compiler_params=_params(semantics),
    )(q, k, k, vsrc, vsrc, pos_col, pos_row, pos_row, slopes, sinks, *p_ins)
    return res[0], list(res[1:])


def swa_bwd(name, q, k, vsrc, v_blk0, pos_col, pos_row, slopes, sinks, do, plan=None):
    b, s, _ = q.shape
    qs, kprev, kcur, vprev, vcur, pcol, prp, prc, smem = _swa_specs(v_blk0)
    grid = (b, N_KV_HEADS_A, s // BLOCK)
    p_ins, p_in_specs, p_outs, p_out_specs, p_scratch, semantics, at_start, at_end = _hosted(plan, 11, 4, grid, ("arbitrary",) * 3)

    def body(*refs):
        at_start(refs)
        compute(*refs[:11], *refs[11 + len(p_ins):15 + len(p_ins)])
        at_end(refs)

    def compute(q_ref, kp_ref, kc_ref, vp_ref, vc_ref, pc_ref, prp_ref, prc_ref, sl_ref, sk_ref, do_ref, dq_ref, dk_ref, dv_ref, ds_ref):
        bi, kv, n = pl.program_id(0), pl.program_id(1), pl.program_id(2)
        kk, vv, dist, valid = _swa_window(n, kp_ref, kc_ref, vp_ref, vc_ref, pc_ref, prp_ref, prc_ref)

        @pl.when((bi == 0) & (kv == 0) & (n == 0))
        def _():
            ds_ref[...] = jnp.zeros_like(ds_ref)

        @pl.when(n == 0)
        def _():
            dk_ref[...] = jnp.zeros_like(dk_ref)
            dv_ref[...] = jnp.zeros_like(dv_ref)

        dkk = jnp.zeros((2 * BLOCK, HEAD_PAD), F32)
        dvv = jnp.zeros((2 * BLOCK, HEAD_PAD), F32)
        head_row = lax.broadcasted_iota(jnp.int32, (N_HEADS_A, LANES), 0)
        dsink = jnp.zeros((N_HEADS_A, LANES), F32)
        for g in range(GROUP_A):
            hd = kv * GROUP_A + g
            lanes = slice(g * HEAD_PAD, (g + 1) * HEAD_PAD)
            qg = q_ref[:, lanes]
            p, ps = _swa_probs(qg, kk, dist, valid, sl_ref[hd], sk_ref[hd])
            dob = do_ref[:, lanes].astype(BF16)
            dvv = dvv + lax.dot_general(p.astype(BF16), dob, _TN, preferred_element_type=F32)
            dp = lax.dot_general(dob, vv, _NT, preferred_element_type=F32)
            rs = jnp.sum(p * dp, axis=-1, keepdims=True)
            dsb = (p * (dp - rs) * (HEAD_DIM_A ** -0.5)).astype(BF16)
            dq_ref[:, lanes] = jnp.dot(dsb, kk, preferred_element_type=F32)
            dkk = dkk + lax.dot_general(dsb, qg, _TN, preferred_element_type=F32)
            dsink = dsink + jnp.where(head_row == hd, -jnp.sum(ps * rs), 0.0)
        ds_ref[...] += dsink

        @pl.when(n > 0)
        def _():
            start = pl.multiple_of((n - 1) * BLOCK, BLOCK)
            dk_ref[pl.ds(start, 2 * BLOCK), :] += dkk
            dv_ref[pl.ds(start, 2 * BLOCK), :] += dvv

        @pl.when(n == 0)
        def _():
            dk_ref[0:BLOCK, :] += dkk[BLOCK:, :]
            dv_ref[0:BLOCK, :] += dvv[BLOCK:, :]

    kv_full = pl.BlockSpec((None, s, HEAD_PAD), lambda bi, kv, n: (bi, 0, kv))
    kv_shape = jax.ShapeDtypeStruct(k.shape, F32)
    res = pl.pallas_call(
        body, name=name, grid=grid,
        out_shape=(jax.ShapeDtypeStruct(q.shape, F32), kv_shape, kv_shape, jax.ShapeDtypeStruct((N_HEADS_A, LANES), F32)) + tuple(p_outs),
        in_specs=[qs, kprev, kcur, vprev, vcur, pcol, prp, prc, smem, smem, qs] + p_in_specs,
        out_specs=(qs, kv_full, kv_full, pl.BlockSpec((N_HEADS_A, LANES), lambda bi, kv, n: (0, 0))) + tuple(p_out_specs),
        scratch_shapes=p_scratch, compiler_params=_params(semantics),
    )(q, k, k, vsrc, vsrc, pos_col, pos_row, pos_row, slopes, sinks, do, *p_ins)
    return res[0], res[1], res[2], res[3], list(res[4:])


MLA_T = 512


def _hosted(plan, n_in, n_out, grid, semantics):
    if plan is None:
        return [], [], [], [], [], semantics, (lambda refs: None), (lambda refs: None)
    ni, no = len(plan.ins), len(plan.out_shapes)

    def split(refs):
        return refs[n_in:n_in + ni], refs[n_in + ni + n_out:n_in + ni + n_out + no], refs[-2], refs[-1]

    def at_start(refs):
        @pl.when(functools.reduce(jnp.logical_and, [pl.program_id(ax) == 0 for ax in range(len(grid))]))
        def _():
            plan.start(*split(refs))

    def at_end(refs):
        @pl.when(functools.reduce(jnp.logical_and, [pl.program_id(ax) == g - 1 for ax, g in enumerate(grid)]))
        def _():
            plan.finish(*split(refs))

    return plan.ins, [_HBM] * ni, plan.out_shapes, [_HBM] * no, plan.scratch(), ("arbitrary",) * len(grid), at_start, at_end


def mla_fwd(name, q, k, v, plan=None):
    b, s, w = q.shape
    h = w // HEAD_PAD
    t = MLA_T if s % MLA_T == 0 else BLOCK
    nb = s // t
    scale = QK_DIM_B ** -0.5
    p_ins, p_in_specs, p_outs, p_out_specs, p_scratch, semantics, at_start, at_end = _hosted(
        plan, 3, 3, (b, h), ("parallel", "parallel"))

    def body(*refs):
        q_ref, k_ref, v_ref = refs[:3]
        o_ref, ob_ref, lse_ref = refs[3 + len(p_ins):6 + len(p_ins)]
        at_start(refs)
        causal = lax.broadcasted_iota(jnp.int32, (t, t), 0) <= lax.broadcasted_iota(jnp.int32, (t, t), 1)
        for i in range(nb):
            q_i = q_ref[i * t:(i + 1) * t, :]

            def step(j, carry, q_i=q_i, diagonal=False):
                m, l, acc = carry
                rows = slice(j * t, (j + 1) * t) if diagonal else pl.ds(pl.multiple_of(j * t, t), t)
                st = lax.dot_general(k_ref[rows, :], q_i, _NT, preferred_element_type=F32) * scale
                if diagonal:
                    st = jnp.where(causal, st, NEG)
                m_new = jnp.maximum(m, jnp.max(st, axis=0, keepdims=True))
                a = jnp.exp(m - m_new)
                p = jnp.exp(st - m_new)
                l = a * l + jnp.sum(p, axis=0, keepdims=True)
                acc = a * acc + lax.dot_general(v_ref[rows, :], p.astype(BF16), _TN, preferred_element_type=F32)
                return m_new, l, acc

            carry = (jnp.full((1, t), NEG, F32), jnp.zeros((1, t), F32), jnp.zeros((HEAD_PAD, t), F32))
            if i > 0:
                carry = lax.fori_loop(0, i, step, carry)
            m, l, acc = step(i, carry, diagonal=True)
            o_i = (acc / l).T
            o_ref[i * t:(i + 1) * t, :] = o_i
            ob_ref[i * t:(i + 1) * t, :] = o_i.astype(BF16)
            lse_ref[i] = m + jnp.log(l)
        at_end(refs)

    hs = _head_spec(s, 0)
    ls = pl.BlockSpec((None, None, nb, 1, t), lambda bi, hi: (bi, hi, 0, 0, 0))
    res = pl.pallas_call(
        body, name=name, grid=(b, h),
        out_shape=(jax.ShapeDtypeStruct((b, s, w), F32), jax.ShapeDtypeStruct((b, s, w), BF16),
                   jax.ShapeDtypeStruct((b, h, nb, 1, t), F32)) + tuple(p_outs),
        in_specs=[hs, hs, hs] + p_in_specs, out_specs=(hs, hs, ls) + tuple(p_out_specs),
        scratch_shapes=p_scratch, compiler_params=_params(semantics),
    )(q, k, v, *p_ins)
    return res[0], res[1], res[2], list(res[3:])


def mla_bwd(name, q, k, v, o, do, lse, plan=None):
    b, s, w = q.shape
    h = w // HEAD_PAD
    _, _, nb, _, t = lse.shape
    scale = QK_DIM_B ** -0.5
    p_ins, p_in_specs, p_outs, p_out_specs, p_scratch, semantics, at_start, at_end = _hosted(
        plan, 6, 3, (b, h), ("parallel", "parallel"))

    def body(*refs):
        q_ref, k_ref, v_ref, o_ref, do_ref, lse_ref = refs[:6]
        dq_ref, dk_ref, dv_ref = refs[6 + len(p_ins):9 + len(p_ins)]
        dv_acc = refs[9 + len(p_ins) + len(p_outs)]
        at_start(refs)
        causal = lax.broadcasted_iota(jnp.int32, (t, t), 0) <= lax.broadcasted_iota(jnp.int32, (t, t), 1)
        dk_ref[...] = jnp.zeros_like(dk_ref)
        dv_acc[...] = jnp.zeros_like(dv_acc)
        for i in range(nb):
            q_i = q_ref[i * t:(i + 1) * t, :]
            do_i = do_ref[i * t:(i + 1) * t, :]
            delta = jnp.sum((o_ref[i * t:(i + 1) * t, :] * do_i).T, axis=0, keepdims=True)
            do_b = do_i.astype(BF16)
            lse_i = lse_ref[i]

            def step(j, dqt, q_i=q_i, do_b=do_b, delta=delta, lse_i=lse_i, diagonal=False):
                rows = slice(j * t, (j + 1) * t) if diagonal else pl.ds(pl.multiple_of(j * t, t), t)
                k_j = k_ref[rows, :]
                st = lax.dot_general(k_j, q_i, _NT, preferred_element_type=F32) * scale
                if diagonal:
                    st = jnp.where(causal, st, NEG)
                pt = jnp.exp(st - lse_i)
                dpt = lax.dot_general(v_ref[rows, :], do_b, _NT, preferred_element_type=F32)
                dst = (pt * (dpt - delta) * scale).astype(BF16)
                dv_acc[rows, :] += jnp.dot(pt.astype(BF16), do_b, preferred_element_type=F32)
                dk_ref[rows, :] += jnp.dot(dst, q_i, preferred_element_type=F32)
                return dqt + lax.dot_general(k_j, dst, _TN, preferred_element_type=F32)

            dqt = jnp.zeros((HEAD_PAD, t), F32)
            if i > 0:
                dqt = lax.fori_loop(0, i, step, dqt)
            dq_ref[i * t:(i + 1) * t, :] = step(i, dqt, diagonal=True).T
        dv_ref[...] = dv_acc[...].astype(BF16)
        at_end(refs)

    hs = _head_spec(s, 0)
    ls = pl.BlockSpec((None, None, nb, 1, t), lambda bi, hi: (bi, hi, 0, 0, 0))
    res = pl.pallas_call(
        body, name=name, grid=(b, h),
        out_shape=(jax.ShapeDtypeStruct((b, s, w), F32), jax.ShapeDtypeStruct((b, s, w), F32),
                   jax.ShapeDtypeStruct((b, s, w), BF16)) + tuple(p_outs),
        in_specs=[hs, hs, hs, hs, hs, ls] + p_in_specs, out_specs=(hs, hs, hs) + tuple(p_out_specs),
        scratch_shapes=[pltpu.VMEM((s, HEAD_PAD), F32)] + p_scratch,
        compiler_params=_params(semantics),
    )(q, k, v, o, do, lse, *p_ins)
    return res[0], res[1], res[2], list(res[3:])


_HBM = pl.BlockSpec(memory_space=pltpu.HBM)


def _place():
    x, y, c = lax.axis_index("x"), lax.axis_index("y"), lax.axis_index("c")
    chips = [(1 - x, y), (x, 1 - y), (1 - x, 1 - y)]
    return x, y, c, chips


def _remote(src, dst, send_sem, recv_sem, dev):
    return pltpu.make_async_remote_copy(src_ref=src, dst_ref=dst, send_sem=send_sem, recv_sem=recv_sem,
                                        device_id=dev, device_id_type=MESH)


class CommPlan:
    def __init__(self, ins, out_shapes, n_sems, start, finish, aliases=None):
        self.ins, self.out_shapes, self.n_sems = list(ins), list(out_shapes), n_sems
        self.start, self.finish, self.aliases = start, finish, dict(aliases or {})

    def scratch(self):
        return [pltpu.SemaphoreType.DMA((self.n_sems,)), pltpu.SemaphoreType.DMA((self.n_sems,))]


def comm_call(name, plan):
    ni, no = len(plan.ins), len(plan.out_shapes)

    def body(*refs):
        ins, outs, (send_sems, recv_sems) = refs[:ni], refs[ni:ni + no], refs[ni + no:]
        plan.start(ins, outs, send_sems, recv_sems)
        plan.finish(ins, outs, send_sems, recv_sems)

    return pl.pallas_call(
        body, name=name, out_shape=tuple(plan.out_shapes), in_specs=[_HBM] * ni, out_specs=tuple([_HBM] * no),
        input_output_aliases=plan.aliases, scratch_shapes=plan.scratch(),
    )(*plan.ins)


def gather_plan(packs, l):
    nk = len(packs)

    def parts(p_refs, o_refs, ss, rs):
        x, y, c, chips = _place()
        s_me = 2 * x + y
        sibling = (x, y, 1 - c)
        first = [_remote(p_refs[k].at[l], o_refs[k].at[s_me], ss.at[j * nk + k], rs.at[j * nk + k], (cx, cy, c))
                 for j, (cx, cy) in enumerate(chips) for k in range(nk)]
        own = [_remote(p_refs[k].at[l], o_refs[k].at[s_me], ss.at[6 * nk + k], rs.at[6 * nk + k], sibling) for k in range(nk)]
        return c, chips, sibling, first, own

    def start(p_refs, o_refs, ss, rs):
        c, _, _, first, own = parts(p_refs, o_refs, ss, rs)
        for cp in own:
            cp.start()

        @pl.when(c == l)
        def _():
            for cp in first:
                cp.start()

    def finish(p_refs, o_refs, ss, rs):
        c, chips, sibling, first, own = parts(p_refs, o_refs, ss, rs)

        @pl.when(c == l)
        def _():
            passed = []
            for j, (cx, cy) in enumerate(chips):
                for k in range(nk):
                    blk = o_refs[k].at[2 * cx + cy]
                    _remote(blk, blk, ss.at[j * nk + k], rs.at[j * nk + k], (cx, cy, c)).wait_recv()
                    fwd = _remote(blk, blk, ss.at[(3 + j) * nk + k], rs.at[(3 + j) * nk + k], sibling)
                    fwd.start()
                    passed.append(fwd)
            for cp in first + passed:
                cp.wait_send()

        @pl.when(c != l)
        def _():
            for j, (cx, cy) in enumerate(chips):
                for k in range(nk):
                    blk = o_refs[k].at[2 * cx + cy]
                    _remote(blk, blk, ss.at[(3 + j) * nk + k], rs.at[(3 + j) * nk + k], sibling).wait_recv()

        for cp in own:
            cp.wait()

    outs = [jax.ShapeDtypeStruct((N_SHARDS,) + p.shape[1:], p.dtype) for p in packs]
    return CommPlan(packs, outs, 7 * nk, start, finish)


def swap_plan(grads, l):
    nk = len(grads)

    def copies(g_refs, o_refs, ss, rs):
        x, y, c, _ = _place()
        return c, [_remote(g_refs[k], o_refs[k], ss.at[k], rs.at[k], (x, y, 1 - c)) for k in range(nk)]

    def start(g_refs, o_refs, ss, rs):
        c, cps = copies(g_refs, o_refs, ss, rs)

        @pl.when(c != l)
        def _():
            for cp in cps:
                cp.start()

    def finish(g_refs, o_refs, ss, rs):
        c, cps = copies(g_refs, o_refs, ss, rs)

        @pl.when(c != l)
        def _():
            for cp in cps:
                cp.wait_send()

        @pl.when(c == l)
        def _():
            for cp in cps:
                cp.wait_recv()

    return CommPlan(grads, [jax.ShapeDtypeStruct(g.shape, g.dtype) for g in grads], nk, start, finish)


def scatter_plan(parts, l):
    nk = len(parts)

    def copies(p_refs, o_refs, ss, rs):
        x, y, c, chips = _place()
        s_me = 2 * x + y
        sends = [_remote(p_refs[k].at[2 * cx + cy], o_refs[k].at[s_me], ss.at[j * nk + k], rs.at[j * nk + k], (cx, cy, c))
                 for j, (cx, cy) in enumerate(chips) for k in range(nk)]
        return c, chips, sends

    def start(p_refs, o_refs, ss, rs):
        c, _, sends = copies(p_refs, o_refs, ss, rs)

        @pl.when(c == l)
        def _():
            for cp in sends:
                cp.start()

    def finish(p_refs, o_refs, ss, rs):
        c, chips, sends = copies(p_refs, o_refs, ss, rs)

        @pl.when(c == l)
        def _():
            for j, (cx, cy) in enumerate(chips):
                for k in range(nk):
                    slot = o_refs[k].at[2 * cx + cy]
                    _remote(slot, slot, ss.at[j * nk + k], rs.at[j * nk + k], (cx, cy, c)).wait_recv()
            for cp in sends:
                cp.wait_send()

    return CommPlan(parts, [jax.ShapeDtypeStruct(p.shape, p.dtype) for p in parts], 3 * nk, start, finish)


def share_plan(finals):
    flat = [(l, f) for l, fs in enumerate(finals) for f in fs]

    def copies(f_refs, o_refs, ss, rs):
        x, y, c, _ = _place()
        return c, [_remote(f_refs[i], o_refs[i], ss.at[i], rs.at[i], (x, y, 1 - c)) for i in range(len(flat))]

    def start(f_refs, o_refs, ss, rs):
        c, cps = copies(f_refs, o_refs, ss, rs)
        for l in range(len(finals)):
            @pl.when(c == l)
            def _(l=l):
                for cp, (lf, _) in zip(cps, flat):
                    if lf == l:
                        cp.start()

    def finish(f_refs, o_refs, ss, rs):
        c, cps = copies(f_refs, o_refs, ss, rs)
        for l in range(len(finals)):
            @pl.when(c == l)
            def _(l=l):
                for cp, (lf, _) in zip(cps, flat):
                    if lf == l:
                        cp.wait_send()

            @pl.when(c != l)
            def _(l=l):
                for cp, (lf, _) in zip(cps, flat):
                    if lf == l:
                        cp.wait_recv()

    arrays = [f for _, f in flat]
    return CommPlan(arrays, [jax.ShapeDtypeStruct(f.shape, f.dtype) for f in arrays], len(flat), start, finish,
                    aliases={i: i for i in range(len(flat))})


def add_pair(name, own, other, tr=256):
    n, cols = own.shape
    tr = _tile(n, tr, 16)

    def body(g_ref, o_ref, out_ref):
        out_ref[...] = (g_ref[...].astype(F32) + o_ref[...].astype(F32)).astype(out_ref.dtype)

    row = pl.BlockSpec((tr, cols), lambda i: (i, 0))
    return pl.pallas_call(
        body, name=name, out_shape=jax.ShapeDtypeStruct((n, cols), BF16), grid=(n // tr,), in_specs=[row, row], out_specs=row,
        compiler_params=_params(("parallel",)),
    )(own, other)


def sum_slots(name, slots, part, shard, tr=256):
    ns, r, cols = slots.shape
    tr = _tile(r, tr, 16)

    def body(s_idx, s_ref, own_ref, o_ref):
        total = own_ref[...].astype(F32)
        for k in range(1, ns):
            total = total + s_ref[(s_idx[0] + k) % ns].astype(F32)
        o_ref[...] = total

    return pl.pallas_call(
        body, name=name, out_shape=jax.ShapeDtypeStruct((r, cols), F32),
        grid_spec=pltpu.PrefetchScalarGridSpec(
            num_scalar_prefetch=1, grid=(r // tr,),
            in_specs=[pl.BlockSpec((ns, tr, cols), lambda i, p: (0, i, 0)), pl.BlockSpec((None, tr, cols), lambda i, p: (p[0], i, 0))],
            out_specs=pl.BlockSpec((tr, cols), lambda i, p: (i, 0))),
        compiler_params=_params(("parallel",)),
    )(shard, slots, part)


def small_allreduce(vec):
    r, cols = vec.shape

    def body(v_ref, o_ref, buf, send_sems, recv_sems):
        x, y, c, _ = _place()
        me = 4 * x + 2 * y + c
        buf[me] = v_ref[...]
        peers = []
        for k in range(1, N_DEV):
            px = (1 - x) if (k & 4) else x
            py = (1 - y) if (k & 2) else y
            pc = (1 - c) if (k & 1) else c
            peers.append((px, py, pc))
        sends = [_remote(buf.at[me], buf.at[me], send_sems.at[k], recv_sems.at[k], peer) for k, peer in enumerate(peers)]
        for cp in sends:
            cp.start()
        for k, (px, py, pc) in enumerate(peers):
            slot = buf.at[4 * px + 2 * py + pc]
            _remote(slot, slot, send_sems.at[k], recv_sems.at[k], (px, py, pc)).wait_recv()
        for cp in sends:
            cp.wait_send()
        total = buf[0]
        for k in range(1, N_DEV):
            total = total + buf[k]
        o_ref[...] = total

    return pl.pallas_call(
        body, name="small_allreduce", out_shape=jax.ShapeDtypeStruct((r, cols), F32),
        in_specs=[pl.BlockSpec(memory_space=pltpu.VMEM)], out_specs=pl.BlockSpec(memory_space=pltpu.VMEM),
        scratch_shapes=[pltpu.VMEM((N_DEV, r, cols), F32), pltpu.SemaphoreType.DMA((N_DEV - 1,)), pltpu.SemaphoreType.DMA((N_DEV - 1,))],
    )(vec)


ARG_NAMES = (("x", "positions") + WEIGHT_NAMES + ("loss_target",) + tuple("m_" + n for n in WEIGHT_NAMES)
             + tuple("v_" + n for n in WEIGHT_NAMES))

_C_KA = N_HEADS_A * HEAD_PAD
_C_VA = _C_KA + N_KV_HEADS_A * HEAD_PAD
_C_CQ = _C_VA + N_KV_HEADS_A * HEAD_PAD
_C_CKV = _C_CQ + Q_LORA_RANK
_C_KR = _C_CKV + KV_LORA_RANK
_C_GA = _C_KR + HEAD_PAD
_R_KA = WIDTH_A
_R_VA = _R_KA + KV_WIDTH_A
_R_CQ = _R_VA + KV_WIDTH_A
_R_KR = _R_CQ + Q_LORA_RANK + KV_LORA_RANK
_R_GA = _R_KR + QK_ROPE_DIM


def _pad_head_cols(w, n_heads):
    r, dh = w.shape[0], w.shape[1] // n_heads
    return jnp.pad(w.reshape(r, n_heads, dh), ((0, 0), (0, 0), (0, HEAD_PAD - dh))).reshape(r, n_heads * HEAD_PAD)


def _unpad_head_cols(w, n_heads, dh):
    r = w.shape[0]
    return w.reshape(r, n_heads, HEAD_PAD)[:, :, :dh].reshape(r, n_heads * dh)


def _pad_head_rows(w, n_heads):
    dh, c = w.shape[0] // n_heads, w.shape[1]
    return jnp.pad(w.reshape(n_heads, dh, c), ((0, 0), (0, HEAD_PAD - dh), (0, 0))).reshape(n_heads * HEAD_PAD, c)


def _unpad_head_rows(w, n_heads, dh):
    c = w.shape[1]
    return w.reshape(n_heads, HEAD_PAD, c)[:, :dh].reshape(n_heads * dh, c)


def _join_cols(shards):
    ns, r, c = shards.shape
    return shards.transpose(1, 0, 2).reshape(r, ns * c)


def _split_cols(mat):
    r, c4 = mat.shape
    return mat.reshape(r, N_SHARDS, c4 // N_SHARDS).transpose(1, 0, 2)


def _ffn_fwd(tag, x, gain, wcol, wdown, plan=None):
    nb = rms_fwd(tag + "_rms", x, gain)
    gu, h, carried = ffn_up_act(tag + "_up", nb, wcol, 0, plan=plan)
    out = gmm_down(tag + "_down", h, wdown, 0, x, 0.5)
    return out, (x, nb, gu, h), carried


def _ffn_bwd(tag, saved, gain, wcol, wdown, dout, doutb, plans=None):
    plans = plans or {}
    x, nb, gu, h = saved
    carried = {}
    dgu, carried["dh"] = ffn_down_dact(tag + "_dh", doutb, wdown, gu, gu.shape[-1], 0, 0.5, plan=plans.get("dh"))
    gdown = gmm_down_dw(tag + "_dwd", h, doutb, 0.5)
    if plans.get("dwgu") is not None:
        gcol, carried["dwgu"] = gmm_up_dw(tag + "_dwgu", nb, dgu, plan=plans["dwgu"])
    else:
        gcol = gmm_up_dw(tag + "_dwgu", nb, dgu)
    if plans.get("dn") is not None:
        dn, carried["dn"] = gmm_up_dx(tag + "_dn", dgu, wcol, x.shape[1], 0, plan=plans["dn"])
    else:
        dn = gmm_up_dx(tag + "_dn", dgu, wcol, x.shape[1], 0)
    dx, dxb, dgain = rms_bwd(tag + "_drms", x, gain, dn, dres=dout)
    return dx, dxb, dgain, gcol, gdown, carried


def _pad_gain(g):
    return jnp.pad(g, ((0, 0), (0, HEAD_PAD - g.shape[1])))


def _mixer_fwd(tag, x1, sm, w, wo, aux, plans=None):
    plans = plans or {}
    b, s = aux["b"], aux["s"]
    t, d = x1.shape
    hb = rms_fwd(tag + "_rms", x1, sm["mix_norm"])
    carried = {}
    if plans.get("proj") is not None:
        proj, carried["proj"] = mm_nn(tag + "_proj", hb, w["win"], tm=512, tn=2048, tk=1024, plan=plans["proj"])
    else:
        proj = mm_nn(tag + "_proj", hb, w["win"], tm=512, tn=2048, tk=1024)
    proj3 = proj.reshape(b, s, proj.shape[1])
    gains = {n: _pad_gain(sm[n]) for n in ("swa_q_norm", "swa_k_norm", "mla_q_norm", "mla_k_norm")}
    cqn = rms_fwd(tag + "_rms_cq", proj, sm["mla_q_lora_norm"], tm=1024, col_blk=_C_CQ // Q_LORA_RANK)
    ckvn = rms_fwd(tag + "_rms_ckv", proj, sm["mla_kv_lora_norm"], tm=1024, col_blk=_C_CKV // KV_LORA_RANK)
    qb_raw = mm_nn(tag + "_uq", cqn, w["wuq"], tm=1024).reshape(b, s, N_HEADS_B * HEAD_PAD)
    kb_raw = mm_nn(tag + "_uk", ckvn, w["wk"], tm=1024).reshape(b, s, N_HEADS_B * HEAD_PAD)
    vb = mm_nn(tag + "_uv", ckvn, w["wv"], tm=1024, out_dtype=BF16).reshape(b, s, N_HEADS_B * HEAD_PAD)
    qah = qk_prep_fwd(tag + "_qa_norm", proj3, 0, N_HEADS_A, HEAD_DIM_A, gains["swa_q_norm"])
    kah = qk_prep_fwd(tag + "_ka_norm", proj3, _C_KA // HEAD_PAD, N_KV_HEADS_A, HEAD_DIM_A, gains["swa_k_norm"])
    sinks = sm["swa_sinks"].reshape(-1)
    oab, carried["swa"] = swa_fwd(tag + "_swa", qah, kah, proj3, _C_VA // HEAD_PAD, aux["pos_col"], aux["pos_row"], aux["slopes"],
                                  sinks, plan=plans.get("swa"))
    kr = (proj3, _C_KR // HEAD_PAD)
    qbh = qk_prep_fwd(tag + "_qb_norm", qb_raw, 0, N_HEADS_B, QK_DIM_B, gains["mla_q_norm"], aux["rope"])
    kbh = qk_prep_fwd(tag + "_kb_norm", kb_raw, 0, N_HEADS_B, QK_DIM_B, gains["mla_k_norm"], aux["rope"], extra=kr)
    ob, obb, lse, carried["mla"] = mla_fwd(tag + "_mla", qbh, kbh, vb, plans.get("mla"))
    oab2, obb2 = oab.reshape(t, -1), obb.reshape(t, -1)
    ya = mm_nn(tag + "_branch_a", oab2, w["wa"], tm=1024)
    yb = mm_nn(tag + "_branch_b", obb2, w["wb"], tm=1024)
    mg = gate_fwd(tag + "_gate", proj, ya, yb, _C_GA, _C_GA + d)
    x2 = gmm_rows(tag + "_out", mg, wo, x1)
    saved = dict(x1=x1, hb=hb, proj=proj, cqn=cqn, ckvn=ckvn, qb_raw=qb_raw, kb_raw=kb_raw, vb=vb, qah=qah, kah=kah,
                 qbh=qbh, kbh=kbh, ob=ob, lse=lse, oab2=oab2, obb2=obb2, ya=ya, yb=yb, mg=mg, gains=gains)
    return x2, saved, carried


def _mixer_bwd(tag, sv, sm, w, wo, aux, dx2, dx2b, plans=None):
    plans = plans or {}
    carried = {}
    b, s = aux["b"], aux["s"]
    t, d = dx2.shape
    proj, gains = sv["proj"], sv["gains"]
    proj3 = proj.reshape(b, s, proj.shape[1])
    kr = (proj3, _C_KR // HEAD_PAD)
    dmg = gmm_rows_dx(tag + "_d_out", dx2b, wo)
    dwo = gmm_rows_dw(tag + "_dw_out", sv["mg"], dx2b)
    dya, dyb, dga, dgb = gate_bwd(tag + "_dgate", proj, sv["ya"], sv["yb"], dmg, _C_GA, _C_GA + d)
    dwa = mm_tn(tag + "_dw_branch_a", sv["oab2"], dya, out_dtype=BF16)
    dwb = mm_tn(tag + "_dw_branch_b", sv["obb2"], dyb, out_dtype=BF16)
    doa = mm_nt(tag + "_d_branch_a", dya, w["wa"], tm=1024).reshape(b, s, -1)
    dob = mm_nt(tag + "_d_branch_b", dyb, w["wb"], tm=1024).reshape(b, s, -1)
    sinks = sm["swa_sinks"].reshape(-1)
    dqah, dkah, dva, dsinks, carried["dswa"] = swa_bwd(tag + "_dswa", sv["qah"], sv["kah"], proj3, _C_VA // HEAD_PAD, aux["pos_col"],
                                                       aux["pos_row"], aux["slopes"], sinks, doa, plan=plans.get("dswa"))
    dqa_raw, dg_swa_q = qk_prep_bwd(tag + "_dqa_norm", proj3, 0, N_HEADS_A, HEAD_DIM_A, gains["swa_q_norm"], dqah)
    dka_raw, dg_swa_k = qk_prep_bwd(tag + "_dka_norm", proj3, _C_KA // HEAD_PAD, N_KV_HEADS_A, HEAD_DIM_A, gains["swa_k_norm"], dkah)
    dqbh, dkbh, dvb, carried["dmla"] = mla_bwd(tag + "_dmla", sv["qbh"], sv["kbh"], sv["vb"], sv["ob"], dob, sv["lse"], plans.get("dmla"))
    dqb_raw, dg_mla_q = qk_prep_bwd(tag + "_dqb_norm", sv["qb_raw"], 0, N_HEADS_B, QK_DIM_B, gains["mla_q_norm"], dqbh, aux["rope"])
    dkb_raw, dkr_sum, dg_mla_k = qk_prep_bwd(tag + "_dkb_norm", sv["kb_raw"], 0, N_HEADS_B, QK_DIM_B, gains["mla_k_norm"], dkbh,
                                             aux["rope"], extra=kr, head_sum=True)
    dq_tok, dk_tok, dv_tok = dqb_raw.reshape(t, -1), dkb_raw.reshape(t, -1), dvb.reshape(t, -1)
    dwuq = mm_tn(tag + "_dw_uq", sv["cqn"], dq_tok, tk=1024, out_dtype=BF16)
    dwk = mm_tn(tag + "_dw_uk", sv["ckvn"], dk_tok, tk=1024, out_dtype=BF16)
    dwv = mm_tn(tag + "_dw_uv", sv["ckvn"], dv_tok, tk=1024, out_dtype=BF16)
    dcqn = mm_nt(tag + "_d_uq", dq_tok, w["wuq"], tm=1024)
    dckvn = mm_nt(tag + "_d_uv", dv_tok, w["wv"], tm=1024, res=mm_nt(tag + "_d_uk", dk_tok, w["wk"], tm=1024))
    dcq, dg_q_lora = rms_bwd(tag + "_drms_cq", proj, sm["mla_q_lora_norm"], dcqn, want_f32=False, tm=1024, col_blk=_C_CQ // Q_LORA_RANK)
    dckv, dg_kv_lora = rms_bwd(tag + "_drms_ckv", proj, sm["mla_kv_lora_norm"], dckvn, want_f32=False, tm=1024,
                               col_blk=_C_CKV // KV_LORA_RANK)
    dproj = jnp.concatenate([dqa_raw.reshape(t, -1), dka_raw.reshape(t, -1), dva.reshape(t, -1).astype(BF16), dcq, dckv,
                             dkr_sum.reshape(t, HEAD_PAD).astype(BF16), dga, dgb], axis=1)
    dwin = mm_tn(tag + "_dw_in", sv["hb"], dproj, tm=1024, tn=2048, tk=1024, out_dtype=BF16)
    dh = mm_nt(tag + "_d_in", dproj, w["win"], tm=512, tn=1024, tk=2048)
    dx1, dx1b, dg_mix = rms_bwd(tag + "_drms", sv["x1"], sm["mix_norm"], dh, dres=dx2)
    wgrads = dict(win=dwin, wuq=dwuq, wk=dwk, wv=dwv, wa=dwa, wb=dwb, wo=dwo)
    sgrads = dict(mix_norm=dg_mix, swa_q_norm=dg_swa_q[:, :HEAD_DIM_A], swa_k_norm=dg_swa_k[:, :HEAD_DIM_A],
                  swa_sinks=dsinks[:, 0].reshape(1, -1), mla_q_lora_norm=dg_q_lora, mla_kv_lora_norm=dg_kv_lora,
                  mla_q_norm=dg_mla_q[:, :QK_DIM_B], mla_k_norm=dg_mla_k[:, :QK_DIM_B])
    return dx1, dx1b, wgrads, sgrads, carried


def _layer_weights(win4, uq4, ukv4, branch4):
    wr = _join_cols(win4)
    kr = jnp.pad(wr[:, _R_KR:_R_GA], ((0, 0), (QK_NOPE_DIM, HEAD_PAD - QK_DIM_B)))
    win = jnp.concatenate([_pad_head_cols(wr[:, :_R_KA], N_HEADS_A), _pad_head_cols(wr[:, _R_KA:_R_VA], N_KV_HEADS_A),
                           _pad_head_cols(wr[:, _R_VA:_R_CQ], N_KV_HEADS_A), wr[:, _R_CQ:_R_KR], kr, wr[:, _R_GA:]], axis=1)
    ukv = _join_cols(ukv4)
    ukv3 = ukv.reshape(ukv.shape[0], N_HEADS_B, QK_NOPE_DIM + V_DIM_B)
    r = branch4.shape[1] // 2
    return dict(win=win, wuq=_pad_head_cols(_join_cols(uq4), N_HEADS_B),
                wk=_pad_head_cols(ukv3[:, :, :QK_NOPE_DIM].reshape(ukv.shape[0], -1), N_HEADS_B),
                wv=_pad_head_cols(ukv3[:, :, QK_NOPE_DIM:].reshape(ukv.shape[0], -1), N_HEADS_B),
                wa=_pad_head_rows(_join_cols(branch4[:, :r]), N_HEADS_A), wb=_pad_head_rows(_join_cols(branch4[:, r:]), N_HEADS_B))


def _mixer_grad_shards(g):
    dw = g["win"]
    win_ref = jnp.concatenate([_unpad_head_cols(dw[:, :_C_KA], N_HEADS_A, HEAD_DIM_A),
                               _unpad_head_cols(dw[:, _C_KA:_C_VA], N_KV_HEADS_A, HEAD_DIM_A),
                               _unpad_head_cols(dw[:, _C_VA:_C_CQ], N_KV_HEADS_A, HEAD_DIM_A), dw[:, _C_CQ:_C_KR],
                               dw[:, _C_KR + QK_NOPE_DIM:_C_KR + QK_DIM_B], dw[:, _C_GA:]], axis=1)
    rk = g["wk"].shape[0]
    ukv = jnp.concatenate([g["wk"].reshape(rk, N_HEADS_B, HEAD_PAD)[:, :, :QK_NOPE_DIM],
                           g["wv"].reshape(rk, N_HEADS_B, HEAD_PAD)[:, :, :V_DIM_B]], axis=2).reshape(rk, -1)
    return (_split_cols(win_ref), _split_cols(_unpad_head_cols(g["wuq"], N_HEADS_B, QK_DIM_B)), _split_cols(ukv),
            jnp.concatenate([_split_cols(_unpad_head_rows(g["wa"], N_HEADS_A, HEAD_DIM_A)),
                             _split_cols(_unpad_head_rows(g["wb"], N_HEADS_B, V_DIM_B))], axis=1))


BUFFER_TAGS = ("col1", "col2", "down1", "down2", "wo", "win", "uq", "ukv", "branch")
FIRST_GATHER = ("col1", "down1")
GATHER_CARRIERS = {(0, "ffn1_up"): (0, ("wo", "win", "uq", "ukv", "branch")), (0, "proj"): (1, ("col1",)),
                   (0, "swa"): (1, ("down1", "wo", "uq", "ukv")), (0, "mla"): (0, ("col2", "down2")),
                   (0, "ffn2_up"): (1, ("win", "branch")), (1, "ffn1_up"): (1, ("col2",)), (1, "swa"): (1, ("down2",))}
SCATTER_CARRIERS = {(0, "ffn2_dh"): (1, ("col1",)), (0, "ffn2_dwgu"): (1, ("col2",)), (0, "ffn2_dn"): (1, ("down1", "down2", "wo")),
                    (0, "dswa"): (1, ("win", "uq", "ukv", "branch")), (0, "dmla"): (0, ("col2", "down2")),
                    (0, "ffn1_dh"): (0, ("wo", "win", "uq", "ukv", "branch"))}
LAST_SCATTER = ("col1", "down1")


def _pick(by_tag, tags):
    return [by_tag[t] for t in tags]


def _train_step(x, positions, target, packs, small, shard):
    depth = 2
    b, s, d = x.shape
    t = b * s
    posf = positions.astype(F32)
    half = QK_ROPE_DIM // 2
    inv_freq = ROPE_BASE ** (-jnp.arange(half, dtype=F32) / half)
    ang = posf[..., None] * inv_freq
    cos, sin = jnp.cos(ang), jnp.sin(ang)
    tail = HEAD_PAD - QK_DIM_B
    rope = (jnp.concatenate([jnp.ones((b, s, QK_NOPE_DIM), F32), cos, cos, jnp.ones((b, s, tail), F32)], axis=-1),
            jnp.concatenate([jnp.zeros((b, s, QK_NOPE_DIM), F32), sin, sin, jnp.zeros((b, s, tail), F32)], axis=-1))
    slopes = jnp.exp2(-8.0 * (jnp.arange(N_HEADS_A, dtype=F32) + 1.0) / N_HEADS_A)
    aux = dict(b=b, s=s, pos_col=posf.reshape(b, s, 1), pos_row=posf.reshape(b, 1, s), rope=rope, slopes=slopes)

    def sm_of(l):
        return {n: small[n][l:l + 1] for n in SMALL_NAMES}

    by_tag = dict(zip(BUFFER_TAGS, packs))
    wts, parts, slots = [{} for _ in range(depth)], [{} for _ in range(depth)], [{} for _ in range(depth)]
    saved, sgrads = [None] * depth, [None] * depth

    def gather_piece(l, kern):
        if (l, kern) not in GATHER_CARRIERS:
            return None
        data_l, tags = GATHER_CARRIERS[(l, kern)]
        return gather_plan(_pick(by_tag, tags), data_l)

    def gathered(l, kern, arrays):
        if (l, kern) in GATHER_CARRIERS:
            data_l, tags = GATHER_CARRIERS[(l, kern)]
            wts[data_l].update(zip(tags, arrays))

    def scatter_piece(l, kern):
        if (l, kern) not in SCATTER_CARRIERS:
            return None
        data_l, tags = SCATTER_CARRIERS[(l, kern)]
        return scatter_plan(_pick(parts[data_l], tags), data_l)

    def scattered(l, kern, arrays):
        if (l, kern) in SCATTER_CARRIERS:
            data_l, tags = SCATTER_CARRIERS[(l, kern)]
            slots[data_l].update(zip(tags, arrays))

    def chip_sums(l, tags, mine):
        theirs = comm_call(f"swap_l{l}_{tags[0]}", swap_plan(_pick(mine, tags), l))
        for tag, o in zip(tags, theirs):
            g = mine[tag]
            parts[l][tag] = add_pair(f"l{l}_add_{tag}", g.reshape(-1, g.shape[-1]), o.reshape(-1, o.shape[-1])).reshape(g.shape)

    wts[0].update(zip(FIRST_GATHER, comm_call("gather_first", gather_plan(_pick(by_tag, FIRST_GATHER), 0))))
    h = x.reshape(t, d)
    for l in range(depth):
        sm, wl = sm_of(l), wts[l]
        h, s1, got = _ffn_fwd(f"l{l}_ffn1", h, sm["ffn1_norm"], wl["col1"], wl["down1"], gather_piece(l, "ffn1_up"))
        gathered(l, "ffn1_up", got)
        weights = _layer_weights(wl["win"], wl["uq"], wl["ukv"], wl["branch"])
        h, s2, got = _mixer_fwd(f"l{l}_mix", h, sm, weights, wl["wo"], aux, {k: gather_piece(l, k) for k in ("proj", "swa", "mla")})
        for kern, arrays in got.items():
            gathered(l, kern, arrays)
        h, s3, got = _ffn_fwd(f"l{l}_ffn2", h, sm["ffn2_norm"], wl["col2"], wl["down2"], gather_piece(l, "ffn2_up"))
        gathered(l, "ffn2_up", got)
        saved[l] = (s1, s2, s3, weights)
    loss, dh, dhb = loss_fwd_bwd("loss", h, target.reshape(t, d))

    for l in reversed(range(depth)):
        sm, wl = sm_of(l), wts[l]
        s1, s2, s3, weights = saved[l]
        staged = l == 0
        dh, dhb, dg_f2, gcol2, gdown2, got = _ffn_bwd(f"l{l}_ffn2", s3, sm["ffn2_norm"], wl["col2"], wl["down2"], dh, dhb,
                                                      {k: scatter_piece(l, "ffn2_" + k) for k in ("dh", "dwgu", "dn")})
        for kern, arrays in got.items():
            scattered(l, "ffn2_" + kern, arrays)
        mine = dict(col2=gcol2, down2=gdown2)
        if staged:
            chip_sums(l, ("col2", "down2"), mine)
        dh, dhb, wg, sg, got = _mixer_bwd(f"l{l}_mix", s2, sm, weights, wl["wo"], aux, dh, dhb,
                                          {k: scatter_piece(l, k) for k in ("dswa", "dmla")})
        for kern, arrays in got.items():
            scattered(l, kern, arrays)
        gwin, guq, gukv, gbranch = _mixer_grad_shards(wg)
        mine.update(wo=wg["wo"], win=gwin, uq=guq, ukv=gukv, branch=gbranch)
        if staged:
            chip_sums(l, ("wo", "win", "uq", "ukv", "branch"), mine)
        dh, dhb, dg_f1, gcol1, gdown1, got = _ffn_bwd(f"l{l}_ffn1", s1, sm["ffn1_norm"], wl["col1"], wl["down1"], dh, dhb,
                                                      {k: scatter_piece(l, "ffn1_" + k) for k in ("dh", "dwgu", "dn")})
        for kern, arrays in got.items():
            scattered(l, "ffn1_" + kern, arrays)
        mine.update(col1=gcol1, down1=gdown1)
        chip_sums(l, LAST_SCATTER if staged else BUFFER_TAGS, mine)
        sg.update(ffn1_norm=dg_f1, ffn2_norm=dg_f2)
        sgrads[l] = sg
    slots[0].update(zip(LAST_SCATTER, comm_call("scatter_last", scatter_plan(_pick(parts[0], LAST_SCATTER), 0))))
    finals = [[sum_slots(f"l{l}_sum_{tag}", slots[l][tag], parts[l][tag], shard) for tag in BUFFER_TAGS] for l in range(depth)]
    shared = comm_call("share", share_plan(finals))
    nk = len(BUFFER_TAGS)
    return loss, dh.reshape(b, s, d), [dict(zip(BUFFER_TAGS, shared[l * nk:(l + 1) * nk])) for l in range(depth)], sgrads


def kernel(x, positions, ffn1_norm, ffn1_w_gate, ffn1_w_up, ffn1_w_down, mix_norm, w_in, swa_q_norm, swa_k_norm, swa_sinks, mla_q_lora_norm, mla_w_uq, mla_kv_lora_norm, mla_w_ukv, mla_q_norm, mla_k_norm, w_branch_a, w_branch_b, w_out, ffn2_norm, ffn2_w_gate, ffn2_w_up, ffn2_w_down, loss_target, m_ffn1_norm, m_ffn1_w_gate, m_ffn1_w_up, m_ffn1_w_down, m_mix_norm, m_w_in, m_swa_q_norm, m_swa_k_norm, m_swa_sinks, m_mla_q_lora_norm, m_mla_w_uq, m_mla_kv_lora_norm, m_mla_w_ukv, m_mla_q_norm, m_mla_k_norm, m_w_branch_a, m_w_branch_b, m_w_out, m_ffn2_norm, m_ffn2_w_gate, m_ffn2_w_up, m_ffn2_w_down, v_ffn1_norm, v_ffn1_w_gate, v_ffn1_w_up, v_ffn1_w_down, v_mix_norm, v_w_in, v_swa_q_norm, v_swa_k_norm, v_swa_sinks, v_mla_q_lora_norm, v_mla_w_uq, v_mla_kv_lora_norm, v_mla_w_ukv, v_mla_q_norm, v_mla_k_norm, v_w_branch_a, v_w_branch_b, v_w_out, v_ffn2_norm, v_ffn2_w_gate, v_ffn2_w_up, v_ffn2_w_down):
    args = (x, positions, ffn1_norm, ffn1_w_gate, ffn1_w_up, ffn1_w_down, mix_norm, w_in, swa_q_norm, swa_k_norm, swa_sinks, mla_q_lora_norm, mla_w_uq, mla_kv_lora_norm, mla_w_ukv, mla_q_norm, mla_k_norm, w_branch_a, w_branch_b, w_out, ffn2_norm, ffn2_w_gate, ffn2_w_up, ffn2_w_down, loss_target, m_ffn1_norm, m_ffn1_w_gate, m_ffn1_w_up, m_ffn1_w_down, m_mix_norm, m_w_in, m_swa_q_norm, m_swa_k_norm, m_swa_sinks, m_mla_q_lora_norm, m_mla_w_uq, m_mla_kv_lora_norm, m_mla_w_ukv, m_mla_q_norm, m_mla_k_norm, m_w_branch_a, m_w_branch_b, m_w_out, m_ffn2_norm, m_ffn2_w_gate, m_ffn2_w_up, m_ffn2_w_down, v_ffn1_norm, v_ffn1_w_gate, v_ffn1_w_up, v_ffn1_w_down, v_mix_norm, v_w_in, v_swa_q_norm, v_swa_k_norm, v_swa_sinks, v_mla_q_lora_norm, v_mla_w_uq, v_mla_kv_lora_norm, v_mla_w_ukv, v_mla_q_norm, v_mla_k_norm, v_w_branch_a, v_w_branch_b, v_w_out, v_ffn2_norm, v_ffn2_w_gate, v_ffn2_w_up, v_ffn2_w_down)
    a = dict(zip(ARG_NAMES, args, strict=True))
    x = a["x"]
    depth = a["ffn1_norm"].shape[0]
    d = x.shape[-1]
    assert depth == 2, "one layer slab per core of a chip"

    def bf16_rows(*names):
        return jnp.concatenate([a[n] for n in names], axis=1).astype(BF16) if len(names) > 1 else a[names[0]].astype(BF16)

    packs = [bf16_rows("ffn1_w_gate", "ffn1_w_up"), bf16_rows("ffn2_w_gate", "ffn2_w_up"), bf16_rows("ffn1_w_down"),
             bf16_rows("ffn2_w_down"), bf16_rows("w_out"), bf16_rows("w_in"), bf16_rows("mla_w_uq"), bf16_rows("mla_w_ukv"),
             bf16_rows("w_branch_a", "w_branch_b")]
    shard_me = (2 * lax.axis_index("x") + lax.axis_index("y")).astype(jnp.int32).reshape(1)
    small = {n: a[n] for n in SMALL_NAMES}

    loss, grad_x, summed, sgrads = _train_step(x, a["positions"], a["loss_target"], packs, small, shard_me)

    full = {tag: jnp.stack([summed[l][tag] for l in range(depth)]) for tag in BUFFER_TAGS}
    r_branch = a["w_branch_a"].shape[1]
    grads = dict(ffn1_w_gate=full["col1"][:, :d], ffn1_w_up=full["col1"][:, d:], ffn2_w_gate=full["col2"][:, :d],
                 ffn2_w_up=full["col2"][:, d:], ffn1_w_down=full["down1"], ffn2_w_down=full["down2"], w_out=full["wo"],
                 w_in=full["win"], mla_w_uq=full["uq"], mla_w_ukv=full["ukv"], w_branch_a=full["branch"][:, :r_branch],
                 w_branch_b=full["branch"][:, r_branch:])

    flat = jnp.concatenate([jnp.concatenate([sgrads[l][n].reshape(-1) for l in range(depth)]) for n in SMALL_NAMES] + [loss.reshape(-1)])
    n_small = flat.shape[0]
    rows = -(-n_small // (8 * LANES)) * 8
    pad = rows * LANES - n_small

    def small_pack(v):
        return jnp.pad(v, (0, pad)).reshape(rows, LANES)

    total = small_allreduce(small_pack(flat))
    w_s, m_s, v_s = (small_pack(jnp.concatenate([a[p + n].reshape(-1) for n in SMALL_NAMES] + [jnp.zeros((1,), F32)]))
                     for p in ("", "m_", "v_"))
    d_s, nm_s, nv_s = adamw("adamw_small", w_s, total, m_s, v_s)

    def small_unpack(buf):
        out, off, flat_b = {}, 0, buf.reshape(-1)
        for n in SMALL_NAMES:
            size = a[n].shape[0] * a[n].shape[1]
            out[n] = flat_b[off:off + size].reshape(a[n].shape)
            off += size
        return out

    grads.update(small_unpack(total))
    delta, new_m, new_v = small_unpack(d_s), small_unpack(nm_s), small_unpack(nv_s)
    for n in PACK_NAMES:
        flip = a[n].shape[2] % LANES != 0

        def view(z):
            return z.transpose(0, 2, 1) if flip else z

        w_v, g_v, m_v, v_v = view(a[n]), view(grads[n]), view(a["m_" + n]), view(a["v_" + n])
        shp = w_v.shape
        two_d = (shp[0] * shp[1], shp[2])
        dn, mn, vn = adamw("adamw_" + n, w_v.reshape(two_d), g_v.reshape(two_d), m_v.reshape(two_d), v_v.reshape(two_d))
        delta[n], new_m[n], new_v[n] = view(dn.reshape(shp)), view(mn.reshape(shp)), view(vn.reshape(shp))
        if flip:
            grads[n] = view(g_v)

    loss_out = total.reshape(-1)[n_small - 1]
    return (loss_out, grad_x, *[grads[n] for n in WEIGHT_NAMES], *[delta[n] for n in WEIGHT_NAMES],
            *[new_m[n] for n in WEIGHT_NAMES], *[new_v[n] for n in WEIGHT_NAMES])
```

```python
import functools

import numpy as np
import jax
import jax.numpy as jnp
from jax import lax
from jax.experimental import pallas as pl
from jax.experimental.pallas import tpu as pltpu

F32 = jnp.float32
BF16 = jnp.bfloat16
MESH = pl.DeviceIdType.MESH

HEAD_DIM_A = 64
N_HEADS_A = 8
N_KV_HEADS_A = 2
GROUP_A = N_HEADS_A // N_KV_HEADS_A
BLOCK = 128
N_HEADS_B = 8
Q_LORA_RANK = 256
KV_LORA_RANK = 128
QK_NOPE_DIM = 64
QK_ROPE_DIM = 32
QK_DIM_B = QK_NOPE_DIM + QK_ROPE_DIM
V_DIM_B = 64
ROPE_BASE = 10000.0
WIDTH_A = N_HEADS_A * HEAD_DIM_A
WIDTH_B = N_HEADS_B * V_DIM_B
KV_WIDTH_A = N_KV_HEADS_A * HEAD_DIM_A
EPS = 1e-6
NEG = -1e30
ADAM_LR = 0.001
ADAM_B1 = 0.9
ADAM_B2 = 0.999
ADAM_EPS = 1e-08
ADAM_WD = 0.01
ADAM_STEP = 10

N_SHARDS = 4
N_DEV = 8
LANES = 128
VMEM_LIMIT = 48 * 1024 * 1024

PACK_NAMES = ("ffn1_w_gate", "ffn1_w_up", "ffn1_w_down", "w_in", "mla_w_uq", "mla_w_ukv",
              "w_branch_a", "w_branch_b", "w_out", "ffn2_w_gate", "ffn2_w_up", "ffn2_w_down")
SMALL_NAMES = ("ffn1_norm", "mix_norm", "swa_q_norm", "swa_k_norm", "swa_sinks", "mla_q_lora_norm",
               "mla_kv_lora_norm", "mla_q_norm", "mla_k_norm", "ffn2_norm")
WEIGHT_NAMES = ("ffn1_norm", "ffn1_w_gate", "ffn1_w_up", "ffn1_w_down", "mix_norm", "w_in", "swa_q_norm",
                "swa_k_norm", "swa_sinks", "mla_q_lora_norm", "mla_w_uq", "mla_kv_lora_norm", "mla_w_ukv",
                "mla_q_norm", "mla_k_norm", "w_branch_a", "w_branch_b", "w_out", "ffn2_norm", "ffn2_w_gate",
                "ffn2_w_up", "ffn2_w_down")


def _params(sem):
    return pltpu.CompilerParams(dimension_semantics=sem, vmem_limit_bytes=VMEM_LIMIT)


def _tile(n, want, align):
    if n <= want:
        return n
    t = (want // align) * align
    while t > align and n % t:
        t -= align
    assert t >= align and n % t == 0, (n, want, align)
    return t


def _mm_call(name, a, b, out_struct, grid, a_spec, b_spec, o_spec, dims, n_red, acc_shape, alpha=1.0, res=None, into=None,
             inner=0, a_cols=0, plan=None):
    if n_red and all(g == 1 for g in grid[len(grid) - n_red:]):
        n_red = 0
    n_par = len(grid) - n_red
    n_in = 2 + int(res is not None) + int(into is not None)
    p_ins, p_in_specs, p_outs, p_out_specs, p_scratch, semantics, at_start, at_end = _hosted(
        plan, n_in, 1, grid, ("parallel",) * n_par + ("arbitrary",) * n_red)

    def body(*refs):
        at_start(refs)
        compute(refs)
        at_end(refs)

    def compute(refs):
        a_ref, b_ref = refs[:2]
        r_ref = refs[2] if res is not None else None
        o_ref, acc_ref = refs[n_in + len(p_ins)], refs[n_in + len(p_ins) + 1 + len(p_outs)]
        if inner:
            def a_of(s):
                return a_ref[:, s * a_cols:(s + 1) * a_cols] if a_cols else a_ref[s]

            part = lax.dot_general(a_of(0), b_ref[0], dims, preferred_element_type=F32)
            for s in range(1, inner):
                part = part + lax.dot_general(a_of(s), b_ref[s], dims, preferred_element_type=F32)
        else:
            part = lax.dot_general(a_ref[...], b_ref[...], dims, preferred_element_type=F32)

        def finish(total):
            if alpha != 1.0:
                total = total * alpha
            if r_ref is not None:
                total = r_ref[...] + total
            o_ref[...] = total.astype(o_ref.dtype)

        if n_red == 0:
            finish(part)
            return
        ids = [pl.program_id(n_par + i) for i in range(n_red)]
        first = functools.reduce(jnp.logical_and, [i == 0 for i in ids])
        last = functools.reduce(jnp.logical_and, [i == grid[n_par + k] - 1 for k, i in enumerate(ids)])

        @pl.when(first)
        def _():
            acc_ref[...] = part

        @pl.when(jnp.logical_not(first))
        def _():
            acc_ref[...] += part

        @pl.when(last)
        def _():
            finish(acc_ref[...])

    in_specs = [a_spec, b_spec] + ([o_spec] if res is not None else [])
    args = (a, b) + ((res,) if res is not None else ())
    aliases = {}
    if into is not None:
        aliases = {len(args): 0}
        in_specs.append(pl.BlockSpec(memory_space=pl.ANY))
        args = args + (into,)
    if plan is None:
        return pl.pallas_call(
            body, name=name, out_shape=out_struct, grid=grid, in_specs=in_specs, out_specs=o_spec,
            scratch_shapes=[pltpu.VMEM(acc_shape, F32)], input_output_aliases=aliases, compiler_params=_params(semantics),
        )(*args)
    res_all = pl.pallas_call(
        body, name=name, out_shape=(out_struct,) + tuple(p_outs), grid=grid, in_specs=in_specs + p_in_specs,
        out_specs=(o_spec,) + tuple(p_out_specs), scratch_shapes=[pltpu.VMEM(acc_shape, F32)] + p_scratch,
        input_output_aliases=aliases, compiler_params=_params(semantics),
    )(*args, *p_ins)
    return res_all[0], list(res_all[1:])


_NN = (((1,), (0,)), ((), ()))
_NT = (((1,), (1,)), ((), ()))
_TN = (((0,), (0,)), ((), ()))


def mm_nn(name, a, b, tm=512, tn=1024, tk=1024, out_dtype=F32, alpha=1.0, res=None, plan=None):
    (m, k), (_, n) = a.shape, b.shape
    tm, tn, tk = _tile(m, tm, 16), _tile(n, tn, LANES), _tile(k, tk, LANES)
    return _mm_call(name, a, b, jax.ShapeDtypeStruct((m, n), out_dtype), (m // tm, n // tn, k // tk),
                    pl.BlockSpec((tm, tk), lambda i, j, kk: (i, kk)), pl.BlockSpec((tk, tn), lambda i, j, kk: (kk, j)),
                    pl.BlockSpec((tm, tn), lambda i, j, kk: (i, j)), _NN, 1, (tm, tn), alpha, res, plan=plan)


def mm_nt(name, a, b, tm=512, tn=1024, tk=1024, out_dtype=F32, alpha=1.0, res=None):
    (m, n), (k, _) = a.shape, b.shape
    tm, tn, tk = _tile(m, tm, 16), _tile(k, tn, LANES), _tile(n, tk, LANES)
    return _mm_call(name, a, b, jax.ShapeDtypeStruct((m, k), out_dtype), (m // tm, k // tn, n // tk),
                    pl.BlockSpec((tm, tk), lambda i, j, kk: (i, kk)), pl.BlockSpec((tn, tk), lambda i, j, kk: (j, kk)),
                    pl.BlockSpec((tm, tn), lambda i, j, kk: (i, j)), _NT, 1, (tm, tn), alpha, res)


def mm_tn(name, a, b, tm=1024, tn=1024, tk=1024, out_dtype=F32, alpha=1.0):
    (m, k), (_, n) = a.shape, b.shape
    tm, tn, tk = _tile(k, tm, LANES), _tile(n, tn, LANES), _tile(m, tk, 16)
    return _mm_call(name, a, b, jax.ShapeDtypeStruct((k, n), out_dtype), (k // tm, n // tn, m // tk),
                    pl.BlockSpec((tk, tm), lambda i, j, kk: (kk, i)), pl.BlockSpec((tk, tn), lambda i, j, kk: (kk, j)),
                    pl.BlockSpec((tm, tn), lambda i, j, kk: (i, j)), _TN, 1, (tm, tn), alpha)


def ffn_up_act(name, a, w, blk, tm=1024, plan=None):
    (m, k), (ns, _, n) = a.shape, w.shape
    tm = _tile(m, tm, 16)
    grid = (ns, m // tm)
    p_ins, p_in_specs, p_outs, p_out_specs, p_scratch, semantics, at_start, at_end = _hosted(plan, 3, 2, grid, ("parallel", "parallel"))

    def body(*refs):
        a_ref, wg_ref, wu_ref = refs[:3]
        gu_ref, h_ref = refs[3 + len(p_ins):5 + len(p_ins)]
        at_start(refs)
        av = a_ref[...]
        gate = jnp.dot(av, wg_ref[...], preferred_element_type=F32)
        up = jnp.dot(av, wu_ref[...], preferred_element_type=F32)
        gu_ref[0] = gate.astype(BF16)
        gu_ref[1] = up.astype(BF16)
        h_ref[...] = (gate * jax.nn.sigmoid(gate) * up).astype(BF16)
        at_end(refs)

    res = pl.pallas_call(
        body, name=name, grid=grid,
        out_shape=(jax.ShapeDtypeStruct((2, ns, m, n), BF16), jax.ShapeDtypeStruct((ns, m, n), BF16)) + tuple(p_outs),
        in_specs=[pl.BlockSpec((tm, k), lambda s, i: (i, 0)), pl.BlockSpec((None, k, n), lambda s, i: (s, blk, 0)),
                  pl.BlockSpec((None, k, n), lambda s, i: (s, blk + 1, 0))] + p_in_specs,
        out_specs=(pl.BlockSpec((2, None, tm, n), lambda s, i: (0, s, i, 0)), pl.BlockSpec((None, tm, n), lambda s, i: (s, i, 0)))
        + tuple(p_out_specs),
        scratch_shapes=p_scratch, compiler_params=_params(semantics),
    )(a, w, w, *p_ins)
    return res[0], res[1], list(res[2:])


def ffn_down_dact(name, a, w, gu, n, blk, alpha, tm=1024, plan=None):
    (m, d), ns = a.shape, w.shape[0]
    tm = _tile(m, tm, 16)
    grid = (ns, m // tm)
    p_ins, p_in_specs, p_outs, p_out_specs, p_scratch, semantics, at_start, at_end = _hosted(plan, 3, 1, grid, ("parallel", "parallel"))

    def body(*refs):
        a_ref, w_ref, gu_ref = refs[:3]
        o_ref = refs[3 + len(p_ins)]
        at_start(refs)
        dh = lax.dot_general(a_ref[...], w_ref[...], _NT, preferred_element_type=F32) * alpha
        gate, up = gu_ref[0].astype(F32), gu_ref[1].astype(F32)
        s = jax.nn.sigmoid(gate)
        o_ref[0] = (dh * up * (s * (1.0 + gate * (1.0 - s)))).astype(BF16)
        o_ref[1] = (dh * (gate * s)).astype(BF16)
        at_end(refs)

    gu_spec = pl.BlockSpec((2, None, tm, n), lambda s, i: (0, s, i, 0))
    res = pl.pallas_call(
        body, name=name, grid=grid, out_shape=(jax.ShapeDtypeStruct((2, ns, m, n), BF16),) + tuple(p_outs),
        in_specs=[pl.BlockSpec((tm, d), lambda s, i: (i, 0)), pl.BlockSpec((None, n, d), lambda s, i: (s, blk, 0)), gu_spec] + p_in_specs,
        out_specs=(gu_spec,) + tuple(p_out_specs), scratch_shapes=p_scratch, compiler_params=_params(semantics),
    )(a, w, gu, *p_ins)
    return res[0], list(res[1:])


def gmm_up_dw(name, a, dgu, tk=2048, plan=None):
    (m, k), (_, ns, _, n) = a.shape, dgu.shape
    tk = _tile(m, tk, 16)
    return _mm_call(name, a, dgu, jax.ShapeDtypeStruct((ns, 2 * k, n), BF16), (2, ns, m // tk),
                    pl.BlockSpec((tk, k), lambda j, s, kk: (kk, 0)), pl.BlockSpec((None, None, tk, n), lambda j, s, kk: (j, s, kk, 0)),
                    pl.BlockSpec((None, k, n), lambda j, s, kk: (s, j, 0)), _TN, 1, (k, n), plan=plan)


def gmm_up_dx(name, dgu, w, k, blk, tm=512, plan=None):
    _, ns, m, n = dgu.shape
    tm = _tile(m, tm, 16)
    return _mm_call(name, dgu, w, jax.ShapeDtypeStruct((m, k), F32), (m // tm, 2),
                    pl.BlockSpec((None, ns, tm, n), lambda i, j: (j, 0, i, 0)),
                    pl.BlockSpec((ns, k, n), lambda i, j: (0, blk + j, 0)),
                    pl.BlockSpec((tm, k), lambda i, j: (i, 0)), _NT, 1, (tm, k), inner=ns, plan=plan)


def gmm_down(name, h, w, blk, res, alpha, tm=512):
    (ns, m, n), d = h.shape, w.shape[2]
    tm = _tile(m, tm, 16)
    return _mm_call(name, h, w, jax.ShapeDtypeStruct((m, d), F32), (m // tm,),
                    pl.BlockSpec((ns, tm, n), lambda i: (0, i, 0)), pl.BlockSpec((ns, n, d), lambda i: (0, blk, 0)),
                    pl.BlockSpec((tm, d), lambda i: (i, 0)), _NN, 0, (8, LANES), alpha, res, inner=ns)


def gmm_down_dw(name, h, b, alpha, tk=2048):
    (ns, m, n), d = h.shape, b.shape[1]
    tk = _tile(m, tk, 16)
    return _mm_call(name, h, b, jax.ShapeDtypeStruct((ns, n, d), BF16), (ns, m // tk),
                    pl.BlockSpec((None, tk, n), lambda s, kk: (s, kk, 0)), pl.BlockSpec((tk, d), lambda s, kk: (kk, 0)),
                    pl.BlockSpec((None, n, d), lambda s, kk: (s, 0, 0)), _TN, 1, (n, d), alpha)


def gmm_rows(name, a, w, res, tm=512):
    (m, _), (ns, r, d) = a.shape, w.shape
    tm = _tile(m, tm, 16)
    return _mm_call(name, a, w, jax.ShapeDtypeStruct((m, d), F32), (m // tm,),
                    pl.BlockSpec((tm, ns * r), lambda i: (i, 0)), pl.BlockSpec((ns, r, d), lambda i: (0, 0, 0)),
                    pl.BlockSpec((tm, d), lambda i: (i, 0)), _NN, 0, (8, LANES), 1.0, res, inner=ns, a_cols=r)


def gmm_rows_dx(name, a, w, tm=512):
    (m, d), (ns, r, _) = a.shape, w.shape
    tm = _tile(m, tm, 16)
    return _mm_call(name, a, w, jax.ShapeDtypeStruct((m, ns * r), F32), (ns, m // tm),
                    pl.BlockSpec((tm, d), lambda s, i: (i, 0)), pl.BlockSpec((None, r, d), lambda s, i: (s, 0, 0)),
                    pl.BlockSpec((tm, r), lambda s, i: (i, s)), _NT, 0, (8, LANES))


def gmm_rows_dw(name, a, b, tk=2048):
    (m, da), d = a.shape, b.shape[1]
    ns, r = N_SHARDS, da // N_SHARDS
    tk = _tile(m, tk, 16)
    return _mm_call(name, a, b, jax.ShapeDtypeStruct((ns, r, d), BF16), (ns, m // tk),
                    pl.BlockSpec((tk, r), lambda s, kk: (kk, s)), pl.BlockSpec((tk, d), lambda s, kk: (kk, 0)),
                    pl.BlockSpec((None, r, d), lambda s, kk: (s, 0, 0)), _TN, 1, (r, d))


def rms_fwd(name, x, gain, tm=512, col_blk=0):
    m, d = x.shape[0], gain.shape[1]
    tm = _tile(m, tm, 16)

    def body(x_ref, g_ref, o_ref):
        xv = x_ref[...]
        r = lax.rsqrt(jnp.mean(xv * xv, axis=-1, keepdims=True) + EPS)
        o_ref[...] = (xv * r * g_ref[...]).astype(o_ref.dtype)

    return pl.pallas_call(
        body, name=name, out_shape=jax.ShapeDtypeStruct((m, d), BF16), grid=(m // tm,),
        in_specs=[pl.BlockSpec((tm, d), lambda i: (i, col_blk)), pl.BlockSpec((1, d), lambda i: (0, 0))],
        out_specs=pl.BlockSpec((tm, d), lambda i: (i, 0)), compiler_params=_params(("parallel",)),
    )(x, gain)


def rms_bwd(name, x, gain, dn, dres=None, want_f32=True, want_bf16=True, tm=512, col_blk=0):
    m, d = x.shape[0], gain.shape[1]
    tm = _tile(m, tm, 16)
    n_out = int(want_f32) + int(want_bf16)

    def body(*refs):
        x_ref, g_ref, dn_ref = refs[:3]
        pos = 3
        r_ref = None
        if dres is not None:
            r_ref = refs[pos]
            pos += 1
        outs = refs[pos:pos + n_out]
        dg_ref = refs[pos + n_out]
        xv = x_ref[...]
        r = lax.rsqrt(jnp.mean(xv * xv, axis=-1, keepdims=True) + EPS)
        xhat = xv * r
        dnv = dn_ref[...]
        dxhat = dnv * g_ref[...]
        dx = r * (dxhat - xhat * jnp.mean(dxhat * xhat, axis=-1, keepdims=True))
        if r_ref is not None:
            dx = r_ref[...] + dx
        for o in outs:
            o[...] = dx.astype(o.dtype)
        part = jnp.sum(dnv * xhat, axis=0, keepdims=True)

        @pl.when(pl.program_id(0) == 0)
        def _():
            dg_ref[...] = part

        @pl.when(pl.program_id(0) > 0)
        def _():
            dg_ref[...] += part

    row = pl.BlockSpec((tm, d), lambda i: (i, 0))
    vec = pl.BlockSpec((1, d), lambda i: (0, 0))
    out_shape = ([jax.ShapeDtypeStruct((m, d), F32)] if want_f32 else []) + ([jax.ShapeDtypeStruct((m, d), BF16)] if want_bf16 else [])
    res = pl.pallas_call(
        body, name=name, out_shape=tuple(out_shape) + (jax.ShapeDtypeStruct((1, d), F32),), grid=(m // tm,),
        in_specs=[pl.BlockSpec((tm, d), lambda i: (i, col_blk)), vec, row] + ([row] if dres is not None else []),
        out_specs=tuple([row] * n_out) + (vec,), compiler_params=_params(("arbitrary",)),
    )(*((x, gain, dn) + ((dres,) if dres is not None else ())))
    return res


def gate_fwd(name, proj, ya, yb, off_a, off_b, tm=256):
    m, d = ya.shape
    tm = _tile(m, tm, 16)
    blk_a, blk_b = off_a // d, off_b // d
    assert blk_a * d == off_a and blk_b * d == off_b, "the gates start on multiples of their width"

    def body(ga_ref, gb_ref, ya_ref, yb_ref, o_ref):
        o_ref[...] = (jax.nn.sigmoid(ga_ref[...]) * ya_ref[...] + jax.nn.sigmoid(gb_ref[...]) * yb_ref[...]).astype(o_ref.dtype)

    row = pl.BlockSpec((tm, d), lambda i: (i, 0))
    return pl.pallas_call(
        body, name=name, out_shape=jax.ShapeDtypeStruct((m, d), BF16), grid=(m // tm,),
        in_specs=[pl.BlockSpec((tm, d), lambda i: (i, blk_a)), pl.BlockSpec((tm, d), lambda i: (i, blk_b)), row, row], out_specs=row,
        compiler_params=_params(("parallel",)),
    )(proj, proj, ya, yb)


def gate_bwd(name, proj, ya, yb, dmerged, off_a, off_b, tm=256):
    m, d = ya.shape
    tm = _tile(m, tm, 16)
    blk_a, blk_b = off_a // d, off_b // d
    assert blk_a * d == off_a and blk_b * d == off_b, "the gates start on multiples of their width"

    def body(ga_ref, gb_ref, ya_ref, yb_ref, dm_ref, dya_ref, dyb_ref, dga_ref, dgb_ref):
        sa, sb = jax.nn.sigmoid(ga_ref[...]), jax.nn.sigmoid(gb_ref[...])
        dm = dm_ref[...]
        dya_ref[...] = (dm * sa).astype(BF16)
        dyb_ref[...] = (dm * sb).astype(BF16)
        dga_ref[...] = (dm * ya_ref[...] * (sa * (1.0 - sa))).astype(BF16)
        dgb_ref[...] = (dm * yb_ref[...] * (sb * (1.0 - sb))).astype(BF16)

    row = pl.BlockSpec((tm, d), lambda i: (i, 0))
    o = jax.ShapeDtypeStruct((m, d), BF16)
    return pl.pallas_call(
        body, name=name, out_shape=(o, o, o, o), grid=(m // tm,),
        in_specs=[pl.BlockSpec((tm, d), lambda i: (i, blk_a)), pl.BlockSpec((tm, d), lambda i: (i, blk_b)), row, row, row],
        out_specs=(row, row, row, row), compiler_params=_params(("parallel",)),
    )(proj, proj, ya, yb, dmerged)


def loss_fwd_bwd(name, y, target, tm=512):
    m, d = y.shape
    tm = _tile(m, tm, 16)

    def body(y_ref, t_ref, l_ref, dy_ref, dyb_ref):
        err = y_ref[...] - t_ref[...]
        dy = err * (1.0 / d)
        dy_ref[...] = dy
        dyb_ref[...] = dy.astype(BF16)
        part = 0.5 * jnp.sum(jnp.mean(err * err, axis=-1, keepdims=True), axis=0, keepdims=True)

        @pl.when(pl.program_id(0) == 0)
        def _():
            l_ref[...] = part

        @pl.when(pl.program_id(0) > 0)
        def _():
            l_ref[...] += part

    row = pl.BlockSpec((tm, d), lambda i: (i, 0))
    return pl.pallas_call(
        body, name=name, grid=(m // tm,),
        out_shape=(jax.ShapeDtypeStruct((1, 1), F32), jax.ShapeDtypeStruct((m, d), F32), jax.ShapeDtypeStruct((m, d), BF16)),
        in_specs=[row, row], out_specs=(pl.BlockSpec((1, 1), lambda i: (0, 0)), row, row),
        compiler_params=_params(("arbitrary",)),
    )(y, target)


def adamw(name, w, g, m, v):
    r, c = w.shape
    tr = _tile(r, max(8, (2 * 1024 * 1024) // (4 * c) // 8 * 8), 8)
    c1 = np.float32(1.0 - ADAM_B1 ** ADAM_STEP)
    c2 = np.float32(1.0 - ADAM_B2 ** ADAM_STEP)

    def body(w_ref, g_ref, m_ref, v_ref, d_ref, nm_ref, nv_ref):
        gv = g_ref[...]
        nm = ADAM_B1 * m_ref[...] + (1.0 - ADAM_B1) * gv
        nv = ADAM_B2 * v_ref[...] + (1.0 - ADAM_B2) * (gv * gv)
        d_ref[...] = -ADAM_LR * ((nm / c1) / (jnp.sqrt(nv / c2) + ADAM_EPS) + ADAM_WD * w_ref[...])
        nm_ref[...] = nm
        nv_ref[...] = nv

    row = pl.BlockSpec((tr, c), lambda i: (i, 0))
    o = jax.ShapeDtypeStruct((r, c), F32)
    return pl.pallas_call(
        body, name=name, out_shape=(o, o, o), grid=(r // tr,), in_specs=[row] * 4, out_specs=(row, row, row),
        compiler_params=_params(("parallel",)),
    )(w, g, m, v)


HEAD_PAD = LANES


def _rope_rot():
    r = np.zeros((HEAD_PAD, HEAD_PAD), np.float32)
    half = QK_ROPE_DIM // 2
    for j in range(half):
        r[QK_NOPE_DIM + half + j, QK_NOPE_DIM + j] = -1.0
        r[QK_NOPE_DIM + j, QK_NOPE_DIM + half + j] = 1.0
    return r


def _head_spec(s, blk0):
    return pl.BlockSpec((None, s, HEAD_PAD), lambda bi, hi: (bi, 0, blk0 + hi))


def qk_prep_fwd(name, x, blk0, n_heads, d_real, gain, rope=None, extra=None):
    b, s, _ = x.shape

    def body(*refs):
        x_ref, g_ref = refs[0], refs[1]
        pos = 2
        xv = x_ref[...]
        if extra is not None:
            xv = xv + refs[pos][...]
            pos += 1
        y = xv * lax.rsqrt(jnp.sum(xv * xv, axis=-1, keepdims=True) * (1.0 / d_real) + EPS) * g_ref[...]
        if rope is not None:
            c_ref, s_ref, r_ref = refs[pos:pos + 3]
            rot = jnp.dot(y, r_ref[...], precision=lax.Precision.HIGHEST, preferred_element_type=F32)
            y = y * c_ref[...] + rot * s_ref[...]
        refs[-1][...] = y.astype(BF16)

    vec = pl.BlockSpec((1, HEAD_PAD), lambda bi, hi: (0, 0))
    tab = pl.BlockSpec((None, s, HEAD_PAD), lambda bi, hi: (bi, 0, 0))
    in_specs, args = [_head_spec(s, blk0), vec], [x, gain]
    if extra is not None:
        e_blk = extra[1]
        in_specs.append(pl.BlockSpec((None, s, HEAD_PAD), lambda bi, hi: (bi, 0, e_blk)))
        args.append(extra[0])
    if rope is not None:
        in_specs += [tab, tab, pl.BlockSpec((HEAD_PAD, HEAD_PAD), lambda bi, hi: (0, 0))]
        args += [rope[0], rope[1], jnp.asarray(_rope_rot())]
    return pl.pallas_call(
        body, name=name, out_shape=jax.ShapeDtypeStruct((b, s, n_heads * HEAD_PAD), BF16), grid=(b, n_heads),
        in_specs=in_specs, out_specs=_head_spec(s, 0), compiler_params=_params(("parallel", "parallel")),
    )(*args)


def qk_prep_bwd(name, x, blk0, n_heads, d_real, gain, dxh, rope=None, extra=None, head_sum=False):
    b, s, _ = x.shape

    def body(*refs):
        x_ref, g_ref, dy_ref = refs[:3]
        pos = 3
        xv = x_ref[...]
        if extra is not None:
            xv = xv + refs[pos][...]
            pos += 1
        dy = dy_ref[...]
        if rope is not None:
            c_ref, s_ref, rt_ref = refs[pos:pos + 3]
            pos += 3
            dy = dy * c_ref[...] + jnp.dot(dy * s_ref[...], rt_ref[...], precision=lax.Precision.HIGHEST, preferred_element_type=F32)
        outs = refs[pos:]
        dx_ref, dg_ref = outs[0], outs[-1]
        r = lax.rsqrt(jnp.sum(xv * xv, axis=-1, keepdims=True) * (1.0 / d_real) + EPS)
        xhat = xv * r
        dxhat = dy * g_ref[...]
        dx = r * (dxhat - xhat * (jnp.sum(dxhat * xhat, axis=-1, keepdims=True) * (1.0 / d_real)))
        dx_ref[...] = dx.astype(dx_ref.dtype)
        part = jnp.sum(dy * xhat, axis=0, keepdims=True)
        first = jnp.logical_and(pl.program_id(0) == 0, pl.program_id(1) == 0)

        @pl.when(first)
        def _():
            dg_ref[...] = part

        @pl.when(jnp.logical_not(first))
        def _():
            dg_ref[...] += part

        if head_sum:
            hs_ref = outs[1]

            @pl.when(pl.program_id(1) == 0)
            def _():
                hs_ref[...] = dx

            @pl.when(pl.program_id(1) > 0)
            def _():
                hs_ref[...] += dx

    vec = pl.BlockSpec((1, HEAD_PAD), lambda bi, hi: (0, 0))
    tab = pl.BlockSpec((None, s, HEAD_PAD), lambda bi, hi: (bi, 0, 0))
    in_specs, args = [_head_spec(s, blk0), vec, _head_spec(s, 0)], [x, gain, dxh]
    if extra is not None:
        e_blk = extra[1]
        in_specs.append(pl.BlockSpec((None, s, HEAD_PAD), lambda bi, hi: (bi, 0, e_blk)))
        args.append(extra[0])
    if rope is not None:
        in_specs += [tab, tab, pl.BlockSpec((HEAD_PAD, HEAD_PAD), lambda bi, hi: (0, 0))]
        args += [rope[0], rope[1], jnp.asarray(_rope_rot().T.copy())]
    out_shape, out_specs = [jax.ShapeDtypeStruct((b, s, n_heads * HEAD_PAD), BF16)], [_head_spec(s, 0)]
    if head_sum:
        out_shape.append(jax.ShapeDtypeStruct((b, s, HEAD_PAD), F32))
        out_specs.append(tab)
    out_shape.append(jax.ShapeDtypeStruct((1, HEAD_PAD), F32))
    out_specs.append(vec)
    return pl.pallas_call(
        body, name=name, out_shape=tuple(out_shape), grid=(b, n_heads), in_specs=in_specs, out_specs=tuple(out_specs),
        compiler_params=_params(("arbitrary", "arbitrary")),
    )(*args)


def _swa_specs(v_blk0):
    q = pl.BlockSpec((None, BLOCK, GROUP_A * HEAD_PAD), lambda b, kv, n: (b, n, kv))
    kprev = pl.BlockSpec((None, BLOCK, HEAD_PAD), lambda b, kv, n: (b, jnp.maximum(n - 1, 0), kv))
    kcur = pl.BlockSpec((None, BLOCK, HEAD_PAD), lambda b, kv, n: (b, n, kv))
    vprev = pl.BlockSpec((None, BLOCK, HEAD_PAD), lambda b, kv, n: (b, jnp.maximum(n - 1, 0), v_blk0 + kv))
    vcur = pl.BlockSpec((None, BLOCK, HEAD_PAD), lambda b, kv, n: (b, n, v_blk0 + kv))
    pcol = pl.BlockSpec((None, BLOCK, 1), lambda b, kv, n: (b, n, 0))
    prow_prev = pl.BlockSpec((None, 1, BLOCK), lambda b, kv, n: (b, 0, jnp.maximum(n - 1, 0)))
    prow_cur = pl.BlockSpec((None, 1, BLOCK), lambda b, kv, n: (b, 0, n))
    smem = pl.BlockSpec(memory_space=pltpu.SMEM)
    return q, kprev, kcur, vprev, vcur, pcol, prow_prev, prow_cur, smem


def _swa_probs(q, kk, dist, valid, slope, sink):
    sc = lax.dot_general(q, kk, _NT, preferred_element_type=F32) * (HEAD_DIM_A ** -0.5)
    sc = sc - slope * dist
    sc = jnp.where(valid, sc, NEG)
    m = jnp.maximum(jnp.max(sc, axis=-1, keepdims=True), sink)
    e = jnp.exp(sc - m)
    es = jnp.exp(sink - m)
    inv = 1.0 / (jnp.sum(e, axis=-1, keepdims=True) + es)
    return e * inv, es * inv


def _swa_window(n, kp_ref, kc_ref, vp_ref, vc_ref, pc_ref, prp_ref, prc_ref):
    kk = jnp.concatenate([kp_ref[...], kc_ref[...]], axis=0)
    vv = jnp.concatenate([vp_ref[...], vc_ref[...]], axis=0).astype(BF16)
    dist = pc_ref[...] - jnp.concatenate([prp_ref[...], prc_ref[...]], axis=1)
    qi = lax.broadcasted_iota(jnp.int32, (BLOCK, 2 * BLOCK), 0) + BLOCK
    ki = lax.broadcasted_iota(jnp.int32, (BLOCK, 2 * BLOCK), 1)
    diff = qi - ki
    valid = (diff >= 0) & (diff < BLOCK) & ((n > 0) | (ki >= BLOCK))
    return kk, vv, dist, valid


def swa_fwd(name, q, k, vsrc, v_blk0, pos_col, pos_row, slopes, sinks, plan=None):
    b, s, _ = q.shape
    qs, kprev, kcur, vprev, vcur, pcol, prp, prc, smem = _swa_specs(v_blk0)
    grid = (b, N_KV_HEADS_A, s // BLOCK)
    p_ins, p_in_specs, p_outs, p_out_specs, p_scratch, semantics, at_start, at_end = _hosted(plan, 10, 1, grid, ("parallel",) * 3)

    def body(*refs):
        q_ref, kp_ref, kc_ref, vp_ref, vc_ref, pc_ref, prp_ref, prc_ref, sl_ref, sk_ref = refs[:10]
        o_ref = refs[10 + len(p_ins)]
        at_start(refs)
        kv, n = pl.program_id(1), pl.program_id(2)
        kk, vv, dist, valid = _swa_window(n, kp_ref, kc_ref, vp_ref, vc_ref, pc_ref, prp_ref, prc_ref)
        for g in range(GROUP_A):
            hd = kv * GROUP_A + g
            lanes = slice(g * HEAD_PAD, (g + 1) * HEAD_PAD)
            p, _ = _swa_probs(q_ref[:, lanes], kk, dist, valid, sl_ref[hd], sk_ref[hd])
            o_ref[:, lanes] = jnp.dot(p.astype(BF16), vv, preferred_element_type=F32).astype(BF16)
        at_end(refs)

    res = pl.pallas_call(
        body, name=name, out_shape=(jax.ShapeDtypeStruct(q.shape, BF16),) + tuple(p_outs), grid=grid,
        in_specs=[qs, kprev, kcur, vprev, vcur, pcol, prp, prc, smem, smem] + p_in_specs, out_specs=(qs,) + tuple(p_out_specs),
        scratch_shapes=p_scratch, compiler_params=_params(semantics),
    )(q, k, k, vsrc, vsrc, pos_col, pos_row, pos_row, slopes, sinks, *p_ins)
    return res[0], list(res[1:])


def swa_bwd(name, q, k, vsrc, v_blk0, pos_col, pos_row, slopes, sinks, do, plan=None):
    b, s, _ = q.shape
    qs, kprev, kcur, vprev, vcur, pcol, prp, prc, smem = _swa_specs(v_blk0)
    grid = (b, N_KV_HEADS_A, s // BLOCK)
    p_ins, p_in_specs, p_outs, p_out_specs, p_scratch, semantics, at_start, at_end = _hosted(plan, 11, 4, grid, ("arbitrary",) * 3)

    def body(*refs):
        at_start(refs)
        compute(*refs[:11], *refs[11 + len(p_ins):15 + len(p_ins)])
        at_end(refs)

    def compute(q_ref, kp_ref, kc_ref, vp_ref, vc_ref, pc_ref, prp_ref, prc_ref, sl_ref, sk_ref, do_ref, dq_ref, dk_ref, dv_ref, ds_ref):
        bi, kv, n = pl.program_id(0), pl.program_id(1), pl.program_id(2)
        kk, vv, dist, valid = _swa_window(n, kp_ref, kc_ref, vp_ref, vc_ref, pc_ref, prp_ref, prc_ref)

        @pl.when((bi == 0) & (kv == 0) & (n == 0))
        def _():
            ds_ref[...] = jnp.zeros_like(ds_ref)

        @pl.when(n == 0)
        def _():
            dk_ref[...] = jnp.zeros_like(dk_ref)
            dv_ref[...] = jnp.zeros_like(dv_ref)

        dkk = jnp.zeros((2 * BLOCK, HEAD_PAD), F32)
        dvv = jnp.zeros((2 * BLOCK, HEAD_PAD), F32)
        head_row = lax.broadcasted_iota(jnp.int32, (N_HEADS_A, LANES), 0)
        dsink = jnp.zeros((N_HEADS_A, LANES), F32)
        for g in range(GROUP_A):
            hd = kv * GROUP_A + g
            lanes = slice(g * HEAD_PAD, (g + 1) * HEAD_PAD)
            qg = q_ref[:, lanes]
            p, ps = _swa_probs(qg, kk, dist, valid, sl_ref[hd], sk_ref[hd])
            dob = do_ref[:, lanes].astype(BF16)
            dvv = dvv + lax.dot_general(p.astype(BF16), dob, _TN, preferred_element_type=F32)
            dp = lax.dot_general(dob, vv, _NT, preferred_element_type=F32)
            rs = jnp.sum(p * dp, axis=-1, keepdims=True)
            dsb = (p * (dp - rs) * (HEAD_DIM_A ** -0.5)).astype(BF16)
            dq_ref[:, lanes] = jnp.dot(dsb, kk, preferred_element_type=F32)
            dkk = dkk + lax.dot_general(dsb, qg, _TN, preferred_element_type=F32)
            dsink = dsink + jnp.where(head_row == hd, -jnp.sum(ps * rs), 0.0)
        ds_ref[...] += dsink

        @pl.when(n > 0)
        def _():
            start = pl.multiple_of((n - 1) * BLOCK, BLOCK)
            dk_ref[pl.ds(start, 2 * BLOCK), :] += dkk
            dv_ref[pl.ds(start, 2 * BLOCK), :] += dvv

        @pl.when(n == 0)
        def _():
            dk_ref[0:BLOCK, :] += dkk[BLOCK:, :]
            dv_ref[0:BLOCK, :] += dvv[BLOCK:, :]

    kv_full = pl.BlockSpec((None, s, HEAD_PAD), lambda bi, kv, n: (bi, 0, kv))
    kv_shape = jax.ShapeDtypeStruct(k.shape, F32)
    res = pl.pallas_call(
        body, name=name, grid=grid,
        out_shape=(jax.ShapeDtypeStruct(q.shape, F32), kv_shape, kv_shape, jax.ShapeDtypeStruct((N_HEADS_A, LANES), F32)) + tuple(p_outs),
        in_specs=[qs, kprev, kcur, vprev, vcur, pcol, prp, prc, smem, smem, qs] + p_in_specs,
        out_specs=(qs, kv_full, kv_full, pl.BlockSpec((N_HEADS_A, LANES), lambda bi, kv, n: (0, 0))) + tuple(p_out_specs),
        scratch_shapes=p_scratch, compiler_params=_params(semantics),
    )(q, k, k, vsrc, vsrc, pos_col, pos_row, pos_row, slopes, sinks, do, *p_ins)
    return res[0], res[1], res[2], res[3], list(res[4:])


MLA_T = 512


def _hosted(plan, n_in, n_out, grid, semantics):
    if plan is None:
        return [], [], [], [], [], semantics, (lambda refs: None), (lambda refs: None)
    ni, no = len(plan.ins), len(plan.out_shapes)

    def split(refs):
        return refs[n_in:n_in + ni], refs[n_in + ni + n_out:n_in + ni + n_out + no], refs[-2], refs[-1]

    def at_start(refs):
        @pl.when(functools.reduce(jnp.logical_and, [pl.program_id(ax) == 0 for ax in range(len(grid))]))
        def _():
            plan.start(*split(refs))

    def at_end(refs):
        @pl.when(functools.reduce(jnp.logical_and, [pl.program_id(ax) == g - 1 for ax, g in enumerate(grid)]))
        def _():
            plan.finish(*split(refs))

    return plan.ins, [_HBM] * ni, plan.out_shapes, [_HBM] * no, plan.scratch(), ("arbitrary",) * len(grid), at_start, at_end


def mla_fwd(name, q, k, v, plan=None):
    b, s, w = q.shape
    h = w // HEAD_PAD
    t = MLA_T if s % MLA_T == 0 else BLOCK
    nb = s // t
    scale = QK_DIM_B ** -0.5
    p_ins, p_in_specs, p_outs, p_out_specs, p_scratch, semantics, at_start, at_end = _hosted(
        plan, 3, 3, (b, h), ("parallel", "parallel"))

    def body(*refs):
        q_ref, k_ref, v_ref = refs[:3]
        o_ref, ob_ref, lse_ref = refs[3 + len(p_ins):6 + len(p_ins)]
        at_start(refs)
        causal = lax.broadcasted_iota(jnp.int32, (t, t), 0) <= lax.broadcasted_iota(jnp.int32, (t, t), 1)
        for i in range(nb):
            q_i = q_ref[i * t:(i + 1) * t, :]

            def step(j, carry, q_i=q_i, diagonal=False):
                m, l, acc = carry
                rows = slice(j * t, (j + 1) * t) if diagonal else pl.ds(pl.multiple_of(j * t, t), t)
                st = lax.dot_general(k_ref[rows, :], q_i, _NT, preferred_element_type=F32) * scale
                if diagonal:
                    st = jnp.where(causal, st, NEG)
                m_new = jnp.maximum(m, jnp.max(st, axis=0, keepdims=True))
                a = jnp.exp(m - m_new)
                p = jnp.exp(st - m_new)
                l = a * l + jnp.sum(p, axis=0, keepdims=True)
                acc = a * acc + lax.dot_general(v_ref[rows, :], p.astype(BF16), _TN, preferred_element_type=F32)
                return m_new, l, acc

            carry = (jnp.full((1, t), NEG, F32), jnp.zeros((1, t), F32), jnp.zeros((HEAD_PAD, t), F32))
            if i > 0:
                carry = lax.fori_loop(0, i, step, carry)
            m, l, acc = step(i, carry, diagonal=True)
            o_i = (acc / l).T
            o_ref[i * t:(i + 1) * t, :] = o_i
            ob_ref[i * t:(i + 1) * t, :] = o_i.astype(BF16)
            lse_ref[i] = m + jnp.log(l)
        at_end(refs)

    hs = _head_spec(s, 0)
    ls = pl.BlockSpec((None, None, nb, 1, t), lambda bi, hi: (bi, hi, 0, 0, 0))
    res = pl.pallas_call(
        body, name=name, grid=(b, h),
        out_shape=(jax.ShapeDtypeStruct((b, s, w), F32), jax.ShapeDtypeStruct((b, s, w), BF16),
                   jax.ShapeDtypeStruct((b, h, nb, 1, t), F32)) + tuple(p_outs),
        in_specs=[hs, hs, hs] + p_in_specs, out_specs=(hs, hs, ls) + tuple(p_out_specs),
        scratch_shapes=p_scratch, compiler_params=_params(semantics),
    )(q, k, v, *p_ins)
    return res[0], res[1], res[2], list(res[3:])


def mla_bwd(name, q, k, v, o, do, lse, plan=None):
    b, s, w = q.shape
    h = w // HEAD_PAD
    _, _, nb, _, t = lse.shape
    scale = QK_DIM_B ** -0.5
    p_ins, p_in_specs, p_outs, p_out_specs, p_scratch, semantics, at_start, at_end = _hosted(
        plan, 6, 3, (b, h), ("parallel", "parallel"))

    def body(*refs):
        q_ref, k_ref, v_ref, o_ref, do_ref, lse_ref = refs[:6]
        dq_ref, dk_ref, dv_ref = refs[6 + len(p_ins):9 + len(p_ins)]
        dv_acc = refs[9 + len(p_ins) + len(p_outs)]
        at_start(refs)
        causal = lax.broadcasted_iota(jnp.int32, (t, t), 0) <= lax.broadcasted_iota(jnp.int32, (t, t), 1)
        dk_ref[...] = jnp.zeros_like(dk_ref)
        dv_acc[...] = jnp.zeros_like(dv_acc)
        for i in range(nb):
            q_i = q_ref[i * t:(i + 1) * t, :]
            do_i = do_ref[i * t:(i + 1) * t, :]
            delta = jnp.sum((o_ref[i * t:(i + 1) * t, :] * do_i).T, axis=0, keepdims=True)
            do_b = do_i.astype(BF16)
            lse_i = lse_ref[i]

            def step(j, dqt, q_i=q_i, do_b=do_b, delta=delta, lse_i=lse_i, diagonal=False):
                rows = slice(j * t, (j + 1) * t) if diagonal else pl.ds(pl.multiple_of(j * t, t), t)
                k_j = k_ref[rows, :]
                st = lax.dot_general(k_j, q_i, _NT, preferred_element_type=F32) * scale
                if diagonal:
                    st = jnp.where(causal, st, NEG)
                pt = jnp.exp(st - lse_i)
                dpt = lax.dot_general(v_ref[rows, :], do_b, _NT, preferred_element_type=F32)
                dst = (pt * (dpt - delta) * scale).astype(BF16)
                dv_acc[rows, :] += jnp.dot(pt.astype(BF16), do_b, preferred_element_type=F32)
                dk_ref[rows, :] += jnp.dot(dst, q_i, preferred_element_type=F32)
                return dqt + lax.dot_general(k_j, dst, _TN, preferred_element_type=F32)

            dqt = jnp.zeros((HEAD_PAD, t), F32)
            if i > 0:
                dqt = lax.fori_loop(0, i, step, dqt)
            dq_ref[i * t:(i + 1) * t, :] = step(i, dqt, diagonal=True).T
        dv_ref[...] = dv_acc[...].astype(BF16)
        at_end(refs)

    hs = _head_spec(s, 0)
    ls = pl.BlockSpec((None, None, nb, 1, t), lambda bi, hi: (bi, hi, 0, 0, 0))
    res = pl.pallas_call(
        body, name=name, grid=(b, h),
        out_shape=(jax.ShapeDtypeStruct((b, s, w), F32), jax.ShapeDtypeStruct((b, s, w), F32),
                   jax.ShapeDtypeStruct((b, s, w), BF16)) + tuple(p_outs),
        in_specs=[hs, hs, hs, hs, hs, ls] + p_in_specs, out_specs=(hs, hs, hs) + tuple(p_out_specs),
        scratch_shapes=[pltpu.VMEM((s, HEAD_PAD), F32)] + p_scratch,
        compiler_params=_params(semantics),
    )(q, k, v, o, do, lse, *p_ins)
    return res[0], res[1], res[2], list(res[3:])


_HBM = pl.BlockSpec(memory_space=pltpu.HBM)


def _place():
    x, y, c = lax.axis_index("x"), lax.axis_index("y"), lax.axis_index("c")
    chips = [(1 - x, y), (x, 1 - y), (1 - x, 1 - y)]
    return x, y, c, chips


def _remote(src, dst, send_sem, recv_sem, dev):
    return pltpu.make_async_remote_copy(src_ref=src, dst_ref=dst, send_sem=send_sem, recv_sem=recv_sem,
                                        device_id=dev, device_id_type=MESH)


class CommPlan:
    def __init__(self, ins, out_shapes, n_sems, start, finish, aliases=None):
        self.ins, self.out_shapes, self.n_sems = list(ins), list(out_shapes), n_sems
        self.start, self.finish, self.aliases = start, finish, dict(aliases or {})

    def scratch(self):
        return [pltpu.SemaphoreType.DMA((self.n_sems,)), pltpu.SemaphoreType.DMA((self.n_sems,))]


def comm_call(name, plan):
    ni, no = len(plan.ins), len(plan.out_shapes)

    def body(*refs):
        ins, outs, (send_sems, recv_sems) = refs[:ni], refs[ni:ni + no], refs[ni + no:]
        plan.start(ins, outs, send_sems, recv_sems)
        plan.finish(ins, outs, send_sems, recv_sems)

    return pl.pallas_call(
        body, name=name, out_shape=tuple(plan.out_shapes), in_specs=[_HBM] * ni, out_specs=tuple([_HBM] * no),
        input_output_aliases=plan.aliases, scratch_shapes=plan.scratch(),
    )(*plan.ins)


def gather_plan(packs, l):
    nk = len(packs)

    def parts(p_refs, o_refs, ss, rs):
        x, y, c, chips = _place()
        s_me = 2 * x + y
        sibling = (x, y, 1 - c)
        first = [_remote(p_refs[k].at[l], o_refs[k].at[s_me], ss.at[j * nk + k], rs.at[j * nk + k], (cx, cy, c))
                 for j, (cx, cy) in enumerate(chips) for k in range(nk)]
        own = [_remote(p_refs[k].at[l], o_refs[k].at[s_me], ss.at[6 * nk + k], rs.at[6 * nk + k], sibling) for k in range(nk)]
        return c, chips, sibling, first, own

    def start(p_refs, o_refs, ss, rs):
        c, _, _, first, own = parts(p_refs, o_refs, ss, rs)
        for cp in own:
            cp.start()

        @pl.when(c == l)
        def _():
            for cp in first:
                cp.start()

    def finish(p_refs, o_refs, ss, rs):
        c, chips, sibling, first, own = parts(p_refs, o_refs, ss, rs)

        @pl.when(c == l)
        def _():
            passed = []
            for j, (cx, cy) in enumerate(chips):
                for k in range(nk):
                    blk = o_refs[k].at[2 * cx + cy]
                    _remote(blk, blk, ss.at[j * nk + k], rs.at[j * nk + k], (cx, cy, c)).wait_recv()
                    fwd = _remote(blk, blk, ss.at[(3 + j) * nk + k], rs.at[(3 + j) * nk + k], sibling)
                    fwd.start()
                    passed.append(fwd)
            for cp in first + passed:
                cp.wait_send()

        @pl.when(c != l)
        def _():
            for j, (cx, cy) in enumerate(chips):
                for k in range(nk):
                    blk = o_refs[k].at[2 * cx + cy]
                    _remote(blk, blk, ss.at[(3 + j) * nk + k], rs.at[(3 + j) * nk + k], sibling).wait_recv()

        for cp in own:
            cp.wait()

    outs = [jax.ShapeDtypeStruct((N_SHARDS,) + p.shape[1:], p.dtype) for p in packs]
    return CommPlan(packs, outs, 7 * nk, start, finish)


def swap_plan(grads, l):
    nk = len(grads)

    def copies(g_refs, o_refs, ss, rs):
        x, y, c, _ = _place()
        return c, [_remote(g_refs[k], o_refs[k], ss.at[k], rs.at[k], (x, y, 1 - c)) for k in range(nk)]

    def start(g_refs, o_refs, ss, rs):
        c, cps = copies(g_refs, o_refs, ss, rs)

        @pl.when(c != l)
        def _():
            for cp in cps:
                cp.start()

    def finish(g_refs, o_refs, ss, rs):
        c, cps = copies(g_refs, o_refs, ss, rs)

        @pl.when(c != l)
        def _():
            for cp in cps:
                cp.wait_send()

        @pl.when(c == l)
        def _():
            for cp in cps:
                cp.wait_recv()

    return CommPlan(grads, [jax.ShapeDtypeStruct(g.shape, g.dtype) for g in grads], nk, start, finish)


def scatter_plan(parts, l):
    nk = len(parts)

    def copies(p_refs, o_refs, ss, rs):
        x, y, c, chips = _place()
        s_me = 2 * x + y
        sends = [_remote(p_refs[k].at[2 * cx + cy], o_refs[k].at[s_me], ss.at[j * nk + k], rs.at[j * nk + k], (cx, cy, c))
                 for j, (cx, cy) in enumerate(chips) for k in range(nk)]
        return c, chips, sends

    def start(p_refs, o_refs, ss, rs):
        c, _, sends = copies(p_refs, o_refs, ss, rs)

        @pl.when(c == l)
        def _():
            for cp in sends:
                cp.start()

    def finish(p_refs, o_refs, ss, rs):
        c, chips, sends = copies(p_refs, o_refs, ss, rs)

        @pl.when(c == l)
        def _():
            for j, (cx, cy) in enumerate(chips):
                for k in range(nk):
                    slot = o_refs[k].at[2 * cx + cy]
                    _remote(slot, slot, ss.at[j * nk + k], rs.at[j * nk + k], (cx, cy, c)).wait_recv()
            for cp in sends:
                cp.wait_send()

    return CommPlan(parts, [jax.ShapeDtypeStruct(p.shape, p.dtype) for p in parts], 3 * nk, start, finish)


def share_plan(finals):
    flat = [(l, f) for l, fs in enumerate(finals) for f in fs]

    def copies(f_refs, o_refs, ss, rs):
        x, y, c, _ = _place()
        return c, [_remote(f_refs[i], o_refs[i], ss.at[i], rs.at[i], (x, y, 1 - c)) for i in range(len(flat))]

    def start(f_refs, o_refs, ss, rs):
        c, cps = copies(f_refs, o_refs, ss, rs)
        for l in range(len(finals)):
            @pl.when(c == l)
            def _(l=l):
                for cp, (lf, _) in zip(cps, flat):
                    if lf == l:
                        cp.start()

    def finish(f_refs, o_refs, ss, rs):
        c, cps = copies(f_refs, o_refs, ss, rs)
        for l in range(len(finals)):
            @pl.when(c == l)
            def _(l=l):
                for cp, (lf, _) in zip(cps, flat):
                    if lf == l:
                        cp.wait_send()

            @pl.when(c != l)
            def _(l=l):
                for cp, (lf, _) in zip(cps, flat):
                    if lf == l:
                        cp.wait_recv()

    arrays = [f for _, f in flat]
    return CommPlan(arrays, [jax.ShapeDtypeStruct(f.shape, f.dtype) for f in arrays], len(flat), start, finish,
                    aliases={i: i for i in range(len(flat))})


def add_pair(name, own, other, tr=1024):
    n, cols = own.shape
    tr = _tile(n, tr, 16)

    def body(g_ref, o_ref, out_ref):
        out_ref[...] = (g_ref[...].astype(F32) + o_ref[...].astype(F32)).astype(out_ref.dtype)

    row = pl.BlockSpec((tr, cols), lambda i: (i, 0))
    return pl.pallas_call(
        body, name=name, out_shape=jax.ShapeDtypeStruct((n, cols), BF16), grid=(n // tr,), in_specs=[row, row], out_specs=row,
        compiler_params=_params(("parallel",)),
    )(own, other)


def sum_slots(name, slots, part, shard, tr=512):
    ns, r, cols = slots.shape
    tr = _tile(r, tr, 16)

    def body(s_idx, s_ref, own_ref, o_ref):
        total = own_ref[...].astype(F32)
        for k in range(1, ns):
            total = total + s_ref[(s_idx[0] + k) % ns].astype(F32)
        o_ref[...] = total

    return pl.pallas_call(
        body, name=name, out_shape=jax.ShapeDtypeStruct((r, cols), F32),
        grid_spec=pltpu.PrefetchScalarGridSpec(
            num_scalar_prefetch=1, grid=(r // tr,),
            in_specs=[pl.BlockSpec((ns, tr, cols), lambda i, p: (0, i, 0)), pl.BlockSpec((None, tr, cols), lambda i, p: (p[0], i, 0))],
            out_specs=pl.BlockSpec((tr, cols), lambda i, p: (i, 0))),
        compiler_params=_params(("parallel",)),
    )(shard, slots, part)


def small_allreduce(vec):
    r, cols = vec.shape

    def body(v_ref, o_ref, buf, send_sems, recv_sems):
        x, y, c, _ = _place()
        me = 4 * x + 2 * y + c
        buf[me] = v_ref[...]
        peers = []
        for k in range(1, N_DEV):
            px = (1 - x) if (k & 4) else x
            py = (1 - y) if (k & 2) else y
            pc = (1 - c) if (k & 1) else c
            peers.append((px, py, pc))
        sends = [_remote(buf.at[me], buf.at[me], send_sems.at[k], recv_sems.at[k], peer) for k, peer in enumerate(peers)]
        for cp in sends:
            cp.start()
        for k, (px, py, pc) in enumerate(peers):
            slot = buf.at[4 * px + 2 * py + pc]
            _remote(slot, slot, send_sems.at[k], recv_sems.at[k], (px, py, pc)).wait_recv()
        for cp in sends:
            cp.wait_send()
        total = buf[0]
        for k in range(1, N_DEV):
            total = total + buf[k]
        o_ref[...] = total

    return pl.pallas_call(
        body, name="small_allreduce", out_shape=jax.ShapeDtypeStruct((r, cols), F32),
        in_specs=[pl.BlockSpec(memory_space=pltpu.VMEM)], out_specs=pl.BlockSpec(memory_space=pltpu.VMEM),
        scratch_shapes=[pltpu.VMEM((N_DEV, r, cols), F32), pltpu.SemaphoreType.DMA((N_DEV - 1,)), pltpu.SemaphoreType.DMA((N_DEV - 1,))],
    )(vec)


ARG_NAMES = (("x", "positions") + WEIGHT_NAMES + ("loss_target",) + tuple("m_" + n for n in WEIGHT_NAMES)
             + tuple("v_" + n for n in WEIGHT_NAMES))

_C_KA = N_HEADS_A * HEAD_PAD
_C_VA = _C_KA + N_KV_HEADS_A * HEAD_PAD
_C_CQ = _C_VA + N_KV_HEADS_A * HEAD_PAD
_C_CKV = _C_CQ + Q_LORA_RANK
_C_KR = _C_CKV + KV_LORA_RANK
_C_GA = _C_KR + HEAD_PAD
_R_KA = WIDTH_A
_R_VA = _R_KA + KV_WIDTH_A
_R_CQ = _R_VA + KV_WIDTH_A
_R_KR = _R_CQ + Q_LORA_RANK + KV_LORA_RANK
_R_GA = _R_KR + QK_ROPE_DIM


def _pad_head_cols(w, n_heads):
    r, dh = w.shape[0], w.shape[1] // n_heads
    return jnp.pad(w.reshape(r, n_heads, dh), ((0, 0), (0, 0), (0, HEAD_PAD - dh))).reshape(r, n_heads * HEAD_PAD)


def _unpad_head_cols(w, n_heads, dh):
    r = w.shape[0]
    return w.reshape(r, n_heads, HEAD_PAD)[:, :, :dh].reshape(r, n_heads * dh)


def _pad_head_rows(w, n_heads):
    dh, c = w.shape[0] // n_heads, w.shape[1]
    return jnp.pad(w.reshape(n_heads, dh, c), ((0, 0), (0, HEAD_PAD - dh), (0, 0))).reshape(n_heads * HEAD_PAD, c)


def _unpad_head_rows(w, n_heads, dh):
    c = w.shape[1]
    return w.reshape(n_heads, HEAD_PAD, c)[:, :dh].reshape(n_heads * dh, c)


def _join_cols(shards):
    ns, r, c = shards.shape
    return shards.transpose(1, 0, 2).reshape(r, ns * c)


def _split_cols(mat):
    r, c4 = mat.shape
    return mat.reshape(r, N_SHARDS, c4 // N_SHARDS).transpose(1, 0, 2)


def _ffn_fwd(tag, x, gain, wcol, wdown, plan=None):
    nb = rms_fwd(tag + "_rms", x, gain)
    gu, h, carried = ffn_up_act(tag + "_up", nb, wcol, 0, plan=plan)
    out = gmm_down(tag + "_down", h, wdown, 0, x, 0.5)
    return out, (x, nb, gu, h), carried


def _ffn_bwd(tag, saved, gain, wcol, wdown, dout, doutb, plans=None):
    plans = plans or {}
    x, nb, gu, h = saved
    carried = {}
    dgu, carried["dh"] = ffn_down_dact(tag + "_dh", doutb, wdown, gu, gu.shape[-1], 0, 0.5, plan=plans.get("dh"))
    gdown = gmm_down_dw(tag + "_dwd", h, doutb, 0.5)
    if plans.get("dwgu") is not None:
        gcol, carried["dwgu"] = gmm_up_dw(tag + "_dwgu", nb, dgu, plan=plans["dwgu"])
    else:
        gcol = gmm_up_dw(tag + "_dwgu", nb, dgu)
    if plans.get("dn") is not None:
        dn, carried["dn"] = gmm_up_dx(tag + "_dn", dgu, wcol, x.shape[1], 0, plan=plans["dn"])
    else:
        dn = gmm_up_dx(tag + "_dn", dgu, wcol, x.shape[1], 0)
    dx, dxb, dgain = rms_bwd(tag + "_drms", x, gain, dn, dres=dout)
    return dx, dxb, dgain, gcol, gdown, carried


def _pad_gain(g):
    return jnp.pad(g, ((0, 0), (0, HEAD_PAD - g.shape[1])))


def _mixer_fwd(tag, x1, sm, w, wo, aux, plans=None):
    plans = plans or {}
    b, s = aux["b"], aux["s"]
    t, d = x1.shape
    hb = rms_fwd(tag + "_rms", x1, sm["mix_norm"])
    carried = {}
    if plans.get("proj") is not None:
        proj, carried["proj"] = mm_nn(tag + "_proj", hb, w["win"], tm=512, tn=2048, tk=1024, plan=plans["proj"])
    else:
        proj = mm_nn(tag + "_proj", hb, w["win"], tm=512, tn=2048, tk=1024)
    proj3 = proj.reshape(b, s, proj.shape[1])
    gains = {n: _pad_gain(sm[n]) for n in ("swa_q_norm", "swa_k_norm", "mla_q_norm", "mla_k_norm")}
    cqn = rms_fwd(tag + "_rms_cq", proj, sm["mla_q_lora_norm"], tm=1024, col_blk=_C_CQ // Q_LORA_RANK)
    ckvn = rms_fwd(tag + "_rms_ckv", proj, sm["mla_kv_lora_norm"], tm=1024, col_blk=_C_CKV // KV_LORA_RANK)
    qb_raw = mm_nn(tag + "_uq", cqn, w["wuq"], tm=1024).reshape(b, s, N_HEADS_B * HEAD_PAD)
    kb_raw = mm_nn(tag + "_uk", ckvn, w["wk"], tm=1024).reshape(b, s, N_HEADS_B * HEAD_PAD)
    vb = mm_nn(tag + "_uv", ckvn, w["wv"], tm=1024, out_dtype=BF16).reshape(b, s, N_HEADS_B * HEAD_PAD)
    qah = qk_prep_fwd(tag + "_qa_norm", proj3, 0, N_HEADS_A, HEAD_DIM_A, gains["swa_q_norm"])
    kah = qk_prep_fwd(tag + "_ka_norm", proj3, _C_KA // HEAD_PAD, N_KV_HEADS_A, HEAD_DIM_A, gains["swa_k_norm"])
    sinks = sm["swa_sinks"].reshape(-1)
    oab, carried["swa"] = swa_fwd(tag + "_swa", qah, kah, proj3, _C_VA // HEAD_PAD, aux["pos_col"], aux["pos_row"], aux["slopes"],
                                  sinks, plan=plans.get("swa"))
    kr = (proj3, _C_KR // HEAD_PAD)
    qbh = qk_prep_fwd(tag + "_qb_norm", qb_raw, 0, N_HEADS_B, QK_DIM_B, gains["mla_q_norm"], aux["rope"])
    kbh = qk_prep_fwd(tag + "_kb_norm", kb_raw, 0, N_HEADS_B, QK_DIM_B, gains["mla_k_norm"], aux["rope"], extra=kr)
    ob, obb, lse, carried["mla"] = mla_fwd(tag + "_mla", qbh, kbh, vb, plans.get("mla"))
    oab2, obb2 = oab.reshape(t, -1), obb.reshape(t, -1)
    ya = mm_nn(tag + "_branch_a", oab2, w["wa"], tm=1024)
    yb = mm_nn(tag + "_branch_b", obb2, w["wb"], tm=1024)
    mg = gate_fwd(tag + "_gate", proj, ya, yb, _C_GA, _C_GA + d)
    x2 = gmm_rows(tag + "_out", mg, wo, x1)
    saved = dict(x1=x1, hb=hb, proj=proj, cqn=cqn, ckvn=ckvn, qb_raw=qb_raw, kb_raw=kb_raw, vb=vb, qah=qah, kah=kah,
                 qbh=qbh, kbh=kbh, ob=ob, lse=lse, oab2=oab2, obb2=obb2, ya=ya, yb=yb, mg=mg, gains=gains)
    return x2, saved, carried


def _mixer_bwd(tag, sv, sm, w, wo, aux, dx2, dx2b, plans=None):
    plans = plans or {}
    carried = {}
    b, s = aux["b"], aux["s"]
    t, d = dx2.shape
    proj, gains = sv["proj"], sv["gains"]
    proj3 = proj.reshape(b, s, proj.shape[1])
    kr = (proj3, _C_KR // HEAD_PAD)
    dmg = gmm_rows_dx(tag + "_d_out", dx2b, wo)
    dwo = gmm_rows_dw(tag + "_dw_out", sv["mg"], dx2b)
    dya, dyb, dga, dgb = gate_bwd(tag + "_dgate", proj, sv["ya"], sv["yb"], dmg, _C_GA, _C_GA + d)
    dwa = mm_tn(tag + "_dw_branch_a", sv["oab2"], dya, out_dtype=BF16)
    dwb = mm_tn(tag + "_dw_branch_b", sv["obb2"], dyb, out_dtype=BF16)
    doa = mm_nt(tag + "_d_branch_a", dya, w["wa"], tm=1024).reshape(b, s, -1)
    dob = mm_nt(tag + "_d_branch_b", dyb, w["wb"], tm=1024).reshape(b, s, -1)
    sinks = sm["swa_sinks"].reshape(-1)
    dqah, dkah, dva, dsinks, carried["dswa"] = swa_bwd(tag + "_dswa", sv["qah"], sv["kah"], proj3, _C_VA // HEAD_PAD, aux["pos_col"],
                                                       aux["pos_row"], aux["slopes"], sinks, doa, plan=plans.get("dswa"))
    dqa_raw, dg_swa_q = qk_prep_bwd(tag + "_dqa_norm", proj3, 0, N_HEADS_A, HEAD_DIM_A, gains["swa_q_norm"], dqah)
    dka_raw, dg_swa_k = qk_prep_bwd(tag + "_dka_norm", proj3, _C_KA // HEAD_PAD, N_KV_HEADS_A, HEAD_DIM_A, gains["swa_k_norm"], dkah)
    dqbh, dkbh, dvb, carried["dmla"] = mla_bwd(tag + "_dmla", sv["qbh"], sv["kbh"], sv["vb"], sv["ob"], dob, sv["lse"], plans.get("dmla"))
    dqb_raw, dg_mla_q = qk_prep_bwd(tag + "_dqb_norm", sv["qb_raw"], 0, N_HEADS_B, QK_DIM_B, gains["mla_q_norm"], dqbh, aux["rope"])
    dkb_raw, dkr_sum, dg_mla_k = qk_prep_bwd(tag + "_dkb_norm", sv["kb_raw"], 0, N_HEADS_B, QK_DIM_B, gains["mla_k_norm"], dkbh,
                                             aux["rope"], extra=kr, head_sum=True)
    dq_tok, dk_tok, dv_tok = dqb_raw.reshape(t, -1), dkb_raw.reshape(t, -1), dvb.reshape(t, -1)
    dwuq = mm_tn(tag + "_dw_uq", sv["cqn"], dq_tok, tk=1024, out_dtype=BF16)
    dwk = mm_tn(tag + "_dw_uk", sv["ckvn"], dk_tok, tk=1024, out_dtype=BF16)
    dwv = mm_tn(tag + "_dw_uv", sv["ckvn"], dv_tok, tk=1024, out_dtype=BF16)
    dcqn = mm_nt(tag + "_d_uq", dq_tok, w["wuq"], tm=1024)
    dckvn = mm_nt(tag + "_d_uv", dv_tok, w["wv"], tm=1024, res=mm_nt(tag + "_d_uk", dk_tok, w["wk"], tm=1024))
    dcq, dg_q_lora = rms_bwd(tag + "_drms_cq", proj, sm["mla_q_lora_norm"], dcqn, want_f32=False, tm=1024, col_blk=_C_CQ // Q_LORA_RANK)
    dckv, dg_kv_lora = rms_bwd(tag + "_drms_ckv", proj, sm["mla_kv_lora_norm"], dckvn, want_f32=False, tm=1024,
                               col_blk=_C_CKV // KV_LORA_RANK)
    dproj = jnp.concatenate([dqa_raw.reshape(t, -1), dka_raw.reshape(t, -1), dva.reshape(t, -1).astype(BF16), dcq, dckv,
                             dkr_sum.reshape(t, HEAD_PAD).astype(BF16), dga, dgb], axis=1)
    dwin = mm_tn(tag + "_dw_in", sv["hb"], dproj, tm=1024, tn=2048, tk=1024, out_dtype=BF16)
    dh = mm_nt(tag + "_d_in", dproj, w["win"], tm=512, tn=1024, tk=2048)
    dx1, dx1b, dg_mix = rms_bwd(tag + "_drms", sv["x1"], sm["mix_norm"], dh, dres=dx2)
    wgrads = dict(win=dwin, wuq=dwuq, wk=dwk, wv=dwv, wa=dwa, wb=dwb, wo=dwo)
    sgrads = dict(mix_norm=dg_mix, swa_q_norm=dg_swa_q[:, :HEAD_DIM_A], swa_k_norm=dg_swa_k[:, :HEAD_DIM_A],
                  swa_sinks=dsinks[:, 0].reshape(1, -1), mla_q_lora_norm=dg_q_lora, mla_kv_lora_norm=dg_kv_lora,
                  mla_q_norm=dg_mla_q[:, :QK_DIM_B], mla_k_norm=dg_mla_k[:, :QK_DIM_B])
    return dx1, dx1b, wgrads, sgrads, carried


def _layer_weights(win4, uq4, ukv4, branch4):
    wr = _join_cols(win4)
    kr = jnp.pad(wr[:, _R_KR:_R_GA], ((0, 0), (QK_NOPE_DIM, HEAD_PAD - QK_DIM_B)))
    win = jnp.concatenate([_pad_head_cols(wr[:, :_R_KA], N_HEADS_A), _pad_head_cols(wr[:, _R_KA:_R_VA], N_KV_HEADS_A),
                           _pad_head_cols(wr[:, _R_VA:_R_CQ], N_KV_HEADS_A), wr[:, _R_CQ:_R_KR], kr, wr[:, _R_GA:]], axis=1)
    ukv = _join_cols(ukv4)
    ukv3 = ukv.reshape(ukv.shape[0], N_HEADS_B, QK_NOPE_DIM + V_DIM_B)
    r = branch4.shape[1] // 2
    return dict(win=win, wuq=_pad_head_cols(_join_cols(uq4), N_HEADS_B),
                wk=_pad_head_cols(ukv3[:, :, :QK_NOPE_DIM].reshape(ukv.shape[0], -1), N_HEADS_B),
                wv=_pad_head_cols(ukv3[:, :, QK_NOPE_DIM:].reshape(ukv.shape[0], -1), N_HEADS_B),
                wa=_pad_head_rows(_join_cols(branch4[:, :r]), N_HEADS_A), wb=_pad_head_rows(_join_cols(branch4[:, r:]), N_HEADS_B))


def _mixer_grad_shards(g):
    dw = g["win"]
    win_ref = jnp.concatenate([_unpad_head_cols(dw[:, :_C_KA], N_HEADS_A, HEAD_DIM_A),
                               _unpad_head_cols(dw[:, _C_KA:_C_VA], N_KV_HEADS_A, HEAD_DIM_A),
                               _unpad_head_cols(dw[:, _C_VA:_C_CQ], N_KV_HEADS_A, HEAD_DIM_A), dw[:, _C_CQ:_C_KR],
                               dw[:, _C_KR + QK_NOPE_DIM:_C_KR + QK_DIM_B], dw[:, _C_GA:]], axis=1)
    rk = g["wk"].shape[0]
    ukv = jnp.concatenate([g["wk"].reshape(rk, N_HEADS_B, HEAD_PAD)[:, :, :QK_NOPE_DIM],
                           g["wv"].reshape(rk, N_HEADS_B, HEAD_PAD)[:, :, :V_DIM_B]], axis=2).reshape(rk, -1)
    return (_split_cols(win_ref), _split_cols(_unpad_head_cols(g["wuq"], N_HEADS_B, QK_DIM_B)), _split_cols(ukv),
            jnp.concatenate([_split_cols(_unpad_head_rows(g["wa"], N_HEADS_A, HEAD_DIM_A)),
                             _split_cols(_unpad_head_rows(g["wb"], N_HEADS_B, V_DIM_B))], axis=1))


BUFFER_TAGS = ("col1", "col2", "down1", "down2", "wo", "win", "uq", "ukv", "branch")
FIRST_GATHER = ("col1", "down1")
GATHER_CARRIERS = {(0, "ffn1_up"): (0, ("wo", "win", "uq", "ukv", "branch")), (0, "proj"): (1, ("col1",)),
                   (0, "swa"): (1, ("down1", "wo", "uq", "ukv")), (0, "mla"): (0, ("col2", "down2")),
                   (0, "ffn2_up"): (1, ("win", "branch")), (1, "ffn1_up"): (1, ("col2",)), (1, "swa"): (1, ("down2",))}
SCATTER_CARRIERS = {(0, "ffn2_dh"): (1, ("col1",)), (0, "ffn2_dwgu"): (1, ("col2",)), (0, "ffn2_dn"): (1, ("down1", "down2", "wo")),
                    (0, "dswa"): (1, ("win", "uq", "ukv", "branch")), (0, "dmla"): (0, ("col2", "down2")),
                    (0, "ffn1_dh"): (0, ("wo", "win", "uq", "ukv", "branch"))}
LAST_SCATTER = ("col1", "down1")


def _pick(by_tag, tags):
    return [by_tag[t] for t in tags]


def _train_step(x, positions, target, packs, small, shard):
    depth = 2
    b, s, d = x.shape
    t = b * s
    posf = positions.astype(F32)
    half = QK_ROPE_DIM // 2
    inv_freq = ROPE_BASE ** (-jnp.arange(half, dtype=F32) / half)
    ang = posf[..., None] * inv_freq
    cos, sin = jnp.cos(ang), jnp.sin(ang)
    tail = HEAD_PAD - QK_DIM_B
    rope = (jnp.concatenate([jnp.ones((b, s, QK_NOPE_DIM), F32), cos, cos, jnp.ones((b, s, tail), F32)], axis=-1),
            jnp.concatenate([jnp.zeros((b, s, QK_NOPE_DIM), F32), sin, sin, jnp.zeros((b, s, tail), F32)], axis=-1))
    slopes = jnp.exp2(-8.0 * (jnp.arange(N_HEADS_A, dtype=F32) + 1.0) / N_HEADS_A)
    aux = dict(b=b, s=s, pos_col=posf.reshape(b, s, 1), pos_row=posf.reshape(b, 1, s), rope=rope, slopes=slopes)

    def sm_of(l):
        return {n: small[n][l:l + 1] for n in SMALL_NAMES}

    by_tag = dict(zip(BUFFER_TAGS, packs))
    wts, parts, slots = [{} for _ in range(depth)], [{} for _ in range(depth)], [{} for _ in range(depth)]
    saved, sgrads = [None] * depth, [None] * depth

    def gather_piece(l, kern):
        if (l, kern) not in GATHER_CARRIERS:
            return None
        data_l, tags = GATHER_CARRIERS[(l, kern)]
        return gather_plan(_pick(by_tag, tags), data_l)

    def gathered(l, kern, arrays):
        if (l, kern) in GATHER_CARRIERS:
            data_l, tags = GATHER_CARRIERS[(l, kern)]
            wts[data_l].update(zip(tags, arrays))

    def scatter_piece(l, kern):
        if (l, kern) not in SCATTER_CARRIERS:
            return None
        data_l, tags = SCATTER_CARRIERS[(l, kern)]
        return scatter_plan(_pick(parts[data_l], tags), data_l)

    def scattered(l, kern, arrays):
        if (l, kern) in SCATTER_CARRIERS:
            data_l, tags = SCATTER_CARRIERS[(l, kern)]
            slots[data_l].update(zip(tags, arrays))

    def chip_sums(l, tags, mine):
        theirs = comm_call(f"swap_l{l}_{tags[0]}", swap_plan(_pick(mine, tags), l))
        for tag, o in zip(tags, theirs):
            g = mine[tag]
            parts[l][tag] = add_pair(f"l{l}_add_{tag}", g.reshape(-1, g.shape[-1]), o.reshape(-1, o.shape[-1])).reshape(g.shape)

    wts[0].update(zip(FIRST_GATHER, comm_call("gather_first", gather_plan(_pick(by_tag, FIRST_GATHER), 0))))
    h = x.reshape(t, d)
    for l in range(depth):
        sm, wl = sm_of(l), wts[l]
        h, s1, got = _ffn_fwd(f"l{l}_ffn1", h, sm["ffn1_norm"], wl["col1"], wl["down1"], gather_piece(l, "ffn1_up"))
        gathered(l, "ffn1_up", got)
        weights = _layer_weights(wl["win"], wl["uq"], wl["ukv"], wl["branch"])
        h, s2, got = _mixer_fwd(f"l{l}_mix", h, sm, weights, wl["wo"], aux, {k: gather_piece(l, k) for k in ("proj", "swa", "mla")})
        for kern, arrays in got.items():
            gathered(l, kern, arrays)
        h, s3, got = _ffn_fwd(f"l{l}_ffn2", h, sm["ffn2_norm"], wl["col2"], wl["down2"], gather_piece(l, "ffn2_up"))
        gathered(l, "ffn2_up", got)
        saved[l] = (s1, s2, s3, weights)
    loss, dh, dhb = loss_fwd_bwd("loss", h, target.reshape(t, d))

    for l in reversed(range(depth)):
        sm, wl = sm_of(l), wts[l]
        s1, s2, s3, weights = saved[l]
        staged = l == 0
        dh, dhb, dg_f2, gcol2, gdown2, got = _ffn_bwd(f"l{l}_ffn2", s3, sm["ffn2_norm"], wl["col2"], wl["down2"], dh, dhb,
                                                      {k: scatter_piece(l, "ffn2_" + k) for k in ("dh", "dwgu", "dn")})
        for kern, arrays in got.items():
            scattered(l, "ffn2_" + kern, arrays)
        mine = dict(col2=gcol2, down2=gdown2)
        if staged:
            chip_sums(l, ("col2", "down2"), mine)
        dh, dhb, wg, sg, got = _mixer_bwd(f"l{l}_mix", s2, sm, weights, wl["wo"], aux, dh, dhb,
                                          {k: scatter_piece(l, k) for k in ("dswa", "dmla")})
        for kern, arrays in got.items():
            scattered(l, kern, arrays)
        gwin, guq, gukv, gbranch = _mixer_grad_shards(wg)
        mine.update(wo=wg["wo"], win=gwin, uq=guq, ukv=gukv, branch=gbranch)
        if staged:
            chip_sums(l, ("wo", "win", "uq", "ukv", "branch"), mine)
        dh, dhb, dg_f1, gcol1, gdown1, got = _ffn_bwd(f"l{l}_ffn1", s1, sm["ffn1_norm"], wl["col1"], wl["down1"], dh, dhb,
                                                      {k: scatter_piece(l, "ffn1_" + k) for k in ("dh", "dwgu", "dn")})
        for kern, arrays in got.items():
            scattered(l, "ffn1_" + kern, arrays)
        mine.update(col1=gcol1, down1=gdown1)
        chip_sums(l, LAST_SCATTER if staged else BUFFER_TAGS, mine)
        sg.update(ffn1_norm=dg_f1, ffn2_norm=dg_f2)
        sgrads[l] = sg
    slots[0].update(zip(LAST_SCATTER, comm_call("scatter_last", scatter_plan(_pick(parts[0], LAST_SCATTER), 0))))
    finals = [[sum_slots(f"l{l}_sum_{tag}", slots[l][tag], parts[l][tag], shard) for tag in BUFFER_TAGS] for l in range(depth)]
    shared = comm_call("share", share_plan(finals))
    nk = len(BUFFER_TAGS)
    return loss, dh.reshape(b, s, d), [dict(zip(BUFFER_TAGS, shared[l * nk:(l + 1) * nk])) for l in range(depth)], sgrads


def kernel(x, positions, ffn1_norm, ffn1_w_gate, ffn1_w_up, ffn1_w_down, mix_norm, w_in, swa_q_norm, swa_k_norm, swa_sinks, mla_q_lora_norm, mla_w_uq, mla_kv_lora_norm, mla_w_ukv, mla_q_norm, mla_k_norm, w_branch_a, w_branch_b, w_out, ffn2_norm, ffn2_w_gate, ffn2_w_up, ffn2_w_down, loss_target, m_ffn1_norm, m_ffn1_w_gate, m_ffn1_w_up, m_ffn1_w_down, m_mix_norm, m_w_in, m_swa_q_norm, m_swa_k_norm, m_swa_sinks, m_mla_q_lora_norm, m_mla_w_uq, m_mla_kv_lora_norm, m_mla_w_ukv, m_mla_q_norm, m_mla_k_norm, m_w_branch_a, m_w_branch_b, m_w_out, m_ffn2_norm, m_ffn2_w_gate, m_ffn2_w_up, m_ffn2_w_down, v_ffn1_norm, v_ffn1_w_gate, v_ffn1_w_up, v_ffn1_w_down, v_mix_norm, v_w_in, v_swa_q_norm, v_swa_k_norm, v_swa_sinks, v_mla_q_lora_norm, v_mla_w_uq, v_mla_kv_lora_norm, v_mla_w_ukv, v_mla_q_norm, v_mla_k_norm, v_w_branch_a, v_w_branch_b, v_w_out, v_ffn2_norm, v_ffn2_w_gate, v_ffn2_w_up, v_ffn2_w_down):
    args = (x, positions, ffn1_norm, ffn1_w_gate, ffn1_w_up, ffn1_w_down, mix_norm, w_in, swa_q_norm, swa_k_norm, swa_sinks, mla_q_lora_norm, mla_w_uq, mla_kv_lora_norm, mla_w_ukv, mla_q_norm, mla_k_norm, w_branch_a, w_branch_b, w_out, ffn2_norm, ffn2_w_gate, ffn2_w_up, ffn2_w_down, loss_target, m_ffn1_norm, m_ffn1_w_gate, m_ffn1_w_up, m_ffn1_w_down, m_mix_norm, m_w_in, m_swa_q_norm, m_swa_k_norm, m_swa_sinks, m_mla_q_lora_norm, m_mla_w_uq, m_mla_kv_lora_norm, m_mla_w_ukv, m_mla_q_norm, m_mla_k_norm, m_w_branch_a, m_w_branch_b, m_w_out, m_ffn2_norm, m_ffn2_w_gate, m_ffn2_w_up, m_ffn2_w_down, v_ffn1_norm, v_ffn1_w_gate, v_ffn1_w_up, v_ffn1_w_down, v_mix_norm, v_w_in, v_swa_q_norm, v_swa_k_norm, v_swa_sinks, v_mla_q_lora_norm, v_mla_w_uq, v_mla_kv_lora_norm, v_mla_w_ukv, v_mla_q_norm, v_mla_k_norm, v_w_branch_a, v_w_branch_b, v_w_out, v_ffn2_norm, v_ffn2_w_gate, v_ffn2_w_up, v_ffn2_w_down)
    a = dict(zip(ARG_NAMES, args, strict=True))
    x = a["x"]
    depth = a["ffn1_norm"].shape[0]
    d = x.shape[-1]
    assert depth == 2, "one layer slab per core of a chip"

    def bf16_rows(*names):
        return jnp.concatenate([a[n] for n in names], axis=1).astype(BF16) if len(names) > 1 else a[names[0]].astype(BF16)

    packs = [bf16_rows("ffn1_w_gate", "ffn1_w_up"), bf16_rows("ffn2_w_gate", "ffn2_w_up"), bf16_rows("ffn1_w_down"),
             bf16_rows("ffn2_w_down"), bf16_rows("w_out"), bf16_rows("w_in"), bf16_rows("mla_w_uq"), bf16_rows("mla_w_ukv"),
             bf16_rows("w_branch_a", "w_branch_b")]
    shard_me = (2 * lax.axis_index("x") + lax.axis_index("y")).astype(jnp.int32).reshape(1)
    small = {n: a[n] for n in SMALL_NAMES}

    loss, grad_x, summed, sgrads = _train_step(x, a["positions"], a["loss_target"], packs, small, shard_me)

    full = {tag: jnp.stack([summed[l][tag] for l in range(depth)]) for tag in BUFFER_TAGS}
    r_branch = a["w_branch_a"].shape[1]
    grads = dict(ffn1_w_gate=full["col1"][:, :d], ffn1_w_up=full["col1"][:, d:], ffn2_w_gate=full["col2"][:, :d],
                 ffn2_w_up=full["col2"][:, d:], ffn1_w_down=full["down1"], ffn2_w_down=full["down2"], w_out=full["wo"],
                 w_in=full["win"], mla_w_uq=full["uq"], mla_w_ukv=full["ukv"], w_branch_a=full["branch"][:, :r_branch],
                 w_branch_b=full["branch"][:, r_branch:])

    flat = jnp.concatenate([jnp.concatenate([sgrads[l][n].reshape(-1) for l in range(depth)]) for n in SMALL_NAMES] + [loss.reshape(-1)])
    n_small = flat.shape[0]
    rows = -(-n_small // (8 * LANES)) * 8
    pad = rows * LANES - n_small

    def small_pack(v):
        return jnp.pad(v, (0, pad)).reshape(rows, LANES)

    total = small_allreduce(small_pack(flat))
    w_s, m_s, v_s = (small_pack(jnp.concatenate([a[p + n].reshape(-1) for n in SMALL_NAMES] + [jnp.zeros((1,), F32)]))
                     for p in ("", "m_", "v_"))
    d_s, nm_s, nv_s = adamw("adamw_small", w_s, total, m_s, v_s)

    def small_unpack(buf):
        out, off, flat_b = {}, 0, buf.reshape(-1)
        for n in SMALL_NAMES:
            size = a[n].shape[0] * a[n].shape[1]
            out[n] = flat_b[off:off + size].reshape(a[n].shape)
            off += size
        return out

    grads.update(small_unpack(total))
    delta, new_m, new_v = small_unpack(d_s), small_unpack(nm_s), small_unpack(nv_s)
    for n in PACK_NAMES:
        flip = a[n].shape[2] % LANES != 0

        def view(z):
            return z.transpose(0, 2, 1) if flip else z

        w_v, g_v, m_v, v_v = view(a[n]), view(grads[n]), view(a["m_" + n]), view(a["v_" + n])
        shp = w_v.shape
        two_d = (shp[0] * shp[1], shp[2])
        dn, mn, vn = adamw("adamw_" + n, w_v.reshape(two_d), g_v.reshape(two_d), m_v.reshape(two_d), v_v.reshape(two_d))
        delta[n], new_m[n], new_v[n] = view(dn.reshape(shp)), view(mn.reshape(shp)), view(vn.reshape(shp))
        if flip:
            grads[n] = view(g_v)

    loss_out = total.reshape(-1)[n_small - 1]
    return (loss_out, grad_x, *[grads[n] for n in WEIGHT_NAMES], *[delta[n] for n in WEIGHT_NAMES],
            *[new_m[n] for n in WEIGHT_NAMES], *[new_v[n] for n in WEIGHT_NAMES])
```

```python
import functools

import numpy as np
import jax
import jax.numpy as jnp
from jax import lax
from jax.experimental import pallas as pl
from jax.experimental.pallas import tpu as pltpu

F32 = jnp.float32
BF16 = jnp.bfloat16
MESH = pl.DeviceIdType.MESH

HEAD_DIM_A = 64
N_HEADS_A = 8
N_KV_HEADS_A = 2
GROUP_A = N_HEADS_A // N_KV_HEADS_A
BLOCK = 128
N_HEADS_B = 8
Q_LORA_RANK = 256
KV_LORA_RANK = 128
QK_NOPE_DIM = 64
QK_ROPE_DIM = 32
QK_DIM_B = QK_NOPE_DIM + QK_ROPE_DIM
V_DIM_B = 64
ROPE_BASE = 10000.0
WIDTH_A = N_HEADS_A * HEAD_DIM_A
KV_WIDTH_A = N_KV_HEADS_A * HEAD_DIM_A
EPS = 1e-6
NEG = -1e30
ADAM_LR = 0.001
ADAM_B1 = 0.9
ADAM_B2 = 0.999
ADAM_EPS = 1e-08
ADAM_WD = 0.01
ADAM_STEP = 10

N_SHARDS = 4
N_DEV = 8
LANES = 128
VMEM_LIMIT = 48 * 1024 * 1024

PACK_NAMES = ("ffn1_w_gate", "ffn1_w_up", "ffn1_w_down", "w_in", "mla_w_uq", "mla_w_ukv",
              "w_branch_a", "w_branch_b", "w_out", "ffn2_w_gate", "ffn2_w_up", "ffn2_w_down")
SMALL_NAMES = ("ffn1_norm", "mix_norm", "swa_q_norm", "swa_k_norm", "swa_sinks", "mla_q_lora_norm",
               "mla_kv_lora_norm", "mla_q_norm", "mla_k_norm", "ffn2_norm")
WEIGHT_NAMES = ("ffn1_norm", "ffn1_w_gate", "ffn1_w_up", "ffn1_w_down", "mix_norm", "w_in", "swa_q_norm",
                "swa_k_norm", "swa_sinks", "mla_q_lora_norm", "mla_w_uq", "mla_kv_lora_norm", "mla_w_ukv",
                "mla_q_norm", "mla_k_norm", "w_branch_a", "w_branch_b", "w_out", "ffn2_norm", "ffn2_w_gate",
                "ffn2_w_up", "ffn2_w_down")


def _params(sem):
    return pltpu.CompilerParams(dimension_semantics=sem, vmem_limit_bytes=VMEM_LIMIT)


def _tile(n, want, align):
    if n <= want:
        return n
    t = (want // align) * align
    while t > align and n % t:
        t -= align
    assert t >= align and n % t == 0, (n, want, align)
    return t


def _mm_call(name, a, b, out_struct, grid, a_spec, b_spec, o_spec, dims, n_red, acc_shape, alpha=1.0, res=None,
             inner=0, a_cols=0, plan=None):
    if n_red and all(g == 1 for g in grid[len(grid) - n_red:]):
        n_red = 0
    n_par = len(grid) - n_red
    n_in = 2 + int(res is not None)
    p_ins, p_in_specs, p_outs, p_out_specs, p_scratch, semantics, at_start, at_end = _hosted(
        plan, n_in, 1, grid, ("parallel",) * n_par + ("arbitrary",) * n_red)

    def body(*refs):
        at_start(refs)
        compute(refs)
        at_end(refs)

    def compute(refs):
        a_ref, b_ref = refs[:2]
        r_ref = refs[2] if res is not None else None
        o_ref, acc_ref = refs[n_in + len(p_ins)], refs[n_in + len(p_ins) + 1 + len(p_outs)]
        if inner:
            def a_of(s):
                return a_ref[:, s * a_cols:(s + 1) * a_cols] if a_cols else a_ref[s]

            part = lax.dot_general(a_of(0), b_ref[0], dims, preferred_element_type=F32)
            for s in range(1, inner):
                part = part + lax.dot_general(a_of(s), b_ref[s], dims, preferred_element_type=F32)
        else:
            part = lax.dot_general(a_ref[...], b_ref[...], dims, preferred_element_type=F32)

        def finish(total):
            if alpha != 1.0:
                total = total * alpha
            if r_ref is not None:
                total = r_ref[...] + total
            o_ref[...] = total.astype(o_ref.dtype)

        if n_red == 0:
            finish(part)
            return
        ids = [pl.program_id(n_par + i) for i in range(n_red)]
        first = functools.reduce(jnp.logical_and, [i == 0 for i in ids])
        last = functools.reduce(jnp.logical_and, [i == grid[n_par + k] - 1 for k, i in enumerate(ids)])

        @pl.when(first)
        def _():
            acc_ref[...] = part

        @pl.when(jnp.logical_not(first))
        def _():
            acc_ref[...] += part

        @pl.when(last)
        def _():
            finish(acc_ref[...])

    in_specs = [a_spec, b_spec] + ([o_spec] if res is not None else [])
    args = (a, b) + ((res,) if res is not None else ())
    if plan is None:
        return pl.pallas_call(
            body, name=name, out_shape=out_struct, grid=grid, in_specs=in_specs, out_specs=o_spec,
            scratch_shapes=[pltpu.VMEM(acc_shape, F32)], compiler_params=_params(semantics),
        )(*args)
    res_all = pl.pallas_call(
        body, name=name, out_shape=(out_struct,) + tuple(p_outs), grid=grid, in_specs=in_specs + p_in_specs,
        out_specs=(o_spec,) + tuple(p_out_specs), scratch_shapes=[pltpu.VMEM(acc_shape, F32)] + p_scratch,
        compiler_params=_params(semantics),
    )(*args, *p_ins)
    return res_all[0], list(res_all[1:])


_NN = (((1,), (0,)), ((), ()))
_NT = (((1,), (1,)), ((), ()))
_TN = (((0,), (0,)), ((), ()))


def mm_nn(name, a, b, tm=512, tn=1024, tk=1024, out_dtype=F32, alpha=1.0, res=None, plan=None):
    (m, k), (_, n) = a.shape, b.shape
    tm, tn, tk = _tile(m, tm, 16), _tile(n, tn, LANES), _tile(k, tk, LANES)
    return _mm_call(name, a, b, jax.ShapeDtypeStruct((m, n), out_dtype), (m // tm, n // tn, k // tk),
                    pl.BlockSpec((tm, tk), lambda i, j, kk: (i, kk)), pl.BlockSpec((tk, tn), lambda i, j, kk: (kk, j)),
                    pl.BlockSpec((tm, tn), lambda i, j, kk: (i, j)), _NN, 1, (tm, tn), alpha, res, plan=plan)


def mm_nt(name, a, b, tm=512, tn=1024, tk=1024, out_dtype=F32, alpha=1.0, res=None):
    (m, n), (k, _) = a.shape, b.shape
    tm, tn, tk = _tile(m, tm, 16), _tile(k, tn, LANES), _tile(n, tk, LANES)
    return _mm_call(name, a, b, jax.ShapeDtypeStruct((m, k), out_dtype), (m // tm, k // tn, n // tk),
                    pl.BlockSpec((tm, tk), lambda i, j, kk: (i, kk)), pl.BlockSpec((tn, tk), lambda i, j, kk: (j, kk)),
                    pl.BlockSpec((tm, tn), lambda i, j, kk: (i, j)), _NT, 1, (tm, tn), alpha, res)


def mm_tn(name, a, b, tm=1024, tn=1024, tk=1024, out_dtype=F32, alpha=1.0):
    (m, k), (_, n) = a.shape, b.shape
    tm, tn, tk = _tile(k, tm, LANES), _tile(n, tn, LANES), _tile(m, tk, 16)
    return _mm_call(name, a, b, jax.ShapeDtypeStruct((k, n), out_dtype), (k // tm, n // tn, m // tk),
                    pl.BlockSpec((tk, tm), lambda i, j, kk: (kk, i)), pl.BlockSpec((tk, tn), lambda i, j, kk: (kk, j)),
                    pl.BlockSpec((tm, tn), lambda i, j, kk: (i, j)), _TN, 1, (tm, tn), alpha)


def ffn_up_act(name, a, w, blk, tm=1024, plan=None):
    (m, k), (ns, _, n) = a.shape, w.shape
    tm = _tile(m, tm, 16)
    grid = (ns, m // tm)
    p_ins, p_in_specs, p_outs, p_out_specs, p_scratch, semantics, at_start, at_end = _hosted(plan, 3, 2, grid, ("parallel", "parallel"))

    def body(*refs):
        a_ref, wg_ref, wu_ref = refs[:3]
        gu_ref, h_ref = refs[3 + len(p_ins):5 + len(p_ins)]
        at_start(refs)
        av = a_ref[...]
        gate = jnp.dot(av, wg_ref[...], preferred_element_type=F32)
        up = jnp.dot(av, wu_ref[...], preferred_element_type=F32)
        gu_ref[0] = gate.astype(BF16)
        gu_ref[1] = up.astype(BF16)
        h_ref[...] = (gate * jax.nn.sigmoid(gate) * up).astype(BF16)
        at_end(refs)

    res = pl.pallas_call(
        body, name=name, grid=grid,
        out_shape=(jax.ShapeDtypeStruct((2, ns, m, n), BF16), jax.ShapeDtypeStruct((ns, m, n), BF16)) + tuple(p_outs),
        in_specs=[pl.BlockSpec((tm, k), lambda s, i: (i, 0)), pl.BlockSpec((None, k, n), lambda s, i: (s, blk, 0)),
                  pl.BlockSpec((None, k, n), lambda s, i: (s, blk + 1, 0))] + p_in_specs,
        out_specs=(pl.BlockSpec((2, None, tm, n), lambda s, i: (0, s, i, 0)), pl.BlockSpec((None, tm, n), lambda s, i: (s, i, 0)))
        + tuple(p_out_specs),
        scratch_shapes=p_scratch, compiler_params=_params(semantics),
    )(a, w, w, *p_ins)
    return res[0], res[1], list(res[2:])


def ffn_down_dact(name, a, w, gu, n, blk, alpha, tm=1024, plan=None):
    (m, d), ns = a.shape, w.shape[0]
    tm = _tile(m, tm, 16)
    grid = (ns, m // tm)
    p_ins, p_in_specs, p_outs, p_out_specs, p_scratch, semantics, at_start, at_end = _hosted(plan, 3, 1, grid, ("parallel", "parallel"))

    def body(*refs):
        a_ref, w_ref, gu_ref = refs[:3]
        o_ref = refs[3 + len(p_ins)]
        at_start(refs)
        dh = lax.dot_general(a_ref[...], w_ref[...], _NT, preferred_element_type=F32) * alpha
        gate, up = gu_ref[0].astype(F32), gu_ref[1].astype(F32)
        s = jax.nn.sigmoid(gate)
        o_ref[0] = (dh * up * (s * (1.0 + gate * (1.0 - s)))).astype(BF16)
        o_ref[1] = (dh * (gate * s)).astype(BF16)
        at_end(refs)

    gu_spec = pl.BlockSpec((2, None, tm, n), lambda s, i: (0, s, i, 0))
    res = pl.pallas_call(
        body, name=name, grid=grid, out_shape=(jax.ShapeDtypeStruct((2, ns, m, n), BF16),) + tuple(p_outs),
        in_specs=[pl.BlockSpec((tm, d), lambda s, i: (i, 0)), pl.BlockSpec((None, n, d), lambda s, i: (s, blk, 0)), gu_spec] + p_in_specs,
        out_specs=(gu_spec,) + tuple(p_out_specs), scratch_shapes=p_scratch, compiler_params=_params(semantics),
    )(a, w, gu, *p_ins)
    return res[0], list(res[1:])


def gmm_up_dw(name, a, dgu, tk=2048, plan=None):
    (m, k), (_, ns, _, n) = a.shape, dgu.shape
    tk = _tile(m, tk, 16)
    return _mm_call(name, a, dgu, jax.ShapeDtypeStruct((ns, 2 * k, n), BF16), (2, ns, m // tk),
                    pl.BlockSpec((tk, k), lambda j, s, kk: (kk, 0)), pl.BlockSpec((None, None, tk, n), lambda j, s, kk: (j, s, kk, 0)),
                    pl.BlockSpec((None, k, n), lambda j, s, kk: (s, j, 0)), _TN, 1, (k, n), plan=plan)


def gmm_up_dx(name, dgu, w, k, blk, tm=512, plan=None):
    _, ns, m, n = dgu.shape
    tm = _tile(m, tm, 16)
    return _mm_call(name, dgu, w, jax.ShapeDtypeStruct((m, k), F32), (m // tm, 2),
                    pl.BlockSpec((None, ns, tm, n), lambda i, j: (j, 0, i, 0)),
                    pl.BlockSpec((ns, k, n), lambda i, j: (0, blk + j, 0)),
                    pl.BlockSpec((tm, k), lambda i, j: (i, 0)), _NT, 1, (tm, k), inner=ns, plan=plan)


def gmm_down(name, h, w, blk, res, alpha, tm=512):
    (ns, m, n), d = h.shape, w.shape[2]
    tm = _tile(m, tm, 16)
    return _mm_call(name, h, w, jax.ShapeDtypeStruct((m, d), F32), (m // tm,),
                    pl.BlockSpec((ns, tm, n), lambda i: (0, i, 0)), pl.BlockSpec((ns, n, d), lambda i: (0, blk, 0)),
                    pl.BlockSpec((tm, d), lambda i: (i, 0)), _NN, 0, (8, LANES), alpha, res, inner=ns)


def gmm_down_dw(name, h, b, alpha, tk=2048):
    (ns, m, n), d = h.shape, b.shape[1]
    tk = _tile(m, tk, 16)
    return _mm_call(name, h, b, jax.ShapeDtypeStruct((ns, n, d), BF16), (ns, m // tk),
                    pl.BlockSpec((None, tk, n), lambda s, kk: (s, kk, 0)), pl.BlockSpec((tk, d), lambda s, kk: (kk, 0)),
                    pl.BlockSpec((None, n, d), lambda s, kk: (s, 0, 0)), _TN, 1, (n, d), alpha)


def gmm_rows(name, a, w, res, tm=512):
    (m, _), (ns, r, d) = a.shape, w.shape
    tm = _tile(m, tm, 16)
    return _mm_call(name, a, w, jax.ShapeDtypeStruct((m, d), F32), (m // tm,),
                    pl.BlockSpec((tm, ns * r), lambda i: (i, 0)), pl.BlockSpec((ns, r, d), lambda i: (0, 0, 0)),
                    pl.BlockSpec((tm, d), lambda i: (i, 0)), _NN, 0, (8, LANES), 1.0, res, inner=ns, a_cols=r)


def gmm_rows_dx(name, a, w, tm=512):
    (m, d), (ns, r, _) = a.shape, w.shape
    tm = _tile(m, tm, 16)
    return _mm_call(name, a, w, jax.ShapeDtypeStruct((m, ns * r), F32), (ns, m // tm),
                    pl.BlockSpec((tm, d), lambda s, i: (i, 0)), pl.BlockSpec((None, r, d), lambda s, i: (s, 0, 0)),
                    pl.BlockSpec((tm, r), lambda s, i: (i, s)), _NT, 0, (8, LANES))


def gmm_rows_dw(name, a, b, tk=2048):
    (m, da), d = a.shape, b.shape[1]
    ns, r = N_SHARDS, da // N_SHARDS
    tk = _tile(m, tk, 16)
    return _mm_call(name, a, b, jax.ShapeDtypeStruct((ns, r, d), BF16), (ns, m // tk),
                    pl.BlockSpec((tk, r), lambda s, kk: (kk, s)), pl.BlockSpec((tk, d), lambda s, kk: (kk, 0)),
                    pl.BlockSpec((None, r, d), lambda s, kk: (s, 0, 0)), _TN, 1, (r, d))


def rms_fwd(name, x, gain, tm=512, col_blk=0):
    m, d = x.shape[0], gain.shape[1]
    tm = _tile(m, tm, 16)

    def body(x_ref, g_ref, o_ref):
        xv = x_ref[...]
        r = lax.rsqrt(jnp.mean(xv * xv, axis=-1, keepdims=True) + EPS)
        o_ref[...] = (xv * r * g_ref[...]).astype(o_ref.dtype)

    return pl.pallas_call(
        body, name=name, out_shape=jax.ShapeDtypeStruct((m, d), BF16), grid=(m // tm,),
        in_specs=[pl.BlockSpec((tm, d), lambda i: (i, col_blk)), pl.BlockSpec((1, d), lambda i: (0, 0))],
        out_specs=pl.BlockSpec((tm, d), lambda i: (i, 0)), compiler_params=_params(("parallel",)),
    )(x, gain)


def rms_bwd(name, x, gain, dn, dres=None, want_f32=True, want_bf16=True, tm=512, col_blk=0):
    m, d = x.shape[0], gain.shape[1]
    tm = _tile(m, tm, 16)
    n_out = int(want_f32) + int(want_bf16)

    def body(*refs):
        x_ref, g_ref, dn_ref = refs[:3]
        pos = 3
        r_ref = None
        if dres is not None:
            r_ref = refs[pos]
            pos += 1
        outs = refs[pos:pos + n_out]
        dg_ref = refs[pos + n_out]
        xv = x_ref[...]
        r = lax.rsqrt(jnp.mean(xv * xv, axis=-1, keepdims=True) + EPS)
        xhat = xv * r
        dnv = dn_ref[...]
        dxhat = dnv * g_ref[...]
        dx = r * (dxhat - xhat * jnp.mean(dxhat * xhat, axis=-1, keepdims=True))
        if r_ref is not None:
            dx = r_ref[...] + dx
        for o in outs:
            o[...] = dx.astype(o.dtype)
        part = jnp.sum(dnv * xhat, axis=0, keepdims=True)

        @pl.when(pl.program_id(0) == 0)
        def _():
            dg_ref[...] = part

        @pl.when(pl.program_id(0) > 0)
        def _():
            dg_ref[...] += part

    row = pl.BlockSpec((tm, d), lambda i: (i, 0))
    vec = pl.BlockSpec((1, d), lambda i: (0, 0))
    out_shape = ([jax.ShapeDtypeStruct((m, d), F32)] if want_f32 else []) + ([jax.ShapeDtypeStruct((m, d), BF16)] if want_bf16 else [])
    res = pl.pallas_call(
        body, name=name, out_shape=tuple(out_shape) + (jax.ShapeDtypeStruct((1, d), F32),), grid=(m // tm,),
        in_specs=[pl.BlockSpec((tm, d), lambda i: (i, col_blk)), vec, row] + ([row] if dres is not None else []),
        out_specs=tuple([row] * n_out) + (vec,), compiler_params=_params(("arbitrary",)),
    )(*((x, gain, dn) + ((dres,) if dres is not None else ())))
    return res


def gate_fwd(name, proj, ya, yb, off_a, off_b, tm=256):
    m, d = ya.shape
    tm = _tile(m, tm, 16)
    blk_a, blk_b = off_a // d, off_b // d
    assert blk_a * d == off_a and blk_b * d == off_b, "the gates start on multiples of their width"

    def body(ga_ref, gb_ref, ya_ref, yb_ref, o_ref):
        o_ref[...] = (jax.nn.sigmoid(ga_ref[...]) * ya_ref[...] + jax.nn.sigmoid(gb_ref[...]) * yb_ref[...]).astype(o_ref.dtype)

    row = pl.BlockSpec((tm, d), lambda i: (i, 0))
    return pl.pallas_call(
        body, name=name, out_shape=jax.ShapeDtypeStruct((m, d), BF16), grid=(m // tm,),
        in_specs=[pl.BlockSpec((tm, d), lambda i: (i, blk_a)), pl.BlockSpec((tm, d), lambda i: (i, blk_b)), row, row], out_specs=row,
        compiler_params=_params(("parallel",)),
    )(proj, proj, ya, yb)


def gate_bwd(name, proj, ya, yb, dmerged, off_a, off_b, tm=256):
    m, d = ya.shape
    tm = _tile(m, tm, 16)
    blk_a, blk_b = off_a // d, off_b // d
    assert blk_a * d == off_a and blk_b * d == off_b, "the gates start on multiples of their width"

    def body(ga_ref, gb_ref, ya_ref, yb_ref, dm_ref, dya_ref, dyb_ref, dga_ref, dgb_ref):
        sa, sb = jax.nn.sigmoid(ga_ref[...]), jax.nn.sigmoid(gb_ref[...])
        dm = dm_ref[...]
        dya_ref[...] = (dm * sa).astype(BF16)
        dyb_ref[...] = (dm * sb).astype(BF16)
        dga_ref[...] = (dm * ya_ref[...] * (sa * (1.0 - sa))).astype(BF16)
        dgb_ref[...] = (dm * yb_ref[...] * (sb * (1.0 - sb))).astype(BF16)

    row = pl.BlockSpec((tm, d), lambda i: (i, 0))
    o = jax.ShapeDtypeStruct((m, d), BF16)
    return pl.pallas_call(
        body, name=name, out_shape=(o, o, o, o), grid=(m // tm,),
        in_specs=[pl.BlockSpec((tm, d), lambda i: (i, blk_a)), pl.BlockSpec((tm, d), lambda i: (i, blk_b)), row, row, row],
        out_specs=(row, row, row, row), compiler_params=_params(("parallel",)),
    )(proj, proj, ya, yb, dmerged)


def loss_fwd_bwd(name, y, target, tm=512):
    m, d = y.shape
    tm = _tile(m, tm, 16)

    def body(y_ref, t_ref, l_ref, dy_ref, dyb_ref):
        err = y_ref[...] - t_ref[...]
        dy = err * (1.0 / d)
        dy_ref[...] = dy
        dyb_ref[...] = dy.astype(BF16)
        part = 0.5 * jnp.sum(jnp.mean(err * err, axis=-1, keepdims=True), axis=0, keepdims=True)

        @pl.when(pl.program_id(0) == 0)
        def _():
            l_ref[...] = part

        @pl.when(pl.program_id(0) > 0)
        def _():
            l_ref[...] += part

    row = pl.BlockSpec((tm, d), lambda i: (i, 0))
    return pl.pallas_call(
        body, name=name, grid=(m // tm,),
        out_shape=(jax.ShapeDtypeStruct((1, 1), F32), jax.ShapeDtypeStruct((m, d), F32), jax.ShapeDtypeStruct((m, d), BF16)),
        in_specs=[row, row], out_specs=(pl.BlockSpec((1, 1), lambda i: (0, 0)), row, row),
        compiler_params=_params(("arbitrary",)),
    )(y, target)


def adamw(name, w, g, m, v):
    r, c = w.shape
    cap = (2 * 1024 * 1024) // 4
    tr, tc = _tile(r, 1024, 8), c
    while tr * tc > cap and tc % (2 * LANES) == 0:
        tc //= 2
    while tr * tc > cap and tr % 16 == 0:
        tr //= 2
    c1 = np.float32(1.0 - ADAM_B1 ** ADAM_STEP)
    c2 = np.float32(1.0 - ADAM_B2 ** ADAM_STEP)

    def body(w_ref, g_ref, m_ref, v_ref, d_ref, nm_ref, nv_ref):
        gv = g_ref[...]
        nm = ADAM_B1 * m_ref[...] + (1.0 - ADAM_B1) * gv
        nv = ADAM_B2 * v_ref[...] + (1.0 - ADAM_B2) * (gv * gv)
        d_ref[...] = -ADAM_LR * ((nm / c1) / (jnp.sqrt(nv / c2) + ADAM_EPS) + ADAM_WD * w_ref[...])
        nm_ref[...] = nm
        nv_ref[...] = nv

    blk = pl.BlockSpec((tr, tc), lambda i, j: (i, j))
    o = jax.ShapeDtypeStruct((r, c), F32)
    return pl.pallas_call(
        body, name=name, out_shape=(o, o, o), grid=(r // tr, c // tc), in_specs=[blk] * 4, out_specs=(blk, blk, blk),
        compiler_params=_params(("parallel", "parallel")),
    )(w, g, m, v)


HEAD_PAD = LANES


def _rope_rot():
    r = np.zeros((HEAD_PAD, HEAD_PAD), np.float32)
    half = QK_ROPE_DIM // 2
    for j in range(half):
        r[QK_NOPE_DIM + half + j, QK_NOPE_DIM + j] = -1.0
        r[QK_NOPE_DIM + j, QK_NOPE_DIM + half + j] = 1.0
    return r


def _head_spec(s, blk0):
    return pl.BlockSpec((None, s, HEAD_PAD), lambda bi, hi: (bi, 0, blk0 + hi))


def qk_prep_fwd(name, x, blk0, n_heads, d_real, gain, rope=None, extra=None):
    b, s, _ = x.shape

    def body(*refs):
        x_ref, g_ref = refs[0], refs[1]
        pos = 2
        xv = x_ref[...]
        if extra is not None:
            xv = xv + refs[pos][...]
            pos += 1
        y = xv * lax.rsqrt(jnp.sum(xv * xv, axis=-1, keepdims=True) * (1.0 / d_real) + EPS) * g_ref[...]
        if rope is not None:
            c_ref, s_ref, r_ref = refs[pos:pos + 3]
            rot = jnp.dot(y, r_ref[...], precision=lax.Precision.HIGHEST, preferred_element_type=F32)
            y = y * c_ref[...] + rot * s_ref[...]
        refs[-1][...] = y.astype(BF16)

    vec = pl.BlockSpec((1, HEAD_PAD), lambda bi, hi: (0, 0))
    tab = pl.BlockSpec((None, s, HEAD_PAD), lambda bi, hi: (bi, 0, 0))
    in_specs, args = [_head_spec(s, blk0), vec], [x, gain]
    if extra is not None:
        e_blk = extra[1]
        in_specs.append(pl.BlockSpec((None, s, HEAD_PAD), lambda bi, hi: (bi, 0, e_blk)))
        args.append(extra[0])
    if rope is not None:
        in_specs += [tab, tab, pl.BlockSpec((HEAD_PAD, HEAD_PAD), lambda bi, hi: (0, 0))]
        args += [rope[0], rope[1], jnp.asarray(_rope_rot())]
    return pl.pallas_call(
        body, name=name, out_shape=jax.ShapeDtypeStruct((b, s, n_heads * HEAD_PAD), BF16), grid=(b, n_heads),
        in_specs=in_specs, out_specs=_head_spec(s, 0), compiler_params=_params(("parallel", "parallel")),
    )(*args)


def qk_prep_bwd(name, x, blk0, n_heads, d_real, gain, dxh, rope=None, extra=None, head_sum=False):
    b, s, _ = x.shape

    def body(*refs):
        x_ref, g_ref, dy_ref = refs[:3]
        pos = 3
        xv = x_ref[...]
        if extra is not None:
            xv = xv + refs[pos][...]
            pos += 1
        dy = dy_ref[...]
        if rope is not None:
            c_ref, s_ref, rt_ref = refs[pos:pos + 3]
            pos += 3
            dy = dy * c_ref[...] + jnp.dot(dy * s_ref[...], rt_ref[...], precision=lax.Precision.HIGHEST, preferred_element_type=F32)
        outs = refs[pos:]
        dx_ref, dg_ref = outs[0], outs[-1]
        r = lax.rsqrt(jnp.sum(xv * xv, axis=-1, keepdims=True) * (1.0 / d_real) + EPS)
        xhat = xv * r
        dxhat = dy * g_ref[...]
        dx = r * (dxhat - xhat * (jnp.sum(dxhat * xhat, axis=-1, keepdims=True) * (1.0 / d_real)))
        dx_ref[...] = dx.astype(dx_ref.dtype)
        part = jnp.sum(dy * xhat, axis=0, keepdims=True)
        first = jnp.logical_and(pl.program_id(0) == 0, pl.program_id(1) == 0)

        @pl.when(first)
        def _():
            dg_ref[...] = part

        @pl.when(jnp.logical_not(first))
        def _():
            dg_ref[...] += part

        if head_sum:
            hs_ref = outs[1]

            @pl.when(pl.program_id(1) == 0)
            def _():
                hs_ref[...] = dx

            @pl.when(pl.program_id(1) > 0)
            def _():
                hs_ref[...] += dx

    vec = pl.BlockSpec((1, HEAD_PAD), lambda bi, hi: (0, 0))
    tab = pl.BlockSpec((None, s, HEAD_PAD), lambda bi, hi: (bi, 0, 0))
    in_specs, args = [_head_spec(s, blk0), vec, _head_spec(s, 0)], [x, gain, dxh]
    if extra is not None:
        e_blk = extra[1]
        in_specs.append(pl.BlockSpec((None, s, HEAD_PAD), lambda bi, hi: (bi, 0, e_blk)))
        args.append(extra[0])
    if rope is not None:
        in_specs += [tab, tab, pl.BlockSpec((HEAD_PAD, HEAD_PAD), lambda bi, hi: (0, 0))]
        args += [rope[0], rope[1], jnp.asarray(_rope_rot().T.copy())]
    out_shape, out_specs = [jax.ShapeDtypeStruct((b, s, n_heads * HEAD_PAD), BF16)], [_head_spec(s, 0)]
    if head_sum:
        out_shape.append(jax.ShapeDtypeStruct((b, s, HEAD_PAD), F32))
        out_specs.append(tab)
    out_shape.append(jax.ShapeDtypeStruct((1, HEAD_PAD), F32))
    out_specs.append(vec)
    return pl.pallas_call(
        body, name=name, out_shape=tuple(out_shape), grid=(b, n_heads), in_specs=in_specs, out_specs=tuple(out_specs),
        compiler_params=_params(("arbitrary", "arbitrary")),
    )(*args)


def _swa_specs(v_blk0):
    q = pl.BlockSpec((None, BLOCK, GROUP_A * HEAD_PAD), lambda b, kv, n: (b, n, kv))
    kprev = pl.BlockSpec((None, BLOCK, HEAD_PAD), lambda b, kv, n: (b, jnp.maximum(n - 1, 0), kv))
    kcur = pl.BlockSpec((None, BLOCK, HEAD_PAD), lambda b, kv, n: (b, n, kv))
    vprev = pl.BlockSpec((None, BLOCK, HEAD_PAD), lambda b, kv, n: (b, jnp.maximum(n - 1, 0), v_blk0 + kv))
    vcur = pl.BlockSpec((None, BLOCK, HEAD_PAD), lambda b, kv, n: (b, n, v_blk0 + kv))
    pcol = pl.BlockSpec((None, BLOCK, 1), lambda b, kv, n: (b, n, 0))
    prow_prev = pl.BlockSpec((None, 1, BLOCK), lambda b, kv, n: (b, 0, jnp.maximum(n - 1, 0)))
    prow_cur = pl.BlockSpec((None, 1, BLOCK), lambda b, kv, n: (b, 0, n))
    smem = pl.BlockSpec(memory_space=pltpu.SMEM)
    return q, kprev, kcur, vprev, vcur, pcol, prow_prev, prow_cur, smem


def _swa_probs(q, kk, dist, valid, slope, sink):
    sc = lax.dot_general(q, kk, _NT, preferred_element_type=F32) * (HEAD_DIM_A ** -0.5)
    sc = sc - slope * dist
    sc = jnp.where(valid, sc, NEG)
    m = jnp.maximum(jnp.max(sc, axis=-1, keepdims=True), sink)
    e = jnp.exp(sc - m)
    es = jnp.exp(sink - m)
    inv = 1.0 / (jnp.sum(e, axis=-1, keepdims=True) + es)
    return e * inv, es * inv


def _swa_window(n, kp_ref, kc_ref, vp_ref, vc_ref, pc_ref, prp_ref, prc_ref):
    kk = jnp.concatenate([kp_ref[...], kc_ref[...]], axis=0)
    vv = jnp.concatenate([vp_ref[...], vc_ref[...]], axis=0).astype(BF16)
    dist = pc_ref[...] - jnp.concatenate([prp_ref[...], prc_ref[...]], axis=1)
    qi = lax.broadcasted_iota(jnp.int32, (BLOCK, 2 * BLOCK), 0) + BLOCK
    ki = lax.broadcasted_iota(jnp.int32, (BLOCK, 2 * BLOCK), 1)
    diff = qi - ki
    valid = (diff >= 0) & (diff < BLOCK) & ((n > 0) | (ki >= BLOCK))
    return kk, vv, dist, valid


def swa_fwd(name, q, k, vsrc, v_blk0, pos_col, pos_row, slopes, sinks, plan=None):
    b, s, _ = q.shape
    qs, kprev, kcur, vprev, vcur, pcol, prp, prc, smem = _swa_specs(v_blk0)
    grid = (b, N_KV_HEADS_A, s // BLOCK)
    p_ins, p_in_specs, p_outs, p_out_specs, p_scratch, semantics, at_start, at_end = _hosted(plan, 10, 1, grid, ("parallel",) * 3)

    def body(*refs):
        q_ref, kp_ref, kc_ref, vp_ref, vc_ref, pc_ref, prp_ref, prc_ref, sl_ref, sk_ref = refs[:10]
        o_ref = refs[10 + len(p_ins)]
        at_start(refs)
        kv, n = pl.program_id(1), pl.program_id(2)
        kk, vv, dist, valid = _swa_window(n, kp_ref, kc_ref, vp_ref, vc_ref, pc_ref, prp_ref, prc_ref)
        for g in range(GROUP_A):
            hd = kv * GROUP_A + g
            lanes = slice(g * HEAD_PAD, (g + 1) * HEAD_PAD)
            p, _ = _swa_probs(q_ref[:, lanes], kk, dist, valid, sl_ref[hd], sk_ref[hd])
            o_ref[:, lanes] = jnp.dot(p.astype(BF16), vv, preferred_element_type=F32).astype(BF16)
        at_end(refs)

    res = pl.pallas_call(
        body, name=name, out_shape=(jax.ShapeDtypeStruct(q.shape, BF16),) + tuple(p_outs), grid=grid,
        in_specs=[qs, kprev, kcur, vprev, vcur, pcol, prp, prc, smem, smem] + p_in_specs, out_specs=(qs,) + tuple(p_out_specs),
        scratch_shapes=p_scratch, compiler_params=_params(semantics),
    )(q, k, k, vsrc, vsrc, pos_col, pos_row, pos_row, slopes, sinks, *p_ins)
    return res[0], list(res[1:])


def swa_bwd(name, q, k, vsrc, v_blk0, pos_col, pos_row, slopes, sinks, do, plan=None):
    b, s, _ = q.shape
    qs, kprev, kcur, vprev, vcur, pcol, prp, prc, smem = _swa_specs(v_blk0)
    grid = (b, N_KV_HEADS_A, s // BLOCK)
    p_ins, p_in_specs, p_outs, p_out_specs, p_scratch, semantics, at_start, at_end = _hosted(plan, 11, 4, grid, ("arbitrary",) * 3)

    def body(*refs):
        at_start(refs)
        compute(*refs[:11], *refs[11 + len(p_ins):15 + len(p_ins)])
        at_end(refs)

    def compute(q_ref, kp_ref, kc_ref, vp_ref, vc_ref, pc_ref, prp_ref, prc_ref, sl_ref, sk_ref, do_ref, dq_ref, dk_ref, dv_ref, ds_ref):
        bi, kv, n = pl.program_id(0), pl.program_id(1), pl.program_id(2)
        kk, vv, dist, valid = _swa_window(n, kp_ref, kc_ref, vp_ref, vc_ref, pc_ref, prp_ref, prc_ref)

        @pl.when((bi == 0) & (kv == 0) & (n == 0))
        def _():
            ds_ref[...] = jnp.zeros_like(ds_ref)

        @pl.when(n == 0)
        def _():
            dk_ref[...] = jnp.zeros_like(dk_ref)
            dv_ref[...] = jnp.zeros_like(dv_ref)

        dkk = jnp.zeros((2 * BLOCK, HEAD_PAD), F32)
        dvv = jnp.zeros((2 * BLOCK, HEAD_PAD), F32)
        head_row = lax.broadcasted_iota(jnp.int32, (N_HEADS_A, LANES), 0)
        dsink = jnp.zeros((N_HEADS_A, LANES), F32)
        for g in range(GROUP_A):
            hd = kv * GROUP_A + g
            lanes = slice(g * HEAD_PAD, (g + 1) * HEAD_PAD)
            qg = q_ref[:, lanes]
            p, ps = _swa_probs(qg, kk, dist, valid, sl_ref[hd], sk_ref[hd])
            dob = do_ref[:, lanes].astype(BF16)
            dvv = dvv + lax.dot_general(p.astype(BF16), dob, _TN, preferred_element_type=F32)
            dp = lax.dot_general(dob, vv, _NT, preferred_element_type=F32)
            rs = jnp.sum(p * dp, axis=-1, keepdims=True)
            dsb = (p * (dp - rs) * (HEAD_DIM_A ** -0.5)).astype(BF16)
            dq_ref[:, lanes] = jnp.dot(dsb, kk, preferred_element_type=F32)
            dkk = dkk + lax.dot_general(dsb, qg, _TN, preferred_element_type=F32)
            dsink = dsink + jnp.where(head_row == hd, -jnp.sum(ps * rs), 0.0)
        ds_ref[...] += dsink

        @pl.when(n > 0)
        def _():
            start = pl.multiple_of((n - 1) * BLOCK, BLOCK)
            dk_ref[pl.ds(start, 2 * BLOCK), :] += dkk
            dv_ref[pl.ds(start, 2 * BLOCK), :] += dvv

        @pl.when(n == 0)
        def _():
            dk_ref[0:BLOCK, :] += dkk[BLOCK:, :]
            dv_ref[0:BLOCK, :] += dvv[BLOCK:, :]

    kv_full = pl.BlockSpec((None, s, HEAD_PAD), lambda bi, kv, n: (bi, 0, kv))
    kv_shape = jax.ShapeDtypeStruct(k.shape, F32)
    res = pl.pallas_call(
        body, name=name, grid=grid,
        out_shape=(jax.ShapeDtypeStruct(q.shape, F32), kv_shape, kv_shape, jax.ShapeDtypeStruct((N_HEADS_A, LANES), F32)) + tuple(p_outs),
        in_specs=[qs, kprev, kcur, vprev, vcur, pcol, prp, prc, smem, smem, qs] + p_in_specs,
        out_specs=(qs, kv_full, kv_full, pl.BlockSpec((N_HEADS_A, LANES), lambda bi, kv, n: (0, 0))) + tuple(p_out_specs),
        scratch_shapes=p_scratch, compiler_params=_params(semantics),
    )(q, k, k, vsrc, vsrc, pos_col, pos_row, pos_row, slopes, sinks, do, *p_ins)
    return res[0], res[1], res[2], res[3], list(res[4:])


MLA_T = 512


def _hosted(plan, n_in, n_out, grid, semantics):
    if plan is None:
        return [], [], [], [], [], semantics, (lambda refs: None), (lambda refs: None)
    ni, no = len(plan.ins), len(plan.out_shapes)

    def split(refs):
        return refs[n_in:n_in + ni], refs[n_in + ni + n_out:n_in + ni + n_out + no], refs[-2], refs[-1]

    def at_start(refs):
        @pl.when(functools.reduce(jnp.logical_and, [pl.program_id(ax) == 0 for ax in range(len(grid))]))
        def _():
            plan.start(*split(refs))

    def at_end(refs):
        @pl.when(functools.reduce(jnp.logical_and, [pl.program_id(ax) == g - 1 for ax, g in enumerate(grid)]))
        def _():
            plan.finish(*split(refs))

    return plan.ins, [_HBM] * ni, plan.out_shapes, [_HBM] * no, plan.scratch(), ("arbitrary",) * len(grid), at_start, at_end


def mla_fwd(name, q, k, v, plan=None):
    b, s, w = q.shape
    h = w // HEAD_PAD
    t = MLA_T if s % MLA_T == 0 else BLOCK
    nb = s // t
    scale = QK_DIM_B ** -0.5
    p_ins, p_in_specs, p_outs, p_out_specs, p_scratch, semantics, at_start, at_end = _hosted(
        plan, 3, 3, (b, h), ("parallel", "parallel"))

    def body(*refs):
        q_ref, k_ref, v_ref = refs[:3]
        o_ref, ob_ref, lse_ref = refs[3 + len(p_ins):6 + len(p_ins)]
        at_start(refs)
        causal = lax.broadcasted_iota(jnp.int32, (t, t), 0) <= lax.broadcasted_iota(jnp.int32, (t, t), 1)
        for i in range(nb):
            q_i = q_ref[i * t:(i + 1) * t, :]

            def step(j, carry, q_i=q_i, diagonal=False):
                m, l, acc = carry
                rows = slice(j * t, (j + 1) * t) if diagonal else pl.ds(pl.multiple_of(j * t, t), t)
                st = lax.dot_general(k_ref[rows, :], q_i, _NT, preferred_element_type=F32) * scale
                if diagonal:
                    st = jnp.where(causal, st, NEG)
                m_new = jnp.maximum(m, jnp.max(st, axis=0, keepdims=True))
                a = jnp.exp(m - m_new)
                p = jnp.exp(st - m_new)
                l = a * l + jnp.sum(p, axis=0, keepdims=True)
                acc = a * acc + lax.dot_general(v_ref[rows, :], p.astype(BF16), _TN, preferred_element_type=F32)
                return m_new, l, acc

            carry = (jnp.full((1, t), NEG, F32), jnp.zeros((1, t), F32), jnp.zeros((HEAD_PAD, t), F32))
            if i > 0:
                carry = lax.fori_loop(0, i, step, carry)
            m, l, acc = step(i, carry, diagonal=True)
            o_i = (acc / l).T
            o_ref[i * t:(i + 1) * t, :] = o_i
            ob_ref[i * t:(i + 1) * t, :] = o_i.astype(BF16)
            lse_ref[i] = m + jnp.log(l)
        at_end(refs)

    hs = _head_spec(s, 0)
    ls = pl.BlockSpec((None, None, nb, 1, t), lambda bi, hi: (bi, hi, 0, 0, 0))
    res = pl.pallas_call(
        body, name=name, grid=(b, h),
        out_shape=(jax.ShapeDtypeStruct((b, s, w), F32), jax.ShapeDtypeStruct((b, s, w), BF16),
                   jax.ShapeDtypeStruct((b, h, nb, 1, t), F32)) + tuple(p_outs),
        in_specs=[hs, hs, hs] + p_in_specs, out_specs=(hs, hs, ls) + tuple(p_out_specs),
        scratch_shapes=p_scratch, compiler_params=_params(semantics),
    )(q, k, v, *p_ins)
    return res[0], res[1], res[2], list(res[3:])


def mla_bwd(name, q, k, v, o, do, lse, plan=None):
    b, s, w = q.shape
    h = w // HEAD_PAD
    _, _, nb, _, t = lse.shape
    scale = QK_DIM_B ** -0.5
    p_ins, p_in_specs, p_outs, p_out_specs, p_scratch, semantics, at_start, at_end = _hosted(
        plan, 6, 3, (b, h), ("parallel", "parallel"))

    def body(*refs):
        q_ref, k_ref, v_ref, o_ref, do_ref, lse_ref = refs[:6]
        dq_ref, dk_ref, dv_ref = refs[6 + len(p_ins):9 + len(p_ins)]
        dv_acc = refs[9 + len(p_ins) + len(p_outs)]
        at_start(refs)
        causal = lax.broadcasted_iota(jnp.int32, (t, t), 0) <= lax.broadcasted_iota(jnp.int32, (t, t), 1)
        dk_ref[...] = jnp.zeros_like(dk_ref)
        dv_acc[...] = jnp.zeros_like(dv_acc)
        for i in range(nb):
            q_i = q_ref[i * t:(i + 1) * t, :]
            do_i = do_ref[i * t:(i + 1) * t, :]
            delta = jnp.sum((o_ref[i * t:(i + 1) * t, :] * do_i).T, axis=0, keepdims=True)
            do_b = do_i.astype(BF16)
            lse_i = lse_ref[i]

            def step(j, dqt, q_i=q_i, do_b=do_b, delta=delta, lse_i=lse_i, diagonal=False):
                rows = slice(j * t, (j + 1) * t) if diagonal else pl.ds(pl.multiple_of(j * t, t), t)
                k_j = k_ref[rows, :]
                st = lax.dot_general(k_j, q_i, _NT, preferred_element_type=F32) * scale
                if diagonal:
                    st = jnp.where(causal, st, NEG)
                pt = jnp.exp(st - lse_i)
                dpt = lax.dot_general(v_ref[rows, :], do_b, _NT, preferred_element_type=F32)
                dst = (pt * (dpt - delta) * scale).astype(BF16)
                dv_acc[rows, :] += jnp.dot(pt.astype(BF16), do_b, preferred_element_type=F32)
                dk_ref[rows, :] += jnp.dot(dst, q_i, preferred_element_type=F32)
                return dqt + lax.dot_general(k_j, dst, _TN, preferred_element_type=F32)

            dqt = jnp.zeros((HEAD_PAD, t), F32)
            if i > 0:
                dqt = lax.fori_loop(0, i, step, dqt)
            dq_ref[i * t:(i + 1) * t, :] = step(i, dqt, diagonal=True).T
        dv_ref[...] = dv_acc[...].astype(BF16)
        at_end(refs)

    hs = _head_spec(s, 0)
    ls = pl.BlockSpec((None, None, nb, 1, t), lambda bi, hi: (bi, hi, 0, 0, 0))
    res = pl.pallas_call(
        body, name=name, grid=(b, h),
        out_shape=(jax.ShapeDtypeStruct((b, s, w), F32), jax.ShapeDtypeStruct((b, s, w), F32),
                   jax.ShapeDtypeStruct((b, s, w), BF16)) + tuple(p_outs),
        in_specs=[hs, hs, hs, hs, hs, ls] + p_in_specs, out_specs=(hs, hs, hs) + tuple(p_out_specs),
        scratch_shapes=[pltpu.VMEM((s, HEAD_PAD), F32)] + p_scratch,
        compiler_params=_params(semantics),
    )(q, k, v, o, do, lse, *p_ins)
    return res[0], res[1], res[2], list(res[3:])


_HBM = pl.BlockSpec(memory_space=pltpu.HBM)


def _place():
    x, y, c = lax.axis_index("x"), lax.axis_index("y"), lax.axis_index("c")
    chips = [(1 - x, y), (x, 1 - y), (1 - x, 1 - y)]
    return x, y, c, chips


def _remote(src, dst, send_sem, recv_sem, dev):
    return pltpu.make_async_remote_copy(src_ref=src, dst_ref=dst, send_sem=send_sem, recv_sem=recv_sem,
                                        device_id=dev, device_id_type=MESH)


class CommPlan:
    def __init__(self, ins, out_shapes, n_sems, start, finish, aliases=None):
        self.ins, self.out_shapes, self.n_sems = list(ins), list(out_shapes), n_sems
        self.start, self.finish, self.aliases = start, finish, dict(aliases or {})

    def scratch(self):
        return [pltpu.SemaphoreType.DMA((self.n_sems,)), pltpu.SemaphoreType.DMA((self.n_sems,))]


def comm_call(name, plan):
    ni, no = len(plan.ins), len(plan.out_shapes)

    def body(*refs):
        ins, outs, (send_sems, recv_sems) = refs[:ni], refs[ni:ni + no], refs[ni + no:]
        plan.start(ins, outs, send_sems, recv_sems)
        plan.finish(ins, outs, send_sems, recv_sems)

    return pl.pallas_call(
        body, name=name, out_shape=tuple(plan.out_shapes), in_specs=[_HBM] * ni, out_specs=tuple([_HBM] * no),
        input_output_aliases=plan.aliases, scratch_shapes=plan.scratch(),
    )(*plan.ins)


def gather_plan(packs, l):
    nk = len(packs)

    def parts(p_refs, o_refs, ss, rs):
        x, y, c, chips = _place()
        s_me = 2 * x + y
        sibling = (x, y, 1 - c)
        first = [_remote(p_refs[k].at[l], o_refs[k].at[s_me], ss.at[j * nk + k], rs.at[j * nk + k], (cx, cy, c))
                 for j, (cx, cy) in enumerate(chips) for k in range(nk)]
        own = [_remote(p_refs[k].at[l], o_refs[k].at[s_me], ss.at[6 * nk + k], rs.at[6 * nk + k], sibling) for k in range(nk)]
        return c, chips, sibling, first, own

    def start(p_refs, o_refs, ss, rs):
        c, _, _, first, own = parts(p_refs, o_refs, ss, rs)
        for cp in own:
            cp.start()

        @pl.when(c == l)
        def _():
            for cp in first:
                cp.start()

    def finish(p_refs, o_refs, ss, rs):
        c, chips, sibling, first, own = parts(p_refs, o_refs, ss, rs)

        @pl.when(c == l)
        def _():
            passed = []
            for j, (cx, cy) in enumerate(chips):
                for k in range(nk):
                    blk = o_refs[k].at[2 * cx + cy]
                    _remote(blk, blk, ss.at[j * nk + k], rs.at[j * nk + k], (cx, cy, c)).wait_recv()
                    fwd = _remote(blk, blk, ss.at[(3 + j) * nk + k], rs.at[(3 + j) * nk + k], sibling)
                    fwd.start()
                    passed.append(fwd)
            for cp in first + passed:
                cp.wait_send()

        @pl.when(c != l)
        def _():
            for j, (cx, cy) in enumerate(chips):
                for k in range(nk):
                    blk = o_refs[k].at[2 * cx + cy]
                    _remote(blk, blk, ss.at[(3 + j) * nk + k], rs.at[(3 + j) * nk + k], sibling).wait_recv()

        for cp in own:
            cp.wait()

    outs = [jax.ShapeDtypeStruct((N_SHARDS,) + p.shape[1:], p.dtype) for p in packs]
    return CommPlan(packs, outs, 7 * nk, start, finish)


def swap_plan(grads, l):
    nk = len(grads)

    def copies(g_refs, o_refs, ss, rs):
        x, y, c, _ = _place()
        return c, [_remote(g_refs[k], o_refs[k], ss.at[k], rs.at[k], (x, y, 1 - c)) for k in range(nk)]

    def start(g_refs, o_refs, ss, rs):
        c, cps = copies(g_refs, o_refs, ss, rs)

        @pl.when(c != l)
        def _():
            for cp in cps:
                cp.start()

    def finish(g_refs, o_refs, ss, rs):
        c, cps = copies(g_refs, o_refs, ss, rs)

        @pl.when(c != l)
        def _():
            for cp in cps:
                cp.wait_send()

        @pl.when(c == l)
        def _():
            for cp in cps:
                cp.wait_recv()

    return CommPlan(grads, [jax.ShapeDtypeStruct(g.shape, g.dtype) for g in grads], nk, start, finish)


def scatter_plan(parts, l):
    nk = len(parts)

    def copies(p_refs, o_refs, ss, rs):
        x, y, c, chips = _place()
        s_me = 2 * x + y
        sends = [_remote(p_refs[k].at[2 * cx + cy], o_refs[k].at[s_me], ss.at[j * nk + k], rs.at[j * nk + k], (cx, cy, c))
                 for j, (cx, cy) in enumerate(chips) for k in range(nk)]
        return c, chips, sends

    def start(p_refs, o_refs, ss, rs):
        c, _, sends = copies(p_refs, o_refs, ss, rs)

        @pl.when(c == l)
        def _():
            for cp in sends:
                cp.start()

    def finish(p_refs, o_refs, ss, rs):
        c, chips, sends = copies(p_refs, o_refs, ss, rs)

        @pl.when(c == l)
        def _():
            for j, (cx, cy) in enumerate(chips):
                for k in range(nk):
                    slot = o_refs[k].at[2 * cx + cy]
                    _remote(slot, slot, ss.at[j * nk + k], rs.at[j * nk + k], (cx, cy, c)).wait_recv()
            for cp in sends:
                cp.wait_send()

    return CommPlan(parts, [jax.ShapeDtypeStruct(p.shape, p.dtype) for p in parts], 3 * nk, start, finish)


def share_plan(finals):
    flat = [(l, f) for l, fs in enumerate(finals) for f in fs]

    def copies(f_refs, o_refs, ss, rs):
        x, y, c, _ = _place()
        return c, [_remote(f_refs[i], o_refs[i], ss.at[i], rs.at[i], (x, y, 1 - c)) for i in range(len(flat))]

    def start(f_refs, o_refs, ss, rs):
        c, cps = copies(f_refs, o_refs, ss, rs)
        for l in range(len(finals)):
            @pl.when(c == l)
            def _(l=l):
                for cp, (lf, _) in zip(cps, flat):
                    if lf == l:
                        cp.start()

    def finish(f_refs, o_refs, ss, rs):
        c, cps = copies(f_refs, o_refs, ss, rs)
        for l in range(len(finals)):
            @pl.when(c == l)
            def _(l=l):
                for cp, (lf, _) in zip(cps, flat):
                    if lf == l:
                        cp.wait_send()

            @pl.when(c != l)
            def _(l=l):
                for cp, (lf, _) in zip(cps, flat):
                    if lf == l:
                        cp.wait_recv()

    arrays = [f for _, f in flat]
    return CommPlan(arrays, [jax.ShapeDtypeStruct(f.shape, f.dtype) for f in arrays], len(flat), start, finish,
                    aliases={i: i for i in range(len(flat))})


def add_pair(name, own, other, tr=1024):
    n, cols = own.shape
    tr = _tile(n, tr, 16)

    def body(g_ref, o_ref, out_ref):
        out_ref[...] = (g_ref[...].astype(F32) + o_ref[...].astype(F32)).astype(out_ref.dtype)

    row = pl.BlockSpec((tr, cols), lambda i: (i, 0))
    return pl.pallas_call(
        body, name=name, out_shape=jax.ShapeDtypeStruct((n, cols), BF16), grid=(n // tr,), in_specs=[row, row], out_specs=row,
        compiler_params=_params(("parallel",)),
    )(own, other)


def sum_slots(name, slots, part, shard, tr=512):
    ns, r, cols = slots.shape
    tr = _tile(r, tr, 16)

    def body(s_idx, s_ref, own_ref, o_ref):
        total = own_ref[...].astype(F32)
        for k in range(1, ns):
            total = total + s_ref[(s_idx[0] + k) % ns].astype(F32)
        o_ref[...] = total

    return pl.pallas_call(
        body, name=name, out_shape=jax.ShapeDtypeStruct((r, cols), F32),
        grid_spec=pltpu.PrefetchScalarGridSpec(
            num_scalar_prefetch=1, grid=(r // tr,),
            in_specs=[pl.BlockSpec((ns, tr, cols), lambda i, p: (0, i, 0)), pl.BlockSpec((None, tr, cols), lambda i, p: (p[0], i, 0))],
            out_specs=pl.BlockSpec((tr, cols), lambda i, p: (i, 0))),
        compiler_params=_params(("parallel",)),
    )(shard, slots, part)


def small_allreduce(vec):
    r, cols = vec.shape

    def body(v_ref, o_ref, buf, send_sems, recv_sems):
        x, y, c, _ = _place()
        me = 4 * x + 2 * y + c
        buf[me] = v_ref[...]
        peers = []
        for k in range(1, N_DEV):
            px = (1 - x) if (k & 4) else x
            py = (1 - y) if (k & 2) else y
            pc = (1 - c) if (k & 1) else c
            peers.append((px, py, pc))
        sends = [_remote(buf.at[me], buf.at[me], send_sems.at[k], recv_sems.at[k], peer) for k, peer in enumerate(peers)]
        for cp in sends:
            cp.start()
        for k, (px, py, pc) in enumerate(peers):
            slot = buf.at[4 * px + 2 * py + pc]
            _remote(slot, slot, send_sems.at[k], recv_sems.at[k], (px, py, pc)).wait_recv()
        for cp in sends:
            cp.wait_send()
        total = buf[0]
        for k in range(1, N_DEV):
            total = total + buf[k]
        o_ref[...] = total

    return pl.pallas_call(
        body, name="small_allreduce", out_shape=jax.ShapeDtypeStruct((r, cols), F32),
        in_specs=[pl.BlockSpec(memory_space=pltpu.VMEM)], out_specs=pl.BlockSpec(memory_space=pltpu.VMEM),
        scratch_shapes=[pltpu.VMEM((N_DEV, r, cols), F32), pltpu.SemaphoreType.DMA((N_DEV - 1,)), pltpu.SemaphoreType.DMA((N_DEV - 1,))],
    )(vec)


ARG_NAMES = (("x", "positions") + WEIGHT_NAMES + ("loss_target",) + tuple("m_" + n for n in WEIGHT_NAMES)
             + tuple("v_" + n for n in WEIGHT_NAMES))

_C_KA = N_HEADS_A * HEAD_PAD
_C_VA = _C_KA + N_KV_HEADS_A * HEAD_PAD
_C_CQ = _C_VA + N_KV_HEADS_A * HEAD_PAD
_C_CKV = _C_CQ + Q_LORA_RANK
_C_KR = _C_CKV + KV_LORA_RANK
_C_GA = _C_KR + HEAD_PAD
_R_KA = WIDTH_A
_R_VA = _R_KA + KV_WIDTH_A
_R_CQ = _R_VA + KV_WIDTH_A
_R_KR = _R_CQ + Q_LORA_RANK + KV_LORA_RANK
_R_GA = _R_KR + QK_ROPE_DIM


def _pad_head_cols(w, n_heads):
    r, dh = w.shape[0], w.shape[1] // n_heads
    return jnp.pad(w.reshape(r, n_heads, dh), ((0, 0), (0, 0), (0, HEAD_PAD - dh))).reshape(r, n_heads * HEAD_PAD)


def _unpad_head_cols(w, n_heads, dh):
    r = w.shape[0]
    return w.reshape(r, n_heads, HEAD_PAD)[:, :, :dh].reshape(r, n_heads * dh)


def _pad_head_rows(w, n_heads):
    dh, c = w.shape[0] // n_heads, w.shape[1]
    return jnp.pad(w.reshape(n_heads, dh, c), ((0, 0), (0, HEAD_PAD - dh), (0, 0))).reshape(n_heads * HEAD_PAD, c)


def _unpad_head_rows(w, n_heads, dh):
    c = w.shape[1]
    return w.reshape(n_heads, HEAD_PAD, c)[:, :dh].reshape(n_heads * dh, c)


def _join_cols(shards):
    ns, r, c = shards.shape
    return shards.transpose(1, 0, 2).reshape(r, ns * c)


def _split_cols(mat):
    r, c4 = mat.shape
    return mat.reshape(r, N_SHARDS, c4 // N_SHARDS).transpose(1, 0, 2)


def _ffn_fwd(tag, x, gain, wcol, wdown, plan=None):
    nb = rms_fwd(tag + "_rms", x, gain)
    gu, h, carried = ffn_up_act(tag + "_up", nb, wcol, 0, plan=plan)
    out = gmm_down(tag + "_down", h, wdown, 0, x, 0.5)
    return out, (x, nb, gu, h), carried


def _ffn_bwd(tag, saved, gain, wcol, wdown, dout, doutb, plans=None):
    plans = plans or {}
    x, nb, gu, h = saved
    carried = {}
    dgu, carried["dh"] = ffn_down_dact(tag + "_dh", doutb, wdown, gu, gu.shape[-1], 0, 0.5, plan=plans.get("dh"))
    gdown = gmm_down_dw(tag + "_dwd", h, doutb, 0.5)
    if plans.get("dwgu") is not None:
        gcol, carried["dwgu"] = gmm_up_dw(tag + "_dwgu", nb, dgu, plan=plans["dwgu"])
    else:
        gcol = gmm_up_dw(tag + "_dwgu", nb, dgu)
    if plans.get("dn") is not None:
        dn, carried["dn"] = gmm_up_dx(tag + "_dn", dgu, wcol, x.shape[1], 0, plan=plans["dn"])
    else:
        dn = gmm_up_dx(tag + "_dn", dgu, wcol, x.shape[1], 0)
    dx, dxb, dgain = rms_bwd(tag + "_drms", x, gain, dn, dres=dout)
    return dx, dxb, dgain, gcol, gdown, carried


def _pad_gain(g):
    return jnp.pad(g, ((0, 0), (0, HEAD_PAD - g.shape[1])))


def _mixer_fwd(tag, x1, sm, w, wo, aux, plans=None):
    plans = plans or {}
    b, s = aux["b"], aux["s"]
    t, d = x1.shape
    hb = rms_fwd(tag + "_rms", x1, sm["mix_norm"])
    carried = {}
    if plans.get("proj") is not None:
        proj, carried["proj"] = mm_nn(tag + "_proj", hb, w["win"], tm=512, tn=2048, tk=1024, plan=plans["proj"])
    else:
        proj = mm_nn(tag + "_proj", hb, w["win"], tm=512, tn=2048, tk=1024)
    proj3 = proj.reshape(b, s, proj.shape[1])
    gains = {n: _pad_gain(sm[n]) for n in ("swa_q_norm", "swa_k_norm", "mla_q_norm", "mla_k_norm")}
    cqn = rms_fwd(tag + "_rms_cq", proj, sm["mla_q_lora_norm"], tm=1024, col_blk=_C_CQ // Q_LORA_RANK)
    ckvn = rms_fwd(tag + "_rms_ckv", proj, sm["mla_kv_lora_norm"], tm=1024, col_blk=_C_CKV // KV_LORA_RANK)
    qb_raw = mm_nn(tag + "_uq", cqn, w["wuq"], tm=1024).reshape(b, s, N_HEADS_B * HEAD_PAD)
    kb_raw = mm_nn(tag + "_uk", ckvn, w["wk"], tm=1024).reshape(b, s, N_HEADS_B * HEAD_PAD)
    vb = mm_nn(tag + "_uv", ckvn, w["wv"], tm=1024, out_dtype=BF16).reshape(b, s, N_HEADS_B * HEAD_PAD)
    qah = qk_prep_fwd(tag + "_qa_norm", proj3, 0, N_HEADS_A, HEAD_DIM_A, gains["swa_q_norm"])
    kah = qk_prep_fwd(tag + "_ka_norm", proj3, _C_KA // HEAD_PAD, N_KV_HEADS_A, HEAD_DIM_A, gains["swa_k_norm"])
    sinks = sm["swa_sinks"].reshape(-1)
    oab, carried["swa"] = swa_fwd(tag + "_swa", qah, kah, proj3, _C_VA // HEAD_PAD, aux["pos_col"], aux["pos_row"], aux["slopes"],
                                  sinks, plan=plans.get("swa"))
    kr = (proj3, _C_KR // HEAD_PAD)
    qbh = qk_prep_fwd(tag + "_qb_norm", qb_raw, 0, N_HEADS_B, QK_DIM_B, gains["mla_q_norm"], aux["rope"])
    kbh = qk_prep_fwd(tag + "_kb_norm", kb_raw, 0, N_HEADS_B, QK_DIM_B, gains["mla_k_norm"], aux["rope"], extra=kr)
    ob, obb, lse, carried["mla"] = mla_fwd(tag + "_mla", qbh, kbh, vb, plans.get("mla"))
    oab2, obb2 = oab.reshape(t, -1), obb.reshape(t, -1)
    ya = mm_nn(tag + "_branch_a", oab2, w["wa"], tm=1024)
    yb = mm_nn(tag + "_branch_b", obb2, w["wb"], tm=1024)
    mg = gate_fwd(tag + "_gate", proj, ya, yb, _C_GA, _C_GA + d)
    x2 = gmm_rows(tag + "_out", mg, wo, x1)
    saved = dict(x1=x1, hb=hb, proj=proj, cqn=cqn, ckvn=ckvn, qb_raw=qb_raw, kb_raw=kb_raw, vb=vb, qah=qah, kah=kah,
                 qbh=qbh, kbh=kbh, ob=ob, lse=lse, oab2=oab2, obb2=obb2, ya=ya, yb=yb, mg=mg, gains=gains)
    return x2, saved, carried


def _mixer_bwd(tag, sv, sm, w, wo, aux, dx2, dx2b, plans=None):
    plans = plans or {}
    carried = {}
    b, s = aux["b"], aux["s"]
    t, d = dx2.shape
    proj, gains = sv["proj"], sv["gains"]
    proj3 = proj.reshape(b, s, proj.shape[1])
    kr = (proj3, _C_KR // HEAD_PAD)
    dmg = gmm_rows_dx(tag + "_d_out", dx2b, wo)
    dwo = gmm_rows_dw(tag + "_dw_out", sv["mg"], dx2b)
    dya, dyb, dga, dgb = gate_bwd(tag + "_dgate", proj, sv["ya"], sv["yb"], dmg, _C_GA, _C_GA + d)
    dwa = mm_tn(tag + "_dw_branch_a", sv["oab2"], dya, out_dtype=BF16)
    dwb = mm_tn(tag + "_dw_branch_b", sv["obb2"], dyb, out_dtype=BF16)
    doa = mm_nt(tag + "_d_branch_a", dya, w["wa"], tm=1024).reshape(b, s, -1)
    dob = mm_nt(tag + "_d_branch_b", dyb, w["wb"], tm=1024).reshape(b, s, -1)
    sinks = sm["swa_sinks"].reshape(-1)
    dqah, dkah, dva, dsinks, carried["dswa"] = swa_bwd(tag + "_dswa", sv["qah"], sv["kah"], proj3, _C_VA // HEAD_PAD, aux["pos_col"],
                                                       aux["pos_row"], aux["slopes"], sinks, doa, plan=plans.get("dswa"))
    dqa_raw, dg_swa_q = qk_prep_bwd(tag + "_dqa_norm", proj3, 0, N_HEADS_A, HEAD_DIM_A, gains["swa_q_norm"], dqah)
    dka_raw, dg_swa_k = qk_prep_bwd(tag + "_dka_norm", proj3, _C_KA // HEAD_PAD, N_KV_HEADS_A, HEAD_DIM_A, gains["swa_k_norm"], dkah)
    dqbh, dkbh, dvb, carried["dmla"] = mla_bwd(tag + "_dmla", sv["qbh"], sv["kbh"], sv["vb"], sv["ob"], dob, sv["lse"], plans.get("dmla"))
    dqb_raw, dg_mla_q = qk_prep_bwd(tag + "_dqb_norm", sv["qb_raw"], 0, N_HEADS_B, QK_DIM_B, gains["mla_q_norm"], dqbh, aux["rope"])
    dkb_raw, dkr_sum, dg_mla_k = qk_prep_bwd(tag + "_dkb_norm", sv["kb_raw"], 0, N_HEADS_B, QK_DIM_B, gains["mla_k_norm"], dkbh,
                                             aux["rope"], extra=kr, head_sum=True)
    dq_tok, dk_tok, dv_tok = dqb_raw.reshape(t, -1), dkb_raw.reshape(t, -1), dvb.reshape(t, -1)
    dwuq = mm_tn(tag + "_dw_uq", sv["cqn"], dq_tok, tk=1024, out_dtype=BF16)
    dwk = mm_tn(tag + "_dw_uk", sv["ckvn"], dk_tok, tk=1024, out_dtype=BF16)
    dwv = mm_tn(tag + "_dw_uv", sv["ckvn"], dv_tok, tk=1024, out_dtype=BF16)
    dcqn = mm_nt(tag + "_d_uq", dq_tok, w["wuq"], tm=1024)
    dckvn = mm_nt(tag + "_d_uv", dv_tok, w["wv"], tm=1024, res=mm_nt(tag + "_d_uk", dk_tok, w["wk"], tm=1024))
    dcq, dg_q_lora = rms_bwd(tag + "_drms_cq", proj, sm["mla_q_lora_norm"], dcqn, want_f32=False, tm=1024, col_blk=_C_CQ // Q_LORA_RANK)
    dckv, dg_kv_lora = rms_bwd(tag + "_drms_ckv", proj, sm["mla_kv_lora_norm"], dckvn, want_f32=False, tm=1024,
                               col_blk=_C_CKV // KV_LORA_RANK)
    dproj = jnp.concatenate([dqa_raw.reshape(t, -1), dka_raw.reshape(t, -1), dva.reshape(t, -1).astype(BF16), dcq, dckv,
                             dkr_sum.reshape(t, HEAD_PAD).astype(BF16), dga, dgb], axis=1)
    dwin = mm_tn(tag + "_dw_in", sv["hb"], dproj, tm=1024, tn=2048, tk=1024, out_dtype=BF16)
    dh = mm_nt(tag + "_d_in", dproj, w["win"], tm=512, tn=1024, tk=2048)
    dx1, dx1b, dg_mix = rms_bwd(tag + "_drms", sv["x1"], sm["mix_norm"], dh, dres=dx2)
    wgrads = dict(win=dwin, wuq=dwuq, wk=dwk, wv=dwv, wa=dwa, wb=dwb, wo=dwo)
    sgrads = dict(mix_norm=dg_mix, swa_q_norm=dg_swa_q[:, :HEAD_DIM_A], swa_k_norm=dg_swa_k[:, :HEAD_DIM_A],
                  swa_sinks=dsinks[:, 0].reshape(1, -1), mla_q_lora_norm=dg_q_lora, mla_kv_lora_norm=dg_kv_lora,
                  mla_q_norm=dg_mla_q[:, :QK_DIM_B], mla_k_norm=dg_mla_k[:, :QK_DIM_B])
    return dx1, dx1b, wgrads, sgrads, carried


def _layer_weights(win4, uq4, ukv4, branch4):
    wr = _join_cols(win4)
    kr = jnp.pad(wr[:, _R_KR:_R_GA], ((0, 0), (QK_NOPE_DIM, HEAD_PAD - QK_DIM_B)))
    win = jnp.concatenate([_pad_head_cols(wr[:, :_R_KA], N_HEADS_A), _pad_head_cols(wr[:, _R_KA:_R_VA], N_KV_HEADS_A),
                           _pad_head_cols(wr[:, _R_VA:_R_CQ], N_KV_HEADS_A), wr[:, _R_CQ:_R_KR], kr, wr[:, _R_GA:]], axis=1)
    ukv = _join_cols(ukv4)
    ukv3 = ukv.reshape(ukv.shape[0], N_HEADS_B, QK_NOPE_DIM + V_DIM_B)
    r = branch4.shape[1] // 2
    return dict(win=win, wuq=_pad_head_cols(_join_cols(uq4), N_HEADS_B),
                wk=_pad_head_cols(ukv3[:, :, :QK_NOPE_DIM].reshape(ukv.shape[0], -1), N_HEADS_B),
                wv=_pad_head_cols(ukv3[:, :, QK_NOPE_DIM:].reshape(ukv.shape[0], -1), N_HEADS_B),
                wa=_pad_head_rows(_join_cols(branch4[:, :r]), N_HEADS_A), wb=_pad_head_rows(_join_cols(branch4[:, r:]), N_HEADS_B))


def _mixer_grad_shards(g):
    dw = g["win"]
    win_ref = jnp.concatenate([_unpad_head_cols(dw[:, :_C_KA], N_HEADS_A, HEAD_DIM_A),
                               _unpad_head_cols(dw[:, _C_KA:_C_VA], N_KV_HEADS_A, HEAD_DIM_A),
                               _unpad_head_cols(dw[:, _C_VA:_C_CQ], N_KV_HEADS_A, HEAD_DIM_A), dw[:, _C_CQ:_C_KR],
                               dw[:, _C_KR + QK_NOPE_DIM:_C_KR + QK_DIM_B], dw[:, _C_GA:]], axis=1)
    rk = g["wk"].shape[0]
    ukv = jnp.concatenate([g["wk"].reshape(rk, N_HEADS_B, HEAD_PAD)[:, :, :QK_NOPE_DIM],
                           g["wv"].reshape(rk, N_HEADS_B, HEAD_PAD)[:, :, :V_DIM_B]], axis=2).reshape(rk, -1)
    return (_split_cols(win_ref), _split_cols(_unpad_head_cols(g["wuq"], N_HEADS_B, QK_DIM_B)), _split_cols(ukv),
            jnp.concatenate([_split_cols(_unpad_head_rows(g["wa"], N_HEADS_A, HEAD_DIM_A)),
                             _split_cols(_unpad_head_rows(g["wb"], N_HEADS_B, V_DIM_B))], axis=1))


BUFFER_TAGS = ("col1", "col2", "down1", "down2", "wo", "win", "uq", "ukv", "branch")
FIRST_GATHER = ("col1", "down1")
GATHER_CARRIERS = {(0, "ffn1_up"): (0, ("wo", "win", "uq", "ukv", "branch")), (0, "proj"): (1, ("col1",)),
                   (0, "swa"): (1, ("down1", "wo", "uq", "ukv")), (0, "mla"): (0, ("col2", "down2")),
                   (0, "ffn2_up"): (1, ("win", "branch")), (1, "ffn1_up"): (1, ("col2",)), (1, "swa"): (1, ("down2",))}
SCATTER_CARRIERS = {(0, "ffn2_dh"): (1, ("col1",)), (0, "ffn2_dwgu"): (1, ("col2",)), (0, "ffn2_dn"): (1, ("down1", "down2", "wo")),
                    (0, "dswa"): (1, ("win", "uq", "ukv", "branch")), (0, "dmla"): (0, ("col2", "down2")),
                    (0, "ffn1_dh"): (0, ("wo", "win", "uq", "ukv", "branch"))}
LAST_SCATTER = ("col1", "down1")


def _pick(by_tag, tags):
    return [by_tag[t] for t in tags]


def _train_step(x, positions, target, packs, small, shard):
    depth = 2
    b, s, d = x.shape
    t = b * s
    posf = positions.astype(F32)
    half = QK_ROPE_DIM // 2
    inv_freq = ROPE_BASE ** (-jnp.arange(half, dtype=F32) / half)
    ang = posf[..., None] * inv_freq
    cos, sin = jnp.cos(ang), jnp.sin(ang)
    tail = HEAD_PAD - QK_DIM_B
    rope = (jnp.concatenate([jnp.ones((b, s, QK_NOPE_DIM), F32), cos, cos, jnp.ones((b, s, tail), F32)], axis=-1),
            jnp.concatenate([jnp.zeros((b, s, QK_NOPE_DIM), F32), sin, sin, jnp.zeros((b, s, tail), F32)], axis=-1))
    slopes = jnp.exp2(-8.0 * (jnp.arange(N_HEADS_A, dtype=F32) + 1.0) / N_HEADS_A)
    aux = dict(b=b, s=s, pos_col=posf.reshape(b, s, 1), pos_row=posf.reshape(b, 1, s), rope=rope, slopes=slopes)

    def sm_of(l):
        return {n: small[n][l:l + 1] for n in SMALL_NAMES}

    by_tag = dict(zip(BUFFER_TAGS, packs))
    wts, parts, slots = [{} for _ in range(depth)], [{} for _ in range(depth)], [{} for _ in range(depth)]
    saved, sgrads = [None] * depth, [None] * depth

    def gather_piece(l, kern):
        if (l, kern) not in GATHER_CARRIERS:
            return None
        data_l, tags = GATHER_CARRIERS[(l, kern)]
        return gather_plan(_pick(by_tag, tags), data_l)

    def gathered(l, kern, arrays):
        if (l, kern) in GATHER_CARRIERS:
            data_l, tags = GATHER_CARRIERS[(l, kern)]
            wts[data_l].update(zip(tags, arrays))

    def scatter_piece(l, kern):
        if (l, kern) not in SCATTER_CARRIERS:
            return None
        data_l, tags = SCATTER_CARRIERS[(l, kern)]
        return scatter_plan(_pick(parts[data_l], tags), data_l)

    def scattered(l, kern, arrays):
        if (l, kern) in SCATTER_CARRIERS:
            data_l, tags = SCATTER_CARRIERS[(l, kern)]
            slots[data_l].update(zip(tags, arrays))

    def chip_sums(l, tags, mine):
        theirs = comm_call(f"swap_l{l}_{tags[0]}", swap_plan(_pick(mine, tags), l))
        for tag, o in zip(tags, theirs):
            g = mine[tag]
            parts[l][tag] = add_pair(f"l{l}_add_{tag}", g.reshape(-1, g.shape[-1]), o.reshape(-1, o.shape[-1])).reshape(g.shape)

    wts[0].update(zip(FIRST_GATHER, comm_call("gather_first", gather_plan(_pick(by_tag, FIRST_GATHER), 0))))
    h = x.reshape(t, d)
    for l in range(depth):
        sm, wl = sm_of(l), wts[l]
        h, s1, got = _ffn_fwd(f"l{l}_ffn1", h, sm["ffn1_norm"], wl["col1"], wl["down1"], gather_piece(l, "ffn1_up"))
        gathered(l, "ffn1_up", got)
        weights = _layer_weights(wl["win"], wl["uq"], wl["ukv"], wl["branch"])
        h, s2, got = _mixer_fwd(f"l{l}_mix", h, sm, weights, wl["wo"], aux, {k: gather_piece(l, k) for k in ("proj", "swa", "mla")})
        for kern, arrays in got.items():
            gathered(l, kern, arrays)
        h, s3, got = _ffn_fwd(f"l{l}_ffn2", h, sm["ffn2_norm"], wl["col2"], wl["down2"], gather_piece(l, "ffn2_up"))
        gathered(l, "ffn2_up", got)
        saved[l] = (s1, s2, s3, weights)
    loss, dh, dhb = loss_fwd_bwd("loss", h, target.reshape(t, d))

    for l in reversed(range(depth)):
        sm, wl = sm_of(l), wts[l]
        s1, s2, s3, weights = saved[l]
        staged = l == 0
        dh, dhb, dg_f2, gcol2, gdown2, got = _ffn_bwd(f"l{l}_ffn2", s3, sm["ffn2_norm"], wl["col2"], wl["down2"], dh, dhb,
                                                      {k: scatter_piece(l, "ffn2_" + k) for k in ("dh", "dwgu", "dn")})
        for kern, arrays in got.items():
            scattered(l, "ffn2_" + kern, arrays)
        mine = dict(col2=gcol2, down2=gdown2)
        if staged:
            chip_sums(l, ("col2", "down2"), mine)
        dh, dhb, wg, sg, got = _mixer_bwd(f"l{l}_mix", s2, sm, weights, wl["wo"], aux, dh, dhb,
                                          {k: scatter_piece(l, k) for k in ("dswa", "dmla")})
        for kern, arrays in got.items():
            scattered(l, kern, arrays)
        gwin, guq, gukv, gbranch = _mixer_grad_shards(wg)
        mine.update(wo=wg["wo"], win=gwin, uq=guq, ukv=gukv, branch=gbranch)
        if staged:
            chip_sums(l, ("wo", "win", "uq", "ukv", "branch"), mine)
        dh, dhb, dg_f1, gcol1, gdown1, got = _ffn_bwd(f"l{l}_ffn1", s1, sm["ffn1_norm"], wl["col1"], wl["down1"], dh, dhb,
                                                      {k: scatter_piece(l, "ffn1_" + k) for k in ("dh", "dwgu", "dn")})
        for kern, arrays in got.items():
            scattered(l, "ffn1_" + kern, arrays)
        mine.update(col1=gcol1, down1=gdown1)
        chip_sums(l, LAST_SCATTER if staged else BUFFER_TAGS, mine)
        sg.update(ffn1_norm=dg_f1, ffn2_norm=dg_f2)
        sgrads[l] = sg
    slots[0].update(zip(LAST_SCATTER, comm_call("scatter_last", scatter_plan(_pick(parts[0], LAST_SCATTER), 0))))
    finals = [[sum_slots(f"l{l}_sum_{tag}", slots[l][tag], parts[l][tag], shard) for tag in BUFFER_TAGS] for l in range(depth)]
    shared = comm_call("share", share_plan(finals))
    nk = len(BUFFER_TAGS)
    return loss, dh.reshape(b, s, d), [dict(zip(BUFFER_TAGS, shared[l * nk:(l + 1) * nk])) for l in range(depth)], sgrads


def kernel(x, positions, ffn1_norm, ffn1_w_gate, ffn1_w_up, ffn1_w_down, mix_norm, w_in, swa_q_norm, swa_k_norm, swa_sinks, mla_q_lora_norm, mla_w_uq, mla_kv_lora_norm, mla_w_ukv, mla_q_norm, mla_k_norm, w_branch_a, w_branch_b, w_out, ffn2_norm, ffn2_w_gate, ffn2_w_up, ffn2_w_down, loss_target, m_ffn1_norm, m_ffn1_w_gate, m_ffn1_w_up, m_ffn1_w_down, m_mix_norm, m_w_in, m_swa_q_norm, m_swa_k_norm, m_swa_sinks, m_mla_q_lora_norm, m_mla_w_uq, m_mla_kv_lora_norm, m_mla_w_ukv, m_mla_q_norm, m_mla_k_norm, m_w_branch_a, m_w_branch_b, m_w_out, m_ffn2_norm, m_ffn2_w_gate, m_ffn2_w_up, m_ffn2_w_down, v_ffn1_norm, v_ffn1_w_gate, v_ffn1_w_up, v_ffn1_w_down, v_mix_norm, v_w_in, v_swa_q_norm, v_swa_k_norm, v_swa_sinks, v_mla_q_lora_norm, v_mla_w_uq, v_mla_kv_lora_norm, v_mla_w_ukv, v_mla_q_norm, v_mla_k_norm, v_w_branch_a, v_w_branch_b, v_w_out, v_ffn2_norm, v_ffn2_w_gate, v_ffn2_w_up, v_ffn2_w_down):
    args = (x, positions, ffn1_norm, ffn1_w_gate, ffn1_w_up, ffn1_w_down, mix_norm, w_in, swa_q_norm, swa_k_norm, swa_sinks, mla_q_lora_norm, mla_w_uq, mla_kv_lora_norm, mla_w_ukv, mla_q_norm, mla_k_norm, w_branch_a, w_branch_b, w_out, ffn2_norm, ffn2_w_gate, ffn2_w_up, ffn2_w_down, loss_target, m_ffn1_norm, m_ffn1_w_gate, m_ffn1_w_up, m_ffn1_w_down, m_mix_norm, m_w_in, m_swa_q_norm, m_swa_k_norm, m_swa_sinks, m_mla_q_lora_norm, m_mla_w_uq, m_mla_kv_lora_norm, m_mla_w_ukv, m_mla_q_norm, m_mla_k_norm, m_w_branch_a, m_w_branch_b, m_w_out, m_ffn2_norm, m_ffn2_w_gate, m_ffn2_w_up, m_ffn2_w_down, v_ffn1_norm, v_ffn1_w_gate, v_ffn1_w_up, v_ffn1_w_down, v_mix_norm, v_w_in, v_swa_q_norm, v_swa_k_norm, v_swa_sinks, v_mla_q_lora_norm, v_mla_w_uq, v_mla_kv_lora_norm, v_mla_w_ukv, v_mla_q_norm, v_mla_k_norm, v_w_branch_a, v_w_branch_b, v_w_out, v_ffn2_norm, v_ffn2_w_gate, v_ffn2_w_up, v_ffn2_w_down)
    a = dict(zip(ARG_NAMES, args, strict=True))
    x = a["x"]
    depth = a["ffn1_norm"].shape[0]
    d = x.shape[-1]
    assert depth == 2, "one layer slab per core of a chip"

    def bf16_rows(*names):
        return jnp.concatenate([a[n] for n in names], axis=1).astype(BF16) if len(names) > 1 else a[names[0]].astype(BF16)

    packs = [bf16_rows("ffn1_w_gate", "ffn1_w_up"), bf16_rows("ffn2_w_gate", "ffn2_w_up"), bf16_rows("ffn1_w_down"),
             bf16_rows("ffn2_w_down"), bf16_rows("w_out"), bf16_rows("w_in"), bf16_rows("mla_w_uq"), bf16_rows("mla_w_ukv"),
             bf16_rows("w_branch_a", "w_branch_b")]
    shard_me = (2 * lax.axis_index("x") + lax.axis_index("y")).astype(jnp.int32).reshape(1)
    small = {n: a[n] for n in SMALL_NAMES}

    loss, grad_x, summed, sgrads = _train_step(x, a["positions"], a["loss_target"], packs, small, shard_me)

    full = {tag: jnp.stack([summed[l][tag] for l in range(depth)]) for tag in BUFFER_TAGS}
    r_branch = a["w_branch_a"].shape[1]
    grads = dict(ffn1_w_gate=full["col1"][:, :d], ffn1_w_up=full["col1"][:, d:], ffn2_w_gate=full["col2"][:, :d],
                 ffn2_w_up=full["col2"][:, d:], ffn1_w_down=full["down1"], ffn2_w_down=full["down2"], w_out=full["wo"],
                 w_in=full["win"], mla_w_uq=full["uq"], mla_w_ukv=full["ukv"], w_branch_a=full["branch"][:, :r_branch],
                 w_branch_b=full["branch"][:, r_branch:])

    flat = jnp.concatenate([jnp.concatenate([sgrads[l][n].reshape(-1) for l in range(depth)]) for n in SMALL_NAMES] + [loss.reshape(-1)])
    n_small = flat.shape[0]
    rows = -(-n_small // (8 * LANES)) * 8
    pad = rows * LANES - n_small

    def small_pack(v):
        return jnp.pad(v, (0, pad)).reshape(rows, LANES)

    total = small_allreduce(small_pack(flat))
    w_s, m_s, v_s = (small_pack(jnp.concatenate([a[p + n].reshape(-1) for n in SMALL_NAMES] + [jnp.zeros((1,), F32)]))
                     for p in ("", "m_", "v_"))
    d_s, nm_s, nv_s = adamw("adamw_small", w_s, total, m_s, v_s)

    def small_unpack(buf):
        out, off, flat_b = {}, 0, buf.reshape(-1)
        for n in SMALL_NAMES:
            size = a[n].shape[0] * a[n].shape[1]
            out[n] = flat_b[off:off + size].reshape(a[n].shape)
            off += size
        return out

    grads.update(small_unpack(total))
    delta, new_m, new_v = small_unpack(d_s), small_unpack(nm_s), small_unpack(nv_s)
    for n in PACK_NAMES:
        flip = a[n].shape[2] % LANES != 0

        def view(z):
            return z.transpose(0, 2, 1) if flip else z

        w_v, g_v, m_v, v_v = view(a[n]), view(grads[n]), view(a["m_" + n]), view(a["v_" + n])
        shp = w_v.shape
        two_d = (shp[0] * shp[1], shp[2])
        dn, mn, vn = adamw("adamw_" + n, w_v.reshape(two_d), g_v.reshape(two_d), m_v.reshape(two_d), v_v.reshape(two_d))
        delta[n], new_m[n], new_v[n] = view(dn.reshape(shp)), view(mn.reshape(shp)), view(vn.reshape(shp))
        if flip:
            grads[n] = view(g_v)

    loss_out = total.reshape(-1)[n_small - 1]
    return (loss_out, grad_x, *[grads[n] for n in WEIGHT_NAMES], *[delta[n] for n in WEIGHT_NAMES],
            *[new_m[n] for n in WEIGHT_NAMES], *[new_v[n] for n in WEIGHT_NAMES])
```

```python
import functools

import numpy as np
import jax
import jax.numpy as jnp
from jax import lax
from jax.experimental import pallas as pl
from jax.experimental.pallas import tpu as pltpu

F32 = jnp.float32
BF16 = jnp.bfloat16
MESH = pl.DeviceIdType.MESH

HEAD_DIM_A = 64
N_HEADS_A = 8
N_KV_HEADS_A = 2
GROUP_A = N_HEADS_A // N_KV_HEADS_A
BLOCK = 128
N_HEADS_B = 8
Q_LORA_RANK = 256
KV_LORA_RANK = 128
QK_NOPE_DIM = 64
QK_ROPE_DIM = 32
QK_DIM_B = QK_NOPE_DIM + QK_ROPE_DIM
V_DIM_B = 64
ROPE_BASE = 10000.0
WIDTH_A = N_HEADS_A * HEAD_DIM_A
KV_WIDTH_A = N_KV_HEADS_A * HEAD_DIM_A
EPS = 1e-6
NEG = -1e30
ADAM_LR = 0.001
ADAM_B1 = 0.9
ADAM_B2 = 0.999
ADAM_EPS = 1e-08
ADAM_WD = 0.01
ADAM_STEP = 10

N_SHARDS = 4
N_DEV = 8
LANES = 128
VMEM_LIMIT = 48 * 1024 * 1024

PACK_NAMES = ("ffn1_w_gate", "ffn1_w_up", "ffn1_w_down", "w_in", "mla_w_uq", "mla_w_ukv",
              "w_branch_a", "w_branch_b", "w_out", "ffn2_w_gate", "ffn2_w_up", "ffn2_w_down")
SMALL_NAMES = ("ffn1_norm", "mix_norm", "swa_q_norm", "swa_k_norm", "swa_sinks", "mla_q_lora_norm",
               "mla_kv_lora_norm", "mla_q_norm", "mla_k_norm", "ffn2_norm")
WEIGHT_NAMES = ("ffn1_norm", "ffn1_w_gate", "ffn1_w_up", "ffn1_w_down", "mix_norm", "w_in", "swa_q_norm",
                "swa_k_norm", "swa_sinks", "mla_q_lora_norm", "mla_w_uq", "mla_kv_lora_norm", "mla_w_ukv",
                "mla_q_norm", "mla_k_norm", "w_branch_a", "w_branch_b", "w_out", "ffn2_norm", "ffn2_w_gate",
                "ffn2_w_up", "ffn2_w_down")


def _params(sem):
    return pltpu.CompilerParams(dimension_semantics=sem, vmem_limit_bytes=VMEM_LIMIT)


def _tile(n, want, align):
    if n <= want:
        return n
    t = (want // align) * align
    while t > align and n % t:
        t -= align
    assert t >= align and n % t == 0, (n, want, align)
    return t


def _mm_call(name, a, b, out_struct, grid, a_spec, b_spec, o_spec, dims, n_red, acc_shape, alpha=1.0, res=None,
             inner=0, a_cols=0, plan=None):
    if n_red and all(g == 1 for g in grid[len(grid) - n_red:]):
        n_red = 0
    n_par = len(grid) - n_red
    n_in = 2 + int(res is not None)
    p_ins, p_in_specs, p_outs, p_out_specs, p_scratch, semantics, at_start, at_end = _hosted(
        plan, n_in, 1, grid, ("parallel",) * n_par + ("arbitrary",) * n_red)

    def body(*refs):
        at_start(refs)
        compute(refs)
        at_end(refs)

    def compute(refs):
        a_ref, b_ref = refs[:2]
        r_ref = refs[2] if res is not None else None
        o_ref, acc_ref = refs[n_in + len(p_ins)], refs[n_in + len(p_ins) + 1 + len(p_outs)]
        if inner:
            def a_of(s):
                return a_ref[:, s * a_cols:(s + 1) * a_cols] if a_cols else a_ref[s]

            part = lax.dot_general(a_of(0), b_ref[0], dims, preferred_element_type=F32)
            for s in range(1, inner):
                part = part + lax.dot_general(a_of(s), b_ref[s], dims, preferred_element_type=F32)
        else:
            part = lax.dot_general(a_ref[...], b_ref[...], dims, preferred_element_type=F32)

        def finish(total):
            if alpha != 1.0:
                total = total * alpha
            if r_ref is not None:
                total = r_ref[...] + total
            o_ref[...] = total.astype(o_ref.dtype)

        if n_red == 0:
            finish(part)
            return
        ids = [pl.program_id(n_par + i) for i in range(n_red)]
        first = functools.reduce(jnp.logical_and, [i == 0 for i in ids])
        last = functools.reduce(jnp.logical_and, [i == grid[n_par + k] - 1 for k, i in enumerate(ids)])

        @pl.when(first)
        def _():
            acc_ref[...] = part

        @pl.when(jnp.logical_not(first))
        def _():
            acc_ref[...] += part

        @pl.when(last)
        def _():
            finish(acc_ref[...])

    in_specs = [a_spec, b_spec] + ([o_spec] if res is not None else [])
    args = (a, b) + ((res,) if res is not None else ())
    if plan is None:
        return pl.pallas_call(
            body, name=name, out_shape=out_struct, grid=grid, in_specs=in_specs, out_specs=o_spec,
            scratch_shapes=[pltpu.VMEM(acc_shape, F32)], compiler_params=_params(semantics),
        )(*args)
    res_all = pl.pallas_call(
        body, name=name, out_shape=(out_struct,) + tuple(p_outs), grid=grid, in_specs=in_specs + p_in_specs,
        out_specs=(o_spec,) + tuple(p_out_specs), scratch_shapes=[pltpu.VMEM(acc_shape, F32)] + p_scratch,
        compiler_params=_params(semantics),
    )(*args, *p_ins)
    return res_all[0], list(res_all[1:])


_NN = (((1,), (0,)), ((), ()))
_NT = (((1,), (1,)), ((), ()))
_TN = (((0,), (0,)), ((), ()))


def mm_nn(name, a, b, tm=512, tn=1024, tk=1024, out_dtype=F32, alpha=1.0, res=None, plan=None):
    (m, k), (_, n) = a.shape, b.shape
    tm, tn, tk = _tile(m, tm, 16), _tile(n, tn, LANES), _tile(k, tk, LANES)
    return _mm_call(name, a, b, jax.ShapeDtypeStruct((m, n), out_dtype), (m // tm, n // tn, k // tk),
                    pl.BlockSpec((tm, tk), lambda i, j, kk: (i, kk)), pl.BlockSpec((tk, tn), lambda i, j, kk: (kk, j)),
                    pl.BlockSpec((tm, tn), lambda i, j, kk: (i, j)), _NN, 1, (tm, tn), alpha, res, plan=plan)


def mm_nt(name, a, b, tm=512, tn=1024, tk=1024, out_dtype=F32, alpha=1.0, res=None):
    (m, n), (k, _) = a.shape, b.shape
    tm, tn, tk = _tile(m, tm, 16), _tile(k, tn, LANES), _tile(n, tk, LANES)
    return _mm_call(name, a, b, jax.ShapeDtypeStruct((m, k), out_dtype), (m // tm, k // tn, n // tk),
                    pl.BlockSpec((tm, tk), lambda i, j, kk: (i, kk)), pl.BlockSpec((tn, tk), lambda i, j, kk: (j, kk)),
                    pl.BlockSpec((tm, tn), lambda i, j, kk: (i, j)), _NT, 1, (tm, tn), alpha, res)


def mm_tn(name, a, b, tm=1024, tn=1024, tk=1024, out_dtype=F32, alpha=1.0):
    (m, k), (_, n) = a.shape, b.shape
    tm, tn, tk = _tile(k, tm, LANES), _tile(n, tn, LANES), _tile(m, tk, 16)
    return _mm_call(name, a, b, jax.ShapeDtypeStruct((k, n), out_dtype), (k // tm, n // tn, m // tk),
                    pl.BlockSpec((tk, tm), lambda i, j, kk: (kk, i)), pl.BlockSpec((tk, tn), lambda i, j, kk: (kk, j)),
                    pl.BlockSpec((tm, tn), lambda i, j, kk: (i, j)), _TN, 1, (tm, tn), alpha)


def ffn_up_act(name, a, w, blk, tm=1024, plan=None):
    (m, k), (ns, _, n) = a.shape, w.shape
    tm = _tile(m, tm, 16)
    grid = (ns, m // tm)
    p_ins, p_in_specs, p_outs, p_out_specs, p_scratch, semantics, at_start, at_end = _hosted(plan, 3, 2, grid, ("parallel", "parallel"))

    def body(*refs):
        a_ref, wg_ref, wu_ref = refs[:3]
        gu_ref, h_ref = refs[3 + len(p_ins):5 + len(p_ins)]
        at_start(refs)
        av = a_ref[...]
        gate = jnp.dot(av, wg_ref[...], preferred_element_type=F32)
        up = jnp.dot(av, wu_ref[...], preferred_element_type=F32)
        gu_ref[0] = gate.astype(BF16)
        gu_ref[1] = up.astype(BF16)
        h_ref[...] = (gate * jax.nn.sigmoid(gate) * up).astype(BF16)
        at_end(refs)

    res = pl.pallas_call(
        body, name=name, grid=grid,
        out_shape=(jax.ShapeDtypeStruct((2, ns, m, n), BF16), jax.ShapeDtypeStruct((ns, m, n), BF16)) + tuple(p_outs),
        in_specs=[pl.BlockSpec((tm, k), lambda s, i: (i, 0)), pl.BlockSpec((None, k, n), lambda s, i: (s, blk, 0)),
                  pl.BlockSpec((None, k, n), lambda s, i: (s, blk + 1, 0))] + p_in_specs,
        out_specs=(pl.BlockSpec((2, None, tm, n), lambda s, i: (0, s, i, 0)), pl.BlockSpec((None, tm, n), lambda s, i: (s, i, 0)))
        + tuple(p_out_specs),
        scratch_shapes=p_scratch, compiler_params=_params(semantics),
    )(a, w, w, *p_ins)
    return res[0], res[1], list(res[2:])


def ffn_down_dact(name, a, w, gu, n, blk, alpha, tm=1024, plan=None):
    (m, d), ns = a.shape, w.shape[0]
    tm = _tile(m, tm, 16)
    grid = (ns, m // tm)
    p_ins, p_in_specs, p_outs, p_out_specs, p_scratch, semantics, at_start, at_end = _hosted(plan, 3, 1, grid, ("parallel", "parallel"))

    def body(*refs):
        a_ref, w_ref, gu_ref = refs[:3]
        o_ref = refs[3 + len(p_ins)]
        at_start(refs)
        dh = lax.dot_general(a_ref[...], w_ref[...], _NT, preferred_element_type=F32) * alpha
        gate, up = gu_ref[0].astype(F32), gu_ref[1].astype(F32)
        s = jax.nn.sigmoid(gate)
        o_ref[0] = (dh * up * (s * (1.0 + gate * (1.0 - s)))).astype(BF16)
        o_ref[1] = (dh * (gate * s)).astype(BF16)
        at_end(refs)

    gu_spec = pl.BlockSpec((2, None, tm, n), lambda s, i: (0, s, i, 0))
    res = pl.pallas_call(
        body, name=name, grid=grid, out_shape=(jax.ShapeDtypeStruct((2, ns, m, n), BF16),) + tuple(p_outs),
        in_specs=[pl.BlockSpec((tm, d), lambda s, i: (i, 0)), pl.BlockSpec((None, n, d), lambda s, i: (s, blk, 0)), gu_spec] + p_in_specs,
        out_specs=(gu_spec,) + tuple(p_out_specs), scratch_shapes=p_scratch, compiler_params=_params(semantics),
    )(a, w, gu, *p_ins)
    return res[0], list(res[1:])


def gmm_up_dw(name, a, dgu, tk=2048, plan=None):
    (m, k), (_, ns, _, n) = a.shape, dgu.shape
    tk = _tile(m, tk, 16)
    return _mm_call(name, a, dgu, jax.ShapeDtypeStruct((ns, 2 * k, n), BF16), (2, ns, m // tk),
                    pl.BlockSpec((tk, k), lambda j, s, kk: (kk, 0)), pl.BlockSpec((None, None, tk, n), lambda j, s, kk: (j, s, kk, 0)),
                    pl.BlockSpec((None, k, n), lambda j, s, kk: (s, j, 0)), _TN, 1, (k, n), plan=plan)


def gmm_up_dx(name, dgu, w, k, blk, tm=512, plan=None):
    _, ns, m, n = dgu.shape
    tm = _tile(m, tm, 16)
    return _mm_call(name, dgu, w, jax.ShapeDtypeStruct((m, k), F32), (m // tm, 2),
                    pl.BlockSpec((None, ns, tm, n), lambda i, j: (j, 0, i, 0)),
                    pl.BlockSpec((ns, k, n), lambda i, j: (0, blk + j, 0)),
                    pl.BlockSpec((tm, k), lambda i, j: (i, 0)), _NT, 1, (tm, k), inner=ns, plan=plan)


def gmm_down(name, h, w, blk, res, alpha, tm=512):
    (ns, m, n), d = h.shape, w.shape[2]
    tm = _tile(m, tm, 16)
    return _mm_call(name, h, w, jax.ShapeDtypeStruct((m, d), F32), (m // tm,),
                    pl.BlockSpec((ns, tm, n), lambda i: (0, i, 0)), pl.BlockSpec((ns, n, d), lambda i: (0, blk, 0)),
                    pl.BlockSpec((tm, d), lambda i: (i, 0)), _NN, 0, (8, LANES), alpha, res, inner=ns)


def gmm_down_dw(name, h, b, alpha, tk=2048):
    (ns, m, n), d = h.shape, b.shape[1]
    tk = _tile(m, tk, 16)
    return _mm_call(name, h, b, jax.ShapeDtypeStruct((ns, n, d), BF16), (ns, m // tk),
                    pl.BlockSpec((None, tk, n), lambda s, kk: (s, kk, 0)), pl.BlockSpec((tk, d), lambda s, kk: (kk, 0)),
                    pl.BlockSpec((None, n, d), lambda s, kk: (s, 0, 0)), _TN, 1, (n, d), alpha)


def gmm_rows(name, a, w, res, tm=1024):
    (m, _), (ns, r, d) = a.shape, w.shape
    tm = _tile(m, tm, 16)
    return _mm_call(name, a, w, jax.ShapeDtypeStruct((m, d), F32), (m // tm,),
                    pl.BlockSpec((tm, ns * r), lambda i: (i, 0)), pl.BlockSpec((ns, r, d), lambda i: (0, 0, 0)),
                    pl.BlockSpec((tm, d), lambda i: (i, 0)), _NN, 0, (8, LANES), 1.0, res, inner=ns, a_cols=r)


def gmm_rows_dx(name, a, w, tm=1024):
    (m, d), (ns, r, _) = a.shape, w.shape
    tm = _tile(m, tm, 16)
    return _mm_call(name, a, w, jax.ShapeDtypeStruct((m, ns * r), F32), (ns, m // tm),
                    pl.BlockSpec((tm, d), lambda s, i: (i, 0)), pl.BlockSpec((None, r, d), lambda s, i: (s, 0, 0)),
                    pl.BlockSpec((tm, r), lambda s, i: (i, s)), _NT, 0, (8, LANES))


def gmm_rows_dw(name, a, b, tk=2048):
    (m, da), d = a.shape, b.shape[1]
    ns, r = N_SHARDS, da // N_SHARDS
    tk = _tile(m, tk, 16)
    return _mm_call(name, a, b, jax.ShapeDtypeStruct((ns, r, d), BF16), (ns, m // tk),
                    pl.BlockSpec((tk, r), lambda s, kk: (kk, s)), pl.BlockSpec((tk, d), lambda s, kk: (kk, 0)),
                    pl.BlockSpec((None, r, d), lambda s, kk: (s, 0, 0)), _TN, 1, (r, d))


def rms_fwd(name, x, gain, tm=1024, col_blk=0):
    m, d = x.shape[0], gain.shape[1]
    tm = _tile(m, tm, 16)

    def body(x_ref, g_ref, o_ref):
        xv = x_ref[...]
        r = lax.rsqrt(jnp.mean(xv * xv, axis=-1, keepdims=True) + EPS)
        o_ref[...] = (xv * r * g_ref[...]).astype(o_ref.dtype)

    return pl.pallas_call(
        body, name=name, out_shape=jax.ShapeDtypeStruct((m, d), BF16), grid=(m // tm,),
        in_specs=[pl.BlockSpec((tm, d), lambda i: (i, col_blk)), pl.BlockSpec((1, d), lambda i: (0, 0))],
        out_specs=pl.BlockSpec((tm, d), lambda i: (i, 0)), compiler_params=_params(("parallel",)),
    )(x, gain)


def rms_bwd(name, x, gain, dn, dres=None, want_f32=True, want_bf16=True, tm=512, col_blk=0):
    m, d = x.shape[0], gain.shape[1]
    tm = _tile(m, tm, 16)
    n_out = int(want_f32) + int(want_bf16)

    def body(*refs):
        x_ref, g_ref, dn_ref = refs[:3]
        pos = 3
        r_ref = None
        if dres is not None:
            r_ref = refs[pos]
            pos += 1
        outs = refs[pos:pos + n_out]
        dg_ref = refs[pos + n_out]
        xv = x_ref[...]
        r = lax.rsqrt(jnp.mean(xv * xv, axis=-1, keepdims=True) + EPS)
        xhat = xv * r
        dnv = dn_ref[...]
        dxhat = dnv * g_ref[...]
        dx = r * (dxhat - xhat * jnp.mean(dxhat * xhat, axis=-1, keepdims=True))
        if r_ref is not None:
            dx = r_ref[...] + dx
        for o in outs:
            o[...] = dx.astype(o.dtype)
        part = jnp.sum(dnv * xhat, axis=0, keepdims=True)

        @pl.when(pl.program_id(0) == 0)
        def _():
            dg_ref[...] = part

        @pl.when(pl.program_id(0) > 0)
        def _():
            dg_ref[...] += part

    row = pl.BlockSpec((tm, d), lambda i: (i, 0))
    vec = pl.BlockSpec((1, d), lambda i: (0, 0))
    out_shape = ([jax.ShapeDtypeStruct((m, d), F32)] if want_f32 else []) + ([jax.ShapeDtypeStruct((m, d), BF16)] if want_bf16 else [])
    res = pl.pallas_call(
        body, name=name, out_shape=tuple(out_shape) + (jax.ShapeDtypeStruct((1, d), F32),), grid=(m // tm,),
        in_specs=[pl.BlockSpec((tm, d), lambda i: (i, col_blk)), vec, row] + ([row] if dres is not None else []),
        out_specs=tuple([row] * n_out) + (vec,), compiler_params=_params(("arbitrary",)),
    )(*((x, gain, dn) + ((dres,) if dres is not None else ())))
    return res


def gate_fwd(name, proj, ya, yb, off_a, off_b, tm=256):
    m, d = ya.shape
    tm = _tile(m, tm, 16)
    blk_a, blk_b = off_a // d, off_b // d
    assert blk_a * d == off_a and blk_b * d == off_b, "the gates start on multiples of their width"

    def body(ga_ref, gb_ref, ya_ref, yb_ref, o_ref):
        o_ref[...] = (jax.nn.sigmoid(ga_ref[...]) * ya_ref[...] + jax.nn.sigmoid(gb_ref[...]) * yb_ref[...]).astype(o_ref.dtype)

    row = pl.BlockSpec((tm, d), lambda i: (i, 0))
    return pl.pallas_call(
        body, name=name, out_shape=jax.ShapeDtypeStruct((m, d), BF16), grid=(m // tm,),
        in_specs=[pl.BlockSpec((tm, d), lambda i: (i, blk_a)), pl.BlockSpec((tm, d), lambda i: (i, blk_b)), row, row], out_specs=row,
        compiler_params=_params(("parallel",)),
    )(proj, proj, ya, yb)


def gate_bwd(name, proj, ya, yb, dmerged, off_a, off_b, tm=256):
    m, d = ya.shape
    tm = _tile(m, tm, 16)
    blk_a, blk_b = off_a // d, off_b // d
    assert blk_a * d == off_a and blk_b * d == off_b, "the gates start on multiples of their width"

    def body(ga_ref, gb_ref, ya_ref, yb_ref, dm_ref, dya_ref, dyb_ref, dga_ref, dgb_ref):
        sa, sb = jax.nn.sigmoid(ga_ref[...]), jax.nn.sigmoid(gb_ref[...])
        dm = dm_ref[...]
        dya_ref[...] = (dm * sa).astype(BF16)
        dyb_ref[...] = (dm * sb).astype(BF16)
        dga_ref[...] = (dm * ya_ref[...] * (sa * (1.0 - sa))).astype(BF16)
        dgb_ref[...] = (dm * yb_ref[...] * (sb * (1.0 - sb))).astype(BF16)

    row = pl.BlockSpec((tm, d), lambda i: (i, 0))
    o = jax.ShapeDtypeStruct((m, d), BF16)
    return pl.pallas_call(
        body, name=name, out_shape=(o, o, o, o), grid=(m // tm,),
        in_specs=[pl.BlockSpec((tm, d), lambda i: (i, blk_a)), pl.BlockSpec((tm, d), lambda i: (i, blk_b)), row, row, row],
        out_specs=(row, row, row, row), compiler_params=_params(("parallel",)),
    )(proj, proj, ya, yb, dmerged)


def loss_fwd_bwd(name, y, target, tm=512):
    m, d = y.shape
    tm = _tile(m, tm, 16)

    def body(y_ref, t_ref, l_ref, dy_ref, dyb_ref):
        err = y_ref[...] - t_ref[...]
        dy = err * (1.0 / d)
        dy_ref[...] = dy
        dyb_ref[...] = dy.astype(BF16)
        part = 0.5 * jnp.sum(jnp.mean(err * err, axis=-1, keepdims=True), axis=0, keepdims=True)

        @pl.when(pl.program_id(0) == 0)
        def _():
            l_ref[...] = part

        @pl.when(pl.program_id(0) > 0)
        def _():
            l_ref[...] += part

    row = pl.BlockSpec((tm, d), lambda i: (i, 0))
    return pl.pallas_call(
        body, name=name, grid=(m // tm,),
        out_shape=(jax.ShapeDtypeStruct((1, 1), F32), jax.ShapeDtypeStruct((m, d), F32), jax.ShapeDtypeStruct((m, d), BF16)),
        in_specs=[row, row], out_specs=(pl.BlockSpec((1, 1), lambda i: (0, 0)), row, row),
        compiler_params=_params(("arbitrary",)),
    )(y, target)


def adamw(name, w, g, m, v):
    r, c = w.shape
    cap = (2 * 1024 * 1024) // 4
    tr, tc = _tile(r, 1024, 8), c
    while tr * tc > cap and tc % (2 * LANES) == 0:
        tc //= 2
    while tr * tc > cap and tr % 16 == 0:
        tr //= 2
    c1 = np.float32(1.0 - ADAM_B1 ** ADAM_STEP)
    c2 = np.float32(1.0 - ADAM_B2 ** ADAM_STEP)

    def body(w_ref, g_ref, m_ref, v_ref, d_ref, nm_ref, nv_ref):
        gv = g_ref[...]
        nm = ADAM_B1 * m_ref[...] + (1.0 - ADAM_B1) * gv
        nv = ADAM_B2 * v_ref[...] + (1.0 - ADAM_B2) * (gv * gv)
        d_ref[...] = -ADAM_LR * ((nm / c1) / (jnp.sqrt(nv / c2) + ADAM_EPS) + ADAM_WD * w_ref[...])
        nm_ref[...] = nm
        nv_ref[...] = nv

    blk = pl.BlockSpec((tr, tc), lambda i, j: (i, j))
    o = jax.ShapeDtypeStruct((r, c), F32)
    return pl.pallas_call(
        body, name=name, out_shape=(o, o, o), grid=(r // tr, c // tc), in_specs=[blk] * 4, out_specs=(blk, blk, blk),
        compiler_params=_params(("parallel", "parallel")),
    )(w, g, m, v)


HEAD_PAD = LANES


def _rope_rot():
    r = np.zeros((HEAD_PAD, HEAD_PAD), np.float32)
    half = QK_ROPE_DIM // 2
    for j in range(half):
        r[QK_NOPE_DIM + half + j, QK_NOPE_DIM + j] = -1.0
        r[QK_NOPE_DIM + j, QK_NOPE_DIM + half + j] = 1.0
    return r


def _head_spec(s, blk0):
    return pl.BlockSpec((None, s, HEAD_PAD), lambda bi, hi: (bi, 0, blk0 + hi))


def qk_prep_fwd(name, x, blk0, n_heads, d_real, gain, rope=None, extra=None):
    b, s, _ = x.shape

    def body(*refs):
        x_ref, g_ref = refs[0], refs[1]
        pos = 2
        xv = x_ref[...]
        if extra is not None:
            xv = xv + refs[pos][...]
            pos += 1
        y = xv * lax.rsqrt(jnp.sum(xv * xv, axis=-1, keepdims=True) * (1.0 / d_real) + EPS) * g_ref[...]
        if rope is not None:
            c_ref, s_ref, r_ref = refs[pos:pos + 3]
            rot = jnp.dot(y, r_ref[...], precision=lax.Precision.HIGHEST, preferred_element_type=F32)
            y = y * c_ref[...] + rot * s_ref[...]
        refs[-1][...] = y.astype(BF16)

    vec = pl.BlockSpec((1, HEAD_PAD), lambda bi, hi: (0, 0))
    tab = pl.BlockSpec((None, s, HEAD_PAD), lambda bi, hi: (bi, 0, 0))
    in_specs, args = [_head_spec(s, blk0), vec], [x, gain]
    if extra is not None:
        e_blk = extra[1]
        in_specs.append(pl.BlockSpec((None, s, HEAD_PAD), lambda bi, hi: (bi, 0, e_blk)))
        args.append(extra[0])
    if rope is not None:
        in_specs += [tab, tab, pl.BlockSpec((HEAD_PAD, HEAD_PAD), lambda bi, hi: (0, 0))]
        args += [rope[0], rope[1], jnp.asarray(_rope_rot())]
    return pl.pallas_call(
        body, name=name, out_shape=jax.ShapeDtypeStruct((b, s, n_heads * HEAD_PAD), BF16), grid=(b, n_heads),
        in_specs=in_specs, out_specs=_head_spec(s, 0), compiler_params=_params(("parallel", "parallel")),
    )(*args)


def qk_prep_bwd(name, x, blk0, n_heads, d_real, gain, dxh, rope=None, extra=None, head_sum=False):
    b, s, _ = x.shape

    def body(*refs):
        x_ref, g_ref, dy_ref = refs[:3]
        pos = 3
        xv = x_ref[...]
        if extra is not None:
            xv = xv + refs[pos][...]
            pos += 1
        dy = dy_ref[...]
        if rope is not None:
            c_ref, s_ref, rt_ref = refs[pos:pos + 3]
            pos += 3
            dy = dy * c_ref[...] + jnp.dot(dy * s_ref[...], rt_ref[...], precision=lax.Precision.HIGHEST, preferred_element_type=F32)
        outs = refs[pos:]
        dx_ref, dg_ref = outs[0], outs[-1]
        r = lax.rsqrt(jnp.sum(xv * xv, axis=-1, keepdims=True) * (1.0 / d_real) + EPS)
        xhat = xv * r
        dxhat = dy * g_ref[...]
        dx = r * (dxhat - xhat * (jnp.sum(dxhat * xhat, axis=-1, keepdims=True) * (1.0 / d_real)))
        dx_ref[...] = dx.astype(dx_ref.dtype)
        part = jnp.sum(dy * xhat, axis=0, keepdims=True)
        first = jnp.logical_and(pl.program_id(0) == 0, pl.program_id(1) == 0)

        @pl.when(first)
        def _():
            dg_ref[...] = part

        @pl.when(jnp.logical_not(first))
        def _():
            dg_ref[...] += part

        if head_sum:
            hs_ref = outs[1]

            @pl.when(pl.program_id(1) == 0)
            def _():
                hs_ref[...] = dx

            @pl.when(pl.program_id(1) > 0)
            def _():
                hs_ref[...] += dx

    vec = pl.BlockSpec((1, HEAD_PAD), lambda bi, hi: (0, 0))
    tab = pl.BlockSpec((None, s, HEAD_PAD), lambda bi, hi: (bi, 0, 0))
    in_specs, args = [_head_spec(s, blk0), vec, _head_spec(s, 0)], [x, gain, dxh]
    if extra is not None:
        e_blk = extra[1]
        in_specs.append(pl.BlockSpec((None, s, HEAD_PAD), lambda bi, hi: (bi, 0, e_blk)))
        args.append(extra[0])
    if rope is not None:
        in_specs += [tab, tab, pl.BlockSpec((HEAD_PAD, HEAD_PAD), lambda bi, hi: (0, 0))]
        args += [rope[0], rope[1], jnp.asarray(_rope_rot().T.copy())]
    out_shape, out_specs = [jax.ShapeDtypeStruct((b, s, n_heads * HEAD_PAD), BF16)], [_head_spec(s, 0)]
    if head_sum:
        out_shape.append(jax.ShapeDtypeStruct((b, s, HEAD_PAD), F32))
        out_specs.append(tab)
    out_shape.append(jax.ShapeDtypeStruct((1, HEAD_PAD), F32))
    out_specs.append(vec)
    return pl.pallas_call(
        body, name=name, out_shape=tuple(out_shape), grid=(b, n_heads), in_specs=in_specs, out_specs=tuple(out_specs),
        compiler_params=_params(("arbitrary", "arbitrary")),
    )(*args)


def _swa_specs(v_blk0):
    q = pl.BlockSpec((None, BLOCK, GROUP_A * HEAD_PAD), lambda b, kv, n: (b, n, kv))
    kprev = pl.BlockSpec((None, BLOCK, HEAD_PAD), lambda b, kv, n: (b, jnp.maximum(n - 1, 0), kv))
    kcur = pl.BlockSpec((None, BLOCK, HEAD_PAD), lambda b, kv, n: (b, n, kv))
    vprev = pl.BlockSpec((None, BLOCK, HEAD_PAD), lambda b, kv, n: (b, jnp.maximum(n - 1, 0), v_blk0 + kv))
    vcur = pl.BlockSpec((None, BLOCK, HEAD_PAD), lambda b, kv, n: (b, n, v_blk0 + kv))
    pcol = pl.BlockSpec((None, BLOCK, 1), lambda b, kv, n: (b, n, 0))
    prow_prev = pl.BlockSpec((None, 1, BLOCK), lambda b, kv, n: (b, 0, jnp.maximum(n - 1, 0)))
    prow_cur = pl.BlockSpec((None, 1, BLOCK), lambda b, kv, n: (b, 0, n))
    smem = pl.BlockSpec(memory_space=pltpu.SMEM)
    return q, kprev, kcur, vprev, vcur, pcol, prow_prev, prow_cur, smem


def _swa_probs(q, kk, dist, valid, slope, sink):
    sc = lax.dot_general(q, kk, _NT, preferred_element_type=F32) * (HEAD_DIM_A ** -0.5)
    sc = sc - slope * dist
    sc = jnp.where(valid, sc, NEG)
    m = jnp.maximum(jnp.max(sc, axis=-1, keepdims=True), sink)
    e = jnp.exp(sc - m)
    es = jnp.exp(sink - m)
    inv = 1.0 / (jnp.sum(e, axis=-1, keepdims=True) + es)
    return e * inv, es * inv


def _swa_window(n, kp_ref, kc_ref, vp_ref, vc_ref, pc_ref, prp_ref, prc_ref):
    kk = jnp.concatenate([kp_ref[...], kc_ref[...]], axis=0)
    vv = jnp.concatenate([vp_ref[...], vc_ref[...]], axis=0).astype(BF16)
    dist = pc_ref[...] - jnp.concatenate([prp_ref[...], prc_ref[...]], axis=1)
    qi = lax.broadcasted_iota(jnp.int32, (BLOCK, 2 * BLOCK), 0) + BLOCK
    ki = lax.broadcasted_iota(jnp.int32, (BLOCK, 2 * BLOCK), 1)
    diff = qi - ki
    valid = (diff >= 0) & (diff < BLOCK) & ((n > 0) | (ki >= BLOCK))
    return kk, vv, dist, valid


def swa_fwd(name, q, k, vsrc, v_blk0, pos_col, pos_row, slopes, sinks, plan=None):
    b, s, _ = q.shape
    qs, kprev, kcur, vprev, vcur, pcol, prp, prc, smem = _swa_specs(v_blk0)
    grid = (b, N_KV_HEADS_A, s // BLOCK)
    p_ins, p_in_specs, p_outs, p_out_specs, p_scratch, semantics, at_start, at_end = _hosted(plan, 10, 1, grid, ("parallel",) * 3)

    def body(*refs):
        q_ref, kp_ref, kc_ref, vp_ref, vc_ref, pc_ref, prp_ref, prc_ref, sl_ref, sk_ref = refs[:10]
        o_ref = refs[10 + len(p_ins)]
        at_start(refs)
        kv, n = pl.program_id(1), pl.program_id(2)
        kk, vv, dist, valid = _swa_window(n, kp_ref, kc_ref, vp_ref, vc_ref, pc_ref, prp_ref, prc_ref)
        for g in range(GROUP_A):
            hd = kv * GROUP_A + g
            lanes = slice(g * HEAD_PAD, (g + 1) * HEAD_PAD)
            p, _ = _swa_probs(q_ref[:, lanes], kk, dist, valid, sl_ref[hd], sk_ref[hd])
            o_ref[:, lanes] = jnp.dot(p.astype(BF16), vv, preferred_element_type=F32).astype(BF16)
        at_end(refs)

    res = pl.pallas_call(
        body, name=name, out_shape=(jax.ShapeDtypeStruct(q.shape, BF16),) + tuple(p_outs), grid=grid,
        in_specs=[qs, kprev, kcur, vprev, vcur, pcol, prp, prc, smem, smem] + p_in_specs, out_specs=(qs,) + tuple(p_out_specs),
        scratch_shapes=p_scratch, compiler_params=_params(semantics),
    )(q, k, k, vsrc, vsrc, pos_col, pos_row, pos_row, slopes, sinks, *p_ins)
    return res[0], list(res[1:])


def swa_bwd(name, q, k, vsrc, v_blk0, pos_col, pos_row, slopes, sinks, do, plan=None):
    b, s, _ = q.shape
    qs, kprev, kcur, vprev, vcur, pcol, prp, prc, smem = _swa_specs(v_blk0)
    grid = (b, N_KV_HEADS_A, s // BLOCK)
    p_ins, p_in_specs, p_outs, p_out_specs, p_scratch, semantics, at_start, at_end = _hosted(plan, 11, 4, grid, ("arbitrary",) * 3)

    def body(*refs):
        at_start(refs)
        compute(*refs[:11], *refs[11 + len(p_ins):15 + len(p_ins)])
        at_end(refs)

    def compute(q_ref, kp_ref, kc_ref, vp_ref, vc_ref, pc_ref, prp_ref, prc_ref, sl_ref, sk_ref, do_ref, dq_ref, dk_ref, dv_ref, ds_ref):
        bi, kv, n = pl.program_id(0), pl.program_id(1), pl.program_id(2)
        kk, vv, dist, valid = _swa_window(n, kp_ref, kc_ref, vp_ref, vc_ref, pc_ref, prp_ref, prc_ref)

        @pl.when((bi == 0) & (kv == 0) & (n == 0))
        def _():
            ds_ref[...] = jnp.zeros_like(ds_ref)

        @pl.when(n == 0)
        def _():
            dk_ref[...] = jnp.zeros_like(dk_ref)
            dv_ref[...] = jnp.zeros_like(dv_ref)

        dkk = jnp.zeros((2 * BLOCK, HEAD_PAD), F32)
        dvv = jnp.zeros((2 * BLOCK, HEAD_PAD), F32)
        head_row = lax.broadcasted_iota(jnp.int32, (N_HEADS_A, LANES), 0)
        dsink = jnp.zeros((N_HEADS_A, LANES), F32)
        for g in range(GROUP_A):
            hd = kv * GROUP_A + g
            lanes = slice(g * HEAD_PAD, (g + 1) * HEAD_PAD)
            qg = q_ref[:, lanes]
            p, ps = _swa_probs(qg, kk, dist, valid, sl_ref[hd], sk_ref[hd])
            dob = do_ref[:, lanes].astype(BF16)
            dvv = dvv + lax.dot_general(p.astype(BF16), dob, _TN, preferred_element_type=F32)
            dp = lax.dot_general(dob, vv, _NT, preferred_element_type=F32)
            rs = jnp.sum(p * dp, axis=-1, keepdims=True)
            dsb = (p * (dp - rs) * (HEAD_DIM_A ** -0.5)).astype(BF16)
            dq_ref[:, lanes] = jnp.dot(dsb, kk, preferred_element_type=F32)
            dkk = dkk + lax.dot_general(dsb, qg, _TN, preferred_element_type=F32)
            dsink = dsink + jnp.where(head_row == hd, -jnp.sum(ps * rs), 0.0)
        ds_ref[...] += dsink

        @pl.when(n > 0)
        def _():
            start = pl.multiple_of((n - 1) * BLOCK, BLOCK)
            dk_ref[pl.ds(start, 2 * BLOCK), :] += dkk
            dv_ref[pl.ds(start, 2 * BLOCK), :] += dvv

        @pl.when(n == 0)
        def _():
            dk_ref[0:BLOCK, :] += dkk[BLOCK:, :]
            dv_ref[0:BLOCK, :] += dvv[BLOCK:, :]

    kv_full = pl.BlockSpec((None, s, HEAD_PAD), lambda bi, kv, n: (bi, 0, kv))
    kv_shape = jax.ShapeDtypeStruct(k.shape, F32)
    res = pl.pallas_call(
        body, name=name, grid=grid,
        out_shape=(jax.ShapeDtypeStruct(q.shape, F32), kv_shape, kv_shape, jax.ShapeDtypeStruct((N_HEADS_A, LANES), F32)) + tuple(p_outs),
        in_specs=[qs, kprev, kcur, vprev, vcur, pcol, prp, prc, smem, smem, qs] + p_in_specs,
        out_specs=(qs, kv_full, kv_full, pl.BlockSpec((N_HEADS_A, LANES), lambda bi, kv, n: (0, 0))) + tuple(p_out_specs),
        scratch_shapes=p_scratch, compiler_params=_params(semantics),
    )(q, k, k, vsrc, vsrc, pos_col, pos_row, pos_row, slopes, sinks, do, *p_ins)
    return res[0], res[1], res[2], res[3], list(res[4:])


MLA_T = 512


def _hosted(plan, n_in, n_out, grid, semantics):
    if plan is None:
        return [], [], [], [], [], semantics, (lambda refs: None), (lambda refs: None)
    ni, no = len(plan.ins), len(plan.out_shapes)

    def split(refs):
        return refs[n_in:n_in + ni], refs[n_in + ni + n_out:n_in + ni + n_out + no], refs[-2], refs[-1]

    def at_start(refs):
        @pl.when(functools.reduce(jnp.logical_and, [pl.program_id(ax) == 0 for ax in range(len(grid))]))
        def _():
            plan.start(*split(refs))

    def at_end(refs):
        @pl.when(functools.reduce(jnp.logical_and, [pl.program_id(ax) == g - 1 for ax, g in enumerate(grid)]))
        def _():
            plan.finish(*split(refs))

    return plan.ins, [_HBM] * ni, plan.out_shapes, [_HBM] * no, plan.scratch(), ("arbitrary",) * len(grid), at_start, at_end


def mla_fwd(name, q, k, v, plan=None):
    b, s, w = q.shape
    h = w // HEAD_PAD
    t = MLA_T if s % MLA_T == 0 else BLOCK
    nb = s // t
    scale = QK_DIM_B ** -0.5
    p_ins, p_in_specs, p_outs, p_out_specs, p_scratch, semantics, at_start, at_end = _hosted(
        plan, 3, 3, (b, h), ("parallel", "parallel"))

    def body(*refs):
        q_ref, k_ref, v_ref = refs[:3]
        o_ref, ob_ref, lse_ref = refs[3 + len(p_ins):6 + len(p_ins)]
        at_start(refs)
        causal = lax.broadcasted_iota(jnp.int32, (t, t), 0) <= lax.broadcasted_iota(jnp.int32, (t, t), 1)
        for i in range(nb):
            q_i = q_ref[i * t:(i + 1) * t, :]

            def step(j, carry, q_i=q_i, diagonal=False):
                m, l, acc = carry
                rows = slice(j * t, (j + 1) * t) if diagonal else pl.ds(pl.multiple_of(j * t, t), t)
                st = lax.dot_general(k_ref[rows, :], q_i, _NT, preferred_element_type=F32) * scale
                if diagonal:
                    st = jnp.where(causal, st, NEG)
                m_new = jnp.maximum(m, jnp.max(st, axis=0, keepdims=True))
                a = jnp.exp(m - m_new)
                p = jnp.exp(st - m_new)
                l = a * l + jnp.sum(p, axis=0, keepdims=True)
                acc = a * acc + lax.dot_general(v_ref[rows, :], p.astype(BF16), _TN, preferred_element_type=F32)
                return m_new, l, acc

            carry = (jnp.full((1, t), NEG, F32), jnp.zeros((1, t), F32), jnp.zeros((HEAD_PAD, t), F32))
            if i > 0:
                carry = lax.fori_loop(0, i, step, carry)
            m, l, acc = step(i, carry, diagonal=True)
            o_i = (acc / l).T
            o_ref[i * t:(i + 1) * t, :] = o_i
            ob_ref[i * t:(i + 1) * t, :] = o_i.astype(BF16)
            lse_ref[i] = m + jnp.log(l)
        at_end(refs)

    hs = _head_spec(s, 0)
    ls = pl.BlockSpec((None, None, nb, 1, t), lambda bi, hi: (bi, hi, 0, 0, 0))
    res = pl.pallas_call(
        body, name=name, grid=(b, h),
        out_shape=(jax.ShapeDtypeStruct((b, s, w), F32), jax.ShapeDtypeStruct((b, s, w), BF16),
                   jax.ShapeDtypeStruct((b, h, nb, 1, t), F32)) + tuple(p_outs),
        in_specs=[hs, hs, hs] + p_in_specs, out_specs=(hs, hs, ls) + tuple(p_out_specs),
        scratch_shapes=p_scratch, compiler_params=_params(semantics),
    )(q, k, v, *p_ins)
    return res[0], res[1], res[2], list(res[3:])


def mla_bwd(name, q, k, v, o, do, lse, plan=None):
    b, s, w = q.shape
    h = w // HEAD_PAD
    _, _, nb, _, t = lse.shape
    scale = QK_DIM_B ** -0.5
    p_ins, p_in_specs, p_outs, p_out_specs, p_scratch, semantics, at_start, at_end = _hosted(
        plan, 6, 3, (b, h), ("parallel", "parallel"))

    def body(*refs):
        q_ref, k_ref, v_ref, o_ref, do_ref, lse_ref = refs[:6]
        dq_ref, dk_ref, dv_ref = refs[6 + len(p_ins):9 + len(p_ins)]
        dv_acc = refs[9 + len(p_ins) + len(p_outs)]
        at_start(refs)
        causal = lax.broadcasted_iota(jnp.int32, (t, t), 0) <= lax.broadcasted_iota(jnp.int32, (t, t), 1)
        dk_ref[...] = jnp.zeros_like(dk_ref)
        dv_acc[...] = jnp.zeros_like(dv_acc)
        for i in range(nb):
            q_i = q_ref[i * t:(i + 1) * t, :]
            do_i = do_ref[i * t:(i + 1) * t, :]
            delta = jnp.sum((o_ref[i * t:(i + 1) * t, :] * do_i).T, axis=0, keepdims=True)
            do_b = do_i.astype(BF16)
            lse_i = lse_ref[i]

            def step(j, dqt, q_i=q_i, do_b=do_b, delta=delta, lse_i=lse_i, diagonal=False):
                rows = slice(j * t, (j + 1) * t) if diagonal else pl.ds(pl.multiple_of(j * t, t), t)
                k_j = k_ref[rows, :]
                st = lax.dot_general(k_j, q_i, _NT, preferred_element_type=F32) * scale
                if diagonal:
                    st = jnp.where(causal, st, NEG)
                pt = jnp.exp(st - lse_i)
                dpt = lax.dot_general(v_ref[rows, :], do_b, _NT, preferred_element_type=F32)
                dst = (pt * (dpt - delta) * scale).astype(BF16)
                dv_acc[rows, :] += jnp.dot(pt.astype(BF16), do_b, preferred_element_type=F32)
                dk_ref[rows, :] += jnp.dot(dst, q_i, preferred_element_type=F32)
                return dqt + lax.dot_general(k_j, dst, _TN, preferred_element_type=F32)

            dqt = jnp.zeros((HEAD_PAD, t), F32)
            if i > 0:
                dqt = lax.fori_loop(0, i, step, dqt)
            dq_ref[i * t:(i + 1) * t, :] = step(i, dqt, diagonal=True).T
        dv_ref[...] = dv_acc[...].astype(BF16)
        at_end(refs)

    hs = _head_spec(s, 0)
    ls = pl.BlockSpec((None, None, nb, 1, t), lambda bi, hi: (bi, hi, 0, 0, 0))
    res = pl.pallas_call(
        body, name=name, grid=(b, h),
        out_shape=(jax.ShapeDtypeStruct((b, s, w), F32), jax.ShapeDtypeStruct((b, s, w), F32),
                   jax.ShapeDtypeStruct((b, s, w), BF16)) + tuple(p_outs),
        in_specs=[hs, hs, hs, hs, hs, ls] + p_in_specs, out_specs=(hs, hs, hs) + tuple(p_out_specs),
        scratch_shapes=[pltpu.VMEM((s, HEAD_PAD), F32)] + p_scratch,
        compiler_params=_params(semantics),
    )(q, k, v, o, do, lse, *p_ins)
    return res[0], res[1], res[2], list(res[3:])


_HBM = pl.BlockSpec(memory_space=pltpu.HBM)


def _place():
    x, y, c = lax.axis_index("x"), lax.axis_index("y"), lax.axis_index("c")
    chips = [(1 - x, y), (x, 1 - y), (1 - x, 1 - y)]
    return x, y, c, chips


def _remote(src, dst, send_sem, recv_sem, dev):
    return pltpu.make_async_remote_copy(src_ref=src, dst_ref=dst, send_sem=send_sem, recv_sem=recv_sem,
                                        device_id=dev, device_id_type=MESH)


class CommPlan:
    def __init__(self, ins, out_shapes, n_sems, start, finish, aliases=None):
        self.ins, self.out_shapes, self.n_sems = list(ins), list(out_shapes), n_sems
        self.start, self.finish, self.aliases = start, finish, dict(aliases or {})

    def scratch(self):
        return [pltpu.SemaphoreType.DMA((self.n_sems,)), pltpu.SemaphoreType.DMA((self.n_sems,))]


def comm_call(name, plan):
    ni, no = len(plan.ins), len(plan.out_shapes)

    def body(*refs):
        ins, outs, (send_sems, recv_sems) = refs[:ni], refs[ni:ni + no], refs[ni + no:]
        plan.start(ins, outs, send_sems, recv_sems)
        plan.finish(ins, outs, send_sems, recv_sems)

    return pl.pallas_call(
        body, name=name, out_shape=tuple(plan.out_shapes), in_specs=[_HBM] * ni, out_specs=tuple([_HBM] * no),
        input_output_aliases=plan.aliases, scratch_shapes=plan.scratch(),
    )(*plan.ins)


def gather_plan(packs, l):
    nk = len(packs)

    def parts(p_refs, o_refs, ss, rs):
        x, y, c, chips = _place()
        s_me = 2 * x + y
        sibling = (x, y, 1 - c)
        first = [_remote(p_refs[k].at[l], o_refs[k].at[s_me], ss.at[j * nk + k], rs.at[j * nk + k], (cx, cy, c))
                 for j, (cx, cy) in enumerate(chips) for k in range(nk)]
        own = [_remote(p_refs[k].at[l], o_refs[k].at[s_me], ss.at[6 * nk + k], rs.at[6 * nk + k], sibling) for k in range(nk)]
        return c, chips, sibling, first, own

    def start(p_refs, o_refs, ss, rs):
        c, _, _, first, own = parts(p_refs, o_refs, ss, rs)
        for cp in own:
            cp.start()

        @pl.when(c == l)
        def _():
            for cp in first:
                cp.start()

    def finish(p_refs, o_refs, ss, rs):
        c, chips, sibling, first, own = parts(p_refs, o_refs, ss, rs)

        @pl.when(c == l)
        def _():
            passed = []
            for j, (cx, cy) in enumerate(chips):
                for k in range(nk):
                    blk = o_refs[k].at[2 * cx + cy]
                    _remote(blk, blk, ss.at[j * nk + k], rs.at[j * nk + k], (cx, cy, c)).wait_recv()
                    fwd = _remote(blk, blk, ss.at[(3 + j) * nk + k], rs.at[(3 + j) * nk + k], sibling)
                    fwd.start()
                    passed.append(fwd)
            for cp in first + passed:
                cp.wait_send()

        @pl.when(c != l)
        def _():
            for j, (cx, cy) in enumerate(chips):
                for k in range(nk):
                    blk = o_refs[k].at[2 * cx + cy]
                    _remote(blk, blk, ss.at[(3 + j) * nk + k], rs.at[(3 + j) * nk + k], sibling).wait_recv()

        for cp in own:
            cp.wait()

    outs = [jax.ShapeDtypeStruct((N_SHARDS,) + p.shape[1:], p.dtype) for p in packs]
    return CommPlan(packs, outs, 7 * nk, start, finish)


def swap_plan(grads, l):
    nk = len(grads)

    def copies(g_refs, o_refs, ss, rs):
        x, y, c, _ = _place()
        return c, [_remote(g_refs[k], o_refs[k], ss.at[k], rs.at[k], (x, y, 1 - c)) for k in range(nk)]

    def start(g_refs, o_refs, ss, rs):
        c, cps = copies(g_refs, o_refs, ss, rs)

        @pl.when(c != l)
        def _():
            for cp in cps:
                cp.start()

    def finish(g_refs, o_refs, ss, rs):
        c, cps = copies(g_refs, o_refs, ss, rs)

        @pl.when(c != l)
        def _():
            for cp in cps:
                cp.wait_send()

        @pl.when(c == l)
        def _():
            for cp in cps:
                cp.wait_recv()

    return CommPlan(grads, [jax.ShapeDtypeStruct(g.shape, g.dtype) for g in grads], nk, start, finish)


def scatter_plan(parts, l):
    nk = len(parts)

    def copies(p_refs, o_refs, ss, rs):
        x, y, c, chips = _place()
        s_me = 2 * x + y
        sends = [_remote(p_refs[k].at[2 * cx + cy], o_refs[k].at[s_me], ss.at[j * nk + k], rs.at[j * nk + k], (cx, cy, c))
                 for j, (cx, cy) in enumerate(chips) for k in range(nk)]
        return c, chips, sends

    def start(p_refs, o_refs, ss, rs):
        c, _, sends = copies(p_refs, o_refs, ss, rs)

        @pl.when(c == l)
        def _():
            for cp in sends:
                cp.start()

    def finish(p_refs, o_refs, ss, rs):
        c, chips, sends = copies(p_refs, o_refs, ss, rs)

        @pl.when(c == l)
        def _():
            for j, (cx, cy) in enumerate(chips):
                for k in range(nk):
                    slot = o_refs[k].at[2 * cx + cy]
                    _remote(slot, slot, ss.at[j * nk + k], rs.at[j * nk + k], (cx, cy, c)).wait_recv()
            for cp in sends:
                cp.wait_send()

    return CommPlan(parts, [jax.ShapeDtypeStruct(p.shape, p.dtype) for p in parts], 3 * nk, start, finish)


def share_plan(finals):
    flat = [(l, f) for l, fs in enumerate(finals) for f in fs]

    def copies(f_refs, o_refs, ss, rs):
        x, y, c, _ = _place()
        return c, [_remote(f_refs[i], o_refs[i], ss.at[i], rs.at[i], (x, y, 1 - c)) for i in range(len(flat))]

    def start(f_refs, o_refs, ss, rs):
        c, cps = copies(f_refs, o_refs, ss, rs)
        for l in range(len(finals)):
            @pl.when(c == l)
            def _(l=l):
                for cp, (lf, _) in zip(cps, flat):
                    if lf == l:
                        cp.start()

    def finish(f_refs, o_refs, ss, rs):
        c, cps = copies(f_refs, o_refs, ss, rs)
        for l in range(len(finals)):
            @pl.when(c == l)
            def _(l=l):
                for cp, (lf, _) in zip(cps, flat):
                    if lf == l:
                        cp.wait_send()

            @pl.when(c != l)
            def _(l=l):
                for cp, (lf, _) in zip(cps, flat):
                    if lf == l:
                        cp.wait_recv()

    arrays = [f for _, f in flat]
    return CommPlan(arrays, [jax.ShapeDtypeStruct(f.shape, f.dtype) for f in arrays], len(flat), start, finish,
                    aliases={i: i for i in range(len(flat))})


def add_pair(name, own, other, tr=1024):
    n, cols = own.shape
    tr = _tile(n, tr, 16)

    def body(g_ref, o_ref, out_ref):
        out_ref[...] = (g_ref[...].astype(F32) + o_ref[...].astype(F32)).astype(out_ref.dtype)

    row = pl.BlockSpec((tr, cols), lambda i: (i, 0))
    return pl.pallas_call(
        body, name=name, out_shape=jax.ShapeDtypeStruct((n, cols), BF16), grid=(n // tr,), in_specs=[row, row], out_specs=row,
        compiler_params=_params(("parallel",)),
    )(own, other)


def sum_slots(name, slots, part, shard, tr=512):
    ns, r, cols = slots.shape
    tr = _tile(r, tr, 16)

    def body(s_idx, s_ref, own_ref, o_ref):
        total = own_ref[...].astype(F32)
        for k in range(1, ns):
            total = total + s_ref[(s_idx[0] + k) % ns].astype(F32)
        o_ref[...] = total

    return pl.pallas_call(
        body, name=name, out_shape=jax.ShapeDtypeStruct((r, cols), F32),
        grid_spec=pltpu.PrefetchScalarGridSpec(
            num_scalar_prefetch=1, grid=(r // tr,),
            in_specs=[pl.BlockSpec((ns, tr, cols), lambda i, p: (0, i, 0)), pl.BlockSpec((None, tr, cols), lambda i, p: (p[0], i, 0))],
            out_specs=pl.BlockSpec((tr, cols), lambda i, p: (i, 0))),
        compiler_params=_params(("parallel",)),
    )(shard, slots, part)


def small_allreduce(vec):
    r, cols = vec.shape

    def body(v_ref, o_ref, buf, send_sems, recv_sems):
        x, y, c, _ = _place()
        me = 4 * x + 2 * y + c
        buf[me] = v_ref[...]
        peers = []
        for k in range(1, N_DEV):
            px = (1 - x) if (k & 4) else x
            py = (1 - y) if (k & 2) else y
            pc = (1 - c) if (k & 1) else c
            peers.append((px, py, pc))
        sends = [_remote(buf.at[me], buf.at[me], send_sems.at[k], recv_sems.at[k], peer) for k, peer in enumerate(peers)]
        for cp in sends:
            cp.start()
        for k, (px, py, pc) in enumerate(peers):
            slot = buf.at[4 * px + 2 * py + pc]
            _remote(slot, slot, send_sems.at[k], recv_sems.at[k], (px, py, pc)).wait_recv()
        for cp in sends:
            cp.wait_send()
        total = buf[0]
        for k in range(1, N_DEV):
            total = total + buf[k]
        o_ref[...] = total

    return pl.pallas_call(
        body, name="small_allreduce", out_shape=jax.ShapeDtypeStruct((r, cols), F32),
        in_specs=[pl.BlockSpec(memory_space=pltpu.VMEM)], out_specs=pl.BlockSpec(memory_space=pltpu.VMEM),
        scratch_shapes=[pltpu.VMEM((N_DEV, r, cols), F32), pltpu.SemaphoreType.DMA((N_DEV - 1,)), pltpu.SemaphoreType.DMA((N_DEV - 1,))],
    )(vec)


ARG_NAMES = (("x", "positions") + WEIGHT_NAMES + ("loss_target",) + tuple("m_" + n for n in WEIGHT_NAMES)
             + tuple("v_" + n for n in WEIGHT_NAMES))

_C_KA = N_HEADS_A * HEAD_PAD
_C_VA = _C_KA + N_KV_HEADS_A * HEAD_PAD
_C_CQ = _C_VA + N_KV_HEADS_A * HEAD_PAD
_C_CKV = _C_CQ + Q_LORA_RANK
_C_KR = _C_CKV + KV_LORA_RANK
_C_GA = _C_KR + HEAD_PAD
_R_KA = WIDTH_A
_R_VA = _R_KA + KV_WIDTH_A
_R_CQ = _R_VA + KV_WIDTH_A
_R_KR = _R_CQ + Q_LORA_RANK + KV_LORA_RANK
_R_GA = _R_KR + QK_ROPE_DIM


def _pad_head_cols(w, n_heads):
    r, dh = w.shape[0], w.shape[1] // n_heads
    return jnp.pad(w.reshape(r, n_heads, dh), ((0, 0), (0, 0), (0, HEAD_PAD - dh))).reshape(r, n_heads * HEAD_PAD)


def _unpad_head_cols(w, n_heads, dh):
    r = w.shape[0]
    return w.reshape(r, n_heads, HEAD_PAD)[:, :, :dh].reshape(r, n_heads * dh)


def _pad_head_rows(w, n_heads):
    dh, c = w.shape[0] // n_heads, w.shape[1]
    return jnp.pad(w.reshape(n_heads, dh, c), ((0, 0), (0, HEAD_PAD - dh), (0, 0))).reshape(n_heads * HEAD_PAD, c)


def _unpad_head_rows(w, n_heads, dh):
    c = w.shape[1]
    return w.reshape(n_heads, HEAD_PAD, c)[:, :dh].reshape(n_heads * dh, c)


def _join_cols(shards):
    ns, r, c = shards.shape
    return shards.transpose(1, 0, 2).reshape(r, ns * c)


def _split_cols(mat):
    r, c4 = mat.shape
    return mat.reshape(r, N_SHARDS, c4 // N_SHARDS).transpose(1, 0, 2)


def _ffn_fwd(tag, x, gain, wcol, wdown, plan=None):
    nb = rms_fwd(tag + "_rms", x, gain)
    gu, h, carried = ffn_up_act(tag + "_up", nb, wcol, 0, plan=plan)
    out = gmm_down(tag + "_down", h, wdown, 0, x, 0.5)
    return out, (x, nb, gu, h), carried


def _ffn_bwd(tag, saved, gain, wcol, wdown, dout, doutb, plans=None):
    plans = plans or {}
    x, nb, gu, h = saved
    carried = {}
    dgu, carried["dh"] = ffn_down_dact(tag + "_dh", doutb, wdown, gu, gu.shape[-1], 0, 0.5, plan=plans.get("dh"))
    gdown = gmm_down_dw(tag + "_dwd", h, doutb, 0.5)
    if plans.get("dwgu") is not None:
        gcol, carried["dwgu"] = gmm_up_dw(tag + "_dwgu", nb, dgu, plan=plans["dwgu"])
    else:
        gcol = gmm_up_dw(tag + "_dwgu", nb, dgu)
    if plans.get("dn") is not None:
        dn, carried["dn"] = gmm_up_dx(tag + "_dn", dgu, wcol, x.shape[1], 0, plan=plans["dn"])
    else:
        dn = gmm_up_dx(tag + "_dn", dgu, wcol, x.shape[1], 0)
    dx, dxb, dgain = rms_bwd(tag + "_drms", x, gain, dn, dres=dout)
    return dx, dxb, dgain, gcol, gdown, carried


def _pad_gain(g):
    return jnp.pad(g, ((0, 0), (0, HEAD_PAD - g.shape[1])))


def _mixer_fwd(tag, x1, sm, w, wo, aux, plans=None):
    plans = plans or {}
    b, s = aux["b"], aux["s"]
    t, d = x1.shape
    hb = rms_fwd(tag + "_rms", x1, sm["mix_norm"])
    carried = {}
    if plans.get("proj") is not None:
        proj, carried["proj"] = mm_nn(tag + "_proj", hb, w["win"], tm=1024, tn=2048, tk=1024, plan=plans["proj"])
    else:
        proj = mm_nn(tag + "_proj", hb, w["win"], tm=1024, tn=2048, tk=1024)
    proj3 = proj.reshape(b, s, proj.shape[1])
    gains = {n: _pad_gain(sm[n]) for n in ("swa_q_norm", "swa_k_norm", "mla_q_norm", "mla_k_norm")}
    cqn = rms_fwd(tag + "_rms_cq", proj, sm["mla_q_lora_norm"], tm=1024, col_blk=_C_CQ // Q_LORA_RANK)
    ckvn = rms_fwd(tag + "_rms_ckv", proj, sm["mla_kv_lora_norm"], tm=1024, col_blk=_C_CKV // KV_LORA_RANK)
    qb_raw = mm_nn(tag + "_uq", cqn, w["wuq"], tm=1024).reshape(b, s, N_HEADS_B * HEAD_PAD)
    kb_raw = mm_nn(tag + "_uk", ckvn, w["wk"], tm=1024).reshape(b, s, N_HEADS_B * HEAD_PAD)
    vb = mm_nn(tag + "_uv", ckvn, w["wv"], tm=1024, out_dtype=BF16).reshape(b, s, N_HEADS_B * HEAD_PAD)
    qah = qk_prep_fwd(tag + "_qa_norm", proj3, 0, N_HEADS_A, HEAD_DIM_A, gains["swa_q_norm"])
    kah = qk_prep_fwd(tag + "_ka_norm", proj3, _C_KA // HEAD_PAD, N_KV_HEADS_A, HEAD_DIM_A, gains["swa_k_norm"])
    sinks = sm["swa_sinks"].reshape(-1)
    oab, carried["swa"] = swa_fwd(tag + "_swa", qah, kah, proj3, _C_VA // HEAD_PAD, aux["pos_col"], aux["pos_row"], aux["slopes"],
                                  sinks, plan=plans.get("swa"))
    kr = (proj3, _C_KR // HEAD_PAD)
    qbh = qk_prep_fwd(tag + "_qb_norm", qb_raw, 0, N_HEADS_B, QK_DIM_B, gains["mla_q_norm"], aux["rope"])
    kbh = qk_prep_fwd(tag + "_kb_norm", kb_raw, 0, N_HEADS_B, QK_DIM_B, gains["mla_k_norm"], aux["rope"], extra=kr)
    ob, obb, lse, carried["mla"] = mla_fwd(tag + "_mla", qbh, kbh, vb, plans.get("mla"))
    oab2, obb2 = oab.reshape(t, -1), obb.reshape(t, -1)
    ya = mm_nn(tag + "_branch_a", oab2, w["wa"], tm=1024)
    yb = mm_nn(tag + "_branch_b", obb2, w["wb"], tm=1024)
    mg = gate_fwd(tag + "_gate", proj, ya, yb, _C_GA, _C_GA + d)
    x2 = gmm_rows(tag + "_out", mg, wo, x1)
    saved = dict(x1=x1, hb=hb, proj=proj, cqn=cqn, ckvn=ckvn, qb_raw=qb_raw, kb_raw=kb_raw, vb=vb, qah=qah, kah=kah,
                 qbh=qbh, kbh=kbh, ob=ob, lse=lse, oab2=oab2, obb2=obb2, ya=ya, yb=yb, mg=mg, gains=gains)
    return x2, saved, carried


def _mixer_bwd(tag, sv, sm, w, wo, aux, dx2, dx2b, plans=None):
    plans = plans or {}
    carried = {}
    b, s = aux["b"], aux["s"]
    t, d = dx2.shape
    proj, gains = sv["proj"], sv["gains"]
    proj3 = proj.reshape(b, s, proj.shape[1])
    kr = (proj3, _C_KR // HEAD_PAD)
    dmg = gmm_rows_dx(tag + "_d_out", dx2b, wo)
    dwo = gmm_rows_dw(tag + "_dw_out", sv["mg"], dx2b)
    dya, dyb, dga, dgb = gate_bwd(tag + "_dgate", proj, sv["ya"], sv["yb"], dmg, _C_GA, _C_GA + d)
    dwa = mm_tn(tag + "_dw_branch_a", sv["oab2"], dya, out_dtype=BF16)
    dwb = mm_tn(tag + "_dw_branch_b", sv["obb2"], dyb, out_dtype=BF16)
    doa = mm_nt(tag + "_d_branch_a", dya, w["wa"], tm=1024).reshape(b, s, -1)
    dob = mm_nt(tag + "_d_branch_b", dyb, w["wb"], tm=1024).reshape(b, s, -1)
    sinks = sm["swa_sinks"].reshape(-1)
    dqah, dkah, dva, dsinks, carried["dswa"] = swa_bwd(tag + "_dswa", sv["qah"], sv["kah"], proj3, _C_VA // HEAD_PAD, aux["pos_col"],
                                                       aux["pos_row"], aux["slopes"], sinks, doa, plan=plans.get("dswa"))
    dqa_raw, dg_swa_q = qk_prep_bwd(tag + "_dqa_norm", proj3, 0, N_HEADS_A, HEAD_DIM_A, gains["swa_q_norm"], dqah)
    dka_raw, dg_swa_k = qk_prep_bwd(tag + "_dka_norm", proj3, _C_KA // HEAD_PAD, N_KV_HEADS_A, HEAD_DIM_A, gains["swa_k_norm"], dkah)
    dqbh, dkbh, dvb, carried["dmla"] = mla_bwd(tag + "_dmla", sv["qbh"], sv["kbh"], sv["vb"], sv["ob"], dob, sv["lse"], plans.get("dmla"))
    dqb_raw, dg_mla_q = qk_prep_bwd(tag + "_dqb_norm", sv["qb_raw"], 0, N_HEADS_B, QK_DIM_B, gains["mla_q_norm"], dqbh, aux["rope"])
    dkb_raw, dkr_sum, dg_mla_k = qk_prep_bwd(tag + "_dkb_norm", sv["kb_raw"], 0, N_HEADS_B, QK_DIM_B, gains["mla_k_norm"], dkbh,
                                             aux["rope"], extra=kr, head_sum=True)
    dq_tok, dk_tok, dv_tok = dqb_raw.reshape(t, -1), dkb_raw.reshape(t, -1), dvb.reshape(t, -1)
    dwuq = mm_tn(tag + "_dw_uq", sv["cqn"], dq_tok, tk=1024, out_dtype=BF16)
    dwk = mm_tn(tag + "_dw_uk", sv["ckvn"], dk_tok, tk=1024, out_dtype=BF16)
    dwv = mm_tn(tag + "_dw_uv", sv["ckvn"], dv_tok, tk=1024, out_dtype=BF16)
    dcqn = mm_nt(tag + "_d_uq", dq_tok, w["wuq"], tm=1024)
    dckvn = mm_nt(tag + "_d_uv", dv_tok, w["wv"], tm=1024, res=mm_nt(tag + "_d_uk", dk_tok, w["wk"], tm=1024))
    dcq, dg_q_lora = rms_bwd(tag + "_drms_cq", proj, sm["mla_q_lora_norm"], dcqn, want_f32=False, tm=1024, col_blk=_C_CQ // Q_LORA_RANK)
    dckv, dg_kv_lora = rms_bwd(tag + "_drms_ckv", proj, sm["mla_kv_lora_norm"], dckvn, want_f32=False, tm=1024,
                               col_blk=_C_CKV // KV_LORA_RANK)
    dproj = jnp.concatenate([dqa_raw.reshape(t, -1), dka_raw.reshape(t, -1), dva.reshape(t, -1).astype(BF16), dcq, dckv,
                             dkr_sum.reshape(t, HEAD_PAD).astype(BF16), dga, dgb], axis=1)
    dwin = mm_tn(tag + "_dw_in", sv["hb"], dproj, tm=1024, tn=2048, tk=1024, out_dtype=BF16)
    dh = mm_nt(tag + "_d_in", dproj, w["win"], tm=1024, tn=1024, tk=2048)
    dx1, dx1b, dg_mix = rms_bwd(tag + "_drms", sv["x1"], sm["mix_norm"], dh, dres=dx2)
    wgrads = dict(win=dwin, wuq=dwuq, wk=dwk, wv=dwv, wa=dwa, wb=dwb, wo=dwo)
    sgrads = dict(mix_norm=dg_mix, swa_q_norm=dg_swa_q[:, :HEAD_DIM_A], swa_k_norm=dg_swa_k[:, :HEAD_DIM_A],
                  swa_sinks=dsinks[:, 0].reshape(1, -1), mla_q_lora_norm=dg_q_lora, mla_kv_lora_norm=dg_kv_lora,
                  mla_q_norm=dg_mla_q[:, :QK_DIM_B], mla_k_norm=dg_mla_k[:, :QK_DIM_B])
    return dx1, dx1b, wgrads, sgrads, carried


def _layer_weights(win4, uq4, ukv4, branch4):
    wr = _join_cols(win4)
    kr = jnp.pad(wr[:, _R_KR:_R_GA], ((0, 0), (QK_NOPE_DIM, HEAD_PAD - QK_DIM_B)))
    win = jnp.concatenate([_pad_head_cols(wr[:, :_R_KA], N_HEADS_A), _pad_head_cols(wr[:, _R_KA:_R_VA], N_KV_HEADS_A),
                           _pad_head_cols(wr[:, _R_VA:_R_CQ], N_KV_HEADS_A), wr[:, _R_CQ:_R_KR], kr, wr[:, _R_GA:]], axis=1)
    ukv = _join_cols(ukv4)
    ukv3 = ukv.reshape(ukv.shape[0], N_HEADS_B, QK_NOPE_DIM + V_DIM_B)
    r = branch4.shape[1] // 2
    return dict(win=win, wuq=_pad_head_cols(_join_cols(uq4), N_HEADS_B),
                wk=_pad_head_cols(ukv3[:, :, :QK_NOPE_DIM].reshape(ukv.shape[0], -1), N_HEADS_B),
                wv=_pad_head_cols(ukv3[:, :, QK_NOPE_DIM:].reshape(ukv.shape[0], -1), N_HEADS_B),
                wa=_pad_head_rows(_join_cols(branch4[:, :r]), N_HEADS_A), wb=_pad_head_rows(_join_cols(branch4[:, r:]), N_HEADS_B))


def _mixer_grad_shards(g):
    dw = g["win"]
    win_ref = jnp.concatenate([_unpad_head_cols(dw[:, :_C_KA], N_HEADS_A, HEAD_DIM_A),
                               _unpad_head_cols(dw[:, _C_KA:_C_VA], N_KV_HEADS_A, HEAD_DIM_A),
                               _unpad_head_cols(dw[:, _C_VA:_C_CQ], N_KV_HEADS_A, HEAD_DIM_A), dw[:, _C_CQ:_C_KR],
                               dw[:, _C_KR + QK_NOPE_DIM:_C_KR + QK_DIM_B], dw[:, _C_GA:]], axis=1)
    rk = g["wk"].shape[0]
    ukv = jnp.concatenate([g["wk"].reshape(rk, N_HEADS_B, HEAD_PAD)[:, :, :QK_NOPE_DIM],
                           g["wv"].reshape(rk, N_HEADS_B, HEAD_PAD)[:, :, :V_DIM_B]], axis=2).reshape(rk, -1)
    return (_split_cols(win_ref), _split_cols(_unpad_head_cols(g["wuq"], N_HEADS_B, QK_DIM_B)), _split_cols(ukv),
            jnp.concatenate([_split_cols(_unpad_head_rows(g["wa"], N_HEADS_A, HEAD_DIM_A)),
                             _split_cols(_unpad_head_rows(g["wb"], N_HEADS_B, V_DIM_B))], axis=1))


BUFFER_TAGS = ("col1", "col2", "down1", "down2", "wo", "win", "uq", "ukv", "branch")
FIRST_GATHER = ("col1", "down1")
GATHER_CARRIERS = {(0, "ffn1_up"): (0, ("wo", "win", "uq", "ukv", "branch")), (0, "proj"): (1, ("col1",)),
                   (0, "swa"): (1, ("down1", "wo", "uq", "ukv")), (0, "mla"): (0, ("col2", "down2")),
                   (0, "ffn2_up"): (1, ("win", "branch")), (1, "ffn1_up"): (1, ("col2",)), (1, "swa"): (1, ("down2",))}
SCATTER_CARRIERS = {(0, "ffn2_dh"): (1, ("col1",)), (0, "ffn2_dwgu"): (1, ("col2",)), (0, "ffn2_dn"): (1, ("down1", "down2", "wo")),
                    (0, "dswa"): (1, ("win", "uq", "ukv", "branch")), (0, "dmla"): (0, ("col2", "down2")),
                    (0, "ffn1_dh"): (0, ("wo", "win", "uq", "ukv", "branch"))}
LAST_SCATTER = ("col1", "down1")


def _pick(by_tag, tags):
    return [by_tag[t] for t in tags]


def _train_step(x, positions, target, packs, small, shard):
    depth = 2
    b, s, d = x.shape
    t = b * s
    posf = positions.astype(F32)
    half = QK_ROPE_DIM // 2
    inv_freq = ROPE_BASE ** (-jnp.arange(half, dtype=F32) / half)
    ang = posf[..., None] * inv_freq
    cos, sin = jnp.cos(ang), jnp.sin(ang)
    tail = HEAD_PAD - QK_DIM_B
    rope = (jnp.concatenate([jnp.ones((b, s, QK_NOPE_DIM), F32), cos, cos, jnp.ones((b, s, tail), F32)], axis=-1),
            jnp.concatenate([jnp.zeros((b, s, QK_NOPE_DIM), F32), sin, sin, jnp.zeros((b, s, tail), F32)], axis=-1))
    slopes = jnp.exp2(-8.0 * (jnp.arange(N_HEADS_A, dtype=F32) + 1.0) / N_HEADS_A)
    aux = dict(b=b, s=s, pos_col=posf.reshape(b, s, 1), pos_row=posf.reshape(b, 1, s), rope=rope, slopes=slopes)

    def sm_of(l):
        return {n: small[n][l:l + 1] for n in SMALL_NAMES}

    by_tag = dict(zip(BUFFER_TAGS, packs))
    wts, parts, slots = [{} for _ in range(depth)], [{} for _ in range(depth)], [{} for _ in range(depth)]
    saved, sgrads = [None] * depth, [None] * depth

    def gather_piece(l, kern):
        if (l, kern) not in GATHER_CARRIERS:
            return None
        data_l, tags = GATHER_CARRIERS[(l, kern)]
        return gather_plan(_pick(by_tag, tags), data_l)

    def gathered(l, kern, arrays):
        if (l, kern) in GATHER_CARRIERS:
            data_l, tags = GATHER_CARRIERS[(l, kern)]
            wts[data_l].update(zip(tags, arrays))

    def scatter_piece(l, kern):
        if (l, kern) not in SCATTER_CARRIERS:
            return None
        data_l, tags = SCATTER_CARRIERS[(l, kern)]
        return scatter_plan(_pick(parts[data_l], tags), data_l)

    def scattered(l, kern, arrays):
        if (l, kern) in SCATTER_CARRIERS:
            data_l, tags = SCATTER_CARRIERS[(l, kern)]
            slots[data_l].update(zip(tags, arrays))

    def chip_sums(l, tags, mine):
        theirs = comm_call(f"swap_l{l}_{tags[0]}", swap_plan(_pick(mine, tags), l))
        for tag, o in zip(tags, theirs):
            g = mine[tag]
            parts[l][tag] = add_pair(f"l{l}_add_{tag}", g.reshape(-1, g.shape[-1]), o.reshape(-1, o.shape[-1])).reshape(g.shape)

    wts[0].update(zip(FIRST_GATHER, comm_call("gather_first", gather_plan(_pick(by_tag, FIRST_GATHER), 0))))
    h = x.reshape(t, d)
    for l in range(depth):
        sm, wl = sm_of(l), wts[l]
        h, s1, got = _ffn_fwd(f"l{l}_ffn1", h, sm["ffn1_norm"], wl["col1"], wl["down1"], gather_piece(l, "ffn1_up"))
        gathered(l, "ffn1_up", got)
        weights = _layer_weights(wl["win"], wl["uq"], wl["ukv"], wl["branch"])
        h, s2, got = _mixer_fwd(f"l{l}_mix", h, sm, weights, wl["wo"], aux, {k: gather_piece(l, k) for k in ("proj", "swa", "mla")})
        for kern, arrays in got.items():
            gathered(l, kern, arrays)
        h, s3, got = _ffn_fwd(f"l{l}_ffn2", h, sm["ffn2_norm"], wl["col2"], wl["down2"], gather_piece(l, "ffn2_up"))
        gathered(l, "ffn2_up", got)
        saved[l] = (s1, s2, s3, weights)
    loss, dh, dhb = loss_fwd_bwd("loss", h, target.reshape(t, d))

    for l in reversed(range(depth)):
        sm, wl = sm_of(l), wts[l]
        s1, s2, s3, weights = saved[l]
        staged = l == 0
        dh, dhb, dg_f2, gcol2, gdown2, got = _ffn_bwd(f"l{l}_ffn2", s3, sm["ffn2_norm"], wl["col2"], wl["down2"], dh, dhb,
                                                      {k: scatter_piece(l, "ffn2_" + k) for k in ("dh", "dwgu", "dn")})
        for kern, arrays in got.items():
            scattered(l, "ffn2_" + kern, arrays)
        mine = dict(col2=gcol2, down2=gdown2)
        if staged:
            chip_sums(l, ("col2", "down2"), mine)
        dh, dhb, wg, sg, got = _mixer_bwd(f"l{l}_mix", s2, sm, weights, wl["wo"], aux, dh, dhb,
                                          {k: scatter_piece(l, k) for k in ("dswa", "dmla")})
        for kern, arrays in got.items():
            scattered(l, kern, arrays)
        gwin, guq, gukv, gbranch = _mixer_grad_shards(wg)
        mine.update(wo=wg["wo"], win=gwin, uq=guq, ukv=gukv, branch=gbranch)
        if staged:
            chip_sums(l, ("wo", "win", "uq", "ukv", "branch"), mine)
        dh, dhb, dg_f1, gcol1, gdown1, got = _ffn_bwd(f"l{l}_ffn1", s1, sm["ffn1_norm"], wl["col1"], wl["down1"], dh, dhb,
                                                      {k: scatter_piece(l, "ffn1_" + k) for k in ("dh", "dwgu", "dn")})
        for kern, arrays in got.items():
            scattered(l, "ffn1_" + kern, arrays)
        mine.update(col1=gcol1, down1=gdown1)
        chip_sums(l, LAST_SCATTER if staged else BUFFER_TAGS, mine)
        sg.update(ffn1_norm=dg_f1, ffn2_norm=dg_f2)
        sgrads[l] = sg
    slots[0].update(zip(LAST_SCATTER, comm_call("scatter_last", scatter_plan(_pick(parts[0], LAST_SCATTER), 0))))
    finals = [[sum_slots(f"l{l}_sum_{tag}", slots[l][tag], parts[l][tag], shard) for tag in BUFFER_TAGS] for l in range(depth)]
    shared = comm_call("share", share_plan(finals))
    nk = len(BUFFER_TAGS)
    return loss, dh.reshape(b, s, d), [dict(zip(BUFFER_TAGS, shared[l * nk:(l + 1) * nk])) for l in range(depth)], sgrads


def kernel(x, positions, ffn1_norm, ffn1_w_gate, ffn1_w_up, ffn1_w_down, mix_norm, w_in, swa_q_norm, swa_k_norm, swa_sinks, mla_q_lora_norm, mla_w_uq, mla_kv_lora_norm, mla_w_ukv, mla_q_norm, mla_k_norm, w_branch_a, w_branch_b, w_out, ffn2_norm, ffn2_w_gate, ffn2_w_up, ffn2_w_down, loss_target, m_ffn1_norm, m_ffn1_w_gate, m_ffn1_w_up, m_ffn1_w_down, m_mix_norm, m_w_in, m_swa_q_norm, m_swa_k_norm, m_swa_sinks, m_mla_q_lora_norm, m_mla_w_uq, m_mla_kv_lora_norm, m_mla_w_ukv, m_mla_q_norm, m_mla_k_norm, m_w_branch_a, m_w_branch_b, m_w_out, m_ffn2_norm, m_ffn2_w_gate, m_ffn2_w_up, m_ffn2_w_down, v_ffn1_norm, v_ffn1_w_gate, v_ffn1_w_up, v_ffn1_w_down, v_mix_norm, v_w_in, v_swa_q_norm, v_swa_k_norm, v_swa_sinks, v_mla_q_lora_norm, v_mla_w_uq, v_mla_kv_lora_norm, v_mla_w_ukv, v_mla_q_norm, v_mla_k_norm, v_w_branch_a, v_w_branch_b, v_w_out, v_ffn2_norm, v_ffn2_w_gate, v_ffn2_w_up, v_ffn2_w_down):
    args = (x, positions, ffn1_norm, ffn1_w_gate, ffn1_w_up, ffn1_w_down, mix_norm, w_in, swa_q_norm, swa_k_norm, swa_sinks, mla_q_lora_norm, mla_w_uq, mla_kv_lora_norm, mla_w_ukv, mla_q_norm, mla_k_norm, w_branch_a, w_branch_b, w_out, ffn2_norm, ffn2_w_gate, ffn2_w_up, ffn2_w_down, loss_target, m_ffn1_norm, m_ffn1_w_gate, m_ffn1_w_up, m_ffn1_w_down, m_mix_norm, m_w_in, m_swa_q_norm, m_swa_k_norm, m_swa_sinks, m_mla_q_lora_norm, m_mla_w_uq, m_mla_kv_lora_norm, m_mla_w_ukv, m_mla_q_norm, m_mla_k_norm, m_w_branch_a, m_w_branch_b, m_w_out, m_ffn2_norm, m_ffn2_w_gate, m_ffn2_w_up, m_ffn2_w_down, v_ffn1_norm, v_ffn1_w_gate, v_ffn1_w_up, v_ffn1_w_down, v_mix_norm, v_w_in, v_swa_q_norm, v_swa_k_norm, v_swa_sinks, v_mla_q_lora_norm, v_mla_w_uq, v_mla_kv_lora_norm, v_mla_w_ukv, v_mla_q_norm, v_mla_k_norm, v_w_branch_a, v_w_branch_b, v_w_out, v_ffn2_norm, v_ffn2_w_gate, v_ffn2_w_up, v_ffn2_w_down)
    a = dict(zip(ARG_NAMES, args, strict=True))
    x = a["x"]
    depth = a["ffn1_norm"].shape[0]
    d = x.shape[-1]
    assert depth == 2, "the exchanges of layer l are driven by core l of every chip: two layers, two cores"

    def bf16_rows(*names):
        return jnp.concatenate([a[n] for n in names], axis=1).astype(BF16) if len(names) > 1 else a[names[0]].astype(BF16)

    packs = [bf16_rows("ffn1_w_gate", "ffn1_w_up"), bf16_rows("ffn2_w_gate", "ffn2_w_up"), bf16_rows("ffn1_w_down"),
             bf16_rows("ffn2_w_down"), bf16_rows("w_out"), bf16_rows("w_in"), bf16_rows("mla_w_uq"), bf16_rows("mla_w_ukv"),
             bf16_rows("w_branch_a", "w_branch_b")]
    shard_me = (2 * lax.axis_index("x") + lax.axis_index("y")).astype(jnp.int32).reshape(1)
    small = {n: a[n] for n in SMALL_NAMES}

    loss, grad_x, summed, sgrads = _train_step(x, a["positions"], a["loss_target"], packs, small, shard_me)

    full = {tag: jnp.stack([summed[l][tag] for l in range(depth)]) for tag in BUFFER_TAGS}
    r_branch = a["w_branch_a"].shape[1]
    grads = dict(ffn1_w_gate=full["col1"][:, :d], ffn1_w_up=full["col1"][:, d:], ffn2_w_gate=full["col2"][:, :d],
                 ffn2_w_up=full["col2"][:, d:], ffn1_w_down=full["down1"], ffn2_w_down=full["down2"], w_out=full["wo"],
                 w_in=full["win"], mla_w_uq=full["uq"], mla_w_ukv=full["ukv"], w_branch_a=full["branch"][:, :r_branch],
                 w_branch_b=full["branch"][:, r_branch:])

    flat = jnp.concatenate([jnp.concatenate([sgrads[l][n].reshape(-1) for l in range(depth)]) for n in SMALL_NAMES] + [loss.reshape(-1)])
    n_small = flat.shape[0]
    rows = -(-n_small // (8 * LANES)) * 8
    pad = rows * LANES - n_small

    def small_pack(v):
        return jnp.pad(v, (0, pad)).reshape(rows, LANES)

    total = small_allreduce(small_pack(flat))
    w_s, m_s, v_s = (small_pack(jnp.concatenate([a[p + n].reshape(-1) for n in SMALL_NAMES] + [jnp.zeros((1,), F32)]))
                     for p in ("", "m_", "v_"))
    d_s, nm_s, nv_s = adamw("adamw_small", w_s, total, m_s, v_s)

    def small_unpack(buf):
        out, off, flat_b = {}, 0, buf.reshape(-1)
        for n in SMALL_NAMES:
            size = a[n].shape[0] * a[n].shape[1]
            out[n] = flat_b[off:off + size].reshape(a[n].shape)
            off += size
        return out

    grads.update(small_unpack(total))
    delta, new_m, new_v = small_unpack(d_s), small_unpack(nm_s), small_unpack(nv_s)
    for n in PACK_NAMES:
        flip = a[n].shape[2] % LANES != 0

        def view(z):
            return z.transpose(0, 2, 1) if flip else z

        w_v, g_v, m_v, v_v = view(a[n]), view(grads[n]), view(a["m_" + n]), view(a["v_" + n])
        shp = w_v.shape
        two_d = (shp[0] * shp[1], shp[2])
        dn, mn, vn = adamw("adamw_" + n, w_v.reshape(two_d), g_v.reshape(two_d), m_v.reshape(two_d), v_v.reshape(two_d))
        delta[n], new_m[n], new_v[n] = view(dn.reshape(shp)), view(mn.reshape(shp)), view(vn.reshape(shp))
        if flip:
            grads[n] = view(g_v)

    loss_out = total.reshape(-1)[n_small - 1]
    return (loss_out, grad_x, *[grads[n] for n in WEIGHT_NAMES], *[delta[n] for n in WEIGHT_NAMES],
            *[new_m[n] for n in WEIGHT_NAMES], *[new_v[n] for n in WEIGHT_NAMES])
```

```python
import functools

import numpy as np
import jax
import jax.numpy as jnp
from jax import lax
from jax.experimental import pallas as pl
from jax.experimental.pallas import tpu as pltpu

F32 = jnp.float32
BF16 = jnp.bfloat16
MESH = pl.DeviceIdType.MESH

HEAD_DIM_A = 64
N_HEADS_A = 8
N_KV_HEADS_A = 2
GROUP_A = N_HEADS_A // N_KV_HEADS_A
BLOCK = 128
N_HEADS_B = 8
Q_LORA_RANK = 256
KV_LORA_RANK = 128
QK_NOPE_DIM = 64
QK_ROPE_DIM = 32
QK_DIM_B = QK_NOPE_DIM + QK_ROPE_DIM
V_DIM_B = 64
ROPE_BASE = 10000.0
WIDTH_A = N_HEADS_A * HEAD_DIM_A
KV_WIDTH_A = N_KV_HEADS_A * HEAD_DIM_A
EPS = 1e-6
NEG = -1e30
ADAM_LR = 0.001
ADAM_B1 = 0.9
ADAM_B2 = 0.999
ADAM_EPS = 1e-08
ADAM_WD = 0.01
ADAM_STEP = 10

N_SHARDS = 4
N_DEV = 8
LANES = 128
VMEM_LIMIT = 48 * 1024 * 1024

PACK_NAMES = ("ffn1_w_gate", "ffn1_w_up", "ffn1_w_down", "w_in", "mla_w_uq", "mla_w_ukv",
              "w_branch_a", "w_branch_b", "w_out", "ffn2_w_gate", "ffn2_w_up", "ffn2_w_down")
SMALL_NAMES = ("ffn1_norm", "mix_norm", "swa_q_norm", "swa_k_norm", "swa_sinks", "mla_q_lora_norm",
               "mla_kv_lora_norm", "mla_q_norm", "mla_k_norm", "ffn2_norm")
WEIGHT_NAMES = ("ffn1_norm", "ffn1_w_gate", "ffn1_w_up", "ffn1_w_down", "mix_norm", "w_in", "swa_q_norm",
                "swa_k_norm", "swa_sinks", "mla_q_lora_norm", "mla_w_uq", "mla_kv_lora_norm", "mla_w_ukv",
                "mla_q_norm", "mla_k_norm", "w_branch_a", "w_branch_b", "w_out", "ffn2_norm", "ffn2_w_gate",
                "ffn2_w_up", "ffn2_w_down")


def _params(sem):
    return pltpu.CompilerParams(dimension_semantics=sem, vmem_limit_bytes=VMEM_LIMIT)


def _tile(n, want, align):
    if n <= want:
        return n
    t = (want // align) * align
    while t > align and n % t:
        t -= align
    assert t >= align and n % t == 0, (n, want, align)
    return t


def _mm_call(name, a, b, out_struct, grid, a_spec, b_spec, o_spec, dims, n_red, acc_shape, alpha=1.0, res=None,
             inner=0, a_cols=0, plan=None):
    if n_red and all(g == 1 for g in grid[len(grid) - n_red:]):
        n_red = 0
    n_par = len(grid) - n_red
    n_in = 2 + int(res is not None)
    p_ins, p_in_specs, p_outs, p_out_specs, p_scratch, semantics, at_start, at_end = _hosted(
        plan, n_in, 1, grid, ("parallel",) * n_par + ("arbitrary",) * n_red)

    def body(*refs):
        at_start(refs)
        compute(refs)
        at_end(refs)

    def compute(refs):
        a_ref, b_ref = refs[:2]
        r_ref = refs[2] if res is not None else None
        o_ref, acc_ref = refs[n_in + len(p_ins)], refs[n_in + len(p_ins) + 1 + len(p_outs)]
        if inner:
            def a_of(s):
                return a_ref[:, s * a_cols:(s + 1) * a_cols] if a_cols else a_ref[s]

            part = lax.dot_general(a_of(0), b_ref[0], dims, preferred_element_type=F32)
            for s in range(1, inner):
                part = part + lax.dot_general(a_of(s), b_ref[s], dims, preferred_element_type=F32)
        else:
            part = lax.dot_general(a_ref[...], b_ref[...], dims, preferred_element_type=F32)

        def finish(total):
            if alpha != 1.0:
                total = total * alpha
            if r_ref is not None:
                total = r_ref[...] + total
            o_ref[...] = total.astype(o_ref.dtype)

        if n_red == 0:
            finish(part)
            return
        ids = [pl.program_id(n_par + i) for i in range(n_red)]
        first = functools.reduce(jnp.logical_and, [i == 0 for i in ids])
        last = functools.reduce(jnp.logical_and, [i == grid[n_par + k] - 1 for k, i in enumerate(ids)])

        @pl.when(first)
        def _():
            acc_ref[...] = part

        @pl.when(jnp.logical_not(first))
        def _():
            acc_ref[...] += part

        @pl.when(last)
        def _():
            finish(acc_ref[...])

    in_specs = [a_spec, b_spec] + ([o_spec] if res is not None else [])
    args = (a, b) + ((res,) if res is not None else ())
    if plan is None:
        return pl.pallas_call(
            body, name=name, out_shape=out_struct, grid=grid, in_specs=in_specs, out_specs=o_spec,
            scratch_shapes=[pltpu.VMEM(acc_shape, F32)], compiler_params=_params(semantics),
        )(*args)
    res_all = pl.pallas_call(
        body, name=name, out_shape=(out_struct,) + tuple(p_outs), grid=grid, in_specs=in_specs + p_in_specs,
        out_specs=(o_spec,) + tuple(p_out_specs), scratch_shapes=[pltpu.VMEM(acc_shape, F32)] + p_scratch,
        compiler_params=_params(semantics),
    )(*args, *p_ins)
    return res_all[0], list(res_all[1:])


_NN = (((1,), (0,)), ((), ()))
_NT = (((1,), (1,)), ((), ()))
_TN = (((0,), (0,)), ((), ()))


def mm_nn(name, a, b, tm=512, tn=1024, tk=1024, out_dtype=F32, alpha=1.0, res=None, plan=None):
    (m, k), (_, n) = a.shape, b.shape
    tm, tn, tk = _tile(m, tm, 16), _tile(n, tn, LANES), _tile(k, tk, LANES)
    return _mm_call(name, a, b, jax.ShapeDtypeStruct((m, n), out_dtype), (m // tm, n // tn, k // tk),
                    pl.BlockSpec((tm, tk), lambda i, j, kk: (i, kk)), pl.BlockSpec((tk, tn), lambda i, j, kk: (kk, j)),
                    pl.BlockSpec((tm, tn), lambda i, j, kk: (i, j)), _NN, 1, (tm, tn), alpha, res, plan=plan)


def mm_nt(name, a, b, tm=512, tn=1024, tk=1024, out_dtype=F32, alpha=1.0, res=None):
    (m, n), (k, _) = a.shape, b.shape
    tm, tn, tk = _tile(m, tm, 16), _tile(k, tn, LANES), _tile(n, tk, LANES)
    return _mm_call(name, a, b, jax.ShapeDtypeStruct((m, k), out_dtype), (m // tm, k // tn, n // tk),
                    pl.BlockSpec((tm, tk), lambda i, j, kk: (i, kk)), pl.BlockSpec((tn, tk), lambda i, j, kk: (j, kk)),
                    pl.BlockSpec((tm, tn), lambda i, j, kk: (i, j)), _NT, 1, (tm, tn), alpha, res)


def mm_tn(name, a, b, tm=1024, tn=1024, tk=1024, out_dtype=F32, alpha=1.0):
    (m, k), (_, n) = a.shape, b.shape
    tm, tn, tk = _tile(k, tm, LANES), _tile(n, tn, LANES), _tile(m, tk, 16)
    return _mm_call(name, a, b, jax.ShapeDtypeStruct((k, n), out_dtype), (k // tm, n // tn, m // tk),
                    pl.BlockSpec((tk, tm), lambda i, j, kk: (kk, i)), pl.BlockSpec((tk, tn), lambda i, j, kk: (kk, j)),
                    pl.BlockSpec((tm, tn), lambda i, j, kk: (i, j)), _TN, 1, (tm, tn), alpha)


def ffn_up_act(name, a, w, blk, tm=1024, plan=None):
    (m, k), (ns, _, n) = a.shape, w.shape
    tm = _tile(m, tm, 16)
    grid = (ns, m // tm)
    p_ins, p_in_specs, p_outs, p_out_specs, p_scratch, semantics, at_start, at_end = _hosted(plan, 3, 2, grid, ("parallel", "parallel"))

    def body(*refs):
        a_ref, wg_ref, wu_ref = refs[:3]
        gu_ref, h_ref = refs[3 + len(p_ins):5 + len(p_ins)]
        at_start(refs)
        av = a_ref[...]
        gate = jnp.dot(av, wg_ref[...], preferred_element_type=F32)
        up = jnp.dot(av, wu_ref[...], preferred_element_type=F32)
        gu_ref[0] = gate.astype(BF16)
        gu_ref[1] = up.astype(BF16)
        h_ref[...] = (gate * jax.nn.sigmoid(gate) * up).astype(BF16)
        at_end(refs)

    res = pl.pallas_call(
        body, name=name, grid=grid,
        out_shape=(jax.ShapeDtypeStruct((2, ns, m, n), BF16), jax.ShapeDtypeStruct((ns, m, n), BF16)) + tuple(p_outs),
        in_specs=[pl.BlockSpec((tm, k), lambda s, i: (i, 0)), pl.BlockSpec((None, k, n), lambda s, i: (s, blk, 0)),
                  pl.BlockSpec((None, k, n), lambda s, i: (s, blk + 1, 0))] + p_in_specs,
        out_specs=(pl.BlockSpec((2, None, tm, n), lambda s, i: (0, s, i, 0)), pl.BlockSpec((None, tm, n), lambda s, i: (s, i, 0)))
        + tuple(p_out_specs),
        scratch_shapes=p_scratch, compiler_params=_params(semantics),
    )(a, w, w, *p_ins)
    return res[0], res[1], list(res[2:])


def ffn_down_dact(name, a, w, gu, n, blk, alpha, tm=1024, plan=None):
    (m, d), ns = a.shape, w.shape[0]
    tm = _tile(m, tm, 16)
    grid = (ns, m // tm)
    p_ins, p_in_specs, p_outs, p_out_specs, p_scratch, semantics, at_start, at_end = _hosted(plan, 3, 1, grid, ("parallel", "parallel"))

    def body(*refs):
        a_ref, w_ref, gu_ref = refs[:3]
        o_ref = refs[3 + len(p_ins)]
        at_start(refs)
        dh = lax.dot_general(a_ref[...], w_ref[...], _NT, preferred_element_type=F32) * alpha
        gate, up = gu_ref[0].astype(F32), gu_ref[1].astype(F32)
        s = jax.nn.sigmoid(gate)
        o_ref[0] = (dh * up * (s * (1.0 + gate * (1.0 - s)))).astype(BF16)
        o_ref[1] = (dh * (gate * s)).astype(BF16)
        at_end(refs)

    gu_spec = pl.BlockSpec((2, None, tm, n), lambda s, i: (0, s, i, 0))
    res = pl.pallas_call(
        body, name=name, grid=grid, out_shape=(jax.ShapeDtypeStruct((2, ns, m, n), BF16),) + tuple(p_outs),
        in_specs=[pl.BlockSpec((tm, d), lambda s, i: (i, 0)), pl.BlockSpec((None, n, d), lambda s, i: (s, blk, 0)), gu_spec] + p_in_specs,
        out_specs=(gu_spec,) + tuple(p_out_specs), scratch_shapes=p_scratch, compiler_params=_params(semantics),
    )(a, w, gu, *p_ins)
    return res[0], list(res[1:])


def gmm_up_dw(name, a, dgu, tk=2048, plan=None):
    (m, k), (_, ns, _, n) = a.shape, dgu.shape
    tk = _tile(m, tk, 16)
    return _mm_call(name, a, dgu, jax.ShapeDtypeStruct((ns, 2 * k, n), BF16), (2, ns, m // tk),
                    pl.BlockSpec((tk, k), lambda j, s, kk: (kk, 0)), pl.BlockSpec((None, None, tk, n), lambda j, s, kk: (j, s, kk, 0)),
                    pl.BlockSpec((None, k, n), lambda j, s, kk: (s, j, 0)), _TN, 1, (k, n), plan=plan)


def gmm_up_dx(name, dgu, w, k, blk, tm=512, plan=None):
    _, ns, m, n = dgu.shape
    tm = _tile(m, tm, 16)
    return _mm_call(name, dgu, w, jax.ShapeDtypeStruct((m, k), F32), (m // tm, 2),
                    pl.BlockSpec((None, ns, tm, n), lambda i, j: (j, 0, i, 0)),
                    pl.BlockSpec((ns, k, n), lambda i, j: (0, blk + j, 0)),
                    pl.BlockSpec((tm, k), lambda i, j: (i, 0)), _NT, 1, (tm, k), inner=ns, plan=plan)


def gmm_down(name, h, w, blk, res, alpha, tm=512):
    (ns, m, n), d = h.shape, w.shape[2]
    tm = _tile(m, tm, 16)
    return _mm_call(name, h, w, jax.ShapeDtypeStruct((m, d), F32), (m // tm,),
                    pl.BlockSpec((ns, tm, n), lambda i: (0, i, 0)), pl.BlockSpec((ns, n, d), lambda i: (0, blk, 0)),
                    pl.BlockSpec((tm, d), lambda i: (i, 0)), _NN, 0, (8, LANES), alpha, res, inner=ns)


def gmm_down_dw(name, h, b, alpha, tk=2048):
    (ns, m, n), d = h.shape, b.shape[1]
    tk = _tile(m, tk, 16)
    return _mm_call(name, h, b, jax.ShapeDtypeStruct((ns, n, d), BF16), (ns, m // tk),
                    pl.BlockSpec((None, tk, n), lambda s, kk: (s, kk, 0)), pl.BlockSpec((tk, d), lambda s, kk: (kk, 0)),
                    pl.BlockSpec((None, n, d), lambda s, kk: (s, 0, 0)), _TN, 1, (n, d), alpha)


def gmm_rows(name, a, w, res, tm=1024):
    (m, _), (ns, r, d) = a.shape, w.shape
    tm = _tile(m, tm, 16)
    return _mm_call(name, a, w, jax.ShapeDtypeStruct((m, d), F32), (m // tm,),
                    pl.BlockSpec((tm, ns * r), lambda i: (i, 0)), pl.BlockSpec((ns, r, d), lambda i: (0, 0, 0)),
                    pl.BlockSpec((tm, d), lambda i: (i, 0)), _NN, 0, (8, LANES), 1.0, res, inner=ns, a_cols=r)


def gmm_rows_dx(name, a, w, tm=1024):
    (m, d), (ns, r, _) = a.shape, w.shape
    tm = _tile(m, tm, 16)
    return _mm_call(name, a, w, jax.ShapeDtypeStruct((m, ns * r), F32), (ns, m // tm),
                    pl.BlockSpec((tm, d), lambda s, i: (i, 0)), pl.BlockSpec((None, r, d), lambda s, i: (s, 0, 0)),
                    pl.BlockSpec((tm, r), lambda s, i: (i, s)), _NT, 0, (8, LANES))


def gmm_rows_dw(name, a, b, tk=2048):
    (m, da), d = a.shape, b.shape[1]
    ns, r = N_SHARDS, da // N_SHARDS
    tk = _tile(m, tk, 16)
    return _mm_call(name, a, b, jax.ShapeDtypeStruct((ns, r, d), BF16), (ns, m // tk),
                    pl.BlockSpec((tk, r), lambda s, kk: (kk, s)), pl.BlockSpec((tk, d), lambda s, kk: (kk, 0)),
                    pl.BlockSpec((None, r, d), lambda s, kk: (s, 0, 0)), _TN, 1, (r, d))


def rms_fwd(name, x, gain, tm=1024, col_blk=0):
    m, d = x.shape[0], gain.shape[1]
    tm = _tile(m, tm, 16)

    def body(x_ref, g_ref, o_ref):
        xv = x_ref[...]
        r = lax.rsqrt(jnp.mean(xv * xv, axis=-1, keepdims=True) + EPS)
        o_ref[...] = (xv * r * g_ref[...]).astype(o_ref.dtype)

    return pl.pallas_call(
        body, name=name, out_shape=jax.ShapeDtypeStruct((m, d), BF16), grid=(m // tm,),
        in_specs=[pl.BlockSpec((tm, d), lambda i: (i, col_blk)), pl.BlockSpec((1, d), lambda i: (0, 0))],
        out_specs=pl.BlockSpec((tm, d), lambda i: (i, 0)), compiler_params=_params(("parallel",)),
    )(x, gain)


def rms_bwd(name, x, gain, dn, dres=None, want_f32=True, want_bf16=True, tm=512, col_blk=0):
    m, d = x.shape[0], gain.shape[1]
    tm = _tile(m, tm, 16)
    n_out = int(want_f32) + int(want_bf16)

    def body(*refs):
        x_ref, g_ref, dn_ref = refs[:3]
        pos = 3
        r_ref = None
        if dres is not None:
            r_ref = refs[pos]
            pos += 1
        outs = refs[pos:pos + n_out]
        dg_ref = refs[pos + n_out]
        xv = x_ref[...]
        r = lax.rsqrt(jnp.mean(xv * xv, axis=-1, keepdims=True) + EPS)
        xhat = xv * r
        dnv = dn_ref[...]
        dxhat = dnv * g_ref[...]
        dx = r * (dxhat - xhat * jnp.mean(dxhat * xhat, axis=-1, keepdims=True))
        if r_ref is not None:
            dx = r_ref[...] + dx
        for o in outs:
            o[...] = dx.astype(o.dtype)
        part = jnp.sum(dnv * xhat, axis=0, keepdims=True)

        @pl.when(pl.program_id(0) == 0)
        def _():
            dg_ref[...] = part

        @pl.when(pl.program_id(0) > 0)
        def _():
            dg_ref[...] += part

    row = pl.BlockSpec((tm, d), lambda i: (i, 0))
    vec = pl.BlockSpec((1, d), lambda i: (0, 0))
    out_shape = ([jax.ShapeDtypeStruct((m, d), F32)] if want_f32 else []) + ([jax.ShapeDtypeStruct((m, d), BF16)] if want_bf16 else [])
    res = pl.pallas_call(
        body, name=name, out_shape=tuple(out_shape) + (jax.ShapeDtypeStruct((1, d), F32),), grid=(m // tm,),
        in_specs=[pl.BlockSpec((tm, d), lambda i: (i, col_blk)), vec, row] + ([row] if dres is not None else []),
        out_specs=tuple([row] * n_out) + (vec,), compiler_params=_params(("arbitrary",)),
    )(*((x, gain, dn) + ((dres,) if dres is not None else ())))
    return res


def gate_fwd(name, proj, ya, yb, off_a, off_b, tm=256):
    m, d = ya.shape
    tm = _tile(m, tm, 16)
    blk_a, blk_b = off_a // d, off_b // d
    assert blk_a * d == off_a and blk_b * d == off_b, "the gates start on multiples of their width"

    def body(ga_ref, gb_ref, ya_ref, yb_ref, o_ref):
        o_ref[...] = (jax.nn.sigmoid(ga_ref[...]) * ya_ref[...] + jax.nn.sigmoid(gb_ref[...]) * yb_ref[...]).astype(o_ref.dtype)

    row = pl.BlockSpec((tm, d), lambda i: (i, 0))
    return pl.pallas_call(
        body, name=name, out_shape=jax.ShapeDtypeStruct((m, d), BF16), grid=(m // tm,),
        in_specs=[pl.BlockSpec((tm, d), lambda i: (i, blk_a)), pl.BlockSpec((tm, d), lambda i: (i, blk_b)), row, row], out_specs=row,
        compiler_params=_params(("parallel",)),
    )(proj, proj, ya, yb)


def gate_bwd(name, proj, ya, yb, dmerged, off_a, off_b, tm=256):
    m, d = ya.shape
    tm = _tile(m, tm, 16)
    blk_a, blk_b = off_a // d, off_b // d
    assert blk_a * d == off_a and blk_b * d == off_b, "the gates start on multiples of their width"

    def body(ga_ref, gb_ref, ya_ref, yb_ref, dm_ref, dya_ref, dyb_ref, dga_ref, dgb_ref):
        sa, sb = jax.nn.sigmoid(ga_ref[...]), jax.nn.sigmoid(gb_ref[...])
        dm = dm_ref[...]
        dya_ref[...] = (dm * sa).astype(BF16)
        dyb_ref[...] = (dm * sb).astype(BF16)
        dga_ref[...] = (dm * ya_ref[...] * (sa * (1.0 - sa))).astype(BF16)
        dgb_ref[...] = (dm * yb_ref[...] * (sb * (1.0 - sb))).astype(BF16)

    row = pl.BlockSpec((tm, d), lambda i: (i, 0))
    o = jax.ShapeDtypeStruct((m, d), BF16)
    return pl.pallas_call(
        body, name=name, out_shape=(o, o, o, o), grid=(m // tm,),
        in_specs=[pl.BlockSpec((tm, d), lambda i: (i, blk_a)), pl.BlockSpec((tm, d), lambda i: (i, blk_b)), row, row, row],
        out_specs=(row, row, row, row), compiler_params=_params(("parallel",)),
    )(proj, proj, ya, yb, dmerged)


def loss_fwd_bwd(name, y, target, tm=512):
    m, d = y.shape
    tm = _tile(m, tm, 16)

    def body(y_ref, t_ref, l_ref, dy_ref, dyb_ref):
        err = y_ref[...] - t_ref[...]
        dy = err * (1.0 / d)
        dy_ref[...] = dy
        dyb_ref[...] = dy.astype(BF16)
        part = 0.5 * jnp.sum(jnp.mean(err * err, axis=-1, keepdims=True), axis=0, keepdims=True)

        @pl.when(pl.program_id(0) == 0)
        def _():
            l_ref[...] = part

        @pl.when(pl.program_id(0) > 0)
        def _():
            l_ref[...] += part

    row = pl.BlockSpec((tm, d), lambda i: (i, 0))
    return pl.pallas_call(
        body, name=name, grid=(m // tm,),
        out_shape=(jax.ShapeDtypeStruct((1, 1), F32), jax.ShapeDtypeStruct((m, d), F32), jax.ShapeDtypeStruct((m, d), BF16)),
        in_specs=[row, row], out_specs=(pl.BlockSpec((1, 1), lambda i: (0, 0)), row, row),
        compiler_params=_params(("arbitrary",)),
    )(y, target)


def adamw(name, w, g, m, v):
    r, c = w.shape
    cap = (2 * 1024 * 1024) // 4
    tr, tc = _tile(r, 1024, 8), c
    while tr * tc > cap and tc % (2 * LANES) == 0:
        tc //= 2
    while tr * tc > cap and tr % 16 == 0:
        tr //= 2
    c1 = np.float32(1.0 - ADAM_B1 ** ADAM_STEP)
    c2 = np.float32(1.0 - ADAM_B2 ** ADAM_STEP)

    def body(w_ref, g_ref, m_ref, v_ref, d_ref, nm_ref, nv_ref):
        gv = g_ref[...]
        nm = ADAM_B1 * m_ref[...] + (1.0 - ADAM_B1) * gv
        nv = ADAM_B2 * v_ref[...] + (1.0 - ADAM_B2) * (gv * gv)
        d_ref[...] = -ADAM_LR * ((nm / c1) / (jnp.sqrt(nv / c2) + ADAM_EPS) + ADAM_WD * w_ref[...])
        nm_ref[...] = nm
        nv_ref[...] = nv

    blk = pl.BlockSpec((tr, tc), lambda i, j: (i, j))
    o = jax.ShapeDtypeStruct((r, c), F32)
    return pl.pallas_call(
        body, name=name, out_shape=(o, o, o), grid=(r // tr, c // tc), in_specs=[blk] * 4, out_specs=(blk, blk, blk),
        compiler_params=_params(("parallel", "parallel")),
    )(w, g, m, v)


HEAD_PAD = LANES


def _rope_rot():
    r = np.zeros((HEAD_PAD, HEAD_PAD), np.float32)
    half = QK_ROPE_DIM // 2
    for j in range(half):
        r[QK_NOPE_DIM + half + j, QK_NOPE_DIM + j] = -1.0
        r[QK_NOPE_DIM + j, QK_NOPE_DIM + half + j] = 1.0
    return r


def _head_spec(s, blk0):
    return pl.BlockSpec((None, s, HEAD_PAD), lambda bi, hi: (bi, 0, blk0 + hi))


def qk_prep_fwd(name, x, blk0, n_heads, d_real, gain, rope=None, extra=None):
    b, s, _ = x.shape

    def body(*refs):
        x_ref, g_ref = refs[0], refs[1]
        pos = 2
        xv = x_ref[...]
        if extra is not None:
            xv = xv + refs[pos][...]
            pos += 1
        y = xv * lax.rsqrt(jnp.sum(xv * xv, axis=-1, keepdims=True) * (1.0 / d_real) + EPS) * g_ref[...]
        if rope is not None:
            c_ref, s_ref, r_ref = refs[pos:pos + 3]
            rot = jnp.dot(y, r_ref[...], precision=lax.Precision.HIGHEST, preferred_element_type=F32)
            y = y * c_ref[...] + rot * s_ref[...]
        refs[-1][...] = y.astype(BF16)

    vec = pl.BlockSpec((1, HEAD_PAD), lambda bi, hi: (0, 0))
    tab = pl.BlockSpec((None, s, HEAD_PAD), lambda bi, hi: (bi, 0, 0))
    in_specs, args = [_head_spec(s, blk0), vec], [x, gain]
    if extra is not None:
        e_blk = extra[1]
        in_specs.append(pl.BlockSpec((None, s, HEAD_PAD), lambda bi, hi: (bi, 0, e_blk)))
        args.append(extra[0])
    if rope is not None:
        in_specs += [tab, tab, pl.BlockSpec((HEAD_PAD, HEAD_PAD), lambda bi, hi: (0, 0))]
        args += [rope[0], rope[1], jnp.asarray(_rope_rot())]
    return pl.pallas_call(
        body, name=name, out_shape=jax.ShapeDtypeStruct((b, s, n_heads * HEAD_PAD), BF16), grid=(b, n_heads),
        in_specs=in_specs, out_specs=_head_spec(s, 0), compiler_params=_params(("parallel", "parallel")),
    )(*args)


def qk_prep_bwd(name, x, blk0, n_heads, d_real, gain, dxh, rope=None, extra=None, head_sum=False):
    b, s, _ = x.shape

    def body(*refs):
        x_ref, g_ref, dy_ref = refs[:3]
        pos = 3
        xv = x_ref[...]
        if extra is not None:
            xv = xv + refs[pos][...]
            pos += 1
        dy = dy_ref[...]
        if rope is not None:
            c_ref, s_ref, rt_ref = refs[pos:pos + 3]
            pos += 3
            dy = dy * c_ref[...] + jnp.dot(dy * s_ref[...], rt_ref[...], precision=lax.Precision.HIGHEST, preferred_element_type=F32)
        outs = refs[pos:]
        dx_ref, dg_ref = outs[0], outs[-1]
        r = lax.rsqrt(jnp.sum(xv * xv, axis=-1, keepdims=True) * (1.0 / d_real) + EPS)
        xhat = xv * r
        dxhat = dy * g_ref[...]
        dx = r * (dxhat - xhat * (jnp.sum(dxhat * xhat, axis=-1, keepdims=True) * (1.0 / d_real)))
        dx_ref[...] = dx.astype(dx_ref.dtype)
        part = jnp.sum(dy * xhat, axis=0, keepdims=True)
        first = jnp.logical_and(pl.program_id(0) == 0, pl.program_id(1) == 0)

        @pl.when(first)
        def _():
            dg_ref[...] = part

        @pl.when(jnp.logical_not(first))
        def _():
            dg_ref[...] += part

        if head_sum:
            hs_ref = outs[1]

            @pl.when(pl.program_id(1) == 0)
            def _():
                hs_ref[...] = dx

            @pl.when(pl.program_id(1) > 0)
            def _():
                hs_ref[...] += dx

    vec = pl.BlockSpec((1, HEAD_PAD), lambda bi, hi: (0, 0))
    tab = pl.BlockSpec((None, s, HEAD_PAD), lambda bi, hi: (bi, 0, 0))
    in_specs, args = [_head_spec(s, blk0), vec, _head_spec(s, 0)], [x, gain, dxh]
    if extra is not None:
        e_blk = extra[1]
        in_specs.append(pl.BlockSpec((None, s, HEAD_PAD), lambda bi, hi: (bi, 0, e_blk)))
        args.append(extra[0])
    if rope is not None:
        in_specs += [tab, tab, pl.BlockSpec((HEAD_PAD, HEAD_PAD), lambda bi, hi: (0, 0))]
        args += [rope[0], rope[1], jnp.asarray(_rope_rot().T.copy())]
    out_shape, out_specs = [jax.ShapeDtypeStruct((b, s, n_heads * HEAD_PAD), BF16)], [_head_spec(s, 0)]
    if head_sum:
        out_shape.append(jax.ShapeDtypeStruct((b, s, HEAD_PAD), F32))
        out_specs.append(tab)
    out_shape.append(jax.ShapeDtypeStruct((1, HEAD_PAD), F32))
    out_specs.append(vec)
    return pl.pallas_call(
        body, name=name, out_shape=tuple(out_shape), grid=(b, n_heads), in_specs=in_specs, out_specs=tuple(out_specs),
        compiler_params=_params(("arbitrary", "arbitrary")),
    )(*args)


def _swa_specs(v_blk0):
    q = pl.BlockSpec((None, BLOCK, GROUP_A * HEAD_PAD), lambda b, kv, n: (b, n, kv))
    kprev = pl.BlockSpec((None, BLOCK, HEAD_PAD), lambda b, kv, n: (b, jnp.maximum(n - 1, 0), kv))
    kcur = pl.BlockSpec((None, BLOCK, HEAD_PAD), lambda b, kv, n: (b, n, kv))
    vprev = pl.BlockSpec((None, BLOCK, HEAD_PAD), lambda b, kv, n: (b, jnp.maximum(n - 1, 0), v_blk0 + kv))
    vcur = pl.BlockSpec((None, BLOCK, HEAD_PAD), lambda b, kv, n: (b, n, v_blk0 + kv))
    pcol = pl.BlockSpec((None, BLOCK, 1), lambda b, kv, n: (b, n, 0))
    prow_prev = pl.BlockSpec((None, 1, BLOCK), lambda b, kv, n: (b, 0, jnp.maximum(n - 1, 0)))
    prow_cur = pl.BlockSpec((None, 1, BLOCK), lambda b, kv, n: (b, 0, n))
    smem = pl.BlockSpec(memory_space=pltpu.SMEM)
    return q, kprev, kcur, vprev, vcur, pcol, prow_prev, prow_cur, smem


def _swa_probs(q, kk, dist, valid, slope, sink):
    sc = lax.dot_general(q, kk, _NT, preferred_element_type=F32) * (HEAD_DIM_A ** -0.5)
    sc = sc - slope * dist
    sc = jnp.where(valid, sc, NEG)
    m = jnp.maximum(jnp.max(sc, axis=-1, keepdims=True), sink)
    e = jnp.exp(sc - m)
    es = jnp.exp(sink - m)
    inv = 1.0 / (jnp.sum(e, axis=-1, keepdims=True) + es)
    return e * inv, es * inv


def _swa_window(n, kp_ref, kc_ref, vp_ref, vc_ref, pc_ref, prp_ref, prc_ref):
    kk = jnp.concatenate([kp_ref[...], kc_ref[...]], axis=0)
    vv = jnp.concatenate([vp_ref[...], vc_ref[...]], axis=0).astype(BF16)
    dist = pc_ref[...] - jnp.concatenate([prp_ref[...], prc_ref[...]], axis=1)
    qi = lax.broadcasted_iota(jnp.int32, (BLOCK, 2 * BLOCK), 0) + BLOCK
    ki = lax.broadcasted_iota(jnp.int32, (BLOCK, 2 * BLOCK), 1)
    diff = qi - ki
    valid = (diff >= 0) & (diff < BLOCK) & ((n > 0) | (ki >= BLOCK))
    return kk, vv, dist, valid


def swa_fwd(name, q, k, vsrc, v_blk0, pos_col, pos_row, slopes, sinks, plan=None):
    b, s, _ = q.shape
    qs, kprev, kcur, vprev, vcur, pcol, prp, prc, smem = _swa_specs(v_blk0)
    grid = (b, N_KV_HEADS_A, s // BLOCK)
    p_ins, p_in_specs, p_outs, p_out_specs, p_scratch, semantics, at_start, at_end = _hosted(plan, 10, 1, grid, ("parallel",) * 3)

    def body(*refs):
        q_ref, kp_ref, kc_ref, vp_ref, vc_ref, pc_ref, prp_ref, prc_ref, sl_ref, sk_ref = refs[:10]
        o_ref = refs[10 + len(p_ins)]
        at_start(refs)
        kv, n = pl.program_id(1), pl.program_id(2)
        kk, vv, dist, valid = _swa_window(n, kp_ref, kc_ref, vp_ref, vc_ref, pc_ref, prp_ref, prc_ref)
        for g in range(GROUP_A):
            hd = kv * GROUP_A + g
            lanes = slice(g * HEAD_PAD, (g + 1) * HEAD_PAD)
            p, _ = _swa_probs(q_ref[:, lanes], kk, dist, valid, sl_ref[hd], sk_ref[hd])
            o_ref[:, lanes] = jnp.dot(p.astype(BF16), vv, preferred_element_type=F32).astype(BF16)
        at_end(refs)

    res = pl.pallas_call(
        body, name=name, out_shape=(jax.ShapeDtypeStruct(q.shape, BF16),) + tuple(p_outs), grid=grid,
        in_specs=[qs, kprev, kcur, vprev, vcur, pcol, prp, prc, smem, smem] + p_in_specs, out_specs=(qs,) + tuple(p_out_specs),
        scratch_shapes=p_scratch, compiler_params=_params(semantics),
    )(q, k, k, vsrc, vsrc, pos_col, pos_row, pos_row, slopes, sinks, *p_ins)
    return res[0], list(res[1:])


def swa_bwd(name, q, k, vsrc, v_blk0, pos_col, pos_row, slopes, sinks, do, plan=None):
    b, s, _ = q.shape
    qs, kprev, kcur, vprev, vcur, pcol, prp, prc, smem = _swa_specs(v_blk0)
    grid = (b, N_KV_HEADS_A, s // BLOCK)
    p_ins, p_in_specs, p_outs, p_out_specs, p_scratch, semantics, at_start, at_end = _hosted(plan, 11, 4, grid, ("arbitrary",) * 3)

    def body(*refs):
        at_start(refs)
        compute(*refs[:11], *refs[11 + len(p_ins):15 + len(p_ins)])
        at_end(refs)

    def compute(q_ref, kp_ref, kc_ref, vp_ref, vc_ref, pc_ref, prp_ref, prc_ref, sl_ref, sk_ref, do_ref, dq_ref, dk_ref, dv_ref, ds_ref):
        bi, kv, n = pl.program_id(0), pl.program_id(1), pl.program_id(2)
        kk, vv, dist, valid = _swa_window(n, kp_ref, kc_ref, vp_ref, vc_ref, pc_ref, prp_ref, prc_ref)

        @pl.when((bi == 0) & (kv == 0) & (n == 0))
        def _():
            ds_ref[...] = jnp.zeros_like(ds_ref)

        @pl.when(n == 0)
        def _():
            dk_ref[...] = jnp.zeros_like(dk_ref)
            dv_ref[...] = jnp.zeros_like(dv_ref)

        dkk = jnp.zeros((2 * BLOCK, HEAD_PAD), F32)
        dvv = jnp.zeros((2 * BLOCK, HEAD_PAD), F32)
        head_row = lax.broadcasted_iota(jnp.int32, (N_HEADS_A, LANES), 0)
        dsink = jnp.zeros((N_HEADS_A, LANES), F32)
        for g in range(GROUP_A):
            hd = kv * GROUP_A + g
            lanes = slice(g * HEAD_PAD, (g + 1) * HEAD_PAD)
            qg = q_ref[:, lanes]
            p, ps = _swa_probs(qg, kk, dist, valid, sl_ref[hd], sk_ref[hd])
            dob = do_ref[:, lanes].astype(BF16)
            dvv = dvv + lax.dot_general(p.astype(BF16), dob, _TN, preferred_element_type=F32)
            dp = lax.dot_general(dob, vv, _NT, preferred_element_type=F32)
            rs = jnp.sum(p * dp, axis=-1, keepdims=True)
            dsb = (p * (dp - rs) * (HEAD_DIM_A ** -0.5)).astype(BF16)
            dq_ref[:, lanes] = jnp.dot(dsb, kk, preferred_element_type=F32)
            dkk = dkk + lax.dot_general(dsb, qg, _TN, preferred_element_type=F32)
            dsink = dsink + jnp.where(head_row == hd, -jnp.sum(ps * rs), 0.0)
        ds_ref[...] += dsink

        @pl.when(n > 0)
        def _():
            start = pl.multiple_of((n - 1) * BLOCK, BLOCK)
            dk_ref[pl.ds(start, 2 * BLOCK), :] += dkk
            dv_ref[pl.ds(start, 2 * BLOCK), :] += dvv

        @pl.when(n == 0)
        def _():
            dk_ref[0:BLOCK, :] += dkk[BLOCK:, :]
            dv_ref[0:BLOCK, :] += dvv[BLOCK:, :]

    kv_full = pl.BlockSpec((None, s, HEAD_PAD), lambda bi, kv, n: (bi, 0, kv))
    kv_shape = jax.ShapeDtypeStruct(k.shape, F32)
    res = pl.pallas_call(
        body, name=name, grid=grid,
        out_shape=(jax.ShapeDtypeStruct(q.shape, F32), kv_shape, kv_shape, jax.ShapeDtypeStruct((N_HEADS_A, LANES), F32)) + tuple(p_outs),
        in_specs=[qs, kprev, kcur, vprev, vcur, pcol, prp, prc, smem, smem, qs] + p_in_specs,
        out_specs=(qs, kv_full, kv_full, pl.BlockSpec((N_HEADS_A, LANES), lambda bi, kv, n: (0, 0))) + tuple(p_out_specs),
        scratch_shapes=p_scratch, compiler_params=_params(semantics),
    )(q, k, k, vsrc, vsrc, pos_col, pos_row, pos_row, slopes, sinks, do, *p_ins)
    return res[0], res[1], res[2], res[3], list(res[4:])


MLA_T = 512


def _hosted(plan, n_in, n_out, grid, semantics):
    if plan is None:
        return [], [], [], [], [], semantics, (lambda refs: None), (lambda refs: None)
    ni, no = len(plan.ins), len(plan.out_shapes)

    def split(refs):
        return refs[n_in:n_in + ni], refs[n_in + ni + n_out:n_in + ni + n_out + no], refs[-2], refs[-1]

    def at_start(refs):
        @pl.when(functools.reduce(jnp.logical_and, [pl.program_id(ax) == 0 for ax in range(len(grid))]))
        def _():
            plan.start(*split(refs))

    def at_end(refs):
        @pl.when(functools.reduce(jnp.logical_and, [pl.program_id(ax) == g - 1 for ax, g in enumerate(grid)]))
        def _():
            plan.finish(*split(refs))

    return plan.ins, [_HBM] * ni, plan.out_shapes, [_HBM] * no, plan.scratch(), ("arbitrary",) * len(grid), at_start, at_end


def mla_fwd(name, q, k, v, plan=None):
    b, s, w = q.shape
    h = w // HEAD_PAD
    t = MLA_T if s % MLA_T == 0 else BLOCK
    nb = s // t
    scale = QK_DIM_B ** -0.5
    p_ins, p_in_specs, p_outs, p_out_specs, p_scratch, semantics, at_start, at_end = _hosted(
        plan, 3, 3, (b, h), ("parallel", "parallel"))

    def body(*refs):
        q_ref, k_ref, v_ref = refs[:3]
        o_ref, ob_ref, lse_ref = refs[3 + len(p_ins):6 + len(p_ins)]
        at_start(refs)
        causal = lax.broadcasted_iota(jnp.int32, (t, t), 0) <= lax.broadcasted_iota(jnp.int32, (t, t), 1)
        for i in range(nb):
            q_i = q_ref[i * t:(i + 1) * t, :]

            def step(j, carry, q_i=q_i, diagonal=False):
                m, l, acc = carry
                rows = slice(j * t, (j + 1) * t) if diagonal else pl.ds(pl.multiple_of(j * t, t), t)
                st = lax.dot_general(k_ref[rows, :], q_i, _NT, preferred_element_type=F32) * scale
                if diagonal:
                    st = jnp.where(causal, st, NEG)
                m_new = jnp.maximum(m, jnp.max(st, axis=0, keepdims=True))
                a = jnp.exp(m - m_new)
                p = jnp.exp(st - m_new)
                l = a * l + jnp.sum(p, axis=0, keepdims=True)
                acc = a * acc + lax.dot_general(v_ref[rows, :], p.astype(BF16), _TN, preferred_element_type=F32)
                return m_new, l, acc

            carry = (jnp.full((1, t), NEG, F32), jnp.zeros((1, t), F32), jnp.zeros((HEAD_PAD, t), F32))
            if i > 0:
                carry = lax.fori_loop(0, i, step, carry)
            m, l, acc = step(i, carry, diagonal=True)
            o_i = (acc / l).T
            o_ref[i * t:(i + 1) * t, :] = o_i
            ob_ref[i * t:(i + 1) * t, :] = o_i.astype(BF16)
            lse_ref[i] = m + jnp.log(l)
        at_end(refs)

    hs = _head_spec(s, 0)
    ls = pl.BlockSpec((None, None, nb, 1, t), lambda bi, hi: (bi, hi, 0, 0, 0))
    res = pl.pallas_call(
        body, name=name, grid=(b, h),
        out_shape=(jax.ShapeDtypeStruct((b, s, w), F32), jax.ShapeDtypeStruct((b, s, w), BF16),
                   jax.ShapeDtypeStruct((b, h, nb, 1, t), F32)) + tuple(p_outs),
        in_specs=[hs, hs, hs] + p_in_specs, out_specs=(hs, hs, ls) + tuple(p_out_specs),
        scratch_shapes=p_scratch, compiler_params=_params(semantics),
    )(q, k, v, *p_ins)
    return res[0], res[1], res[2], list(res[3:])


def mla_bwd(name, q, k, v, o, do, lse, plan=None):
    b, s, w = q.shape
    h = w // HEAD_PAD
    _, _, nb, _, t = lse.shape
    scale = QK_DIM_B ** -0.5
    p_ins, p_in_specs, p_outs, p_out_specs, p_scratch, semantics, at_start, at_end = _hosted(
        plan, 6, 3, (b, h), ("parallel", "parallel"))

    def body(*refs):
        q_ref, k_ref, v_ref, o_ref, do_ref, lse_ref = refs[:6]
        dq_ref, dk_ref, dv_ref = refs[6 + len(p_ins):9 + len(p_ins)]
        dv_acc = refs[9 + len(p_ins) + len(p_outs)]
        at_start(refs)
        causal = lax.broadcasted_iota(jnp.int32, (t, t), 0) <= lax.broadcasted_iota(jnp.int32, (t, t), 1)
        dk_ref[...] = jnp.zeros_like(dk_ref)
        dv_acc[...] = jnp.zeros_like(dv_acc)
        for i in range(nb):
            q_i = q_ref[i * t:(i + 1) * t, :]
            do_i = do_ref[i * t:(i + 1) * t, :]
            delta = jnp.sum((o_ref[i * t:(i + 1) * t, :] * do_i).T, axis=0, keepdims=True)
            do_b = do_i.astype(BF16)
            lse_i = lse_ref[i]

            def step(j, dqt, q_i=q_i, do_b=do_b, delta=delta, lse_i=lse_i, diagonal=False):
                rows = slice(j * t, (j + 1) * t) if diagonal else pl.ds(pl.multiple_of(j * t, t), t)
                k_j = k_ref[rows, :]
                st = lax.dot_general(k_j, q_i, _NT, preferred_element_type=F32) * scale
                if diagonal:
                    st = jnp.where(causal, st, NEG)
                pt = jnp.exp(st - lse_i)
                dpt = lax.dot_general(v_ref[rows, :], do_b, _NT, preferred_element_type=F32)
                dst = (pt * (dpt - delta) * scale).astype(BF16)
                dv_acc[rows, :] += jnp.dot(pt.astype(BF16), do_b, preferred_element_type=F32)
                dk_ref[rows, :] += jnp.dot(dst, q_i, preferred_element_type=F32)
                return dqt + lax.dot_general(k_j, dst, _TN, preferred_element_type=F32)

            dqt = jnp.zeros((HEAD_PAD, t), F32)
            if i > 0:
                dqt = lax.fori_loop(0, i, step, dqt)
            dq_ref[i * t:(i + 1) * t, :] = step(i, dqt, diagonal=True).T
        dv_ref[...] = dv_acc[...].astype(BF16)
        at_end(refs)

    hs = _head_spec(s, 0)
    ls = pl.BlockSpec((None, None, nb, 1, t), lambda bi, hi: (bi, hi, 0, 0, 0))
    res = pl.pallas_call(
        body, name=name, grid=(b, h),
        out_shape=(jax.ShapeDtypeStruct((b, s, w), F32), jax.ShapeDtypeStruct((b, s, w), F32),
                   jax.ShapeDtypeStruct((b, s, w), BF16)) + tuple(p_outs),
        in_specs=[hs, hs, hs, hs, hs, ls] + p_in_specs, out_specs=(hs, hs, hs) + tuple(p_out_specs),
        scratch_shapes=[pltpu.VMEM((s, HEAD_PAD), F32)] + p_scratch,
        compiler_params=_params(semantics),
    )(q, k, v, o, do, lse, *p_ins)
    return res[0], res[1], res[2], list(res[3:])


_HBM = pl.BlockSpec(memory_space=pltpu.HBM)


def _place():
    x, y, c = lax.axis_index("x"), lax.axis_index("y"), lax.axis_index("c")
    chips = [(1 - x, y), (x, 1 - y), (1 - x, 1 - y)]
    return x, y, c, chips


def _remote(src, dst, send_sem, recv_sem, dev):
    return pltpu.make_async_remote_copy(src_ref=src, dst_ref=dst, send_sem=send_sem, recv_sem=recv_sem,
                                        device_id=dev, device_id_type=MESH)


class CommPlan:
    def __init__(self, ins, out_shapes, n_sems, start, finish, aliases=None):
        self.ins, self.out_shapes, self.n_sems = list(ins), list(out_shapes), n_sems
        self.start, self.finish, self.aliases = start, finish, dict(aliases or {})

    def scratch(self):
        return [pltpu.SemaphoreType.DMA((self.n_sems,)), pltpu.SemaphoreType.DMA((self.n_sems,))]


def comm_call(name, plan):
    ni, no = len(plan.ins), len(plan.out_shapes)

    def body(*refs):
        ins, outs, (send_sems, recv_sems) = refs[:ni], refs[ni:ni + no], refs[ni + no:]
        plan.start(ins, outs, send_sems, recv_sems)
        plan.finish(ins, outs, send_sems, recv_sems)

    return pl.pallas_call(
        body, name=name, out_shape=tuple(plan.out_shapes), in_specs=[_HBM] * ni, out_specs=tuple([_HBM] * no),
        input_output_aliases=plan.aliases, scratch_shapes=plan.scratch(),
    )(*plan.ins)


def gather_plan(packs, l):
    nk = len(packs)

    def parts(p_refs, o_refs, ss, rs):
        x, y, c, chips = _place()
        s_me = 2 * x + y
        sibling = (x, y, 1 - c)
        first = [_remote(p_refs[k].at[l], o_refs[k].at[s_me], ss.at[j * nk + k], rs.at[j * nk + k], (cx, cy, c))
                 for j, (cx, cy) in enumerate(chips) for k in range(nk)]
        own = [_remote(p_refs[k].at[l], o_refs[k].at[s_me], ss.at[6 * nk + k], rs.at[6 * nk + k], sibling) for k in range(nk)]
        return c, chips, sibling, first, own

    def start(p_refs, o_refs, ss, rs):
        c, _, _, first, own = parts(p_refs, o_refs, ss, rs)
        for cp in own:
            cp.start()

        @pl.when(c == l)
        def _():
            for cp in first:
                cp.start()

    def finish(p_refs, o_refs, ss, rs):
        c, chips, sibling, first, own = parts(p_refs, o_refs, ss, rs)

        @pl.when(c == l)
        def _():
            passed = []
            for j, (cx, cy) in enumerate(chips):
                for k in range(nk):
                    blk = o_refs[k].at[2 * cx + cy]
                    _remote(blk, blk, ss.at[j * nk + k], rs.at[j * nk + k], (cx, cy, c)).wait_recv()
                    fwd = _remote(blk, blk, ss.at[(3 + j) * nk + k], rs.at[(3 + j) * nk + k], sibling)
                    fwd.start()
                    passed.append(fwd)
            for cp in first + passed:
                cp.wait_send()

        @pl.when(c != l)
        def _():
            for j, (cx, cy) in enumerate(chips):
                for k in range(nk):
                    blk = o_refs[k].at[2 * cx + cy]
                    _remote(blk, blk, ss.at[(3 + j) * nk + k], rs.at[(3 + j) * nk + k], sibling).wait_recv()

        for cp in own:
            cp.wait()

    outs = [jax.ShapeDtypeStruct((N_SHARDS,) + p.shape[1:], p.dtype) for p in packs]
    return CommPlan(packs, outs, 7 * nk, start, finish)


def swap_plan(grads, l):
    nk = len(grads)

    def copies(g_refs, o_refs, ss, rs):
        x, y, c, _ = _place()
        return c, [_remote(g_refs[k], o_refs[k], ss.at[k], rs.at[k], (x, y, 1 - c)) for k in range(nk)]

    def start(g_refs, o_refs, ss, rs):
        c, cps = copies(g_refs, o_refs, ss, rs)

        @pl.when(c != l)
        def _():
            for cp in cps:
                cp.start()

    def finish(g_refs, o_refs, ss, rs):
        c, cps = copies(g_refs, o_refs, ss, rs)

        @pl.when(c != l)
        def _():
            for cp in cps:
                cp.wait_send()

        @pl.when(c == l)
        def _():
            for cp in cps:
                cp.wait_recv()

    return CommPlan(grads, [jax.ShapeDtypeStruct(g.shape, g.dtype) for g in grads], nk, start, finish)


def scatter_plan(parts, l):
    nk = len(parts)

    def copies(p_refs, o_refs, ss, rs):
        x, y, c, chips = _place()
        s_me = 2 * x + y
        sends = [_remote(p_refs[k].at[2 * cx + cy], o_refs[k].at[s_me], ss.at[j * nk + k], rs.at[j * nk + k], (cx, cy, c))
                 for j, (cx, cy) in enumerate(chips) for k in range(nk)]
        return c, chips, sends

    def start(p_refs, o_refs, ss, rs):
        c, _, sends = copies(p_refs, o_refs, ss, rs)

        @pl.when(c == l)
        def _():
            for cp in sends:
                cp.start()

    def finish(p_refs, o_refs, ss, rs):
        c, chips, sends = copies(p_refs, o_refs, ss, rs)

        @pl.when(c == l)
        def _():
            for j, (cx, cy) in enumerate(chips):
                for k in range(nk):
                    slot = o_refs[k].at[2 * cx + cy]
                    _remote(slot, slot, ss.at[j * nk + k], rs.at[j * nk + k], (cx, cy, c)).wait_recv()
            for cp in sends:
                cp.wait_send()

    return CommPlan(parts, [jax.ShapeDtypeStruct(p.shape, p.dtype) for p in parts], 3 * nk, start, finish)


def share_plan(finals):
    flat = [(l, f) for l, fs in enumerate(finals) for f in fs]

    def copies(f_refs, o_refs, ss, rs):
        x, y, c, _ = _place()
        return c, [_remote(f_refs[i], o_refs[i], ss.at[i], rs.at[i], (x, y, 1 - c)) for i in range(len(flat))]

    def start(f_refs, o_refs, ss, rs):
        c, cps = copies(f_refs, o_refs, ss, rs)
        for l in range(len(finals)):
            @pl.when(c == l)
            def _(l=l):
                for cp, (lf, _) in zip(cps, flat):
                    if lf == l:
                        cp.start()

    def finish(f_refs, o_refs, ss, rs):
        c, cps = copies(f_refs, o_refs, ss, rs)
        for l in range(len(finals)):
            @pl.when(c == l)
            def _(l=l):
                for cp, (lf, _) in zip(cps, flat):
                    if lf == l:
                        cp.wait_send()

            @pl.when(c != l)
            def _(l=l):
                for cp, (lf, _) in zip(cps, flat):
                    if lf == l:
                        cp.wait_recv()

    arrays = [f for _, f in flat]
    return CommPlan(arrays, [jax.ShapeDtypeStruct(f.shape, f.dtype) for f in arrays], len(flat), start, finish,
                    aliases={i: i for i in range(len(flat))})


def add_pair(name, own, other, tr=1024):
    n, cols = own.shape
    tr = _tile(n, tr, 16)

    def body(g_ref, o_ref, out_ref):
        out_ref[...] = (g_ref[...].astype(F32) + o_ref[...].astype(F32)).astype(out_ref.dtype)

    row = pl.BlockSpec((tr, cols), lambda i: (i, 0))
    return pl.pallas_call(
        body, name=name, out_shape=jax.ShapeDtypeStruct((n, cols), BF16), grid=(n // tr,), in_specs=[row, row], out_specs=row,
        compiler_params=_params(("parallel",)),
    )(own, other)


def sum_slots(name, slots, part, shard, tr=512):
    ns, r, cols = slots.shape
    tr = _tile(r, tr, 16)

    def body(s_idx, s_ref, own_ref, o_ref):
        total = own_ref[...].astype(F32)
        for k in range(1, ns):
            total = total + s_ref[(s_idx[0] + k) % ns].astype(F32)
        o_ref[...] = total

    return pl.pallas_call(
        body, name=name, out_shape=jax.ShapeDtypeStruct((r, cols), F32),
        grid_spec=pltpu.PrefetchScalarGridSpec(
            num_scalar_prefetch=1, grid=(r // tr,),
            in_specs=[pl.BlockSpec((ns, tr, cols), lambda i, p: (0, i, 0)), pl.BlockSpec((None, tr, cols), lambda i, p: (p[0], i, 0))],
            out_specs=pl.BlockSpec((tr, cols), lambda i, p: (i, 0))),
        compiler_params=_params(("parallel",)),
    )(shard, slots, part)


def small_allreduce(vec):
    r, cols = vec.shape

    def body(v_ref, o_ref, buf, send_sems, recv_sems):
        x, y, c, _ = _place()
        me = 4 * x + 2 * y + c
        buf[me] = v_ref[...]
        peers = []
        for k in range(1, N_DEV):
            px = (1 - x) if (k & 4) else x
            py = (1 - y) if (k & 2) else y
            pc = (1 - c) if (k & 1) else c
            peers.append((px, py, pc))
        sends = [_remote(buf.at[me], buf.at[me], send_sems.at[k], recv_sems.at[k], peer) for k, peer in enumerate(peers)]
        for cp in sends:
            cp.start()
        for k, (px, py, pc) in enumerate(peers):
            slot = buf.at[4 * px + 2 * py + pc]
            _remote(slot, slot, send_sems.at[k], recv_sems.at[k], (px, py, pc)).wait_recv()
        for cp in sends:
            cp.wait_send()
        total = buf[0]
        for k in range(1, N_DEV):
            total = total + buf[k]
        o_ref[...] = total

    return pl.pallas_call(
        body, name="small_allreduce", out_shape=jax.ShapeDtypeStruct((r, cols), F32),
        in_specs=[pl.BlockSpec(memory_space=pltpu.VMEM)], out_specs=pl.BlockSpec(memory_space=pltpu.VMEM),
        scratch_shapes=[pltpu.VMEM((N_DEV, r, cols), F32), pltpu.SemaphoreType.DMA((N_DEV - 1,)), pltpu.SemaphoreType.DMA((N_DEV - 1,))],
    )(vec)


ARG_NAMES = (("x", "positions") + WEIGHT_NAMES + ("loss_target",) + tuple("m_" + n for n in WEIGHT_NAMES)
             + tuple("v_" + n for n in WEIGHT_NAMES))

_C_KA = N_HEADS_A * HEAD_PAD
_C_VA = _C_KA + N_KV_HEADS_A * HEAD_PAD
_C_CQ = _C_VA + N_KV_HEADS_A * HEAD_PAD
_C_CKV = _C_CQ + Q_LORA_RANK
_C_KR = _C_CKV + KV_LORA_RANK
_C_GA = _C_KR + HEAD_PAD
_R_KA = WIDTH_A
_R_VA = _R_KA + KV_WIDTH_A
_R_CQ = _R_VA + KV_WIDTH_A
_R_KR = _R_CQ + Q_LORA_RANK + KV_LORA_RANK
_R_GA = _R_KR + QK_ROPE_DIM


def _pad_head_cols(w, n_heads):
    r, dh = w.shape[0], w.shape[1] // n_heads
    return jnp.pad(w.reshape(r, n_heads, dh), ((0, 0), (0, 0), (0, HEAD_PAD - dh))).reshape(r, n_heads * HEAD_PAD)


def _unpad_head_cols(w, n_heads, dh):
    r = w.shape[0]
    return w.reshape(r, n_heads, HEAD_PAD)[:, :, :dh].reshape(r, n_heads * dh)


def _pad_head_rows(w, n_heads):
    dh, c = w.shape[0] // n_heads, w.shape[1]
    return jnp.pad(w.reshape(n_heads, dh, c), ((0, 0), (0, HEAD_PAD - dh), (0, 0))).reshape(n_heads * HEAD_PAD, c)


def _unpad_head_rows(w, n_heads, dh):
    c = w.shape[1]
    return w.reshape(n_heads, HEAD_PAD, c)[:, :dh].reshape(n_heads * dh, c)


def _join_cols(shards):
    ns, r, c = shards.shape
    return shards.transpose(1, 0, 2).reshape(r, ns * c)


def _split_cols(mat):
    r, c4 = mat.shape
    return mat.reshape(r, N_SHARDS, c4 // N_SHARDS).transpose(1, 0, 2)


def _ffn_fwd(tag, x, gain, wcol, wdown, plan=None):
    nb = rms_fwd(tag + "_rms", x, gain)
    gu, h, carried = ffn_up_act(tag + "_up", nb, wcol, 0, plan=plan)
    out = gmm_down(tag + "_down", h, wdown, 0, x, 0.5)
    return out, (x, nb, gu, h), carried


def _ffn_bwd(tag, saved, gain, wcol, wdown, dout, doutb, plans=None):
    plans = plans or {}
    x, nb, gu, h = saved
    carried = {}
    dgu, carried["dh"] = ffn_down_dact(tag + "_dh", doutb, wdown, gu, gu.shape[-1], 0, 0.5, plan=plans.get("dh"))
    gdown = gmm_down_dw(tag + "_dwd", h, doutb, 0.5)
    if plans.get("dwgu") is not None:
        gcol, carried["dwgu"] = gmm_up_dw(tag + "_dwgu", nb, dgu, plan=plans["dwgu"])
    else:
        gcol = gmm_up_dw(tag + "_dwgu", nb, dgu)
    if plans.get("dn") is not None:
        plan_dn = plans["dn"](gcol, gdown) if callable(plans["dn"]) else plans["dn"]
        dn, carried["dn"] = gmm_up_dx(tag + "_dn", dgu, wcol, x.shape[1], 0, plan=plan_dn)
    else:
        dn = gmm_up_dx(tag + "_dn", dgu, wcol, x.shape[1], 0)
    dx, dxb, dgain = rms_bwd(tag + "_drms", x, gain, dn, dres=dout)
    return dx, dxb, dgain, gcol, gdown, carried


def _pad_gain(g):
    return jnp.pad(g, ((0, 0), (0, HEAD_PAD - g.shape[1])))


def _mixer_fwd(tag, x1, sm, w, wo, aux, plans=None):
    plans = plans or {}
    b, s = aux["b"], aux["s"]
    t, d = x1.shape
    hb = rms_fwd(tag + "_rms", x1, sm["mix_norm"])
    carried = {}
    if plans.get("proj") is not None:
        proj, carried["proj"] = mm_nn(tag + "_proj", hb, w["win"], tm=1024, tn=2048, tk=1024, plan=plans["proj"])
    else:
        proj = mm_nn(tag + "_proj", hb, w["win"], tm=1024, tn=2048, tk=1024)
    proj3 = proj.reshape(b, s, proj.shape[1])
    gains = {n: _pad_gain(sm[n]) for n in ("swa_q_norm", "swa_k_norm", "mla_q_norm", "mla_k_norm")}
    cqn = rms_fwd(tag + "_rms_cq", proj, sm["mla_q_lora_norm"], tm=1024, col_blk=_C_CQ // Q_LORA_RANK)
    ckvn = rms_fwd(tag + "_rms_ckv", proj, sm["mla_kv_lora_norm"], tm=1024, col_blk=_C_CKV // KV_LORA_RANK)
    qb_raw = mm_nn(tag + "_uq", cqn, w["wuq"], tm=1024).reshape(b, s, N_HEADS_B * HEAD_PAD)
    kb_raw = mm_nn(tag + "_uk", ckvn, w["wk"], tm=1024).reshape(b, s, N_HEADS_B * HEAD_PAD)
    vb = mm_nn(tag + "_uv", ckvn, w["wv"], tm=1024, out_dtype=BF16).reshape(b, s, N_HEADS_B * HEAD_PAD)
    qah = qk_prep_fwd(tag + "_qa_norm", proj3, 0, N_HEADS_A, HEAD_DIM_A, gains["swa_q_norm"])
    kah = qk_prep_fwd(tag + "_ka_norm", proj3, _C_KA // HEAD_PAD, N_KV_HEADS_A, HEAD_DIM_A, gains["swa_k_norm"])
    sinks = sm["swa_sinks"].reshape(-1)
    oab, carried["swa"] = swa_fwd(tag + "_swa", qah, kah, proj3, _C_VA // HEAD_PAD, aux["pos_col"], aux["pos_row"], aux["slopes"],
                                  sinks, plan=plans.get("swa"))
    kr = (proj3, _C_KR // HEAD_PAD)
    qbh = qk_prep_fwd(tag + "_qb_norm", qb_raw, 0, N_HEADS_B, QK_DIM_B, gains["mla_q_norm"], aux["rope"])
    kbh = qk_prep_fwd(tag + "_kb_norm", kb_raw, 0, N_HEADS_B, QK_DIM_B, gains["mla_k_norm"], aux["rope"], extra=kr)
    ob, obb, lse, carried["mla"] = mla_fwd(tag + "_mla", qbh, kbh, vb, plans.get("mla"))
    oab2, obb2 = oab.reshape(t, -1), obb.reshape(t, -1)
    ya = mm_nn(tag + "_branch_a", oab2, w["wa"], tm=1024)
    yb = mm_nn(tag + "_branch_b", obb2, w["wb"], tm=1024)
    mg = gate_fwd(tag + "_gate", proj, ya, yb, _C_GA, _C_GA + d)
    x2 = gmm_rows(tag + "_out", mg, wo, x1)
    saved = dict(x1=x1, hb=hb, proj=proj, cqn=cqn, ckvn=ckvn, qb_raw=qb_raw, kb_raw=kb_raw, vb=vb, qah=qah, kah=kah,
                 qbh=qbh, kbh=kbh, ob=ob, lse=lse, oab2=oab2, obb2=obb2, ya=ya, yb=yb, mg=mg, gains=gains)
    return x2, saved, carried


def _mixer_bwd(tag, sv, sm, w, wo, aux, dx2, dx2b, plans=None):
    plans = plans or {}
    carried = {}
    b, s = aux["b"], aux["s"]
    t, d = dx2.shape
    proj, gains = sv["proj"], sv["gains"]
    proj3 = proj.reshape(b, s, proj.shape[1])
    kr = (proj3, _C_KR // HEAD_PAD)
    dmg = gmm_rows_dx(tag + "_d_out", dx2b, wo)
    dwo = gmm_rows_dw(tag + "_dw_out", sv["mg"], dx2b)
    dya, dyb, dga, dgb = gate_bwd(tag + "_dgate", proj, sv["ya"], sv["yb"], dmg, _C_GA, _C_GA + d)
    dwa = mm_tn(tag + "_dw_branch_a", sv["oab2"], dya, out_dtype=BF16)
    dwb = mm_tn(tag + "_dw_branch_b", sv["obb2"], dyb, out_dtype=BF16)
    doa = mm_nt(tag + "_d_branch_a", dya, w["wa"], tm=1024).reshape(b, s, -1)
    dob = mm_nt(tag + "_d_branch_b", dyb, w["wb"], tm=1024).reshape(b, s, -1)
    sinks = sm["swa_sinks"].reshape(-1)
    dqah, dkah, dva, dsinks, carried["dswa"] = swa_bwd(tag + "_dswa", sv["qah"], sv["kah"], proj3, _C_VA // HEAD_PAD, aux["pos_col"],
                                                       aux["pos_row"], aux["slopes"], sinks, doa, plan=plans.get("dswa"))
    dqa_raw, dg_swa_q = qk_prep_bwd(tag + "_dqa_norm", proj3, 0, N_HEADS_A, HEAD_DIM_A, gains["swa_q_norm"], dqah)
    dka_raw, dg_swa_k = qk_prep_bwd(tag + "_dka_norm", proj3, _C_KA // HEAD_PAD, N_KV_HEADS_A, HEAD_DIM_A, gains["swa_k_norm"], dkah)
    dqbh, dkbh, dvb, carried["dmla"] = mla_bwd(tag + "_dmla", sv["qbh"], sv["kbh"], sv["vb"], sv["ob"], dob, sv["lse"], plans.get("dmla"))
    dqb_raw, dg_mla_q = qk_prep_bwd(tag + "_dqb_norm", sv["qb_raw"], 0, N_HEADS_B, QK_DIM_B, gains["mla_q_norm"], dqbh, aux["rope"])
    dkb_raw, dkr_sum, dg_mla_k = qk_prep_bwd(tag + "_dkb_norm", sv["kb_raw"], 0, N_HEADS_B, QK_DIM_B, gains["mla_k_norm"], dkbh,
                                             aux["rope"], extra=kr, head_sum=True)
    dq_tok, dk_tok, dv_tok = dqb_raw.reshape(t, -1), dkb_raw.reshape(t, -1), dvb.reshape(t, -1)
    dwuq = mm_tn(tag + "_dw_uq", sv["cqn"], dq_tok, tk=1024, out_dtype=BF16)
    dwk = mm_tn(tag + "_dw_uk", sv["ckvn"], dk_tok, tk=1024, out_dtype=BF16)
    dwv = mm_tn(tag + "_dw_uv", sv["ckvn"], dv_tok, tk=1024, out_dtype=BF16)
    dcqn = mm_nt(tag + "_d_uq", dq_tok, w["wuq"], tm=1024)
    dckvn = mm_nt(tag + "_d_uv", dv_tok, w["wv"], tm=1024, res=mm_nt(tag + "_d_uk", dk_tok, w["wk"], tm=1024))
    dcq, dg_q_lora = rms_bwd(tag + "_drms_cq", proj, sm["mla_q_lora_norm"], dcqn, want_f32=False, tm=1024, col_blk=_C_CQ // Q_LORA_RANK)
    dckv, dg_kv_lora = rms_bwd(tag + "_drms_ckv", proj, sm["mla_kv_lora_norm"], dckvn, want_f32=False, tm=1024,
                               col_blk=_C_CKV // KV_LORA_RANK)
    dproj = jnp.concatenate([dqa_raw.reshape(t, -1), dka_raw.reshape(t, -1), dva.reshape(t, -1).astype(BF16), dcq, dckv,
                             dkr_sum.reshape(t, HEAD_PAD).astype(BF16), dga, dgb], axis=1)
    dwin = mm_tn(tag + "_dw_in", sv["hb"], dproj, tm=1024, tn=2048, tk=1024, out_dtype=BF16)
    dh = mm_nt(tag + "_d_in", dproj, w["win"], tm=1024, tn=1024, tk=2048)
    dx1, dx1b, dg_mix = rms_bwd(tag + "_drms", sv["x1"], sm["mix_norm"], dh, dres=dx2)
    wgrads = dict(win=dwin, wuq=dwuq, wk=dwk, wv=dwv, wa=dwa, wb=dwb, wo=dwo)
    sgrads = dict(mix_norm=dg_mix, swa_q_norm=dg_swa_q[:, :HEAD_DIM_A], swa_k_norm=dg_swa_k[:, :HEAD_DIM_A],
                  swa_sinks=dsinks[:, 0].reshape(1, -1), mla_q_lora_norm=dg_q_lora, mla_kv_lora_norm=dg_kv_lora,
                  mla_q_norm=dg_mla_q[:, :QK_DIM_B], mla_k_norm=dg_mla_k[:, :QK_DIM_B])
    return dx1, dx1b, wgrads, sgrads, carried


def _layer_weights(win4, uq4, ukv4, branch4):
    wr = _join_cols(win4)
    kr = jnp.pad(wr[:, _R_KR:_R_GA], ((0, 0), (QK_NOPE_DIM, HEAD_PAD - QK_DIM_B)))
    win = jnp.concatenate([_pad_head_cols(wr[:, :_R_KA], N_HEADS_A), _pad_head_cols(wr[:, _R_KA:_R_VA], N_KV_HEADS_A),
                           _pad_head_cols(wr[:, _R_VA:_R_CQ], N_KV_HEADS_A), wr[:, _R_CQ:_R_KR], kr, wr[:, _R_GA:]], axis=1)
    ukv = _join_cols(ukv4)
    ukv3 = ukv.reshape(ukv.shape[0], N_HEADS_B, QK_NOPE_DIM + V_DIM_B)
    r = branch4.shape[1] // 2
    return dict(win=win, wuq=_pad_head_cols(_join_cols(uq4), N_HEADS_B),
                wk=_pad_head_cols(ukv3[:, :, :QK_NOPE_DIM].reshape(ukv.shape[0], -1), N_HEADS_B),
                wv=_pad_head_cols(ukv3[:, :, QK_NOPE_DIM:].reshape(ukv.shape[0], -1), N_HEADS_B),
                wa=_pad_head_rows(_join_cols(branch4[:, :r]), N_HEADS_A), wb=_pad_head_rows(_join_cols(branch4[:, r:]), N_HEADS_B))


def _mixer_grad_shards(g):
    dw = g["win"]
    win_ref = jnp.concatenate([_unpad_head_cols(dw[:, :_C_KA], N_HEADS_A, HEAD_DIM_A),
                               _unpad_head_cols(dw[:, _C_KA:_C_VA], N_KV_HEADS_A, HEAD_DIM_A),
                               _unpad_head_cols(dw[:, _C_VA:_C_CQ], N_KV_HEADS_A, HEAD_DIM_A), dw[:, _C_CQ:_C_KR],
                               dw[:, _C_KR + QK_NOPE_DIM:_C_KR + QK_DIM_B], dw[:, _C_GA:]], axis=1)
    rk = g["wk"].shape[0]
    ukv = jnp.concatenate([g["wk"].reshape(rk, N_HEADS_B, HEAD_PAD)[:, :, :QK_NOPE_DIM],
                           g["wv"].reshape(rk, N_HEADS_B, HEAD_PAD)[:, :, :V_DIM_B]], axis=2).reshape(rk, -1)
    return (_split_cols(win_ref), _split_cols(_unpad_head_cols(g["wuq"], N_HEADS_B, QK_DIM_B)), _split_cols(ukv),
            jnp.concatenate([_split_cols(_unpad_head_rows(g["wa"], N_HEADS_A, HEAD_DIM_A)),
                             _split_cols(_unpad_head_rows(g["wb"], N_HEADS_B, V_DIM_B))], axis=1))


BUFFER_TAGS = ("col1", "col2", "down1", "down2", "wo", "win", "uq", "ukv", "branch")
FIRST_GATHER = ("col1", "down1")
GATHER_CARRIERS = {(0, "ffn1_up"): (0, ("wo", "win", "uq", "ukv", "branch")), (0, "proj"): (1, ("col1",)),
                   (0, "swa"): (1, ("down1", "wo", "uq", "ukv")), (0, "mla"): (0, ("col2", "down2")),
                   (0, "ffn2_up"): (1, ("win", "branch")), (1, "ffn1_up"): (1, ("col2",)), (1, "swa"): (1, ("down2",))}
SCATTER_CARRIERS = {(0, "ffn2_dh"): (1, ("col1",)), (0, "ffn2_dwgu"): (1, ("col2",)), (0, "ffn2_dn"): (1, ("down1", "down2", "wo")),
                    (0, "dswa"): (1, ("win", "uq", "ukv", "branch")), (0, "dmla"): (0, ("col2", "down2")),
                    (0, "ffn1_dh"): (0, ("wo", "win", "uq", "ukv", "branch"))}
LAST_SCATTER = ("col1", "down1")


def _pick(by_tag, tags):
    return [by_tag[t] for t in tags]


def _train_step(x, positions, target, packs, small, shard):
    depth = 2
    b, s, d = x.shape
    t = b * s
    posf = positions.astype(F32)
    half = QK_ROPE_DIM // 2
    inv_freq = ROPE_BASE ** (-jnp.arange(half, dtype=F32) / half)
    ang = posf[..., None] * inv_freq
    cos, sin = jnp.cos(ang), jnp.sin(ang)
    tail = HEAD_PAD - QK_DIM_B
    rope = (jnp.concatenate([jnp.ones((b, s, QK_NOPE_DIM), F32), cos, cos, jnp.ones((b, s, tail), F32)], axis=-1),
            jnp.concatenate([jnp.zeros((b, s, QK_NOPE_DIM), F32), sin, sin, jnp.zeros((b, s, tail), F32)], axis=-1))
    slopes = jnp.exp2(-8.0 * (jnp.arange(N_HEADS_A, dtype=F32) + 1.0) / N_HEADS_A)
    aux = dict(b=b, s=s, pos_col=posf.reshape(b, s, 1), pos_row=posf.reshape(b, 1, s), rope=rope, slopes=slopes)

    def sm_of(l):
        return {n: small[n][l:l + 1] for n in SMALL_NAMES}

    by_tag = dict(zip(BUFFER_TAGS, packs))
    wts, parts, slots = [{} for _ in range(depth)], [{} for _ in range(depth)], [{} for _ in range(depth)]
    saved, sgrads = [None] * depth, [None] * depth

    def gather_piece(l, kern):
        if (l, kern) not in GATHER_CARRIERS:
            return None
        data_l, tags = GATHER_CARRIERS[(l, kern)]
        return gather_plan(_pick(by_tag, tags), data_l)

    def gathered(l, kern, arrays):
        if (l, kern) in GATHER_CARRIERS:
            data_l, tags = GATHER_CARRIERS[(l, kern)]
            wts[data_l].update(zip(tags, arrays))

    def scatter_piece(l, kern):
        if (l, kern) not in SCATTER_CARRIERS:
            return None
        data_l, tags = SCATTER_CARRIERS[(l, kern)]
        return scatter_plan(_pick(parts[data_l], tags), data_l)

    def scattered(l, kern, arrays):
        if (l, kern) in SCATTER_CARRIERS:
            data_l, tags = SCATTER_CARRIERS[(l, kern)]
            slots[data_l].update(zip(tags, arrays))

    def chip_sums(l, tags, mine, theirs=None):
        if theirs is None:
            theirs = comm_call(f"swap_l{l}_{tags[0]}", swap_plan(_pick(mine, tags), l))
        for tag, o in zip(tags, theirs):
            g = mine[tag]
            parts[l][tag] = add_pair(f"l{l}_add_{tag}", g.reshape(-1, g.shape[-1]), o.reshape(-1, o.shape[-1])).reshape(g.shape)

    wts[0].update(zip(FIRST_GATHER, comm_call("gather_first", gather_plan(_pick(by_tag, FIRST_GATHER), 0))))
    h = x.reshape(t, d)
    for l in range(depth):
        sm, wl = sm_of(l), wts[l]
        h, s1, got = _ffn_fwd(f"l{l}_ffn1", h, sm["ffn1_norm"], wl["col1"], wl["down1"], gather_piece(l, "ffn1_up"))
        gathered(l, "ffn1_up", got)
        weights = _layer_weights(wl["win"], wl["uq"], wl["ukv"], wl["branch"])
        h, s2, got = _mixer_fwd(f"l{l}_mix", h, sm, weights, wl["wo"], aux, {k: gather_piece(l, k) for k in ("proj", "swa", "mla")})
        for kern, arrays in got.items():
            gathered(l, kern, arrays)
        h, s3, got = _ffn_fwd(f"l{l}_ffn2", h, sm["ffn2_norm"], wl["col2"], wl["down2"], gather_piece(l, "ffn2_up"))
        gathered(l, "ffn2_up", got)
        saved[l] = (s1, s2, s3, weights)
    loss, dh, dhb = loss_fwd_bwd("loss", h, target.reshape(t, d))

    for l in reversed(range(depth)):
        sm, wl = sm_of(l), wts[l]
        s1, s2, s3, weights = saved[l]
        staged = l == 0
        dh, dhb, dg_f2, gcol2, gdown2, got = _ffn_bwd(f"l{l}_ffn2", s3, sm["ffn2_norm"], wl["col2"], wl["down2"], dh, dhb,
                                                      {k: scatter_piece(l, "ffn2_" + k) for k in ("dh", "dwgu", "dn")})
        for kern, arrays in got.items():
            scattered(l, "ffn2_" + kern, arrays)
        mine = dict(col2=gcol2, down2=gdown2)
        if staged:
            chip_sums(l, ("col2", "down2"), mine)
        dh, dhb, wg, sg, got = _mixer_bwd(f"l{l}_mix", s2, sm, weights, wl["wo"], aux, dh, dhb,
                                          {k: scatter_piece(l, k) for k in ("dswa", "dmla")})
        for kern, arrays in got.items():
            scattered(l, kern, arrays)
        gwin, guq, gukv, gbranch = _mixer_grad_shards(wg)
        mine.update(wo=wg["wo"], win=gwin, uq=guq, ukv=gukv, branch=gbranch)
        if staged:
            chip_sums(l, ("wo", "win", "uq", "ukv", "branch"), mine)
        last_tags = LAST_SCATTER if staged else BUFFER_TAGS

        def last_swap(gcol1, gdown1, l=l, mine=mine, last_tags=last_tags):
            return swap_plan(_pick(dict(mine, col1=gcol1, down1=gdown1), last_tags), l)

        plans = {k: scatter_piece(l, "ffn1_" + k) for k in ("dh", "dwgu")}
        plans["dn"] = last_swap
        dh, dhb, dg_f1, gcol1, gdown1, got = _ffn_bwd(f"l{l}_ffn1", s1, sm["ffn1_norm"], wl["col1"], wl["down1"], dh, dhb, plans)
        theirs = got.pop("dn")
        for kern, arrays in got.items():
            scattered(l, "ffn1_" + kern, arrays)
        mine.update(col1=gcol1, down1=gdown1)
        chip_sums(l, last_tags, mine, theirs)
        sg.update(ffn1_norm=dg_f1, ffn2_norm=dg_f2)
        sgrads[l] = sg
    slots[0].update(zip(LAST_SCATTER, comm_call("scatter_last", scatter_plan(_pick(parts[0], LAST_SCATTER), 0))))
    finals = [[sum_slots(f"l{l}_sum_{tag}", slots[l][tag], parts[l][tag], shard) for tag in BUFFER_TAGS] for l in range(depth)]
    shared = comm_call("share", share_plan(finals))
    nk = len(BUFFER_TAGS)
    return loss, dh.reshape(b, s, d), [dict(zip(BUFFER_TAGS, shared[l * nk:(l + 1) * nk])) for l in range(depth)], sgrads


def kernel(x, positions, ffn1_norm, ffn1_w_gate, ffn1_w_up, ffn1_w_down, mix_norm, w_in, swa_q_norm, swa_k_norm, swa_sinks, mla_q_lora_norm, mla_w_uq, mla_kv_lora_norm, mla_w_ukv, mla_q_norm, mla_k_norm, w_branch_a, w_branch_b, w_out, ffn2_norm, ffn2_w_gate, ffn2_w_up, ffn2_w_down, loss_target, m_ffn1_norm, m_ffn1_w_gate, m_ffn1_w_up, m_ffn1_w_down, m_mix_norm, m_w_in, m_swa_q_norm, m_swa_k_norm, m_swa_sinks, m_mla_q_lora_norm, m_mla_w_uq, m_mla_kv_lora_norm, m_mla_w_ukv, m_mla_q_norm, m_mla_k_norm, m_w_branch_a, m_w_branch_b, m_w_out, m_ffn2_norm, m_ffn2_w_gate, m_ffn2_w_up, m_ffn2_w_down, v_ffn1_norm, v_ffn1_w_gate, v_ffn1_w_up, v_ffn1_w_down, v_mix_norm, v_w_in, v_swa_q_norm, v_swa_k_norm, v_swa_sinks, v_mla_q_lora_norm, v_mla_w_uq, v_mla_kv_lora_norm, v_mla_w_ukv, v_mla_q_norm, v_mla_k_norm, v_w_branch_a, v_w_branch_b, v_w_out, v_ffn2_norm, v_ffn2_w_gate, v_ffn2_w_up, v_ffn2_w_down):
    args = (x, positions, ffn1_norm, ffn1_w_gate, ffn1_w_up, ffn1_w_down, mix_norm, w_in, swa_q_norm, swa_k_norm, swa_sinks, mla_q_lora_norm, mla_w_uq, mla_kv_lora_norm, mla_w_ukv, mla_q_norm, mla_k_norm, w_branch_a, w_branch_b, w_out, ffn2_norm, ffn2_w_gate, ffn2_w_up, ffn2_w_down, loss_target, m_ffn1_norm, m_ffn1_w_gate, m_ffn1_w_up, m_ffn1_w_down, m_mix_norm, m_w_in, m_swa_q_norm, m_swa_k_norm, m_swa_sinks, m_mla_q_lora_norm, m_mla_w_uq, m_mla_kv_lora_norm, m_mla_w_ukv, m_mla_q_norm, m_mla_k_norm, m_w_branch_a, m_w_branch_b, m_w_out, m_ffn2_norm, m_ffn2_w_gate, m_ffn2_w_up, m_ffn2_w_down, v_ffn1_norm, v_ffn1_w_gate, v_ffn1_w_up, v_ffn1_w_down, v_mix_norm, v_w_in, v_swa_q_norm, v_swa_k_norm, v_swa_sinks, v_mla_q_lora_norm, v_mla_w_uq, v_mla_kv_lora_norm, v_mla_w_ukv, v_mla_q_norm, v_mla_k_norm, v_w_branch_a, v_w_branch_b, v_w_out, v_ffn2_norm, v_ffn2_w_gate, v_ffn2_w_up, v_ffn2_w_down)
    a = dict(zip(ARG_NAMES, args, strict=True))
    x = a["x"]
    depth = a["ffn1_norm"].shape[0]
    d = x.shape[-1]
    assert depth == 2, "the exchanges of layer l are driven by core l of every chip: two layers, two cores"

    def bf16_rows(*names):
        return jnp.concatenate([a[n] for n in names], axis=1).astype(BF16) if len(names) > 1 else a[names[0]].astype(BF16)

    packs = [bf16_rows("ffn1_w_gate", "ffn1_w_up"), bf16_rows("ffn2_w_gate", "ffn2_w_up"), bf16_rows("ffn1_w_down"),
             bf16_rows("ffn2_w_down"), bf16_rows("w_out"), bf16_rows("w_in"), bf16_rows("mla_w_uq"), bf16_rows("mla_w_ukv"),
             bf16_rows("w_branch_a", "w_branch_b")]
    shard_me = (2 * lax.axis_index("x") + lax.axis_index("y")).astype(jnp.int32).reshape(1)
    small = {n: a[n] for n in SMALL_NAMES}

    loss, grad_x, summed, sgrads = _train_step(x, a["positions"], a["loss_target"], packs, small, shard_me)

    full = {tag: jnp.stack([summed[l][tag] for l in range(depth)]) for tag in BUFFER_TAGS}
    r_branch = a["w_branch_a"].shape[1]
    grads = dict(ffn1_w_gate=full["col1"][:, :d], ffn1_w_up=full["col1"][:, d:], ffn2_w_gate=full["col2"][:, :d],
                 ffn2_w_up=full["col2"][:, d:], ffn1_w_down=full["down1"], ffn2_w_down=full["down2"], w_out=full["wo"],
                 w_in=full["win"], mla_w_uq=full["uq"], mla_w_ukv=full["ukv"], w_branch_a=full["branch"][:, :r_branch],
                 w_branch_b=full["branch"][:, r_branch:])

    flat = jnp.concatenate([jnp.concatenate([sgrads[l][n].reshape(-1) for l in range(depth)]) for n in SMALL_NAMES] + [loss.reshape(-1)])
    n_small = flat.shape[0]
    rows = -(-n_small // (8 * LANES)) * 8
    pad = rows * LANES - n_small

    def small_pack(v):
        return jnp.pad(v, (0, pad)).reshape(rows, LANES)

    total = small_allreduce(small_pack(flat))
    w_s, m_s, v_s = (small_pack(jnp.concatenate([a[p + n].reshape(-1) for n in SMALL_NAMES] + [jnp.zeros((1,), F32)]))
                     for p in ("", "m_", "v_"))
    d_s, nm_s, nv_s = adamw("adamw_small", w_s, total, m_s, v_s)

    def small_unpack(buf):
        out, off, flat_b = {}, 0, buf.reshape(-1)
        for n in SMALL_NAMES:
            size = a[n].shape[0] * a[n].shape[1]
            out[n] = flat_b[off:off + size].reshape(a[n].shape)
            off += size
        return out

    grads.update(small_unpack(total))
    delta, new_m, new_v = small_unpack(d_s), small_unpack(nm_s), small_unpack(nv_s)
    for n in PACK_NAMES:
        flip = a[n].shape[2] % LANES != 0

        def view(z):
            return z.transpose(0, 2, 1) if flip else z

        w_v, g_v, m_v, v_v = view(a[n]), view(grads[n]), view(a["m_" + n]), view(a["v_" + n])
        shp = w_v.shape
        two_d = (shp[0] * shp[1], shp[2])
        dn, mn, vn = adamw("adamw_" + n, w_v.reshape(two_d), g_v.reshape(two_d), m_v.reshape(two_d), v_v.reshape(two_d))
        delta[n], new_m[n], new_v[n] = view(dn.reshape(shp)), view(mn.reshape(shp)), view(vn.reshape(shp))
        if flip:
            grads[n] = view(g_v)

    loss_out = total.reshape(-1)[n_small - 1]
    return (loss_out, grad_x, *[grads[n] for n in WEIGHT_NAMES], *[delta[n] for n in WEIGHT_NAMES],
            *[new_m[n] for n in WEIGHT_NAMES], *[new_v[n] for n in WEIGHT_NAMES])
```

```python
import functools

import numpy as np
import jax
import jax.numpy as jnp
from jax import lax
from jax.experimental import pallas as pl
from jax.experimental.pallas import tpu as pltpu

F32 = jnp.float32
BF16 = jnp.bfloat16
MESH = pl.DeviceIdType.MESH

HEAD_DIM_A = 64
N_HEADS_A = 8
N_KV_HEADS_A = 2
GROUP_A = N_HEADS_A // N_KV_HEADS_A
BLOCK = 128
N_HEADS_B = 8
Q_LORA_RANK = 256
KV_LORA_RANK = 128
QK_NOPE_DIM = 64
QK_ROPE_DIM = 32
QK_DIM_B = QK_NOPE_DIM + QK_ROPE_DIM
V_DIM_B = 64
ROPE_BASE = 10000.0
WIDTH_A = N_HEADS_A * HEAD_DIM_A
KV_WIDTH_A = N_KV_HEADS_A * HEAD_DIM_A
EPS = 1e-6
NEG = -1e30
ADAM_LR = 0.001
ADAM_B1 = 0.9
ADAM_B2 = 0.999
ADAM_EPS = 1e-08
ADAM_WD = 0.01
ADAM_STEP = 10

N_SHARDS = 4
N_DEV = 8
LANES = 128
VMEM_LIMIT = 48 * 1024 * 1024

PACK_NAMES = ("ffn1_w_gate", "ffn1_w_up", "ffn1_w_down", "w_in", "mla_w_uq", "mla_w_ukv",
              "w_branch_a", "w_branch_b", "w_out", "ffn2_w_gate", "ffn2_w_up", "ffn2_w_down")
SMALL_NAMES = ("ffn1_norm", "mix_norm", "swa_q_norm", "swa_k_norm", "swa_sinks", "mla_q_lora_norm",
               "mla_kv_lora_norm", "mla_q_norm", "mla_k_norm", "ffn2_norm")
WEIGHT_NAMES = ("ffn1_norm", "ffn1_w_gate", "ffn1_w_up", "ffn1_w_down", "mix_norm", "w_in", "swa_q_norm",
                "swa_k_norm", "swa_sinks", "mla_q_lora_norm", "mla_w_uq", "mla_kv_lora_norm", "mla_w_ukv",
                "mla_q_norm", "mla_k_norm", "w_branch_a", "w_branch_b", "w_out", "ffn2_norm", "ffn2_w_gate",
                "ffn2_w_up", "ffn2_w_down")


def _params(sem):
    return pltpu.CompilerParams(dimension_semantics=sem, vmem_limit_bytes=VMEM_LIMIT)


def _tile(n, want, align):
    if n <= want:
        return n
    t = (want // align) * align
    while t > align and n % t:
        t -= align
    assert t >= align and n % t == 0, (n, want, align)
    return t


def _mm_call(name, a, b, out_struct, grid, a_spec, b_spec, o_spec, dims, n_red, acc_shape, alpha=1.0, res=None,
             inner=0, a_cols=0, plan=None):
    if n_red and all(g == 1 for g in grid[len(grid) - n_red:]):
        n_red = 0
    n_par = len(grid) - n_red
    n_in = 2 + int(res is not None)
    p_ins, p_in_specs, p_outs, p_out_specs, p_scratch, semantics, at_start, at_end = _hosted(
        plan, n_in, 1, grid, ("parallel",) * n_par + ("arbitrary",) * n_red)

    def body(*refs):
        at_start(refs)
        compute(refs)
        at_end(refs)

    def compute(refs):
        a_ref, b_ref = refs[:2]
        r_ref = refs[2] if res is not None else None
        o_ref, acc_ref = refs[n_in + len(p_ins)], refs[n_in + len(p_ins) + 1 + len(p_outs)]
        if inner:
            def a_of(s):
                return a_ref[:, s * a_cols:(s + 1) * a_cols] if a_cols else a_ref[s]

            part = lax.dot_general(a_of(0), b_ref[0], dims, preferred_element_type=F32)
            for s in range(1, inner):
                part = part + lax.dot_general(a_of(s), b_ref[s], dims, preferred_element_type=F32)
        else:
            part = lax.dot_general(a_ref[...], b_ref[...], dims, preferred_element_type=F32)

        def finish(total):
            if alpha != 1.0:
                total = total * alpha
            if r_ref is not None:
                total = r_ref[...] + total
            o_ref[...] = total.astype(o_ref.dtype)

        if n_red == 0:
            finish(part)
            return
        ids = [pl.program_id(n_par + i) for i in range(n_red)]
        first = functools.reduce(jnp.logical_and, [i == 0 for i in ids])
        last = functools.reduce(jnp.logical_and, [i == grid[n_par + k] - 1 for k, i in enumerate(ids)])

        @pl.when(first)
        def _():
            acc_ref[...] = part

        @pl.when(jnp.logical_not(first))
        def _():
            acc_ref[...] += part

        @pl.when(last)
        def _():
            finish(acc_ref[...])

    in_specs = [a_spec, b_spec] + ([o_spec] if res is not None else [])
    args = (a, b) + ((res,) if res is not None else ())
    if plan is None:
        return pl.pallas_call(
            body, name=name, out_shape=out_struct, grid=grid, in_specs=in_specs, out_specs=o_spec,
            scratch_shapes=[pltpu.VMEM(acc_shape, F32)], compiler_params=_params(semantics),
        )(*args)
    res_all = pl.pallas_call(
        body, name=name, out_shape=(out_struct,) + tuple(p_outs), grid=grid, in_specs=in_specs + p_in_specs,
        out_specs=(o_spec,) + tuple(p_out_specs), scratch_shapes=[pltpu.VMEM(acc_shape, F32)] + p_scratch,
        compiler_params=_params(semantics),
    )(*args, *p_ins)
    return res_all[0], list(res_all[1:])


_NN = (((1,), (0,)), ((), ()))
_NT = (((1,), (1,)), ((), ()))
_TN = (((0,), (0,)), ((), ()))


def mm_nn(name, a, b, tm=512, tn=1024, tk=1024, out_dtype=F32, alpha=1.0, res=None, plan=None):
    (m, k), (_, n) = a.shape, b.shape
    tm, tn, tk = _tile(m, tm, 16), _tile(n, tn, LANES), _tile(k, tk, LANES)
    return _mm_call(name, a, b, jax.ShapeDtypeStruct((m, n), out_dtype), (m // tm, n // tn, k // tk),
                    pl.BlockSpec((tm, tk), lambda i, j, kk: (i, kk)), pl.BlockSpec((tk, tn), lambda i, j, kk: (kk, j)),
                    pl.BlockSpec((tm, tn), lambda i, j, kk: (i, j)), _NN, 1, (tm, tn), alpha, res, plan=plan)


def mm_nt(name, a, b, tm=512, tn=1024, tk=1024, out_dtype=F32, alpha=1.0, res=None):
    (m, n), (k, _) = a.shape, b.shape
    tm, tn, tk = _tile(m, tm, 16), _tile(k, tn, LANES), _tile(n, tk, LANES)
    return _mm_call(name, a, b, jax.ShapeDtypeStruct((m, k), out_dtype), (m // tm, k // tn, n // tk),
                    pl.BlockSpec((tm, tk), lambda i, j, kk: (i, kk)), pl.BlockSpec((tn, tk), lambda i, j, kk: (j, kk)),
                    pl.BlockSpec((tm, tn), lambda i, j, kk: (i, j)), _NT, 1, (tm, tn), alpha, res)


def mm_tn(name, a, b, tm=1024, tn=1024, tk=1024, out_dtype=F32, alpha=1.0):
    (m, k), (_, n) = a.shape, b.shape
    tm, tn, tk = _tile(k, tm, LANES), _tile(n, tn, LANES), _tile(m, tk, 16)
    return _mm_call(name, a, b, jax.ShapeDtypeStruct((k, n), out_dtype), (k // tm, n // tn, m // tk),
                    pl.BlockSpec((tk, tm), lambda i, j, kk: (kk, i)), pl.BlockSpec((tk, tn), lambda i, j, kk: (kk, j)),
                    pl.BlockSpec((tm, tn), lambda i, j, kk: (i, j)), _TN, 1, (tm, tn), alpha)


def ffn_up_act(name, a, w, blk, tm=1024, plan=None):
    (m, k), (ns, _, n) = a.shape, w.shape
    tm = _tile(m, tm, 16)
    grid = (ns, m // tm)
    p_ins, p_in_specs, p_outs, p_out_specs, p_scratch, semantics, at_start, at_end = _hosted(plan, 3, 2, grid, ("parallel", "parallel"))

    def body(*refs):
        a_ref, wg_ref, wu_ref = refs[:3]
        gu_ref, h_ref = refs[3 + len(p_ins):5 + len(p_ins)]
        at_start(refs)
        av = a_ref[...]
        gate = jnp.dot(av, wg_ref[...], preferred_element_type=F32)
        up = jnp.dot(av, wu_ref[...], preferred_element_type=F32)
        gu_ref[0] = gate.astype(BF16)
        gu_ref[1] = up.astype(BF16)
        h_ref[...] = (gate * jax.nn.sigmoid(gate) * up).astype(BF16)
        at_end(refs)

    res = pl.pallas_call(
        body, name=name, grid=grid,
        out_shape=(jax.ShapeDtypeStruct((2, ns, m, n), BF16), jax.ShapeDtypeStruct((ns, m, n), BF16)) + tuple(p_outs),
        in_specs=[pl.BlockSpec((tm, k), lambda s, i: (i, 0)), pl.BlockSpec((None, k, n), lambda s, i: (s, blk, 0)),
                  pl.BlockSpec((None, k, n), lambda s, i: (s, blk + 1, 0))] + p_in_specs,
        out_specs=(pl.BlockSpec((2, None, tm, n), lambda s, i: (0, s, i, 0)), pl.BlockSpec((None, tm, n), lambda s, i: (s, i, 0)))
        + tuple(p_out_specs),
        scratch_shapes=p_scratch, compiler_params=_params(semantics),
    )(a, w, w, *p_ins)
    return res[0], res[1], list(res[2:])


def ffn_down_dact(name, a, w, gu, n, blk, alpha, tm=1024, plan=None):
    (m, d), ns = a.shape, w.shape[0]
    tm = _tile(m, tm, 16)
    grid = (ns, m // tm)
    p_ins, p_in_specs, p_outs, p_out_specs, p_scratch, semantics, at_start, at_end = _hosted(plan, 3, 1, grid, ("parallel", "parallel"))

    def body(*refs):
        a_ref, w_ref, gu_ref = refs[:3]
        o_ref = refs[3 + len(p_ins)]
        at_start(refs)
        dh = lax.dot_general(a_ref[...], w_ref[...], _NT, preferred_element_type=F32) * alpha
        gate, up = gu_ref[0].astype(F32), gu_ref[1].astype(F32)
        s = jax.nn.sigmoid(gate)
        o_ref[0] = (dh * up * (s * (1.0 + gate * (1.0 - s)))).astype(BF16)
        o_ref[1] = (dh * (gate * s)).astype(BF16)
        at_end(refs)

    gu_spec = pl.BlockSpec((2, None, tm, n), lambda s, i: (0, s, i, 0))
    res = pl.pallas_call(
        body, name=name, grid=grid, out_shape=(jax.ShapeDtypeStruct((2, ns, m, n), BF16),) + tuple(p_outs),
        in_specs=[pl.BlockSpec((tm, d), lambda s, i: (i, 0)), pl.BlockSpec((None, n, d), lambda s, i: (s, blk, 0)), gu_spec] + p_in_specs,
        out_specs=(gu_spec,) + tuple(p_out_specs), scratch_shapes=p_scratch, compiler_params=_params(semantics),
    )(a, w, gu, *p_ins)
    return res[0], list(res[1:])


def gmm_up_dw(name, a, dgu, tk=2048, plan=None):
    (m, k), (_, ns, _, n) = a.shape, dgu.shape
    tk = _tile(m, tk, 16)
    return _mm_call(name, a, dgu, jax.ShapeDtypeStruct((ns, 2 * k, n), BF16), (2, ns, m // tk),
                    pl.BlockSpec((tk, k), lambda j, s, kk: (kk, 0)), pl.BlockSpec((None, None, tk, n), lambda j, s, kk: (j, s, kk, 0)),
                    pl.BlockSpec((None, k, n), lambda j, s, kk: (s, j, 0)), _TN, 1, (k, n), plan=plan)


def gmm_up_dx(name, dgu, w, k, blk, tm=512, plan=None):
    _, ns, m, n = dgu.shape
    tm = _tile(m, tm, 16)
    return _mm_call(name, dgu, w, jax.ShapeDtypeStruct((m, k), F32), (m // tm, 2),
                    pl.BlockSpec((None, ns, tm, n), lambda i, j: (j, 0, i, 0)),
                    pl.BlockSpec((ns, k, n), lambda i, j: (0, blk + j, 0)),
                    pl.BlockSpec((tm, k), lambda i, j: (i, 0)), _NT, 1, (tm, k), inner=ns, plan=plan)


def gmm_down(name, h, w, blk, res, alpha, tm=512):
    (ns, m, n), d = h.shape, w.shape[2]
    tm = _tile(m, tm, 16)
    return _mm_call(name, h, w, jax.ShapeDtypeStruct((m, d), F32), (m // tm,),
                    pl.BlockSpec((ns, tm, n), lambda i: (0, i, 0)), pl.BlockSpec((ns, n, d), lambda i: (0, blk, 0)),
                    pl.BlockSpec((tm, d), lambda i: (i, 0)), _NN, 0, (8, LANES), alpha, res, inner=ns)


def gmm_down_dw(name, h, b, alpha, tk=2048):
    (ns, m, n), d = h.shape, b.shape[1]
    tk = _tile(m, tk, 16)
    return _mm_call(name, h, b, jax.ShapeDtypeStruct((ns, n, d), BF16), (ns, m // tk),
                    pl.BlockSpec((None, tk, n), lambda s, kk: (s, kk, 0)), pl.BlockSpec((tk, d), lambda s, kk: (kk, 0)),
                    pl.BlockSpec((None, n, d), lambda s, kk: (s, 0, 0)), _TN, 1, (n, d), alpha)


def gmm_rows(name, a, w, res, tm=1024):
    (m, _), (ns, r, d) = a.shape, w.shape
    tm = _tile(m, tm, 16)
    return _mm_call(name, a, w, jax.ShapeDtypeStruct((m, d), F32), (m // tm,),
                    pl.BlockSpec((tm, ns * r), lambda i: (i, 0)), pl.BlockSpec((ns, r, d), lambda i: (0, 0, 0)),
                    pl.BlockSpec((tm, d), lambda i: (i, 0)), _NN, 0, (8, LANES), 1.0, res, inner=ns, a_cols=r)


def gmm_rows_dx(name, a, w, tm=1024, plan=None):
    (m, d), (ns, r, _) = a.shape, w.shape
    tm = _tile(m, tm, 16)
    return _mm_call(name, a, w, jax.ShapeDtypeStruct((m, ns * r), F32), (ns, m // tm),
                    pl.BlockSpec((tm, d), lambda s, i: (i, 0)), pl.BlockSpec((None, r, d), lambda s, i: (s, 0, 0)),
                    pl.BlockSpec((tm, r), lambda s, i: (i, s)), _NT, 0, (8, LANES), plan=plan)


def gmm_rows_dw(name, a, b, tk=2048):
    (m, da), d = a.shape, b.shape[1]
    ns, r = N_SHARDS, da // N_SHARDS
    tk = _tile(m, tk, 16)
    return _mm_call(name, a, b, jax.ShapeDtypeStruct((ns, r, d), BF16), (ns, m // tk),
                    pl.BlockSpec((tk, r), lambda s, kk: (kk, s)), pl.BlockSpec((tk, d), lambda s, kk: (kk, 0)),
                    pl.BlockSpec((None, r, d), lambda s, kk: (s, 0, 0)), _TN, 1, (r, d))


def rms_fwd(name, x, gain, tm=1024, col_blk=0):
    m, d = x.shape[0], gain.shape[1]
    tm = _tile(m, tm, 16)

    def body(x_ref, g_ref, o_ref):
        xv = x_ref[...]
        r = lax.rsqrt(jnp.mean(xv * xv, axis=-1, keepdims=True) + EPS)
        o_ref[...] = (xv * r * g_ref[...]).astype(o_ref.dtype)

    return pl.pallas_call(
        body, name=name, out_shape=jax.ShapeDtypeStruct((m, d), BF16), grid=(m // tm,),
        in_specs=[pl.BlockSpec((tm, d), lambda i: (i, col_blk)), pl.BlockSpec((1, d), lambda i: (0, 0))],
        out_specs=pl.BlockSpec((tm, d), lambda i: (i, 0)), compiler_params=_params(("parallel",)),
    )(x, gain)


def rms_bwd(name, x, gain, dn, dres=None, want_f32=True, want_bf16=True, tm=512, col_blk=0):
    m, d = x.shape[0], gain.shape[1]
    tm = _tile(m, tm, 16)
    n_out = int(want_f32) + int(want_bf16)

    def body(*refs):
        x_ref, g_ref, dn_ref = refs[:3]
        pos = 3
        r_ref = None
        if dres is not None:
            r_ref = refs[pos]
            pos += 1
        outs = refs[pos:pos + n_out]
        dg_ref = refs[pos + n_out]
        xv = x_ref[...]
        r = lax.rsqrt(jnp.mean(xv * xv, axis=-1, keepdims=True) + EPS)
        xhat = xv * r
        dnv = dn_ref[...]
        dxhat = dnv * g_ref[...]
        dx = r * (dxhat - xhat * jnp.mean(dxhat * xhat, axis=-1, keepdims=True))
        if r_ref is not None:
            dx = r_ref[...] + dx
        for o in outs:
            o[...] = dx.astype(o.dtype)
        part = jnp.sum(dnv * xhat, axis=0, keepdims=True)

        @pl.when(pl.program_id(0) == 0)
        def _():
            dg_ref[...] = part

        @pl.when(pl.program_id(0) > 0)
        def _():
            dg_ref[...] += part

    row = pl.BlockSpec((tm, d), lambda i: (i, 0))
    vec = pl.BlockSpec((1, d), lambda i: (0, 0))
    out_shape = ([jax.ShapeDtypeStruct((m, d), F32)] if want_f32 else []) + ([jax.ShapeDtypeStruct((m, d), BF16)] if want_bf16 else [])
    res = pl.pallas_call(
        body, name=name, out_shape=tuple(out_shape) + (jax.ShapeDtypeStruct((1, d), F32),), grid=(m // tm,),
        in_specs=[pl.BlockSpec((tm, d), lambda i: (i, col_blk)), vec, row] + ([row] if dres is not None else []),
        out_specs=tuple([row] * n_out) + (vec,), compiler_params=_params(("arbitrary",)),
    )(*((x, gain, dn) + ((dres,) if dres is not None else ())))
    return res


def gate_fwd(name, proj, ya, yb, off_a, off_b, tm=256):
    m, d = ya.shape
    tm = _tile(m, tm, 16)
    blk_a, blk_b = off_a // d, off_b // d
    assert blk_a * d == off_a and blk_b * d == off_b, "the gates start on multiples of their width"

    def body(ga_ref, gb_ref, ya_ref, yb_ref, o_ref):
        o_ref[...] = (jax.nn.sigmoid(ga_ref[...]) * ya_ref[...] + jax.nn.sigmoid(gb_ref[...]) * yb_ref[...]).astype(o_ref.dtype)

    row = pl.BlockSpec((tm, d), lambda i: (i, 0))
    return pl.pallas_call(
        body, name=name, out_shape=jax.ShapeDtypeStruct((m, d), BF16), grid=(m // tm,),
        in_specs=[pl.BlockSpec((tm, d), lambda i: (i, blk_a)), pl.BlockSpec((tm, d), lambda i: (i, blk_b)), row, row], out_specs=row,
        compiler_params=_params(("parallel",)),
    )(proj, proj, ya, yb)


def gate_bwd(name, proj, ya, yb, dmerged, off_a, off_b, tm=256):
    m, d = ya.shape
    tm = _tile(m, tm, 16)
    blk_a, blk_b = off_a // d, off_b // d
    assert blk_a * d == off_a and blk_b * d == off_b, "the gates start on multiples of their width"

    def body(ga_ref, gb_ref, ya_ref, yb_ref, dm_ref, dya_ref, dyb_ref, dga_ref, dgb_ref):
        sa, sb = jax.nn.sigmoid(ga_ref[...]), jax.nn.sigmoid(gb_ref[...])
        dm = dm_ref[...]
        dya_ref[...] = (dm * sa).astype(BF16)
        dyb_ref[...] = (dm * sb).astype(BF16)
        dga_ref[...] = (dm * ya_ref[...] * (sa * (1.0 - sa))).astype(BF16)
        dgb_ref[...] = (dm * yb_ref[...] * (sb * (1.0 - sb))).astype(BF16)

    row = pl.BlockSpec((tm, d), lambda i: (i, 0))
    o = jax.ShapeDtypeStruct((m, d), BF16)
    return pl.pallas_call(
        body, name=name, out_shape=(o, o, o, o), grid=(m // tm,),
        in_specs=[pl.BlockSpec((tm, d), lambda i: (i, blk_a)), pl.BlockSpec((tm, d), lambda i: (i, blk_b)), row, row, row],
        out_specs=(row, row, row, row), compiler_params=_params(("parallel",)),
    )(proj, proj, ya, yb, dmerged)


def loss_fwd_bwd(name, y, target, tm=512):
    m, d = y.shape
    tm = _tile(m, tm, 16)

    def body(y_ref, t_ref, l_ref, dy_ref, dyb_ref):
        err = y_ref[...] - t_ref[...]
        dy = err * (1.0 / d)
        dy_ref[...] = dy
        dyb_ref[...] = dy.astype(BF16)
        part = 0.5 * jnp.sum(jnp.mean(err * err, axis=-1, keepdims=True), axis=0, keepdims=True)

        @pl.when(pl.program_id(0) == 0)
        def _():
            l_ref[...] = part

        @pl.when(pl.program_id(0) > 0)
        def _():
            l_ref[...] += part

    row = pl.BlockSpec((tm, d), lambda i: (i, 0))
    return pl.pallas_call(
        body, name=name, grid=(m // tm,),
        out_shape=(jax.ShapeDtypeStruct((1, 1), F32), jax.ShapeDtypeStruct((m, d), F32), jax.ShapeDtypeStruct((m, d), BF16)),
        in_specs=[row, row], out_specs=(pl.BlockSpec((1, 1), lambda i: (0, 0)), row, row),
        compiler_params=_params(("arbitrary",)),
    )(y, target)


def adamw(name, w, g, m, v):
    r, c = w.shape
    cap = (2 * 1024 * 1024) // 4
    tr, tc = _tile(r, 1024, 8), c
    while tr * tc > cap and tc % (2 * LANES) == 0:
        tc //= 2
    while tr * tc > cap and tr % 16 == 0:
        tr //= 2
    c1 = np.float32(1.0 - ADAM_B1 ** ADAM_STEP)
    c2 = np.float32(1.0 - ADAM_B2 ** ADAM_STEP)

    def body(w_ref, g_ref, m_ref, v_ref, d_ref, nm_ref, nv_ref):
        gv = g_ref[...]
        nm = ADAM_B1 * m_ref[...] + (1.0 - ADAM_B1) * gv
        nv = ADAM_B2 * v_ref[...] + (1.0 - ADAM_B2) * (gv * gv)
        d_ref[...] = -ADAM_LR * ((nm / c1) / (jnp.sqrt(nv / c2) + ADAM_EPS) + ADAM_WD * w_ref[...])
        nm_ref[...] = nm
        nv_ref[...] = nv

    blk = pl.BlockSpec((tr, tc), lambda i, j: (i, j))
    o = jax.ShapeDtypeStruct((r, c), F32)
    return pl.pallas_call(
        body, name=name, out_shape=(o, o, o), grid=(r // tr, c // tc), in_specs=[blk] * 4, out_specs=(blk, blk, blk),
        compiler_params=_params(("parallel", "parallel")),
    )(w, g, m, v)


HEAD_PAD = LANES


def _rope_rot():
    r = np.zeros((HEAD_PAD, HEAD_PAD), np.float32)
    half = QK_ROPE_DIM // 2
    for j in range(half):
        r[QK_NOPE_DIM + half + j, QK_NOPE_DIM + j] = -1.0
        r[QK_NOPE_DIM + j, QK_NOPE_DIM + half + j] = 1.0
    return r


def _head_spec(s, blk0):
    return pl.BlockSpec((None, s, HEAD_PAD), lambda bi, hi: (bi, 0, blk0 + hi))


def qk_prep_fwd(name, x, blk0, n_heads, d_real, gain, rope=None, extra=None):
    b, s, _ = x.shape

    def body(*refs):
        x_ref, g_ref = refs[0], refs[1]
        pos = 2
        xv = x_ref[...]
        if extra is not None:
            xv = xv + refs[pos][...]
            pos += 1
        y = xv * lax.rsqrt(jnp.sum(xv * xv, axis=-1, keepdims=True) * (1.0 / d_real) + EPS) * g_ref[...]
        if rope is not None:
            c_ref, s_ref, r_ref = refs[pos:pos + 3]
            rot = jnp.dot(y, r_ref[...], precision=lax.Precision.HIGHEST, preferred_element_type=F32)
            y = y * c_ref[...] + rot * s_ref[...]
        refs[-1][...] = y.astype(BF16)

    vec = pl.BlockSpec((1, HEAD_PAD), lambda bi, hi: (0, 0))
    tab = pl.BlockSpec((None, s, HEAD_PAD), lambda bi, hi: (bi, 0, 0))
    in_specs, args = [_head_spec(s, blk0), vec], [x, gain]
    if extra is not None:
        e_blk = extra[1]
        in_specs.append(pl.BlockSpec((None, s, HEAD_PAD), lambda bi, hi: (bi, 0, e_blk)))
        args.append(extra[0])
    if rope is not None:
        in_specs += [tab, tab, pl.BlockSpec((HEAD_PAD, HEAD_PAD), lambda bi, hi: (0, 0))]
        args += [rope[0], rope[1], jnp.asarray(_rope_rot())]
    return pl.pallas_call(
        body, name=name, out_shape=jax.ShapeDtypeStruct((b, s, n_heads * HEAD_PAD), BF16), grid=(b, n_heads),
        in_specs=in_specs, out_specs=_head_spec(s, 0), compiler_params=_params(("parallel", "parallel")),
    )(*args)


def qk_prep_bwd(name, x, blk0, n_heads, d_real, gain, dxh, rope=None, extra=None, head_sum=False):
    b, s, _ = x.shape

    def body(*refs):
        x_ref, g_ref, dy_ref = refs[:3]
        pos = 3
        xv = x_ref[...]
        if extra is not None:
            xv = xv + refs[pos][...]
            pos += 1
        dy = dy_ref[...]
        if rope is not None:
            c_ref, s_ref, rt_ref = refs[pos:pos + 3]
            pos += 3
            dy = dy * c_ref[...] + jnp.dot(dy * s_ref[...], rt_ref[...], precision=lax.Precision.HIGHEST, preferred_element_type=F32)
        outs = refs[pos:]
        dx_ref, dg_ref = outs[0], outs[-1]
        r = lax.rsqrt(jnp.sum(xv * xv, axis=-1, keepdims=True) * (1.0 / d_real) + EPS)
        xhat = xv * r
        dxhat = dy * g_ref[...]
        dx = r * (dxhat - xhat * (jnp.sum(dxhat * xhat, axis=-1, keepdims=True) * (1.0 / d_real)))
        dx_ref[...] = dx.astype(dx_ref.dtype)
        part = jnp.sum(dy * xhat, axis=0, keepdims=True)
        first = jnp.logical_and(pl.program_id(0) == 0, pl.program_id(1) == 0)

        @pl.when(first)
        def _():
            dg_ref[...] = part

        @pl.when(jnp.logical_not(first))
        def _():
            dg_ref[...] += part

        if head_sum:
            hs_ref = outs[1]

            @pl.when(pl.program_id(1) == 0)
            def _():
                hs_ref[...] = dx

            @pl.when(pl.program_id(1) > 0)
            def _():
                hs_ref[...] += dx

    vec = pl.BlockSpec((1, HEAD_PAD), lambda bi, hi: (0, 0))
    tab = pl.BlockSpec((None, s, HEAD_PAD), lambda bi, hi: (bi, 0, 0))
    in_specs, args = [_head_spec(s, blk0), vec, _head_spec(s, 0)], [x, gain, dxh]
    if extra is not None:
        e_blk = extra[1]
        in_specs.append(pl.BlockSpec((None, s, HEAD_PAD), lambda bi, hi: (bi, 0, e_blk)))
        args.append(extra[0])
    if rope is not None:
        in_specs += [tab, tab, pl.BlockSpec((HEAD_PAD, HEAD_PAD), lambda bi, hi: (0, 0))]
        args += [rope[0], rope[1], jnp.asarray(_rope_rot().T.copy())]
    out_shape, out_specs = [jax.ShapeDtypeStruct((b, s, n_heads * HEAD_PAD), BF16)], [_head_spec(s, 0)]
    if head_sum:
        out_shape.append(jax.ShapeDtypeStruct((b, s, HEAD_PAD), F32))
        out_specs.append(tab)
    out_shape.append(jax.ShapeDtypeStruct((1, HEAD_PAD), F32))
    out_specs.append(vec)
    return pl.pallas_call(
        body, name=name, out_shape=tuple(out_shape), grid=(b, n_heads), in_specs=in_specs, out_specs=tuple(out_specs),
        compiler_params=_params(("arbitrary", "arbitrary")),
    )(*args)


def _swa_specs(v_blk0):
    q = pl.BlockSpec((None, BLOCK, GROUP_A * HEAD_PAD), lambda b, kv, n: (b, n, kv))
    kprev = pl.BlockSpec((None, BLOCK, HEAD_PAD), lambda b, kv, n: (b, jnp.maximum(n - 1, 0), kv))
    kcur = pl.BlockSpec((None, BLOCK, HEAD_PAD), lambda b, kv, n: (b, n, kv))
    vprev = pl.BlockSpec((None, BLOCK, HEAD_PAD), lambda b, kv, n: (b, jnp.maximum(n - 1, 0), v_blk0 + kv))
    vcur = pl.BlockSpec((None, BLOCK, HEAD_PAD), lambda b, kv, n: (b, n, v_blk0 + kv))
    pcol = pl.BlockSpec((None, BLOCK, 1), lambda b, kv, n: (b, n, 0))
    prow_prev = pl.BlockSpec((None, 1, BLOCK), lambda b, kv, n: (b, 0, jnp.maximum(n - 1, 0)))
    prow_cur = pl.BlockSpec((None, 1, BLOCK), lambda b, kv, n: (b, 0, n))
    smem = pl.BlockSpec(memory_space=pltpu.SMEM)
    return q, kprev, kcur, vprev, vcur, pcol, prow_prev, prow_cur, smem


def _swa_probs(q, kk, dist, valid, slope, sink):
    sc = lax.dot_general(q, kk, _NT, preferred_element_type=F32) * (HEAD_DIM_A ** -0.5)
    sc = sc - slope * dist
    sc = jnp.where(valid, sc, NEG)
    m = jnp.maximum(jnp.max(sc, axis=-1, keepdims=True), sink)
    e = jnp.exp(sc - m)
    es = jnp.exp(sink - m)
    inv = 1.0 / (jnp.sum(e, axis=-1, keepdims=True) + es)
    return e * inv, es * inv


def _swa_window(n, kp_ref, kc_ref, vp_ref, vc_ref, pc_ref, prp_ref, prc_ref):
    kk = jnp.concatenate([kp_ref[...], kc_ref[...]], axis=0)
    vv = jnp.concatenate([vp_ref[...], vc_ref[...]], axis=0).astype(BF16)
    dist = pc_ref[...] - jnp.concatenate([prp_ref[...], prc_ref[...]], axis=1)
    qi = lax.broadcasted_iota(jnp.int32, (BLOCK, 2 * BLOCK), 0) + BLOCK
    ki = lax.broadcasted_iota(jnp.int32, (BLOCK, 2 * BLOCK), 1)
    diff = qi - ki
    valid = (diff >= 0) & (diff < BLOCK) & ((n > 0) | (ki >= BLOCK))
    return kk, vv, dist, valid


def swa_fwd(name, q, k, vsrc, v_blk0, pos_col, pos_row, slopes, sinks, plan=None):
    b, s, _ = q.shape
    qs, kprev, kcur, vprev, vcur, pcol, prp, prc, smem = _swa_specs(v_blk0)
    grid = (b, N_KV_HEADS_A, s // BLOCK)
    p_ins, p_in_specs, p_outs, p_out_specs, p_scratch, semantics, at_start, at_end = _hosted(plan, 10, 1, grid, ("parallel",) * 3)

    def body(*refs):
        q_ref, kp_ref, kc_ref, vp_ref, vc_ref, pc_ref, prp_ref, prc_ref, sl_ref, sk_ref = refs[:10]
        o_ref = refs[10 + len(p_ins)]
        at_start(refs)
        kv, n = pl.program_id(1), pl.program_id(2)
        kk, vv, dist, valid = _swa_window(n, kp_ref, kc_ref, vp_ref, vc_ref, pc_ref, prp_ref, prc_ref)
        for g in range(GROUP_A):
            hd = kv * GROUP_A + g
            lanes = slice(g * HEAD_PAD, (g + 1) * HEAD_PAD)
            p, _ = _swa_probs(q_ref[:, lanes], kk, dist, valid, sl_ref[hd], sk_ref[hd])
            o_ref[:, lanes] = jnp.dot(p.astype(BF16), vv, preferred_element_type=F32).astype(BF16)
        at_end(refs)

    res = pl.pallas_call(
        body, name=name, out_shape=(jax.ShapeDtypeStruct(q.shape, BF16),) + tuple(p_outs), grid=grid,
        in_specs=[qs, kprev, kcur, vprev, vcur, pcol, prp, prc, smem, smem] + p_in_specs, out_specs=(qs,) + tuple(p_out_specs),
        scratch_shapes=p_scratch, compiler_params=_params(semantics),
    )(q, k, k, vsrc, vsrc, pos_col, pos_row, pos_row, slopes, sinks, *p_ins)
    return res[0], list(res[1:])


def swa_bwd(name, q, k, vsrc, v_blk0, pos_col, pos_row, slopes, sinks, do, plan=None):
    b, s, _ = q.shape
    qs, kprev, kcur, vprev, vcur, pcol, prp, prc, smem = _swa_specs(v_blk0)
    grid = (b, N_KV_HEADS_A, s // BLOCK)
    p_ins, p_in_specs, p_outs, p_out_specs, p_scratch, semantics, at_start, at_end = _hosted(plan, 11, 4, grid, ("arbitrary",) * 3)

    def body(*refs):
        at_start(refs)
        compute(*refs[:11], *refs[11 + len(p_ins):15 + len(p_ins)])
        at_end(refs)

    def compute(q_ref, kp_ref, kc_ref, vp_ref, vc_ref, pc_ref, prp_ref, prc_ref, sl_ref, sk_ref, do_ref, dq_ref, dk_ref, dv_ref, ds_ref):
        bi, kv, n = pl.program_id(0), pl.program_id(1), pl.program_id(2)
        kk, vv, dist, valid = _swa_window(n, kp_ref, kc_ref, vp_ref, vc_ref, pc_ref, prp_ref, prc_ref)

        @pl.when((bi == 0) & (kv == 0) & (n == 0))
        def _():
            ds_ref[...] = jnp.zeros_like(ds_ref)

        @pl.when(n == 0)
        def _():
            dk_ref[...] = jnp.zeros_like(dk_ref)
            dv_ref[...] = jnp.zeros_like(dv_ref)

        dkk = jnp.zeros((2 * BLOCK, HEAD_PAD), F32)
        dvv = jnp.zeros((2 * BLOCK, HEAD_PAD), F32)
        head_row = lax.broadcasted_iota(jnp.int32, (N_HEADS_A, LANES), 0)
        dsink = jnp.zeros((N_HEADS_A, LANES), F32)
        for g in range(GROUP_A):
            hd = kv * GROUP_A + g
            lanes = slice(g * HEAD_PAD, (g + 1) * HEAD_PAD)
            qg = q_ref[:, lanes]
            p, ps = _swa_probs(qg, kk, dist, valid, sl_ref[hd], sk_ref[hd])
            dob = do_ref[:, lanes].astype(BF16)
            dvv = dvv + lax.dot_general(p.astype(BF16), dob, _TN, preferred_element_type=F32)
            dp = lax.dot_general(dob, vv, _NT, preferred_element_type=F32)
            rs = jnp.sum(p * dp, axis=-1, keepdims=True)
            dsb = (p * (dp - rs) * (HEAD_DIM_A ** -0.5)).astype(BF16)
            dq_ref[:, lanes] = jnp.dot(dsb, kk, preferred_element_type=F32)
            dkk = dkk + lax.dot_general(dsb, qg, _TN, preferred_element_type=F32)
            dsink = dsink + jnp.where(head_row == hd, -jnp.sum(ps * rs), 0.0)
        ds_ref[...] += dsink

        @pl.when(n > 0)
        def _():
            start = pl.multiple_of((n - 1) * BLOCK, BLOCK)
            dk_ref[pl.ds(start, 2 * BLOCK), :] += dkk
            dv_ref[pl.ds(start, 2 * BLOCK), :] += dvv

        @pl.when(n == 0)
        def _():
            dk_ref[0:BLOCK, :] += dkk[BLOCK:, :]
            dv_ref[0:BLOCK, :] += dvv[BLOCK:, :]

    kv_full = pl.BlockSpec((None, s, HEAD_PAD), lambda bi, kv, n: (bi, 0, kv))
    kv_shape = jax.ShapeDtypeStruct(k.shape, F32)
    res = pl.pallas_call(
        body, name=name, grid=grid,
        out_shape=(jax.ShapeDtypeStruct(q.shape, F32), kv_shape, kv_shape, jax.ShapeDtypeStruct((N_HEADS_A, LANES), F32)) + tuple(p_outs),
        in_specs=[qs, kprev, kcur, vprev, vcur, pcol, prp, prc, smem, smem, qs] + p_in_specs,
        out_specs=(qs, kv_full, kv_full, pl.BlockSpec((N_HEADS_A, LANES), lambda bi, kv, n: (0, 0))) + tuple(p_out_specs),
        scratch_shapes=p_scratch, compiler_params=_params(semantics),
    )(q, k, k, vsrc, vsrc, pos_col, pos_row, pos_row, slopes, sinks, do, *p_ins)
    return res[0], res[1], res[2], res[3], list(res[4:])


MLA_T = 512


def _hosted(plan, n_in, n_out, grid, semantics):
    if plan is None:
        return [], [], [], [], [], semantics, (lambda refs: None), (lambda refs: None)
    ni, no = len(plan.ins), len(plan.out_shapes)

    def split(refs):
        return refs[n_in:n_in + ni], refs[n_in + ni + n_out:n_in + ni + n_out + no], refs[-2], refs[-1]

    def at_start(refs):
        @pl.when(functools.reduce(jnp.logical_and, [pl.program_id(ax) == 0 for ax in range(len(grid))]))
        def _():
            plan.start(*split(refs))

    def at_end(refs):
        @pl.when(functools.reduce(jnp.logical_and, [pl.program_id(ax) == g - 1 for ax, g in enumerate(grid)]))
        def _():
            plan.finish(*split(refs))

    return plan.ins, [_HBM] * ni, plan.out_shapes, [_HBM] * no, plan.scratch(), ("arbitrary",) * len(grid), at_start, at_end


def mla_fwd(name, q, k, v, plan=None):
    b, s, w = q.shape
    h = w // HEAD_PAD
    t = MLA_T if s % MLA_T == 0 else BLOCK
    nb = s // t
    scale = QK_DIM_B ** -0.5
    p_ins, p_in_specs, p_outs, p_out_specs, p_scratch, semantics, at_start, at_end = _hosted(
        plan, 3, 3, (b, h), ("parallel", "parallel"))

    def body(*refs):
        q_ref, k_ref, v_ref = refs[:3]
        o_ref, ob_ref, lse_ref = refs[3 + len(p_ins):6 + len(p_ins)]
        at_start(refs)
        causal = lax.broadcasted_iota(jnp.int32, (t, t), 0) <= lax.broadcasted_iota(jnp.int32, (t, t), 1)
        for i in range(nb):
            q_i = q_ref[i * t:(i + 1) * t, :]

            def step(j, carry, q_i=q_i, diagonal=False):
                m, l, acc = carry
                rows = slice(j * t, (j + 1) * t) if diagonal else pl.ds(pl.multiple_of(j * t, t), t)
                st = lax.dot_general(k_ref[rows, :], q_i, _NT, preferred_element_type=F32) * scale
                if diagonal:
                    st = jnp.where(causal, st, NEG)
                m_new = jnp.maximum(m, jnp.max(st, axis=0, keepdims=True))
                a = jnp.exp(m - m_new)
                p = jnp.exp(st - m_new)
                l = a * l + jnp.sum(p, axis=0, keepdims=True)
                acc = a * acc + lax.dot_general(v_ref[rows, :], p.astype(BF16), _TN, preferred_element_type=F32)
                return m_new, l, acc

            carry = (jnp.full((1, t), NEG, F32), jnp.zeros((1, t), F32), jnp.zeros((HEAD_PAD, t), F32))
            if i > 0:
                carry = lax.fori_loop(0, i, step, carry)
            m, l, acc = step(i, carry, diagonal=True)
            o_i = (acc / l).T
            o_ref[i * t:(i + 1) * t, :] = o_i
            ob_ref[i * t:(i + 1) * t, :] = o_i.astype(BF16)
            lse_ref[i] = m + jnp.log(l)
        at_end(refs)

    hs = _head_spec(s, 0)
    ls = pl.BlockSpec((None, None, nb, 1, t), lambda bi, hi: (bi, hi, 0, 0, 0))
    res = pl.pallas_call(
        body, name=name, grid=(b, h),
        out_shape=(jax.ShapeDtypeStruct((b, s, w), F32), jax.ShapeDtypeStruct((b, s, w), BF16),
                   jax.ShapeDtypeStruct((b, h, nb, 1, t), F32)) + tuple(p_outs),
        in_specs=[hs, hs, hs] + p_in_specs, out_specs=(hs, hs, ls) + tuple(p_out_specs),
        scratch_shapes=p_scratch, compiler_params=_params(semantics),
    )(q, k, v, *p_ins)
    return res[0], res[1], res[2], list(res[3:])


def mla_bwd(name, q, k, v, o, do, lse, plan=None):
    b, s, w = q.shape
    h = w // HEAD_PAD
    _, _, nb, _, t = lse.shape
    scale = QK_DIM_B ** -0.5
    p_ins, p_in_specs, p_outs, p_out_specs, p_scratch, semantics, at_start, at_end = _hosted(
        plan, 6, 3, (b, h), ("parallel", "parallel"))

    def body(*refs):
        q_ref, k_ref, v_ref, o_ref, do_ref, lse_ref = refs[:6]
        dq_ref, dk_ref, dv_ref = refs[6 + len(p_ins):9 + len(p_ins)]
        dv_acc = refs[9 + len(p_ins) + len(p_outs)]
        at_start(refs)
        causal = lax.broadcasted_iota(jnp.int32, (t, t), 0) <= lax.broadcasted_iota(jnp.int32, (t, t), 1)
        dk_ref[...] = jnp.zeros_like(dk_ref)
        dv_acc[...] = jnp.zeros_like(dv_acc)
        for i in range(nb):
            q_i = q_ref[i * t:(i + 1) * t, :]
            do_i = do_ref[i * t:(i + 1) * t, :]
            delta = jnp.sum((o_ref[i * t:(i + 1) * t, :] * do_i).T, axis=0, keepdims=True)
            do_b = do_i.astype(BF16)
            lse_i = lse_ref[i]

            def step(j, dqt, q_i=q_i, do_b=do_b, delta=delta, lse_i=lse_i, diagonal=False):
                rows = slice(j * t, (j + 1) * t) if diagonal else pl.ds(pl.multiple_of(j * t, t), t)
                k_j = k_ref[rows, :]
                st = lax.dot_general(k_j, q_i, _NT, preferred_element_type=F32) * scale
                if diagonal:
                    st = jnp.where(causal, st, NEG)
                pt = jnp.exp(st - lse_i)
                dpt = lax.dot_general(v_ref[rows, :], do_b, _NT, preferred_element_type=F32)
                dst = (pt * (dpt - delta) * scale).astype(BF16)
                dv_acc[rows, :] += jnp.dot(pt.astype(BF16), do_b, preferred_element_type=F32)
                dk_ref[rows, :] += jnp.dot(dst, q_i, preferred_element_type=F32)
                return dqt + lax.dot_general(k_j, dst, _TN, preferred_element_type=F32)

            dqt = jnp.zeros((HEAD_PAD, t), F32)
            if i > 0:
                dqt = lax.fori_loop(0, i, step, dqt)
            dq_ref[i * t:(i + 1) * t, :] = step(i, dqt, diagonal=True).T
        dv_ref[...] = dv_acc[...].astype(BF16)
        at_end(refs)

    hs = _head_spec(s, 0)
    ls = pl.BlockSpec((None, None, nb, 1, t), lambda bi, hi: (bi, hi, 0, 0, 0))
    res = pl.pallas_call(
        body, name=name, grid=(b, h),
        out_shape=(jax.ShapeDtypeStruct((b, s, w), F32), jax.ShapeDtypeStruct((b, s, w), F32),
                   jax.ShapeDtypeStruct((b, s, w), BF16)) + tuple(p_outs),
        in_specs=[hs, hs, hs, hs, hs, ls] + p_in_specs, out_specs=(hs, hs, hs) + tuple(p_out_specs),
        scratch_shapes=[pltpu.VMEM((s, HEAD_PAD), F32)] + p_scratch,
        compiler_params=_params(semantics),
    )(q, k, v, o, do, lse, *p_ins)
    return res[0], res[1], res[2], list(res[3:])


_HBM = pl.BlockSpec(memory_space=pltpu.HBM)


def _place():
    x, y, c = lax.axis_index("x"), lax.axis_index("y"), lax.axis_index("c")
    chips = [(1 - x, y), (x, 1 - y), (1 - x, 1 - y)]
    return x, y, c, chips


def _remote(src, dst, send_sem, recv_sem, dev):
    return pltpu.make_async_remote_copy(src_ref=src, dst_ref=dst, send_sem=send_sem, recv_sem=recv_sem,
                                        device_id=dev, device_id_type=MESH)


class CommPlan:
    def __init__(self, ins, out_shapes, n_sems, start, finish, aliases=None):
        self.ins, self.out_shapes, self.n_sems = list(ins), list(out_shapes), n_sems
        self.start, self.finish, self.aliases = start, finish, dict(aliases or {})

    def scratch(self):
        return [pltpu.SemaphoreType.DMA((self.n_sems,)), pltpu.SemaphoreType.DMA((self.n_sems,))]


def comm_call(name, plan):
    ni, no = len(plan.ins), len(plan.out_shapes)

    def body(*refs):
        ins, outs, (send_sems, recv_sems) = refs[:ni], refs[ni:ni + no], refs[ni + no:]
        plan.start(ins, outs, send_sems, recv_sems)
        plan.finish(ins, outs, send_sems, recv_sems)

    return pl.pallas_call(
        body, name=name, out_shape=tuple(plan.out_shapes), in_specs=[_HBM] * ni, out_specs=tuple([_HBM] * no),
        input_output_aliases=plan.aliases, scratch_shapes=plan.scratch(),
    )(*plan.ins)


def gather_plan(packs, l):
    nk = len(packs)

    def parts(p_refs, o_refs, ss, rs):
        x, y, c, chips = _place()
        s_me = 2 * x + y
        sibling = (x, y, 1 - c)
        first = [_remote(p_refs[k].at[l], o_refs[k].at[s_me], ss.at[j * nk + k], rs.at[j * nk + k], (cx, cy, c))
                 for j, (cx, cy) in enumerate(chips) for k in range(nk)]
        own = [_remote(p_refs[k].at[l], o_refs[k].at[s_me], ss.at[6 * nk + k], rs.at[6 * nk + k], sibling) for k in range(nk)]
        return c, chips, sibling, first, own

    def start(p_refs, o_refs, ss, rs):
        c, _, _, first, own = parts(p_refs, o_refs, ss, rs)
        for cp in own:
            cp.start()

        @pl.when(c == l)
        def _():
            for cp in first:
                cp.start()

    def finish(p_refs, o_refs, ss, rs):
        c, chips, sibling, first, own = parts(p_refs, o_refs, ss, rs)

        @pl.when(c == l)
        def _():
            passed = []
            for j, (cx, cy) in enumerate(chips):
                for k in range(nk):
                    blk = o_refs[k].at[2 * cx + cy]
                    _remote(blk, blk, ss.at[j * nk + k], rs.at[j * nk + k], (cx, cy, c)).wait_recv()
                    fwd = _remote(blk, blk, ss.at[(3 + j) * nk + k], rs.at[(3 + j) * nk + k], sibling)
                    fwd.start()
                    passed.append(fwd)
            for cp in first + passed:
                cp.wait_send()

        @pl.when(c != l)
        def _():
            for j, (cx, cy) in enumerate(chips):
                for k in range(nk):
                    blk = o_refs[k].at[2 * cx + cy]
                    _remote(blk, blk, ss.at[(3 + j) * nk + k], rs.at[(3 + j) * nk + k], sibling).wait_recv()

        for cp in own:
            cp.wait()

    outs = [jax.ShapeDtypeStruct((N_SHARDS,) + p.shape[1:], p.dtype) for p in packs]
    return CommPlan(packs, outs, 7 * nk, start, finish)


def swap_plan(grads, l):
    nk = len(grads)

    def copies(g_refs, o_refs, ss, rs):
        x, y, c, _ = _place()
        return c, [_remote(g_refs[k], o_refs[k], ss.at[k], rs.at[k], (x, y, 1 - c)) for k in range(nk)]

    def start(g_refs, o_refs, ss, rs):
        c, cps = copies(g_refs, o_refs, ss, rs)

        @pl.when(c != l)
        def _():
            for cp in cps:
                cp.start()

    def finish(g_refs, o_refs, ss, rs):
        c, cps = copies(g_refs, o_refs, ss, rs)

        @pl.when(c != l)
        def _():
            for cp in cps:
                cp.wait_send()

        @pl.when(c == l)
        def _():
            for cp in cps:
                cp.wait_recv()

    return CommPlan(grads, [jax.ShapeDtypeStruct(g.shape, g.dtype) for g in grads], nk, start, finish)


def scatter_plan(parts, l):
    nk = len(parts)

    def copies(p_refs, o_refs, ss, rs):
        x, y, c, chips = _place()
        s_me = 2 * x + y
        sends = [_remote(p_refs[k].at[2 * cx + cy], o_refs[k].at[s_me], ss.at[j * nk + k], rs.at[j * nk + k], (cx, cy, c))
                 for j, (cx, cy) in enumerate(chips) for k in range(nk)]
        return c, chips, sends

    def start(p_refs, o_refs, ss, rs):
        c, _, sends = copies(p_refs, o_refs, ss, rs)

        @pl.when(c == l)
        def _():
            for cp in sends:
                cp.start()

    def finish(p_refs, o_refs, ss, rs):
        c, chips, sends = copies(p_refs, o_refs, ss, rs)

        @pl.when(c == l)
        def _():
            for j, (cx, cy) in enumerate(chips):
                for k in range(nk):
                    slot = o_refs[k].at[2 * cx + cy]
                    _remote(slot, slot, ss.at[j * nk + k], rs.at[j * nk + k], (cx, cy, c)).wait_recv()
            for cp in sends:
                cp.wait_send()

    return CommPlan(parts, [jax.ShapeDtypeStruct(p.shape, p.dtype) for p in parts], 3 * nk, start, finish)


def share_plan(finals):
    flat = [(l, f) for l, fs in enumerate(finals) for f in fs]

    def copies(f_refs, o_refs, ss, rs):
        x, y, c, _ = _place()
        return c, [_remote(f_refs[i], o_refs[i], ss.at[i], rs.at[i], (x, y, 1 - c)) for i in range(len(flat))]

    def start(f_refs, o_refs, ss, rs):
        c, cps = copies(f_refs, o_refs, ss, rs)
        for l in range(len(finals)):
            @pl.when(c == l)
            def _(l=l):
                for cp, (lf, _) in zip(cps, flat):
                    if lf == l:
                        cp.start()

    def finish(f_refs, o_refs, ss, rs):
        c, cps = copies(f_refs, o_refs, ss, rs)
        for l in range(len(finals)):
            @pl.when(c == l)
            def _(l=l):
                for cp, (lf, _) in zip(cps, flat):
                    if lf == l:
                        cp.wait_send()

            @pl.when(c != l)
            def _(l=l):
                for cp, (lf, _) in zip(cps, flat):
                    if lf == l:
                        cp.wait_recv()

    arrays = [f for _, f in flat]
    return CommPlan(arrays, [jax.ShapeDtypeStruct(f.shape, f.dtype) for f in arrays], len(flat), start, finish,
                    aliases={i: i for i in range(len(flat))})


def add_pair(name, own, other, tr=1024):
    n, cols = own.shape
    tr = _tile(n, tr, 16)

    def body(g_ref, o_ref, out_ref):
        out_ref[...] = (g_ref[...].astype(F32) + o_ref[...].astype(F32)).astype(out_ref.dtype)

    row = pl.BlockSpec((tr, cols), lambda i: (i, 0))
    return pl.pallas_call(
        body, name=name, out_shape=jax.ShapeDtypeStruct((n, cols), BF16), grid=(n // tr,), in_specs=[row, row], out_specs=row,
        compiler_params=_params(("parallel",)),
    )(own, other)


def sum_slots(name, slots, part, shard, tr=512):
    ns, r, cols = slots.shape
    tr = _tile(r, tr, 16)

    def body(s_idx, s_ref, own_ref, o_ref):
        total = own_ref[...].astype(F32)
        for k in range(1, ns):
            total = total + s_ref[(s_idx[0] + k) % ns].astype(F32)
        o_ref[...] = total

    return pl.pallas_call(
        body, name=name, out_shape=jax.ShapeDtypeStruct((r, cols), F32),
        grid_spec=pltpu.PrefetchScalarGridSpec(
            num_scalar_prefetch=1, grid=(r // tr,),
            in_specs=[pl.BlockSpec((ns, tr, cols), lambda i, p: (0, i, 0)), pl.BlockSpec((None, tr, cols), lambda i, p: (p[0], i, 0))],
            out_specs=pl.BlockSpec((tr, cols), lambda i, p: (i, 0))),
        compiler_params=_params(("parallel",)),
    )(shard, slots, part)


def small_allreduce(vec):
    r, cols = vec.shape

    def body(v_ref, o_ref, buf, send_sems, recv_sems):
        x, y, c, _ = _place()
        me = 4 * x + 2 * y + c
        buf[me] = v_ref[...]
        peers = []
        for k in range(1, N_DEV):
            px = (1 - x) if (k & 4) else x
            py = (1 - y) if (k & 2) else y
            pc = (1 - c) if (k & 1) else c
            peers.append((px, py, pc))
        sends = [_remote(buf.at[me], buf.at[me], send_sems.at[k], recv_sems.at[k], peer) for k, peer in enumerate(peers)]
        for cp in sends:
            cp.start()
        for k, (px, py, pc) in enumerate(peers):
            slot = buf.at[4 * px + 2 * py + pc]
            _remote(slot, slot, send_sems.at[k], recv_sems.at[k], (px, py, pc)).wait_recv()
        for cp in sends:
            cp.wait_send()
        total = buf[0]
        for k in range(1, N_DEV):
            total = total + buf[k]
        o_ref[...] = total

    return pl.pallas_call(
        body, name="small_allreduce", out_shape=jax.ShapeDtypeStruct((r, cols), F32),
        in_specs=[pl.BlockSpec(memory_space=pltpu.VMEM)], out_specs=pl.BlockSpec(memory_space=pltpu.VMEM),
        scratch_shapes=[pltpu.VMEM((N_DEV, r, cols), F32), pltpu.SemaphoreType.DMA((N_DEV - 1,)), pltpu.SemaphoreType.DMA((N_DEV - 1,))],
    )(vec)


ARG_NAMES = (("x", "positions") + WEIGHT_NAMES + ("loss_target",) + tuple("m_" + n for n in WEIGHT_NAMES)
             + tuple("v_" + n for n in WEIGHT_NAMES))

_C_KA = N_HEADS_A * HEAD_PAD
_C_VA = _C_KA + N_KV_HEADS_A * HEAD_PAD
_C_CQ = _C_VA + N_KV_HEADS_A * HEAD_PAD
_C_CKV = _C_CQ + Q_LORA_RANK
_C_KR = _C_CKV + KV_LORA_RANK
_C_GA = _C_KR + HEAD_PAD
_R_KA = WIDTH_A
_R_VA = _R_KA + KV_WIDTH_A
_R_CQ = _R_VA + KV_WIDTH_A
_R_KR = _R_CQ + Q_LORA_RANK + KV_LORA_RANK
_R_GA = _R_KR + QK_ROPE_DIM


def _pad_head_cols(w, n_heads):
    r, dh = w.shape[0], w.shape[1] // n_heads
    return jnp.pad(w.reshape(r, n_heads, dh), ((0, 0), (0, 0), (0, HEAD_PAD - dh))).reshape(r, n_heads * HEAD_PAD)


def _unpad_head_cols(w, n_heads, dh):
    r = w.shape[0]
    return w.reshape(r, n_heads, HEAD_PAD)[:, :, :dh].reshape(r, n_heads * dh)


def _pad_head_rows(w, n_heads):
    dh, c = w.shape[0] // n_heads, w.shape[1]
    return jnp.pad(w.reshape(n_heads, dh, c), ((0, 0), (0, HEAD_PAD - dh), (0, 0))).reshape(n_heads * HEAD_PAD, c)


def _unpad_head_rows(w, n_heads, dh):
    c = w.shape[1]
    return w.reshape(n_heads, HEAD_PAD, c)[:, :dh].reshape(n_heads * dh, c)


def _join_cols(shards):
    ns, r, c = shards.shape
    return shards.transpose(1, 0, 2).reshape(r, ns * c)


def _split_cols(mat):
    r, c4 = mat.shape
    return mat.reshape(r, N_SHARDS, c4 // N_SHARDS).transpose(1, 0, 2)


def _ffn_fwd(tag, x, gain, wcol, wdown, plan=None):
    nb = rms_fwd(tag + "_rms", x, gain)
    gu, h, carried = ffn_up_act(tag + "_up", nb, wcol, 0, plan=plan)
    out = gmm_down(tag + "_down", h, wdown, 0, x, 0.5)
    return out, (x, nb, gu, h), carried


def _ffn_bwd(tag, saved, gain, wcol, wdown, dout, doutb, plans=None):
    plans = plans or {}
    x, nb, gu, h = saved
    carried = {}
    dgu, carried["dh"] = ffn_down_dact(tag + "_dh", doutb, wdown, gu, gu.shape[-1], 0, 0.5, plan=plans.get("dh"))
    gdown = gmm_down_dw(tag + "_dwd", h, doutb, 0.5)
    if plans.get("dwgu") is not None:
        gcol, carried["dwgu"] = gmm_up_dw(tag + "_dwgu", nb, dgu, plan=plans["dwgu"])
    else:
        gcol = gmm_up_dw(tag + "_dwgu", nb, dgu)
    if plans.get("dn") is not None:
        plan_dn = plans["dn"](gcol, gdown) if callable(plans["dn"]) else plans["dn"]
        dn, carried["dn"] = gmm_up_dx(tag + "_dn", dgu, wcol, x.shape[1], 0, plan=plan_dn)
    else:
        dn = gmm_up_dx(tag + "_dn", dgu, wcol, x.shape[1], 0)
    dx, dxb, dgain = rms_bwd(tag + "_drms", x, gain, dn, dres=dout)
    return dx, dxb, dgain, gcol, gdown, carried


def _pad_gain(g):
    return jnp.pad(g, ((0, 0), (0, HEAD_PAD - g.shape[1])))


def _mixer_fwd(tag, x1, sm, w, wo, aux, plans=None):
    plans = plans or {}
    b, s = aux["b"], aux["s"]
    t, d = x1.shape
    hb = rms_fwd(tag + "_rms", x1, sm["mix_norm"])
    carried = {}
    if plans.get("proj") is not None:
        proj, carried["proj"] = mm_nn(tag + "_proj", hb, w["win"], tm=1024, tn=2048, tk=1024, plan=plans["proj"])
    else:
        proj = mm_nn(tag + "_proj", hb, w["win"], tm=1024, tn=2048, tk=1024)
    proj3 = proj.reshape(b, s, proj.shape[1])
    gains = {n: _pad_gain(sm[n]) for n in ("swa_q_norm", "swa_k_norm", "mla_q_norm", "mla_k_norm")}
    cqn = rms_fwd(tag + "_rms_cq", proj, sm["mla_q_lora_norm"], tm=1024, col_blk=_C_CQ // Q_LORA_RANK)
    ckvn = rms_fwd(tag + "_rms_ckv", proj, sm["mla_kv_lora_norm"], tm=1024, col_blk=_C_CKV // KV_LORA_RANK)
    qb_raw = mm_nn(tag + "_uq", cqn, w["wuq"], tm=1024).reshape(b, s, N_HEADS_B * HEAD_PAD)
    kb_raw = mm_nn(tag + "_uk", ckvn, w["wk"], tm=1024).reshape(b, s, N_HEADS_B * HEAD_PAD)
    vb = mm_nn(tag + "_uv", ckvn, w["wv"], tm=1024, out_dtype=BF16).reshape(b, s, N_HEADS_B * HEAD_PAD)
    qah = qk_prep_fwd(tag + "_qa_norm", proj3, 0, N_HEADS_A, HEAD_DIM_A, gains["swa_q_norm"])
    kah = qk_prep_fwd(tag + "_ka_norm", proj3, _C_KA // HEAD_PAD, N_KV_HEADS_A, HEAD_DIM_A, gains["swa_k_norm"])
    sinks = sm["swa_sinks"].reshape(-1)
    oab, carried["swa"] = swa_fwd(tag + "_swa", qah, kah, proj3, _C_VA // HEAD_PAD, aux["pos_col"], aux["pos_row"], aux["slopes"],
                                  sinks, plan=plans.get("swa"))
    kr = (proj3, _C_KR // HEAD_PAD)
    qbh = qk_prep_fwd(tag + "_qb_norm", qb_raw, 0, N_HEADS_B, QK_DIM_B, gains["mla_q_norm"], aux["rope"])
    kbh = qk_prep_fwd(tag + "_kb_norm", kb_raw, 0, N_HEADS_B, QK_DIM_B, gains["mla_k_norm"], aux["rope"], extra=kr)
    ob, obb, lse, carried["mla"] = mla_fwd(tag + "_mla", qbh, kbh, vb, plans.get("mla"))
    oab2, obb2 = oab.reshape(t, -1), obb.reshape(t, -1)
    ya = mm_nn(tag + "_branch_a", oab2, w["wa"], tm=1024)
    yb = mm_nn(tag + "_branch_b", obb2, w["wb"], tm=1024)
    mg = gate_fwd(tag + "_gate", proj, ya, yb, _C_GA, _C_GA + d)
    x2 = gmm_rows(tag + "_out", mg, wo, x1)
    saved = dict(x1=x1, hb=hb, proj=proj, cqn=cqn, ckvn=ckvn, qb_raw=qb_raw, kb_raw=kb_raw, vb=vb, qah=qah, kah=kah,
                 qbh=qbh, kbh=kbh, ob=ob, lse=lse, oab2=oab2, obb2=obb2, ya=ya, yb=yb, mg=mg, gains=gains)
    return x2, saved, carried


def _mixer_bwd(tag, sv, sm, w, wo, aux, dx2, dx2b, plans=None):
    plans = plans or {}
    carried = {}
    b, s = aux["b"], aux["s"]
    t, d = dx2.shape
    proj, gains = sv["proj"], sv["gains"]
    proj3 = proj.reshape(b, s, proj.shape[1])
    kr = (proj3, _C_KR // HEAD_PAD)
    if plans.get("d_out") is not None:
        dmg, carried["d_out"] = gmm_rows_dx(tag + "_d_out", dx2b, wo, plan=plans["d_out"])
    else:
        dmg = gmm_rows_dx(tag + "_d_out", dx2b, wo)
    dwo = gmm_rows_dw(tag + "_dw_out", sv["mg"], dx2b)
    dya, dyb, dga, dgb = gate_bwd(tag + "_dgate", proj, sv["ya"], sv["yb"], dmg, _C_GA, _C_GA + d)
    dwa = mm_tn(tag + "_dw_branch_a", sv["oab2"], dya, out_dtype=BF16)
    dwb = mm_tn(tag + "_dw_branch_b", sv["obb2"], dyb, out_dtype=BF16)
    doa = mm_nt(tag + "_d_branch_a", dya, w["wa"], tm=1024).reshape(b, s, -1)
    dob = mm_nt(tag + "_d_branch_b", dyb, w["wb"], tm=1024).reshape(b, s, -1)
    sinks = sm["swa_sinks"].reshape(-1)
    dqah, dkah, dva, dsinks, carried["dswa"] = swa_bwd(tag + "_dswa", sv["qah"], sv["kah"], proj3, _C_VA // HEAD_PAD, aux["pos_col"],
                                                       aux["pos_row"], aux["slopes"], sinks, doa, plan=plans.get("dswa"))
    dqa_raw, dg_swa_q = qk_prep_bwd(tag + "_dqa_norm", proj3, 0, N_HEADS_A, HEAD_DIM_A, gains["swa_q_norm"], dqah)
    dka_raw, dg_swa_k = qk_prep_bwd(tag + "_dka_norm", proj3, _C_KA // HEAD_PAD, N_KV_HEADS_A, HEAD_DIM_A, gains["swa_k_norm"], dkah)
    plan_dmla = plans["dmla"](carried) if callable(plans.get("dmla")) else plans.get("dmla")
    dqbh, dkbh, dvb, carried["dmla"] = mla_bwd(tag + "_dmla", sv["qbh"], sv["kbh"], sv["vb"], sv["ob"], dob, sv["lse"], plan_dmla)
    dqb_raw, dg_mla_q = qk_prep_bwd(tag + "_dqb_norm", sv["qb_raw"], 0, N_HEADS_B, QK_DIM_B, gains["mla_q_norm"], dqbh, aux["rope"])
    dkb_raw, dkr_sum, dg_mla_k = qk_prep_bwd(tag + "_dkb_norm", sv["kb_raw"], 0, N_HEADS_B, QK_DIM_B, gains["mla_k_norm"], dkbh,
                                             aux["rope"], extra=kr, head_sum=True)
    dq_tok, dk_tok, dv_tok = dqb_raw.reshape(t, -1), dkb_raw.reshape(t, -1), dvb.reshape(t, -1)
    dwuq = mm_tn(tag + "_dw_uq", sv["cqn"], dq_tok, tk=1024, out_dtype=BF16)
    dwk = mm_tn(tag + "_dw_uk", sv["ckvn"], dk_tok, tk=1024, out_dtype=BF16)
    dwv = mm_tn(tag + "_dw_uv", sv["ckvn"], dv_tok, tk=1024, out_dtype=BF16)
    dcqn = mm_nt(tag + "_d_uq", dq_tok, w["wuq"], tm=1024)
    dckvn = mm_nt(tag + "_d_uv", dv_tok, w["wv"], tm=1024, res=mm_nt(tag + "_d_uk", dk_tok, w["wk"], tm=1024))
    dcq, dg_q_lora = rms_bwd(tag + "_drms_cq", proj, sm["mla_q_lora_norm"], dcqn, want_f32=False, tm=1024, col_blk=_C_CQ // Q_LORA_RANK)
    dckv, dg_kv_lora = rms_bwd(tag + "_drms_ckv", proj, sm["mla_kv_lora_norm"], dckvn, want_f32=False, tm=1024,
                               col_blk=_C_CKV // KV_LORA_RANK)
    dproj = jnp.concatenate([dqa_raw.reshape(t, -1), dka_raw.reshape(t, -1), dva.reshape(t, -1).astype(BF16), dcq, dckv,
                             dkr_sum.reshape(t, HEAD_PAD).astype(BF16), dga, dgb], axis=1)
    dwin = mm_tn(tag + "_dw_in", sv["hb"], dproj, tm=1024, tn=2048, tk=1024, out_dtype=BF16)
    dh = mm_nt(tag + "_d_in", dproj, w["win"], tm=1024, tn=1024, tk=2048)
    dx1, dx1b, dg_mix = rms_bwd(tag + "_drms", sv["x1"], sm["mix_norm"], dh, dres=dx2)
    wgrads = dict(win=dwin, wuq=dwuq, wk=dwk, wv=dwv, wa=dwa, wb=dwb, wo=dwo)
    sgrads = dict(mix_norm=dg_mix, swa_q_norm=dg_swa_q[:, :HEAD_DIM_A], swa_k_norm=dg_swa_k[:, :HEAD_DIM_A],
                  swa_sinks=dsinks[:, 0].reshape(1, -1), mla_q_lora_norm=dg_q_lora, mla_kv_lora_norm=dg_kv_lora,
                  mla_q_norm=dg_mla_q[:, :QK_DIM_B], mla_k_norm=dg_mla_k[:, :QK_DIM_B])
    return dx1, dx1b, wgrads, sgrads, carried


def _layer_weights(win4, uq4, ukv4, branch4):
    wr = _join_cols(win4)
    kr = jnp.pad(wr[:, _R_KR:_R_GA], ((0, 0), (QK_NOPE_DIM, HEAD_PAD - QK_DIM_B)))
    win = jnp.concatenate([_pad_head_cols(wr[:, :_R_KA], N_HEADS_A), _pad_head_cols(wr[:, _R_KA:_R_VA], N_KV_HEADS_A),
                           _pad_head_cols(wr[:, _R_VA:_R_CQ], N_KV_HEADS_A), wr[:, _R_CQ:_R_KR], kr, wr[:, _R_GA:]], axis=1)
    ukv = _join_cols(ukv4)
    ukv3 = ukv.reshape(ukv.shape[0], N_HEADS_B, QK_NOPE_DIM + V_DIM_B)
    r = branch4.shape[1] // 2
    return dict(win=win, wuq=_pad_head_cols(_join_cols(uq4), N_HEADS_B),
                wk=_pad_head_cols(ukv3[:, :, :QK_NOPE_DIM].reshape(ukv.shape[0], -1), N_HEADS_B),
                wv=_pad_head_cols(ukv3[:, :, QK_NOPE_DIM:].reshape(ukv.shape[0], -1), N_HEADS_B),
                wa=_pad_head_rows(_join_cols(branch4[:, :r]), N_HEADS_A), wb=_pad_head_rows(_join_cols(branch4[:, r:]), N_HEADS_B))


def _mixer_grad_shards(g):
    dw = g["win"]
    win_ref = jnp.concatenate([_unpad_head_cols(dw[:, :_C_KA], N_HEADS_A, HEAD_DIM_A),
                               _unpad_head_cols(dw[:, _C_KA:_C_VA], N_KV_HEADS_A, HEAD_DIM_A),
                               _unpad_head_cols(dw[:, _C_VA:_C_CQ], N_KV_HEADS_A, HEAD_DIM_A), dw[:, _C_CQ:_C_KR],
                               dw[:, _C_KR + QK_NOPE_DIM:_C_KR + QK_DIM_B], dw[:, _C_GA:]], axis=1)
    rk = g["wk"].shape[0]
    ukv = jnp.concatenate([g["wk"].reshape(rk, N_HEADS_B, HEAD_PAD)[:, :, :QK_NOPE_DIM],
                           g["wv"].reshape(rk, N_HEADS_B, HEAD_PAD)[:, :, :V_DIM_B]], axis=2).reshape(rk, -1)
    return (_split_cols(win_ref), _split_cols(_unpad_head_cols(g["wuq"], N_HEADS_B, QK_DIM_B)), _split_cols(ukv),
            jnp.concatenate([_split_cols(_unpad_head_rows(g["wa"], N_HEADS_A, HEAD_DIM_A)),
                             _split_cols(_unpad_head_rows(g["wb"], N_HEADS_B, V_DIM_B))], axis=1))


BUFFER_TAGS = ("col1", "col2", "down1", "down2", "wo", "win", "uq", "ukv", "branch")
FIRST_GATHER = ("col1", "down1")
GATHER_CARRIERS = {(0, "ffn1_up"): (0, ("wo", "win", "uq", "ukv", "branch")), (0, "proj"): (1, ("col1",)),
                   (0, "swa"): (1, ("down1", "wo", "uq", "ukv")), (0, "mla"): (0, ("col2", "down2")),
                   (0, "ffn2_up"): (1, ("win", "branch")), (1, "ffn1_up"): (1, ("col2",)), (1, "swa"): (1, ("down2",))}
SCATTER_CARRIERS = {(0, "ffn2_dh"): (1, ("col1",)), (0, "ffn2_dwgu"): (1, ("col2",)), (0, "ffn2_dn"): (1, ("down1", "down2", "wo")),
                    (0, "dswa"): (1, ("win", "uq", "ukv", "branch")), (0, "dmla"): (0, ("col2", "down2")),
                    (0, "ffn1_dh"): (0, ("wo", "win", "uq", "ukv", "branch"))}
LAST_SCATTER = ("col1", "down1")


def _pick(by_tag, tags):
    return [by_tag[t] for t in tags]


def _train_step(x, positions, target, packs, small, shard):
    depth = 2
    b, s, d = x.shape
    t = b * s
    posf = positions.astype(F32)
    half = QK_ROPE_DIM // 2
    inv_freq = ROPE_BASE ** (-jnp.arange(half, dtype=F32) / half)
    ang = posf[..., None] * inv_freq
    cos, sin = jnp.cos(ang), jnp.sin(ang)
    tail = HEAD_PAD - QK_DIM_B
    rope = (jnp.concatenate([jnp.ones((b, s, QK_NOPE_DIM), F32), cos, cos, jnp.ones((b, s, tail), F32)], axis=-1),
            jnp.concatenate([jnp.zeros((b, s, QK_NOPE_DIM), F32), sin, sin, jnp.zeros((b, s, tail), F32)], axis=-1))
    slopes = jnp.exp2(-8.0 * (jnp.arange(N_HEADS_A, dtype=F32) + 1.0) / N_HEADS_A)
    aux = dict(b=b, s=s, pos_col=posf.reshape(b, s, 1), pos_row=posf.reshape(b, 1, s), rope=rope, slopes=slopes)

    def sm_of(l):
        return {n: small[n][l:l + 1] for n in SMALL_NAMES}

    by_tag = dict(zip(BUFFER_TAGS, packs))
    wts, parts, slots = [{} for _ in range(depth)], [{} for _ in range(depth)], [{} for _ in range(depth)]
    saved, sgrads = [None] * depth, [None] * depth

    def gather_piece(l, kern):
        if (l, kern) not in GATHER_CARRIERS:
            return None
        data_l, tags = GATHER_CARRIERS[(l, kern)]
        return gather_plan(_pick(by_tag, tags), data_l)

    def gathered(l, kern, arrays):
        if (l, kern) in GATHER_CARRIERS:
            data_l, tags = GATHER_CARRIERS[(l, kern)]
            wts[data_l].update(zip(tags, arrays))

    def scatter_piece(l, kern):
        if (l, kern) not in SCATTER_CARRIERS:
            return None
        data_l, tags = SCATTER_CARRIERS[(l, kern)]
        return scatter_plan(_pick(parts[data_l], tags), data_l)

    def scattered(l, kern, arrays):
        if (l, kern) in SCATTER_CARRIERS:
            data_l, tags = SCATTER_CARRIERS[(l, kern)]
            slots[data_l].update(zip(tags, arrays))

    def chip_sums(l, tags, mine, theirs=None):
        if theirs is None:
            theirs = comm_call(f"swap_l{l}_{tags[0]}", swap_plan(_pick(mine, tags), l))
        for tag, o in zip(tags, theirs):
            g = mine[tag]
            parts[l][tag] = add_pair(f"l{l}_add_{tag}", g.reshape(-1, g.shape[-1]), o.reshape(-1, o.shape[-1])).reshape(g.shape)

    wts[0].update(zip(FIRST_GATHER, comm_call("gather_first", gather_plan(_pick(by_tag, FIRST_GATHER), 0))))
    h = x.reshape(t, d)
    for l in range(depth):
        sm, wl = sm_of(l), wts[l]
        h, s1, got = _ffn_fwd(f"l{l}_ffn1", h, sm["ffn1_norm"], wl["col1"], wl["down1"], gather_piece(l, "ffn1_up"))
        gathered(l, "ffn1_up", got)
        weights = _layer_weights(wl["win"], wl["uq"], wl["ukv"], wl["branch"])
        h, s2, got = _mixer_fwd(f"l{l}_mix", h, sm, weights, wl["wo"], aux, {k: gather_piece(l, k) for k in ("proj", "swa", "mla")})
        for kern, arrays in got.items():
            gathered(l, kern, arrays)
        h, s3, got = _ffn_fwd(f"l{l}_ffn2", h, sm["ffn2_norm"], wl["col2"], wl["down2"], gather_piece(l, "ffn2_up"))
        gathered(l, "ffn2_up", got)
        saved[l] = (s1, s2, s3, weights)
    loss, dh, dhb = loss_fwd_bwd("loss", h, target.reshape(t, d))

    for l in reversed(range(depth)):
        sm, wl = sm_of(l), wts[l]
        s1, s2, s3, weights = saved[l]
        staged = l == 0
        dh, dhb, dg_f2, gcol2, gdown2, got = _ffn_bwd(f"l{l}_ffn2", s3, sm["ffn2_norm"], wl["col2"], wl["down2"], dh, dhb,
                                                      {k: scatter_piece(l, "ffn2_" + k) for k in ("dh", "dwgu", "dn")})
        for kern, arrays in got.items():
            scattered(l, "ffn2_" + kern, arrays)
        mine = dict(col2=gcol2, down2=gdown2)
        mplans = {"dswa": scatter_piece(l, "dswa")}
        if staged:
            mplans["d_out"] = swap_plan(_pick(mine, ("col2", "down2")), l)

            def after_swap(carried, l=l, mine=mine):
                chip_sums(l, ("col2", "down2"), mine, carried["d_out"])
                return scatter_piece(l, "dmla")

            mplans["dmla"] = after_swap
        dh, dhb, wg, sg, got = _mixer_bwd(f"l{l}_mix", s2, sm, weights, wl["wo"], aux, dh, dhb, mplans)
        for kern, arrays in got.items():
            scattered(l, kern, arrays)
        gwin, guq, gukv, gbranch = _mixer_grad_shards(wg)
        mine.update(wo=wg["wo"], win=gwin, uq=guq, ukv=gukv, branch=gbranch)
        if staged:
            chip_sums(l, ("wo", "win", "uq", "ukv", "branch"), mine)
        last_tags = LAST_SCATTER if staged else BUFFER_TAGS

        def last_swap(gcol1, gdown1, l=l, mine=mine, last_tags=last_tags):
            return swap_plan(_pick(dict(mine, col1=gcol1, down1=gdown1), last_tags), l)

        plans = {k: scatter_piece(l, "ffn1_" + k) for k in ("dh", "dwgu")}
        plans["dn"] = last_swap
        dh, dhb, dg_f1, gcol1, gdown1, got = _ffn_bwd(f"l{l}_ffn1", s1, sm["ffn1_norm"], wl["col1"], wl["down1"], dh, dhb, plans)
        theirs = got.pop("dn")
        for kern, arrays in got.items():
            scattered(l, "ffn1_" + kern, arrays)
        mine.update(col1=gcol1, down1=gdown1)
        chip_sums(l, last_tags, mine, theirs)
        sg.update(ffn1_norm=dg_f1, ffn2_norm=dg_f2)
        sgrads[l] = sg
    slots[0].update(zip(LAST_SCATTER, comm_call("scatter_last", scatter_plan(_pick(parts[0], LAST_SCATTER), 0))))
    finals = [[sum_slots(f"l{l}_sum_{tag}", slots[l][tag], parts[l][tag], shard) for tag in BUFFER_TAGS] for l in range(depth)]
    shared = comm_call("share", share_plan(finals))
    nk = len(BUFFER_TAGS)
    return loss, dh.reshape(b, s, d), [dict(zip(BUFFER_TAGS, shared[l * nk:(l + 1) * nk])) for l in range(depth)], sgrads


def kernel(x, positions, ffn1_norm, ffn1_w_gate, ffn1_w_up, ffn1_w_down, mix_norm, w_in, swa_q_norm, swa_k_norm, swa_sinks, mla_q_lora_norm, mla_w_uq, mla_kv_lora_norm, mla_w_ukv, mla_q_norm, mla_k_norm, w_branch_a, w_branch_b, w_out, ffn2_norm, ffn2_w_gate, ffn2_w_up, ffn2_w_down, loss_target, m_ffn1_norm, m_ffn1_w_gate, m_ffn1_w_up, m_ffn1_w_down, m_mix_norm, m_w_in, m_swa_q_norm, m_swa_k_norm, m_swa_sinks, m_mla_q_lora_norm, m_mla_w_uq, m_mla_kv_lora_norm, m_mla_w_ukv, m_mla_q_norm, m_mla_k_norm, m_w_branch_a, m_w_branch_b, m_w_out, m_ffn2_norm, m_ffn2_w_gate, m_ffn2_w_up, m_ffn2_w_down, v_ffn1_norm, v_ffn1_w_gate, v_ffn1_w_up, v_ffn1_w_down, v_mix_norm, v_w_in, v_swa_q_norm, v_swa_k_norm, v_swa_sinks, v_mla_q_lora_norm, v_mla_w_uq, v_mla_kv_lora_norm, v_mla_w_ukv, v_mla_q_norm, v_mla_k_norm, v_w_branch_a, v_w_branch_b, v_w_out, v_ffn2_norm, v_ffn2_w_gate, v_ffn2_w_up, v_ffn2_w_down):
    args = (x, positions, ffn1_norm, ffn1_w_gate, ffn1_w_up, ffn1_w_down, mix_norm, w_in, swa_q_norm, swa_k_norm, swa_sinks, mla_q_lora_norm, mla_w_uq, mla_kv_lora_norm, mla_w_ukv, mla_q_norm, mla_k_norm, w_branch_a, w_branch_b, w_out, ffn2_norm, ffn2_w_gate, ffn2_w_up, ffn2_w_down, loss_target, m_ffn1_norm, m_ffn1_w_gate, m_ffn1_w_up, m_ffn1_w_down, m_mix_norm, m_w_in, m_swa_q_norm, m_swa_k_norm, m_swa_sinks, m_mla_q_lora_norm, m_mla_w_uq, m_mla_kv_lora_norm, m_mla_w_ukv, m_mla_q_norm, m_mla_k_norm, m_w_branch_a, m_w_branch_b, m_w_out, m_ffn2_norm, m_ffn2_w_gate, m_ffn2_w_up, m_ffn2_w_down, v_ffn1_norm, v_ffn1_w_gate, v_ffn1_w_up, v_ffn1_w_down, v_mix_norm, v_w_in, v_swa_q_norm, v_swa_k_norm, v_swa_sinks, v_mla_q_lora_norm, v_mla_w_uq, v_mla_kv_lora_norm, v_mla_w_ukv, v_mla_q_norm, v_mla_k_norm, v_w_branch_a, v_w_branch_b, v_w_out, v_ffn2_norm, v_ffn2_w_gate, v_ffn2_w_up, v_ffn2_w_down)
    a = dict(zip(ARG_NAMES, args, strict=True))
    x = a["x"]
    depth = a["ffn1_norm"].shape[0]
    d = x.shape[-1]
    assert depth == 2, "the exchanges of layer l are driven by core l of every chip: two layers, two cores"

    def bf16_rows(*names):
        return jnp.concatenate([a[n] for n in names], axis=1).astype(BF16) if len(names) > 1 else a[names[0]].astype(BF16)

    packs = [bf16_rows("ffn1_w_gate", "ffn1_w_up"), bf16_rows("ffn2_w_gate", "ffn2_w_up"), bf16_rows("ffn1_w_down"),
             bf16_rows("ffn2_w_down"), bf16_rows("w_out"), bf16_rows("w_in"), bf16_rows("mla_w_uq"), bf16_rows("mla_w_ukv"),
             bf16_rows("w_branch_a", "w_branch_b")]
    shard_me = (2 * lax.axis_index("x") + lax.axis_index("y")).astype(jnp.int32).reshape(1)
    small = {n: a[n] for n in SMALL_NAMES}

    loss, grad_x, summed, sgrads = _train_step(x, a["positions"], a["loss_target"], packs, small, shard_me)

    full = {tag: jnp.stack([summed[l][tag] for l in range(depth)]) for tag in BUFFER_TAGS}
    r_branch = a["w_branch_a"].shape[1]
    grads = dict(ffn1_w_gate=full["col1"][:, :d], ffn1_w_up=full["col1"][:, d:], ffn2_w_gate=full["col2"][:, :d],
                 ffn2_w_up=full["col2"][:, d:], ffn1_w_down=full["down1"], ffn2_w_down=full["down2"], w_out=full["wo"],
                 w_in=full["win"], mla_w_uq=full["uq"], mla_w_ukv=full["ukv"], w_branch_a=full["branch"][:, :r_branch],
                 w_branch_b=full["branch"][:, r_branch:])

    flat = jnp.concatenate([jnp.concatenate([sgrads[l][n].reshape(-1) for l in range(depth)]) for n in SMALL_NAMES] + [loss.reshape(-1)])
    n_small = flat.shape[0]
    rows = -(-n_small // (8 * LANES)) * 8
    pad = rows * LANES - n_small

    def small_pack(v):
        return jnp.pad(v, (0, pad)).reshape(rows, LANES)

    total = small_allreduce(small_pack(flat))
    w_s, m_s, v_s = (small_pack(jnp.concatenate([a[p + n].reshape(-1) for n in SMALL_NAMES] + [jnp.zeros((1,), F32)]))
                     for p in ("", "m_", "v_"))
    d_s, nm_s, nv_s = adamw("adamw_small", w_s, total, m_s, v_s)

    def small_unpack(buf):
        out, off, flat_b = {}, 0, buf.reshape(-1)
        for n in SMALL_NAMES:
            size = a[n].shape[0] * a[n].shape[1]
            out[n] = flat_b[off:off + size].reshape(a[n].shape)
            off += size
        return out

    grads.update(small_unpack(total))
    delta, new_m, new_v = small_unpack(d_s), small_unpack(nm_s), small_unpack(nv_s)
    for n in PACK_NAMES:
        flip = a[n].shape[2] % LANES != 0

        def view(z):
            return z.transpose(0, 2, 1) if flip else z

        w_v, g_v, m_v, v_v = view(a[n]), view(grads[n]), view(a["m_" + n]), view(a["v_" + n])
        shp = w_v.shape
        two_d = (shp[0] * shp[1], shp[2])
        dn, mn, vn = adamw("adamw_" + n, w_v.reshape(two_d), g_v.reshape(two_d), m_v.reshape(two_d), v_v.reshape(two_d))
        delta[n], new_m[n], new_v[n] = view(dn.reshape(shp)), view(mn.reshape(shp)), view(vn.reshape(shp))
        if flip:
            grads[n] = view(g_v)

    loss_out = total.reshape(-1)[n_small - 1]
    return (loss_out, grad_x, *[grads[n] for n in WEIGHT_NAMES], *[delta[n] for n in WEIGHT_NAMES],
            *[new_m[n] for n in WEIGHT_NAMES], *[new_v[n] for n in WEIGHT_NAMES])
```

```python
import functools

import numpy as np
import jax
import jax.numpy as jnp
from jax import lax
from jax.experimental import pallas as pl
from jax.experimental.pallas import tpu as pltpu

F32 = jnp.float32
BF16 = jnp.bfloat16
MESH = pl.DeviceIdType.MESH

HEAD_DIM_A = 64
N_HEADS_A = 8
N_KV_HEADS_A = 2
GROUP_A = N_HEADS_A // N_KV_HEADS_A
BLOCK = 128
N_HEADS_B = 8
Q_LORA_RANK = 256
KV_LORA_RANK = 128
QK_NOPE_DIM = 64
QK_ROPE_DIM = 32
QK_DIM_B = QK_NOPE_DIM + QK_ROPE_DIM
V_DIM_B = 64
ROPE_BASE = 10000.0
WIDTH_A = N_HEADS_A * HEAD_DIM_A
KV_WIDTH_A = N_KV_HEADS_A * HEAD_DIM_A
EPS = 1e-6
NEG = -1e30
ADAM_LR = 0.001
ADAM_B1 = 0.9
ADAM_B2 = 0.999
ADAM_EPS = 1e-08
ADAM_WD = 0.01
ADAM_STEP = 10

N_SHARDS = 4
N_DEV = 8
LANES = 128
VMEM_LIMIT = 48 * 1024 * 1024

PACK_NAMES = ("ffn1_w_gate", "ffn1_w_up", "ffn1_w_down", "w_in", "mla_w_uq", "mla_w_ukv",
              "w_branch_a", "w_branch_b", "w_out", "ffn2_w_gate", "ffn2_w_up", "ffn2_w_down")
SMALL_NAMES = ("ffn1_norm", "mix_norm", "swa_q_norm", "swa_k_norm", "swa_sinks", "mla_q_lora_norm",
               "mla_kv_lora_norm", "mla_q_norm", "mla_k_norm", "ffn2_norm")
WEIGHT_NAMES = ("ffn1_norm", "ffn1_w_gate", "ffn1_w_up", "ffn1_w_down", "mix_norm", "w_in", "swa_q_norm",
                "swa_k_norm", "swa_sinks", "mla_q_lora_norm", "mla_w_uq", "mla_kv_lora_norm", "mla_w_ukv",
                "mla_q_norm", "mla_k_norm", "w_branch_a", "w_branch_b", "w_out", "ffn2_norm", "ffn2_w_gate",
                "ffn2_w_up", "ffn2_w_down")


def _params(sem):
    return pltpu.CompilerParams(dimension_semantics=sem, vmem_limit_bytes=VMEM_LIMIT)


def _tile(n, want, align):
    if n <= want:
        return n
    t = (want // align) * align
    while t > align and n % t:
        t -= align
    assert t >= align and n % t == 0, (n, want, align)
    return t


def _mm_call(name, a, b, out_struct, grid, a_spec, b_spec, o_spec, dims, n_red, acc_shape, alpha=1.0, res=None,
             inner=0, a_cols=0, plan=None):
    if n_red and all(g == 1 for g in grid[len(grid) - n_red:]):
        n_red = 0
    n_par = len(grid) - n_red
    n_in = 2 + int(res is not None)
    p_ins, p_in_specs, p_outs, p_out_specs, p_scratch, semantics, at_start, at_end = _hosted(
        plan, n_in, 1, grid, ("parallel",) * n_par + ("arbitrary",) * n_red)

    def body(*refs):
        at_start(refs)
        compute(refs)
        at_end(refs)

    def compute(refs):
        a_ref, b_ref = refs[:2]
        r_ref = refs[2] if res is not None else None
        o_ref, acc_ref = refs[n_in + len(p_ins)], refs[n_in + len(p_ins) + 1 + len(p_outs)]
        if inner:
            def a_of(s):
                return a_ref[:, s * a_cols:(s + 1) * a_cols] if a_cols else a_ref[s]

            part = lax.dot_general(a_of(0), b_ref[0], dims, preferred_element_type=F32)
            for s in range(1, inner):
                part = part + lax.dot_general(a_of(s), b_ref[s], dims, preferred_element_type=F32)
        else:
            part = lax.dot_general(a_ref[...], b_ref[...], dims, preferred_element_type=F32)

        def finish(total):
            if alpha != 1.0:
                total = total * alpha
            if r_ref is not None:
                total = r_ref[...] + total
            o_ref[...] = total.astype(o_ref.dtype)

        if n_red == 0:
            finish(part)
            return
        ids = [pl.program_id(n_par + i) for i in range(n_red)]
        first = functools.reduce(jnp.logical_and, [i == 0 for i in ids])
        last = functools.reduce(jnp.logical_and, [i == grid[n_par + k] - 1 for k, i in enumerate(ids)])

        @pl.when(first)
        def _():
            acc_ref[...] = part

        @pl.when(jnp.logical_not(first))
        def _():
            acc_ref[...] += part

        @pl.when(last)
        def _():
            finish(acc_ref[...])

    in_specs = [a_spec, b_spec] + ([o_spec] if res is not None else [])
    args = (a, b) + ((res,) if res is not None else ())
    if plan is None:
        return pl.pallas_call(
            body, name=name, out_shape=out_struct, grid=grid, in_specs=in_specs, out_specs=o_spec,
            scratch_shapes=[pltpu.VMEM(acc_shape, F32)], compiler_params=_params(semantics),
        )(*args)
    res_all = pl.pallas_call(
        body, name=name, out_shape=(out_struct,) + tuple(p_outs), grid=grid, in_specs=in_specs + p_in_specs,
        out_specs=(o_spec,) + tuple(p_out_specs), scratch_shapes=[pltpu.VMEM(acc_shape, F32)] + p_scratch,
        input_output_aliases={n_in + i: 1 + o for i, o in plan.aliases.items()}, compiler_params=_params(semantics),
    )(*args, *p_ins)
    return res_all[0], list(res_all[1:])


_NN = (((1,), (0,)), ((), ()))
_NT = (((1,), (1,)), ((), ()))
_TN = (((0,), (0,)), ((), ()))


def mm_nn(name, a, b, tm=512, tn=1024, tk=1024, out_dtype=F32, alpha=1.0, res=None, plan=None):
    (m, k), (_, n) = a.shape, b.shape
    tm, tn, tk = _tile(m, tm, 16), _tile(n, tn, LANES), _tile(k, tk, LANES)
    return _mm_call(name, a, b, jax.ShapeDtypeStruct((m, n), out_dtype), (m // tm, n // tn, k // tk),
                    pl.BlockSpec((tm, tk), lambda i, j, kk: (i, kk)), pl.BlockSpec((tk, tn), lambda i, j, kk: (kk, j)),
                    pl.BlockSpec((tm, tn), lambda i, j, kk: (i, j)), _NN, 1, (tm, tn), alpha, res, plan=plan)


def mm_nt(name, a, b, tm=512, tn=1024, tk=1024, out_dtype=F32, alpha=1.0, res=None):
    (m, n), (k, _) = a.shape, b.shape
    tm, tn, tk = _tile(m, tm, 16), _tile(k, tn, LANES), _tile(n, tk, LANES)
    return _mm_call(name, a, b, jax.ShapeDtypeStruct((m, k), out_dtype), (m // tm, k // tn, n // tk),
                    pl.BlockSpec((tm, tk), lambda i, j, kk: (i, kk)), pl.BlockSpec((tn, tk), lambda i, j, kk: (j, kk)),
                    pl.BlockSpec((tm, tn), lambda i, j, kk: (i, j)), _NT, 1, (tm, tn), alpha, res)


def mm_tn(name, a, b, tm=1024, tn=1024, tk=1024, out_dtype=F32, alpha=1.0):
    (m, k), (_, n) = a.shape, b.shape
    tm, tn, tk = _tile(k, tm, LANES), _tile(n, tn, LANES), _tile(m, tk, 16)
    return _mm_call(name, a, b, jax.ShapeDtypeStruct((k, n), out_dtype), (k // tm, n // tn, m // tk),
                    pl.BlockSpec((tk, tm), lambda i, j, kk: (kk, i)), pl.BlockSpec((tk, tn), lambda i, j, kk: (kk, j)),
                    pl.BlockSpec((tm, tn), lambda i, j, kk: (i, j)), _TN, 1, (tm, tn), alpha)


def ffn_up_act(name, a, w, blk, tm=1024, plan=None):
    (m, k), (ns, _, n) = a.shape, w.shape
    tm = _tile(m, tm, 16)
    grid = (ns, m // tm)
    p_ins, p_in_specs, p_outs, p_out_specs, p_scratch, semantics, at_start, at_end = _hosted(plan, 3, 2, grid, ("parallel", "parallel"))

    def body(*refs):
        a_ref, wg_ref, wu_ref = refs[:3]
        gu_ref, h_ref = refs[3 + len(p_ins):5 + len(p_ins)]
        at_start(refs)
        av = a_ref[...]
        gate = jnp.dot(av, wg_ref[...], preferred_element_type=F32)
        up = jnp.dot(av, wu_ref[...], preferred_element_type=F32)
        gu_ref[0] = gate.astype(BF16)
        gu_ref[1] = up.astype(BF16)
        h_ref[...] = (gate * jax.nn.sigmoid(gate) * up).astype(BF16)
        at_end(refs)

    res = pl.pallas_call(
        body, name=name, grid=grid,
        out_shape=(jax.ShapeDtypeStruct((2, ns, m, n), BF16), jax.ShapeDtypeStruct((ns, m, n), BF16)) + tuple(p_outs),
        in_specs=[pl.BlockSpec((tm, k), lambda s, i: (i, 0)), pl.BlockSpec((None, k, n), lambda s, i: (s, blk, 0)),
                  pl.BlockSpec((None, k, n), lambda s, i: (s, blk + 1, 0))] + p_in_specs,
        out_specs=(pl.BlockSpec((2, None, tm, n), lambda s, i: (0, s, i, 0)), pl.BlockSpec((None, tm, n), lambda s, i: (s, i, 0)))
        + tuple(p_out_specs),
        scratch_shapes=p_scratch, compiler_params=_params(semantics),
    )(a, w, w, *p_ins)
    return res[0], res[1], list(res[2:])


def ffn_down_dact(name, a, w, gu, n, blk, alpha, tm=1024, plan=None):
    (m, d), ns = a.shape, w.shape[0]
    tm = _tile(m, tm, 16)
    grid = (ns, m // tm)
    p_ins, p_in_specs, p_outs, p_out_specs, p_scratch, semantics, at_start, at_end = _hosted(plan, 3, 1, grid, ("parallel", "parallel"))

    def body(*refs):
        a_ref, w_ref, gu_ref = refs[:3]
        o_ref = refs[3 + len(p_ins)]
        at_start(refs)
        dh = lax.dot_general(a_ref[...], w_ref[...], _NT, preferred_element_type=F32) * alpha
        gate, up = gu_ref[0].astype(F32), gu_ref[1].astype(F32)
        s = jax.nn.sigmoid(gate)
        o_ref[0] = (dh * up * (s * (1.0 + gate * (1.0 - s)))).astype(BF16)
        o_ref[1] = (dh * (gate * s)).astype(BF16)
        at_end(refs)

    gu_spec = pl.BlockSpec((2, None, tm, n), lambda s, i: (0, s, i, 0))
    res = pl.pallas_call(
        body, name=name, grid=grid, out_shape=(jax.ShapeDtypeStruct((2, ns, m, n), BF16),) + tuple(p_outs),
        in_specs=[pl.BlockSpec((tm, d), lambda s, i: (i, 0)), pl.BlockSpec((None, n, d), lambda s, i: (s, blk, 0)), gu_spec] + p_in_specs,
        out_specs=(gu_spec,) + tuple(p_out_specs), scratch_shapes=p_scratch, compiler_params=_params(semantics),
    )(a, w, gu, *p_ins)
    return res[0], list(res[1:])


def gmm_up_dw(name, a, dgu, tk=2048, plan=None):
    (m, k), (_, ns, _, n) = a.shape, dgu.shape
    tk = _tile(m, tk, 16)
    return _mm_call(name, a, dgu, jax.ShapeDtypeStruct((ns, 2 * k, n), BF16), (2, ns, m // tk),
                    pl.BlockSpec((tk, k), lambda j, s, kk: (kk, 0)), pl.BlockSpec((None, None, tk, n), lambda j, s, kk: (j, s, kk, 0)),
                    pl.BlockSpec((None, k, n), lambda j, s, kk: (s, j, 0)), _TN, 1, (k, n), plan=plan)


def gmm_up_dx(name, dgu, w, k, blk, tm=512, plan=None):
    _, ns, m, n = dgu.shape
    tm = _tile(m, tm, 16)
    return _mm_call(name, dgu, w, jax.ShapeDtypeStruct((m, k), F32), (m // tm, 2),
                    pl.BlockSpec((None, ns, tm, n), lambda i, j: (j, 0, i, 0)),
                    pl.BlockSpec((ns, k, n), lambda i, j: (0, blk + j, 0)),
                    pl.BlockSpec((tm, k), lambda i, j: (i, 0)), _NT, 1, (tm, k), inner=ns, plan=plan)


def gmm_down(name, h, w, blk, res, alpha, tm=512):
    (ns, m, n), d = h.shape, w.shape[2]
    tm = _tile(m, tm, 16)
    return _mm_call(name, h, w, jax.ShapeDtypeStruct((m, d), F32), (m // tm,),
                    pl.BlockSpec((ns, tm, n), lambda i: (0, i, 0)), pl.BlockSpec((ns, n, d), lambda i: (0, blk, 0)),
                    pl.BlockSpec((tm, d), lambda i: (i, 0)), _NN, 0, (8, LANES), alpha, res, inner=ns)


def gmm_down_dw(name, h, b, alpha, tk=2048):
    (ns, m, n), d = h.shape, b.shape[1]
    tk = _tile(m, tk, 16)
    return _mm_call(name, h, b, jax.ShapeDtypeStruct((ns, n, d), BF16), (ns, m // tk),
                    pl.BlockSpec((None, tk, n), lambda s, kk: (s, kk, 0)), pl.BlockSpec((tk, d), lambda s, kk: (kk, 0)),
                    pl.BlockSpec((None, n, d), lambda s, kk: (s, 0, 0)), _TN, 1, (n, d), alpha)


def gmm_rows(name, a, w, res, tm=1024):
    (m, _), (ns, r, d) = a.shape, w.shape
    tm = _tile(m, tm, 16)
    return _mm_call(name, a, w, jax.ShapeDtypeStruct((m, d), F32), (m // tm,),
                    pl.BlockSpec((tm, ns * r), lambda i: (i, 0)), pl.BlockSpec((ns, r, d), lambda i: (0, 0, 0)),
                    pl.BlockSpec((tm, d), lambda i: (i, 0)), _NN, 0, (8, LANES), 1.0, res, inner=ns, a_cols=r)


def gmm_rows_dx(name, a, w, tm=1024, plan=None):
    (m, d), (ns, r, _) = a.shape, w.shape
    tm = _tile(m, tm, 16)
    return _mm_call(name, a, w, jax.ShapeDtypeStruct((m, ns * r), F32), (ns, m // tm),
                    pl.BlockSpec((tm, d), lambda s, i: (i, 0)), pl.BlockSpec((None, r, d), lambda s, i: (s, 0, 0)),
                    pl.BlockSpec((tm, r), lambda s, i: (i, s)), _NT, 0, (8, LANES), plan=plan)


def gmm_rows_dw(name, a, b, tk=2048):
    (m, da), d = a.shape, b.shape[1]
    ns, r = N_SHARDS, da // N_SHARDS
    tk = _tile(m, tk, 16)
    return _mm_call(name, a, b, jax.ShapeDtypeStruct((ns, r, d), BF16), (ns, m // tk),
                    pl.BlockSpec((tk, r), lambda s, kk: (kk, s)), pl.BlockSpec((tk, d), lambda s, kk: (kk, 0)),
                    pl.BlockSpec((None, r, d), lambda s, kk: (s, 0, 0)), _TN, 1, (r, d))


def rms_fwd(name, x, gain, tm=1024, col_blk=0):
    m, d = x.shape[0], gain.shape[1]
    tm = _tile(m, tm, 16)

    def body(x_ref, g_ref, o_ref):
        xv = x_ref[...]
        r = lax.rsqrt(jnp.mean(xv * xv, axis=-1, keepdims=True) + EPS)
        o_ref[...] = (xv * r * g_ref[...]).astype(o_ref.dtype)

    return pl.pallas_call(
        body, name=name, out_shape=jax.ShapeDtypeStruct((m, d), BF16), grid=(m // tm,),
        in_specs=[pl.BlockSpec((tm, d), lambda i: (i, col_blk)), pl.BlockSpec((1, d), lambda i: (0, 0))],
        out_specs=pl.BlockSpec((tm, d), lambda i: (i, 0)), compiler_params=_params(("parallel",)),
    )(x, gain)


def rms_bwd(name, x, gain, dn, dres=None, want_f32=True, want_bf16=True, tm=512, col_blk=0):
    m, d = x.shape[0], gain.shape[1]
    tm = _tile(m, tm, 16)
    n_out = int(want_f32) + int(want_bf16)

    def body(*refs):
        x_ref, g_ref, dn_ref = refs[:3]
        pos = 3
        r_ref = None
        if dres is not None:
            r_ref = refs[pos]
            pos += 1
        outs = refs[pos:pos + n_out]
        dg_ref = refs[pos + n_out]
        xv = x_ref[...]
        r = lax.rsqrt(jnp.mean(xv * xv, axis=-1, keepdims=True) + EPS)
        xhat = xv * r
        dnv = dn_ref[...]
        dxhat = dnv * g_ref[...]
        dx = r * (dxhat - xhat * jnp.mean(dxhat * xhat, axis=-1, keepdims=True))
        if r_ref is not None:
            dx = r_ref[...] + dx
        for o in outs:
            o[...] = dx.astype(o.dtype)
        part = jnp.sum(dnv * xhat, axis=0, keepdims=True)

        @pl.when(pl.program_id(0) == 0)
        def _():
            dg_ref[...] = part

        @pl.when(pl.program_id(0) > 0)
        def _():
            dg_ref[...] += part

    row = pl.BlockSpec((tm, d), lambda i: (i, 0))
    vec = pl.BlockSpec((1, d), lambda i: (0, 0))
    out_shape = ([jax.ShapeDtypeStruct((m, d), F32)] if want_f32 else []) + ([jax.ShapeDtypeStruct((m, d), BF16)] if want_bf16 else [])
    res = pl.pallas_call(
        body, name=name, out_shape=tuple(out_shape) + (jax.ShapeDtypeStruct((1, d), F32),), grid=(m // tm,),
        in_specs=[pl.BlockSpec((tm, d), lambda i: (i, col_blk)), vec, row] + ([row] if dres is not None else []),
        out_specs=tuple([row] * n_out) + (vec,), compiler_params=_params(("arbitrary",)),
    )(*((x, gain, dn) + ((dres,) if dres is not None else ())))
    return res


def gate_fwd(name, proj, ya, yb, off_a, off_b, tm=256):
    m, d = ya.shape
    tm = _tile(m, tm, 16)
    blk_a, blk_b = off_a // d, off_b // d
    assert blk_a * d == off_a and blk_b * d == off_b, "the gates start on multiples of their width"

    def body(ga_ref, gb_ref, ya_ref, yb_ref, o_ref):
        o_ref[...] = (jax.nn.sigmoid(ga_ref[...]) * ya_ref[...] + jax.nn.sigmoid(gb_ref[...]) * yb_ref[...]).astype(o_ref.dtype)

    row = pl.BlockSpec((tm, d), lambda i: (i, 0))
    return pl.pallas_call(
        body, name=name, out_shape=jax.ShapeDtypeStruct((m, d), BF16), grid=(m // tm,),
        in_specs=[pl.BlockSpec((tm, d), lambda i: (i, blk_a)), pl.BlockSpec((tm, d), lambda i: (i, blk_b)), row, row], out_specs=row,
        compiler_params=_params(("parallel",)),
    )(proj, proj, ya, yb)


def gate_bwd(name, proj, ya, yb, dmerged, off_a, off_b, tm=256):
    m, d = ya.shape
    tm = _tile(m, tm, 16)
    blk_a, blk_b = off_a // d, off_b // d
    assert blk_a * d == off_a and blk_b * d == off_b, "the gates start on multiples of their width"

    def body(ga_ref, gb_ref, ya_ref, yb_ref, dm_ref, dya_ref, dyb_ref, dga_ref, dgb_ref):
        sa, sb = jax.nn.sigmoid(ga_ref[...]), jax.nn.sigmoid(gb_ref[...])
        dm = dm_ref[...]
        dya_ref[...] = (dm * sa).astype(BF16)
        dyb_ref[...] = (dm * sb).astype(BF16)
        dga_ref[...] = (dm * ya_ref[...] * (sa * (1.0 - sa))).astype(BF16)
        dgb_ref[...] = (dm * yb_ref[...] * (sb * (1.0 - sb))).astype(BF16)

    row = pl.BlockSpec((tm, d), lambda i: (i, 0))
    o = jax.ShapeDtypeStruct((m, d), BF16)
    return pl.pallas_call(
        body, name=name, out_shape=(o, o, o, o), grid=(m // tm,),
        in_specs=[pl.BlockSpec((tm, d), lambda i: (i, blk_a)), pl.BlockSpec((tm, d), lambda i: (i, blk_b)), row, row, row],
        out_specs=(row, row, row, row), compiler_params=_params(("parallel",)),
    )(proj, proj, ya, yb, dmerged)


def loss_fwd_bwd(name, y, target, tm=512):
    m, d = y.shape
    tm = _tile(m, tm, 16)

    def body(y_ref, t_ref, l_ref, dy_ref, dyb_ref):
        err = y_ref[...] - t_ref[...]
        dy = err * (1.0 / d)
        dy_ref[...] = dy
        dyb_ref[...] = dy.astype(BF16)
        part = 0.5 * jnp.sum(jnp.mean(err * err, axis=-1, keepdims=True), axis=0, keepdims=True)

        @pl.when(pl.program_id(0) == 0)
        def _():
            l_ref[...] = part

        @pl.when(pl.program_id(0) > 0)
        def _():
            l_ref[...] += part

    row = pl.BlockSpec((tm, d), lambda i: (i, 0))
    return pl.pallas_call(
        body, name=name, grid=(m // tm,),
        out_shape=(jax.ShapeDtypeStruct((1, 1), F32), jax.ShapeDtypeStruct((m, d), F32), jax.ShapeDtypeStruct((m, d), BF16)),
        in_specs=[row, row], out_specs=(pl.BlockSpec((1, 1), lambda i: (0, 0)), row, row),
        compiler_params=_params(("arbitrary",)),
    )(y, target)


def adamw(name, w, g, m, v):
    r, c = w.shape
    cap = (2 * 1024 * 1024) // 4
    tr, tc = _tile(r, 1024, 8), c
    while tr * tc > cap and tc % (2 * LANES) == 0:
        tc //= 2
    while tr * tc > cap and tr % 16 == 0:
        tr //= 2
    c1 = np.float32(1.0 - ADAM_B1 ** ADAM_STEP)
    c2 = np.float32(1.0 - ADAM_B2 ** ADAM_STEP)

    def body(w_ref, g_ref, m_ref, v_ref, d_ref, nm_ref, nv_ref):
        gv = g_ref[...]
        nm = ADAM_B1 * m_ref[...] + (1.0 - ADAM_B1) * gv
        nv = ADAM_B2 * v_ref[...] + (1.0 - ADAM_B2) * (gv * gv)
        d_ref[...] = -ADAM_LR * ((nm / c1) / (jnp.sqrt(nv / c2) + ADAM_EPS) + ADAM_WD * w_ref[...])
        nm_ref[...] = nm
        nv_ref[...] = nv

    blk = pl.BlockSpec((tr, tc), lambda i, j: (i, j))
    o = jax.ShapeDtypeStruct((r, c), F32)
    return pl.pallas_call(
        body, name=name, out_shape=(o, o, o), grid=(r // tr, c // tc), in_specs=[blk] * 4, out_specs=(blk, blk, blk),
        compiler_params=_params(("parallel", "parallel")),
    )(w, g, m, v)


HEAD_PAD = LANES


def _rope_rot():
    r = np.zeros((HEAD_PAD, HEAD_PAD), np.float32)
    half = QK_ROPE_DIM // 2
    for j in range(half):
        r[QK_NOPE_DIM + half + j, QK_NOPE_DIM + j] = -1.0
        r[QK_NOPE_DIM + j, QK_NOPE_DIM + half + j] = 1.0
    return r


def _head_spec(s, blk0):
    return pl.BlockSpec((None, s, HEAD_PAD), lambda bi, hi: (bi, 0, blk0 + hi))


def qk_prep_fwd(name, x, blk0, n_heads, d_real, gain, rope=None, extra=None):
    b, s, _ = x.shape

    def body(*refs):
        x_ref, g_ref = refs[0], refs[1]
        pos = 2
        xv = x_ref[...]
        if extra is not None:
            xv = xv + refs[pos][...]
            pos += 1
        y = xv * lax.rsqrt(jnp.sum(xv * xv, axis=-1, keepdims=True) * (1.0 / d_real) + EPS) * g_ref[...]
        if rope is not None:
            c_ref, s_ref, r_ref = refs[pos:pos + 3]
            rot = jnp.dot(y, r_ref[...], precision=lax.Precision.HIGHEST, preferred_element_type=F32)
            y = y * c_ref[...] + rot * s_ref[...]
        refs[-1][...] = y.astype(BF16)

    vec = pl.BlockSpec((1, HEAD_PAD), lambda bi, hi: (0, 0))
    tab = pl.BlockSpec((None, s, HEAD_PAD), lambda bi, hi: (bi, 0, 0))
    in_specs, args = [_head_spec(s, blk0), vec], [x, gain]
    if extra is not None:
        e_blk = extra[1]
        in_specs.append(pl.BlockSpec((None, s, HEAD_PAD), lambda bi, hi: (bi, 0, e_blk)))
        args.append(extra[0])
    if rope is not None:
        in_specs += [tab, tab, pl.BlockSpec((HEAD_PAD, HEAD_PAD), lambda bi, hi: (0, 0))]
        args += [rope[0], rope[1], jnp.asarray(_rope_rot())]
    return pl.pallas_call(
        body, name=name, out_shape=jax.ShapeDtypeStruct((b, s, n_heads * HEAD_PAD), BF16), grid=(b, n_heads),
        in_specs=in_specs, out_specs=_head_spec(s, 0), compiler_params=_params(("parallel", "parallel")),
    )(*args)


def qk_prep_bwd(name, x, blk0, n_heads, d_real, gain, dxh, rope=None, extra=None, head_sum=False):
    b, s, _ = x.shape

    def body(*refs):
        x_ref, g_ref, dy_ref = refs[:3]
        pos = 3
        xv = x_ref[...]
        if extra is not None:
            xv = xv + refs[pos][...]
            pos += 1
        dy = dy_ref[...]
        if rope is not None:
            c_ref, s_ref, rt_ref = refs[pos:pos + 3]
            pos += 3
            dy = dy * c_ref[...] + jnp.dot(dy * s_ref[...], rt_ref[...], precision=lax.Precision.HIGHEST, preferred_element_type=F32)
        outs = refs[pos:]
        dx_ref, dg_ref = outs[0], outs[-1]
        r = lax.rsqrt(jnp.sum(xv * xv, axis=-1, keepdims=True) * (1.0 / d_real) + EPS)
        xhat = xv * r
        dxhat = dy * g_ref[...]
        dx = r * (dxhat - xhat * (jnp.sum(dxhat * xhat, axis=-1, keepdims=True) * (1.0 / d_real)))
        dx_ref[...] = dx.astype(dx_ref.dtype)
        part = jnp.sum(dy * xhat, axis=0, keepdims=True)
        first = jnp.logical_and(pl.program_id(0) == 0, pl.program_id(1) == 0)

        @pl.when(first)
        def _():
            dg_ref[...] = part

        @pl.when(jnp.logical_not(first))
        def _():
            dg_ref[...] += part

        if head_sum:
            hs_ref = outs[1]

            @pl.when(pl.program_id(1) == 0)
            def _():
                hs_ref[...] = dx

            @pl.when(pl.program_id(1) > 0)
            def _():
                hs_ref[...] += dx

    vec = pl.BlockSpec((1, HEAD_PAD), lambda bi, hi: (0, 0))
    tab = pl.BlockSpec((None, s, HEAD_PAD), lambda bi, hi: (bi, 0, 0))
    in_specs, args = [_head_spec(s, blk0), vec, _head_spec(s, 0)], [x, gain, dxh]
    if extra is not None:
        e_blk = extra[1]
        in_specs.append(pl.BlockSpec((None, s, HEAD_PAD), lambda bi, hi: (bi, 0, e_blk)))
        args.append(extra[0])
    if rope is not None:
        in_specs += [tab, tab, pl.BlockSpec((HEAD_PAD, HEAD_PAD), lambda bi, hi: (0, 0))]
        args += [rope[0], rope[1], jnp.asarray(_rope_rot().T.copy())]
    out_shape, out_specs = [jax.ShapeDtypeStruct((b, s, n_heads * HEAD_PAD), BF16)], [_head_spec(s, 0)]
    if head_sum:
        out_shape.append(jax.ShapeDtypeStruct((b, s, HEAD_PAD), F32))
        out_specs.append(tab)
    out_shape.append(jax.ShapeDtypeStruct((1, HEAD_PAD), F32))
    out_specs.append(vec)
    return pl.pallas_call(
        body, name=name, out_shape=tuple(out_shape), grid=(b, n_heads), in_specs=in_specs, out_specs=tuple(out_specs),
        compiler_params=_params(("arbitrary", "arbitrary")),
    )(*args)


def _swa_specs(v_blk0):
    q = pl.BlockSpec((None, BLOCK, GROUP_A * HEAD_PAD), lambda b, kv, n: (b, n, kv))
    kprev = pl.BlockSpec((None, BLOCK, HEAD_PAD), lambda b, kv, n: (b, jnp.maximum(n - 1, 0), kv))
    kcur = pl.BlockSpec((None, BLOCK, HEAD_PAD), lambda b, kv, n: (b, n, kv))
    vprev = pl.BlockSpec((None, BLOCK, HEAD_PAD), lambda b, kv, n: (b, jnp.maximum(n - 1, 0), v_blk0 + kv))
    vcur = pl.BlockSpec((None, BLOCK, HEAD_PAD), lambda b, kv, n: (b, n, v_blk0 + kv))
    pcol = pl.BlockSpec((None, BLOCK, 1), lambda b, kv, n: (b, n, 0))
    prow_prev = pl.BlockSpec((None, 1, BLOCK), lambda b, kv, n: (b, 0, jnp.maximum(n - 1, 0)))
    prow_cur = pl.BlockSpec((None, 1, BLOCK), lambda b, kv, n: (b, 0, n))
    smem = pl.BlockSpec(memory_space=pltpu.SMEM)
    return q, kprev, kcur, vprev, vcur, pcol, prow_prev, prow_cur, smem


def _swa_probs(q, kk, dist, valid, slope, sink):
    sc = lax.dot_general(q, kk, _NT, preferred_element_type=F32) * (HEAD_DIM_A ** -0.5)
    sc = sc - slope * dist
    sc = jnp.where(valid, sc, NEG)
    m = jnp.maximum(jnp.max(sc, axis=-1, keepdims=True), sink)
    e = jnp.exp(sc - m)
    es = jnp.exp(sink - m)
    inv = 1.0 / (jnp.sum(e, axis=-1, keepdims=True) + es)
    return e * inv, es * inv


def _swa_window(n, kp_ref, kc_ref, vp_ref, vc_ref, pc_ref, prp_ref, prc_ref):
    kk = jnp.concatenate([kp_ref[...], kc_ref[...]], axis=0)
    vv = jnp.concatenate([vp_ref[...], vc_ref[...]], axis=0).astype(BF16)
    dist = pc_ref[...] - jnp.concatenate([prp_ref[...], prc_ref[...]], axis=1)
    qi = lax.broadcasted_iota(jnp.int32, (BLOCK, 2 * BLOCK), 0) + BLOCK
    ki = lax.broadcasted_iota(jnp.int32, (BLOCK, 2 * BLOCK), 1)
    diff = qi - ki
    valid = (diff >= 0) & (diff < BLOCK) & ((n > 0) | (ki >= BLOCK))
    return kk, vv, dist, valid


def swa_fwd(name, q, k, vsrc, v_blk0, pos_col, pos_row, slopes, sinks, plan=None):
    b, s, _ = q.shape
    qs, kprev, kcur, vprev, vcur, pcol, prp, prc, smem = _swa_specs(v_blk0)
    grid = (b, N_KV_HEADS_A, s // BLOCK)
    p_ins, p_in_specs, p_outs, p_out_specs, p_scratch, semantics, at_start, at_end = _hosted(plan, 10, 1, grid, ("parallel",) * 3)

    def body(*refs):
        q_ref, kp_ref, kc_ref, vp_ref, vc_ref, pc_ref, prp_ref, prc_ref, sl_ref, sk_ref = refs[:10]
        o_ref = refs[10 + len(p_ins)]
        at_start(refs)
        kv, n = pl.program_id(1), pl.program_id(2)
        kk, vv, dist, valid = _swa_window(n, kp_ref, kc_ref, vp_ref, vc_ref, pc_ref, prp_ref, prc_ref)
        for g in range(GROUP_A):
            hd = kv * GROUP_A + g
            lanes = slice(g * HEAD_PAD, (g + 1) * HEAD_PAD)
            p, _ = _swa_probs(q_ref[:, lanes], kk, dist, valid, sl_ref[hd], sk_ref[hd])
            o_ref[:, lanes] = jnp.dot(p.astype(BF16), vv, preferred_element_type=F32).astype(BF16)
        at_end(refs)

    res = pl.pallas_call(
        body, name=name, out_shape=(jax.ShapeDtypeStruct(q.shape, BF16),) + tuple(p_outs), grid=grid,
        in_specs=[qs, kprev, kcur, vprev, vcur, pcol, prp, prc, smem, smem] + p_in_specs, out_specs=(qs,) + tuple(p_out_specs),
        scratch_shapes=p_scratch, compiler_params=_params(semantics),
    )(q, k, k, vsrc, vsrc, pos_col, pos_row, pos_row, slopes, sinks, *p_ins)
    return res[0], list(res[1:])


def swa_bwd(name, q, k, vsrc, v_blk0, pos_col, pos_row, slopes, sinks, do, plan=None):
    b, s, _ = q.shape
    qs, kprev, kcur, vprev, vcur, pcol, prp, prc, smem = _swa_specs(v_blk0)
    grid = (b, N_KV_HEADS_A, s // BLOCK)
    p_ins, p_in_specs, p_outs, p_out_specs, p_scratch, semantics, at_start, at_end = _hosted(plan, 11, 4, grid, ("arbitrary",) * 3)

    def body(*refs):
        at_start(refs)
        compute(*refs[:11], *refs[11 + len(p_ins):15 + len(p_ins)])
        at_end(refs)

    def compute(q_ref, kp_ref, kc_ref, vp_ref, vc_ref, pc_ref, prp_ref, prc_ref, sl_ref, sk_ref, do_ref, dq_ref, dk_ref, dv_ref, ds_ref):
        bi, kv, n = pl.program_id(0), pl.program_id(1), pl.program_id(2)
        kk, vv, dist, valid = _swa_window(n, kp_ref, kc_ref, vp_ref, vc_ref, pc_ref, prp_ref, prc_ref)

        @pl.when((bi == 0) & (kv == 0) & (n == 0))
        def _():
            ds_ref[...] = jnp.zeros_like(ds_ref)

        @pl.when(n == 0)
        def _():
            dk_ref[...] = jnp.zeros_like(dk_ref)
            dv_ref[...] = jnp.zeros_like(dv_ref)

        dkk = jnp.zeros((2 * BLOCK, HEAD_PAD), F32)
        dvv = jnp.zeros((2 * BLOCK, HEAD_PAD), F32)
        head_row = lax.broadcasted_iota(jnp.int32, (N_HEADS_A, LANES), 0)
        dsink = jnp.zeros((N_HEADS_A, LANES), F32)
        for g in range(GROUP_A):
            hd = kv * GROUP_A + g
            lanes = slice(g * HEAD_PAD, (g + 1) * HEAD_PAD)
            qg = q_ref[:, lanes]
            p, ps = _swa_probs(qg, kk, dist, valid, sl_ref[hd], sk_ref[hd])
            dob = do_ref[:, lanes].astype(BF16)
            dvv = dvv + lax.dot_general(p.astype(BF16), dob, _TN, preferred_element_type=F32)
            dp = lax.dot_general(dob, vv, _NT, preferred_element_type=F32)
            rs = jnp.sum(p * dp, axis=-1, keepdims=True)
            dsb = (p * (dp - rs) * (HEAD_DIM_A ** -0.5)).astype(BF16)
            dq_ref[:, lanes] = jnp.dot(dsb, kk, preferred_element_type=F32)
            dkk = dkk + lax.dot_general(dsb, qg, _TN, preferred_element_type=F32)
            dsink = dsink + jnp.where(head_row == hd, -jnp.sum(ps * rs), 0.0)
        ds_ref[...] += dsink

        @pl.when(n > 0)
        def _():
            start = pl.multiple_of((n - 1) * BLOCK, BLOCK)
            dk_ref[pl.ds(start, 2 * BLOCK), :] += dkk
            dv_ref[pl.ds(start, 2 * BLOCK), :] += dvv

        @pl.when(n == 0)
        def _():
            dk_ref[0:BLOCK, :] += dkk[BLOCK:, :]
            dv_ref[0:BLOCK, :] += dvv[BLOCK:, :]

    kv_full = pl.BlockSpec((None, s, HEAD_PAD), lambda bi, kv, n: (bi, 0, kv))
    kv_shape = jax.ShapeDtypeStruct(k.shape, F32)
    res = pl.pallas_call(
        body, name=name, grid=grid,
        out_shape=(jax.ShapeDtypeStruct(q.shape, F32), kv_shape, kv_shape, jax.ShapeDtypeStruct((N_HEADS_A, LANES), F32)) + tuple(p_outs),
        in_specs=[qs, kprev, kcur, vprev, vcur, pcol, prp, prc, smem, smem, qs] + p_in_specs,
        out_specs=(qs, kv_full, kv_full, pl.BlockSpec((N_HEADS_A, LANES), lambda bi, kv, n: (0, 0))) + tuple(p_out_specs),
        scratch_shapes=p_scratch, compiler_params=_params(semantics),
    )(q, k, k, vsrc, vsrc, pos_col, pos_row, pos_row, slopes, sinks, do, *p_ins)
    return res[0], res[1], res[2], res[3], list(res[4:])


MLA_T = 512


def _hosted(plan, n_in, n_out, grid, semantics):
    if plan is None:
        return [], [], [], [], [], semantics, (lambda refs: None), (lambda refs: None)
    ni, no = len(plan.ins), len(plan.out_shapes)

    def split(refs):
        return refs[n_in:n_in + ni], refs[n_in + ni + n_out:n_in + ni + n_out + no], refs[-2], refs[-1]

    def at_start(refs):
        @pl.when(functools.reduce(jnp.logical_and, [pl.program_id(ax) == 0 for ax in range(len(grid))]))
        def _():
            plan.start(*split(refs))

    def at_end(refs):
        @pl.when(functools.reduce(jnp.logical_and, [pl.program_id(ax) == g - 1 for ax, g in enumerate(grid)]))
        def _():
            plan.finish(*split(refs))

    return plan.ins, [_HBM] * ni, plan.out_shapes, [_HBM] * no, plan.scratch(), ("arbitrary",) * len(grid), at_start, at_end


def mla_fwd(name, q, k, v, plan=None):
    b, s, w = q.shape
    h = w // HEAD_PAD
    t = MLA_T if s % MLA_T == 0 else BLOCK
    nb = s // t
    scale = QK_DIM_B ** -0.5
    p_ins, p_in_specs, p_outs, p_out_specs, p_scratch, semantics, at_start, at_end = _hosted(
        plan, 3, 3, (b, h), ("parallel", "parallel"))

    def body(*refs):
        q_ref, k_ref, v_ref = refs[:3]
        o_ref, ob_ref, lse_ref = refs[3 + len(p_ins):6 + len(p_ins)]
        at_start(refs)
        causal = lax.broadcasted_iota(jnp.int32, (t, t), 0) <= lax.broadcasted_iota(jnp.int32, (t, t), 1)
        for i in range(nb):
            q_i = q_ref[i * t:(i + 1) * t, :]

            def step(j, carry, q_i=q_i, diagonal=False):
                m, l, acc = carry
                rows = slice(j * t, (j + 1) * t) if diagonal else pl.ds(pl.multiple_of(j * t, t), t)
                st = lax.dot_general(k_ref[rows, :], q_i, _NT, preferred_element_type=F32) * scale
                if diagonal:
                    st = jnp.where(causal, st, NEG)
                m_new = jnp.maximum(m, jnp.max(st, axis=0, keepdims=True))
                a = jnp.exp(m - m_new)
                p = jnp.exp(st - m_new)
                l = a * l + jnp.sum(p, axis=0, keepdims=True)
                acc = a * acc + lax.dot_general(v_ref[rows, :], p.astype(BF16), _TN, preferred_element_type=F32)
                return m_new, l, acc

            carry = (jnp.full((1, t), NEG, F32), jnp.zeros((1, t), F32), jnp.zeros((HEAD_PAD, t), F32))
            if i > 0:
                carry = lax.fori_loop(0, i, step, carry)
            m, l, acc = step(i, carry, diagonal=True)
            o_i = (acc / l).T
            o_ref[i * t:(i + 1) * t, :] = o_i
            ob_ref[i * t:(i + 1) * t, :] = o_i.astype(BF16)
            lse_ref[i] = m + jnp.log(l)
        at_end(refs)

    hs = _head_spec(s, 0)
    ls = pl.BlockSpec((None, None, nb, 1, t), lambda bi, hi: (bi, hi, 0, 0, 0))
    res = pl.pallas_call(
        body, name=name, grid=(b, h),
        out_shape=(jax.ShapeDtypeStruct((b, s, w), F32), jax.ShapeDtypeStruct((b, s, w), BF16),
                   jax.ShapeDtypeStruct((b, h, nb, 1, t), F32)) + tuple(p_outs),
        in_specs=[hs, hs, hs] + p_in_specs, out_specs=(hs, hs, ls) + tuple(p_out_specs),
        scratch_shapes=p_scratch, compiler_params=_params(semantics),
    )(q, k, v, *p_ins)
    return res[0], res[1], res[2], list(res[3:])


def mla_bwd(name, q, k, v, o, do, lse, plan=None):
    b, s, w = q.shape
    h = w // HEAD_PAD
    _, _, nb, _, t = lse.shape
    scale = QK_DIM_B ** -0.5
    p_ins, p_in_specs, p_outs, p_out_specs, p_scratch, semantics, at_start, at_end = _hosted(
        plan, 6, 3, (b, h), ("parallel", "parallel"))

    def body(*refs):
        q_ref, k_ref, v_ref, o_ref, do_ref, lse_ref = refs[:6]
        dq_ref, dk_ref, dv_ref = refs[6 + len(p_ins):9 + len(p_ins)]
        dv_acc = refs[9 + len(p_ins) + len(p_outs)]
        at_start(refs)
        causal = lax.broadcasted_iota(jnp.int32, (t, t), 0) <= lax.broadcasted_iota(jnp.int32, (t, t), 1)
        dk_ref[...] = jnp.zeros_like(dk_ref)
        dv_acc[...] = jnp.zeros_like(dv_acc)
        for i in range(nb):
            q_i = q_ref[i * t:(i + 1) * t, :]
            do_i = do_ref[i * t:(i + 1) * t, :]
            delta = jnp.sum((o_ref[i * t:(i + 1) * t, :] * do_i).T, axis=0, keepdims=True)
            do_b = do_i.astype(BF16)
            lse_i = lse_ref[i]

            def step(j, dqt, q_i=q_i, do_b=do_b, delta=delta, lse_i=lse_i, diagonal=False):
                rows = slice(j * t, (j + 1) * t) if diagonal else pl.ds(pl.multiple_of(j * t, t), t)
                k_j = k_ref[rows, :]
                st = lax.dot_general(k_j, q_i, _NT, preferred_element_type=F32) * scale
                if diagonal:
                    st = jnp.where(causal, st, NEG)
                pt = jnp.exp(st - lse_i)
                dpt = lax.dot_general(v_ref[rows, :], do_b, _NT, preferred_element_type=F32)
                dst = (pt * (dpt - delta) * scale).astype(BF16)
                dv_acc[rows, :] += jnp.dot(pt.astype(BF16), do_b, preferred_element_type=F32)
                dk_ref[rows, :] += jnp.dot(dst, q_i, preferred_element_type=F32)
                return dqt + lax.dot_general(k_j, dst, _TN, preferred_element_type=F32)

            dqt = jnp.zeros((HEAD_PAD, t), F32)
            if i > 0:
                dqt = lax.fori_loop(0, i, step, dqt)
            dq_ref[i * t:(i + 1) * t, :] = step(i, dqt, diagonal=True).T
        dv_ref[...] = dv_acc[...].astype(BF16)
        at_end(refs)

    hs = _head_spec(s, 0)
    ls = pl.BlockSpec((None, None, nb, 1, t), lambda bi, hi: (bi, hi, 0, 0, 0))
    res = pl.pallas_call(
        body, name=name, grid=(b, h),
        out_shape=(jax.ShapeDtypeStruct((b, s, w), F32), jax.ShapeDtypeStruct((b, s, w), F32),
                   jax.ShapeDtypeStruct((b, s, w), BF16)) + tuple(p_outs),
        in_specs=[hs, hs, hs, hs, hs, ls] + p_in_specs, out_specs=(hs, hs, hs) + tuple(p_out_specs),
        scratch_shapes=[pltpu.VMEM((s, HEAD_PAD), F32)] + p_scratch,
        compiler_params=_params(semantics),
    )(q, k, v, o, do, lse, *p_ins)
    return res[0], res[1], res[2], list(res[3:])


_HBM = pl.BlockSpec(memory_space=pltpu.HBM)


def _place():
    x, y, c = lax.axis_index("x"), lax.axis_index("y"), lax.axis_index("c")
    chips = [(1 - x, y), (x, 1 - y), (1 - x, 1 - y)]
    return x, y, c, chips


def _remote(src, dst, send_sem, recv_sem, dev):
    return pltpu.make_async_remote_copy(src_ref=src, dst_ref=dst, send_sem=send_sem, recv_sem=recv_sem,
                                        device_id=dev, device_id_type=MESH)


class CommPlan:
    def __init__(self, ins, out_shapes, n_sems, start, finish, aliases=None):
        self.ins, self.out_shapes, self.n_sems = list(ins), list(out_shapes), n_sems
        self.start, self.finish, self.aliases = start, finish, dict(aliases or {})

    def scratch(self):
        return [pltpu.SemaphoreType.DMA((self.n_sems,)), pltpu.SemaphoreType.DMA((self.n_sems,))]


def comm_call(name, plan):
    ni, no = len(plan.ins), len(plan.out_shapes)

    def body(*refs):
        ins, outs, (send_sems, recv_sems) = refs[:ni], refs[ni:ni + no], refs[ni + no:]
        plan.start(ins, outs, send_sems, recv_sems)
        plan.finish(ins, outs, send_sems, recv_sems)

    return pl.pallas_call(
        body, name=name, out_shape=tuple(plan.out_shapes), in_specs=[_HBM] * ni, out_specs=tuple([_HBM] * no),
        input_output_aliases=plan.aliases, scratch_shapes=plan.scratch(),
    )(*plan.ins)


def gather_plan(packs, l):
    nk = len(packs)

    def parts(p_refs, o_refs, ss, rs):
        x, y, c, chips = _place()
        s_me = 2 * x + y
        sibling = (x, y, 1 - c)
        first = [_remote(p_refs[k].at[l], o_refs[k].at[s_me], ss.at[j * nk + k], rs.at[j * nk + k], (cx, cy, c))
                 for j, (cx, cy) in enumerate(chips) for k in range(nk)]
        own = [_remote(p_refs[k].at[l], o_refs[k].at[s_me], ss.at[6 * nk + k], rs.at[6 * nk + k], sibling) for k in range(nk)]
        return c, chips, sibling, first, own

    def start(p_refs, o_refs, ss, rs):
        c, _, _, first, own = parts(p_refs, o_refs, ss, rs)
        for cp in own:
            cp.start()

        @pl.when(c == l)
        def _():
            for cp in first:
                cp.start()

    def finish(p_refs, o_refs, ss, rs):
        c, chips, sibling, first, own = parts(p_refs, o_refs, ss, rs)

        @pl.when(c == l)
        def _():
            passed = []
            for j, (cx, cy) in enumerate(chips):
                for k in range(nk):
                    blk = o_refs[k].at[2 * cx + cy]
                    _remote(blk, blk, ss.at[j * nk + k], rs.at[j * nk + k], (cx, cy, c)).wait_recv()
                    fwd = _remote(blk, blk, ss.at[(3 + j) * nk + k], rs.at[(3 + j) * nk + k], sibling)
                    fwd.start()
                    passed.append(fwd)
            for cp in first + passed:
                cp.wait_send()

        @pl.when(c != l)
        def _():
            for j, (cx, cy) in enumerate(chips):
                for k in range(nk):
                    blk = o_refs[k].at[2 * cx + cy]
                    _remote(blk, blk, ss.at[(3 + j) * nk + k], rs.at[(3 + j) * nk + k], sibling).wait_recv()

        for cp in own:
            cp.wait()

    outs = [jax.ShapeDtypeStruct((N_SHARDS,) + p.shape[1:], p.dtype) for p in packs]
    return CommPlan(packs, outs, 7 * nk, start, finish)


def swap_plan(grads, l):
    nk = len(grads)

    def copies(g_refs, o_refs, ss, rs):
        x, y, c, _ = _place()
        return c, [_remote(g_refs[k], o_refs[k], ss.at[k], rs.at[k], (x, y, 1 - c)) for k in range(nk)]

    def start(g_refs, o_refs, ss, rs):
        c, cps = copies(g_refs, o_refs, ss, rs)

        @pl.when(c != l)
        def _():
            for cp in cps:
                cp.start()

    def finish(g_refs, o_refs, ss, rs):
        c, cps = copies(g_refs, o_refs, ss, rs)

        @pl.when(c != l)
        def _():
            for cp in cps:
                cp.wait_send()

        @pl.when(c == l)
        def _():
            for cp in cps:
                cp.wait_recv()

    return CommPlan(grads, [jax.ShapeDtypeStruct(g.shape, g.dtype) for g in grads], nk, start, finish)


def scatter_plan(parts, l):
    nk = len(parts)

    def copies(p_refs, o_refs, ss, rs):
        x, y, c, chips = _place()
        s_me = 2 * x + y
        sends = [_remote(p_refs[k].at[2 * cx + cy], o_refs[k].at[s_me], ss.at[j * nk + k], rs.at[j * nk + k], (cx, cy, c))
                 for j, (cx, cy) in enumerate(chips) for k in range(nk)]
        return c, chips, sends

    def start(p_refs, o_refs, ss, rs):
        c, _, sends = copies(p_refs, o_refs, ss, rs)

        @pl.when(c == l)
        def _():
            for cp in sends:
                cp.start()

    def finish(p_refs, o_refs, ss, rs):
        c, chips, sends = copies(p_refs, o_refs, ss, rs)

        @pl.when(c == l)
        def _():
            for j, (cx, cy) in enumerate(chips):
                for k in range(nk):
                    slot = o_refs[k].at[2 * cx + cy]
                    _remote(slot, slot, ss.at[j * nk + k], rs.at[j * nk + k], (cx, cy, c)).wait_recv()
            for cp in sends:
                cp.wait_send()

    return CommPlan(parts, [jax.ShapeDtypeStruct(p.shape, p.dtype) for p in parts], 3 * nk, start, finish)


def share_plan(finals):
    flat = [(l, f) for l, fs in enumerate(finals) for f in fs]

    def copies(f_refs, o_refs, ss, rs):
        x, y, c, _ = _place()
        return c, [_remote(f_refs[i], o_refs[i], ss.at[i], rs.at[i], (x, y, 1 - c)) for i in range(len(flat))]

    def start(f_refs, o_refs, ss, rs):
        c, cps = copies(f_refs, o_refs, ss, rs)
        for l in range(len(finals)):
            @pl.when(c == l)
            def _(l=l):
                for cp, (lf, _) in zip(cps, flat):
                    if lf == l:
                        cp.start()

    def finish(f_refs, o_refs, ss, rs):
        c, cps = copies(f_refs, o_refs, ss, rs)
        for l in range(len(finals)):
            @pl.when(c == l)
            def _(l=l):
                for cp, (lf, _) in zip(cps, flat):
                    if lf == l:
                        cp.wait_send()

            @pl.when(c != l)
            def _(l=l):
                for cp, (lf, _) in zip(cps, flat):
                    if lf == l:
                        cp.wait_recv()

    arrays = [f for _, f in flat]
    return CommPlan(arrays, [jax.ShapeDtypeStruct(f.shape, f.dtype) for f in arrays], len(flat), start, finish,
                    aliases={i: i for i in range(len(flat))})


def add_pair(name, own, other, tr=1024):
    n, cols = own.shape
    tr = _tile(n, tr, 16)

    def body(g_ref, o_ref, out_ref):
        out_ref[...] = (g_ref[...].astype(F32) + o_ref[...].astype(F32)).astype(out_ref.dtype)

    row = pl.BlockSpec((tr, cols), lambda i: (i, 0))
    return pl.pallas_call(
        body, name=name, out_shape=jax.ShapeDtypeStruct((n, cols), BF16), grid=(n // tr,), in_specs=[row, row], out_specs=row,
        compiler_params=_params(("parallel",)),
    )(own, other)


def sum_slots(name, slots, part, shard, tr=512):
    ns, r, cols = slots.shape
    tr = _tile(r, tr, 16)

    def body(s_idx, s_ref, own_ref, o_ref):
        total = own_ref[...].astype(F32)
        for k in range(1, ns):
            total = total + s_ref[(s_idx[0] + k) % ns].astype(F32)
        o_ref[...] = total

    return pl.pallas_call(
        body, name=name, out_shape=jax.ShapeDtypeStruct((r, cols), F32),
        grid_spec=pltpu.PrefetchScalarGridSpec(
            num_scalar_prefetch=1, grid=(r // tr,),
            in_specs=[pl.BlockSpec((ns, tr, cols), lambda i, p: (0, i, 0)), pl.BlockSpec((None, tr, cols), lambda i, p: (p[0], i, 0))],
            out_specs=pl.BlockSpec((tr, cols), lambda i, p: (i, 0))),
        compiler_params=_params(("parallel",)),
    )(shard, slots, part)


def small_allreduce(vec):
    r, cols = vec.shape

    def body(v_ref, o_ref, buf, send_sems, recv_sems):
        x, y, c, _ = _place()
        me = 4 * x + 2 * y + c
        buf[me] = v_ref[...]
        peers = []
        for k in range(1, N_DEV):
            px = (1 - x) if (k & 4) else x
            py = (1 - y) if (k & 2) else y
            pc = (1 - c) if (k & 1) else c
            peers.append((px, py, pc))
        sends = [_remote(buf.at[me], buf.at[me], send_sems.at[k], recv_sems.at[k], peer) for k, peer in enumerate(peers)]
        for cp in sends:
            cp.start()
        for k, (px, py, pc) in enumerate(peers):
            slot = buf.at[4 * px + 2 * py + pc]
            _remote(slot, slot, send_sems.at[k], recv_sems.at[k], (px, py, pc)).wait_recv()
        for cp in sends:
            cp.wait_send()
        total = buf[0]
        for k in range(1, N_DEV):
            total = total + buf[k]
        o_ref[...] = total

    return pl.pallas_call(
        body, name="small_allreduce", out_shape=jax.ShapeDtypeStruct((r, cols), F32),
        in_specs=[pl.BlockSpec(memory_space=pltpu.VMEM)], out_specs=pl.BlockSpec(memory_space=pltpu.VMEM),
        scratch_shapes=[pltpu.VMEM((N_DEV, r, cols), F32), pltpu.SemaphoreType.DMA((N_DEV - 1,)), pltpu.SemaphoreType.DMA((N_DEV - 1,))],
    )(vec)


ARG_NAMES = (("x", "positions") + WEIGHT_NAMES + ("loss_target",) + tuple("m_" + n for n in WEIGHT_NAMES)
             + tuple("v_" + n for n in WEIGHT_NAMES))

_C_KA = N_HEADS_A * HEAD_PAD
_C_VA = _C_KA + N_KV_HEADS_A * HEAD_PAD
_C_CQ = _C_VA + N_KV_HEADS_A * HEAD_PAD
_C_CKV = _C_CQ + Q_LORA_RANK
_C_KR = _C_CKV + KV_LORA_RANK
_C_GA = _C_KR + HEAD_PAD
_R_KA = WIDTH_A
_R_VA = _R_KA + KV_WIDTH_A
_R_CQ = _R_VA + KV_WIDTH_A
_R_KR = _R_CQ + Q_LORA_RANK + KV_LORA_RANK
_R_GA = _R_KR + QK_ROPE_DIM


def _pad_head_cols(w, n_heads):
    r, dh = w.shape[0], w.shape[1] // n_heads
    return jnp.pad(w.reshape(r, n_heads, dh), ((0, 0), (0, 0), (0, HEAD_PAD - dh))).reshape(r, n_heads * HEAD_PAD)


def _unpad_head_cols(w, n_heads, dh):
    r = w.shape[0]
    return w.reshape(r, n_heads, HEAD_PAD)[:, :, :dh].reshape(r, n_heads * dh)


def _pad_head_rows(w, n_heads):
    dh, c = w.shape[0] // n_heads, w.shape[1]
    return jnp.pad(w.reshape(n_heads, dh, c), ((0, 0), (0, HEAD_PAD - dh), (0, 0))).reshape(n_heads * HEAD_PAD, c)


def _unpad_head_rows(w, n_heads, dh):
    c = w.shape[1]
    return w.reshape(n_heads, HEAD_PAD, c)[:, :dh].reshape(n_heads * dh, c)


def _join_cols(shards):
    ns, r, c = shards.shape
    return shards.transpose(1, 0, 2).reshape(r, ns * c)


def _split_cols(mat):
    r, c4 = mat.shape
    return mat.reshape(r, N_SHARDS, c4 // N_SHARDS).transpose(1, 0, 2)


def _ffn_fwd(tag, x, gain, wcol, wdown, plan=None):
    nb = rms_fwd(tag + "_rms", x, gain)
    gu, h, carried = ffn_up_act(tag + "_up", nb, wcol, 0, plan=plan)
    out = gmm_down(tag + "_down", h, wdown, 0, x, 0.5)
    return out, (x, nb, gu, h), carried


def _ffn_bwd(tag, saved, gain, wcol, wdown, dout, doutb, plans=None):
    plans = plans or {}
    x, nb, gu, h = saved
    carried = {}
    dgu, carried["dh"] = ffn_down_dact(tag + "_dh", doutb, wdown, gu, gu.shape[-1], 0, 0.5, plan=plans.get("dh"))
    gdown = gmm_down_dw(tag + "_dwd", h, doutb, 0.5)
    if plans.get("dwgu") is not None:
        gcol, carried["dwgu"] = gmm_up_dw(tag + "_dwgu", nb, dgu, plan=plans["dwgu"])
    else:
        gcol = gmm_up_dw(tag + "_dwgu", nb, dgu)
    if plans.get("dn") is not None:
        plan_dn = plans["dn"](gcol, gdown) if callable(plans["dn"]) else plans["dn"]
        dn, carried["dn"] = gmm_up_dx(tag + "_dn", dgu, wcol, x.shape[1], 0, plan=plan_dn)
    else:
        dn = gmm_up_dx(tag + "_dn", dgu, wcol, x.shape[1], 0)
    dx, dxb, dgain = rms_bwd(tag + "_drms", x, gain, dn, dres=dout)
    return dx, dxb, dgain, gcol, gdown, carried


def _pad_gain(g):
    return jnp.pad(g, ((0, 0), (0, HEAD_PAD - g.shape[1])))


def _mixer_fwd(tag, x1, sm, w, wo, aux, plans=None):
    plans = plans or {}
    b, s = aux["b"], aux["s"]
    t, d = x1.shape
    hb = rms_fwd(tag + "_rms", x1, sm["mix_norm"])
    carried = {}
    if plans.get("proj") is not None:
        proj, carried["proj"] = mm_nn(tag + "_proj", hb, w["win"], tm=1024, tn=2048, tk=1024, plan=plans["proj"])
    else:
        proj = mm_nn(tag + "_proj", hb, w["win"], tm=1024, tn=2048, tk=1024)
    proj3 = proj.reshape(b, s, proj.shape[1])
    gains = {n: _pad_gain(sm[n]) for n in ("swa_q_norm", "swa_k_norm", "mla_q_norm", "mla_k_norm")}
    cqn = rms_fwd(tag + "_rms_cq", proj, sm["mla_q_lora_norm"], tm=1024, col_blk=_C_CQ // Q_LORA_RANK)
    ckvn = rms_fwd(tag + "_rms_ckv", proj, sm["mla_kv_lora_norm"], tm=1024, col_blk=_C_CKV // KV_LORA_RANK)
    qb_raw = mm_nn(tag + "_uq", cqn, w["wuq"], tm=1024).reshape(b, s, N_HEADS_B * HEAD_PAD)
    kb_raw = mm_nn(tag + "_uk", ckvn, w["wk"], tm=1024).reshape(b, s, N_HEADS_B * HEAD_PAD)
    vb = mm_nn(tag + "_uv", ckvn, w["wv"], tm=1024, out_dtype=BF16).reshape(b, s, N_HEADS_B * HEAD_PAD)
    qah = qk_prep_fwd(tag + "_qa_norm", proj3, 0, N_HEADS_A, HEAD_DIM_A, gains["swa_q_norm"])
    kah = qk_prep_fwd(tag + "_ka_norm", proj3, _C_KA // HEAD_PAD, N_KV_HEADS_A, HEAD_DIM_A, gains["swa_k_norm"])
    sinks = sm["swa_sinks"].reshape(-1)
    oab, carried["swa"] = swa_fwd(tag + "_swa", qah, kah, proj3, _C_VA // HEAD_PAD, aux["pos_col"], aux["pos_row"], aux["slopes"],
                                  sinks, plan=plans.get("swa"))
    kr = (proj3, _C_KR // HEAD_PAD)
    qbh = qk_prep_fwd(tag + "_qb_norm", qb_raw, 0, N_HEADS_B, QK_DIM_B, gains["mla_q_norm"], aux["rope"])
    kbh = qk_prep_fwd(tag + "_kb_norm", kb_raw, 0, N_HEADS_B, QK_DIM_B, gains["mla_k_norm"], aux["rope"], extra=kr)
    ob, obb, lse, carried["mla"] = mla_fwd(tag + "_mla", qbh, kbh, vb, plans.get("mla"))
    oab2, obb2 = oab.reshape(t, -1), obb.reshape(t, -1)
    ya = mm_nn(tag + "_branch_a", oab2, w["wa"], tm=1024)
    yb = mm_nn(tag + "_branch_b", obb2, w["wb"], tm=1024)
    mg = gate_fwd(tag + "_gate", proj, ya, yb, _C_GA, _C_GA + d)
    x2 = gmm_rows(tag + "_out", mg, wo, x1)
    saved = dict(x1=x1, hb=hb, proj=proj, cqn=cqn, ckvn=ckvn, qb_raw=qb_raw, kb_raw=kb_raw, vb=vb, qah=qah, kah=kah,
                 qbh=qbh, kbh=kbh, ob=ob, lse=lse, oab2=oab2, obb2=obb2, ya=ya, yb=yb, mg=mg, gains=gains)
    return x2, saved, carried


def _mixer_bwd(tag, sv, sm, w, wo, aux, dx2, dx2b, plans=None):
    plans = plans or {}
    carried = {}
    b, s = aux["b"], aux["s"]
    t, d = dx2.shape
    proj, gains = sv["proj"], sv["gains"]
    proj3 = proj.reshape(b, s, proj.shape[1])
    kr = (proj3, _C_KR // HEAD_PAD)
    if plans.get("d_out") is not None:
        dmg, carried["d_out"] = gmm_rows_dx(tag + "_d_out", dx2b, wo, plan=plans["d_out"])
    else:
        dmg = gmm_rows_dx(tag + "_d_out", dx2b, wo)
    dwo = gmm_rows_dw(tag + "_dw_out", sv["mg"], dx2b)
    dya, dyb, dga, dgb = gate_bwd(tag + "_dgate", proj, sv["ya"], sv["yb"], dmg, _C_GA, _C_GA + d)
    dwa = mm_tn(tag + "_dw_branch_a", sv["oab2"], dya, out_dtype=BF16)
    dwb = mm_tn(tag + "_dw_branch_b", sv["obb2"], dyb, out_dtype=BF16)
    doa = mm_nt(tag + "_d_branch_a", dya, w["wa"], tm=1024).reshape(b, s, -1)
    dob = mm_nt(tag + "_d_branch_b", dyb, w["wb"], tm=1024).reshape(b, s, -1)
    sinks = sm["swa_sinks"].reshape(-1)
    dqah, dkah, dva, dsinks, carried["dswa"] = swa_bwd(tag + "_dswa", sv["qah"], sv["kah"], proj3, _C_VA // HEAD_PAD, aux["pos_col"],
                                                       aux["pos_row"], aux["slopes"], sinks, doa, plan=plans.get("dswa"))
    dqa_raw, dg_swa_q = qk_prep_bwd(tag + "_dqa_norm", proj3, 0, N_HEADS_A, HEAD_DIM_A, gains["swa_q_norm"], dqah)
    dka_raw, dg_swa_k = qk_prep_bwd(tag + "_dka_norm", proj3, _C_KA // HEAD_PAD, N_KV_HEADS_A, HEAD_DIM_A, gains["swa_k_norm"], dkah)
    plan_dmla = plans["dmla"](carried) if callable(plans.get("dmla")) else plans.get("dmla")
    dqbh, dkbh, dvb, carried["dmla"] = mla_bwd(tag + "_dmla", sv["qbh"], sv["kbh"], sv["vb"], sv["ob"], dob, sv["lse"], plan_dmla)
    dqb_raw, dg_mla_q = qk_prep_bwd(tag + "_dqb_norm", sv["qb_raw"], 0, N_HEADS_B, QK_DIM_B, gains["mla_q_norm"], dqbh, aux["rope"])
    dkb_raw, dkr_sum, dg_mla_k = qk_prep_bwd(tag + "_dkb_norm", sv["kb_raw"], 0, N_HEADS_B, QK_DIM_B, gains["mla_k_norm"], dkbh,
                                             aux["rope"], extra=kr, head_sum=True)
    dq_tok, dk_tok, dv_tok = dqb_raw.reshape(t, -1), dkb_raw.reshape(t, -1), dvb.reshape(t, -1)
    dwuq = mm_tn(tag + "_dw_uq", sv["cqn"], dq_tok, tk=1024, out_dtype=BF16)
    dwk = mm_tn(tag + "_dw_uk", sv["ckvn"], dk_tok, tk=1024, out_dtype=BF16)
    dwv = mm_tn(tag + "_dw_uv", sv["ckvn"], dv_tok, tk=1024, out_dtype=BF16)
    dcqn = mm_nt(tag + "_d_uq", dq_tok, w["wuq"], tm=1024)
    dckvn = mm_nt(tag + "_d_uv", dv_tok, w["wv"], tm=1024, res=mm_nt(tag + "_d_uk", dk_tok, w["wk"], tm=1024))
    dcq, dg_q_lora = rms_bwd(tag + "_drms_cq", proj, sm["mla_q_lora_norm"], dcqn, want_f32=False, tm=1024, col_blk=_C_CQ // Q_LORA_RANK)
    dckv, dg_kv_lora = rms_bwd(tag + "_drms_ckv", proj, sm["mla_kv_lora_norm"], dckvn, want_f32=False, tm=1024,
                               col_blk=_C_CKV // KV_LORA_RANK)
    dproj = jnp.concatenate([dqa_raw.reshape(t, -1), dka_raw.reshape(t, -1), dva.reshape(t, -1).astype(BF16), dcq, dckv,
                             dkr_sum.reshape(t, HEAD_PAD).astype(BF16), dga, dgb], axis=1)
    dwin = mm_tn(tag + "_dw_in", sv["hb"], dproj, tm=1024, tn=2048, tk=1024, out_dtype=BF16)
    dh = mm_nt(tag + "_d_in", dproj, w["win"], tm=1024, tn=1024, tk=2048)
    dx1, dx1b, dg_mix = rms_bwd(tag + "_drms", sv["x1"], sm["mix_norm"], dh, dres=dx2)
    wgrads = dict(win=dwin, wuq=dwuq, wk=dwk, wv=dwv, wa=dwa, wb=dwb, wo=dwo)
    sgrads = dict(mix_norm=dg_mix, swa_q_norm=dg_swa_q[:, :HEAD_DIM_A], swa_k_norm=dg_swa_k[:, :HEAD_DIM_A],
                  swa_sinks=dsinks[:, 0].reshape(1, -1), mla_q_lora_norm=dg_q_lora, mla_kv_lora_norm=dg_kv_lora,
                  mla_q_norm=dg_mla_q[:, :QK_DIM_B], mla_k_norm=dg_mla_k[:, :QK_DIM_B])
    return dx1, dx1b, wgrads, sgrads, carried


def _layer_weights(win4, uq4, ukv4, branch4):
    wr = _join_cols(win4)
    kr = jnp.pad(wr[:, _R_KR:_R_GA], ((0, 0), (QK_NOPE_DIM, HEAD_PAD - QK_DIM_B)))
    win = jnp.concatenate([_pad_head_cols(wr[:, :_R_KA], N_HEADS_A), _pad_head_cols(wr[:, _R_KA:_R_VA], N_KV_HEADS_A),
                           _pad_head_cols(wr[:, _R_VA:_R_CQ], N_KV_HEADS_A), wr[:, _R_CQ:_R_KR], kr, wr[:, _R_GA:]], axis=1)
    ukv = _join_cols(ukv4)
    ukv3 = ukv.reshape(ukv.shape[0], N_HEADS_B, QK_NOPE_DIM + V_DIM_B)
    r = branch4.shape[1] // 2
    return dict(win=win, wuq=_pad_head_cols(_join_cols(uq4), N_HEADS_B),
                wk=_pad_head_cols(ukv3[:, :, :QK_NOPE_DIM].reshape(ukv.shape[0], -1), N_HEADS_B),
                wv=_pad_head_cols(ukv3[:, :, QK_NOPE_DIM:].reshape(ukv.shape[0], -1), N_HEADS_B),
                wa=_pad_head_rows(_join_cols(branch4[:, :r]), N_HEADS_A), wb=_pad_head_rows(_join_cols(branch4[:, r:]), N_HEADS_B))


def _mixer_grad_shards(g):
    dw = g["win"]
    win_ref = jnp.concatenate([_unpad_head_cols(dw[:, :_C_KA], N_HEADS_A, HEAD_DIM_A),
                               _unpad_head_cols(dw[:, _C_KA:_C_VA], N_KV_HEADS_A, HEAD_DIM_A),
                               _unpad_head_cols(dw[:, _C_VA:_C_CQ], N_KV_HEADS_A, HEAD_DIM_A), dw[:, _C_CQ:_C_KR],
                               dw[:, _C_KR + QK_NOPE_DIM:_C_KR + QK_DIM_B], dw[:, _C_GA:]], axis=1)
    rk = g["wk"].shape[0]
    ukv = jnp.concatenate([g["wk"].reshape(rk, N_HEADS_B, HEAD_PAD)[:, :, :QK_NOPE_DIM],
                           g["wv"].reshape(rk, N_HEADS_B, HEAD_PAD)[:, :, :V_DIM_B]], axis=2).reshape(rk, -1)
    return (_split_cols(win_ref), _split_cols(_unpad_head_cols(g["wuq"], N_HEADS_B, QK_DIM_B)), _split_cols(ukv),
            jnp.concatenate([_split_cols(_unpad_head_rows(g["wa"], N_HEADS_A, HEAD_DIM_A)),
                             _split_cols(_unpad_head_rows(g["wb"], N_HEADS_B, V_DIM_B))], axis=1))


BUFFER_TAGS = ("col1", "col2", "down1", "down2", "wo", "win", "uq", "ukv", "branch")
FIRST_GATHER = ("col1", "down1")
GATHER_CARRIERS = {(0, "ffn1_up"): (0, ("wo", "win", "uq", "ukv", "branch")), (0, "proj"): (1, ("col1",)),
                   (0, "swa"): (1, ("down1", "wo", "uq", "ukv")), (0, "mla"): (0, ("col2", "down2")),
                   (0, "ffn2_up"): (1, ("win", "branch")), (1, "ffn1_up"): (1, ("col2",)), (1, "swa"): (1, ("down2",))}
SCATTER_CARRIERS = {(0, "ffn2_dh"): (1, ("col1",)), (0, "ffn2_dwgu"): (1, ("col2",)), (0, "ffn2_dn"): (1, ("down1", "down2", "wo")),
                    (0, "dswa"): (1, ("win", "uq", "ukv", "branch")), (0, "dmla"): (0, ("col2", "down2")),
                    (0, "ffn1_dh"): (0, ("wo", "win", "uq", "ukv", "branch"))}
LAST_SCATTER = ("col1", "down1")


def _pick(by_tag, tags):
    return [by_tag[t] for t in tags]


def _train_step(x, positions, target, packs, small, shard):
    depth = 2
    b, s, d = x.shape
    t = b * s
    posf = positions.astype(F32)
    half = QK_ROPE_DIM // 2
    inv_freq = ROPE_BASE ** (-jnp.arange(half, dtype=F32) / half)
    ang = posf[..., None] * inv_freq
    cos, sin = jnp.cos(ang), jnp.sin(ang)
    tail = HEAD_PAD - QK_DIM_B
    rope = (jnp.concatenate([jnp.ones((b, s, QK_NOPE_DIM), F32), cos, cos, jnp.ones((b, s, tail), F32)], axis=-1),
            jnp.concatenate([jnp.zeros((b, s, QK_NOPE_DIM), F32), sin, sin, jnp.zeros((b, s, tail), F32)], axis=-1))
    slopes = jnp.exp2(-8.0 * (jnp.arange(N_HEADS_A, dtype=F32) + 1.0) / N_HEADS_A)
    aux = dict(b=b, s=s, pos_col=posf.reshape(b, s, 1), pos_row=posf.reshape(b, 1, s), rope=rope, slopes=slopes)

    def sm_of(l):
        return {n: small[n][l:l + 1] for n in SMALL_NAMES}

    by_tag = dict(zip(BUFFER_TAGS, packs))
    wts, parts, slots = [{} for _ in range(depth)], [{} for _ in range(depth)], [{} for _ in range(depth)]
    saved, sgrads = [None] * depth, [None] * depth

    def gather_piece(l, kern):
        if (l, kern) not in GATHER_CARRIERS:
            return None
        data_l, tags = GATHER_CARRIERS[(l, kern)]
        return gather_plan(_pick(by_tag, tags), data_l)

    def gathered(l, kern, arrays):
        if (l, kern) in GATHER_CARRIERS:
            data_l, tags = GATHER_CARRIERS[(l, kern)]
            wts[data_l].update(zip(tags, arrays))

    def scatter_piece(l, kern):
        if (l, kern) not in SCATTER_CARRIERS:
            return None
        data_l, tags = SCATTER_CARRIERS[(l, kern)]
        return scatter_plan(_pick(parts[data_l], tags), data_l)

    def scattered(l, kern, arrays):
        if (l, kern) in SCATTER_CARRIERS:
            data_l, tags = SCATTER_CARRIERS[(l, kern)]
            slots[data_l].update(zip(tags, arrays))

    def chip_sums(l, tags, mine, theirs=None):
        if theirs is None:
            theirs = comm_call(f"swap_l{l}_{tags[0]}", swap_plan(_pick(mine, tags), l))
        for tag, o in zip(tags, theirs):
            g = mine[tag]
            parts[l][tag] = add_pair(f"l{l}_add_{tag}", g.reshape(-1, g.shape[-1]), o.reshape(-1, o.shape[-1])).reshape(g.shape)

    wts[0].update(zip(FIRST_GATHER, comm_call("gather_first", gather_plan(_pick(by_tag, FIRST_GATHER), 0))))
    h = x.reshape(t, d)
    for l in range(depth):
        sm, wl = sm_of(l), wts[l]
        h, s1, got = _ffn_fwd(f"l{l}_ffn1", h, sm["ffn1_norm"], wl["col1"], wl["down1"], gather_piece(l, "ffn1_up"))
        gathered(l, "ffn1_up", got)
        weights = _layer_weights(wl["win"], wl["uq"], wl["ukv"], wl["branch"])
        h, s2, got = _mixer_fwd(f"l{l}_mix", h, sm, weights, wl["wo"], aux, {k: gather_piece(l, k) for k in ("proj", "swa", "mla")})
        for kern, arrays in got.items():
            gathered(l, kern, arrays)
        h, s3, got = _ffn_fwd(f"l{l}_ffn2", h, sm["ffn2_norm"], wl["col2"], wl["down2"], gather_piece(l, "ffn2_up"))
        gathered(l, "ffn2_up", got)
        saved[l] = (s1, s2, s3, weights)
    loss, dh, dhb = loss_fwd_bwd("loss", h, target.reshape(t, d))

    early_shared = None
    for l in reversed(range(depth)):
        sm, wl = sm_of(l), wts[l]
        s1, s2, s3, weights = saved[l]
        staged = l == 0
        dh, dhb, dg_f2, gcol2, gdown2, got = _ffn_bwd(f"l{l}_ffn2", s3, sm["ffn2_norm"], wl["col2"], wl["down2"], dh, dhb,
                                                      {k: scatter_piece(l, "ffn2_" + k) for k in ("dh", "dwgu", "dn")})
        for kern, arrays in got.items():
            scattered(l, "ffn2_" + kern, arrays)
        mine = dict(col2=gcol2, down2=gdown2)
        mplans = {"dswa": scatter_piece(l, "dswa")}
        if staged:
            mplans["d_out"] = swap_plan(_pick(mine, ("col2", "down2")), l)

            def after_swap(carried, l=l, mine=mine):
                chip_sums(l, ("col2", "down2"), mine, carried["d_out"])
                return scatter_piece(l, "dmla")

            mplans["dmla"] = after_swap
        dh, dhb, wg, sg, got = _mixer_bwd(f"l{l}_mix", s2, sm, weights, wl["wo"], aux, dh, dhb, mplans)
        for kern, arrays in got.items():
            scattered(l, kern, arrays)
        gwin, guq, gukv, gbranch = _mixer_grad_shards(wg)
        mine.update(wo=wg["wo"], win=gwin, uq=guq, ukv=gukv, branch=gbranch)
        if staged:
            chip_sums(l, ("wo", "win", "uq", "ukv", "branch"), mine)
        last_tags = LAST_SCATTER if staged else BUFFER_TAGS

        def last_swap(gcol1, gdown1, l=l, mine=mine, last_tags=last_tags):
            return swap_plan(_pick(dict(mine, col1=gcol1, down1=gdown1), last_tags), l)

        plans = {k: scatter_piece(l, "ffn1_" + k) for k in ("dh", "dwgu")}
        plans["dn"] = last_swap
        early = staged and l == 0 and depth == 2 and plans["dwgu"] is None
        if early:
            plans["dwgu"] = share_plan([[], [sum_slots(f"l1_sum_{tag}", slots[1][tag], parts[1][tag], shard) for tag in BUFFER_TAGS]])
        dh, dhb, dg_f1, gcol1, gdown1, got = _ffn_bwd(f"l{l}_ffn1", s1, sm["ffn1_norm"], wl["col1"], wl["down1"], dh, dhb, plans)
        if early:
            early_shared = got.pop("dwgu")
        theirs = got.pop("dn")
        for kern, arrays in got.items():
            scattered(l, "ffn1_" + kern, arrays)
        mine.update(col1=gcol1, down1=gdown1)
        chip_sums(l, last_tags, mine, theirs)
        sg.update(ffn1_norm=dg_f1, ffn2_norm=dg_f2)
        sgrads[l] = sg
    slots[0].update(zip(LAST_SCATTER, comm_call("scatter_last", scatter_plan(_pick(parts[0], LAST_SCATTER), 0))))
    late = range(1) if early_shared is not None else range(depth)
    finals = [[sum_slots(f"l{l}_sum_{tag}", slots[l][tag], parts[l][tag], shard) for tag in BUFFER_TAGS] for l in late]
    shared = list(comm_call("share", share_plan(finals))) + list(early_shared or [])
    nk = len(BUFFER_TAGS)
    return loss, dh.reshape(b, s, d), [dict(zip(BUFFER_TAGS, shared[l * nk:(l + 1) * nk])) for l in range(depth)], sgrads


def kernel(x, positions, ffn1_norm, ffn1_w_gate, ffn1_w_up, ffn1_w_down, mix_norm, w_in, swa_q_norm, swa_k_norm, swa_sinks, mla_q_lora_norm, mla_w_uq, mla_kv_lora_norm, mla_w_ukv, mla_q_norm, mla_k_norm, w_branch_a, w_branch_b, w_out, ffn2_norm, ffn2_w_gate, ffn2_w_up, ffn2_w_down, loss_target, m_ffn1_norm, m_ffn1_w_gate, m_ffn1_w_up, m_ffn1_w_down, m_mix_norm, m_w_in, m_swa_q_norm, m_swa_k_norm, m_swa_sinks, m_mla_q_lora_norm, m_mla_w_uq, m_mla_kv_lora_norm, m_mla_w_ukv, m_mla_q_norm, m_mla_k_norm, m_w_branch_a, m_w_branch_b, m_w_out, m_ffn2_norm, m_ffn2_w_gate, m_ffn2_w_up, m_ffn2_w_down, v_ffn1_norm, v_ffn1_w_gate, v_ffn1_w_up, v_ffn1_w_down, v_mix_norm, v_w_in, v_swa_q_norm, v_swa_k_norm, v_swa_sinks, v_mla_q_lora_norm, v_mla_w_uq, v_mla_kv_lora_norm, v_mla_w_ukv, v_mla_q_norm, v_mla_k_norm, v_w_branch_a, v_w_branch_b, v_w_out, v_ffn2_norm, v_ffn2_w_gate, v_ffn2_w_up, v_ffn2_w_down):
    args = (x, positions, ffn1_norm, ffn1_w_gate, ffn1_w_up, ffn1_w_down, mix_norm, w_in, swa_q_norm, swa_k_norm, swa_sinks, mla_q_lora_norm, mla_w_uq, mla_kv_lora_norm, mla_w_ukv, mla_q_norm, mla_k_norm, w_branch_a, w_branch_b, w_out, ffn2_norm, ffn2_w_gate, ffn2_w_up, ffn2_w_down, loss_target, m_ffn1_norm, m_ffn1_w_gate, m_ffn1_w_up, m_ffn1_w_down, m_mix_norm, m_w_in, m_swa_q_norm, m_swa_k_norm, m_swa_sinks, m_mla_q_lora_norm, m_mla_w_uq, m_mla_kv_lora_norm, m_mla_w_ukv, m_mla_q_norm, m_mla_k_norm, m_w_branch_a, m_w_branch_b, m_w_out, m_ffn2_norm, m_ffn2_w_gate, m_ffn2_w_up, m_ffn2_w_down, v_ffn1_norm, v_ffn1_w_gate, v_ffn1_w_up, v_ffn1_w_down, v_mix_norm, v_w_in, v_swa_q_norm, v_swa_k_norm, v_swa_sinks, v_mla_q_lora_norm, v_mla_w_uq, v_mla_kv_lora_norm, v_mla_w_ukv, v_mla_q_norm, v_mla_k_norm, v_w_branch_a, v_w_branch_b, v_w_out, v_ffn2_norm, v_ffn2_w_gate, v_ffn2_w_up, v_ffn2_w_down)
    a = dict(zip(ARG_NAMES, args, strict=True))
    x = a["x"]
    depth = a["ffn1_norm"].shape[0]
    d = x.shape[-1]
    assert depth == 2, "the exchanges of layer l are driven by core l of every chip: two layers, two cores"

    def bf16_rows(*names):
        return jnp.concatenate([a[n] for n in names], axis=1).astype(BF16) if len(names) > 1 else a[names[0]].astype(BF16)

    packs = [bf16_rows("ffn1_w_gate", "ffn1_w_up"), bf16_rows("ffn2_w_gate", "ffn2_w_up"), bf16_rows("ffn1_w_down"),
             bf16_rows("ffn2_w_down"), bf16_rows("w_out"), bf16_rows("w_in"), bf16_rows("mla_w_uq"), bf16_rows("mla_w_ukv"),
             bf16_rows("w_branch_a", "w_branch_b")]
    shard_me = (2 * lax.axis_index("x") + lax.axis_index("y")).astype(jnp.int32).reshape(1)
    small = {n: a[n] for n in SMALL_NAMES}

    loss, grad_x, summed, sgrads = _train_step(x, a["positions"], a["loss_target"], packs, small, shard_me)

    full = {tag: jnp.stack([summed[l][tag] for l in range(depth)]) for tag in BUFFER_TAGS}
    r_branch = a["w_branch_a"].shape[1]
    grads = dict(ffn1_w_gate=full["col1"][:, :d], ffn1_w_up=full["col1"][:, d:], ffn2_w_gate=full["col2"][:, :d],
                 ffn2_w_up=full["col2"][:, d:], ffn1_w_down=full["down1"], ffn2_w_down=full["down2"], w_out=full["wo"],
                 w_in=full["win"], mla_w_uq=full["uq"], mla_w_ukv=full["ukv"], w_branch_a=full["branch"][:, :r_branch],
                 w_branch_b=full["branch"][:, r_branch:])

    flat = jnp.concatenate([jnp.concatenate([sgrads[l][n].reshape(-1) for l in range(depth)]) for n in SMALL_NAMES] + [loss.reshape(-1)])
    n_small = flat.shape[0]
    rows = -(-n_small // (8 * LANES)) * 8
    pad = rows * LANES - n_small

    def small_pack(v):
        return jnp.pad(v, (0, pad)).reshape(rows, LANES)

    total = small_allreduce(small_pack(flat))
    w_s, m_s, v_s = (small_pack(jnp.concatenate([a[p + n].reshape(-1) for n in SMALL_NAMES] + [jnp.zeros((1,), F32)]))
                     for p in ("", "m_", "v_"))
    d_s, nm_s, nv_s = adamw("adamw_small", w_s, total, m_s, v_s)

    def small_unpack(buf):
        out, off, flat_b = {}, 0, buf.reshape(-1)
        for n in SMALL_NAMES:
            size = a[n].shape[0] * a[n].shape[1]
            out[n] = flat_b[off:off + size].reshape(a[n].shape)
            off += size
        return out

    grads.update(small_unpack(total))
    delta, new_m, new_v = small_unpack(d_s), small_unpack(nm_s), small_unpack(nv_s)
    for n in PACK_NAMES:
        flip = a[n].shape[2] % LANES != 0

        def view(z):
            return z.transpose(0, 2, 1) if flip else z

        w_v, g_v, m_v, v_v = view(a[n]), view(grads[n]), view(a["m_" + n]), view(a["v_" + n])
        shp = w_v.shape
        two_d = (shp[0] * shp[1], shp[2])
        dn, mn, vn = adamw("adamw_" + n, w_v.reshape(two_d), g_v.reshape(two_d), m_v.reshape(two_d), v_v.reshape(two_d))
        delta[n], new_m[n], new_v[n] = view(dn.reshape(shp)), view(mn.reshape(shp)), view(vn.reshape(shp))
        if flip:
            grads[n] = view(g_v)

    loss_out = total.reshape(-1)[n_small - 1]
    return (loss_out, grad_x, *[grads[n] for n in WEIGHT_NAMES], *[delta[n] for n in WEIGHT_NAMES],
            *[new_m[n] for n in WEIGHT_NAMES], *[new_v[n] for n in WEIGHT_NAMES])
```

```python
import functools

import numpy as np
import jax
import jax.numpy as jnp
from jax import lax
from jax.experimental import pallas as pl
from jax.experimental.pallas import tpu as pltpu

F32 = jnp.float32
BF16 = jnp.bfloat16
MESH = pl.DeviceIdType.MESH

HEAD_DIM_A = 64
N_HEADS_A = 8
N_KV_HEADS_A = 2
GROUP_A = N_HEADS_A // N_KV_HEADS_A
BLOCK = 128
N_HEADS_B = 8
Q_LORA_RANK = 256
KV_LORA_RANK = 128
QK_NOPE_DIM = 64
QK_ROPE_DIM = 32
QK_DIM_B = QK_NOPE_DIM + QK_ROPE_DIM
V_DIM_B = 64
ROPE_BASE = 10000.0
WIDTH_A = N_HEADS_A * HEAD_DIM_A
KV_WIDTH_A = N_KV_HEADS_A * HEAD_DIM_A
EPS = 1e-6
NEG = -1e30
ADAM_LR = 0.001
ADAM_B1 = 0.9
ADAM_B2 = 0.999
ADAM_EPS = 1e-08
ADAM_WD = 0.01
ADAM_STEP = 10

N_SHARDS = 4
N_DEV = 8
LANES = 128
VMEM_LIMIT = 48 * 1024 * 1024

PACK_NAMES = ("ffn1_w_gate", "ffn1_w_up", "ffn1_w_down", "w_in", "mla_w_uq", "mla_w_ukv",
              "w_branch_a", "w_branch_b", "w_out", "ffn2_w_gate", "ffn2_w_up", "ffn2_w_down")
SMALL_NAMES = ("ffn1_norm", "mix_norm", "swa_q_norm", "swa_k_norm", "swa_sinks", "mla_q_lora_norm",
               "mla_kv_lora_norm", "mla_q_norm", "mla_k_norm", "ffn2_norm")
WEIGHT_NAMES = ("ffn1_norm", "ffn1_w_gate", "ffn1_w_up", "ffn1_w_down", "mix_norm", "w_in", "swa_q_norm",
                "swa_k_norm", "swa_sinks", "mla_q_lora_norm", "mla_w_uq", "mla_kv_lora_norm", "mla_w_ukv",
                "mla_q_norm", "mla_k_norm", "w_branch_a", "w_branch_b", "w_out", "ffn2_norm", "ffn2_w_gate",
                "ffn2_w_up", "ffn2_w_down")


def _params(sem):
    return pltpu.CompilerParams(dimension_semantics=sem, vmem_limit_bytes=VMEM_LIMIT)


def _tile(n, want, align):
    if n <= want:
        return n
    t = (want // align) * align
    while t > align and n % t:
        t -= align
    assert t >= align and n % t == 0, (n, want, align)
    return t


def _mm_call(name, a, b, out_struct, grid, a_spec, b_spec, o_spec, dims, n_red, acc_shape, alpha=1.0, res=None,
             inner=0, a_cols=0, plan=None):
    if n_red and all(g == 1 for g in grid[len(grid) - n_red:]):
        n_red = 0
    n_par = len(grid) - n_red
    n_in = 2 + int(res is not None)
    p_ins, p_in_specs, p_outs, p_out_specs, p_scratch, semantics, at_start, at_end = _hosted(
        plan, n_in, 1, grid, ("parallel",) * n_par + ("arbitrary",) * n_red)

    def body(*refs):
        at_start(refs)
        compute(refs)
        at_end(refs)

    def compute(refs):
        a_ref, b_ref = refs[:2]
        r_ref = refs[2] if res is not None else None
        o_ref, acc_ref = refs[n_in + len(p_ins)], refs[n_in + len(p_ins) + 1 + len(p_outs)]
        if inner:
            def a_of(s):
                return a_ref[:, s * a_cols:(s + 1) * a_cols] if a_cols else a_ref[s]

            part = lax.dot_general(a_of(0), b_ref[0], dims, preferred_element_type=F32)
            for s in range(1, inner):
                part = part + lax.dot_general(a_of(s), b_ref[s], dims, preferred_element_type=F32)
        else:
            part = lax.dot_general(a_ref[...], b_ref[...], dims, preferred_element_type=F32)

        def finish(total):
            if alpha != 1.0:
                total = total * alpha
            if r_ref is not None:
                total = r_ref[...] + total
            o_ref[...] = total.astype(o_ref.dtype)

        if n_red == 0:
            finish(part)
            return
        ids = [pl.program_id(n_par + i) for i in range(n_red)]
        first = functools.reduce(jnp.logical_and, [i == 0 for i in ids])
        last = functools.reduce(jnp.logical_and, [i == grid[n_par + k] - 1 for k, i in enumerate(ids)])

        @pl.when(first)
        def _():
            acc_ref[...] = part

        @pl.when(jnp.logical_not(first))
        def _():
            acc_ref[...] += part

        @pl.when(last)
        def _():
            finish(acc_ref[...])

    in_specs = [a_spec, b_spec] + ([o_spec] if res is not None else [])
    args = (a, b) + ((res,) if res is not None else ())
    if plan is None:
        return pl.pallas_call(
            body, name=name, out_shape=out_struct, grid=grid, in_specs=in_specs, out_specs=o_spec,
            scratch_shapes=[pltpu.VMEM(acc_shape, F32)], compiler_params=_params(semantics),
        )(*args)
    res_all = pl.pallas_call(
        body, name=name, out_shape=(out_struct,) + tuple(p_outs), grid=grid, in_specs=in_specs + p_in_specs,
        out_specs=(o_spec,) + tuple(p_out_specs), scratch_shapes=[pltpu.VMEM(acc_shape, F32)] + p_scratch,
        input_output_aliases={n_in + i: 1 + o for i, o in plan.aliases.items()}, compiler_params=_params(semantics),
    )(*args, *p_ins)
    return res_all[0], list(res_all[1:])


_NN = (((1,), (0,)), ((), ()))
_NT = (((1,), (1,)), ((), ()))
_TN = (((0,), (0,)), ((), ()))


def mm_nn(name, a, b, tm=512, tn=1024, tk=1024, out_dtype=F32, alpha=1.0, res=None, plan=None):
    (m, k), (_, n) = a.shape, b.shape
    tm, tn, tk = _tile(m, tm, 16), _tile(n, tn, LANES), _tile(k, tk, LANES)
    return _mm_call(name, a, b, jax.ShapeDtypeStruct((m, n), out_dtype), (m // tm, n // tn, k // tk),
                    pl.BlockSpec((tm, tk), lambda i, j, kk: (i, kk)), pl.BlockSpec((tk, tn), lambda i, j, kk: (kk, j)),
                    pl.BlockSpec((tm, tn), lambda i, j, kk: (i, j)), _NN, 1, (tm, tn), alpha, res, plan=plan)


def mm_nt(name, a, b, tm=512, tn=1024, tk=1024, out_dtype=F32, alpha=1.0, res=None):
    (m, n), (k, _) = a.shape, b.shape
    tm, tn, tk = _tile(m, tm, 16), _tile(k, tn, LANES), _tile(n, tk, LANES)
    return _mm_call(name, a, b, jax.ShapeDtypeStruct((m, k), out_dtype), (m // tm, k // tn, n // tk),
                    pl.BlockSpec((tm, tk), lambda i, j, kk: (i, kk)), pl.BlockSpec((tn, tk), lambda i, j, kk: (j, kk)),
                    pl.BlockSpec((tm, tn), lambda i, j, kk: (i, j)), _NT, 1, (tm, tn), alpha, res)


def mm_tn(name, a, b, tm=1024, tn=1024, tk=1024, out_dtype=F32, alpha=1.0):
    (m, k), (_, n) = a.shape, b.shape
    tm, tn, tk = _tile(k, tm, LANES), _tile(n, tn, LANES), _tile(m, tk, 16)
    return _mm_call(name, a, b, jax.ShapeDtypeStruct((k, n), out_dtype), (k // tm, n // tn, m // tk),
                    pl.BlockSpec((tk, tm), lambda i, j, kk: (kk, i)), pl.BlockSpec((tk, tn), lambda i, j, kk: (kk, j)),
                    pl.BlockSpec((tm, tn), lambda i, j, kk: (i, j)), _TN, 1, (tm, tn), alpha)


def ffn_up_act(name, a, w, blk, tm=1024, plan=None):
    (m, k), (ns, _, n) = a.shape, w.shape
    tm = _tile(m, tm, 16)
    grid = (ns, m // tm)
    p_ins, p_in_specs, p_outs, p_out_specs, p_scratch, semantics, at_start, at_end = _hosted(plan, 3, 2, grid, ("parallel", "parallel"))

    def body(*refs):
        a_ref, wg_ref, wu_ref = refs[:3]
        gu_ref, h_ref = refs[3 + len(p_ins):5 + len(p_ins)]
        at_start(refs)
        av = a_ref[...]
        gate = jnp.dot(av, wg_ref[...], preferred_element_type=F32)
        up = jnp.dot(av, wu_ref[...], preferred_element_type=F32)
        gu_ref[0] = gate.astype(BF16)
        gu_ref[1] = up.astype(BF16)
        h_ref[...] = (gate * jax.nn.sigmoid(gate) * up).astype(BF16)
        at_end(refs)

    res = pl.pallas_call(
        body, name=name, grid=grid,
        out_shape=(jax.ShapeDtypeStruct((2, ns, m, n), BF16), jax.ShapeDtypeStruct((ns, m, n), BF16)) + tuple(p_outs),
        in_specs=[pl.BlockSpec((tm, k), lambda s, i: (i, 0)), pl.BlockSpec((None, k, n), lambda s, i: (s, blk, 0)),
                  pl.BlockSpec((None, k, n), lambda s, i: (s, blk + 1, 0))] + p_in_specs,
        out_specs=(pl.BlockSpec((2, None, tm, n), lambda s, i: (0, s, i, 0)), pl.BlockSpec((None, tm, n), lambda s, i: (s, i, 0)))
        + tuple(p_out_specs),
        scratch_shapes=p_scratch, compiler_params=_params(semantics),
    )(a, w, w, *p_ins)
    return res[0], res[1], list(res[2:])


def ffn_down_dact(name, a, w, gu, n, blk, alpha, tm=1024, plan=None):
    (m, d), ns = a.shape, w.shape[0]
    tm = _tile(m, tm, 16)
    grid = (ns, m // tm)
    p_ins, p_in_specs, p_outs, p_out_specs, p_scratch, semantics, at_start, at_end = _hosted(plan, 3, 1, grid, ("parallel", "parallel"))

    def body(*refs):
        a_ref, w_ref, gu_ref = refs[:3]
        o_ref = refs[3 + len(p_ins)]
        at_start(refs)
        dh = lax.dot_general(a_ref[...], w_ref[...], _NT, preferred_element_type=F32) * alpha
        gate, up = gu_ref[0].astype(F32), gu_ref[1].astype(F32)
        s = jax.nn.sigmoid(gate)
        o_ref[0] = (dh * up * (s * (1.0 + gate * (1.0 - s)))).astype(BF16)
        o_ref[1] = (dh * (gate * s)).astype(BF16)
        at_end(refs)

    gu_spec = pl.BlockSpec((2, None, tm, n), lambda s, i: (0, s, i, 0))
    res = pl.pallas_call(
        body, name=name, grid=grid, out_shape=(jax.ShapeDtypeStruct((2, ns, m, n), BF16),) + tuple(p_outs),
        in_specs=[pl.BlockSpec((tm, d), lambda s, i: (i, 0)), pl.BlockSpec((None, n, d), lambda s, i: (s, blk, 0)), gu_spec] + p_in_specs,
        out_specs=(gu_spec,) + tuple(p_out_specs), scratch_shapes=p_scratch, compiler_params=_params(semantics),
    )(a, w, gu, *p_ins)
    return res[0], list(res[1:])


def gmm_up_dw(name, a, dgu, tk=2048, plan=None):
    (m, k), (_, ns, _, n) = a.shape, dgu.shape
    tk = _tile(m, tk, 16)
    return _mm_call(name, a, dgu, jax.ShapeDtypeStruct((ns, 2 * k, n), BF16), (2, ns, m // tk),
                    pl.BlockSpec((tk, k), lambda j, s, kk: (kk, 0)), pl.BlockSpec((None, None, tk, n), lambda j, s, kk: (j, s, kk, 0)),
                    pl.BlockSpec((None, k, n), lambda j, s, kk: (s, j, 0)), _TN, 1, (k, n), plan=plan)


def gmm_up_dx(name, dgu, w, k, blk, tm=512, plan=None):
    _, ns, m, n = dgu.shape
    tm = _tile(m, tm, 16)
    return _mm_call(name, dgu, w, jax.ShapeDtypeStruct((m, k), F32), (m // tm, 2),
                    pl.BlockSpec((None, ns, tm, n), lambda i, j: (j, 0, i, 0)),
                    pl.BlockSpec((ns, k, n), lambda i, j: (0, blk + j, 0)),
                    pl.BlockSpec((tm, k), lambda i, j: (i, 0)), _NT, 1, (tm, k), inner=ns, plan=plan)


def gmm_down(name, h, w, blk, res, alpha, tm=512):
    (ns, m, n), d = h.shape, w.shape[2]
    tm = _tile(m, tm, 16)
    return _mm_call(name, h, w, jax.ShapeDtypeStruct((m, d), F32), (m // tm,),
                    pl.BlockSpec((ns, tm, n), lambda i: (0, i, 0)), pl.BlockSpec((ns, n, d), lambda i: (0, blk, 0)),
                    pl.BlockSpec((tm, d), lambda i: (i, 0)), _NN, 0, (8, LANES), alpha, res, inner=ns)


def gmm_down_dw(name, h, b, alpha, tk=2048):
    (ns, m, n), d = h.shape, b.shape[1]
    tk = _tile(m, tk, 16)
    return _mm_call(name, h, b, jax.ShapeDtypeStruct((ns, n, d), BF16), (ns, m // tk),
                    pl.BlockSpec((None, tk, n), lambda s, kk: (s, kk, 0)), pl.BlockSpec((tk, d), lambda s, kk: (kk, 0)),
                    pl.BlockSpec((None, n, d), lambda s, kk: (s, 0, 0)), _TN, 1, (n, d), alpha)


def gmm_rows(name, a, w, res, tm=1024):
    (m, _), (ns, r, d) = a.shape, w.shape
    tm = _tile(m, tm, 16)
    return _mm_call(name, a, w, jax.ShapeDtypeStruct((m, d), F32), (m // tm,),
                    pl.BlockSpec((tm, ns * r), lambda i: (i, 0)), pl.BlockSpec((ns, r, d), lambda i: (0, 0, 0)),
                    pl.BlockSpec((tm, d), lambda i: (i, 0)), _NN, 0, (8, LANES), 1.0, res, inner=ns, a_cols=r)


def gmm_rows_dx(name, a, w, tm=1024, plan=None):
    (m, d), (ns, r, _) = a.shape, w.shape
    tm = _tile(m, tm, 16)
    return _mm_call(name, a, w, jax.ShapeDtypeStruct((m, ns * r), F32), (ns, m // tm),
                    pl.BlockSpec((tm, d), lambda s, i: (i, 0)), pl.BlockSpec((None, r, d), lambda s, i: (s, 0, 0)),
                    pl.BlockSpec((tm, r), lambda s, i: (i, s)), _NT, 0, (8, LANES), plan=plan)


def gmm_rows_dw(name, a, b, tk=2048):
    (m, da), d = a.shape, b.shape[1]
    ns, r = N_SHARDS, da // N_SHARDS
    tk = _tile(m, tk, 16)
    return _mm_call(name, a, b, jax.ShapeDtypeStruct((ns, r, d), BF16), (ns, m // tk),
                    pl.BlockSpec((tk, r), lambda s, kk: (kk, s)), pl.BlockSpec((tk, d), lambda s, kk: (kk, 0)),
                    pl.BlockSpec((None, r, d), lambda s, kk: (s, 0, 0)), _TN, 1, (r, d))


def rms_fwd(name, x, gain, tm=1024, col_blk=0):
    m, d = x.shape[0], gain.shape[1]
    tm = _tile(m, tm, 16)

    def body(x_ref, g_ref, o_ref):
        xv = x_ref[...]
        r = lax.rsqrt(jnp.mean(xv * xv, axis=-1, keepdims=True) + EPS)
        o_ref[...] = (xv * r * g_ref[...]).astype(o_ref.dtype)

    return pl.pallas_call(
        body, name=name, out_shape=jax.ShapeDtypeStruct((m, d), BF16), grid=(m // tm,),
        in_specs=[pl.BlockSpec((tm, d), lambda i: (i, col_blk)), pl.BlockSpec((1, d), lambda i: (0, 0))],
        out_specs=pl.BlockSpec((tm, d), lambda i: (i, 0)), compiler_params=_params(("parallel",)),
    )(x, gain)


def rms_bwd(name, x, gain, dn, dres=None, want_f32=True, want_bf16=True, tm=512, col_blk=0):
    m, d = x.shape[0], gain.shape[1]
    tm = _tile(m, tm, 16)
    n_out = int(want_f32) + int(want_bf16)

    def body(*refs):
        x_ref, g_ref, dn_ref = refs[:3]
        pos = 3
        r_ref = None
        if dres is not None:
            r_ref = refs[pos]
            pos += 1
        outs = refs[pos:pos + n_out]
        dg_ref = refs[pos + n_out]
        xv = x_ref[...]
        r = lax.rsqrt(jnp.mean(xv * xv, axis=-1, keepdims=True) + EPS)
        xhat = xv * r
        dnv = dn_ref[...]
        dxhat = dnv * g_ref[...]
        dx = r * (dxhat - xhat * jnp.mean(dxhat * xhat, axis=-1, keepdims=True))
        if r_ref is not None:
            dx = r_ref[...] + dx
        for o in outs:
            o[...] = dx.astype(o.dtype)
        part = jnp.sum(dnv * xhat, axis=0, keepdims=True)

        @pl.when(pl.program_id(0) == 0)
        def _():
            dg_ref[...] = part

        @pl.when(pl.program_id(0) > 0)
        def _():
            dg_ref[...] += part

    row = pl.BlockSpec((tm, d), lambda i: (i, 0))
    vec = pl.BlockSpec((1, d), lambda i: (0, 0))
    out_shape = ([jax.ShapeDtypeStruct((m, d), F32)] if want_f32 else []) + ([jax.ShapeDtypeStruct((m, d), BF16)] if want_bf16 else [])
    res = pl.pallas_call(
        body, name=name, out_shape=tuple(out_shape) + (jax.ShapeDtypeStruct((1, d), F32),), grid=(m // tm,),
        in_specs=[pl.BlockSpec((tm, d), lambda i: (i, col_blk)), vec, row] + ([row] if dres is not None else []),
        out_specs=tuple([row] * n_out) + (vec,), compiler_params=_params(("arbitrary",)),
    )(*((x, gain, dn) + ((dres,) if dres is not None else ())))
    return res


def gate_fwd(name, proj, ya, yb, off_a, off_b, tm=256):
    m, d = ya.shape
    tm = _tile(m, tm, 16)
    blk_a, blk_b = off_a // d, off_b // d
    assert blk_a * d == off_a and blk_b * d == off_b, "the gates start on multiples of their width"

    def body(ga_ref, gb_ref, ya_ref, yb_ref, o_ref):
        o_ref[...] = (jax.nn.sigmoid(ga_ref[...]) * ya_ref[...] + jax.nn.sigmoid(gb_ref[...]) * yb_ref[...]).astype(o_ref.dtype)

    row = pl.BlockSpec((tm, d), lambda i: (i, 0))
    return pl.pallas_call(
        body, name=name, out_shape=jax.ShapeDtypeStruct((m, d), BF16), grid=(m // tm,),
        in_specs=[pl.BlockSpec((tm, d), lambda i: (i, blk_a)), pl.BlockSpec((tm, d), lambda i: (i, blk_b)), row, row], out_specs=row,
        compiler_params=_params(("parallel",)),
    )(proj, proj, ya, yb)


def gate_bwd(name, proj, ya, yb, dmerged, off_a, off_b, tm=256):
    m, d = ya.shape
    tm = _tile(m, tm, 16)
    blk_a, blk_b = off_a // d, off_b // d
    assert blk_a * d == off_a and blk_b * d == off_b, "the gates start on multiples of their width"

    def body(ga_ref, gb_ref, ya_ref, yb_ref, dm_ref, dya_ref, dyb_ref, dga_ref, dgb_ref):
        sa, sb = jax.nn.sigmoid(ga_ref[...]), jax.nn.sigmoid(gb_ref[...])
        dm = dm_ref[...]
        dya_ref[...] = (dm * sa).astype(BF16)
        dyb_ref[...] = (dm * sb).astype(BF16)
        dga_ref[...] = (dm * ya_ref[...] * (sa * (1.0 - sa))).astype(BF16)
        dgb_ref[...] = (dm * yb_ref[...] * (sb * (1.0 - sb))).astype(BF16)

    row = pl.BlockSpec((tm, d), lambda i: (i, 0))
    o = jax.ShapeDtypeStruct((m, d), BF16)
    return pl.pallas_call(
        body, name=name, out_shape=(o, o, o, o), grid=(m // tm,),
        in_specs=[pl.BlockSpec((tm, d), lambda i: (i, blk_a)), pl.BlockSpec((tm, d), lambda i: (i, blk_b)), row, row, row],
        out_specs=(row, row, row, row), compiler_params=_params(("parallel",)),
    )(proj, proj, ya, yb, dmerged)


def loss_fwd_bwd(name, y, target, tm=512):
    m, d = y.shape
    tm = _tile(m, tm, 16)

    def body(y_ref, t_ref, l_ref, dy_ref, dyb_ref):
        err = y_ref[...] - t_ref[...]
        dy = err * (1.0 / d)
        dy_ref[...] = dy
        dyb_ref[...] = dy.astype(BF16)
        part = 0.5 * jnp.sum(jnp.mean(err * err, axis=-1, keepdims=True), axis=0, keepdims=True)

        @pl.when(pl.program_id(0) == 0)
        def _():
            l_ref[...] = part

        @pl.when(pl.program_id(0) > 0)
        def _():
            l_ref[...] += part

    row = pl.BlockSpec((tm, d), lambda i: (i, 0))
    return pl.pallas_call(
        body, name=name, grid=(m // tm,),
        out_shape=(jax.ShapeDtypeStruct((1, 1), F32), jax.ShapeDtypeStruct((m, d), F32), jax.ShapeDtypeStruct((m, d), BF16)),
        in_specs=[row, row], out_specs=(pl.BlockSpec((1, 1), lambda i: (0, 0)), row, row),
        compiler_params=_params(("arbitrary",)),
    )(y, target)


def adamw(name, w, g, m, v):
    r, c = w.shape
    cap = (2 * 1024 * 1024) // 4
    tr, tc = _tile(r, 1024, 8), c
    while tr * tc > cap and tc % (2 * LANES) == 0:
        tc //= 2
    while tr * tc > cap and tr % 16 == 0:
        tr //= 2
    c1 = np.float32(1.0 - ADAM_B1 ** ADAM_STEP)
    c2 = np.float32(1.0 - ADAM_B2 ** ADAM_STEP)

    def body(w_ref, g_ref, m_ref, v_ref, d_ref, nm_ref, nv_ref):
        gv = g_ref[...]
        nm = ADAM_B1 * m_ref[...] + (1.0 - ADAM_B1) * gv
        nv = ADAM_B2 * v_ref[...] + (1.0 - ADAM_B2) * (gv * gv)
        d_ref[...] = -ADAM_LR * ((nm / c1) / (jnp.sqrt(nv / c2) + ADAM_EPS) + ADAM_WD * w_ref[...])
        nm_ref[...] = nm
        nv_ref[...] = nv

    blk = pl.BlockSpec((tr, tc), lambda i, j: (i, j))
    o = jax.ShapeDtypeStruct((r, c), F32)
    return pl.pallas_call(
        body, name=name, out_shape=(o, o, o), grid=(r // tr, c // tc), in_specs=[blk] * 4, out_specs=(blk, blk, blk),
        compiler_params=_params(("parallel", "parallel")),
    )(w, g, m, v)


HEAD_PAD = LANES


def _rope_rot():
    r = np.zeros((HEAD_PAD, HEAD_PAD), np.float32)
    half = QK_ROPE_DIM // 2
    for j in range(half):
        r[QK_NOPE_DIM + half + j, QK_NOPE_DIM + j] = -1.0
        r[QK_NOPE_DIM + j, QK_NOPE_DIM + half + j] = 1.0
    return r


def _head_spec(s, blk0):
    return pl.BlockSpec((None, s, HEAD_PAD), lambda bi, hi: (bi, 0, blk0 + hi))


def qk_prep_fwd(name, x, blk0, n_heads, d_real, gain, rope=None, extra=None):
    b, s, _ = x.shape

    def body(*refs):
        x_ref, g_ref = refs[0], refs[1]
        pos = 2
        xv = x_ref[...]
        if extra is not None:
            xv = xv + refs[pos][...]
            pos += 1
        y = xv * lax.rsqrt(jnp.sum(xv * xv, axis=-1, keepdims=True) * (1.0 / d_real) + EPS) * g_ref[...]
        if rope is not None:
            c_ref, s_ref, r_ref = refs[pos:pos + 3]
            rot = jnp.dot(y, r_ref[...], precision=lax.Precision.HIGHEST, preferred_element_type=F32)
            y = y * c_ref[...] + rot * s_ref[...]
        refs[-1][...] = y.astype(BF16)

    vec = pl.BlockSpec((1, HEAD_PAD), lambda bi, hi: (0, 0))
    tab = pl.BlockSpec((None, s, HEAD_PAD), lambda bi, hi: (bi, 0, 0))
    in_specs, args = [_head_spec(s, blk0), vec], [x, gain]
    if extra is not None:
        e_blk = extra[1]
        in_specs.append(pl.BlockSpec((None, s, HEAD_PAD), lambda bi, hi: (bi, 0, e_blk)))
        args.append(extra[0])
    if rope is not None:
        in_specs += [tab, tab, pl.BlockSpec((HEAD_PAD, HEAD_PAD), lambda bi, hi: (0, 0))]
        args += [rope[0], rope[1], jnp.asarray(_rope_rot())]
    return pl.pallas_call(
        body, name=name, out_shape=jax.ShapeDtypeStruct((b, s, n_heads * HEAD_PAD), BF16), grid=(b, n_heads),
        in_specs=in_specs, out_specs=_head_spec(s, 0), compiler_params=_params(("parallel", "parallel")),
    )(*args)


def qk_prep_bwd(name, x, blk0, n_heads, d_real, gain, dxh, rope=None, extra=None, head_sum=False):
    b, s, _ = x.shape

    def body(*refs):
        x_ref, g_ref, dy_ref = refs[:3]
        pos = 3
        xv = x_ref[...]
        if extra is not None:
            xv = xv + refs[pos][...]
            pos += 1
        dy = dy_ref[...]
        if rope is not None:
            c_ref, s_ref, rt_ref = refs[pos:pos + 3]
            pos += 3
            dy = dy * c_ref[...] + jnp.dot(dy * s_ref[...], rt_ref[...], precision=lax.Precision.HIGHEST, preferred_element_type=F32)
        outs = refs[pos:]
        dx_ref, dg_ref = outs[0], outs[-1]
        r = lax.rsqrt(jnp.sum(xv * xv, axis=-1, keepdims=True) * (1.0 / d_real) + EPS)
        xhat = xv * r
        dxhat = dy * g_ref[...]
        dx = r * (dxhat - xhat * (jnp.sum(dxhat * xhat, axis=-1, keepdims=True) * (1.0 / d_real)))
        dx_ref[...] = dx.astype(dx_ref.dtype)
        part = jnp.sum(dy * xhat, axis=0, keepdims=True)
        first = jnp.logical_and(pl.program_id(0) == 0, pl.program_id(1) == 0)

        @pl.when(first)
        def _():
            dg_ref[...] = part

        @pl.when(jnp.logical_not(first))
        def _():
            dg_ref[...] += part

        if head_sum:
            hs_ref = outs[1]

            @pl.when(pl.program_id(1) == 0)
            def _():
                hs_ref[...] = dx

            @pl.when(pl.program_id(1) > 0)
            def _():
                hs_ref[...] += dx

    vec = pl.BlockSpec((1, HEAD_PAD), lambda bi, hi: (0, 0))
    tab = pl.BlockSpec((None, s, HEAD_PAD), lambda bi, hi: (bi, 0, 0))
    in_specs, args = [_head_spec(s, blk0), vec, _head_spec(s, 0)], [x, gain, dxh]
    if extra is not None:
        e_blk = extra[1]
        in_specs.append(pl.BlockSpec((None, s, HEAD_PAD), lambda bi, hi: (bi, 0, e_blk)))
        args.append(extra[0])
    if rope is not None:
        in_specs += [tab, tab, pl.BlockSpec((HEAD_PAD, HEAD_PAD), lambda bi, hi: (0, 0))]
        args += [rope[0], rope[1], jnp.asarray(_rope_rot().T.copy())]
    out_shape, out_specs = [jax.ShapeDtypeStruct((b, s, n_heads * HEAD_PAD), BF16)], [_head_spec(s, 0)]
    if head_sum:
        out_shape.append(jax.ShapeDtypeStruct((b, s, HEAD_PAD), F32))
        out_specs.append(tab)
    out_shape.append(jax.ShapeDtypeStruct((1, HEAD_PAD), F32))
    out_specs.append(vec)
    return pl.pallas_call(
        body, name=name, out_shape=tuple(out_shape), grid=(b, n_heads), in_specs=in_specs, out_specs=tuple(out_specs),
        compiler_params=_params(("arbitrary", "arbitrary")),
    )(*args)


def _swa_specs(v_blk0):
    q = pl.BlockSpec((None, BLOCK, GROUP_A * HEAD_PAD), lambda b, kv, n: (b, n, kv))
    kprev = pl.BlockSpec((None, BLOCK, HEAD_PAD), lambda b, kv, n: (b, jnp.maximum(n - 1, 0), kv))
    kcur = pl.BlockSpec((None, BLOCK, HEAD_PAD), lambda b, kv, n: (b, n, kv))
    vprev = pl.BlockSpec((None, BLOCK, HEAD_PAD), lambda b, kv, n: (b, jnp.maximum(n - 1, 0), v_blk0 + kv))
    vcur = pl.BlockSpec((None, BLOCK, HEAD_PAD), lambda b, kv, n: (b, n, v_blk0 + kv))
    pcol = pl.BlockSpec((None, BLOCK, 1), lambda b, kv, n: (b, n, 0))
    prow_prev = pl.BlockSpec((None, 1, BLOCK), lambda b, kv, n: (b, 0, jnp.maximum(n - 1, 0)))
    prow_cur = pl.BlockSpec((None, 1, BLOCK), lambda b, kv, n: (b, 0, n))
    smem = pl.BlockSpec(memory_space=pltpu.SMEM)
    return q, kprev, kcur, vprev, vcur, pcol, prow_prev, prow_cur, smem


def _swa_probs(q, kk, dist, valid, slope, sink):
    sc = lax.dot_general(q, kk, _NT, preferred_element_type=F32) * (HEAD_DIM_A ** -0.5)
    sc = sc - slope * dist
    sc = jnp.where(valid, sc, NEG)
    m = jnp.maximum(jnp.max(sc, axis=-1, keepdims=True), sink)
    e = jnp.exp(sc - m)
    es = jnp.exp(sink - m)
    inv = 1.0 / (jnp.sum(e, axis=-1, keepdims=True) + es)
    return e * inv, es * inv


def _swa_window(n, kp_ref, kc_ref, vp_ref, vc_ref, pc_ref, prp_ref, prc_ref):
    kk = jnp.concatenate([kp_ref[...], kc_ref[...]], axis=0)
    vv = jnp.concatenate([vp_ref[...], vc_ref[...]], axis=0).astype(BF16)
    dist = pc_ref[...] - jnp.concatenate([prp_ref[...], prc_ref[...]], axis=1)
    qi = lax.broadcasted_iota(jnp.int32, (BLOCK, 2 * BLOCK), 0) + BLOCK
    ki = lax.broadcasted_iota(jnp.int32, (BLOCK, 2 * BLOCK), 1)
    diff = qi - ki
    valid = (diff >= 0) & (diff < BLOCK) & ((n > 0) | (ki >= BLOCK))
    return kk, vv, dist, valid


def swa_fwd(name, q, k, vsrc, v_blk0, pos_col, pos_row, slopes, sinks, plan=None):
    b, s, _ = q.shape
    qs, kprev, kcur, vprev, vcur, pcol, prp, prc, smem = _swa_specs(v_blk0)
    grid = (b, N_KV_HEADS_A, s // BLOCK)
    p_ins, p_in_specs, p_outs, p_out_specs, p_scratch, semantics, at_start, at_end = _hosted(plan, 10, 1, grid, ("parallel",) * 3)

    def body(*refs):
        q_ref, kp_ref, kc_ref, vp_ref, vc_ref, pc_ref, prp_ref, prc_ref, sl_ref, sk_ref = refs[:10]
        o_ref = refs[10 + len(p_ins)]
        at_start(refs)
        kv, n = pl.program_id(1), pl.program_id(2)
        kk, vv, dist, valid = _swa_window(n, kp_ref, kc_ref, vp_ref, vc_ref, pc_ref, prp_ref, prc_ref)
        for g in range(GROUP_A):
            hd = kv * GROUP_A + g
            lanes = slice(g * HEAD_PAD, (g + 1) * HEAD_PAD)
            p, _ = _swa_probs(q_ref[:, lanes], kk, dist, valid, sl_ref[hd], sk_ref[hd])
            o_ref[:, lanes] = jnp.dot(p.astype(BF16), vv, preferred_element_type=F32).astype(BF16)
        at_end(refs)

    res = pl.pallas_call(
        body, name=name, out_shape=(jax.ShapeDtypeStruct(q.shape, BF16),) + tuple(p_outs), grid=grid,
        in_specs=[qs, kprev, kcur, vprev, vcur, pcol, prp, prc, smem, smem] + p_in_specs, out_specs=(qs,) + tuple(p_out_specs),
        scratch_shapes=p_scratch, compiler_params=_params(semantics),
    )(q, k, k, vsrc, vsrc, pos_col, pos_row, pos_row, slopes, sinks, *p_ins)
    return res[0], list(res[1:])


def swa_bwd(name, q, k, vsrc, v_blk0, pos_col, pos_row, slopes, sinks, do, plan=None):
    b, s, _ = q.shape
    qs, kprev, kcur, vprev, vcur, pcol, prp, prc, smem = _swa_specs(v_blk0)
    grid = (b, N_KV_HEADS_A, s // BLOCK)
    p_ins, p_in_specs, p_outs, p_out_specs, p_scratch, semantics, at_start, at_end = _hosted(plan, 11, 4, grid, ("arbitrary",) * 3)

    def body(*refs):
        at_start(refs)
        compute(*refs[:11], *refs[11 + len(p_ins):15 + len(p_ins)])
        at_end(refs)

    def compute(q_ref, kp_ref, kc_ref, vp_ref, vc_ref, pc_ref, prp_ref, prc_ref, sl_ref, sk_ref, do_ref, dq_ref, dk_ref, dv_ref, ds_ref):
        bi, kv, n = pl.program_id(0), pl.program_id(1), pl.program_id(2)
        kk, vv, dist, valid = _swa_window(n, kp_ref, kc_ref, vp_ref, vc_ref, pc_ref, prp_ref, prc_ref)

        @pl.when((bi == 0) & (kv == 0) & (n == 0))
        def _():
            ds_ref[...] = jnp.zeros_like(ds_ref)

        @pl.when(n == 0)
        def _():
            dk_ref[...] = jnp.zeros_like(dk_ref)
            dv_ref[...] = jnp.zeros_like(dv_ref)

        dkk = jnp.zeros((2 * BLOCK, HEAD_PAD), F32)
        dvv = jnp.zeros((2 * BLOCK, HEAD_PAD), F32)
        head_row = lax.broadcasted_iota(jnp.int32, (N_HEADS_A, LANES), 0)
        dsink = jnp.zeros((N_HEADS_A, LANES), F32)
        for g in range(GROUP_A):
            hd = kv * GROUP_A + g
            lanes = slice(g * HEAD_PAD, (g + 1) * HEAD_PAD)
            qg = q_ref[:, lanes]
            p, ps = _swa_probs(qg, kk, dist, valid, sl_ref[hd], sk_ref[hd])
            dob = do_ref[:, lanes].astype(BF16)
            dvv = dvv + lax.dot_general(p.astype(BF16), dob, _TN, preferred_element_type=F32)
            dp = lax.dot_general(dob, vv, _NT, preferred_element_type=F32)
            rs = jnp.sum(p * dp, axis=-1, keepdims=True)
            dsb = (p * (dp - rs) * (HEAD_DIM_A ** -0.5)).astype(BF16)
            dq_ref[:, lanes] = jnp.dot(dsb, kk, preferred_element_type=F32)
            dkk = dkk + lax.dot_general(dsb, qg, _TN, preferred_element_type=F32)
            dsink = dsink + jnp.where(head_row == hd, -jnp.sum(ps * rs), 0.0)
        ds_ref[...] += dsink

        @pl.when(n > 0)
        def _():
            start = pl.multiple_of((n - 1) * BLOCK, BLOCK)
            dk_ref[pl.ds(start, 2 * BLOCK), :] += dkk
            dv_ref[pl.ds(start, 2 * BLOCK), :] += dvv

        @pl.when(n == 0)
        def _():
            dk_ref[0:BLOCK, :] += dkk[BLOCK:, :]
            dv_ref[0:BLOCK, :] += dvv[BLOCK:, :]

    kv_full = pl.BlockSpec((None, s, HEAD_PAD), lambda bi, kv, n: (bi, 0, kv))
    kv_shape = jax.ShapeDtypeStruct(k.shape, F32)
    res = pl.pallas_call(
        body, name=name, grid=grid,
        out_shape=(jax.ShapeDtypeStruct(q.shape, F32), kv_shape, kv_shape, jax.ShapeDtypeStruct((N_HEADS_A, LANES), F32)) + tuple(p_outs),
        in_specs=[qs, kprev, kcur, vprev, vcur, pcol, prp, prc, smem, smem, qs] + p_in_specs,
        out_specs=(qs, kv_full, kv_full, pl.BlockSpec((N_HEADS_A, LANES), lambda bi, kv, n: (0, 0))) + tuple(p_out_specs),
        scratch_shapes=p_scratch, compiler_params=_params(semantics),
    )(q, k, k, vsrc, vsrc, pos_col, pos_row, pos_row, slopes, sinks, do, *p_ins)
    return res[0], res[1], res[2], res[3], list(res[4:])


MLA_T = 512


def _hosted(plan, n_in, n_out, grid, semantics):
    if plan is None:
        return [], [], [], [], [], semantics, (lambda refs: None), (lambda refs: None)
    ni, no = len(plan.ins), len(plan.out_shapes)

    def split(refs):
        return refs[n_in:n_in + ni], refs[n_in + ni + n_out:n_in + ni + n_out + no], refs[-2], refs[-1]

    def at_start(refs):
        @pl.when(functools.reduce(jnp.logical_and, [pl.program_id(ax) == 0 for ax in range(len(grid))]))
        def _():
            plan.start(*split(refs))

    def at_end(refs):
        @pl.when(functools.reduce(jnp.logical_and, [pl.program_id(ax) == g - 1 for ax, g in enumerate(grid)]))
        def _():
            plan.finish(*split(refs))

    return plan.ins, [_HBM] * ni, plan.out_shapes, [_HBM] * no, plan.scratch(), ("arbitrary",) * len(grid), at_start, at_end


def mla_fwd(name, q, k, v, plan=None):
    b, s, w = q.shape
    h = w // HEAD_PAD
    t = MLA_T if s % MLA_T == 0 else BLOCK
    nb = s // t
    scale = QK_DIM_B ** -0.5
    p_ins, p_in_specs, p_outs, p_out_specs, p_scratch, semantics, at_start, at_end = _hosted(
        plan, 3, 3, (b, h), ("parallel", "parallel"))

    def body(*refs):
        q_ref, k_ref, v_ref = refs[:3]
        o_ref, ob_ref, lse_ref = refs[3 + len(p_ins):6 + len(p_ins)]
        at_start(refs)
        causal = lax.broadcasted_iota(jnp.int32, (t, t), 0) <= lax.broadcasted_iota(jnp.int32, (t, t), 1)
        for i in range(nb):
            q_i = q_ref[i * t:(i + 1) * t, :]

            def step(j, carry, q_i=q_i, diagonal=False):
                m, l, acc = carry
                rows = slice(j * t, (j + 1) * t) if diagonal else pl.ds(pl.multiple_of(j * t, t), t)
                st = lax.dot_general(k_ref[rows, :], q_i, _NT, preferred_element_type=F32) * scale
                if diagonal:
                    st = jnp.where(causal, st, NEG)
                m_new = jnp.maximum(m, jnp.max(st, axis=0, keepdims=True))
                a = jnp.exp(m - m_new)
                p = jnp.exp(st - m_new)
                l = a * l + jnp.sum(p, axis=0, keepdims=True)
                acc = a * acc + lax.dot_general(v_ref[rows, :], p.astype(BF16), _TN, preferred_element_type=F32)
                return m_new, l, acc

            carry = (jnp.full((1, t), NEG, F32), jnp.zeros((1, t), F32), jnp.zeros((HEAD_PAD, t), F32))
            if i > 0:
                carry = lax.fori_loop(0, i, step, carry)
            m, l, acc = step(i, carry, diagonal=True)
            o_i = (acc / l).T
            o_ref[i * t:(i + 1) * t, :] = o_i
            ob_ref[i * t:(i + 1) * t, :] = o_i.astype(BF16)
            lse_ref[i] = m + jnp.log(l)
        at_end(refs)

    hs = _head_spec(s, 0)
    ls = pl.BlockSpec((None, None, nb, 1, t), lambda bi, hi: (bi, hi, 0, 0, 0))
    res = pl.pallas_call(
        body, name=name, grid=(b, h),
        out_shape=(jax.ShapeDtypeStruct((b, s, w), F32), jax.ShapeDtypeStruct((b, s, w), BF16),
                   jax.ShapeDtypeStruct((b, h, nb, 1, t), F32)) + tuple(p_outs),
        in_specs=[hs, hs, hs] + p_in_specs, out_specs=(hs, hs, ls) + tuple(p_out_specs),
        scratch_shapes=p_scratch, compiler_params=_params(semantics),
    )(q, k, v, *p_ins)
    return res[0], res[1], res[2], list(res[3:])


def mla_bwd(name, q, k, v, o, do, lse, plan=None):
    b, s, w = q.shape
    h = w // HEAD_PAD
    _, _, nb, _, t = lse.shape
    scale = QK_DIM_B ** -0.5
    p_ins, p_in_specs, p_outs, p_out_specs, p_scratch, semantics, at_start, at_end = _hosted(
        plan, 6, 3, (b, h), ("parallel", "parallel"))

    def body(*refs):
        q_ref, k_ref, v_ref, o_ref, do_ref, lse_ref = refs[:6]
        dq_ref, dk_ref, dv_ref = refs[6 + len(p_ins):9 + len(p_ins)]
        dv_acc = refs[9 + len(p_ins) + len(p_outs)]
        at_start(refs)
        causal = lax.broadcasted_iota(jnp.int32, (t, t), 0) <= lax.broadcasted_iota(jnp.int32, (t, t), 1)
        dk_ref[...] = jnp.zeros_like(dk_ref)
        dv_acc[...] = jnp.zeros_like(dv_acc)
        for i in range(nb):
            q_i = q_ref[i * t:(i + 1) * t, :]
            do_i = do_ref[i * t:(i + 1) * t, :]
            delta = jnp.sum((o_ref[i * t:(i + 1) * t, :] * do_i).T, axis=0, keepdims=True)
            do_b = do_i.astype(BF16)
            lse_i = lse_ref[i]

            def step(j, dqt, q_i=q_i, do_b=do_b, delta=delta, lse_i=lse_i, diagonal=False):
                rows = slice(j * t, (j + 1) * t) if diagonal else pl.ds(pl.multiple_of(j * t, t), t)
                k_j = k_ref[rows, :]
                st = lax.dot_general(k_j, q_i, _NT, preferred_element_type=F32) * scale
                if diagonal:
                    st = jnp.where(causal, st, NEG)
                pt = jnp.exp(st - lse_i)
                dpt = lax.dot_general(v_ref[rows, :], do_b, _NT, preferred_element_type=F32)
                dst = (pt * (dpt - delta) * scale).astype(BF16)
                dv_acc[rows, :] += jnp.dot(pt.astype(BF16), do_b, preferred_element_type=F32)
                dk_ref[rows, :] += jnp.dot(dst, q_i, preferred_element_type=F32)
                return dqt + lax.dot_general(k_j, dst, _TN, preferred_element_type=F32)

            dqt = jnp.zeros((HEAD_PAD, t), F32)
            if i > 0:
                dqt = lax.fori_loop(0, i, step, dqt)
            dq_ref[i * t:(i + 1) * t, :] = step(i, dqt, diagonal=True).T
        dv_ref[...] = dv_acc[...].astype(BF16)
        at_end(refs)

    hs = _head_spec(s, 0)
    ls = pl.BlockSpec((None, None, nb, 1, t), lambda bi, hi: (bi, hi, 0, 0, 0))
    res = pl.pallas_call(
        body, name=name, grid=(b, h),
        out_shape=(jax.ShapeDtypeStruct((b, s, w), F32), jax.ShapeDtypeStruct((b, s, w), F32),
                   jax.ShapeDtypeStruct((b, s, w), BF16)) + tuple(p_outs),
        in_specs=[hs, hs, hs, hs, hs, ls] + p_in_specs, out_specs=(hs, hs, hs) + tuple(p_out_specs),
        scratch_shapes=[pltpu.VMEM((s, HEAD_PAD), F32)] + p_scratch,
        compiler_params=_params(semantics),
    )(q, k, v, o, do, lse, *p_ins)
    return res[0], res[1], res[2], list(res[3:])


_HBM = pl.BlockSpec(memory_space=pltpu.HBM)


def _place():
    x, y, c = lax.axis_index("x"), lax.axis_index("y"), lax.axis_index("c")
    chips = [(1 - x, y), (x, 1 - y), (1 - x, 1 - y)]
    return x, y, c, chips


def _remote(src, dst, send_sem, recv_sem, dev):
    return pltpu.make_async_remote_copy(src_ref=src, dst_ref=dst, send_sem=send_sem, recv_sem=recv_sem,
                                        device_id=dev, device_id_type=MESH)


class CommPlan:
    def __init__(self, ins, out_shapes, n_sems, start, finish, aliases=None):
        self.ins, self.out_shapes, self.n_sems = list(ins), list(out_shapes), n_sems
        self.start, self.finish, self.aliases = start, finish, dict(aliases or {})

    def scratch(self):
        return [pltpu.SemaphoreType.DMA((self.n_sems,)), pltpu.SemaphoreType.DMA((self.n_sems,))]


def comm_call(name, plan):
    ni, no = len(plan.ins), len(plan.out_shapes)

    def body(*refs):
        ins, outs, (send_sems, recv_sems) = refs[:ni], refs[ni:ni + no], refs[ni + no:]
        plan.start(ins, outs, send_sems, recv_sems)
        plan.finish(ins, outs, send_sems, recv_sems)

    return pl.pallas_call(
        body, name=name, out_shape=tuple(plan.out_shapes), in_specs=[_HBM] * ni, out_specs=tuple([_HBM] * no),
        input_output_aliases=plan.aliases, scratch_shapes=plan.scratch(),
    )(*plan.ins)


def gather_plan(packs, l):
    nk = len(packs)

    def parts(p_refs, o_refs, ss, rs):
        x, y, c, chips = _place()
        s_me = 2 * x + y
        sibling = (x, y, 1 - c)
        first = [_remote(p_refs[k].at[l], o_refs[k].at[s_me], ss.at[j * nk + k], rs.at[j * nk + k], (cx, cy, c))
                 for j, (cx, cy) in enumerate(chips) for k in range(nk)]
        own = [_remote(p_refs[k].at[l], o_refs[k].at[s_me], ss.at[6 * nk + k], rs.at[6 * nk + k], sibling) for k in range(nk)]
        return c, chips, sibling, first, own

    def start(p_refs, o_refs, ss, rs):
        c, _, _, first, own = parts(p_refs, o_refs, ss, rs)
        for cp in own:
            cp.start()

        @pl.when(c == l)
        def _():
            for cp in first:
                cp.start()

    def finish(p_refs, o_refs, ss, rs):
        c, chips, sibling, first, own = parts(p_refs, o_refs, ss, rs)

        @pl.when(c == l)
        def _():
            passed = []
            for j, (cx, cy) in enumerate(chips):
                for k in range(nk):
                    blk = o_refs[k].at[2 * cx + cy]
                    _remote(blk, blk, ss.at[j * nk + k], rs.at[j * nk + k], (cx, cy, c)).wait_recv()
                    fwd = _remote(blk, blk, ss.at[(3 + j) * nk + k], rs.at[(3 + j) * nk + k], sibling)
                    fwd.start()
                    passed.append(fwd)
            for cp in first + passed:
                cp.wait_send()

        @pl.when(c != l)
        def _():
            for j, (cx, cy) in enumerate(chips):
                for k in range(nk):
                    blk = o_refs[k].at[2 * cx + cy]
                    _remote(blk, blk, ss.at[(3 + j) * nk + k], rs.at[(3 + j) * nk + k], sibling).wait_recv()

        for cp in own:
            cp.wait()

    outs = [jax.ShapeDtypeStruct((N_SHARDS,) + p.shape[1:], p.dtype) for p in packs]
    return CommPlan(packs, outs, 7 * nk, start, finish)


def swap_plan(grads, l):
    nk = len(grads)

    def copies(g_refs, o_refs, ss, rs):
        x, y, c, _ = _place()
        return c, [_remote(g_refs[k], o_refs[k], ss.at[k], rs.at[k], (x, y, 1 - c)) for k in range(nk)]

    def start(g_refs, o_refs, ss, rs):
        c, cps = copies(g_refs, o_refs, ss, rs)

        @pl.when(c != l)
        def _():
            for cp in cps:
                cp.start()

    def finish(g_refs, o_refs, ss, rs):
        c, cps = copies(g_refs, o_refs, ss, rs)

        @pl.when(c != l)
        def _():
            for cp in cps:
                cp.wait_send()

        @pl.when(c == l)
        def _():
            for cp in cps:
                cp.wait_recv()

    return CommPlan(grads, [jax.ShapeDtypeStruct(g.shape, g.dtype) for g in grads], nk, start, finish)


def scatter_plan(parts, l):
    nk = len(parts)

    def copies(p_refs, o_refs, ss, rs):
        x, y, c, chips = _place()
        s_me = 2 * x + y
        sends = [_remote(p_refs[k].at[2 * cx + cy], o_refs[k].at[s_me], ss.at[j * nk + k], rs.at[j * nk + k], (cx, cy, c))
                 for j, (cx, cy) in enumerate(chips) for k in range(nk)]
        return c, chips, sends

    def start(p_refs, o_refs, ss, rs):
        c, _, sends = copies(p_refs, o_refs, ss, rs)

        @pl.when(c == l)
        def _():
            for cp in sends:
                cp.start()

    def finish(p_refs, o_refs, ss, rs):
        c, chips, sends = copies(p_refs, o_refs, ss, rs)

        @pl.when(c == l)
        def _():
            for j, (cx, cy) in enumerate(chips):
                for k in range(nk):
                    slot = o_refs[k].at[2 * cx + cy]
                    _remote(slot, slot, ss.at[j * nk + k], rs.at[j * nk + k], (cx, cy, c)).wait_recv()
            for cp in sends:
                cp.wait_send()

    return CommPlan(parts, [jax.ShapeDtypeStruct(p.shape, p.dtype) for p in parts], 3 * nk, start, finish)


def share_plan(finals):
    flat = [(l, f) for l, fs in enumerate(finals) for f in fs]

    def copies(f_refs, o_refs, ss, rs):
        x, y, c, _ = _place()
        return c, [_remote(f_refs[i], o_refs[i], ss.at[i], rs.at[i], (x, y, 1 - c)) for i in range(len(flat))]

    def start(f_refs, o_refs, ss, rs):
        c, cps = copies(f_refs, o_refs, ss, rs)
        for l in range(len(finals)):
            @pl.when(c == l)
            def _(l=l):
                for cp, (lf, _) in zip(cps, flat):
                    if lf == l:
                        cp.start()

    def finish(f_refs, o_refs, ss, rs):
        c, cps = copies(f_refs, o_refs, ss, rs)
        for l in range(len(finals)):
            @pl.when(c == l)
            def _(l=l):
                for cp, (lf, _) in zip(cps, flat):
                    if lf == l:
                        cp.wait_send()

            @pl.when(c != l)
            def _(l=l):
                for cp, (lf, _) in zip(cps, flat):
                    if lf == l:
                        cp.wait_recv()

    arrays = [f for _, f in flat]
    return CommPlan(arrays, [jax.ShapeDtypeStruct(f.shape, f.dtype) for f in arrays], len(flat), start, finish,
                    aliases={i: i for i in range(len(flat))})


def add_pair(name, own, other, tr=1024):
    n, cols = own.shape
    tr = _tile(n, tr, 16)

    def body(g_ref, o_ref, out_ref):
        out_ref[...] = (g_ref[...].astype(F32) + o_ref[...].astype(F32)).astype(out_ref.dtype)

    row = pl.BlockSpec((tr, cols), lambda i: (i, 0))
    return pl.pallas_call(
        body, name=name, out_shape=jax.ShapeDtypeStruct((n, cols), BF16), grid=(n // tr,), in_specs=[row, row], out_specs=row,
        compiler_params=_params(("parallel",)),
    )(own, other)


def sum_slots(name, slots, part, shard, tr=512):
    ns, r, cols = slots.shape
    tr = _tile(r, tr, 16)

    def body(s_idx, s_ref, own_ref, o_ref):
        total = own_ref[...].astype(F32)
        for k in range(1, ns):
            total = total + s_ref[(s_idx[0] + k) % ns].astype(F32)
        o_ref[...] = total

    return pl.pallas_call(
        body, name=name, out_shape=jax.ShapeDtypeStruct((r, cols), F32),
        grid_spec=pltpu.PrefetchScalarGridSpec(
            num_scalar_prefetch=1, grid=(r // tr,),
            in_specs=[pl.BlockSpec((ns, tr, cols), lambda i, p: (0, i, 0)), pl.BlockSpec((None, tr, cols), lambda i, p: (p[0], i, 0))],
            out_specs=pl.BlockSpec((tr, cols), lambda i, p: (i, 0))),
        compiler_params=_params(("parallel",)),
    )(shard, slots, part)


def small_allreduce(vec):
    r, cols = vec.shape

    def body(v_ref, o_ref, buf, send_sems, recv_sems):
        x, y, c, _ = _place()
        me = 4 * x + 2 * y + c
        buf[me] = v_ref[...]
        peers = []
        for k in range(1, N_DEV):
            px = (1 - x) if (k & 4) else x
            py = (1 - y) if (k & 2) else y
            pc = (1 - c) if (k & 1) else c
            peers.append((px, py, pc))
        sends = [_remote(buf.at[me], buf.at[me], send_sems.at[k], recv_sems.at[k], peer) for k, peer in enumerate(peers)]
        for cp in sends:
            cp.start()
        for k, (px, py, pc) in enumerate(peers):
            slot = buf.at[4 * px + 2 * py + pc]
            _remote(slot, slot, send_sems.at[k], recv_sems.at[k], (px, py, pc)).wait_recv()
        for cp in sends:
            cp.wait_send()
        total = buf[0]
        for k in range(1, N_DEV):
            total = total + buf[k]
        o_ref[...] = total

    return pl.pallas_call(
        body, name="small_allreduce", out_shape=jax.ShapeDtypeStruct((r, cols), F32),
        in_specs=[pl.BlockSpec(memory_space=pltpu.VMEM)], out_specs=pl.BlockSpec(memory_space=pltpu.VMEM),
        scratch_shapes=[pltpu.VMEM((N_DEV, r, cols), F32), pltpu.SemaphoreType.DMA((N_DEV - 1,)), pltpu.SemaphoreType.DMA((N_DEV - 1,))],
    )(vec)


ARG_NAMES = (("x", "positions") + WEIGHT_NAMES + ("loss_target",) + tuple("m_" + n for n in WEIGHT_NAMES)
             + tuple("v_" + n for n in WEIGHT_NAMES))

_C_KA = N_HEADS_A * HEAD_PAD
_C_VA = _C_KA + N_KV_HEADS_A * HEAD_PAD
_C_CQ = _C_VA + N_KV_HEADS_A * HEAD_PAD
_C_CKV = _C_CQ + Q_LORA_RANK
_C_KR = _C_CKV + KV_LORA_RANK
_C_GA = _C_KR + HEAD_PAD
_R_KA = WIDTH_A
_R_VA = _R_KA + KV_WIDTH_A
_R_CQ = _R_VA + KV_WIDTH_A
_R_KR = _R_CQ + Q_LORA_RANK + KV_LORA_RANK
_R_GA = _R_KR + QK_ROPE_DIM


def _pad_head_cols(w, n_heads):
    r, dh = w.shape[0], w.shape[1] // n_heads
    return jnp.pad(w.reshape(r, n_heads, dh), ((0, 0), (0, 0), (0, HEAD_PAD - dh))).reshape(r, n_heads * HEAD_PAD)


def _unpad_head_cols(w, n_heads, dh):
    r = w.shape[0]
    return w.reshape(r, n_heads, HEAD_PAD)[:, :, :dh].reshape(r, n_heads * dh)


def _pad_head_rows(w, n_heads):
    dh, c = w.shape[0] // n_heads, w.shape[1]
    return jnp.pad(w.reshape(n_heads, dh, c), ((0, 0), (0, HEAD_PAD - dh), (0, 0))).reshape(n_heads * HEAD_PAD, c)


def _unpad_head_rows(w, n_heads, dh):
    c = w.shape[1]
    return w.reshape(n_heads, HEAD_PAD, c)[:, :dh].reshape(n_heads * dh, c)


def _join_cols(shards):
    ns, r, c = shards.shape
    return shards.transpose(1, 0, 2).reshape(r, ns * c)


def _split_cols(mat):
    r, c4 = mat.shape
    return mat.reshape(r, N_SHARDS, c4 // N_SHARDS).transpose(1, 0, 2)


def _ffn_fwd(tag, x, gain, wcol, wdown, plan=None):
    nb = rms_fwd(tag + "_rms", x, gain)
    gu, h, carried = ffn_up_act(tag + "_up", nb, wcol, 0, plan=plan)
    out = gmm_down(tag + "_down", h, wdown, 0, x, 0.5)
    return out, (x, nb, gu, h), carried


def _ffn_bwd(tag, saved, gain, wcol, wdown, dout, doutb, plans=None):
    plans = plans or {}
    x, nb, gu, h = saved
    carried = {}
    dgu, carried["dh"] = ffn_down_dact(tag + "_dh", doutb, wdown, gu, gu.shape[-1], 0, 0.5, plan=plans.get("dh"))
    gdown = gmm_down_dw(tag + "_dwd", h, doutb, 0.5)
    if plans.get("dwgu") is not None:
        gcol, carried["dwgu"] = gmm_up_dw(tag + "_dwgu", nb, dgu, plan=plans["dwgu"])
    else:
        gcol = gmm_up_dw(tag + "_dwgu", nb, dgu)
    if plans.get("dn") is not None:
        plan_dn = plans["dn"](gcol, gdown) if callable(plans["dn"]) else plans["dn"]
        dn, carried["dn"] = gmm_up_dx(tag + "_dn", dgu, wcol, x.shape[1], 0, plan=plan_dn)
    else:
        dn = gmm_up_dx(tag + "_dn", dgu, wcol, x.shape[1], 0)
    dx, dxb, dgain = rms_bwd(tag + "_drms", x, gain, dn, dres=dout)
    return dx, dxb, dgain, gcol, gdown, carried


def _pad_gain(g):
    return jnp.pad(g, ((0, 0), (0, HEAD_PAD - g.shape[1])))


def _mixer_fwd(tag, x1, sm, w, wo, aux, plans=None):
    plans = plans or {}
    b, s = aux["b"], aux["s"]
    t, d = x1.shape
    hb = rms_fwd(tag + "_rms", x1, sm["mix_norm"])
    carried = {}
    if plans.get("proj") is not None:
        proj, carried["proj"] = mm_nn(tag + "_proj", hb, w["win"], tm=1024, tn=2048, tk=1024, plan=plans["proj"])
    else:
        proj = mm_nn(tag + "_proj", hb, w["win"], tm=1024, tn=2048, tk=1024)
    proj3 = proj.reshape(b, s, proj.shape[1])
    gains = {n: _pad_gain(sm[n]) for n in ("swa_q_norm", "swa_k_norm", "mla_q_norm", "mla_k_norm")}
    cqn = rms_fwd(tag + "_rms_cq", proj, sm["mla_q_lora_norm"], tm=1024, col_blk=_C_CQ // Q_LORA_RANK)
    ckvn = rms_fwd(tag + "_rms_ckv", proj, sm["mla_kv_lora_norm"], tm=1024, col_blk=_C_CKV // KV_LORA_RANK)
    qb_raw = mm_nn(tag + "_uq", cqn, w["wuq"], tm=1024).reshape(b, s, N_HEADS_B * HEAD_PAD)
    kb_raw = mm_nn(tag + "_uk", ckvn, w["wk"], tm=1024).reshape(b, s, N_HEADS_B * HEAD_PAD)
    vb = mm_nn(tag + "_uv", ckvn, w["wv"], tm=1024, out_dtype=BF16).reshape(b, s, N_HEADS_B * HEAD_PAD)
    qah = qk_prep_fwd(tag + "_qa_norm", proj3, 0, N_HEADS_A, HEAD_DIM_A, gains["swa_q_norm"])
    kah = qk_prep_fwd(tag + "_ka_norm", proj3, _C_KA // HEAD_PAD, N_KV_HEADS_A, HEAD_DIM_A, gains["swa_k_norm"])
    sinks = sm["swa_sinks"].reshape(-1)
    oab, carried["swa"] = swa_fwd(tag + "_swa", qah, kah, proj3, _C_VA // HEAD_PAD, aux["pos_col"], aux["pos_row"], aux["slopes"],
                                  sinks, plan=plans.get("swa"))
    kr = (proj3, _C_KR // HEAD_PAD)
    qbh = qk_prep_fwd(tag + "_qb_norm", qb_raw, 0, N_HEADS_B, QK_DIM_B, gains["mla_q_norm"], aux["rope"])
    kbh = qk_prep_fwd(tag + "_kb_norm", kb_raw, 0, N_HEADS_B, QK_DIM_B, gains["mla_k_norm"], aux["rope"], extra=kr)
    ob, obb, lse, carried["mla"] = mla_fwd(tag + "_mla", qbh, kbh, vb, plans.get("mla"))
    oab2, obb2 = oab.reshape(t, -1), obb.reshape(t, -1)
    ya = mm_nn(tag + "_branch_a", oab2, w["wa"], tm=1024)
    yb = mm_nn(tag + "_branch_b", obb2, w["wb"], tm=1024)
    mg = gate_fwd(tag + "_gate", proj, ya, yb, _C_GA, _C_GA + d)
    x2 = gmm_rows(tag + "_out", mg, wo, x1)
    saved = dict(x1=x1, hb=hb, proj=proj, cqn=cqn, ckvn=ckvn, qb_raw=qb_raw, kb_raw=kb_raw, vb=vb, qah=qah, kah=kah,
                 qbh=qbh, kbh=kbh, ob=ob, lse=lse, oab2=oab2, obb2=obb2, ya=ya, yb=yb, mg=mg, gains=gains)
    return x2, saved, carried


def _mixer_bwd(tag, sv, sm, w, wo, aux, dx2, dx2b, plans=None):
    plans = plans or {}
    carried = {}
    b, s = aux["b"], aux["s"]
    t, d = dx2.shape
    proj, gains = sv["proj"], sv["gains"]
    proj3 = proj.reshape(b, s, proj.shape[1])
    kr = (proj3, _C_KR // HEAD_PAD)
    if plans.get("d_out") is not None:
        dmg, carried["d_out"] = gmm_rows_dx(tag + "_d_out", dx2b, wo, plan=plans["d_out"])
    else:
        dmg = gmm_rows_dx(tag + "_d_out", dx2b, wo)
    dwo = gmm_rows_dw(tag + "_dw_out", sv["mg"], dx2b)
    dya, dyb, dga, dgb = gate_bwd(tag + "_dgate", proj, sv["ya"], sv["yb"], dmg, _C_GA, _C_GA + d)
    dwa = mm_tn(tag + "_dw_branch_a", sv["oab2"], dya, out_dtype=BF16)
    dwb = mm_tn(tag + "_dw_branch_b", sv["obb2"], dyb, out_dtype=BF16)
    doa = mm_nt(tag + "_d_branch_a", dya, w["wa"], tm=1024).reshape(b, s, -1)
    dob = mm_nt(tag + "_d_branch_b", dyb, w["wb"], tm=1024).reshape(b, s, -1)
    sinks = sm["swa_sinks"].reshape(-1)
    dqah, dkah, dva, dsinks, carried["dswa"] = swa_bwd(tag + "_dswa", sv["qah"], sv["kah"], proj3, _C_VA // HEAD_PAD, aux["pos_col"],
                                                       aux["pos_row"], aux["slopes"], sinks, doa, plan=plans.get("dswa"))
    dqa_raw, dg_swa_q = qk_prep_bwd(tag + "_dqa_norm", proj3, 0, N_HEADS_A, HEAD_DIM_A, gains["swa_q_norm"], dqah)
    dka_raw, dg_swa_k = qk_prep_bwd(tag + "_dka_norm", proj3, _C_KA // HEAD_PAD, N_KV_HEADS_A, HEAD_DIM_A, gains["swa_k_norm"], dkah)
    plan_dmla = plans["dmla"](carried) if callable(plans.get("dmla")) else plans.get("dmla")
    dqbh, dkbh, dvb, carried["dmla"] = mla_bwd(tag + "_dmla", sv["qbh"], sv["kbh"], sv["vb"], sv["ob"], dob, sv["lse"], plan_dmla)
    dqb_raw, dg_mla_q = qk_prep_bwd(tag + "_dqb_norm", sv["qb_raw"], 0, N_HEADS_B, QK_DIM_B, gains["mla_q_norm"], dqbh, aux["rope"])
    dkb_raw, dkr_sum, dg_mla_k = qk_prep_bwd(tag + "_dkb_norm", sv["kb_raw"], 0, N_HEADS_B, QK_DIM_B, gains["mla_k_norm"], dkbh,
                                             aux["rope"], extra=kr, head_sum=True)
    dq_tok, dk_tok, dv_tok = dqb_raw.reshape(t, -1), dkb_raw.reshape(t, -1), dvb.reshape(t, -1)
    dwuq = mm_tn(tag + "_dw_uq", sv["cqn"], dq_tok, tk=1024, out_dtype=BF16)
    dwk = mm_tn(tag + "_dw_uk", sv["ckvn"], dk_tok, tk=1024, out_dtype=BF16)
    dwv = mm_tn(tag + "_dw_uv", sv["ckvn"], dv_tok, tk=1024, out_dtype=BF16)
    dcqn = mm_nt(tag + "_d_uq", dq_tok, w["wuq"], tm=1024)
    dckvn = mm_nt(tag + "_d_uv", dv_tok, w["wv"], tm=1024, res=mm_nt(tag + "_d_uk", dk_tok, w["wk"], tm=1024))
    dcq, dg_q_lora = rms_bwd(tag + "_drms_cq", proj, sm["mla_q_lora_norm"], dcqn, want_f32=False, tm=1024, col_blk=_C_CQ // Q_LORA_RANK)
    dckv, dg_kv_lora = rms_bwd(tag + "_drms_ckv", proj, sm["mla_kv_lora_norm"], dckvn, want_f32=False, tm=1024,
                               col_blk=_C_CKV // KV_LORA_RANK)
    dproj = jnp.concatenate([dqa_raw.reshape(t, -1), dka_raw.reshape(t, -1), dva.reshape(t, -1).astype(BF16), dcq, dckv,
                             dkr_sum.reshape(t, HEAD_PAD).astype(BF16), dga, dgb], axis=1)
    dwin = mm_tn(tag + "_dw_in", sv["hb"], dproj, tm=1024, tn=2048, tk=1024, out_dtype=BF16)
    dh = mm_nt(tag + "_d_in", dproj, w["win"], tm=1024, tn=1024, tk=2048)
    dx1, dx1b, dg_mix = rms_bwd(tag + "_drms", sv["x1"], sm["mix_norm"], dh, dres=dx2)
    wgrads = dict(win=dwin, wuq=dwuq, wk=dwk, wv=dwv, wa=dwa, wb=dwb, wo=dwo)
    sgrads = dict(mix_norm=dg_mix, swa_q_norm=dg_swa_q[:, :HEAD_DIM_A], swa_k_norm=dg_swa_k[:, :HEAD_DIM_A],
                  swa_sinks=dsinks[:, 0].reshape(1, -1), mla_q_lora_norm=dg_q_lora, mla_kv_lora_norm=dg_kv_lora,
                  mla_q_norm=dg_mla_q[:, :QK_DIM_B], mla_k_norm=dg_mla_k[:, :QK_DIM_B])
    return dx1, dx1b, wgrads, sgrads, carried


def _layer_weights(win4, uq4, ukv4, branch4):
    wr = _join_cols(win4)
    kr = jnp.pad(wr[:, _R_KR:_R_GA], ((0, 0), (QK_NOPE_DIM, HEAD_PAD - QK_DIM_B)))
    win = jnp.concatenate([_pad_head_cols(wr[:, :_R_KA], N_HEADS_A), _pad_head_cols(wr[:, _R_KA:_R_VA], N_KV_HEADS_A),
                           _pad_head_cols(wr[:, _R_VA:_R_CQ], N_KV_HEADS_A), wr[:, _R_CQ:_R_KR], kr, wr[:, _R_GA:]], axis=1)
    ukv = _join_cols(ukv4)
    ukv3 = ukv.reshape(ukv.shape[0], N_HEADS_B, QK_NOPE_DIM + V_DIM_B)
    r = branch4.shape[1] // 2
    return dict(win=win, wuq=_pad_head_cols(_join_cols(uq4), N_HEADS_B),
                wk=_pad_head_cols(ukv3[:, :, :QK_NOPE_DIM].reshape(ukv.shape[0], -1), N_HEADS_B),
                wv=_pad_head_cols(ukv3[:, :, QK_NOPE_DIM:].reshape(ukv.shape[0], -1), N_HEADS_B),
                wa=_pad_head_rows(_join_cols(branch4[:, :r]), N_HEADS_A), wb=_pad_head_rows(_join_cols(branch4[:, r:]), N_HEADS_B))


def _mixer_grad_shards(g):
    dw = g["win"]
    win_ref = jnp.concatenate([_unpad_head_cols(dw[:, :_C_KA], N_HEADS_A, HEAD_DIM_A),
                               _unpad_head_cols(dw[:, _C_KA:_C_VA], N_KV_HEADS_A, HEAD_DIM_A),
                               _unpad_head_cols(dw[:, _C_VA:_C_CQ], N_KV_HEADS_A, HEAD_DIM_A), dw[:, _C_CQ:_C_KR],
                               dw[:, _C_KR + QK_NOPE_DIM:_C_KR + QK_DIM_B], dw[:, _C_GA:]], axis=1)
    rk = g["wk"].shape[0]
    ukv = jnp.concatenate([g["wk"].reshape(rk, N_HEADS_B, HEAD_PAD)[:, :, :QK_NOPE_DIM],
                           g["wv"].reshape(rk, N_HEADS_B, HEAD_PAD)[:, :, :V_DIM_B]], axis=2).reshape(rk, -1)
    return (_split_cols(win_ref), _split_cols(_unpad_head_cols(g["wuq"], N_HEADS_B, QK_DIM_B)), _split_cols(ukv),
            jnp.concatenate([_split_cols(_unpad_head_rows(g["wa"], N_HEADS_A, HEAD_DIM_A)),
                             _split_cols(_unpad_head_rows(g["wb"], N_HEADS_B, V_DIM_B))], axis=1))


BUFFER_TAGS = ("col1", "col2", "down1", "down2", "wo", "win", "uq", "ukv", "branch")
FIRST_GATHER = ("col1", "down1")
GATHER_CARRIERS = {(0, "ffn1_up"): (0, ("wo", "win", "uq", "ukv", "branch")), (0, "proj"): (1, ("col1",)),
                   (0, "swa"): (1, ("down1", "wo", "uq", "ukv")), (0, "mla"): (0, ("col2", "down2")),
                   (0, "ffn2_up"): (1, ("win", "branch")), (1, "ffn1_up"): (1, ("col2",)), (1, "swa"): (1, ("down2",))}
SCATTER_CARRIERS = {(0, "ffn2_dh"): (1, ("col1",)), (0, "ffn2_dwgu"): (1, ("col2",)), (0, "ffn2_dn"): (1, ("down1", "down2", "wo")),
                    (0, "dswa"): (1, ("win", "uq", "ukv", "branch")), (0, "dmla"): (0, ("col2", "down2")),
                    (0, "ffn1_dh"): (0, ("wo", "win", "uq", "ukv", "branch"))}
LAST_SCATTER = ("col1", "down1")
EARLY_SHARE = ("col2", "down2")


def _pick(by_tag, tags):
    return [by_tag[t] for t in tags]


def _train_step(x, positions, target, packs, small, shard):
    depth = 2
    b, s, d = x.shape
    t = b * s
    posf = positions.astype(F32)
    half = QK_ROPE_DIM // 2
    inv_freq = ROPE_BASE ** (-jnp.arange(half, dtype=F32) / half)
    ang = posf[..., None] * inv_freq
    cos, sin = jnp.cos(ang), jnp.sin(ang)
    tail = HEAD_PAD - QK_DIM_B
    rope = (jnp.concatenate([jnp.ones((b, s, QK_NOPE_DIM), F32), cos, cos, jnp.ones((b, s, tail), F32)], axis=-1),
            jnp.concatenate([jnp.zeros((b, s, QK_NOPE_DIM), F32), sin, sin, jnp.zeros((b, s, tail), F32)], axis=-1))
    slopes = jnp.exp2(-8.0 * (jnp.arange(N_HEADS_A, dtype=F32) + 1.0) / N_HEADS_A)
    aux = dict(b=b, s=s, pos_col=posf.reshape(b, s, 1), pos_row=posf.reshape(b, 1, s), rope=rope, slopes=slopes)

    def sm_of(l):
        return {n: small[n][l:l + 1] for n in SMALL_NAMES}

    by_tag = dict(zip(BUFFER_TAGS, packs))
    wts, parts, slots = [{} for _ in range(depth)], [{} for _ in range(depth)], [{} for _ in range(depth)]
    saved, sgrads = [None] * depth, [None] * depth

    def gather_piece(l, kern):
        if (l, kern) not in GATHER_CARRIERS:
            return None
        data_l, tags = GATHER_CARRIERS[(l, kern)]
        return gather_plan(_pick(by_tag, tags), data_l)

    def gathered(l, kern, arrays):
        if (l, kern) in GATHER_CARRIERS:
            data_l, tags = GATHER_CARRIERS[(l, kern)]
            wts[data_l].update(zip(tags, arrays))

    def scatter_piece(l, kern):
        if (l, kern) not in SCATTER_CARRIERS:
            return None
        data_l, tags = SCATTER_CARRIERS[(l, kern)]
        return scatter_plan(_pick(parts[data_l], tags), data_l)

    def scattered(l, kern, arrays):
        if (l, kern) in SCATTER_CARRIERS:
            data_l, tags = SCATTER_CARRIERS[(l, kern)]
            slots[data_l].update(zip(tags, arrays))

    def chip_sums(l, tags, mine, theirs=None):
        if theirs is None:
            theirs = comm_call(f"swap_l{l}_{tags[0]}", swap_plan(_pick(mine, tags), l))
        for tag, o in zip(tags, theirs):
            g = mine[tag]
            parts[l][tag] = add_pair(f"l{l}_add_{tag}", g.reshape(-1, g.shape[-1]), o.reshape(-1, o.shape[-1])).reshape(g.shape)

    wts[0].update(zip(FIRST_GATHER, comm_call("gather_first", gather_plan(_pick(by_tag, FIRST_GATHER), 0))))
    h = x.reshape(t, d)
    for l in range(depth):
        sm, wl = sm_of(l), wts[l]
        h, s1, got = _ffn_fwd(f"l{l}_ffn1", h, sm["ffn1_norm"], wl["col1"], wl["down1"], gather_piece(l, "ffn1_up"))
        gathered(l, "ffn1_up", got)
        weights = _layer_weights(wl["win"], wl["uq"], wl["ukv"], wl["branch"])
        h, s2, got = _mixer_fwd(f"l{l}_mix", h, sm, weights, wl["wo"], aux, {k: gather_piece(l, k) for k in ("proj", "swa", "mla")})
        for kern, arrays in got.items():
            gathered(l, kern, arrays)
        h, s3, got = _ffn_fwd(f"l{l}_ffn2", h, sm["ffn2_norm"], wl["col2"], wl["down2"], gather_piece(l, "ffn2_up"))
        gathered(l, "ffn2_up", got)
        saved[l] = (s1, s2, s3, weights)
    loss, dh, dhb = loss_fwd_bwd("loss", h, target.reshape(t, d))

    early_shared = None
    for l in reversed(range(depth)):
        sm, wl = sm_of(l), wts[l]
        s1, s2, s3, weights = saved[l]
        staged = l == 0
        dh, dhb, dg_f2, gcol2, gdown2, got = _ffn_bwd(f"l{l}_ffn2", s3, sm["ffn2_norm"], wl["col2"], wl["down2"], dh, dhb,
                                                      {k: scatter_piece(l, "ffn2_" + k) for k in ("dh", "dwgu", "dn")})
        for kern, arrays in got.items():
            scattered(l, "ffn2_" + kern, arrays)
        mine = dict(col2=gcol2, down2=gdown2)
        mplans = {"dswa": scatter_piece(l, "dswa")}
        if staged:
            mplans["d_out"] = swap_plan(_pick(mine, ("col2", "down2")), l)

            def after_swap(carried, l=l, mine=mine):
                chip_sums(l, ("col2", "down2"), mine, carried["d_out"])
                return scatter_piece(l, "dmla")

            mplans["dmla"] = after_swap
        dh, dhb, wg, sg, got = _mixer_bwd(f"l{l}_mix", s2, sm, weights, wl["wo"], aux, dh, dhb, mplans)
        for kern, arrays in got.items():
            scattered(l, kern, arrays)
        gwin, guq, gukv, gbranch = _mixer_grad_shards(wg)
        mine.update(wo=wg["wo"], win=gwin, uq=guq, ukv=gukv, branch=gbranch)
        if staged:
            chip_sums(l, ("wo", "win", "uq", "ukv", "branch"), mine)
        last_tags = LAST_SCATTER if staged else BUFFER_TAGS

        def last_swap(gcol1, gdown1, l=l, mine=mine, last_tags=last_tags):
            return swap_plan(_pick(dict(mine, col1=gcol1, down1=gdown1), last_tags), l)

        plans = {k: scatter_piece(l, "ffn1_" + k) for k in ("dh", "dwgu")}
        plans["dn"] = last_swap
        early = staged and l == 0 and depth == 2 and plans["dwgu"] is None
        if early:
            plans["dwgu"] = share_plan([[sum_slots(f"l0_sum_{tag}", slots[0][tag], parts[0][tag], shard) for tag in EARLY_SHARE],
                                        [sum_slots(f"l1_sum_{tag}", slots[1][tag], parts[1][tag], shard) for tag in BUFFER_TAGS]])
        dh, dhb, dg_f1, gcol1, gdown1, got = _ffn_bwd(f"l{l}_ffn1", s1, sm["ffn1_norm"], wl["col1"], wl["down1"], dh, dhb, plans)
        if early:
            early_shared = got.pop("dwgu")
        theirs = got.pop("dn")
        for kern, arrays in got.items():
            scattered(l, "ffn1_" + kern, arrays)
        mine.update(col1=gcol1, down1=gdown1)
        chip_sums(l, last_tags, mine, theirs)
        sg.update(ffn1_norm=dg_f1, ffn2_norm=dg_f2)
        sgrads[l] = sg
    slots[0].update(zip(LAST_SCATTER, comm_call("scatter_last", scatter_plan(_pick(parts[0], LAST_SCATTER), 0))))
    nk = len(BUFFER_TAGS)
    if early_shared is not None:
        late_tags = [tag for tag in BUFFER_TAGS if tag not in EARLY_SHARE]
        finals = [[sum_slots(f"l0_sum_{tag}", slots[0][tag], parts[0][tag], shard) for tag in late_tags]]
        first = dict(zip(late_tags, comm_call("share", share_plan(finals))))
        first.update(zip(EARLY_SHARE, early_shared[:len(EARLY_SHARE)]))
        shared = [first[tag] for tag in BUFFER_TAGS] + list(early_shared[len(EARLY_SHARE):])
    else:
        finals = [[sum_slots(f"l{l}_sum_{tag}", slots[l][tag], parts[l][tag], shard) for tag in BUFFER_TAGS] for l in range(depth)]
        shared = list(comm_call("share", share_plan(finals)))
    return loss, dh.reshape(b, s, d), [dict(zip(BUFFER_TAGS, shared[l * nk:(l + 1) * nk])) for l in range(depth)], sgrads


def kernel(x, positions, ffn1_norm, ffn1_w_gate, ffn1_w_up, ffn1_w_down, mix_norm, w_in, swa_q_norm, swa_k_norm, swa_sinks, mla_q_lora_norm, mla_w_uq, mla_kv_lora_norm, mla_w_ukv, mla_q_norm, mla_k_norm, w_branch_a, w_branch_b, w_out, ffn2_norm, ffn2_w_gate, ffn2_w_up, ffn2_w_down, loss_target, m_ffn1_norm, m_ffn1_w_gate, m_ffn1_w_up, m_ffn1_w_down, m_mix_norm, m_w_in, m_swa_q_norm, m_swa_k_norm, m_swa_sinks, m_mla_q_lora_norm, m_mla_w_uq, m_mla_kv_lora_norm, m_mla_w_ukv, m_mla_q_norm, m_mla_k_norm, m_w_branch_a, m_w_branch_b, m_w_out, m_ffn2_norm, m_ffn2_w_gate, m_ffn2_w_up, m_ffn2_w_down, v_ffn1_norm, v_ffn1_w_gate, v_ffn1_w_up, v_ffn1_w_down, v_mix_norm, v_w_in, v_swa_q_norm, v_swa_k_norm, v_swa_sinks, v_mla_q_lora_norm, v_mla_w_uq, v_mla_kv_lora_norm, v_mla_w_ukv, v_mla_q_norm, v_mla_k_norm, v_w_branch_a, v_w_branch_b, v_w_out, v_ffn2_norm, v_ffn2_w_gate, v_ffn2_w_up, v_ffn2_w_down):
    args = (x, positions, ffn1_norm, ffn1_w_gate, ffn1_w_up, ffn1_w_down, mix_norm, w_in, swa_q_norm, swa_k_norm, swa_sinks, mla_q_lora_norm, mla_w_uq, mla_kv_lora_norm, mla_w_ukv, mla_q_norm, mla_k_norm, w_branch_a, w_branch_b, w_out, ffn2_norm, ffn2_w_gate, ffn2_w_up, ffn2_w_down, loss_target, m_ffn1_norm, m_ffn1_w_gate, m_ffn1_w_up, m_ffn1_w_down, m_mix_norm, m_w_in, m_swa_q_norm, m_swa_k_norm, m_swa_sinks, m_mla_q_lora_norm, m_mla_w_uq, m_mla_kv_lora_norm, m_mla_w_ukv, m_mla_q_norm, m_mla_k_norm, m_w_branch_a, m_w_branch_b, m_w_out, m_ffn2_norm, m_ffn2_w_gate, m_ffn2_w_up, m_ffn2_w_down, v_ffn1_norm, v_ffn1_w_gate, v_ffn1_w_up, v_ffn1_w_down, v_mix_norm, v_w_in, v_swa_q_norm, v_swa_k_norm, v_swa_sinks, v_mla_q_lora_norm, v_mla_w_uq, v_mla_kv_lora_norm, v_mla_w_ukv, v_mla_q_norm, v_mla_k_norm, v_w_branch_a, v_w_branch_b, v_w_out, v_ffn2_norm, v_ffn2_w_gate, v_ffn2_w_up, v_ffn2_w_down)
    a = dict(zip(ARG_NAMES, args, strict=True))
    x = a["x"]
    depth = a["ffn1_norm"].shape[0]
    d = x.shape[-1]
    assert depth == 2, "the exchanges of layer l are driven by core l of every chip: two layers, two cores"

    def bf16_rows(*names):
        return jnp.concatenate([a[n] for n in names], axis=1).astype(BF16) if len(names) > 1 else a[names[0]].astype(BF16)

    packs = [bf16_rows("ffn1_w_gate", "ffn1_w_up"), bf16_rows("ffn2_w_gate", "ffn2_w_up"), bf16_rows("ffn1_w_down"),
             bf16_rows("ffn2_w_down"), bf16_rows("w_out"), bf16_rows("w_in"), bf16_rows("mla_w_uq"), bf16_rows("mla_w_ukv"),
             bf16_rows("w_branch_a", "w_branch_b")]
    shard_me = (2 * lax.axis_index("x") + lax.axis_index("y")).astype(jnp.int32).reshape(1)
    small = {n: a[n] for n in SMALL_NAMES}

    loss, grad_x, summed, sgrads = _train_step(x, a["positions"], a["loss_target"], packs, small, shard_me)

    full = {tag: jnp.stack([summed[l][tag] for l in range(depth)]) for tag in BUFFER_TAGS}
    r_branch = a["w_branch_a"].shape[1]
    grads = dict(ffn1_w_gate=full["col1"][:, :d], ffn1_w_up=full["col1"][:, d:], ffn2_w_gate=full["col2"][:, :d],
                 ffn2_w_up=full["col2"][:, d:], ffn1_w_down=full["down1"], ffn2_w_down=full["down2"], w_out=full["wo"],
                 w_in=full["win"], mla_w_uq=full["uq"], mla_w_ukv=full["ukv"], w_branch_a=full["branch"][:, :r_branch],
                 w_branch_b=full["branch"][:, r_branch:])

    flat = jnp.concatenate([jnp.concatenate([sgrads[l][n].reshape(-1) for l in range(depth)]) for n in SMALL_NAMES] + [loss.reshape(-1)])
    n_small = flat.shape[0]
    rows = -(-n_small // (8 * LANES)) * 8
    pad = rows * LANES - n_small

    def small_pack(v):
        return jnp.pad(v, (0, pad)).reshape(rows, LANES)

    total = small_allreduce(small_pack(flat))
    w_s, m_s, v_s = (small_pack(jnp.concatenate([a[p + n].reshape(-1) for n in SMALL_NAMES] + [jnp.zeros((1,), F32)]))
                     for p in ("", "m_", "v_"))
    d_s, nm_s, nv_s = adamw("adamw_small", w_s, total, m_s, v_s)

    def small_unpack(buf):
        out, off, flat_b = {}, 0, buf.reshape(-1)
        for n in SMALL_NAMES:
            size = a[n].shape[0] * a[n].shape[1]
            out[n] = flat_b[off:off + size].reshape(a[n].shape)
            off += size
        return out

    grads.update(small_unpack(total))
    delta, new_m, new_v = small_unpack(d_s), small_unpack(nm_s), small_unpack(nv_s)
    for n in PACK_NAMES:
        flip = a[n].shape[2] % LANES != 0

        def view(z):
            return z.transpose(0, 2, 1) if flip else z

        w_v, g_v, m_v, v_v = view(a[n]), view(grads[n]), view(a["m_" + n]), view(a["v_" + n])
        shp = w_v.shape
        two_d = (shp[0] * shp[1], shp[2])
        dn, mn, vn = adamw("adamw_" + n, w_v.reshape(two_d), g_v.reshape(two_d), m_v.reshape(two_d), v_v.reshape(two_d))
        delta[n], new_m[n], new_v[n] = view(dn.reshape(shp)), view(mn.reshape(shp)), view(vn.reshape(shp))
        if flip:
            grads[n] = view(g_v)

    loss_out = total.reshape(-1)[n_small - 1]
    return (loss_out, grad_x, *[grads[n] for n in WEIGHT_NAMES], *[delta[n] for n in WEIGHT_NAMES],
            *[new_m[n] for n in WEIGHT_NAMES], *[new_v[n] for n in WEIGHT_NAMES])
```
